```python
import math
import jax, jax.numpy as jnp
from jax import lax
import numpy as np

D_MODEL = 2048
BATCH = 8
SEQ = 4096
DEPTH = 2

N_A_LAYERS = DEPTH // 2
N_B_LAYERS = DEPTH - N_A_LAYERS
CONV_WIDTH = 31
CONV_DIM = D_MODEL
HEAD_DIM = 128
N_HEADS = D_MODEL // HEAD_DIM
ATTN_DIM = N_HEADS * HEAD_DIM
DILATED_GROUPS = ((128, 1), (512, 4), (2048, 16))
N_GROUPS = len(DILATED_GROUPS)
BLOCK = 128
N_BUCKETS = 32
MAX_EXACT = N_BUCKETS // 2
MAX_DISTANCE = 2048
EPS = 1e-6

kernel_name = "yoco_conformer_dilated_hybrid"


def rmsnorm(x, g):
    xf = x.astype(jnp.float32)
    y = xf * lax.rsqrt(jnp.mean(xf * xf, axis=-1, keepdims=True) + EPS)
    return (y * g.astype(jnp.float32)).astype(x.dtype)


def t5_bucket(dist):
    large = MAX_EXACT + (jnp.log(jnp.maximum(dist, 1).astype(jnp.float32) / MAX_EXACT)
                         / math.log(MAX_DISTANCE / MAX_EXACT)
                         * (N_BUCKETS - MAX_EXACT)).astype(jnp.int32)
    large = jnp.minimum(large, N_BUCKETS - 1)
    return jnp.where(dist < MAX_EXACT, dist, large)


def conformer_mixer(h, w_in, conv_w, conv_b, ln_g, ln_b, w_out):
    proj = jnp.einsum('bsd,de->bse', h, w_in)
    a, b, z = jnp.split(proj, 3, axis=-1)
    u = a * jax.nn.sigmoid(b)
    c = lax.conv_general_dilated(u, conv_w[:, None, :].astype(u.dtype), (1,),
                                 [(CONV_WIDTH - 1, 0)],
                                 dimension_numbers=('NWC', 'WIO', 'NWC'),
                                 feature_group_count=CONV_DIM) + conv_b
    cf = c.astype(jnp.float32)
    mu = jnp.mean(cf, axis=-1, keepdims=True)
    var = jnp.mean(jnp.square(cf - mu), axis=-1, keepdims=True)
    cn = (cf - mu) * lax.rsqrt(var + EPS) * ln_g.astype(jnp.float32) + ln_b.astype(jnp.float32)
    y = jax.nn.silu(cn).astype(h.dtype) * jax.nn.silu(z)
    return jnp.einsum('bse,ed->bsd', y, w_out)


def dilated_branch(q, k, v, window, dil, rel_bias):
    B, S, H, Dh = q.shape
    L = S // dil
    w_sub = window // dil
    nb = -(-L // BLOCK)
    Lp = nb * BLOCK
    N = B * dil

    def to_blocks(t):
        t = t.reshape(B, L, dil, H, Dh).transpose(0, 2, 1, 3, 4).reshape(N, L, H, Dh)
        t = jnp.pad(t, ((0, 0), (0, Lp - L), (0, 0), (0, 0)))
        return t.reshape(N, nb, BLOCK, H, Dh)

    def with_prev(t):
        prev = jnp.pad(t[:, :-1], ((0, 0), (1, 0), (0, 0), (0, 0), (0, 0)))
        return jnp.concatenate([prev, t], axis=2)

    qb = to_blocks(q).astype(jnp.float32)
    kk = with_prev(to_blocks(k)).astype(jnp.float32)
    vv = with_prev(to_blocks(v)).astype(jnp.float32)

    delta = (jnp.arange(BLOCK)[:, None] + BLOCK) - jnp.arange(2 * BLOCK)[None, :]
    local = (delta >= 0) & (delta <= w_sub)
    key_pos = jnp.arange(nb)[:, None] * BLOCK - BLOCK + jnp.arange(2 * BLOCK)[None, :]
    mask = local[None] & (key_pos >= 0)[:, None, :]
    bias = rel_bias[t5_bucket(jnp.clip(delta, 0) * dil)]
    bias = bias.astype(jnp.float32).transpose(2, 0, 1)

    s = jnp.einsum('nbqhd,nbkhd->nbhqk', qb, kk) * (HEAD_DIM ** -0.5) + bias
    s = jnp.where(mask[None, :, None], s, -jnp.inf)
    m = jnp.max(s, axis=-1, keepdims=True)
    p = jnp.exp(s - m)
    den = jnp.sum(p, axis=-1, keepdims=True)
    o = jnp.einsum('nbhqk,nbkhd->nbqhd', p, vv) / den.transpose(0, 1, 3, 2, 4)
    lse = (m + jnp.log(den))[..., 0].transpose(0, 1, 3, 2)

    def from_blocks(t):
        rest = t.shape[3:]
        t = t.reshape(N, Lp, *rest)[:, :L].reshape(B, dil, L, *rest)
        return jnp.swapaxes(t, 1, 2).reshape(B, S, *rest)

    return from_blocks(o), from_blocks(lse)


def shared_kv(x, kv_norm, w_kv):
    B, S, _ = x.shape
    kv = jnp.einsum('bsd,de->bse', rmsnorm(x, kv_norm), w_kv)
    parts = jnp.split(kv, 2 * N_GROUPS, axis=-1)
    return [(parts[2 * g].reshape(B, S, N_HEADS, HEAD_DIM),
             parts[2 * g + 1].reshape(B, S, N_HEADS, HEAD_DIM)) for g in range(N_GROUPS)]


def dilated_mixer(h, w_in, w_out, kv, rel_bias):
    B, S, _ = h.shape
    parts = jnp.split(jnp.einsum('bsd,de->bse', h, w_in), N_GROUPS + 1, axis=-1)
    outs, lses = [], []
    for g, (window, dil) in enumerate(DILATED_GROUPS):
        q = parts[g].reshape(B, S, N_HEADS, HEAD_DIM)
        o, lse = dilated_branch(q, kv[g][0], kv[g][1], window, dil, rel_bias)
        outs.append(o)
        lses.append(lse)
    alpha = jax.nn.softmax(jnp.stack(lses), axis=0)
    o = jnp.sum(alpha[..., None] * jnp.stack(outs), axis=0).reshape(B, S, ATTN_DIM)
    y = o.astype(h.dtype) * jax.nn.silu(parts[-1])
    return jnp.einsum('bse,ed->bsd', y, w_out)


def _fwd_setup_inputs(seed: int = 0) -> dict:
    key = jax.random.key(seed)
    ks = jax.random.split(key, 16)
    f32 = jnp.float32
    D, E = D_MODEL, CONV_DIM
    nrm = lambda k, shape, scale: jax.random.normal(k, shape, f32) * scale
    return {
        "x": nrm(ks[0], (BATCH, SEQ, D), 1.0),
        "a_norm": 1.0 + nrm(ks[1], (N_A_LAYERS, D), 0.02),
        "a_w_in": nrm(ks[2], (N_A_LAYERS, D, 3 * E), D ** -0.5),
        "a_conv_w": nrm(ks[3], (N_A_LAYERS, CONV_WIDTH, E), CONV_WIDTH ** -0.5),
        "a_conv_b": nrm(ks[4], (N_A_LAYERS, E), 0.02),
        "a_ln_g": 1.0 + nrm(ks[5], (N_A_LAYERS, E), 0.02),
        "a_ln_b": nrm(ks[6], (N_A_LAYERS, E), 0.02),
        "a_w_out": nrm(ks[7], (N_A_LAYERS, E, D), E ** -0.5),
        "kv_norm": 1.0 + nrm(ks[8], (D,), 0.02),
        "w_kv": nrm(ks[9], (D, 2 * N_GROUPS * ATTN_DIM), D ** -0.5),
        "b_norm": 1.0 + nrm(ks[10], (N_B_LAYERS, D), 0.02),
        "b_w_in": nrm(ks[11], (N_B_LAYERS, D, (N_GROUPS + 1) * ATTN_DIM), D ** -0.5),
        "b_w_out": nrm(ks[12], (N_B_LAYERS, ATTN_DIM, D), ATTN_DIM ** -0.5),
        "rel_bias": nrm(ks[13], (N_BUCKETS, N_HEADS), 0.2),
        "final_norm": 1.0 + nrm(ks[14], (D,), 0.02),
    }


def _fwd_reference(x, a_norm, a_w_in, a_conv_w, a_conv_b, a_ln_g, a_ln_b, a_w_out,
              kv_norm, w_kv, b_norm, b_w_in, b_w_out, rel_bias, final_norm):
    kv = None
    for layer in range(DEPTH):
        if layer < N_A_LAYERS:
            i = layer
            x = x + conformer_mixer(rmsnorm(x, a_norm[i]), a_w_in[i], a_conv_w[i], a_conv_b[i],
                                    a_ln_g[i], a_ln_b[i], a_w_out[i])
        else:
            if kv is None:
                kv = shared_kv(x, kv_norm, w_kv)
            j = layer - N_A_LAYERS
            x = x + dilated_mixer(rmsnorm(x, b_norm[j]), b_w_in[j], b_w_out[j], kv, rel_bias)
    return rmsnorm(x, final_norm)


import jax as _jax
import jax.numpy as _jnp

TWIN_FORMAT = 'train_step'
FWD_PARAMS = ['x', 'a_norm', 'a_w_in', 'a_conv_w', 'a_conv_b', 'a_ln_g', 'a_ln_b', 'a_w_out', 'kv_norm', 'w_kv', 'b_norm', 'b_w_in', 'b_w_out', 'rel_bias', 'final_norm']
TWIN_WEIGHTS = ['a_norm', 'a_w_in', 'a_conv_w', 'a_conv_b', 'a_ln_g', 'a_ln_b', 'a_w_out', 'kv_norm', 'w_kv', 'b_norm', 'b_w_in', 'b_w_out', 'rel_bias', 'final_norm']
TWIN_DIFF_INPUT = 'x'
TWIN_INPUTS = ['x', 'a_norm', 'a_w_in', 'a_conv_w', 'a_conv_b', 'a_ln_g', 'a_ln_b', 'a_w_out', 'kv_norm', 'w_kv', 'b_norm', 'b_w_in', 'b_w_out', 'rel_bias', 'final_norm', 'loss_target', 'm_a_norm', 'm_a_w_in', 'm_a_conv_w', 'm_a_conv_b', 'm_a_ln_g', 'm_a_ln_b', 'm_a_w_out', 'm_kv_norm', 'm_w_kv', 'm_b_norm', 'm_b_w_in', 'm_b_w_out', 'm_rel_bias', 'm_final_norm', 'v_a_norm', 'v_a_w_in', 'v_a_conv_w', 'v_a_conv_b', 'v_a_ln_g', 'v_a_ln_b', 'v_a_w_out', 'v_kv_norm', 'v_w_kv', 'v_b_norm', 'v_b_w_in', 'v_b_w_out', 'v_rel_bias', 'v_final_norm']
TWIN_OUTPUTS = ['loss', 'grad_x', 'grad_a_norm', 'grad_a_w_in', 'grad_a_conv_w', 'grad_a_conv_b', 'grad_a_ln_g', 'grad_a_ln_b', 'grad_a_w_out', 'grad_kv_norm', 'grad_w_kv', 'grad_b_norm', 'grad_b_w_in', 'grad_b_w_out', 'grad_rel_bias', 'grad_final_norm', 'delta_a_norm', 'delta_a_w_in', 'delta_a_conv_w', 'delta_a_conv_b', 'delta_a_ln_g', 'delta_a_ln_b', 'delta_a_w_out', 'delta_kv_norm', 'delta_w_kv', 'delta_b_norm', 'delta_b_w_in', 'delta_b_w_out', 'delta_rel_bias', 'delta_final_norm', 'new_m_a_norm', 'new_m_a_w_in', 'new_m_a_conv_w', 'new_m_a_conv_b', 'new_m_a_ln_g', 'new_m_a_ln_b', 'new_m_a_w_out', 'new_m_kv_norm', 'new_m_w_kv', 'new_m_b_norm', 'new_m_b_w_in', 'new_m_b_w_out', 'new_m_rel_bias', 'new_m_final_norm', 'new_v_a_norm', 'new_v_a_w_in', 'new_v_a_conv_w', 'new_v_a_conv_b', 'new_v_a_ln_g', 'new_v_a_ln_b', 'new_v_a_w_out', 'new_v_kv_norm', 'new_v_w_kv', 'new_v_b_norm', 'new_v_b_w_in', 'new_v_b_w_out', 'new_v_rel_bias', 'new_v_final_norm']
TWIN_LEAF_KINDS = {'loss': 'loss', 'grad_x': 'grad_x', 'grad_a_norm': 'grad_w', 'grad_a_w_in': 'grad_w', 'grad_a_conv_w': 'grad_w', 'grad_a_conv_b': 'grad_w', 'grad_a_ln_g': 'grad_w', 'grad_a_ln_b': 'grad_w', 'grad_a_w_out': 'grad_w', 'grad_kv_norm': 'grad_w', 'grad_w_kv': 'grad_w', 'grad_b_norm': 'grad_w', 'grad_b_w_in': 'grad_w', 'grad_b_w_out': 'grad_w', 'grad_rel_bias': 'grad_w', 'grad_final_norm': 'grad_w', 'delta_a_norm': 'delta_w', 'delta_a_w_in': 'delta_w', 'delta_a_conv_w': 'delta_w', 'delta_a_conv_b': 'delta_w', 'delta_a_ln_g': 'delta_w', 'delta_a_ln_b': 'delta_w', 'delta_a_w_out': 'delta_w', 'delta_kv_norm': 'delta_w', 'delta_w_kv': 'delta_w', 'delta_b_norm': 'delta_w', 'delta_b_w_in': 'delta_w', 'delta_b_w_out': 'delta_w', 'delta_rel_bias': 'delta_w', 'delta_final_norm': 'delta_w', 'new_m_a_norm': 'new_m', 'new_m_a_w_in': 'new_m', 'new_m_a_conv_w': 'new_m', 'new_m_a_conv_b': 'new_m', 'new_m_a_ln_g': 'new_m', 'new_m_a_ln_b': 'new_m', 'new_m_a_w_out': 'new_m', 'new_m_kv_norm': 'new_m', 'new_m_w_kv': 'new_m', 'new_m_b_norm': 'new_m', 'new_m_b_w_in': 'new_m', 'new_m_b_w_out': 'new_m', 'new_m_rel_bias': 'new_m', 'new_m_final_norm': 'new_m', 'new_v_a_norm': 'new_v', 'new_v_a_w_in': 'new_v', 'new_v_a_conv_w': 'new_v', 'new_v_a_conv_b': 'new_v', 'new_v_a_ln_g': 'new_v', 'new_v_a_ln_b': 'new_v', 'new_v_a_w_out': 'new_v', 'new_v_kv_norm': 'new_v', 'new_v_w_kv': 'new_v', 'new_v_b_norm': 'new_v', 'new_v_b_w_in': 'new_v', 'new_v_b_w_out': 'new_v', 'new_v_rel_bias': 'new_v', 'new_v_final_norm': 'new_v'}


def _forward(args):
    return _fwd_reference(*[args[k] for k in FWD_PARAMS])


def _output_shape():
    def fwd():
        inp = _fwd_setup_inputs(0)
        return _fwd_reference(*[inp[k] for k in FWD_PARAMS])
    out = _jax.eval_shape(fwd)
    return out.shape, out.dtype

N_MICROBATCH = 1
ADAM_LR = 0.001
ADAM_B1 = 0.9
ADAM_B2 = 0.999
ADAM_EPS = 1e-08
ADAM_WD = 0.01
ADAM_STEP = 10
PER_EXAMPLE_BATCH_AXIS = {'x': 0, 'loss_target': 0}
SHARED_INPUTS = []
_WEIGHT_DTYPES = {'a_norm': _jnp.float32, 'a_w_in': _jnp.float32, 'a_conv_w': _jnp.float32, 'a_conv_b': _jnp.float32, 'a_ln_g': _jnp.float32, 'a_ln_b': _jnp.float32, 'a_w_out': _jnp.float32, 'kv_norm': _jnp.float32, 'w_kv': _jnp.float32, 'b_norm': _jnp.float32, 'b_w_in': _jnp.float32, 'b_w_out': _jnp.float32, 'rel_bias': _jnp.float32, 'final_norm': _jnp.float32}
MOMENT_SCALE = {'a_norm': 4.719968e-02, 'a_w_in': 2.646950e-02, 'a_conv_w': 3.120232e-02, 'a_conv_b': 6.806127e-02, 'a_ln_g': 3.621830e-02, 'a_ln_b': 3.104001e-02, 'a_w_out': 3.028166e-02, 'kv_norm': 1.247209e-02, 'w_kv': 5.072616e-03, 'b_norm': 1.270869e-02, 'b_w_in': 6.314181e-03, 'b_w_out': 9.387157e-03, 'rel_bias': 1.440165e-02, 'final_norm': 1.598036e+01}


def _to_microbatches(a, axis):
    t = _jnp.moveaxis(a, axis, 0)
    t = t.reshape((N_MICROBATCH, t.shape[0] // N_MICROBATCH) + t.shape[1:])
    return _jnp.moveaxis(t, 1, axis + 1)


def setup_inputs(seed: int = 0) -> dict:
    inp = _fwd_setup_inputs(seed)
    key = _jax.random.fold_in(_jax.random.key(seed), 7919)
    shape, _ = _output_shape()
    out = dict(inp)
    out["loss_target"] = _jax.random.normal(_jax.random.fold_in(key, 0), shape, _jnp.float32)
    for i, name in enumerate(TWIN_WEIGHTS):
        w = inp[name].astype(_jnp.float32)
        if MOMENT_SCALE is None:
            s = _jnp.sqrt(_jnp.mean(_jnp.square(w)) + 1e-30)
        else:
            s = MOMENT_SCALE[name]
        km, kv = _jax.random.split(_jax.random.fold_in(key, i + 1))
        out[name] = w
        out["m_" + name] = s * _jax.random.normal(km, w.shape, _jnp.float32)
        out["v_" + name] = (s * s) * _jax.random.uniform(kv, w.shape, _jnp.float32, 0.5, 1.5)
    if N_MICROBATCH > 1:
        for name, axis in PER_EXAMPLE_BATCH_AXIS.items():
            out[name] = _to_microbatches(out[name], axis)
    return {'x': out['x'], 'a_norm': out['a_norm'], 'a_w_in': out['a_w_in'], 'a_conv_w': out['a_conv_w'], 'a_conv_b': out['a_conv_b'], 'a_ln_g': out['a_ln_g'], 'a_ln_b': out['a_ln_b'], 'a_w_out': out['a_w_out'], 'kv_norm': out['kv_norm'], 'w_kv': out['w_kv'], 'b_norm': out['b_norm'], 'b_w_in': out['b_w_in'], 'b_w_out': out['b_w_out'], 'rel_bias': out['rel_bias'], 'final_norm': out['final_norm'], 'loss_target': out['loss_target'], 'm_a_norm': out['m_a_norm'], 'm_a_w_in': out['m_a_w_in'], 'm_a_conv_w': out['m_a_conv_w'], 'm_a_conv_b': out['m_a_conv_b'], 'm_a_ln_g': out['m_a_ln_g'], 'm_a_ln_b': out['m_a_ln_b'], 'm_a_w_out': out['m_a_w_out'], 'm_kv_norm': out['m_kv_norm'], 'm_w_kv': out['m_w_kv'], 'm_b_norm': out['m_b_norm'], 'm_b_w_in': out['m_b_w_in'], 'm_b_w_out': out['m_b_w_out'], 'm_rel_bias': out['m_rel_bias'], 'm_final_norm': out['m_final_norm'], 'v_a_norm': out['v_a_norm'], 'v_a_w_in': out['v_a_w_in'], 'v_a_conv_w': out['v_a_conv_w'], 'v_a_conv_b': out['v_a_conv_b'], 'v_a_ln_g': out['v_a_ln_g'], 'v_a_ln_b': out['v_a_ln_b'], 'v_a_w_out': out['v_a_w_out'], 'v_kv_norm': out['v_kv_norm'], 'v_w_kv': out['v_w_kv'], 'v_b_norm': out['v_b_norm'], 'v_b_w_in': out['v_b_w_in'], 'v_b_w_out': out['v_b_w_out'], 'v_rel_bias': out['v_rel_bias'], 'v_final_norm': out['v_final_norm']}


def _loss(weights, diff, rest, loss_target):
    with _jax.named_scope("forward"):
        args = {**rest, TWIN_DIFF_INPUT: diff, **{k: w.astype(_WEIGHT_DTYPES[k]) for k, w in weights.items()}}
        y = _forward(args)
    with _jax.named_scope("loss_head"):
        err = _jnp.square(y.astype(_jnp.float32) - loss_target)
        return 0.5 * _jnp.sum(_jnp.mean(err, axis=-1)) if err.ndim else 0.5 * err


def _adamw(w, g, m, v):
    m = ADAM_B1 * m + (1.0 - ADAM_B1) * g
    v = ADAM_B2 * v + (1.0 - ADAM_B2) * _jnp.square(g)
    m_hat = m / (1.0 - ADAM_B1 ** ADAM_STEP)
    v_hat = v / (1.0 - ADAM_B2 ** ADAM_STEP)
    delta = -ADAM_LR * (m_hat / (_jnp.sqrt(v_hat) + ADAM_EPS) + ADAM_WD * w)
    return delta, m, v


def reference(x, a_norm, a_w_in, a_conv_w, a_conv_b, a_ln_g, a_ln_b, a_w_out, kv_norm, w_kv, b_norm, b_w_in, b_w_out, rel_bias, final_norm, loss_target, m_a_norm, m_a_w_in, m_a_conv_w, m_a_conv_b, m_a_ln_g, m_a_ln_b, m_a_w_out, m_kv_norm, m_w_kv, m_b_norm, m_b_w_in, m_b_w_out, m_rel_bias, m_final_norm, v_a_norm, v_a_w_in, v_a_conv_w, v_a_conv_b, v_a_ln_g, v_a_ln_b, v_a_w_out, v_kv_norm, v_w_kv, v_b_norm, v_b_w_in, v_b_w_out, v_rel_bias, v_final_norm):
    given = dict(x=x, a_norm=a_norm, a_w_in=a_w_in, a_conv_w=a_conv_w, a_conv_b=a_conv_b, a_ln_g=a_ln_g, a_ln_b=a_ln_b, a_w_out=a_w_out, kv_norm=kv_norm, w_kv=w_kv, b_norm=b_norm, b_w_in=b_w_in, b_w_out=b_w_out, rel_bias=rel_bias, final_norm=final_norm, loss_target=loss_target, m_a_norm=m_a_norm, m_a_w_in=m_a_w_in, m_a_conv_w=m_a_conv_w, m_a_conv_b=m_a_conv_b, m_a_ln_g=m_a_ln_g, m_a_ln_b=m_a_ln_b, m_a_w_out=m_a_w_out, m_kv_norm=m_kv_norm, m_w_kv=m_w_kv, m_b_norm=m_b_norm, m_b_w_in=m_b_w_in, m_b_w_out=m_b_w_out, m_rel_bias=m_rel_bias, m_final_norm=m_final_norm, v_a_norm=v_a_norm, v_a_w_in=v_a_w_in, v_a_conv_w=v_a_conv_w, v_a_conv_b=v_a_conv_b, v_a_ln_g=v_a_ln_g, v_a_ln_b=v_a_ln_b, v_a_w_out=v_a_w_out, v_kv_norm=v_kv_norm, v_w_kv=v_w_kv, v_b_norm=v_b_norm, v_b_w_in=v_b_w_in, v_b_w_out=v_b_w_out, v_rel_bias=v_rel_bias, v_final_norm=v_final_norm)
    weights = {n: given[n] for n in TWIN_WEIGHTS}
    shared = {n: given[n] for n in SHARED_INPUTS}
    per_example = {n: given[n] for n in ['x']}
    grad_fn = _jax.value_and_grad(_loss, argnums=(0, 1))

    def one_microbatch(ex, loss_target):
        ex = dict(ex)
        diff = ex.pop(TWIN_DIFF_INPUT)
        return grad_fn(weights, diff, {**shared, **ex}, loss_target)

    if N_MICROBATCH == 1:
        loss, (grad_w, grad_x) = one_microbatch(per_example, given["loss_target"])
    else:
        def body(carry, xs):
            loss_sum, grad_sum = carry
            l_k, (gw_k, gx_k) = one_microbatch(xs[0], xs[1])
            with _jax.named_scope("update"):
                return (loss_sum + l_k, _jax.tree.map(_jnp.add, grad_sum, gw_k)), gx_k

        init = (_jnp.zeros((), _jnp.float32), _jax.tree.map(_jnp.zeros_like, weights))
        (loss, grad_w), grad_x = _jax.lax.scan(body, init, (per_example, given["loss_target"]))
    with _jax.named_scope("update"):
        delta_w, new_m, new_v = {}, {}, {}
        for n in TWIN_WEIGHTS:
            delta_w[n], new_m[n], new_v[n] = _adamw(weights[n], grad_w[n], given["m_" + n], given["v_" + n])
    return (loss, grad_x, *[grad_w[n] for n in TWIN_WEIGHTS], *[delta_w[n] for n in TWIN_WEIGHTS],
            *[new_m[n] for n in TWIN_WEIGHTS], *[new_v[n] for n in TWIN_WEIGHTS])
```

```python
import functools

import numpy as np
import jax
import jax.numpy as jnp
from jax import lax
from jax.experimental import pallas as pl
from jax.experimental.pallas import tpu as pltpu

F32 = jnp.float32
BF16 = jnp.bfloat16
MESH = pl.DeviceIdType.MESH
ANY = pl.BlockSpec(memory_space=pl.ANY)
VMEM_SPEC = pl.BlockSpec(memory_space=pltpu.VMEM)

EPS = 1e-6
HEAD_DIM = 128
BLOCK = 128
GROUPS = ((128, 1), (512, 4), (2048, 16))
SCALE = HEAD_DIM ** -0.5
CONV_TAPS = 31
HALO = 32
N_BUCKETS = 32
MAX_EXACT = 16
MAX_DISTANCE = 2048
NEG = -1e30
N_CHIPS = 4
N_DEV = 8
LANES = 128
VMEM_LIMIT = 56 * 1024 * 1024

ADAM_LR = 0.001
ADAM_B1 = 0.9
ADAM_B2 = 0.999
ADAM_EPS = 1e-08
ADAM_WD = 0.01
ADAM_STEP = 10


def _tile(n, pref, mult=LANES):
    t = (min(pref, n) // mult) * mult
    while t >= mult:
        if n % t == 0:
            return t
        t -= mult
    return n


def _params(*sem):
    return pltpu.CompilerParams(dimension_semantics=sem, vmem_limit_bytes=VMEM_LIMIT)


def _sigmoid(v):
    return 1.0 / (1.0 + jnp.exp(-v))


def _dot(a, b, dims):
    return lax.dot_general(a, b, (dims, ((), ())), preferred_element_type=F32)


NN = ((1,), (0,))
NT = ((1,), (1,))
TN = ((0,), (0,))


def _stack_rows(rows, total):
    width = rows[0].shape[1]
    rid = lax.broadcasted_iota(jnp.int32, (total, width), 0)
    out = jnp.zeros((total, width), F32)
    for q, row in enumerate(rows):
        out = jnp.where(rid == q, jnp.broadcast_to(row, (total, width)), out)
    return out


def _lane_col(arr, h, lane):
    return jnp.sum(jnp.where(lane == h, arr, 0.0), axis=-1, keepdims=True)


def _matmul(a, b, mode, out_dtype, name, res=None, a_slab=False, b_slab=False, out_slab=0,
            tm=512, tn=1024, tk=2048):
    if a_slab:
        na, M, W = a.shape
        K = na * W
    elif mode == "tn":
        K, M = a.shape
    else:
        M, K = a.shape
    if b_slab:
        nbs, _, Wb = b.shape
        N = nbs * Wb
    elif mode == "nt":
        N = b.shape[0]
    else:
        N = b.shape[1]
    tm = _tile(M, tm)
    tn = _tile(Wb if b_slab else (out_slab if out_slab else N), tn)
    tk = _tile(W if a_slab else K, tk)
    nk = K // tk
    grid = (M // tm, N // tn, nk)

    if a_slab:
        per = W // tk
        a_spec = pl.BlockSpec((None, tm, tk), lambda i, j, k: (k // per, i, k % per))
    elif mode == "tn":
        a_spec = pl.BlockSpec((tk, tm), lambda i, j, k: (k, i))
    else:
        a_spec = pl.BlockSpec((tm, tk), lambda i, j, k: (i, k))
    if b_slab:
        perb = Wb // tn
        b_spec = pl.BlockSpec((None, tk, tn), lambda i, j, k: (j // perb, k, j % perb))
    elif mode == "nt":
        b_spec = pl.BlockSpec((tn, tk), lambda i, j, k: (j, k))
    else:
        b_spec = pl.BlockSpec((tk, tn), lambda i, j, k: (k, j))
    if out_slab:
        pero = out_slab // tn
        o_spec = pl.BlockSpec((None, tm, tn), lambda i, j, k: (j // pero, i, j % pero))
        out_shape = jax.ShapeDtypeStruct((N // out_slab, M, out_slab), out_dtype)
    else:
        o_spec = pl.BlockSpec((tm, tn), lambda i, j, k: (i, j))
        out_shape = jax.ShapeDtypeStruct((M, N), out_dtype)
    in_specs = [a_spec, b_spec]
    operands = [a, b]
    if res is not None:
        in_specs.append(pl.BlockSpec((tm, tn), lambda i, j, k: (i, j)))
        operands.append(res)
    dims = {"nn": NN, "nt": NT, "tn": TN}[mode]
    has_res = res is not None

    def body(*refs):
        a_ref, b_ref = refs[0], refs[1]
        r_ref = refs[2] if has_res else None
        o_ref = refs[3] if has_res else refs[2]
        prod = _dot(a_ref[...], b_ref[...], dims)

        def finish(val):
            if has_res:
                val = val + r_ref[...]
            o_ref[...] = val.astype(out_dtype)

        if nk == 1:
            finish(prod)
        else:
            acc_ref = refs[-1]
            k = pl.program_id(2)

            @pl.when(k == 0)
            def _():
                acc_ref[...] = prod

            @pl.when(k > 0)
            def _():
                acc_ref[...] += prod

            @pl.when(k == nk - 1)
            def _():
                finish(acc_ref[...])

    scratch = [pltpu.VMEM((tm, tn), F32)] if nk > 1 else []
    return pl.pallas_call(
        body, name=name, grid=grid, in_specs=in_specs, out_specs=o_spec, out_shape=out_shape,
        scratch_shapes=scratch,
        compiler_params=_params("parallel", "parallel", "arbitrary"),
    )(*operands)


def _rms_fwd(x, gains, name, ts=256):
    S, D = x.shape
    ts = _tile(S, ts, 16)
    n = len(gains)

    def body(*refs):
        xv = refs[0][...]
        nrm = xv * lax.rsqrt(jnp.mean(xv * xv, axis=-1, keepdims=True) + EPS)
        for q in range(n):
            refs[1 + n + q][...] = (nrm * refs[1 + q][...]).astype(BF16)

    row = pl.BlockSpec((ts, D), lambda i: (i, 0))
    vec = pl.BlockSpec((1, D), lambda i: (0, 0))
    return pl.pallas_call(
        body, name=name, grid=(S // ts,), in_specs=[row] + [vec] * n, out_specs=[row] * n,
        out_shape=[jax.ShapeDtypeStruct((S, D), BF16)] * n,
        compiler_params=_params("parallel"),
    )(x, *gains)


def _rms_bwd(x, dhs, gains, dres, name, ts=256):
    S, D = x.shape
    ts = _tile(S, ts, 16)
    n = len(dhs)

    def body(*refs):
        x_ref = refs[0]
        dh_refs = refs[1:1 + n]
        g_refs = refs[1 + n:1 + 2 * n]
        dres_ref = refs[1 + 2 * n]
        dx_ref, dxb_ref, gg_ref = refs[2 + 2 * n:5 + 2 * n]
        i = pl.program_id(0)
        xv = x_ref[...]
        r = lax.rsqrt(jnp.mean(xv * xv, axis=-1, keepdims=True) + EPS)
        nrm = xv * r
        dn = jnp.zeros_like(xv)
        rows = []
        for q in range(n):
            dh = dh_refs[q][...].astype(F32)
            dn = dn + dh * g_refs[q][...]
            rows.append(jnp.sum(dh * nrm, axis=0, keepdims=True))
        dx = dres_ref[...] + r * (dn - nrm * jnp.mean(dn * nrm, axis=-1, keepdims=True))
        dx_ref[...] = dx
        dxb_ref[...] = dx.astype(BF16)
        upd = _stack_rows(rows, 8)

        @pl.when(i == 0)
        def _():
            gg_ref[...] = upd

        @pl.when(i > 0)
        def _():
            gg_ref[...] += upd

    row = pl.BlockSpec((ts, D), lambda i: (i, 0))
    vec = pl.BlockSpec((1, D), lambda i: (0, 0))
    acc = pl.BlockSpec((8, D), lambda i: (0, 0))
    return pl.pallas_call(
        body, name=name, grid=(S // ts,), in_specs=[row] + [row] * n + [vec] * n + [row],
        out_specs=[row, row, acc],
        out_shape=[jax.ShapeDtypeStruct((S, D), F32), jax.ShapeDtypeStruct((S, D), BF16),
                   jax.ShapeDtypeStruct((8, D), F32)],
        compiler_params=_params("arbitrary"),
    )(x, *dhs, *gains, dres)


def _final_head(x2, gain, target, name, ts=256):
    S, D = x2.shape
    ts = _tile(S, ts, 16)

    def body(x_ref, g_ref, t_ref, loss_ref, dx_ref, dxb_ref, gg_ref):
        i = pl.program_id(0)
        xv = x_ref[...]
        g = g_ref[...]
        r = lax.rsqrt(jnp.mean(xv * xv, axis=-1, keepdims=True) + EPS)
        nrm = xv * r
        err = nrm * g - t_ref[...]
        part = 0.5 * jnp.sum(jnp.mean(err * err, axis=-1, keepdims=True), axis=0, keepdims=True)
        dout = err * (1.0 / D)
        dn = dout * g
        dx = r * (dn - nrm * jnp.mean(dn * nrm, axis=-1, keepdims=True))
        dx_ref[...] = dx
        dxb_ref[...] = dx.astype(BF16)
        upd = _stack_rows([jnp.sum(dout * nrm, axis=0, keepdims=True)], 8)
        lpart = jnp.broadcast_to(part, (1, LANES))

        @pl.when(i == 0)
        def _():
            gg_ref[...] = upd
            loss_ref[...] = lpart

        @pl.when(i > 0)
        def _():
            gg_ref[...] += upd
            loss_ref[...] += lpart

    row = pl.BlockSpec((ts, D), lambda i: (i, 0))
    vec = pl.BlockSpec((1, D), lambda i: (0, 0))
    return pl.pallas_call(
        body, name=name, grid=(S // ts,), in_specs=[row, vec, row],
        out_specs=[pl.BlockSpec((1, LANES), lambda i: (0, 0)), row, row, pl.BlockSpec((8, D), lambda i: (0, 0))],
        out_shape=[jax.ShapeDtypeStruct((1, LANES), F32), jax.ShapeDtypeStruct((S, D), F32),
                   jax.ShapeDtypeStruct((S, D), BF16), jax.ShapeDtypeStruct((8, D), F32)],
        compiler_params=_params("arbitrary"),
    )(x2, gain, target)


CONV_ROWS = 64


def _conv_fwd(proj3, conv_w32, conv_b, name, ts=256, cw=256):
    _, S, E = proj3.shape
    ts = _tile(S, ts, HALO)
    cw = _tile(E, cw)
    per = ts // HALO
    rc = min(CONV_ROWS, ts)

    def body(a_ref, b_ref, ap_ref, bp_ref, w_ref, cb_ref, c_ref, ubuf):
        i = pl.program_id(0)
        up = ap_ref[...].astype(F32) * _sigmoid(bp_ref[...].astype(F32))
        ubuf[0:HALO, :] = jnp.where(i > 0, up, 0.0)
        ubuf[HALO:HALO + ts, :] = a_ref[...].astype(F32) * _sigmoid(b_ref[...].astype(F32))
        for r0 in range(0, ts, rc):
            acc = jnp.broadcast_to(cb_ref[...], (rc, cw))
            for k in range(CONV_TAPS):
                off = r0 + HALO - (CONV_TAPS - 1) + k
                acc = acc + ubuf[off:off + rc, :] * w_ref[k:k + 1, :]
            c_ref[r0:r0 + rc, :] = acc

    return pl.pallas_call(
        body, name=name, grid=(S // ts, E // cw),
        in_specs=[
            pl.BlockSpec((None, ts, cw), lambda i, j: (0, i, j)),
            pl.BlockSpec((None, ts, cw), lambda i, j: (1, i, j)),
            pl.BlockSpec((None, HALO, cw), lambda i, j: (0, jnp.maximum(i * per - 1, 0), j)),
            pl.BlockSpec((None, HALO, cw), lambda i, j: (1, jnp.maximum(i * per - 1, 0), j)),
            pl.BlockSpec((HALO, cw), lambda i, j: (0, j)),
            pl.BlockSpec((1, cw), lambda i, j: (0, j)),
        ],
        out_specs=pl.BlockSpec((ts, cw), lambda i, j: (i, j)),
        out_shape=jax.ShapeDtypeStruct((S, E), F32),
        scratch_shapes=[pltpu.VMEM((HALO + ts, cw), F32)],
        compiler_params=_params("parallel", "parallel"),
    )(proj3, proj3, proj3, proj3, conv_w32, conv_b)


def _ln_gate_fwd(c, proj3, ln_g, ln_b, name, ts=256):
    S, E = c.shape
    ts = _tile(S, ts, 16)

    def body(c_ref, z_ref, g_ref, b_ref, y_ref):
        cv = c_ref[...]
        mu = jnp.mean(cv, axis=-1, keepdims=True)
        d = cv - mu
        var = jnp.mean(d * d, axis=-1, keepdims=True)
        cn = d * lax.rsqrt(var + EPS) * g_ref[...] + b_ref[...]
        z = z_ref[...].astype(F32)
        y_ref[...] = ((cn * _sigmoid(cn)).astype(F32) * (z * _sigmoid(z))).astype(BF16)

    row = pl.BlockSpec((ts, E), lambda i: (i, 0))
    vec = pl.BlockSpec((1, E), lambda i: (0, 0))
    return pl.pallas_call(
        body, name=name, grid=(S // ts,),
        in_specs=[row, pl.BlockSpec((None, ts, E), lambda i: (2, i, 0)), vec, vec],
        out_specs=row, out_shape=jax.ShapeDtypeStruct((S, E), BF16),
        compiler_params=_params("parallel"),
    )(c, proj3, ln_g, ln_b)


def _ln_gate_bwd(c, proj3, dy, ln_g, ln_b, name, ts=256):
    S, E = c.shape
    ts = _tile(S, ts, 16)

    def body(c_ref, z_ref, dy_ref, g_ref, b_ref, dc_ref, dz_ref, acc_ref):
        i = pl.program_id(0)
        cv = c_ref[...]
        g = g_ref[...]
        mu = jnp.mean(cv, axis=-1, keepdims=True)
        d = cv - mu
        var = jnp.mean(d * d, axis=-1, keepdims=True)
        rstd = lax.rsqrt(var + EPS)
        chat = d * rstd
        cn = chat * g + b_ref[...]
        z = z_ref[...].astype(F32)
        dyv = dy_ref[...].astype(F32)
        sc = _sigmoid(cn)
        sz = _sigmoid(z)
        dcn = dyv * (z * sz) * (sc * (1.0 + cn * (1.0 - sc)))
        dz_ref[...] = (dyv * (cn * sc) * (sz * (1.0 + z * (1.0 - sz)))).astype(BF16)
        dchat = dcn * g
        dcv = rstd * (dchat - jnp.mean(dchat, axis=-1, keepdims=True)
                      - chat * jnp.mean(dchat * chat, axis=-1, keepdims=True))
        dc_ref[...] = dcv
        upd = _stack_rows([jnp.sum(dcn * chat, axis=0, keepdims=True),
                           jnp.sum(dcn, axis=0, keepdims=True),
                           jnp.sum(dcv, axis=0, keepdims=True)], 8)

        @pl.when(i == 0)
        def _():
            acc_ref[...] = upd

        @pl.when(i > 0)
        def _():
            acc_ref[...] += upd

    row = pl.BlockSpec((ts, E), lambda i: (i, 0))
    vec = pl.BlockSpec((1, E), lambda i: (0, 0))
    return pl.pallas_call(
        body, name=name, grid=(S // ts,),
        in_specs=[row, pl.BlockSpec((None, ts, E), lambda i: (2, i, 0)), row, vec, vec],
        out_specs=[row, pl.BlockSpec((None, ts, E), lambda i: (2, i, 0)), pl.BlockSpec((8, E), lambda i: (0, 0))],
        out_shape=[jax.ShapeDtypeStruct((S, E), F32), jax.ShapeDtypeStruct((3, S, E), BF16),
                   jax.ShapeDtypeStruct((8, E), F32)],
        compiler_params=_params("arbitrary"),
    )(c, proj3, dy, ln_g, ln_b)


def _conv_bwd(proj3, dc, conv_w32, dproj3, name, ts=256, cw=256):
    _, S, E = proj3.shape
    ts = _tile(S, ts, HALO)
    cw = _tile(E, cw)
    per = ts // HALO
    n_i = S // ts
    last_halo = S // HALO - 1
    rc = min(CONV_ROWS, ts)

    def body(a_ref, b_ref, dc_ref, dcn_ref, w_ref, dp_in, dab_ref, dw_ref, dcbuf, ubuf, dwacc):
        del dp_in
        i = pl.program_id(1)
        dcbuf[0:ts, :] = dc_ref[...]
        dcbuf[ts:ts + HALO, :] = jnp.where(i < n_i - 1, dcn_ref[...], 0.0)
        av = a_ref[...].astype(F32)
        sb = _sigmoid(b_ref[...].astype(F32))
        ubuf[...] = av * sb

        @pl.when(i == 0)
        def _():
            dwacc[...] = jnp.zeros_like(dwacc)

        for r0 in range(0, ts, rc):
            uv = ubuf[r0:r0 + rc, :]
            du = jnp.zeros((rc, cw), F32)
            for d in range(CONV_TAPS):
                k = CONV_TAPS - 1 - d
                win = dcbuf[r0 + d:r0 + d + rc, :]
                du = du + win * w_ref[k:k + 1, :]
                dwacc[k:k + 1, :] += jnp.sum(uv * win, axis=0, keepdims=True)
            a_c = a_ref[r0:r0 + rc, :].astype(F32)
            s_c = _sigmoid(b_ref[r0:r0 + rc, :].astype(F32))
            dab_ref[0, r0:r0 + rc, :] = (du * s_c).astype(BF16)
            dab_ref[1, r0:r0 + rc, :] = (du * a_c * s_c * (1.0 - s_c)).astype(BF16)

        @pl.when(i == n_i - 1)
        def _():
            dw_ref[...] = dwacc[...]

    return pl.pallas_call(
        body, name=name, grid=(E // cw, n_i),
        in_specs=[
            pl.BlockSpec((None, ts, cw), lambda j, i: (0, i, j)),
            pl.BlockSpec((None, ts, cw), lambda j, i: (1, i, j)),
            pl.BlockSpec((ts, cw), lambda j, i: (i, j)),
            pl.BlockSpec((HALO, cw), lambda j, i: (jnp.minimum((i + 1) * per, last_halo), j)),
            pl.BlockSpec((HALO, cw), lambda j, i: (0, j)),
            ANY,
        ],
        out_specs=[pl.BlockSpec((2, ts, cw), lambda j, i: (0, i, j)),
                   pl.BlockSpec((HALO, cw), lambda j, i: (0, j))],
        out_shape=[jax.ShapeDtypeStruct((3, S, E), BF16), jax.ShapeDtypeStruct((HALO, E), F32)],
        scratch_shapes=[pltpu.VMEM((ts + HALO, cw), F32), pltpu.VMEM((ts, cw), F32), pltpu.VMEM((HALO, cw), F32)],
        input_output_aliases={5: 0},
        compiler_params=_params("parallel", "arbitrary"),
    )(proj3, proj3, dc, dc, conv_w32, dproj3)


def _bucket_table(dil):
    delta = (np.arange(BLOCK)[:, None] + BLOCK) - np.arange(2 * BLOCK)[None, :]
    dist = np.clip(delta, 0, None) * dil
    large = MAX_EXACT + (np.log(np.maximum(dist, 1).astype(np.float32) / MAX_EXACT)
                         / np.log(MAX_DISTANCE / MAX_EXACT) * (N_BUCKETS - MAX_EXACT)).astype(np.int32)
    large = np.minimum(large, N_BUCKETS - 1)
    return np.where(dist < MAX_EXACT, dist, large).astype(np.int32).reshape(-1)


def _onehot(dil):
    tbl = jnp.asarray(_bucket_table(dil))
    return (tbl[None, :] == jnp.arange(LANES, dtype=jnp.int32)[:, None]).astype(BF16)


def _split3(v):
    hi = v.astype(BF16)
    r1 = v - hi.astype(F32)
    mid = r1.astype(BF16)
    lo = (r1 - mid.astype(F32)).astype(BF16)
    return hi, mid, lo


def _bias_table(rb_t, onehot, name):
    H = rb_t.shape[0]
    N = onehot.shape[1]

    def body(r_ref, oh_ref, o_ref):
        oh = oh_ref[...]
        hi, mid, lo = _split3(r_ref[...])
        o_ref[...] = (_dot(lo, oh, NN) + _dot(mid, oh, NN)) + _dot(hi, oh, NN)

    return pl.pallas_call(
        body, name=name, in_specs=[VMEM_SPEC, VMEM_SPEC], out_specs=VMEM_SPEC,
        out_shape=jax.ShapeDtypeStruct((H, N), F32),
        compiler_params=pltpu.CompilerParams(vmem_limit_bytes=VMEM_LIMIT),
    )(rb_t, onehot)


def _bias_grad(dbs, onehots, name):
    H = dbs[0].shape[0]
    n = len(dbs)

    def body(*refs):
        acc = jnp.zeros((H, LANES), F32)
        for q in range(n):
            oh = refs[n + q][...]
            hi, mid, lo = _split3(refs[q][...])
            acc = acc + ((_dot(lo, oh, NT) + _dot(mid, oh, NT)) + _dot(hi, oh, NT))
        refs[2 * n][...] = acc

    return pl.pallas_call(
        body, name=name, in_specs=[VMEM_SPEC] * (2 * n), out_specs=VMEM_SPEC,
        out_shape=jax.ShapeDtypeStruct((H, LANES), F32),
        compiler_params=pltpu.CompilerParams(vmem_limit_bytes=VMEM_LIMIT),
    )(*dbs, *onehots)


def _attn_fwd(pb4, kv6, bias, g, dil, name):
    _, S, A = pb4.shape
    H = A // HEAD_DIM
    L = S // dil
    nb = L // BLOCK
    qv = pb4.reshape(4, L, dil * A)
    kvv = kv6.reshape(6, L, dil * A)

    def body(q_ref, kp_ref, kc_ref, vp_ref, vc_ref, b_ref, o_ref, lse_ref):
        i = pl.program_id(1)
        qi = lax.broadcasted_iota(jnp.int32, (BLOCK, BLOCK), 0)
        ki = lax.broadcasted_iota(jnp.int32, (BLOCK, BLOCK), 1)
        mask_c = ki <= qi
        mask_p = jnp.logical_and(ki >= qi, i > 0)
        lane = lax.broadcasted_iota(jnp.int32, (BLOCK, LANES), 1)
        lse_acc = jnp.zeros((BLOCK, LANES), F32)
        for h in range(H):
            sl = slice(h * HEAD_DIM, (h + 1) * HEAD_DIM)
            qh = q_ref[:, sl]
            s_c = jnp.where(mask_c, _dot(qh, kc_ref[:, sl], NT) * SCALE + b_ref[h, :, BLOCK:], NEG)
            s_p = jnp.where(mask_p, _dot(qh, kp_ref[:, sl], NT) * SCALE + b_ref[h, :, :BLOCK], NEG)
            m = jnp.maximum(jnp.max(s_c, axis=-1, keepdims=True), jnp.max(s_p, axis=-1, keepdims=True))
            p_c = jnp.exp(s_c - m)
            p_p = jnp.exp(s_p - m)
            den = jnp.sum(p_c, axis=-1, keepdims=True) + jnp.sum(p_p, axis=-1, keepdims=True)
            acc = _dot(p_c.astype(BF16), vc_ref[:, sl], NN) + _dot(p_p.astype(BF16), vp_ref[:, sl], NN)
            o_ref[:, sl] = acc / den
            lse_acc = jnp.where(lane == h, m + jnp.log(den), lse_acc)
        lse_ref[...] = lse_acc

    def blk(slab, prev):
        if prev:
            return pl.BlockSpec((None, BLOCK, A), lambda r, i: (slab, jnp.maximum(i - 1, 0), r))
        return pl.BlockSpec((None, BLOCK, A), lambda r, i: (slab, i, r))

    o, lse = pl.pallas_call(
        body, name=name, grid=(dil, nb),
        in_specs=[blk(g, False), blk(2 * g, True), blk(2 * g, False), blk(2 * g + 1, True), blk(2 * g + 1, False),
                  pl.BlockSpec((H, BLOCK, 2 * BLOCK), lambda r, i: (0, 0, 0))],
        out_specs=[pl.BlockSpec((BLOCK, A), lambda r, i: (i, r)),
                   pl.BlockSpec((BLOCK, LANES), lambda r, i: (i, r))],
        out_shape=[jax.ShapeDtypeStruct((L, dil * A), F32), jax.ShapeDtypeStruct((L, dil * LANES), F32)],
        compiler_params=_params("parallel", "parallel"),
    )(qv, kvv, kvv, kvv, kvv, bias)
    return o.reshape(S, A), lse.reshape(S, LANES)


def _attn_merge(os_, lses, pb4, name, ts=256):
    S, A = os_[0].shape
    H = A // HEAD_DIM
    ts = _tile(S, ts, 16)
    n = len(os_)

    def body(*refs):
        o_refs = refs[:n]
        l_refs = refs[n:2 * n]
        z_ref = refs[2 * n]
        y_ref, om_ref, lse_ref = refs[2 * n + 1:2 * n + 4]
        ls = [r[...] for r in l_refs]
        m = ls[0]
        for q in range(1, n):
            m = jnp.maximum(m, ls[q])
        es = [jnp.exp(v - m) for v in ls]
        den = es[0]
        for q in range(1, n):
            den = den + es[q]
        alphas = [e / den for e in es]
        lse_ref[...] = m + jnp.log(den)
        lane = lax.broadcasted_iota(jnp.int32, (ts, LANES), 1)
        for h in range(H):
            sl = slice(h * HEAD_DIM, (h + 1) * HEAD_DIM)
            om = _lane_col(alphas[0], h, lane) * o_refs[0][:, sl]
            for q in range(1, n):
                om = om + _lane_col(alphas[q], h, lane) * o_refs[q][:, sl]
            z = z_ref[:, sl].astype(F32)
            y_ref[:, sl] = (om * (z * _sigmoid(z))).astype(BF16)
            om_ref[:, sl] = om.astype(BF16)

    row = pl.BlockSpec((ts, A), lambda i: (i, 0))
    lrow = pl.BlockSpec((ts, LANES), lambda i: (i, 0))
    return pl.pallas_call(
        body, name=name, grid=(S // ts,),
        in_specs=[row] * n + [lrow] * n + [pl.BlockSpec((None, ts, A), lambda i: (3, i, 0))],
        out_specs=[row, row, lrow],
        out_shape=[jax.ShapeDtypeStruct((S, A), BF16), jax.ShapeDtypeStruct((S, A), BF16),
                   jax.ShapeDtypeStruct((S, LANES), F32)],
        compiler_params=_params("parallel"),
    )(*os_, *lses, pb4)


def _gate_bwd(dy, om, pb4, name, ts=256):
    S, A = dy.shape
    H = A // HEAD_DIM
    ts = _tile(S, ts, 16)

    def body(dy_ref, om_ref, z_ref, do_ref, dh_ref, dz_ref):
        lane = lax.broadcasted_iota(jnp.int32, (ts, LANES), 1)
        acc = jnp.zeros((ts, LANES), F32)
        for h in range(H):
            sl = slice(h * HEAD_DIM, (h + 1) * HEAD_DIM)
            dyv = dy_ref[:, sl].astype(F32)
            omv = om_ref[:, sl].astype(F32)
            z = z_ref[:, sl].astype(F32)
            sz = _sigmoid(z)
            dob = (dyv * (z * sz)).astype(BF16)
            do_ref[:, sl] = dob
            dz_ref[:, sl] = (dyv * omv * (sz * (1.0 + z * (1.0 - sz)))).astype(BF16)
            acc = jnp.where(lane == h, jnp.sum(dob.astype(F32) * omv, axis=-1, keepdims=True), acc)
        dh_ref[...] = acc

    row = pl.BlockSpec((ts, A), lambda i: (i, 0))
    lrow = pl.BlockSpec((ts, LANES), lambda i: (i, 0))
    slab3 = pl.BlockSpec((None, ts, A), lambda i: (3, i, 0))
    return pl.pallas_call(
        body, name=name, grid=(S // ts,), in_specs=[row, row, slab3], out_specs=[row, lrow, slab3],
        out_shape=[jax.ShapeDtypeStruct((S, A), BF16), jax.ShapeDtypeStruct((S, LANES), F32),
                   jax.ShapeDtypeStruct((4, S, A), BF16)],
        compiler_params=_params("parallel"),
    )(dy, om, pb4)


def _attn_bwd(pb4, kv6, do, lse, dh, bias, dpb4, dkv6, g, dil, name):
    _, S, A = pb4.shape
    H = A // HEAD_DIM
    L = S // dil
    nb = L // BLOCK
    qv = pb4.reshape(4, L, dil * A)
    kvv = kv6.reshape(6, L, dil * A)
    dov = do.reshape(L, dil * A)
    lsev = lse.reshape(L, dil * LANES)
    dhv = dh.reshape(L, dil * LANES)
    dpv = dpb4.reshape(4, L, dil * A)
    has_dkv = dkv6 is not None
    n_in = 14 + (1 if has_dkv else 0)

    def body(*refs):
        (q_ref, qn_ref, kp_ref, kc_ref, vp_ref, vc_ref, do_ref, don_ref, l_ref, ln_ref, d_ref, dn_ref,
         b_ref) = refs[:13]
        dq_ref, dkv_ref, db_ref = refs[n_in:n_in + 3]
        r = pl.program_id(0)
        i = pl.program_id(1)
        qi = lax.broadcasted_iota(jnp.int32, (BLOCK, BLOCK), 0)
        ki = lax.broadcasted_iota(jnp.int32, (BLOCK, BLOCK), 1)
        mask_c = ki <= qi
        band = ki >= qi
        mask_p = jnp.logical_and(band, i > 0)
        mask_n = jnp.logical_and(band, i < nb - 1)
        lane = lax.broadcasted_iota(jnp.int32, (BLOCK, LANES), 1)

        @pl.when(jnp.logical_and(r == 0, i == 0))
        def _():
            db_ref[...] = jnp.zeros_like(db_ref)

        for h in range(H):
            sl = slice(h * HEAD_DIM, (h + 1) * HEAD_DIM)
            q_i, q_n = q_ref[:, sl], qn_ref[:, sl]
            k_p, k_c = kp_ref[:, sl], kc_ref[:, sl]
            v_p, v_c = vp_ref[:, sl], vc_ref[:, sl]
            do_i, do_n = do_ref[:, sl], don_ref[:, sl]
            l_i, l_n = _lane_col(l_ref[...], h, lane), _lane_col(ln_ref[...], h, lane)
            d_i, d_n = _lane_col(d_ref[...], h, lane), _lane_col(dn_ref[...], h, lane)
            b_c = b_ref[h, :, BLOCK:]
            b_p = b_ref[h, :, :BLOCK]
            s = jnp.where(mask_c, _dot(q_i, k_c, NT) * SCALE + b_c, NEG)
            p1 = jnp.exp(s - l_i)
            ds1 = p1 * (_dot(do_i, v_c, NT) - d_i)
            ds1b = ds1.astype(BF16)
            p1b = p1.astype(BF16)
            s = jnp.where(mask_p, _dot(q_i, k_p, NT) * SCALE + b_p, NEG)
            p2 = jnp.exp(s - l_i)
            ds2 = p2 * (_dot(do_i, v_p, NT) - d_i)
            ds2b = ds2.astype(BF16)
            s = jnp.where(mask_n, _dot(q_n, k_c, NT) * SCALE + b_p, NEG)
            p3 = jnp.exp(s - l_n)
            ds3b = (p3 * (_dot(do_n, v_c, NT) - d_n)).astype(BF16)
            p3b = p3.astype(BF16)
            dq = _dot(ds1b, k_c, NN) + _dot(ds2b, k_p, NN)
            dk = _dot(ds1b, q_i, TN) + _dot(ds3b, q_n, TN)
            dv = _dot(p1b, do_i, TN) + _dot(p3b, do_n, TN)
            dq_ref[:, sl] = (dq * SCALE).astype(BF16)
            dkv_ref[0, :, sl] = (dk * SCALE).astype(BF16)
            dkv_ref[1, :, sl] = dv.astype(BF16)
            db_ref[h, :, BLOCK:] += ds1
            db_ref[h, :, :BLOCK] += ds2

    def blk(slab, shift):
        if shift < 0:
            return pl.BlockSpec((None, BLOCK, A), lambda r, i: (slab, jnp.maximum(i - 1, 0), r))
        if shift > 0:
            return pl.BlockSpec((None, BLOCK, A), lambda r, i: (slab, jnp.minimum(i + 1, nb - 1), r))
        return pl.BlockSpec((None, BLOCK, A), lambda r, i: (slab, i, r))

    def row(width, shift):
        if shift > 0:
            return pl.BlockSpec((BLOCK, width), lambda r, i: (jnp.minimum(i + 1, nb - 1), r))
        return pl.BlockSpec((BLOCK, width), lambda r, i: (i, r))

    in_specs = [blk(g, 0), blk(g, 1), blk(2 * g, -1), blk(2 * g, 0), blk(2 * g + 1, -1), blk(2 * g + 1, 0),
                row(A, 0), row(A, 1), row(LANES, 0), row(LANES, 1), row(LANES, 0), row(LANES, 1),
                pl.BlockSpec((H, BLOCK, 2 * BLOCK), lambda r, i: (0, 0, 0)), ANY]
    operands = [qv, qv, kvv, kvv, kvv, kvv, dov, dov, lsev, lsev, dhv, dhv, bias, dpv]
    aliases = {13: 0}
    if has_dkv:
        in_specs.append(ANY)
        operands.append(dkv6.reshape(6, L, dil * A))
        aliases[14] = 1
    dpv, dkvv, db = pl.pallas_call(
        body, name=name, grid=(dil, nb), in_specs=in_specs,
        out_specs=[pl.BlockSpec((None, BLOCK, A), lambda r, i: (g, i, r)),
                   pl.BlockSpec((2, BLOCK, A), lambda r, i: (g, i, r)),
                   pl.BlockSpec((H, BLOCK, 2 * BLOCK), lambda r, i: (0, 0, 0))],
        out_shape=[jax.ShapeDtypeStruct((4, L, dil * A), BF16), jax.ShapeDtypeStruct((6, L, dil * A), BF16),
                   jax.ShapeDtypeStruct((H, BLOCK, 2 * BLOCK), F32)],
        input_output_aliases=aliases,
        compiler_params=_params("arbitrary", "arbitrary"),
    )(*operands)
    return dpv.reshape(4, S, A), dkvv.reshape(6, S, A), db


def _sum_leading(stack, out_dtype, name, tr=256, tc=2048):
    n, R, C = stack.shape
    tr = _tile(R, tr, 16)
    tc = _tile(C, tc)

    def body(s_ref, o_ref):
        acc = s_ref[0].astype(F32)
        for q in range(1, n):
            acc = acc + s_ref[q].astype(F32)
        o_ref[...] = acc.astype(out_dtype)

    return pl.pallas_call(
        body, name=name, grid=(R // tr, C // tc),
        in_specs=[pl.BlockSpec((n, tr, tc), lambda i, j: (0, i, j))],
        out_specs=pl.BlockSpec((tr, tc), lambda i, j: (i, j)),
        out_shape=jax.ShapeDtypeStruct((R, C), out_dtype),
        compiler_params=_params("parallel", "parallel"),
    )(stack)


def _add_half(g, t, c_idx, kind, name, tr=256, tc=2048):
    R, C = t.shape
    tr = _tile(R, tr, 16)
    tc = _tile(C, tc)
    nrb, ncb = R // tr, C // tc

    def body(c_ref, g_ref, t_ref, o_ref):
        del c_ref
        o_ref[...] = (g_ref[...].astype(F32) + t_ref[...].astype(F32)).astype(BF16)

    if kind == "col":
        g_map = lambda i, j, c_ref: (c_ref[0] * nrb + i, j)
    else:
        g_map = lambda i, j, c_ref: (i, c_ref[0] * ncb + j)
    same = lambda i, j, c_ref: (i, j)
    return pl.pallas_call(
        body, name=name,
        grid_spec=pltpu.PrefetchScalarGridSpec(
            num_scalar_prefetch=1, grid=(nrb, ncb),
            in_specs=[pl.BlockSpec((tr, tc), g_map), pl.BlockSpec((tr, tc), same)],
            out_specs=pl.BlockSpec((tr, tc), same)),
        out_shape=jax.ShapeDtypeStruct((R, C), BF16),
        compiler_params=_params("parallel", "parallel"),
    )(c_idx, g, t)


def _adamw(w, g, m, v, name, tr=256, tc=2048):
    R, C = w.shape
    tr = _tile(R, tr, 8)
    tc = _tile(C, tc)
    c1 = 1.0 - ADAM_B1 ** ADAM_STEP
    c2 = 1.0 - ADAM_B2 ** ADAM_STEP

    def body(w_ref, g_ref, m_ref, v_ref, d_ref, nm_ref, nv_ref):
        gv = g_ref[...]
        nm = ADAM_B1 * m_ref[...] + (1.0 - ADAM_B1) * gv
        nv = ADAM_B2 * v_ref[...] + (1.0 - ADAM_B2) * (gv * gv)
        d_ref[...] = -ADAM_LR * ((nm / c1) / (jnp.sqrt(nv / c2) + ADAM_EPS) + ADAM_WD * w_ref[...])
        nm_ref[...] = nm
        nv_ref[...] = nv

    blk = pl.BlockSpec((tr, tc), lambda i, j: (i, j))
    sh = jax.ShapeDtypeStruct((R, C), F32)
    return pl.pallas_call(
        body, name=name, grid=(R // tr, C // tc), in_specs=[blk] * 4, out_specs=[blk] * 3,
        out_shape=[sh, sh, sh], compiler_params=_params("parallel", "parallel"),
    )(w, g, m, v)


def _xyc():
    return lax.axis_index("x"), lax.axis_index("y"), lax.axis_index("c")


def _drain(copies):
    for cp in copies:
        if cp.is_remote:
            cp.wait_send()
        else:
            cp.wait()


def _other_chips(x, y):
    return [(1 - x, y), (x, 1 - y), (1 - x, 1 - y)]


def _allgather_small(blk, name):
    R, C = blk.shape

    def body(x_ref, out_ref, send_sems, recv_sems, local_sem):
        x, y, c = _xyc()
        me = 4 * x + 2 * y + c
        mine = pltpu.make_async_copy(x_ref, out_ref.at[me], local_sem)
        mine.start()
        peers = []
        for k in range(1, N_DEV):
            px = 1 - x if (k >> 2) & 1 else x
            py = 1 - y if (k >> 1) & 1 else y
            pc = 1 - c if k & 1 else c
            peers.append((px, py, pc))
        sends = []
        for k, peer in enumerate(peers):
            cp = pltpu.make_async_remote_copy(
                src_ref=x_ref, dst_ref=out_ref.at[me], send_sem=send_sems.at[k], recv_sem=recv_sems.at[k],
                device_id=peer, device_id_type=MESH)
            cp.start()
            sends.append(cp)
        for k, (px, py, pc) in enumerate(peers):
            pltpu.make_async_remote_copy(
                src_ref=x_ref, dst_ref=out_ref.at[4 * px + 2 * py + pc], send_sem=send_sems.at[k],
                recv_sem=recv_sems.at[k], device_id=(px, py, pc), device_id_type=MESH).wait_recv()
        for cp in sends:
            cp.wait_send()
        mine.wait()

    return pl.pallas_call(
        body, name=name, in_specs=[VMEM_SPEC], out_specs=VMEM_SPEC,
        out_shape=jax.ShapeDtypeStruct((N_DEV, R, C), blk.dtype),
        scratch_shapes=[pltpu.SemaphoreType.DMA((N_DEV - 1,)), pltpu.SemaphoreType.DMA((N_DEV - 1,)),
                        pltpu.SemaphoreType.DMA],
        compiler_params=pltpu.CompilerParams(vmem_limit_bytes=VMEM_LIMIT),
    )(blk)


def _full_region(ref, kind, chip, half, shard_shape):
    r, cn = shard_shape
    hr = r // 2
    if kind == "col":
        rows = pl.ds(0, r) if half is None else pl.ds(pl.multiple_of(half * hr, 16), hr)
        return ref.at[rows, pl.ds(pl.multiple_of(chip * cn, LANES), cn)]
    if half is None:
        return ref.at[pl.ds(pl.multiple_of(chip * r, 16), r), :]
    return ref.at[pl.ds(pl.multiple_of(chip * r + half * hr, 16), hr), :]


def _allgather_weights(shards, kinds, name):
    n = len(shards)
    shapes = [s.shape for s in shards]

    def body(*refs):
        srcs = refs[:n]
        outs = refs[n:2 * n]
        send_sems, recv_sems, local_sems = refs[2 * n:]
        x, y, c = _xyc()
        chip = 2 * x + y
        sib = (x, y, 1 - c)
        others = _other_chips(x, y)
        started = []
        for w in range(n):
            r = shapes[w][0]
            cp = pltpu.make_async_copy(srcs[w], _full_region(outs[w], kinds[w], chip, None, shapes[w]),
                                       local_sems.at[w])
            cp.start()
            started.append(cp)
            for j, (ox, oy) in enumerate(others):
                cp = pltpu.make_async_remote_copy(
                    src_ref=srcs[w].at[pl.ds(pl.multiple_of(c * (r // 2), 16), r // 2), :],
                    dst_ref=_full_region(outs[w], kinds[w], chip, c, shapes[w]),
                    send_sem=send_sems.at[6 * w + j], recv_sem=recv_sems.at[6 * w + j],
                    device_id=(ox, oy, c), device_id_type=MESH)
                cp.start()
                started.append(cp)
        for w in range(n):
            for j, (ox, oy) in enumerate(others):
                landed = _full_region(outs[w], kinds[w], 2 * ox + oy, c, shapes[w])
                pltpu.make_async_remote_copy(
                    src_ref=landed, dst_ref=landed, send_sem=send_sems.at[6 * w + j], recv_sem=recv_sems.at[6 * w + j],
                    device_id=(ox, oy, c), device_id_type=MESH).wait_recv()
                cp = pltpu.make_async_remote_copy(
                    src_ref=landed, dst_ref=landed, send_sem=send_sems.at[6 * w + 3 + j],
                    recv_sem=recv_sems.at[6 * w + 3 + j], device_id=sib, device_id_type=MESH)
                cp.start()
                started.append(cp)
        for w in range(n):
            for j, (ox, oy) in enumerate(others):
                theirs = _full_region(outs[w], kinds[w], 2 * ox + oy, 1 - c, shapes[w])
                pltpu.make_async_remote_copy(
                    src_ref=theirs, dst_ref=theirs, send_sem=send_sems.at[6 * w + 3 + j],
                    recv_sem=recv_sems.at[6 * w + 3 + j], device_id=sib, device_id_type=MESH).wait_recv()
        _drain(started)

    out_shape = []
    for s, kind in zip(shards, kinds):
        r, cn = s.shape
        full = (r, N_CHIPS * cn) if kind == "col" else (N_CHIPS * r, cn)
        out_shape.append(jax.ShapeDtypeStruct(full, s.dtype))
    return pl.pallas_call(
        body, name=name, in_specs=[ANY] * n, out_specs=[ANY] * n, out_shape=out_shape,
        scratch_shapes=[pltpu.SemaphoreType.DMA((6 * n,)), pltpu.SemaphoreType.DMA((6 * n,)),
                        pltpu.SemaphoreType.DMA((n,))],
    )(*shards)


def _half_of(ref, kind, half):
    r, cn = ref.shape
    if kind == "col":
        return ref.at[pl.ds(pl.multiple_of(half * (r // 2), 16), r // 2), :]
    return ref.at[:, pl.ds(pl.multiple_of(half * (cn // 2), LANES), cn // 2)]


def _shard_of(ref, kind, chip):
    r, cn = ref.shape
    if kind == "col":
        return ref.at[:, pl.ds(pl.multiple_of(chip * (cn // N_CHIPS), LANES), cn // N_CHIPS)]
    return ref.at[pl.ds(pl.multiple_of(chip * (r // N_CHIPS), 16), r // N_CHIPS), :]


def _exchange_halves(grads, kinds, name):
    n = len(grads)

    def body(*refs):
        gs = refs[:n]
        ts = refs[n:2 * n]
        send_sems, recv_sems = refs[2 * n:]
        x, y, c = _xyc()
        cps = []
        for w in range(n):
            cp = pltpu.make_async_remote_copy(
                src_ref=_half_of(gs[w], kinds[w], 1 - c), dst_ref=ts[w], send_sem=send_sems.at[w],
                recv_sem=recv_sems.at[w], device_id=(x, y, 1 - c), device_id_type=MESH)
            cp.start()
            cps.append(cp)
        for cp in cps:
            cp.wait()

    out_shape = []
    for gr, kind in zip(grads, kinds):
        r, cn = gr.shape
        out_shape.append(jax.ShapeDtypeStruct((r // 2, cn) if kind == "col" else (r, cn // 2), gr.dtype))
    return pl.pallas_call(
        body, name=name, in_specs=[ANY] * n, out_specs=[ANY] * n, out_shape=out_shape,
        scratch_shapes=[pltpu.SemaphoreType.DMA((n,)), pltpu.SemaphoreType.DMA((n,))],
    )(*grads)


def _scatter_partials(parts, kinds, name):
    n = len(parts)

    def body(*refs):
        ps = refs[:n]
        us = refs[n:2 * n]
        send_sems, recv_sems, local_sems = refs[2 * n:]
        x, y, c = _xyc()
        chip = 2 * x + y
        others = _other_chips(x, y)
        cps = []
        for w in range(n):
            cp = pltpu.make_async_copy(_shard_of(ps[w], kinds[w], chip), us[w].at[chip], local_sems.at[w])
            cp.start()
            cps.append(cp)
            for j, (ox, oy) in enumerate(others):
                cp = pltpu.make_async_remote_copy(
                    src_ref=_shard_of(ps[w], kinds[w], 2 * ox + oy), dst_ref=us[w].at[chip],
                    send_sem=send_sems.at[3 * w + j], recv_sem=recv_sems.at[3 * w + j],
                    device_id=(ox, oy, c), device_id_type=MESH)
                cp.start()
                cps.append(cp)
        for w in range(n):
            for j, (ox, oy) in enumerate(others):
                slot = us[w].at[2 * ox + oy]
                pltpu.make_async_remote_copy(
                    src_ref=slot, dst_ref=slot, send_sem=send_sems.at[3 * w + j], recv_sem=recv_sems.at[3 * w + j],
                    device_id=(ox, oy, c), device_id_type=MESH).wait_recv()
        _drain(cps)

    out_shape = []
    for p, kind in zip(parts, kinds):
        r, cn = p.shape
        hs = (r, cn // N_CHIPS) if kind == "col" else (r // N_CHIPS, cn)
        out_shape.append(jax.ShapeDtypeStruct((N_CHIPS,) + hs, p.dtype))
    return pl.pallas_call(
        body, name=name, in_specs=[ANY] * n, out_specs=[ANY] * n, out_shape=out_shape,
        scratch_shapes=[pltpu.SemaphoreType.DMA((3 * n,)), pltpu.SemaphoreType.DMA((3 * n,)),
                        pltpu.SemaphoreType.DMA((n,))],
    )(*parts)


def _join_halves(halves, kinds, name):
    n = len(halves)

    def body(*refs):
        hs = refs[:n]
        outs = refs[n:2 * n]
        send_sems, recv_sems, local_sems = refs[2 * n:]
        x, y, c = _xyc()
        cps = []
        for w in range(n):
            mine = _half_of(outs[w], kinds[w], c)
            cp = pltpu.make_async_copy(hs[w], mine, local_sems.at[w])
            cp.start()
            cps.append(cp)
            cp = pltpu.make_async_remote_copy(
                src_ref=hs[w], dst_ref=mine, send_sem=send_sems.at[w], recv_sem=recv_sems.at[w],
                device_id=(x, y, 1 - c), device_id_type=MESH)
            cp.start()
            cps.append(cp)
        for w in range(n):
            theirs = _half_of(outs[w], kinds[w], 1 - c)
            pltpu.make_async_remote_copy(
                src_ref=theirs, dst_ref=theirs, send_sem=send_sems.at[w], recv_sem=recv_sems.at[w],
                device_id=(x, y, 1 - c), device_id_type=MESH).wait_recv()
        _drain(cps)

    out_shape = []
    for h, kind in zip(halves, kinds):
        r, cn = h.shape
        out_shape.append(jax.ShapeDtypeStruct((2 * r, cn) if kind == "col" else (r, 2 * cn), h.dtype))
    return pl.pallas_call(
        body, name=name, in_specs=[ANY] * n, out_specs=[ANY] * n, out_shape=out_shape,
        scratch_shapes=[pltpu.SemaphoreType.DMA((n,)), pltpu.SemaphoreType.DMA((n,)),
                        pltpu.SemaphoreType.DMA((n,))],
    )(*halves)


def kernel(x, a_norm, a_w_in, a_conv_w, a_conv_b, a_ln_g, a_ln_b, a_w_out, kv_norm, w_kv, b_norm, b_w_in, b_w_out, rel_bias, final_norm, loss_target, m_a_norm, m_a_w_in, m_a_conv_w, m_a_conv_b, m_a_ln_g, m_a_ln_b, m_a_w_out, m_kv_norm, m_w_kv, m_b_norm, m_b_w_in, m_b_w_out, m_rel_bias, m_final_norm, v_a_norm, v_a_w_in, v_a_conv_w, v_a_conv_b, v_a_ln_g, v_a_ln_b, v_a_w_out, v_kv_norm, v_w_kv, v_b_norm, v_b_w_in, v_b_w_out, v_rel_bias, v_final_norm):
    S, D = x.shape[1], x.shape[2]
    E = a_w_out.shape[1] * N_CHIPS
    A = b_w_out.shape[1] * N_CHIPS
    H = A // HEAD_DIM
    DC = D // N_CHIPS
    xs = x.reshape(S, D)
    tgt = loss_target.reshape(S, D)
    cx, cy, cc = _xyc()
    chip = 2 * cx + cy
    c_idx = jnp.reshape(cc, (1,)).astype(jnp.int32)

    big_names = ["a_w_in", "a_w_out", "w_kv", "b_w_in", "b_w_out"]
    kinds = ["col", "row", "col", "col", "row"]
    big_w = [a_w_in[0], a_w_out[0], w_kv, b_w_in[0], b_w_out[0]]
    big_m = [m_a_w_in[0], m_a_w_out[0], m_w_kv, m_b_w_in[0], m_b_w_out[0]]
    big_v = [v_a_w_in[0], v_a_w_out[0], v_w_kv, v_b_w_in[0], v_b_w_out[0]]
    wa_in, wa_out, wkv, wb_in, wb_out = _allgather_weights([w.astype(BF16) for w in big_w], kinds, "ag_weights")

    def pack_sharded(an, cw, cb, lg, lb):
        return jnp.concatenate([an, cw[0], cb, lg, lb, jnp.zeros((5, DC), F32)], axis=0)

    small_w = pack_sharded(a_norm, a_conv_w, a_conv_b, a_ln_g, a_ln_b)
    gathered = _allgather_small(small_w, "ag_small_params")
    small_full = jnp.concatenate([gathered[2 * k] for k in range(N_CHIPS)], axis=1)
    g_a = small_full[0:1]
    conv_w32 = small_full[1:1 + HALO]
    conv_b = small_full[32:33]
    ln_g = small_full[33:34]
    ln_b = small_full[34:35]
    g_kv = kv_norm.reshape(1, D)
    g_b = b_norm.reshape(1, D)
    g_f = final_norm.reshape(1, D)

    rb_t = jnp.pad(rel_bias.T, ((0, 0), (0, LANES - N_BUCKETS)))
    onehots = [_onehot(dil) for _, dil in GROUPS]
    biases = [_bias_table(rb_t, onehots[g], "bias_table_%d" % g).reshape(H, BLOCK, 2 * BLOCK)
              for g in range(len(GROUPS))]

    (h0,) = _rms_fwd(xs, [g_a], "rms_a")
    proj3 = _matmul(h0, wa_in, "nn", BF16, "mm_a_in", out_slab=E)
    conv = _conv_fwd(proj3, conv_w32, conv_b, "conv_fwd")
    y_a = _ln_gate_fwd(conv, proj3, ln_g, ln_b, "ln_gate_fwd")
    x1 = _matmul(y_a, wa_out, "nn", F32, "mm_a_out", res=xs)
    hk, hb = _rms_fwd(x1, [g_kv, g_b], "rms_kv_b")
    kv6 = _matmul(hk, wkv, "nn", BF16, "mm_kv", out_slab=A)
    pb4 = _matmul(hb, wb_in, "nn", BF16, "mm_b_in", out_slab=A)
    os_, lses = [], []
    for g, (_, dil) in enumerate(GROUPS):
        o_g, lse_g = _attn_fwd(pb4, kv6, biases[g], g, dil, "attn_fwd_%d" % g)
        os_.append(o_g)
        lses.append(lse_g)
    y_b, o_m, lse = _attn_merge(os_, lses, pb4, "attn_merge")
    x2 = _matmul(y_b, wb_out, "nn", F32, "mm_b_out", res=x1)
    loss_part, dx2, dx2b, gg_f = _final_head(x2, g_f, tgt, "final_head")
    loss = lax.psum(loss_part[0, 0], ("x", "y", "c"))

    dy_b = _matmul(dx2b, wb_out, "nt", BF16, "mm_b_out_dx")
    dwb_out = _matmul(y_b, dx2b, "tn", BF16, "mm_b_out_dw", tm=512, tn=1024, tk=1024)
    do, dh, dpb4 = _gate_bwd(dy_b, o_m, pb4, "gate_bwd")
    dkv6 = None
    dbs = []
    for g, (_, dil) in enumerate(GROUPS):
        dpb4, dkv6, db = _attn_bwd(pb4, kv6, do, lse, dh, biases[g], dpb4, dkv6, g, dil, "attn_bwd_%d" % g)
        dbs.append(db.reshape(H, BLOCK * 2 * BLOCK))
    g_rel_t = _bias_grad(dbs, onehots, "bias_grad")
    dhb = _matmul(dpb4, wb_in, "nt", BF16, "mm_b_in_dx", a_slab=True)
    dwb_in = _matmul(hb, dpb4, "tn", BF16, "mm_b_in_dw", b_slab=True, tm=512, tn=1024, tk=1024)
    dhk = _matmul(dkv6, wkv, "nt", BF16, "mm_kv_dx", a_slab=True)
    dwkv = _matmul(hk, dkv6, "tn", BF16, "mm_kv_dw", b_slab=True, tm=512, tn=1024, tk=1024)
    dx1, dx1b, gg_kvb = _rms_bwd(x1, [dhk, dhb], [g_kv, g_b], dx2, "rms_kv_b_bwd")
    dy_a = _matmul(dx1b, wa_out, "nt", BF16, "mm_a_out_dx")
    dwa_out = _matmul(y_a, dx1b, "tn", BF16, "mm_a_out_dw", tm=512, tn=1024, tk=1024)
    dconv, dproj3, gg_ln = _ln_gate_bwd(conv, proj3, dy_a, ln_g, ln_b, "ln_gate_bwd")
    dproj3, g_conv_w = _conv_bwd(proj3, dconv, conv_w32, dproj3, "conv_bwd")
    dh0 = _matmul(dproj3, wa_in, "nt", BF16, "mm_a_in_dx", a_slab=True)
    dwa_in = _matmul(h0, dproj3, "tn", BF16, "mm_a_in_dw", b_slab=True, tm=512, tn=1024, tk=1024)
    grad_x, _, gg_a = _rms_bwd(xs, [dh0], [g_a], dx1, "rms_a_bwd")

    full_grads = [dwa_in, dwa_out, dwkv, dwb_in, dwb_out]
    theirs = _exchange_halves(full_grads, kinds, "rs_exchange_halves")
    parts = [_add_half(full_grads[w], theirs[w], c_idx, kinds[w], "rs_add_half_%d" % w) for w in range(5)]
    slots = _scatter_partials(parts, kinds, "rs_scatter_partials")
    halves = []
    for w in range(5):
        n4, r, cn = slots[w].shape
        halves.append(_sum_leading(slots[w], F32, "rs_sum_chips_%d" % w))
    big_g = _join_halves(halves, kinds, "rs_join_halves")

    g_rel_row = jnp.pad(g_rel_t[:, :N_BUCKETS].T.reshape(1, N_BUCKETS * H), ((0, 0), (0, D - N_BUCKETS * H)))
    small_g = jnp.concatenate([
        gg_a[0:1], g_conv_w[0:CONV_TAPS], gg_ln[2:3], gg_ln[0:1], gg_ln[1:2],
        gg_kvb[0:1], gg_kvb[1:2], gg_f[0:1], g_rel_row, jnp.zeros((1, D), F32)], axis=0)
    small_sum = _sum_leading(_allgather_small(small_g, "ag_small_grads"), F32, "sum_small_grads", tr=40)
    g_sharded = lax.dynamic_slice(small_sum, (0, chip * DC), (40, DC))
    g_repl = jnp.concatenate([small_sum[35:39], jnp.zeros((4, D), F32)], axis=0)

    outs_g, outs_d, outs_m, outs_v = {}, {}, {}, {}
    for w, nm in enumerate(big_names):
        d_, m_, v_ = _adamw(big_w[w], big_g[w], big_m[w], big_v[w], "adamw_" + nm)
        outs_g[nm], outs_d[nm], outs_m[nm], outs_v[nm] = big_g[w], d_, m_, v_
    sm_m = pack_sharded(m_a_norm, m_a_conv_w, m_a_conv_b, m_a_ln_g, m_a_ln_b)
    sm_v = pack_sharded(v_a_norm, v_a_conv_w, v_a_conv_b, v_a_ln_g, v_a_ln_b)
    sd, smm, svv = _adamw(small_w, g_sharded, sm_m, sm_v, "adamw_small_sharded")

    def unpack_sharded(p):
        return {"a_norm": p[0:1], "a_conv_w": p[1:32].reshape(1, CONV_TAPS, DC), "a_conv_b": p[32:33],
                "a_ln_g": p[33:34], "a_ln_b": p[34:35]}

    for src, dst in ((g_sharded, outs_g), (sd, outs_d), (smm, outs_m), (svv, outs_v)):
        dst.update(unpack_sharded(src))

    def pack_repl(kn, bn, fn, rb):
        rbrow = jnp.pad(rb.reshape(1, N_BUCKETS * H), ((0, 0), (0, D - N_BUCKETS * H)))
        return jnp.concatenate([kn.reshape(1, D), bn.reshape(1, D), fn.reshape(1, D), rbrow, jnp.zeros((4, D), F32)], axis=0)

    rp_w = pack_repl(kv_norm, b_norm, final_norm, rel_bias)
    rp_m = pack_repl(m_kv_norm, m_b_norm, m_final_norm, m_rel_bias)
    rp_v = pack_repl(v_kv_norm, v_b_norm, v_final_norm, v_rel_bias)
    rd, rmm, rvv = _adamw(rp_w, g_repl, rp_m, rp_v, "adamw_small_replicated")

    def unpack_repl(p):
        return {"kv_norm": p[0], "b_norm": p[1:2], "final_norm": p[2],
                "rel_bias": p[3, :N_BUCKETS * H].reshape(N_BUCKETS, H)}

    for src, dst in ((g_repl, outs_g), (rd, outs_d), (rmm, outs_m), (rvv, outs_v)):
        dst.update(unpack_repl(src))

    order = ["a_norm", "a_w_in", "a_conv_w", "a_conv_b", "a_ln_g", "a_ln_b", "a_w_out", "kv_norm", "w_kv",
             "b_norm", "b_w_in", "b_w_out", "rel_bias", "final_norm"]
    lead = {"a_w_in", "a_w_out", "b_w_in", "b_w_out"}

    def shaped(nm, val):
        return val[None] if nm in lead else val

    result = [loss, grad_x.reshape(1, S, D)]
    for table in (outs_g, outs_d, outs_m, outs_v):
        result.extend(shaped(nm, table[nm]) for nm in order)
    return tuple(result)
```

```python
import functools

import numpy as np
import jax
import jax.numpy as jnp
from jax import lax
from jax.experimental import pallas as pl
from jax.experimental.pallas import tpu as pltpu

F32 = jnp.float32
BF16 = jnp.bfloat16
MESH = pl.DeviceIdType.MESH
ANY = pl.BlockSpec(memory_space=pl.ANY)
VMEM_SPEC = pl.BlockSpec(memory_space=pltpu.VMEM)

EPS = 1e-6
HEAD_DIM = 128
BLOCK = 128
GROUPS = ((128, 1), (512, 4), (2048, 16))
SCALE = HEAD_DIM ** -0.5
CONV_TAPS = 31
HALO = 32
N_BUCKETS = 32
MAX_EXACT = 16
MAX_DISTANCE = 2048
NEG = -1e30
N_CHIPS = 4
N_DEV = 8
LANES = 128
VMEM_LIMIT = 56 * 1024 * 1024

ADAM_LR = 0.001
ADAM_B1 = 0.9
ADAM_B2 = 0.999
ADAM_EPS = 1e-08
ADAM_WD = 0.01
ADAM_STEP = 10


def _tile(n, pref, mult=LANES):
    t = (min(pref, n) // mult) * mult
    while t >= mult:
        if n % t == 0:
            return t
        t -= mult
    return n


def _params(*sem):
    return pltpu.CompilerParams(dimension_semantics=sem, vmem_limit_bytes=VMEM_LIMIT)


def _sigmoid(v):
    return 1.0 / (1.0 + jnp.exp(-v))


def _dot(a, b, dims):
    return lax.dot_general(a, b, (dims, ((), ())), preferred_element_type=F32)


NN = ((1,), (0,))
NT = ((1,), (1,))
TN = ((0,), (0,))


def _stack_rows(rows, total):
    width = rows[0].shape[1]
    rid = lax.broadcasted_iota(jnp.int32, (total, width), 0)
    out = jnp.zeros((total, width), F32)
    for q, row in enumerate(rows):
        out = jnp.where(rid == q, jnp.broadcast_to(row, (total, width)), out)
    return out


def _lane_col(arr, h, lane):
    return jnp.sum(jnp.where(lane == h, arr, 0.0), axis=-1, keepdims=True)


def _matmul(a, b, mode, out_dtype, name, res=None, a_slab=False, b_slab=False, out_slab=0,
            tm=512, tn=1024, tk=2048):
    if a_slab:
        na, M, W = a.shape
        K = na * W
    elif mode == "tn":
        K, M = a.shape
    else:
        M, K = a.shape
    if b_slab:
        nbs, _, Wb = b.shape
        N = nbs * Wb
    elif mode == "nt":
        N = b.shape[0]
    else:
        N = b.shape[1]
    tm = _tile(M, tm)
    tn = _tile(Wb if b_slab else (out_slab if out_slab else N), tn)
    tk = _tile(W if a_slab else K, tk)
    nk = K // tk
    grid = (M // tm, N // tn, nk)

    if a_slab:
        per = W // tk
        a_spec = pl.BlockSpec((None, tm, tk), lambda i, j, k: (k // per, i, k % per))
    elif mode == "tn":
        a_spec = pl.BlockSpec((tk, tm), lambda i, j, k: (k, i))
    else:
        a_spec = pl.BlockSpec((tm, tk), lambda i, j, k: (i, k))
    if b_slab:
        perb = Wb // tn
        b_spec = pl.BlockSpec((None, tk, tn), lambda i, j, k: (j // perb, k, j % perb))
    elif mode == "nt":
        b_spec = pl.BlockSpec((tn, tk), lambda i, j, k: (j, k))
    else:
        b_spec = pl.BlockSpec((tk, tn), lambda i, j, k: (k, j))
    if out_slab:
        pero = out_slab // tn
        o_spec = pl.BlockSpec((None, tm, tn), lambda i, j, k: (j // pero, i, j % pero))
        out_shape = jax.ShapeDtypeStruct((N // out_slab, M, out_slab), out_dtype)
    else:
        o_spec = pl.BlockSpec((tm, tn), lambda i, j, k: (i, j))
        out_shape = jax.ShapeDtypeStruct((M, N), out_dtype)
    in_specs = [a_spec, b_spec]
    operands = [a, b]
    if res is not None:
        in_specs.append(pl.BlockSpec((tm, tn), lambda i, j, k: (i, j)))
        operands.append(res)
    dims = {"nn": NN, "nt": NT, "tn": TN}[mode]
    has_res = res is not None

    def body(*refs):
        a_ref, b_ref = refs[0], refs[1]
        r_ref = refs[2] if has_res else None
        o_ref = refs[3] if has_res else refs[2]
        prod = _dot(a_ref[...], b_ref[...], dims)

        def finish(val):
            if has_res:
                val = val + r_ref[...]
            o_ref[...] = val.astype(out_dtype)

        if nk == 1:
            finish(prod)
        else:
            acc_ref = refs[-1]
            k = pl.program_id(2)

            @pl.when(k == 0)
            def _():
                acc_ref[...] = prod

            @pl.when(k > 0)
            def _():
                acc_ref[...] += prod

            @pl.when(k == nk - 1)
            def _():
                finish(acc_ref[...])

    scratch = [pltpu.VMEM((tm, tn), F32)] if nk > 1 else []
    return pl.pallas_call(
        body, name=name, grid=grid, in_specs=in_specs, out_specs=o_spec, out_shape=out_shape,
        scratch_shapes=scratch,
        compiler_params=_params("parallel", "parallel", "arbitrary"),
    )(*operands)


def _rms_fwd(x, gains, name, ts=256):
    S, D = x.shape
    ts = _tile(S, ts, 16)
    n = len(gains)

    def body(*refs):
        xv = refs[0][...]
        nrm = xv * lax.rsqrt(jnp.mean(xv * xv, axis=-1, keepdims=True) + EPS)
        for q in range(n):
            refs[1 + n + q][...] = (nrm * refs[1 + q][...]).astype(BF16)

    row = pl.BlockSpec((ts, D), lambda i: (i, 0))
    vec = pl.BlockSpec((1, D), lambda i: (0, 0))
    return pl.pallas_call(
        body, name=name, grid=(S // ts,), in_specs=[row] + [vec] * n, out_specs=[row] * n,
        out_shape=[jax.ShapeDtypeStruct((S, D), BF16)] * n,
        compiler_params=_params("parallel"),
    )(x, *gains)


def _rms_bwd(x, dhs, gains, dres, name, ts=256):
    S, D = x.shape
    ts = _tile(S, ts, 16)
    n = len(dhs)

    def body(*refs):
        x_ref = refs[0]
        dh_refs = refs[1:1 + n]
        g_refs = refs[1 + n:1 + 2 * n]
        dres_ref = refs[1 + 2 * n]
        dx_ref, dxb_ref, gg_ref = refs[2 + 2 * n:5 + 2 * n]
        i = pl.program_id(0)
        xv = x_ref[...]
        r = lax.rsqrt(jnp.mean(xv * xv, axis=-1, keepdims=True) + EPS)
        nrm = xv * r
        dn = jnp.zeros_like(xv)
        rows = []
        for q in range(n):
            dh = dh_refs[q][...].astype(F32)
            dn = dn + dh * g_refs[q][...]
            rows.append(jnp.sum(dh * nrm, axis=0, keepdims=True))
        dx = dres_ref[...] + r * (dn - nrm * jnp.mean(dn * nrm, axis=-1, keepdims=True))
        dx_ref[...] = dx
        dxb_ref[...] = dx.astype(BF16)
        upd = _stack_rows(rows, 8)

        @pl.when(i == 0)
        def _():
            gg_ref[...] = upd

        @pl.when(i > 0)
        def _():
            gg_ref[...] += upd

    row = pl.BlockSpec((ts, D), lambda i: (i, 0))
    vec = pl.BlockSpec((1, D), lambda i: (0, 0))
    acc = pl.BlockSpec((8, D), lambda i: (0, 0))
    return pl.pallas_call(
        body, name=name, grid=(S // ts,), in_specs=[row] + [row] * n + [vec] * n + [row],
        out_specs=[row, row, acc],
        out_shape=[jax.ShapeDtypeStruct((S, D), F32), jax.ShapeDtypeStruct((S, D), BF16),
                   jax.ShapeDtypeStruct((8, D), F32)],
        compiler_params=_params("arbitrary"),
    )(x, *dhs, *gains, dres)


def _final_head(x2, gain, target, name, ts=256):
    S, D = x2.shape
    ts = _tile(S, ts, 16)

    def body(x_ref, g_ref, t_ref, loss_ref, dx_ref, dxb_ref, gg_ref):
        i = pl.program_id(0)
        xv = x_ref[...]
        g = g_ref[...]
        r = lax.rsqrt(jnp.mean(xv * xv, axis=-1, keepdims=True) + EPS)
        nrm = xv * r
        err = nrm * g - t_ref[...]
        part = 0.5 * jnp.sum(jnp.mean(err * err, axis=-1, keepdims=True), axis=0, keepdims=True)
        dout = err * (1.0 / D)
        dn = dout * g
        dx = r * (dn - nrm * jnp.mean(dn * nrm, axis=-1, keepdims=True))
        dx_ref[...] = dx
        dxb_ref[...] = dx.astype(BF16)
        upd = _stack_rows([jnp.sum(dout * nrm, axis=0, keepdims=True)], 8)
        lpart = jnp.broadcast_to(part, (1, LANES))

        @pl.when(i == 0)
        def _():
            gg_ref[...] = upd
            loss_ref[...] = lpart

        @pl.when(i > 0)
        def _():
            gg_ref[...] += upd
            loss_ref[...] += lpart

    row = pl.BlockSpec((ts, D), lambda i: (i, 0))
    vec = pl.BlockSpec((1, D), lambda i: (0, 0))
    return pl.pallas_call(
        body, name=name, grid=(S // ts,), in_specs=[row, vec, row],
        out_specs=[pl.BlockSpec((1, LANES), lambda i: (0, 0)), row, row, pl.BlockSpec((8, D), lambda i: (0, 0))],
        out_shape=[jax.ShapeDtypeStruct((1, LANES), F32), jax.ShapeDtypeStruct((S, D), F32),
                   jax.ShapeDtypeStruct((S, D), BF16), jax.ShapeDtypeStruct((8, D), F32)],
        compiler_params=_params("arbitrary"),
    )(x2, gain, target)


CONV_ROWS = 64


def _conv_fwd(proj3, conv_w32, conv_b, name, ts=256, cw=256):
    _, S, E = proj3.shape
    ts = _tile(S, ts, HALO)
    cw = _tile(E, cw)
    per = ts // HALO
    rc = min(CONV_ROWS, ts)

    def body(a_ref, b_ref, ap_ref, bp_ref, w_ref, cb_ref, c_ref, ubuf):
        i = pl.program_id(0)
        up = ap_ref[...].astype(F32) * _sigmoid(bp_ref[...].astype(F32))
        ubuf[0:HALO, :] = jnp.where(i > 0, up, 0.0)
        ubuf[HALO:HALO + ts, :] = a_ref[...].astype(F32) * _sigmoid(b_ref[...].astype(F32))
        for r0 in range(0, ts, rc):
            acc = jnp.broadcast_to(cb_ref[...], (rc, cw))
            for k in range(CONV_TAPS):
                off = r0 + HALO - (CONV_TAPS - 1) + k
                acc = acc + ubuf[off:off + rc, :] * w_ref[k:k + 1, :]
            c_ref[r0:r0 + rc, :] = acc

    return pl.pallas_call(
        body, name=name, grid=(S // ts, E // cw),
        in_specs=[
            pl.BlockSpec((None, ts, cw), lambda i, j: (0, i, j)),
            pl.BlockSpec((None, ts, cw), lambda i, j: (1, i, j)),
            pl.BlockSpec((None, HALO, cw), lambda i, j: (0, jnp.maximum(i * per - 1, 0), j)),
            pl.BlockSpec((None, HALO, cw), lambda i, j: (1, jnp.maximum(i * per - 1, 0), j)),
            pl.BlockSpec((HALO, cw), lambda i, j: (0, j)),
            pl.BlockSpec((1, cw), lambda i, j: (0, j)),
        ],
        out_specs=pl.BlockSpec((ts, cw), lambda i, j: (i, j)),
        out_shape=jax.ShapeDtypeStruct((S, E), F32),
        scratch_shapes=[pltpu.VMEM((HALO + ts, cw), F32)],
        compiler_params=_params("parallel", "parallel"),
    )(proj3, proj3, proj3, proj3, conv_w32, conv_b)


def _ln_gate_fwd(c, proj3, ln_g, ln_b, name, ts=256):
    S, E = c.shape
    ts = _tile(S, ts, 16)

    def body(c_ref, z_ref, g_ref, b_ref, y_ref):
        cv = c_ref[...]
        mu = jnp.mean(cv, axis=-1, keepdims=True)
        d = cv - mu
        var = jnp.mean(d * d, axis=-1, keepdims=True)
        cn = d * lax.rsqrt(var + EPS) * g_ref[...] + b_ref[...]
        z = z_ref[...].astype(F32)
        y_ref[...] = ((cn * _sigmoid(cn)).astype(F32) * (z * _sigmoid(z))).astype(BF16)

    row = pl.BlockSpec((ts, E), lambda i: (i, 0))
    vec = pl.BlockSpec((1, E), lambda i: (0, 0))
    return pl.pallas_call(
        body, name=name, grid=(S // ts,),
        in_specs=[row, pl.BlockSpec((None, ts, E), lambda i: (2, i, 0)), vec, vec],
        out_specs=row, out_shape=jax.ShapeDtypeStruct((S, E), BF16),
        compiler_params=_params("parallel"),
    )(c, proj3, ln_g, ln_b)


def _ln_gate_bwd(c, proj3, dy, ln_g, ln_b, name, ts=256):
    S, E = c.shape
    ts = _tile(S, ts, 16)

    def body(c_ref, z_ref, dy_ref, g_ref, b_ref, dc_ref, dz_ref, acc_ref):
        i = pl.program_id(0)
        cv = c_ref[...]
        g = g_ref[...]
        mu = jnp.mean(cv, axis=-1, keepdims=True)
        d = cv - mu
        var = jnp.mean(d * d, axis=-1, keepdims=True)
        rstd = lax.rsqrt(var + EPS)
        chat = d * rstd
        cn = chat * g + b_ref[...]
        z = z_ref[...].astype(F32)
        dyv = dy_ref[...].astype(F32)
        sc = _sigmoid(cn)
        sz = _sigmoid(z)
        dcn = dyv * (z * sz) * (sc * (1.0 + cn * (1.0 - sc)))
        dz_ref[...] = (dyv * (cn * sc) * (sz * (1.0 + z * (1.0 - sz)))).astype(BF16)
        dchat = dcn * g
        dcv = rstd * (dchat - jnp.mean(dchat, axis=-1, keepdims=True)
                      - chat * jnp.mean(dchat * chat, axis=-1, keepdims=True))
        dc_ref[...] = dcv
        upd = _stack_rows([jnp.sum(dcn * chat, axis=0, keepdims=True),
                           jnp.sum(dcn, axis=0, keepdims=True),
                           jnp.sum(dcv, axis=0, keepdims=True)], 8)

        @pl.when(i == 0)
        def _():
            acc_ref[...] = upd

        @pl.when(i > 0)
        def _():
            acc_ref[...] += upd

    row = pl.BlockSpec((ts, E), lambda i: (i, 0))
    vec = pl.BlockSpec((1, E), lambda i: (0, 0))
    return pl.pallas_call(
        body, name=name, grid=(S // ts,),
        in_specs=[row, pl.BlockSpec((None, ts, E), lambda i: (2, i, 0)), row, vec, vec],
        out_specs=[row, pl.BlockSpec((None, ts, E), lambda i: (2, i, 0)), pl.BlockSpec((8, E), lambda i: (0, 0))],
        out_shape=[jax.ShapeDtypeStruct((S, E), F32), jax.ShapeDtypeStruct((3, S, E), BF16),
                   jax.ShapeDtypeStruct((8, E), F32)],
        compiler_params=_params("arbitrary"),
    )(c, proj3, dy, ln_g, ln_b)


def _conv_bwd(proj3, dc, conv_w32, dproj3, name, ts=256, cw=256):
    _, S, E = proj3.shape
    ts = _tile(S, ts, HALO)
    cw = _tile(E, cw)
    per = ts // HALO
    n_i = S // ts
    last_halo = S // HALO - 1
    rc = min(CONV_ROWS, ts)

    def body(a_ref, b_ref, dc_ref, dcn_ref, w_ref, dp_in, dab_ref, dw_ref, dcbuf, ubuf, dwacc):
        del dp_in
        i = pl.program_id(1)
        dcbuf[0:ts, :] = dc_ref[...]
        dcbuf[ts:ts + HALO, :] = jnp.where(i < n_i - 1, dcn_ref[...], 0.0)
        av = a_ref[...].astype(F32)
        sb = _sigmoid(b_ref[...].astype(F32))
        ubuf[...] = av * sb

        @pl.when(i == 0)
        def _():
            dwacc[...] = jnp.zeros_like(dwacc)

        for r0 in range(0, ts, rc):
            uv = ubuf[r0:r0 + rc, :]
            du = jnp.zeros((rc, cw), F32)
            for d in range(CONV_TAPS):
                k = CONV_TAPS - 1 - d
                win = dcbuf[r0 + d:r0 + d + rc, :]
                du = du + win * w_ref[k:k + 1, :]
                dwacc[k:k + 1, :] += jnp.sum(uv * win, axis=0, keepdims=True)
            a_c = a_ref[r0:r0 + rc, :].astype(F32)
            s_c = _sigmoid(b_ref[r0:r0 + rc, :].astype(F32))
            dab_ref[0, r0:r0 + rc, :] = (du * s_c).astype(BF16)
            dab_ref[1, r0:r0 + rc, :] = (du * a_c * s_c * (1.0 - s_c)).astype(BF16)

        @pl.when(i == n_i - 1)
        def _():
            dw_ref[...] = dwacc[...]

    return pl.pallas_call(
        body, name=name, grid=(E // cw, n_i),
        in_specs=[
            pl.BlockSpec((None, ts, cw), lambda j, i: (0, i, j)),
            pl.BlockSpec((None, ts, cw), lambda j, i: (1, i, j)),
            pl.BlockSpec((ts, cw), lambda j, i: (i, j)),
            pl.BlockSpec((HALO, cw), lambda j, i: (jnp.minimum((i + 1) * per, last_halo), j)),
            pl.BlockSpec((HALO, cw), lambda j, i: (0, j)),
            ANY,
        ],
        out_specs=[pl.BlockSpec((2, ts, cw), lambda j, i: (0, i, j)),
                   pl.BlockSpec((HALO, cw), lambda j, i: (0, j))],
        out_shape=[jax.ShapeDtypeStruct((3, S, E), BF16), jax.ShapeDtypeStruct((HALO, E), F32)],
        scratch_shapes=[pltpu.VMEM((ts + HALO, cw), F32), pltpu.VMEM((ts, cw), F32), pltpu.VMEM((HALO, cw), F32)],
        input_output_aliases={5: 0},
        compiler_params=_params("parallel", "arbitrary"),
    )(proj3, proj3, dc, dc, conv_w32, dproj3)


def _bucket_table(dil):
    delta = (np.arange(BLOCK)[:, None] + BLOCK) - np.arange(2 * BLOCK)[None, :]
    dist = np.clip(delta, 0, None) * dil
    large = MAX_EXACT + (np.log(np.maximum(dist, 1).astype(np.float32) / MAX_EXACT)
                         / np.log(MAX_DISTANCE / MAX_EXACT) * (N_BUCKETS - MAX_EXACT)).astype(np.int32)
    large = np.minimum(large, N_BUCKETS - 1)
    return np.where(dist < MAX_EXACT, dist, large).astype(np.int32).reshape(-1)


def _onehot(dil):
    tbl = jnp.asarray(_bucket_table(dil))
    return (tbl[None, :] == jnp.arange(LANES, dtype=jnp.int32)[:, None]).astype(BF16)


def _split3(v):
    hi = v.astype(BF16)
    r1 = v - hi.astype(F32)
    mid = r1.astype(BF16)
    lo = (r1 - mid.astype(F32)).astype(BF16)
    return hi, mid, lo


def _bias_table(rb_t, onehot, name):
    H = rb_t.shape[0]
    N = onehot.shape[1]

    def body(r_ref, oh_ref, o_ref):
        oh = oh_ref[...]
        hi, mid, lo = _split3(r_ref[...])
        o_ref[...] = (_dot(lo, oh, NN) + _dot(mid, oh, NN)) + _dot(hi, oh, NN)

    return pl.pallas_call(
        body, name=name, in_specs=[VMEM_SPEC, VMEM_SPEC], out_specs=VMEM_SPEC,
        out_shape=jax.ShapeDtypeStruct((H, N), F32),
        compiler_params=pltpu.CompilerParams(vmem_limit_bytes=VMEM_LIMIT),
    )(rb_t, onehot)


def _bias_grad(dbs, onehots, name):
    H = dbs[0].shape[0]
    n = len(dbs)

    def body(*refs):
        acc = jnp.zeros((H, LANES), F32)
        for q in range(n):
            oh = refs[n + q][...]
            hi, mid, lo = _split3(refs[q][...])
            acc = acc + ((_dot(lo, oh, NT) + _dot(mid, oh, NT)) + _dot(hi, oh, NT))
        refs[2 * n][...] = acc

    return pl.pallas_call(
        body, name=name, in_specs=[VMEM_SPEC] * (2 * n), out_specs=VMEM_SPEC,
        out_shape=jax.ShapeDtypeStruct((H, LANES), F32),
        compiler_params=pltpu.CompilerParams(vmem_limit_bytes=VMEM_LIMIT),
    )(*dbs, *onehots)


def _attn_fwd(pb4, kv6, bias, g, dil, name):
    _, S, A = pb4.shape
    H = A // HEAD_DIM
    L = S // dil
    nb = L // BLOCK
    qv = pb4.reshape(4, L, dil * A)
    kvv = kv6.reshape(6, L, dil * A)

    def body(q_ref, kp_ref, kc_ref, vp_ref, vc_ref, b_ref, o_ref, lse_ref):
        i = pl.program_id(1)
        qi = lax.broadcasted_iota(jnp.int32, (BLOCK, BLOCK), 0)
        ki = lax.broadcasted_iota(jnp.int32, (BLOCK, BLOCK), 1)
        mask_c = ki <= qi
        mask_p = jnp.logical_and(ki >= qi, i > 0)
        lane = lax.broadcasted_iota(jnp.int32, (BLOCK, LANES), 1)
        lse_acc = jnp.zeros((BLOCK, LANES), F32)
        for h in range(H):
            sl = slice(h * HEAD_DIM, (h + 1) * HEAD_DIM)
            qh = q_ref[:, sl]
            s_c = jnp.where(mask_c, _dot(qh, kc_ref[:, sl], NT) * SCALE + b_ref[h, :, BLOCK:], NEG)
            s_p = jnp.where(mask_p, _dot(qh, kp_ref[:, sl], NT) * SCALE + b_ref[h, :, :BLOCK], NEG)
            m = jnp.maximum(jnp.max(s_c, axis=-1, keepdims=True), jnp.max(s_p, axis=-1, keepdims=True))
            p_c = jnp.exp(s_c - m)
            p_p = jnp.exp(s_p - m)
            den = jnp.sum(p_c, axis=-1, keepdims=True) + jnp.sum(p_p, axis=-1, keepdims=True)
            acc = _dot(p_c.astype(BF16), vc_ref[:, sl], NN) + _dot(p_p.astype(BF16), vp_ref[:, sl], NN)
            o_ref[:, sl] = acc / den
            lse_acc = jnp.where(lane == h, m + jnp.log(den), lse_acc)
        lse_ref[...] = lse_acc

    def blk(slab, prev):
        if prev:
            return pl.BlockSpec((None, BLOCK, A), lambda r, i: (slab, jnp.maximum(i - 1, 0), r))
        return pl.BlockSpec((None, BLOCK, A), lambda r, i: (slab, i, r))

    o, lse = pl.pallas_call(
        body, name=name, grid=(dil, nb),
        in_specs=[blk(g, False), blk(2 * g, True), blk(2 * g, False), blk(2 * g + 1, True), blk(2 * g + 1, False),
                  pl.BlockSpec((H, BLOCK, 2 * BLOCK), lambda r, i: (0, 0, 0))],
        out_specs=[pl.BlockSpec((BLOCK, A), lambda r, i: (i, r)),
                   pl.BlockSpec((BLOCK, LANES), lambda r, i: (i, r))],
        out_shape=[jax.ShapeDtypeStruct((L, dil * A), F32), jax.ShapeDtypeStruct((L, dil * LANES), F32)],
        compiler_params=_params("parallel", "parallel"),
    )(qv, kvv, kvv, kvv, kvv, bias)
    return o.reshape(S, A), lse.reshape(S, LANES)


def _attn_merge(os_, lses, pb4, name, ts=256):
    S, A = os_[0].shape
    H = A // HEAD_DIM
    ts = _tile(S, ts, 16)
    n = len(os_)

    def body(*refs):
        o_refs = refs[:n]
        l_refs = refs[n:2 * n]
        z_ref = refs[2 * n]
        y_ref, om_ref, lse_ref = refs[2 * n + 1:2 * n + 4]
        ls = [r[...] for r in l_refs]
        m = ls[0]
        for q in range(1, n):
            m = jnp.maximum(m, ls[q])
        es = [jnp.exp(v - m) for v in ls]
        den = es[0]
        for q in range(1, n):
            den = den + es[q]
        alphas = [e / den for e in es]
        lse_ref[...] = m + jnp.log(den)
        lane = lax.broadcasted_iota(jnp.int32, (ts, LANES), 1)
        for h in range(H):
            sl = slice(h * HEAD_DIM, (h + 1) * HEAD_DIM)
            om = _lane_col(alphas[0], h, lane) * o_refs[0][:, sl]
            for q in range(1, n):
                om = om + _lane_col(alphas[q], h, lane) * o_refs[q][:, sl]
            z = z_ref[:, sl].astype(F32)
            y_ref[:, sl] = (om * (z * _sigmoid(z))).astype(BF16)
            om_ref[:, sl] = om.astype(BF16)

    row = pl.BlockSpec((ts, A), lambda i: (i, 0))
    lrow = pl.BlockSpec((ts, LANES), lambda i: (i, 0))
    return pl.pallas_call(
        body, name=name, grid=(S // ts,),
        in_specs=[row] * n + [lrow] * n + [pl.BlockSpec((None, ts, A), lambda i: (3, i, 0))],
        out_specs=[row, row, lrow],
        out_shape=[jax.ShapeDtypeStruct((S, A), BF16), jax.ShapeDtypeStruct((S, A), BF16),
                   jax.ShapeDtypeStruct((S, LANES), F32)],
        compiler_params=_params("parallel"),
    )(*os_, *lses, pb4)


def _gate_bwd(dy, om, pb4, name, ts=256):
    S, A = dy.shape
    H = A // HEAD_DIM
    ts = _tile(S, ts, 16)

    def body(dy_ref, om_ref, z_ref, do_ref, dh_ref, dz_ref):
        lane = lax.broadcasted_iota(jnp.int32, (ts, LANES), 1)
        acc = jnp.zeros((ts, LANES), F32)
        for h in range(H):
            sl = slice(h * HEAD_DIM, (h + 1) * HEAD_DIM)
            dyv = dy_ref[:, sl].astype(F32)
            omv = om_ref[:, sl].astype(F32)
            z = z_ref[:, sl].astype(F32)
            sz = _sigmoid(z)
            dob = (dyv * (z * sz)).astype(BF16)
            do_ref[:, sl] = dob
            dz_ref[:, sl] = (dyv * omv * (sz * (1.0 + z * (1.0 - sz)))).astype(BF16)
            acc = jnp.where(lane == h, jnp.sum(dob.astype(F32) * omv, axis=-1, keepdims=True), acc)
        dh_ref[...] = acc

    row = pl.BlockSpec((ts, A), lambda i: (i, 0))
    lrow = pl.BlockSpec((ts, LANES), lambda i: (i, 0))
    slab3 = pl.BlockSpec((None, ts, A), lambda i: (3, i, 0))
    return pl.pallas_call(
        body, name=name, grid=(S // ts,), in_specs=[row, row, slab3], out_specs=[row, lrow, slab3],
        out_shape=[jax.ShapeDtypeStruct((S, A), BF16), jax.ShapeDtypeStruct((S, LANES), F32),
                   jax.ShapeDtypeStruct((4, S, A), BF16)],
        compiler_params=_params("parallel"),
    )(dy, om, pb4)


def _attn_bwd(pb4, kv6, do, lse, dh, bias, dpb4, dkv6, g, dil, name):
    _, S, A = pb4.shape
    H = A // HEAD_DIM
    L = S // dil
    nb = L // BLOCK
    qv = pb4.reshape(4, L, dil * A)
    kvv = kv6.reshape(6, L, dil * A)
    dov = do.reshape(L, dil * A)
    lsev = lse.reshape(L, dil * LANES)
    dhv = dh.reshape(L, dil * LANES)
    dpv = dpb4.reshape(4, L, dil * A)
    has_dkv = dkv6 is not None
    n_in = 14 + (1 if has_dkv else 0)

    def body(*refs):
        (q_ref, qn_ref, kp_ref, kc_ref, vp_ref, vc_ref, do_ref, don_ref, l_ref, ln_ref, d_ref, dn_ref,
         b_ref) = refs[:13]
        dq_ref, dkv_ref, db_ref = refs[n_in:n_in + 3]
        r = pl.program_id(0)
        i = pl.program_id(1)
        qi = lax.broadcasted_iota(jnp.int32, (BLOCK, BLOCK), 0)
        ki = lax.broadcasted_iota(jnp.int32, (BLOCK, BLOCK), 1)
        mask_c = ki <= qi
        band = ki >= qi
        mask_p = jnp.logical_and(band, i > 0)
        mask_n = jnp.logical_and(band, i < nb - 1)
        lane = lax.broadcasted_iota(jnp.int32, (BLOCK, LANES), 1)

        @pl.when(jnp.logical_and(r == 0, i == 0))
        def _():
            db_ref[...] = jnp.zeros_like(db_ref)

        for h in range(H):
            sl = slice(h * HEAD_DIM, (h + 1) * HEAD_DIM)
            q_i, q_n = q_ref[:, sl], qn_ref[:, sl]
            k_p, k_c = kp_ref[:, sl], kc_ref[:, sl]
            v_p, v_c = vp_ref[:, sl], vc_ref[:, sl]
            do_i, do_n = do_ref[:, sl], don_ref[:, sl]
            l_i, l_n = _lane_col(l_ref[...], h, lane), _lane_col(ln_ref[...], h, lane)
            d_i, d_n = _lane_col(d_ref[...], h, lane), _lane_col(dn_ref[...], h, lane)
            b_c = b_ref[h, :, BLOCK:]
            b_p = b_ref[h, :, :BLOCK]
            s = jnp.where(mask_c, _dot(q_i, k_c, NT) * SCALE + b_c, NEG)
            p1 = jnp.exp(s - l_i)
            ds1 = p1 * (_dot(do_i, v_c, NT) - d_i)
            ds1b = ds1.astype(BF16)
            p1b = p1.astype(BF16)
            s = jnp.where(mask_p, _dot(q_i, k_p, NT) * SCALE + b_p, NEG)
            p2 = jnp.exp(s - l_i)
            ds2 = p2 * (_dot(do_i, v_p, NT) - d_i)
            ds2b = ds2.astype(BF16)
            s = jnp.where(mask_n, _dot(q_n, k_c, NT) * SCALE + b_p, NEG)
            p3 = jnp.exp(s - l_n)
            ds3b = (p3 * (_dot(do_n, v_c, NT) - d_n)).astype(BF16)
            p3b = p3.astype(BF16)
            dq = _dot(ds1b, k_c, NN) + _dot(ds2b, k_p, NN)
            dk = _dot(ds1b, q_i, TN) + _dot(ds3b, q_n, TN)
            dv = _dot(p1b, do_i, TN) + _dot(p3b, do_n, TN)
            dq_ref[:, sl] = (dq * SCALE).astype(BF16)
            dkv_ref[0, :, sl] = (dk * SCALE).astype(BF16)
            dkv_ref[1, :, sl] = dv.astype(BF16)
            db_ref[h, :, BLOCK:] += ds1
            db_ref[h, :, :BLOCK] += ds2

    def blk(slab, shift):
        if shift < 0:
            return pl.BlockSpec((None, BLOCK, A), lambda r, i: (slab, jnp.maximum(i - 1, 0), r))
        if shift > 0:
            return pl.BlockSpec((None, BLOCK, A), lambda r, i: (slab, jnp.minimum(i + 1, nb - 1), r))
        return pl.BlockSpec((None, BLOCK, A), lambda r, i: (slab, i, r))

    def row(width, shift):
        if shift > 0:
            return pl.BlockSpec((BLOCK, width), lambda r, i: (jnp.minimum(i + 1, nb - 1), r))
        return pl.BlockSpec((BLOCK, width), lambda r, i: (i, r))

    in_specs = [blk(g, 0), blk(g, 1), blk(2 * g, -1), blk(2 * g, 0), blk(2 * g + 1, -1), blk(2 * g + 1, 0),
                row(A, 0), row(A, 1), row(LANES, 0), row(LANES, 1), row(LANES, 0), row(LANES, 1),
                pl.BlockSpec((H, BLOCK, 2 * BLOCK), lambda r, i: (0, 0, 0)), ANY]
    operands = [qv, qv, kvv, kvv, kvv, kvv, dov, dov, lsev, lsev, dhv, dhv, bias, dpv]
    aliases = {13: 0}
    if has_dkv:
        in_specs.append(ANY)
        operands.append(dkv6.reshape(6, L, dil * A))
        aliases[14] = 1
    dpv, dkvv, db = pl.pallas_call(
        body, name=name, grid=(dil, nb), in_specs=in_specs,
        out_specs=[pl.BlockSpec((None, BLOCK, A), lambda r, i: (g, i, r)),
                   pl.BlockSpec((2, BLOCK, A), lambda r, i: (g, i, r)),
                   pl.BlockSpec((H, BLOCK, 2 * BLOCK), lambda r, i: (0, 0, 0))],
        out_shape=[jax.ShapeDtypeStruct((4, L, dil * A), BF16), jax.ShapeDtypeStruct((6, L, dil * A), BF16),
                   jax.ShapeDtypeStruct((H, BLOCK, 2 * BLOCK), F32)],
        input_output_aliases=aliases,
        compiler_params=_params("arbitrary", "arbitrary"),
    )(*operands)
    return dpv.reshape(4, S, A), dkvv.reshape(6, S, A), db


def _sum_leading(stack, out_dtype, name, tr=256, tc=2048):
    n, R, C = stack.shape
    tr = _tile(R, tr, 16)
    tc = _tile(C, tc)

    def body(s_ref, o_ref):
        acc = s_ref[0].astype(F32)
        for q in range(1, n):
            acc = acc + s_ref[q].astype(F32)
        o_ref[...] = acc.astype(out_dtype)

    return pl.pallas_call(
        body, name=name, grid=(R // tr, C // tc),
        in_specs=[pl.BlockSpec((n, tr, tc), lambda i, j: (0, i, j))],
        out_specs=pl.BlockSpec((tr, tc), lambda i, j: (i, j)),
        out_shape=jax.ShapeDtypeStruct((R, C), out_dtype),
        compiler_params=_params("parallel", "parallel"),
    )(stack)


def _add_half(g, t, c_idx, kind, name, tr=256, tc=2048):
    R, C = t.shape
    tr = _tile(R, tr, 16)
    tc = _tile(C, tc)
    nrb, ncb = R // tr, C // tc

    def body(c_ref, g_ref, t_ref, o_ref):
        del c_ref
        o_ref[...] = (g_ref[...].astype(F32) + t_ref[...].astype(F32)).astype(BF16)

    if kind == "col":
        g_map = lambda i, j, c_ref: (c_ref[0] * nrb + i, j)
    else:
        g_map = lambda i, j, c_ref: (i, c_ref[0] * ncb + j)
    same = lambda i, j, c_ref: (i, j)
    return pl.pallas_call(
        body, name=name,
        grid_spec=pltpu.PrefetchScalarGridSpec(
            num_scalar_prefetch=1, grid=(nrb, ncb),
            in_specs=[pl.BlockSpec((tr, tc), g_map), pl.BlockSpec((tr, tc), same)],
            out_specs=pl.BlockSpec((tr, tc), same)),
        out_shape=jax.ShapeDtypeStruct((R, C), BF16),
        compiler_params=_params("parallel", "parallel"),
    )(c_idx, g, t)


def _cast_into_full(w, kind, chip_idx, name, tr=256, tc=2048):
    R, C = w.shape
    tr = _tile(R, tr, 16)
    tc = _tile(C, tc)
    nrb, ncb = R // tr, C // tc

    def body(k_ref, w_ref, o_ref):
        del k_ref
        o_ref[...] = w_ref[...].astype(BF16)

    if kind == "col":
        o_map = lambda i, j, k_ref: (i, k_ref[0] * ncb + j)
        full = (R, N_CHIPS * C)
    else:
        o_map = lambda i, j, k_ref: (k_ref[0] * nrb + i, j)
        full = (N_CHIPS * R, C)
    return pl.pallas_call(
        body, name=name,
        grid_spec=pltpu.PrefetchScalarGridSpec(
            num_scalar_prefetch=1, grid=(nrb, ncb),
            in_specs=[pl.BlockSpec((tr, tc), lambda i, j, k_ref: (i, j))],
            out_specs=pl.BlockSpec((tr, tc), o_map)),
        out_shape=jax.ShapeDtypeStruct(full, BF16),
        compiler_params=_params("parallel", "parallel"),
    )(chip_idx, w)


def _sum_into_shard(p, u, idx, kind, name, tr=256, tc=2048):
    _, R, C = u.shape
    tr = _tile(R, tr, 16)
    tc = _tile(C, tc)
    nrb, ncb = R // tr, C // tc

    def body(i_ref, p_ref, u_ref, o_ref):
        del i_ref
        acc = p_ref[...].astype(F32)
        for q in range(N_CHIPS - 1):
            acc = acc + u_ref[q].astype(F32)
        o_ref[...] = acc

    if kind == "col":
        p_map = lambda i, j, r: (i, r[0] * ncb + j)
        o_map = lambda i, j, r: (r[1] * nrb + i, j)
        full = (2 * R, C)
    else:
        p_map = lambda i, j, r: (r[0] * nrb + i, j)
        o_map = lambda i, j, r: (i, r[1] * ncb + j)
        full = (R, 2 * C)
    return pl.pallas_call(
        body, name=name,
        grid_spec=pltpu.PrefetchScalarGridSpec(
            num_scalar_prefetch=1, grid=(nrb, ncb),
            in_specs=[pl.BlockSpec((tr, tc), p_map), pl.BlockSpec((N_CHIPS - 1, tr, tc), lambda i, j, r: (0, i, j))],
            out_specs=pl.BlockSpec((tr, tc), o_map)),
        out_shape=jax.ShapeDtypeStruct(full, F32),
        compiler_params=_params("parallel", "parallel"),
    )(idx, p, u)


def _adamw(w, g, m, v, name, tr=256, tc=2048):
    R, C = w.shape
    tr = _tile(R, tr, 8)
    tc = _tile(C, tc)
    c1 = 1.0 - ADAM_B1 ** ADAM_STEP
    c2 = 1.0 - ADAM_B2 ** ADAM_STEP

    def body(w_ref, g_ref, m_ref, v_ref, d_ref, nm_ref, nv_ref):
        gv = g_ref[...]
        nm = ADAM_B1 * m_ref[...] + (1.0 - ADAM_B1) * gv
        nv = ADAM_B2 * v_ref[...] + (1.0 - ADAM_B2) * (gv * gv)
        d_ref[...] = -ADAM_LR * ((nm / c1) / (jnp.sqrt(nv / c2) + ADAM_EPS) + ADAM_WD * w_ref[...])
        nm_ref[...] = nm
        nv_ref[...] = nv

    blk = pl.BlockSpec((tr, tc), lambda i, j: (i, j))
    sh = jax.ShapeDtypeStruct((R, C), F32)
    return pl.pallas_call(
        body, name=name, grid=(R // tr, C // tc), in_specs=[blk] * 4, out_specs=[blk] * 3,
        out_shape=[sh, sh, sh], compiler_params=_params("parallel", "parallel"),
    )(w, g, m, v)


def _xyc():
    return lax.axis_index("x"), lax.axis_index("y"), lax.axis_index("c")


def _drain(copies):
    for cp in copies:
        if cp.is_remote:
            cp.wait_send()
        else:
            cp.wait()


def _other_chips(x, y):
    return [(1 - x, y), (x, 1 - y), (1 - x, 1 - y)]


def _allgather_small(blk, name):
    R, C = blk.shape

    def body(x_ref, out_ref, send_sems, recv_sems, local_sem):
        x, y, c = _xyc()
        me = 4 * x + 2 * y + c
        mine = pltpu.make_async_copy(x_ref, out_ref.at[me], local_sem)
        mine.start()
        peers = []
        for k in range(1, N_DEV):
            px = 1 - x if (k >> 2) & 1 else x
            py = 1 - y if (k >> 1) & 1 else y
            pc = 1 - c if k & 1 else c
            peers.append((px, py, pc))
        sends = []
        for k, peer in enumerate(peers):
            cp = pltpu.make_async_remote_copy(
                src_ref=x_ref, dst_ref=out_ref.at[me], send_sem=send_sems.at[k], recv_sem=recv_sems.at[k],
                device_id=peer, device_id_type=MESH)
            cp.start()
            sends.append(cp)
        for k, (px, py, pc) in enumerate(peers):
            pltpu.make_async_remote_copy(
                src_ref=x_ref, dst_ref=out_ref.at[4 * px + 2 * py + pc], send_sem=send_sems.at[k],
                recv_sem=recv_sems.at[k], device_id=(px, py, pc), device_id_type=MESH).wait_recv()
        for cp in sends:
            cp.wait_send()
        mine.wait()

    return pl.pallas_call(
        body, name=name, in_specs=[VMEM_SPEC], out_specs=VMEM_SPEC,
        out_shape=jax.ShapeDtypeStruct((N_DEV, R, C), blk.dtype),
        scratch_shapes=[pltpu.SemaphoreType.DMA((N_DEV - 1,)), pltpu.SemaphoreType.DMA((N_DEV - 1,)),
                        pltpu.SemaphoreType.DMA],
        compiler_params=pltpu.CompilerParams(vmem_limit_bytes=VMEM_LIMIT),
    )(blk)


def _full_region(ref, kind, chip, half, shard_shape):
    r, cn = shard_shape
    hr = r // 2
    if kind == "col":
        rows = pl.ds(0, r) if half is None else pl.ds(pl.multiple_of(half * hr, 16), hr)
        return ref.at[rows, pl.ds(pl.multiple_of(chip * cn, LANES), cn)]
    if half is None:
        return ref.at[pl.ds(pl.multiple_of(chip * r, 16), r), :]
    return ref.at[pl.ds(pl.multiple_of(chip * r + half * hr, 16), hr), :]


def _allgather_weights(fulls, kinds, shapes, name):
    n = len(fulls)

    def body(*refs):
        outs = refs[n:2 * n]
        send_sems, recv_sems = refs[2 * n:]
        x, y, c = _xyc()
        chip = 2 * x + y
        sib = (x, y, 1 - c)
        others = _other_chips(x, y)
        started = []
        for w in range(n):
            mine = _full_region(outs[w], kinds[w], chip, c, shapes[w])
            for j, (ox, oy) in enumerate(others):
                cp = pltpu.make_async_remote_copy(
                    src_ref=mine, dst_ref=mine, send_sem=send_sems.at[6 * w + j], recv_sem=recv_sems.at[6 * w + j],
                    device_id=(ox, oy, c), device_id_type=MESH)
                cp.start()
                started.append(cp)
        for w in range(n):
            for j, (ox, oy) in enumerate(others):
                landed = _full_region(outs[w], kinds[w], 2 * ox + oy, c, shapes[w])
                pltpu.make_async_remote_copy(
                    src_ref=landed, dst_ref=landed, send_sem=send_sems.at[6 * w + j], recv_sem=recv_sems.at[6 * w + j],
                    device_id=(ox, oy, c), device_id_type=MESH).wait_recv()
                cp = pltpu.make_async_remote_copy(
                    src_ref=landed, dst_ref=landed, send_sem=send_sems.at[6 * w + 3 + j],
                    recv_sem=recv_sems.at[6 * w + 3 + j], device_id=sib, device_id_type=MESH)
                cp.start()
                started.append(cp)
        for w in range(n):
            for j, (ox, oy) in enumerate(others):
                theirs = _full_region(outs[w], kinds[w], 2 * ox + oy, 1 - c, shapes[w])
                pltpu.make_async_remote_copy(
                    src_ref=theirs, dst_ref=theirs, send_sem=send_sems.at[6 * w + 3 + j],
                    recv_sem=recv_sems.at[6 * w + 3 + j], device_id=sib, device_id_type=MESH).wait_recv()
        _drain(started)

    return pl.pallas_call(
        body, name=name, in_specs=[ANY] * n, out_specs=[ANY] * n,
        out_shape=[jax.ShapeDtypeStruct(f.shape, f.dtype) for f in fulls],
        input_output_aliases={w: w for w in range(n)},
        scratch_shapes=[pltpu.SemaphoreType.DMA((6 * n,)), pltpu.SemaphoreType.DMA((6 * n,))],
    )(*fulls)


def _half_of(ref, kind, half):
    r, cn = ref.shape
    if kind == "col":
        return ref.at[pl.ds(pl.multiple_of(half * (r // 2), 16), r // 2), :]
    return ref.at[:, pl.ds(pl.multiple_of(half * (cn // 2), LANES), cn // 2)]


def _shard_of(ref, kind, chip):
    r, cn = ref.shape
    if kind == "col":
        return ref.at[:, pl.ds(pl.multiple_of(chip * (cn // N_CHIPS), LANES), cn // N_CHIPS)]
    return ref.at[pl.ds(pl.multiple_of(chip * (r // N_CHIPS), 16), r // N_CHIPS), :]


def _exchange_halves(grads, kinds, name):
    n = len(grads)

    def body(*refs):
        gs = refs[:n]
        ts = refs[n:2 * n]
        send_sems, recv_sems = refs[2 * n:]
        x, y, c = _xyc()
        cps = []
        for w in range(n):
            cp = pltpu.make_async_remote_copy(
                src_ref=_half_of(gs[w], kinds[w], 1 - c), dst_ref=ts[w], send_sem=send_sems.at[w],
                recv_sem=recv_sems.at[w], device_id=(x, y, 1 - c), device_id_type=MESH)
            cp.start()
            cps.append(cp)
        for cp in cps:
            cp.wait()

    out_shape = []
    for gr, kind in zip(grads, kinds):
        r, cn = gr.shape
        out_shape.append(jax.ShapeDtypeStruct((r // 2, cn) if kind == "col" else (r, cn // 2), gr.dtype))
    return pl.pallas_call(
        body, name=name, in_specs=[ANY] * n, out_specs=[ANY] * n, out_shape=out_shape,
        scratch_shapes=[pltpu.SemaphoreType.DMA((n,)), pltpu.SemaphoreType.DMA((n,))],
    )(*grads)


def _scatter_partials(parts, kinds, name):
    n = len(parts)

    def body(*refs):
        ps = refs[:n]
        us = refs[n:2 * n]
        send_sems, recv_sems = refs[2 * n:]
        x, y, c = _xyc()
        others = _other_chips(x, y)
        cps = []
        for w in range(n):
            for j, (ox, oy) in enumerate(others):
                cp = pltpu.make_async_remote_copy(
                    src_ref=_shard_of(ps[w], kinds[w], 2 * ox + oy), dst_ref=us[w].at[j],
                    send_sem=send_sems.at[3 * w + j], recv_sem=recv_sems.at[3 * w + j],
                    device_id=(ox, oy, c), device_id_type=MESH)
                cp.start()
                cps.append(cp)
        for cp in cps:
            cp.wait()

    out_shape = []
    for p, kind in zip(parts, kinds):
        r, cn = p.shape
        hs = (r, cn // N_CHIPS) if kind == "col" else (r // N_CHIPS, cn)
        out_shape.append(jax.ShapeDtypeStruct((N_CHIPS - 1,) + hs, p.dtype))
    return pl.pallas_call(
        body, name=name, in_specs=[ANY] * n, out_specs=[ANY] * n, out_shape=out_shape,
        scratch_shapes=[pltpu.SemaphoreType.DMA((3 * n,)), pltpu.SemaphoreType.DMA((3 * n,))],
    )(*parts)


def _join_halves(halves, kinds, name):
    n = len(halves)

    def body(*refs):
        outs = refs[n:2 * n]
        send_sems, recv_sems = refs[2 * n:]
        x, y, c = _xyc()
        cps = []
        for w in range(n):
            mine = _half_of(outs[w], kinds[w], c)
            cp = pltpu.make_async_remote_copy(
                src_ref=mine, dst_ref=mine, send_sem=send_sems.at[w], recv_sem=recv_sems.at[w],
                device_id=(x, y, 1 - c), device_id_type=MESH)
            cp.start()
            cps.append(cp)
        for w in range(n):
            theirs = _half_of(outs[w], kinds[w], 1 - c)
            pltpu.make_async_remote_copy(
                src_ref=theirs, dst_ref=theirs, send_sem=send_sems.at[w], recv_sem=recv_sems.at[w],
                device_id=(x, y, 1 - c), device_id_type=MESH).wait_recv()
        _drain(cps)

    return pl.pallas_call(
        body, name=name, in_specs=[ANY] * n, out_specs=[ANY] * n,
        out_shape=[jax.ShapeDtypeStruct(h.shape, h.dtype) for h in halves],
        input_output_aliases={w: w for w in range(n)},
        scratch_shapes=[pltpu.SemaphoreType.DMA((n,)), pltpu.SemaphoreType.DMA((n,))],
    )(*halves)


def kernel(x, a_norm, a_w_in, a_conv_w, a_conv_b, a_ln_g, a_ln_b, a_w_out, kv_norm, w_kv, b_norm, b_w_in, b_w_out, rel_bias, final_norm, loss_target, m_a_norm, m_a_w_in, m_a_conv_w, m_a_conv_b, m_a_ln_g, m_a_ln_b, m_a_w_out, m_kv_norm, m_w_kv, m_b_norm, m_b_w_in, m_b_w_out, m_rel_bias, m_final_norm, v_a_norm, v_a_w_in, v_a_conv_w, v_a_conv_b, v_a_ln_g, v_a_ln_b, v_a_w_out, v_kv_norm, v_w_kv, v_b_norm, v_b_w_in, v_b_w_out, v_rel_bias, v_final_norm):
    S, D = x.shape[1], x.shape[2]
    E = a_w_out.shape[1] * N_CHIPS
    A = b_w_out.shape[1] * N_CHIPS
    H = A // HEAD_DIM
    DC = D // N_CHIPS
    xs = x.reshape(S, D)
    tgt = loss_target.reshape(S, D)
    cx, cy, cc = _xyc()
    chip = 2 * cx + cy
    c_idx = jnp.reshape(cc, (1,)).astype(jnp.int32)

    big_names = ["a_w_in", "a_w_out", "w_kv", "b_w_in", "b_w_out"]
    kinds = ["col", "row", "col", "col", "row"]
    big_w = [a_w_in[0], a_w_out[0], w_kv, b_w_in[0], b_w_out[0]]
    big_m = [m_a_w_in[0], m_a_w_out[0], m_w_kv, m_b_w_in[0], m_b_w_out[0]]
    big_v = [v_a_w_in[0], v_a_w_out[0], v_w_kv, v_b_w_in[0], v_b_w_out[0]]
    chip_idx = jnp.reshape(chip, (1,)).astype(jnp.int32)
    placed = [_cast_into_full(big_w[w], kinds[w], chip_idx, "cast_" + big_names[w]) for w in range(5)]
    wa_in, wa_out, wkv, wb_in, wb_out = _allgather_weights(placed, kinds, [w.shape for w in big_w], "ag_weights")

    def pack_sharded(an, cw, cb, lg, lb):
        return jnp.concatenate([an, cw[0], cb, lg, lb, jnp.zeros((5, DC), F32)], axis=0)

    small_w = pack_sharded(a_norm, a_conv_w, a_conv_b, a_ln_g, a_ln_b)
    gathered = _allgather_small(small_w, "ag_small_params")
    small_full = jnp.concatenate([gathered[2 * k] for k in range(N_CHIPS)], axis=1)
    g_a = small_full[0:1]
    conv_w32 = small_full[1:1 + HALO]
    conv_b = small_full[32:33]
    ln_g = small_full[33:34]
    ln_b = small_full[34:35]
    g_kv = kv_norm.reshape(1, D)
    g_b = b_norm.reshape(1, D)
    g_f = final_norm.reshape(1, D)

    rb_t = jnp.pad(rel_bias.T, ((0, 0), (0, LANES - N_BUCKETS)))
    onehots = [_onehot(dil) for _, dil in GROUPS]
    biases = [_bias_table(rb_t, onehots[g], "bias_table_%d" % g).reshape(H, BLOCK, 2 * BLOCK)
              for g in range(len(GROUPS))]

    (h0,) = _rms_fwd(xs, [g_a], "rms_a")
    proj3 = _matmul(h0, wa_in, "nn", BF16, "mm_a_in", out_slab=E)
    conv = _conv_fwd(proj3, conv_w32, conv_b, "conv_fwd")
    y_a = _ln_gate_fwd(conv, proj3, ln_g, ln_b, "ln_gate_fwd")
    x1 = _matmul(y_a, wa_out, "nn", F32, "mm_a_out", res=xs)
    hk, hb = _rms_fwd(x1, [g_kv, g_b], "rms_kv_b")
    kv6 = _matmul(hk, wkv, "nn", BF16, "mm_kv", out_slab=A)
    pb4 = _matmul(hb, wb_in, "nn", BF16, "mm_b_in", out_slab=A)
    os_, lses = [], []
    for g, (_, dil) in enumerate(GROUPS):
        o_g, lse_g = _attn_fwd(pb4, kv6, biases[g], g, dil, "attn_fwd_%d" % g)
        os_.append(o_g)
        lses.append(lse_g)
    y_b, o_m, lse = _attn_merge(os_, lses, pb4, "attn_merge")
    x2 = _matmul(y_b, wb_out, "nn", F32, "mm_b_out", res=x1)
    loss_part, dx2, dx2b, gg_f = _final_head(x2, g_f, tgt, "final_head")
    loss = lax.psum(loss_part[0, 0], ("x", "y", "c"))

    dy_b = _matmul(dx2b, wb_out, "nt", BF16, "mm_b_out_dx")
    dwb_out = _matmul(y_b, dx2b, "tn", BF16, "mm_b_out_dw", tm=512, tn=1024, tk=1024)
    do, dh, dpb4 = _gate_bwd(dy_b, o_m, pb4, "gate_bwd")
    dkv6 = None
    dbs = []
    for g, (_, dil) in enumerate(GROUPS):
        dpb4, dkv6, db = _attn_bwd(pb4, kv6, do, lse, dh, biases[g], dpb4, dkv6, g, dil, "attn_bwd_%d" % g)
        dbs.append(db.reshape(H, BLOCK * 2 * BLOCK))
    g_rel_t = _bias_grad(dbs, onehots, "bias_grad")
    dhb = _matmul(dpb4, wb_in, "nt", BF16, "mm_b_in_dx", a_slab=True)
    dwb_in = _matmul(hb, dpb4, "tn", BF16, "mm_b_in_dw", b_slab=True, tm=512, tn=1024, tk=1024)
    dhk = _matmul(dkv6, wkv, "nt", BF16, "mm_kv_dx", a_slab=True)
    dwkv = _matmul(hk, dkv6, "tn", BF16, "mm_kv_dw", b_slab=True, tm=512, tn=1024, tk=1024)
    dx1, dx1b, gg_kvb = _rms_bwd(x1, [dhk, dhb], [g_kv, g_b], dx2, "rms_kv_b_bwd")
    dy_a = _matmul(dx1b, wa_out, "nt", BF16, "mm_a_out_dx")
    dwa_out = _matmul(y_a, dx1b, "tn", BF16, "mm_a_out_dw", tm=512, tn=1024, tk=1024)
    dconv, dproj3, gg_ln = _ln_gate_bwd(conv, proj3, dy_a, ln_g, ln_b, "ln_gate_bwd")
    dproj3, g_conv_w = _conv_bwd(proj3, dconv, conv_w32, dproj3, "conv_bwd")
    dh0 = _matmul(dproj3, wa_in, "nt", BF16, "mm_a_in_dx", a_slab=True)
    dwa_in = _matmul(h0, dproj3, "tn", BF16, "mm_a_in_dw", b_slab=True, tm=512, tn=1024, tk=1024)
    grad_x, _, gg_a = _rms_bwd(xs, [dh0], [g_a], dx1, "rms_a_bwd")

    full_grads = [dwa_in, dwa_out, dwkv, dwb_in, dwb_out]
    theirs = _exchange_halves(full_grads, kinds, "rs_exchange_halves")
    parts = [_add_half(full_grads[w], theirs[w], c_idx, kinds[w], "rs_add_half_%d" % w) for w in range(5)]
    slots = _scatter_partials(parts, kinds, "rs_scatter_partials")
    chip_c = jnp.stack([chip, cc]).astype(jnp.int32)
    halves = [_sum_into_shard(parts[w], slots[w], chip_c, kinds[w], "rs_sum_chips_%d" % w) for w in range(5)]
    big_g = _join_halves(halves, kinds, "rs_join_halves")

    g_rel_row = jnp.pad(g_rel_t[:, :N_BUCKETS].T.reshape(1, N_BUCKETS * H), ((0, 0), (0, D - N_BUCKETS * H)))
    small_g = jnp.concatenate([
        gg_a[0:1], g_conv_w[0:CONV_TAPS], gg_ln[2:3], gg_ln[0:1], gg_ln[1:2],
        gg_kvb[0:1], gg_kvb[1:2], gg_f[0:1], g_rel_row, jnp.zeros((1, D), F32)], axis=0)
    small_sum = _sum_leading(_allgather_small(small_g, "ag_small_grads"), F32, "sum_small_grads", tr=40)
    g_sharded = lax.dynamic_slice(small_sum, (0, chip * DC), (40, DC))
    g_repl = jnp.concatenate([small_sum[35:39], jnp.zeros((4, D), F32)], axis=0)

    outs_g, outs_d, outs_m, outs_v = {}, {}, {}, {}
    for w, nm in enumerate(big_names):
        d_, m_, v_ = _adamw(big_w[w], big_g[w], big_m[w], big_v[w], "adamw_" + nm)
        outs_g[nm], outs_d[nm], outs_m[nm], outs_v[nm] = big_g[w], d_, m_, v_
    sm_m = pack_sharded(m_a_norm, m_a_conv_w, m_a_conv_b, m_a_ln_g, m_a_ln_b)
    sm_v = pack_sharded(v_a_norm, v_a_conv_w, v_a_conv_b, v_a_ln_g, v_a_ln_b)
    sd, smm, svv = _adamw(small_w, g_sharded, sm_m, sm_v, "adamw_small_sharded")

    def unpack_sharded(p):
        return {"a_norm": p[0:1], "a_conv_w": p[1:32].reshape(1, CONV_TAPS, DC), "a_conv_b": p[32:33],
                "a_ln_g": p[33:34], "a_ln_b": p[34:35]}

    for src, dst in ((g_sharded, outs_g), (sd, outs_d), (smm, outs_m), (svv, outs_v)):
        dst.update(unpack_sharded(src))

    def pack_repl(kn, bn, fn, rb):
        rbrow = jnp.pad(rb.reshape(1, N_BUCKETS * H), ((0, 0), (0, D - N_BUCKETS * H)))
        return jnp.concatenate([kn.reshape(1, D), bn.reshape(1, D), fn.reshape(1, D), rbrow, jnp.zeros((4, D), F32)], axis=0)

    rp_w = pack_repl(kv_norm, b_norm, final_norm, rel_bias)
    rp_m = pack_repl(m_kv_norm, m_b_norm, m_final_norm, m_rel_bias)
    rp_v = pack_repl(v_kv_norm, v_b_norm, v_final_norm, v_rel_bias)
    rd, rmm, rvv = _adamw(rp_w, g_repl, rp_m, rp_v, "adamw_small_replicated")

    def unpack_repl(p):
        return {"kv_norm": p[0], "b_norm": p[1:2], "final_norm": p[2],
                "rel_bias": p[3, :N_BUCKETS * H].reshape(N_BUCKETS, H)}

    for src, dst in ((g_repl, outs_g), (rd, outs_d), (rmm, outs_m), (rvv, outs_v)):
        dst.update(unpack_repl(src))

    order = ["a_norm", "a_w_in", "a_conv_w", "a_conv_b", "a_ln_g", "a_ln_b", "a_w_out", "kv_norm", "w_kv",
             "b_norm", "b_w_in", "b_w_out", "rel_bias", "final_norm"]
    lead = {"a_w_in", "a_w_out", "b_w_in", "b_w_out"}

    def shaped(nm, val):
        return val[None] if nm in lead else val

    result = [loss, grad_x.reshape(1, S, D)]
    for table in (outs_g, outs_d, outs_m, outs_v):
        result.extend(shaped(nm, table[nm]) for nm in order)
    return tuple(result)
```

```python
import functools

import numpy as np
import jax
import jax.numpy as jnp
from jax import lax
from jax.experimental import pallas as pl
from jax.experimental.pallas import tpu as pltpu

F32 = jnp.float32
BF16 = jnp.bfloat16
MESH = pl.DeviceIdType.MESH
ANY = pl.BlockSpec(memory_space=pl.ANY)
VMEM_SPEC = pl.BlockSpec(memory_space=pltpu.VMEM)

EPS = 1e-6
HEAD_DIM = 128
BLOCK = 128
GROUPS = ((128, 1), (512, 4), (2048, 16))
SCALE = HEAD_DIM ** -0.5
CONV_TAPS = 31
HALO = 32
N_BUCKETS = 32
MAX_EXACT = 16
MAX_DISTANCE = 2048
NEG = -1e30
N_CHIPS = 4
N_DEV = 8
LANES = 128
VMEM_LIMIT = 56 * 1024 * 1024

ADAM_LR = 0.001
ADAM_B1 = 0.9
ADAM_B2 = 0.999
ADAM_EPS = 1e-08
ADAM_WD = 0.01
ADAM_STEP = 10


def _tile(n, pref, mult=LANES):
    t = (min(pref, n) // mult) * mult
    while t >= mult:
        if n % t == 0:
            return t
        t -= mult
    return n


def _params(*sem):
    return pltpu.CompilerParams(dimension_semantics=sem, vmem_limit_bytes=VMEM_LIMIT)


def _sigmoid(v):
    return 1.0 / (1.0 + jnp.exp(-v))


def _dot(a, b, dims):
    return lax.dot_general(a, b, (dims, ((), ())), preferred_element_type=F32)


NN = ((1,), (0,))
NT = ((1,), (1,))
TN = ((0,), (0,))


def _stack_rows(rows, total):
    width = rows[0].shape[1]
    rid = lax.broadcasted_iota(jnp.int32, (total, width), 0)
    out = jnp.zeros((total, width), F32)
    for q, row in enumerate(rows):
        out = jnp.where(rid == q, jnp.broadcast_to(row, (total, width)), out)
    return out


def _lane_col(arr, h, lane):
    return jnp.sum(jnp.where(lane == h, arr, 0.0), axis=-1, keepdims=True)


def _matmul(a, b, mode, out_dtype, name, res=None, a_slab=False, b_slab=False, out_slab=0,
            b_off=0, n_cols=None, out_off=0, out_cols=None, out_alias=None, tm=512, tn=1024, tk=2048):
    if a_slab:
        na, M, W = a.shape
        K = na * W
    elif mode == "tn":
        K, M = a.shape
    else:
        M, K = a.shape
    if b_slab:
        nbs, _, Wb = b.shape
        N = nbs * Wb
    elif mode == "nt":
        N = b.shape[0]
    else:
        N = n_cols if n_cols else b.shape[1]
    tm = _tile(M, tm)
    tn = _tile(Wb if b_slab else (out_slab if out_slab else N), tn)
    tk = _tile(W if a_slab else K, tk)
    nk = K // tk
    grid = (M // tm, N // tn, nk)
    bo = b_off // (tk if mode == "nt" else tn)
    oo = out_off // tn

    if a_slab:
        per = W // tk
        a_spec = pl.BlockSpec((None, tm, tk), lambda i, j, k: (k // per, i, k % per))
    elif mode == "tn":
        a_spec = pl.BlockSpec((tk, tm), lambda i, j, k: (k, i))
    else:
        a_spec = pl.BlockSpec((tm, tk), lambda i, j, k: (i, k))
    if b_slab:
        perb = Wb // tn
        b_spec = pl.BlockSpec((None, tk, tn), lambda i, j, k: (j // perb, k, j % perb))
    elif mode == "nt":
        b_spec = pl.BlockSpec((tn, tk), lambda i, j, k: (j, k + bo))
    else:
        b_spec = pl.BlockSpec((tk, tn), lambda i, j, k: (k, j + bo))
    if out_slab:
        pero = out_slab // tn
        o_spec = pl.BlockSpec((None, tm, tn), lambda i, j, k: (j // pero, i, j % pero))
        out_shape = jax.ShapeDtypeStruct((N // out_slab, M, out_slab), out_dtype)
    else:
        o_spec = pl.BlockSpec((tm, tn), lambda i, j, k: (i, j + oo))
        out_shape = jax.ShapeDtypeStruct((M, out_cols if out_cols else N), out_dtype)
    in_specs = [a_spec, b_spec]
    operands = [a, b]
    if res is not None:
        in_specs.append(pl.BlockSpec((tm, tn), lambda i, j, k: (i, j)))
        operands.append(res)
    aliases = {}
    if out_alias is not None:
        aliases[len(operands)] = 0
        in_specs.append(ANY)
        operands.append(out_alias)
    dims = {"nn": NN, "nt": NT, "tn": TN}[mode]
    has_res = res is not None
    n_in = len(operands)

    def body(*refs):
        a_ref, b_ref = refs[0], refs[1]
        r_ref = refs[2] if has_res else None
        o_ref = refs[n_in]
        prod = _dot(a_ref[...], b_ref[...], dims)

        def finish(val):
            if has_res:
                val = val + r_ref[...]
            o_ref[...] = val.astype(out_dtype)

        if nk == 1:
            finish(prod)
        else:
            acc_ref = refs[n_in + 1]
            k = pl.program_id(2)

            @pl.when(k == 0)
            def _():
                acc_ref[...] = prod

            @pl.when(k > 0)
            def _():
                acc_ref[...] += prod

            @pl.when(k == nk - 1)
            def _():
                finish(acc_ref[...])

    scratch = [pltpu.VMEM((tm, tn), F32)] if nk > 1 else []
    return pl.pallas_call(
        body, name=name, grid=grid, in_specs=in_specs, out_specs=o_spec, out_shape=out_shape,
        scratch_shapes=scratch, input_output_aliases=aliases,
        compiler_params=_params("parallel", "parallel", "arbitrary"),
    )(*operands)


def _group_spec(d, ts, width):
    if d == 1:
        return pl.BlockSpec((ts, width), lambda i: (i, 0))
    return pl.BlockSpec((d, ts // d, width), lambda i: (0, i, 0))


def _group_shape(d, S, width, dtype):
    return jax.ShapeDtypeStruct((S, width) if d == 1 else (d, S // d, width), dtype)


def _chunk_buf(ts, width):
    return pltpu.VMEM((width // LANES, ts, LANES), F32)


def _fill_chunks(buf, val):
    for c in range(buf.shape[0]):
        buf[c] = val[:, c * LANES:(c + 1) * LANES]


def _read_chunks(buf):
    return jnp.concatenate([buf[c] for c in range(buf.shape[0])], axis=1)


def _emit_group_order(o_ref, buf, d, dtype):
    n = buf.shape[1] // d
    for r in range(d):
        for c in range(buf.shape[0]):
            o_ref[r, :, c * LANES:(c + 1) * LANES] = buf[c, pl.ds(r, n, stride=d), :].astype(dtype)


def _store_token_order(buf, i_ref, d):
    n = buf.shape[1] // d
    for r in range(d):
        for c in range(buf.shape[0]):
            buf[c, pl.ds(r, n, stride=d), :] = i_ref[r, :, c * LANES:(c + 1) * LANES].astype(F32)


def _rms_fwd(x, gains, dils, name, ts=256):
    S, D = x.shape
    ts = _tile(S, ts, 16 * max(dils))
    n = len(gains)
    nd = len(dils)

    def body(*refs):
        buf = refs[1 + n + n * nd]
        xv = refs[0][...]
        nrm = xv * lax.rsqrt(jnp.mean(xv * xv, axis=-1, keepdims=True) + EPS)
        for q in range(n):
            val = nrm * refs[1 + q][...]
            if max(dils) > 1:
                _fill_chunks(buf, val)
            for e, d in enumerate(dils):
                if d == 1:
                    refs[1 + n + q * nd + e][...] = val.astype(BF16)
                else:
                    _emit_group_order(refs[1 + n + q * nd + e], buf, d, BF16)

    row = pl.BlockSpec((ts, D), lambda i: (i, 0))
    vec = pl.BlockSpec((1, D), lambda i: (0, 0))
    outs = pl.pallas_call(
        body, name=name, grid=(S // ts,), in_specs=[row] + [vec] * n,
        out_specs=[_group_spec(d, ts, D) for _ in range(n) for d in dils],
        out_shape=[_group_shape(d, S, D, BF16) for _ in range(n) for d in dils],
        scratch_shapes=[_chunk_buf(ts, D)],
        compiler_params=_params("parallel"),
    )(x, *gains)
    return [[outs[q * nd + e].reshape(S, D) for e in range(nd)] for q in range(n)]


def _rms_bwd(x, cots, gains, dres, name, ts=256):
    S, D = x.shape
    ts = _tile(S, ts, 16 * max(d for _, _, d in cots))
    n = len(cots)
    ng = len(gains)

    def body(*refs):
        x_ref = refs[0]
        dh_refs = refs[1:1 + n]
        g_refs = refs[1 + n:1 + n + ng]
        dres_ref = refs[1 + n + ng]
        dx_ref, dxb_ref, gg_ref, buf = refs[2 + n + ng:6 + n + ng]
        i = pl.program_id(0)
        xv = x_ref[...]
        r = lax.rsqrt(jnp.mean(xv * xv, axis=-1, keepdims=True) + EPS)
        nrm = xv * r
        dn = jnp.zeros_like(xv)
        rows = [jnp.zeros((1, D), F32) for _ in range(ng)]
        for q, (_, gi, d) in enumerate(cots):
            if d == 1:
                dh = dh_refs[q][...].astype(F32)
            else:
                _store_token_order(buf, dh_refs[q], d)
                dh = _read_chunks(buf)
            dn = dn + dh * g_refs[gi][...]
            rows[gi] = rows[gi] + jnp.sum(dh * nrm, axis=0, keepdims=True)
        dx = dres_ref[...] + r * (dn - nrm * jnp.mean(dn * nrm, axis=-1, keepdims=True))
        dx_ref[...] = dx
        dxb_ref[...] = dx.astype(BF16)
        upd = _stack_rows(rows, 8)

        @pl.when(i == 0)
        def _():
            gg_ref[...] = upd

        @pl.when(i > 0)
        def _():
            gg_ref[...] += upd

    row = pl.BlockSpec((ts, D), lambda i: (i, 0))
    vec = pl.BlockSpec((1, D), lambda i: (0, 0))
    acc = pl.BlockSpec((8, D), lambda i: (0, 0))
    return pl.pallas_call(
        body, name=name, grid=(S // ts,),
        in_specs=[row] + [_group_spec(d, ts, D) for _, _, d in cots] + [vec] * ng + [row],
        out_specs=[row, row, acc],
        out_shape=[jax.ShapeDtypeStruct((S, D), F32), jax.ShapeDtypeStruct((S, D), BF16),
                   jax.ShapeDtypeStruct((8, D), F32)],
        scratch_shapes=[_chunk_buf(ts, D)],
        compiler_params=_params("arbitrary"),
    )(x, *[a if d == 1 else a.reshape(d, S // d, D) for a, _, d in cots], *gains, dres)


def _final_head(x2, gain, target, name, ts=256):
    S, D = x2.shape
    ts = _tile(S, ts, 16)

    def body(x_ref, g_ref, t_ref, loss_ref, dx_ref, dxb_ref, gg_ref):
        i = pl.program_id(0)
        xv = x_ref[...]
        g = g_ref[...]
        r = lax.rsqrt(jnp.mean(xv * xv, axis=-1, keepdims=True) + EPS)
        nrm = xv * r
        err = nrm * g - t_ref[...]
        part = 0.5 * jnp.sum(jnp.mean(err * err, axis=-1, keepdims=True), axis=0, keepdims=True)
        dout = err * (1.0 / D)
        dn = dout * g
        dx = r * (dn - nrm * jnp.mean(dn * nrm, axis=-1, keepdims=True))
        dx_ref[...] = dx
        dxb_ref[...] = dx.astype(BF16)
        upd = _stack_rows([jnp.sum(dout * nrm, axis=0, keepdims=True)], 8)
        lpart = jnp.broadcast_to(part, (1, LANES))

        @pl.when(i == 0)
        def _():
            gg_ref[...] = upd
            loss_ref[...] = lpart

        @pl.when(i > 0)
        def _():
            gg_ref[...] += upd
            loss_ref[...] += lpart

    row = pl.BlockSpec((ts, D), lambda i: (i, 0))
    vec = pl.BlockSpec((1, D), lambda i: (0, 0))
    return pl.pallas_call(
        body, name=name, grid=(S // ts,), in_specs=[row, vec, row],
        out_specs=[pl.BlockSpec((1, LANES), lambda i: (0, 0)), row, row, pl.BlockSpec((8, D), lambda i: (0, 0))],
        out_shape=[jax.ShapeDtypeStruct((1, LANES), F32), jax.ShapeDtypeStruct((S, D), F32),
                   jax.ShapeDtypeStruct((S, D), BF16), jax.ShapeDtypeStruct((8, D), F32)],
        compiler_params=_params("arbitrary"),
    )(x2, gain, target)


CONV_ROWS = 64


def _conv_fwd(proj3, conv_w32, conv_b, name, ts=256, cw=256):
    _, S, E = proj3.shape
    ts = _tile(S, ts, HALO)
    cw = _tile(E, cw)
    per = ts // HALO
    rc = min(CONV_ROWS, ts)

    def body(a_ref, b_ref, ap_ref, bp_ref, w_ref, cb_ref, c_ref, ubuf):
        i = pl.program_id(0)
        up = ap_ref[...].astype(F32) * _sigmoid(bp_ref[...].astype(F32))
        ubuf[0:HALO, :] = jnp.where(i > 0, up, 0.0)
        ubuf[HALO:HALO + ts, :] = a_ref[...].astype(F32) * _sigmoid(b_ref[...].astype(F32))
        for r0 in range(0, ts, rc):
            acc = jnp.broadcast_to(cb_ref[...], (rc, cw))
            for k in range(CONV_TAPS):
                off = r0 + HALO - (CONV_TAPS - 1) + k
                acc = acc + ubuf[off:off + rc, :] * w_ref[k:k + 1, :]
            c_ref[r0:r0 + rc, :] = acc

    return pl.pallas_call(
        body, name=name, grid=(S // ts, E // cw),
        in_specs=[
            pl.BlockSpec((None, ts, cw), lambda i, j: (0, i, j)),
            pl.BlockSpec((None, ts, cw), lambda i, j: (1, i, j)),
            pl.BlockSpec((None, HALO, cw), lambda i, j: (0, jnp.maximum(i * per - 1, 0), j)),
            pl.BlockSpec((None, HALO, cw), lambda i, j: (1, jnp.maximum(i * per - 1, 0), j)),
            pl.BlockSpec((HALO, cw), lambda i, j: (0, j)),
            pl.BlockSpec((1, cw), lambda i, j: (0, j)),
        ],
        out_specs=pl.BlockSpec((ts, cw), lambda i, j: (i, j)),
        out_shape=jax.ShapeDtypeStruct((S, E), F32),
        scratch_shapes=[pltpu.VMEM((HALO + ts, cw), F32)],
        compiler_params=_params("parallel", "parallel"),
    )(proj3, proj3, proj3, proj3, conv_w32, conv_b)


def _ln_gate_fwd(c, proj3, ln_g, ln_b, name, ts=256):
    S, E = c.shape
    ts = _tile(S, ts, 16)

    def body(c_ref, z_ref, g_ref, b_ref, y_ref):
        cv = c_ref[...]
        mu = jnp.mean(cv, axis=-1, keepdims=True)
        d = cv - mu
        var = jnp.mean(d * d, axis=-1, keepdims=True)
        cn = d * lax.rsqrt(var + EPS) * g_ref[...] + b_ref[...]
        z = z_ref[...].astype(F32)
        y_ref[...] = ((cn * _sigmoid(cn)).astype(F32) * (z * _sigmoid(z))).astype(BF16)

    row = pl.BlockSpec((ts, E), lambda i: (i, 0))
    vec = pl.BlockSpec((1, E), lambda i: (0, 0))
    return pl.pallas_call(
        body, name=name, grid=(S // ts,),
        in_specs=[row, pl.BlockSpec((None, ts, E), lambda i: (2, i, 0)), vec, vec],
        out_specs=row, out_shape=jax.ShapeDtypeStruct((S, E), BF16),
        compiler_params=_params("parallel"),
    )(c, proj3, ln_g, ln_b)


def _ln_gate_bwd(c, proj3, dy, ln_g, ln_b, name, ts=256):
    S, E = c.shape
    ts = _tile(S, ts, 16)

    def body(c_ref, z_ref, dy_ref, g_ref, b_ref, dc_ref, dz_ref, acc_ref):
        i = pl.program_id(0)
        cv = c_ref[...]
        g = g_ref[...]
        mu = jnp.mean(cv, axis=-1, keepdims=True)
        d = cv - mu
        var = jnp.mean(d * d, axis=-1, keepdims=True)
        rstd = lax.rsqrt(var + EPS)
        chat = d * rstd
        cn = chat * g + b_ref[...]
        z = z_ref[...].astype(F32)
        dyv = dy_ref[...].astype(F32)
        sc = _sigmoid(cn)
        sz = _sigmoid(z)
        dcn = dyv * (z * sz) * (sc * (1.0 + cn * (1.0 - sc)))
        dz_ref[...] = (dyv * (cn * sc) * (sz * (1.0 + z * (1.0 - sz)))).astype(BF16)
        dchat = dcn * g
        dcv = rstd * (dchat - jnp.mean(dchat, axis=-1, keepdims=True)
                      - chat * jnp.mean(dchat * chat, axis=-1, keepdims=True))
        dc_ref[...] = dcv
        upd = _stack_rows([jnp.sum(dcn * chat, axis=0, keepdims=True),
                           jnp.sum(dcn, axis=0, keepdims=True),
                           jnp.sum(dcv, axis=0, keepdims=True)], 8)

        @pl.when(i == 0)
        def _():
            acc_ref[...] = upd

        @pl.when(i > 0)
        def _():
            acc_ref[...] += upd

    row = pl.BlockSpec((ts, E), lambda i: (i, 0))
    vec = pl.BlockSpec((1, E), lambda i: (0, 0))
    return pl.pallas_call(
        body, name=name, grid=(S // ts,),
        in_specs=[row, pl.BlockSpec((None, ts, E), lambda i: (2, i, 0)), row, vec, vec],
        out_specs=[row, pl.BlockSpec((None, ts, E), lambda i: (2, i, 0)), pl.BlockSpec((8, E), lambda i: (0, 0))],
        out_shape=[jax.ShapeDtypeStruct((S, E), F32), jax.ShapeDtypeStruct((3, S, E), BF16),
                   jax.ShapeDtypeStruct((8, E), F32)],
        compiler_params=_params("arbitrary"),
    )(c, proj3, dy, ln_g, ln_b)


def _conv_bwd(proj3, dc, conv_w32, dproj3, name, ts=256, cw=256):
    _, S, E = proj3.shape
    ts = _tile(S, ts, HALO)
    cw = _tile(E, cw)
    per = ts // HALO
    n_i = S // ts
    last_halo = S // HALO - 1
    rc = min(CONV_ROWS, ts)

    def body(a_ref, b_ref, dc_ref, dcn_ref, w_ref, dp_in, dab_ref, dw_ref, dcbuf, ubuf, dwacc):
        del dp_in
        i = pl.program_id(1)
        dcbuf[0:ts, :] = dc_ref[...]
        dcbuf[ts:ts + HALO, :] = jnp.where(i < n_i - 1, dcn_ref[...], 0.0)
        av = a_ref[...].astype(F32)
        sb = _sigmoid(b_ref[...].astype(F32))
        ubuf[...] = av * sb

        @pl.when(i == 0)
        def _():
            dwacc[...] = jnp.zeros_like(dwacc)

        for r0 in range(0, ts, rc):
            uv = ubuf[r0:r0 + rc, :]
            du = jnp.zeros((rc, cw), F32)
            for d in range(CONV_TAPS):
                k = CONV_TAPS - 1 - d
                win = dcbuf[r0 + d:r0 + d + rc, :]
                du = du + win * w_ref[k:k + 1, :]
                dwacc[k:k + 1, :] += jnp.sum(uv * win, axis=0, keepdims=True)
            a_c = a_ref[r0:r0 + rc, :].astype(F32)
            s_c = _sigmoid(b_ref[r0:r0 + rc, :].astype(F32))
            dab_ref[0, r0:r0 + rc, :] = (du * s_c).astype(BF16)
            dab_ref[1, r0:r0 + rc, :] = (du * a_c * s_c * (1.0 - s_c)).astype(BF16)

        @pl.when(i == n_i - 1)
        def _():
            dw_ref[...] = dwacc[...]

    return pl.pallas_call(
        body, name=name, grid=(E // cw, n_i),
        in_specs=[
            pl.BlockSpec((None, ts, cw), lambda j, i: (0, i, j)),
            pl.BlockSpec((None, ts, cw), lambda j, i: (1, i, j)),
            pl.BlockSpec((ts, cw), lambda j, i: (i, j)),
            pl.BlockSpec((HALO, cw), lambda j, i: (jnp.minimum((i + 1) * per, last_halo), j)),
            pl.BlockSpec((HALO, cw), lambda j, i: (0, j)),
            ANY,
        ],
        out_specs=[pl.BlockSpec((2, ts, cw), lambda j, i: (0, i, j)),
                   pl.BlockSpec((HALO, cw), lambda j, i: (0, j))],
        out_shape=[jax.ShapeDtypeStruct((3, S, E), BF16), jax.ShapeDtypeStruct((HALO, E), F32)],
        scratch_shapes=[pltpu.VMEM((ts + HALO, cw), F32), pltpu.VMEM((ts, cw), F32), pltpu.VMEM((HALO, cw), F32)],
        input_output_aliases={5: 0},
        compiler_params=_params("parallel", "arbitrary"),
    )(proj3, proj3, dc, dc, conv_w32, dproj3)


def _bucket_table(dil):
    delta = (np.arange(BLOCK)[:, None] + BLOCK) - np.arange(2 * BLOCK)[None, :]
    dist = np.clip(delta, 0, None) * dil
    large = MAX_EXACT + (np.log(np.maximum(dist, 1).astype(np.float32) / MAX_EXACT)
                         / np.log(MAX_DISTANCE / MAX_EXACT) * (N_BUCKETS - MAX_EXACT)).astype(np.int32)
    large = np.minimum(large, N_BUCKETS - 1)
    return np.where(dist < MAX_EXACT, dist, large).astype(np.int32).reshape(-1)


def _onehot(dil):
    tbl = jnp.asarray(_bucket_table(dil))
    return (tbl[None, :] == jnp.arange(LANES, dtype=jnp.int32)[:, None]).astype(BF16)


def _split3(v):
    hi = v.astype(BF16)
    r1 = v - hi.astype(F32)
    mid = r1.astype(BF16)
    lo = (r1 - mid.astype(F32)).astype(BF16)
    return hi, mid, lo


def _bias_table(rb_t, onehot, name):
    H = rb_t.shape[0]
    N = onehot.shape[1]

    def body(r_ref, oh_ref, o_ref):
        oh = oh_ref[...]
        hi, mid, lo = _split3(r_ref[...])
        o_ref[...] = (_dot(lo, oh, NN) + _dot(mid, oh, NN)) + _dot(hi, oh, NN)

    return pl.pallas_call(
        body, name=name, in_specs=[VMEM_SPEC, VMEM_SPEC], out_specs=VMEM_SPEC,
        out_shape=jax.ShapeDtypeStruct((H, N), F32),
        compiler_params=pltpu.CompilerParams(vmem_limit_bytes=VMEM_LIMIT),
    )(rb_t, onehot)


def _bias_grad(dbs, onehots, name):
    H = dbs[0].shape[0]
    n = len(dbs)

    def body(*refs):
        acc = jnp.zeros((H, LANES), F32)
        for q in range(n):
            oh = refs[n + q][...]
            hi, mid, lo = _split3(refs[q][...])
            acc = acc + ((_dot(lo, oh, NT) + _dot(mid, oh, NT)) + _dot(hi, oh, NT))
        refs[2 * n][...] = acc

    return pl.pallas_call(
        body, name=name, in_specs=[VMEM_SPEC] * (2 * n), out_specs=VMEM_SPEC,
        out_shape=jax.ShapeDtypeStruct((H, LANES), F32),
        compiler_params=pltpu.CompilerParams(vmem_limit_bytes=VMEM_LIMIT),
    )(*dbs, *onehots)


def _attn_fwd(q, kv, bias, dil, name):
    S, A = q.shape
    H = A // HEAD_DIM
    L = S // dil
    nb = L // BLOCK
    qv = q.reshape(dil, L, A)
    kvv = kv.reshape(2, dil, L, A)

    def body(q_ref, kp_ref, kc_ref, vp_ref, vc_ref, b_ref, o_ref, lse_ref):
        i = pl.program_id(1)
        qi = lax.broadcasted_iota(jnp.int32, (BLOCK, BLOCK), 0)
        ki = lax.broadcasted_iota(jnp.int32, (BLOCK, BLOCK), 1)
        mask_c = ki <= qi
        mask_p = jnp.logical_and(ki >= qi, i > 0)
        lane = lax.broadcasted_iota(jnp.int32, (BLOCK, LANES), 1)
        lse_acc = jnp.zeros((BLOCK, LANES), F32)
        for h in range(H):
            sl = slice(h * HEAD_DIM, (h + 1) * HEAD_DIM)
            qh = q_ref[:, sl]
            s_c = jnp.where(mask_c, _dot(qh, kc_ref[:, sl], NT) * SCALE + b_ref[h, :, BLOCK:], NEG)
            s_p = jnp.where(mask_p, _dot(qh, kp_ref[:, sl], NT) * SCALE + b_ref[h, :, :BLOCK], NEG)
            m = jnp.maximum(jnp.max(s_c, axis=-1, keepdims=True), jnp.max(s_p, axis=-1, keepdims=True))
            p_c = jnp.exp(s_c - m)
            p_p = jnp.exp(s_p - m)
            den = jnp.sum(p_c, axis=-1, keepdims=True) + jnp.sum(p_p, axis=-1, keepdims=True)
            acc = _dot(p_c.astype(BF16), vc_ref[:, sl], NN) + _dot(p_p.astype(BF16), vp_ref[:, sl], NN)
            o_ref[:, sl] = acc / den
            lse_acc = jnp.where(lane == h, m + jnp.log(den), lse_acc)
        lse_ref[...] = lse_acc

    def blk(slab, prev):
        if prev:
            return pl.BlockSpec((None, None, BLOCK, A), lambda r, i: (slab, r, jnp.maximum(i - 1, 0), 0))
        return pl.BlockSpec((None, None, BLOCK, A), lambda r, i: (slab, r, i, 0))

    o, lse = pl.pallas_call(
        body, name=name, grid=(dil, nb),
        in_specs=[pl.BlockSpec((None, BLOCK, A), lambda r, i: (r, i, 0)),
                  blk(0, True), blk(0, False), blk(1, True), blk(1, False),
                  pl.BlockSpec((H, BLOCK, 2 * BLOCK), lambda r, i: (0, 0, 0))],
        out_specs=[pl.BlockSpec((None, BLOCK, A), lambda r, i: (r, i, 0)),
                   pl.BlockSpec((None, BLOCK, LANES), lambda r, i: (r, i, 0))],
        out_shape=[jax.ShapeDtypeStruct((dil, L, A), F32), jax.ShapeDtypeStruct((dil, L, LANES), F32)],
        compiler_params=_params("parallel", "parallel"),
    )(qv, kvv, kvv, kvv, kvv, bias)
    return o.reshape(S, A), lse.reshape(S, LANES)


def _attn_merge(os_, lses, z, dils, name, ts=256):
    S, A = z.shape
    H = A // HEAD_DIM
    ts = _tile(S, ts, 16 * max(dils))
    n = len(os_)

    def body(*refs):
        z_ref = refs[2 * n]
        y_ref, om_ref = refs[2 * n + 1:2 * n + 3]
        lse_refs = refs[2 * n + 3:3 * n + 3]
        o_refs = refs[3 * n + 3:4 * n + 3]
        l_bufs = refs[4 * n + 3:5 * n + 3]
        lse_buf = refs[5 * n + 3]
        ls = []
        for q, d in enumerate(dils):
            if d == 1:
                ls.append(refs[n + q][...])
            else:
                _store_token_order(o_refs[q], refs[q], d)
                _store_token_order(l_bufs[q], refs[n + q], d)
                ls.append(l_bufs[q][0])
        m = ls[0]
        for q in range(1, n):
            m = jnp.maximum(m, ls[q])
        es = [jnp.exp(v - m) for v in ls]
        den = es[0]
        for q in range(1, n):
            den = den + es[q]
        alphas = [e / den for e in es]
        lse = m + jnp.log(den)
        lse_buf[0] = lse
        for q, d in enumerate(dils):
            if d == 1:
                lse_refs[q][...] = lse
            else:
                _emit_group_order(lse_refs[q], lse_buf, d, F32)
        lane = lax.broadcasted_iota(jnp.int32, (ts, LANES), 1)
        for h in range(H):
            sl = slice(h * HEAD_DIM, (h + 1) * HEAD_DIM)
            om = jnp.zeros((ts, HEAD_DIM), F32)
            for q, d in enumerate(dils):
                o_h = refs[q][:, sl] if d == 1 else o_refs[q][h]
                om = om + _lane_col(alphas[q], h, lane) * o_h
            z = z_ref[:, sl].astype(F32)
            y_ref[:, sl] = (om * (z * _sigmoid(z))).astype(BF16)
            om_ref[:, sl] = om.astype(BF16)

    row = pl.BlockSpec((ts, A), lambda i: (i, 0))
    outs = pl.pallas_call(
        body, name=name, grid=(S // ts,),
        in_specs=[_group_spec(d, ts, A) for d in dils] + [_group_spec(d, ts, LANES) for d in dils] + [row],
        out_specs=[row, row] + [_group_spec(d, ts, LANES) for d in dils],
        out_shape=[jax.ShapeDtypeStruct((S, A), BF16), jax.ShapeDtypeStruct((S, A), BF16)]
        + [_group_shape(d, S, LANES, F32) for d in dils],
        scratch_shapes=[_chunk_buf(ts, A)] * n + [_chunk_buf(ts, LANES)] * (n + 1),
        compiler_params=_params("parallel"),
    )(*[o if d == 1 else o.reshape(d, S // d, A) for o, d in zip(os_, dils)],
      *[v if d == 1 else v.reshape(d, S // d, LANES) for v, d in zip(lses, dils)], z)
    return outs[0], outs[1], [v.reshape(S, LANES) for v in outs[2:]]


def _gate_bwd(dy, om, z, dils, name, ts=256):
    S, A = dy.shape
    H = A // HEAD_DIM
    ts = _tile(S, ts, 16 * max(dils))
    n = len(dils)

    def body(*refs):
        dy_ref, om_ref, z_ref = refs[:3]
        do_refs = refs[3:3 + n]
        dh_refs = refs[3 + n:3 + 2 * n]
        dz_ref = refs[3 + 2 * n]
        do_buf, dh_buf = refs[4 + 2 * n:6 + 2 * n]
        lane = lax.broadcasted_iota(jnp.int32, (ts, LANES), 1)
        acc = jnp.zeros((ts, LANES), F32)
        for h in range(H):
            sl = slice(h * HEAD_DIM, (h + 1) * HEAD_DIM)
            dyv = dy_ref[:, sl].astype(F32)
            omv = om_ref[:, sl].astype(F32)
            zv = z_ref[:, sl].astype(F32)
            sz = _sigmoid(zv)
            dob = (dyv * (zv * sz)).astype(BF16)
            do_buf[h] = dob.astype(F32)
            for q, d in enumerate(dils):
                if d == 1:
                    do_refs[q][:, sl] = dob
            dz_ref[:, sl] = (dyv * omv * (sz * (1.0 + zv * (1.0 - sz)))).astype(BF16)
            acc = jnp.where(lane == h, jnp.sum(dob.astype(F32) * omv, axis=-1, keepdims=True), acc)
        dh_buf[0] = acc
        for q, d in enumerate(dils):
            if d == 1:
                dh_refs[q][...] = acc
            else:
                _emit_group_order(do_refs[q], do_buf, d, BF16)
                _emit_group_order(dh_refs[q], dh_buf, d, F32)

    row = pl.BlockSpec((ts, A), lambda i: (i, 0))
    outs = pl.pallas_call(
        body, name=name, grid=(S // ts,), in_specs=[row, row, row],
        out_specs=[_group_spec(d, ts, A) for d in dils] + [_group_spec(d, ts, LANES) for d in dils] + [row],
        out_shape=[_group_shape(d, S, A, BF16) for d in dils] + [_group_shape(d, S, LANES, F32) for d in dils]
        + [jax.ShapeDtypeStruct((S, A), BF16)],
        scratch_shapes=[_chunk_buf(ts, A), _chunk_buf(ts, LANES)],
        compiler_params=_params("parallel"),
    )(dy, om, z)
    return ([v.reshape(S, A) for v in outs[:n]], [v.reshape(S, LANES) for v in outs[n:2 * n]], outs[2 * n])


def _attn_bwd(q, kv, do, lse, dh, bias, dil, name):
    S, A = q.shape
    H = A // HEAD_DIM
    L = S // dil
    nb = L // BLOCK
    qv = q.reshape(dil, L, A)
    kvv = kv.reshape(2, dil, L, A)
    dov = do.reshape(dil, L, A)
    lsev = lse.reshape(dil, L, LANES)
    dhv = dh.reshape(dil, L, LANES)

    def body(*refs):
        (q_ref, qn_ref, kp_ref, kc_ref, vp_ref, vc_ref, do_ref, don_ref, l_ref, ln_ref, d_ref, dn_ref,
         b_ref) = refs[:13]
        dq_ref, dkv_ref, db_ref = refs[13:16]
        r = pl.program_id(0)
        i = pl.program_id(1)
        qi = lax.broadcasted_iota(jnp.int32, (BLOCK, BLOCK), 0)
        ki = lax.broadcasted_iota(jnp.int32, (BLOCK, BLOCK), 1)
        mask_c = ki <= qi
        band = ki >= qi
        mask_p = jnp.logical_and(band, i > 0)
        mask_n = jnp.logical_and(band, i < nb - 1)
        lane = lax.broadcasted_iota(jnp.int32, (BLOCK, LANES), 1)

        @pl.when(jnp.logical_and(r == 0, i == 0))
        def _():
            db_ref[...] = jnp.zeros_like(db_ref)

        for h in range(H):
            sl = slice(h * HEAD_DIM, (h + 1) * HEAD_DIM)
            q_i, q_n = q_ref[:, sl], qn_ref[:, sl]
            k_p, k_c = kp_ref[:, sl], kc_ref[:, sl]
            v_p, v_c = vp_ref[:, sl], vc_ref[:, sl]
            do_i, do_n = do_ref[:, sl], don_ref[:, sl]
            l_i, l_n = _lane_col(l_ref[...], h, lane), _lane_col(ln_ref[...], h, lane)
            d_i, d_n = _lane_col(d_ref[...], h, lane), _lane_col(dn_ref[...], h, lane)
            b_c = b_ref[h, :, BLOCK:]
            b_p = b_ref[h, :, :BLOCK]
            s = jnp.where(mask_c, _dot(q_i, k_c, NT) * SCALE + b_c, NEG)
            p1 = jnp.exp(s - l_i)
            ds1 = p1 * (_dot(do_i, v_c, NT) - d_i)
            ds1b = ds1.astype(BF16)
            p1b = p1.astype(BF16)
            s = jnp.where(mask_p, _dot(q_i, k_p, NT) * SCALE + b_p, NEG)
            p2 = jnp.exp(s - l_i)
            ds2 = p2 * (_dot(do_i, v_p, NT) - d_i)
            ds2b = ds2.astype(BF16)
            s = jnp.where(mask_n, _dot(q_n, k_c, NT) * SCALE + b_p, NEG)
            p3 = jnp.exp(s - l_n)
            ds3b = (p3 * (_dot(do_n, v_c, NT) - d_n)).astype(BF16)
            p3b = p3.astype(BF16)
            dq = _dot(ds1b, k_c, NN) + _dot(ds2b, k_p, NN)
            dk = _dot(ds1b, q_i, TN) + _dot(ds3b, q_n, TN)
            dv = _dot(p1b, do_i, TN) + _dot(p3b, do_n, TN)
            dq_ref[:, sl] = (dq * SCALE).astype(BF16)
            dkv_ref[0, :, sl] = (dk * SCALE).astype(BF16)
            dkv_ref[1, :, sl] = dv.astype(BF16)
            db_ref[h, :, BLOCK:] += ds1
            db_ref[h, :, :BLOCK] += ds2

    def blk(slab, shift):
        if shift < 0:
            return pl.BlockSpec((None, None, BLOCK, A), lambda r, i: (slab, r, jnp.maximum(i - 1, 0), 0))
        return pl.BlockSpec((None, None, BLOCK, A), lambda r, i: (slab, r, i, 0))

    def row(width, shift):
        if shift > 0:
            return pl.BlockSpec((None, BLOCK, width), lambda r, i: (r, jnp.minimum(i + 1, nb - 1), 0))
        return pl.BlockSpec((None, BLOCK, width), lambda r, i: (r, i, 0))

    in_specs = [row(A, 0), row(A, 1), blk(0, -1), blk(0, 0), blk(1, -1), blk(1, 0),
                row(A, 0), row(A, 1), row(LANES, 0), row(LANES, 1), row(LANES, 0), row(LANES, 1),
                pl.BlockSpec((H, BLOCK, 2 * BLOCK), lambda r, i: (0, 0, 0))]
    dq, dkv, db = pl.pallas_call(
        body, name=name, grid=(dil, nb), in_specs=in_specs,
        out_specs=[pl.BlockSpec((None, BLOCK, A), lambda r, i: (r, i, 0)),
                   pl.BlockSpec((2, None, BLOCK, A), lambda r, i: (0, r, i, 0)),
                   pl.BlockSpec((H, BLOCK, 2 * BLOCK), lambda r, i: (0, 0, 0))],
        out_shape=[jax.ShapeDtypeStruct((dil, L, A), BF16), jax.ShapeDtypeStruct((2, dil, L, A), BF16),
                   jax.ShapeDtypeStruct((H, BLOCK, 2 * BLOCK), F32)],
        compiler_params=_params("arbitrary", "arbitrary"),
    )(qv, qv, kvv, kvv, kvv, kvv, dov, dov, lsev, lsev, dhv, dhv, bias)
    return dq.reshape(S, A), dkv.reshape(2, S, A), db


def _sum_leading(stack, out_dtype, name, tr=256, tc=2048):
    n, R, C = stack.shape
    tr = _tile(R, tr, 16)
    tc = _tile(C, tc)

    def body(s_ref, o_ref):
        acc = s_ref[0].astype(F32)
        for q in range(1, n):
            acc = acc + s_ref[q].astype(F32)
        o_ref[...] = acc.astype(out_dtype)

    return pl.pallas_call(
        body, name=name, grid=(R // tr, C // tc),
        in_specs=[pl.BlockSpec((n, tr, tc), lambda i, j: (0, i, j))],
        out_specs=pl.BlockSpec((tr, tc), lambda i, j: (i, j)),
        out_shape=jax.ShapeDtypeStruct((R, C), out_dtype),
        compiler_params=_params("parallel", "parallel"),
    )(stack)


def _add_half(g, t, c_idx, kind, name, tr=256, tc=2048):
    R, C = t.shape
    tr = _tile(R, tr, 16)
    tc = _tile(C, tc)
    nrb, ncb = R // tr, C // tc

    def body(c_ref, g_ref, t_ref, o_ref):
        del c_ref
        o_ref[...] = (g_ref[...].astype(F32) + t_ref[...].astype(F32)).astype(BF16)

    if kind == "col":
        g_map = lambda i, j, c_ref: (c_ref[0] * nrb + i, j)
    else:
        g_map = lambda i, j, c_ref: (i, c_ref[0] * ncb + j)
    same = lambda i, j, c_ref: (i, j)
    return pl.pallas_call(
        body, name=name,
        grid_spec=pltpu.PrefetchScalarGridSpec(
            num_scalar_prefetch=1, grid=(nrb, ncb),
            in_specs=[pl.BlockSpec((tr, tc), g_map), pl.BlockSpec((tr, tc), same)],
            out_specs=pl.BlockSpec((tr, tc), same)),
        out_shape=jax.ShapeDtypeStruct((R, C), BF16),
        compiler_params=_params("parallel", "parallel"),
    )(c_idx, g, t)


def _cast_into_full(w, kind, chip_idx, name, tr=256, tc=2048):
    R, C = w.shape
    tr = _tile(R, tr, 16)
    tc = _tile(C, tc)
    nrb, ncb = R // tr, C // tc

    def body(k_ref, w_ref, o_ref):
        del k_ref
        o_ref[...] = w_ref[...].astype(BF16)

    if kind == "col":
        o_map = lambda i, j, k_ref: (i, k_ref[0] * ncb + j)
        full = (R, N_CHIPS * C)
    else:
        o_map = lambda i, j, k_ref: (k_ref[0] * nrb + i, j)
        full = (N_CHIPS * R, C)
    return pl.pallas_call(
        body, name=name,
        grid_spec=pltpu.PrefetchScalarGridSpec(
            num_scalar_prefetch=1, grid=(nrb, ncb),
            in_specs=[pl.BlockSpec((tr, tc), lambda i, j, k_ref: (i, j))],
            out_specs=pl.BlockSpec((tr, tc), o_map)),
        out_shape=jax.ShapeDtypeStruct(full, BF16),
        compiler_params=_params("parallel", "parallel"),
    )(chip_idx, w)


def _sum_into_shard(p, u, idx, kind, name, tr=256, tc=2048):
    _, R, C = u.shape
    tr = _tile(R, tr, 16)
    tc = _tile(C, tc)
    nrb, ncb = R // tr, C // tc

    def body(i_ref, p_ref, u_ref, o_ref):
        del i_ref
        acc = p_ref[...].astype(F32)
        for q in range(N_CHIPS - 1):
            acc = acc + u_ref[q].astype(F32)
        o_ref[...] = acc

    if kind == "col":
        p_map = lambda i, j, r: (i, r[0] * ncb + j)
        o_map = lambda i, j, r: (r[1] * nrb + i, j)
        full = (2 * R, C)
    else:
        p_map = lambda i, j, r: (r[0] * nrb + i, j)
        o_map = lambda i, j, r: (i, r[1] * ncb + j)
        full = (R, 2 * C)
    return pl.pallas_call(
        body, name=name,
        grid_spec=pltpu.PrefetchScalarGridSpec(
            num_scalar_prefetch=1, grid=(nrb, ncb),
            in_specs=[pl.BlockSpec((tr, tc), p_map), pl.BlockSpec((N_CHIPS - 1, tr, tc), lambda i, j, r: (0, i, j))],
            out_specs=pl.BlockSpec((tr, tc), o_map)),
        out_shape=jax.ShapeDtypeStruct(full, F32),
        compiler_params=_params("parallel", "parallel"),
    )(idx, p, u)


def _adamw(w, g, m, v, name, tr=256, tc=2048):
    R, C = w.shape
    tr = _tile(R, tr, 8)
    tc = _tile(C, tc)
    c1 = 1.0 - ADAM_B1 ** ADAM_STEP
    c2 = 1.0 - ADAM_B2 ** ADAM_STEP

    def body(w_ref, g_ref, m_ref, v_ref, d_ref, nm_ref, nv_ref):
        gv = g_ref[...]
        nm = ADAM_B1 * m_ref[...] + (1.0 - ADAM_B1) * gv
        nv = ADAM_B2 * v_ref[...] + (1.0 - ADAM_B2) * (gv * gv)
        d_ref[...] = -ADAM_LR * ((nm / c1) / (jnp.sqrt(nv / c2) + ADAM_EPS) + ADAM_WD * w_ref[...])
        nm_ref[...] = nm
        nv_ref[...] = nv

    blk = pl.BlockSpec((tr, tc), lambda i, j: (i, j))
    sh = jax.ShapeDtypeStruct((R, C), F32)
    return pl.pallas_call(
        body, name=name, grid=(R // tr, C // tc), in_specs=[blk] * 4, out_specs=[blk] * 3,
        out_shape=[sh, sh, sh], compiler_params=_params("parallel", "parallel"),
    )(w, g, m, v)


def _xyc():
    return lax.axis_index("x"), lax.axis_index("y"), lax.axis_index("c")


def _drain(copies):
    for cp in copies:
        if cp.is_remote:
            cp.wait_send()
        else:
            cp.wait()


def _other_chips(x, y):
    return [(1 - x, y), (x, 1 - y), (1 - x, 1 - y)]


def _allgather_small(blk, name):
    R, C = blk.shape

    def body(x_ref, out_ref, send_sems, recv_sems, local_sem):
        x, y, c = _xyc()
        me = 4 * x + 2 * y + c
        mine = pltpu.make_async_copy(x_ref, out_ref.at[me], local_sem)
        mine.start()
        peers = []
        for k in range(1, N_DEV):
            px = 1 - x if (k >> 2) & 1 else x
            py = 1 - y if (k >> 1) & 1 else y
            pc = 1 - c if k & 1 else c
            peers.append((px, py, pc))
        sends = []
        for k, peer in enumerate(peers):
            cp = pltpu.make_async_remote_copy(
                src_ref=x_ref, dst_ref=out_ref.at[me], send_sem=send_sems.at[k], recv_sem=recv_sems.at[k],
                device_id=peer, device_id_type=MESH)
            cp.start()
            sends.append(cp)
        for k, (px, py, pc) in enumerate(peers):
            pltpu.make_async_remote_copy(
                src_ref=x_ref, dst_ref=out_ref.at[4 * px + 2 * py + pc], send_sem=send_sems.at[k],
                recv_sem=recv_sems.at[k], device_id=(px, py, pc), device_id_type=MESH).wait_recv()
        for cp in sends:
            cp.wait_send()
        mine.wait()

    return pl.pallas_call(
        body, name=name, in_specs=[VMEM_SPEC], out_specs=VMEM_SPEC,
        out_shape=jax.ShapeDtypeStruct((N_DEV, R, C), blk.dtype),
        scratch_shapes=[pltpu.SemaphoreType.DMA((N_DEV - 1,)), pltpu.SemaphoreType.DMA((N_DEV - 1,)),
                        pltpu.SemaphoreType.DMA],
        compiler_params=pltpu.CompilerParams(vmem_limit_bytes=VMEM_LIMIT),
    )(blk)


def _full_region(ref, kind, chip, half, shard_shape):
    r, cn = shard_shape
    hr = r // 2
    if kind == "col":
        rows = pl.ds(0, r) if half is None else pl.ds(pl.multiple_of(half * hr, 16), hr)
        return ref.at[rows, pl.ds(pl.multiple_of(chip * cn, LANES), cn)]
    if half is None:
        return ref.at[pl.ds(pl.multiple_of(chip * r, 16), r), :]
    return ref.at[pl.ds(pl.multiple_of(chip * r + half * hr, 16), hr), :]


def _allgather_weights(fulls, kinds, shapes, name):
    n = len(fulls)

    def body(*refs):
        outs = refs[n:2 * n]
        send_sems, recv_sems = refs[2 * n:]
        x, y, c = _xyc()
        chip = 2 * x + y
        sib = (x, y, 1 - c)
        others = _other_chips(x, y)
        started = []
        for w in range(n):
            mine = _full_region(outs[w], kinds[w], chip, c, shapes[w])
            for j, (ox, oy) in enumerate(others):
                cp = pltpu.make_async_remote_copy(
                    src_ref=mine, dst_ref=mine, send_sem=send_sems.at[6 * w + j], recv_sem=recv_sems.at[6 * w + j],
                    device_id=(ox, oy, c), device_id_type=MESH)
                cp.start()
                started.append(cp)
        for w in range(n):
            for j, (ox, oy) in enumerate(others):
                landed = _full_region(outs[w], kinds[w], 2 * ox + oy, c, shapes[w])
                pltpu.make_async_remote_copy(
                    src_ref=landed, dst_ref=landed, send_sem=send_sems.at[6 * w + j], recv_sem=recv_sems.at[6 * w + j],
                    device_id=(ox, oy, c), device_id_type=MESH).wait_recv()
                cp = pltpu.make_async_remote_copy(
                    src_ref=landed, dst_ref=landed, send_sem=send_sems.at[6 * w + 3 + j],
                    recv_sem=recv_sems.at[6 * w + 3 + j], device_id=sib, device_id_type=MESH)
                cp.start()
                started.append(cp)
        for w in range(n):
            for j, (ox, oy) in enumerate(others):
                theirs = _full_region(outs[w], kinds[w], 2 * ox + oy, 1 - c, shapes[w])
                pltpu.make_async_remote_copy(
                    src_ref=theirs, dst_ref=theirs, send_sem=send_sems.at[6 * w + 3 + j],
                    recv_sem=recv_sems.at[6 * w + 3 + j], device_id=sib, device_id_type=MESH).wait_recv()
        _drain(started)

    return pl.pallas_call(
        body, name=name, in_specs=[ANY] * n, out_specs=[ANY] * n,
        out_shape=[jax.ShapeDtypeStruct(f.shape, f.dtype) for f in fulls],
        input_output_aliases={w: w for w in range(n)},
        scratch_shapes=[pltpu.SemaphoreType.DMA((6 * n,)), pltpu.SemaphoreType.DMA((6 * n,))],
    )(*fulls)


def _half_of(ref, kind, half):
    r, cn = ref.shape
    if kind == "col":
        return ref.at[pl.ds(pl.multiple_of(half * (r // 2), 16), r // 2), :]
    return ref.at[:, pl.ds(pl.multiple_of(half * (cn // 2), LANES), cn // 2)]


def _shard_of(ref, kind, chip):
    r, cn = ref.shape
    if kind == "col":
        return ref.at[:, pl.ds(pl.multiple_of(chip * (cn // N_CHIPS), LANES), cn // N_CHIPS)]
    return ref.at[pl.ds(pl.multiple_of(chip * (r // N_CHIPS), 16), r // N_CHIPS), :]


def _exchange_halves(grads, kinds, name):
    n = len(grads)

    def body(*refs):
        gs = refs[:n]
        ts = refs[n:2 * n]
        send_sems, recv_sems = refs[2 * n:]
        x, y, c = _xyc()
        cps = []
        for w in range(n):
            cp = pltpu.make_async_remote_copy(
                src_ref=_half_of(gs[w], kinds[w], 1 - c), dst_ref=ts[w], send_sem=send_sems.at[w],
                recv_sem=recv_sems.at[w], device_id=(x, y, 1 - c), device_id_type=MESH)
            cp.start()
            cps.append(cp)
        for cp in cps:
            cp.wait()

    out_shape = []
    for gr, kind in zip(grads, kinds):
        r, cn = gr.shape
        out_shape.append(jax.ShapeDtypeStruct((r // 2, cn) if kind == "col" else (r, cn // 2), gr.dtype))
    return pl.pallas_call(
        body, name=name, in_specs=[ANY] * n, out_specs=[ANY] * n, out_shape=out_shape,
        scratch_shapes=[pltpu.SemaphoreType.DMA((n,)), pltpu.SemaphoreType.DMA((n,))],
    )(*grads)


def _scatter_partials(parts, kinds, name):
    n = len(parts)

    def body(*refs):
        ps = refs[:n]
        us = refs[n:2 * n]
        send_sems, recv_sems = refs[2 * n:]
        x, y, c = _xyc()
        others = _other_chips(x, y)
        cps = []
        for w in range(n):
            for j, (ox, oy) in enumerate(others):
                cp = pltpu.make_async_remote_copy(
                    src_ref=_shard_of(ps[w], kinds[w], 2 * ox + oy), dst_ref=us[w].at[j],
                    send_sem=send_sems.at[3 * w + j], recv_sem=recv_sems.at[3 * w + j],
                    device_id=(ox, oy, c), device_id_type=MESH)
                cp.start()
                cps.append(cp)
        for cp in cps:
            cp.wait()

    out_shape = []
    for p, kind in zip(parts, kinds):
        r, cn = p.shape
        hs = (r, cn // N_CHIPS) if kind == "col" else (r // N_CHIPS, cn)
        out_shape.append(jax.ShapeDtypeStruct((N_CHIPS - 1,) + hs, p.dtype))
    return pl.pallas_call(
        body, name=name, in_specs=[ANY] * n, out_specs=[ANY] * n, out_shape=out_shape,
        scratch_shapes=[pltpu.SemaphoreType.DMA((3 * n,)), pltpu.SemaphoreType.DMA((3 * n,))],
    )(*parts)


def _join_halves(halves, kinds, name):
    n = len(halves)

    def body(*refs):
        outs = refs[n:2 * n]
        send_sems, recv_sems = refs[2 * n:]
        x, y, c = _xyc()
        cps = []
        for w in range(n):
            mine = _half_of(outs[w], kinds[w], c)
            cp = pltpu.make_async_remote_copy(
                src_ref=mine, dst_ref=mine, send_sem=send_sems.at[w], recv_sem=recv_sems.at[w],
                device_id=(x, y, 1 - c), device_id_type=MESH)
            cp.start()
            cps.append(cp)
        for w in range(n):
            theirs = _half_of(outs[w], kinds[w], 1 - c)
            pltpu.make_async_remote_copy(
                src_ref=theirs, dst_ref=theirs, send_sem=send_sems.at[w], recv_sem=recv_sems.at[w],
                device_id=(x, y, 1 - c), device_id_type=MESH).wait_recv()
        _drain(cps)

    return pl.pallas_call(
        body, name=name, in_specs=[ANY] * n, out_specs=[ANY] * n,
        out_shape=[jax.ShapeDtypeStruct(h.shape, h.dtype) for h in halves],
        input_output_aliases={w: w for w in range(n)},
        scratch_shapes=[pltpu.SemaphoreType.DMA((n,)), pltpu.SemaphoreType.DMA((n,))],
    )(*halves)


def kernel(x, a_norm, a_w_in, a_conv_w, a_conv_b, a_ln_g, a_ln_b, a_w_out, kv_norm, w_kv, b_norm, b_w_in, b_w_out, rel_bias, final_norm, loss_target, m_a_norm, m_a_w_in, m_a_conv_w, m_a_conv_b, m_a_ln_g, m_a_ln_b, m_a_w_out, m_kv_norm, m_w_kv, m_b_norm, m_b_w_in, m_b_w_out, m_rel_bias, m_final_norm, v_a_norm, v_a_w_in, v_a_conv_w, v_a_conv_b, v_a_ln_g, v_a_ln_b, v_a_w_out, v_kv_norm, v_w_kv, v_b_norm, v_b_w_in, v_b_w_out, v_rel_bias, v_final_norm):
    S, D = x.shape[1], x.shape[2]
    E = a_w_out.shape[1] * N_CHIPS
    A = b_w_out.shape[1] * N_CHIPS
    H = A // HEAD_DIM
    DC = D // N_CHIPS
    xs = x.reshape(S, D)
    tgt = loss_target.reshape(S, D)
    cx, cy, cc = _xyc()
    chip = 2 * cx + cy
    c_idx = jnp.reshape(cc, (1,)).astype(jnp.int32)

    big_names = ["a_w_in", "a_w_out", "w_kv", "b_w_in", "b_w_out"]
    kinds = ["col", "row", "col", "col", "row"]
    big_w = [a_w_in[0], a_w_out[0], w_kv, b_w_in[0], b_w_out[0]]
    big_m = [m_a_w_in[0], m_a_w_out[0], m_w_kv, m_b_w_in[0], m_b_w_out[0]]
    big_v = [v_a_w_in[0], v_a_w_out[0], v_w_kv, v_b_w_in[0], v_b_w_out[0]]
    chip_idx = jnp.reshape(chip, (1,)).astype(jnp.int32)
    placed = [_cast_into_full(big_w[w], kinds[w], chip_idx, "cast_" + big_names[w]) for w in range(5)]
    wa_in, wa_out, wkv, wb_in, wb_out = _allgather_weights(placed, kinds, [w.shape for w in big_w], "ag_weights")

    def pack_sharded(an, cw, cb, lg, lb):
        return jnp.concatenate([an, cw[0], cb, lg, lb, jnp.zeros((5, DC), F32)], axis=0)

    small_w = pack_sharded(a_norm, a_conv_w, a_conv_b, a_ln_g, a_ln_b)
    gathered = _allgather_small(small_w, "ag_small_params")
    small_full = jnp.concatenate([gathered[2 * k] for k in range(N_CHIPS)], axis=1)
    g_a = small_full[0:1]
    conv_w32 = small_full[1:1 + HALO]
    conv_b = small_full[32:33]
    ln_g = small_full[33:34]
    ln_b = small_full[34:35]
    g_kv = kv_norm.reshape(1, D)
    g_b = b_norm.reshape(1, D)
    g_f = final_norm.reshape(1, D)

    rb_t = jnp.pad(rel_bias.T, ((0, 0), (0, LANES - N_BUCKETS)))
    onehots = [_onehot(dil) for _, dil in GROUPS]
    biases = [_bias_table(rb_t, onehots[g], "bias_table_%d" % g).reshape(H, BLOCK, 2 * BLOCK)
              for g in range(len(GROUPS))]

    dils = tuple(dil for _, dil in GROUPS)
    assert dils[0] == 1
    n_g = len(GROUPS)
    ((h0,),) = _rms_fwd(xs, [g_a], (1,), "rms_a")
    proj3 = _matmul(h0, wa_in, "nn", BF16, "mm_a_in", out_slab=E)
    conv = _conv_fwd(proj3, conv_w32, conv_b, "conv_fwd")
    y_a = _ln_gate_fwd(conv, proj3, ln_g, ln_b, "ln_gate_fwd")
    x1 = _matmul(y_a, wa_out, "nn", F32, "mm_a_out", res=xs)
    hks, hbs = _rms_fwd(x1, [g_kv, g_b], dils, "rms_kv_b")
    kvs = [_matmul(hks[g], wkv, "nn", BF16, "mm_kv_%d" % g, out_slab=A, b_off=2 * g * A, n_cols=2 * A)
           for g in range(n_g)]
    qs = [_matmul(hbs[g], wb_in, "nn", BF16, "mm_q_%d" % g, b_off=g * A, n_cols=A) for g in range(n_g)]
    zb = _matmul(hbs[0], wb_in, "nn", BF16, "mm_zb", b_off=n_g * A, n_cols=A)
    os_, lses = [], []
    for g, dil in enumerate(dils):
        o_g, lse_g = _attn_fwd(qs[g], kvs[g], biases[g], dil, "attn_fwd_%d" % g)
        os_.append(o_g)
        lses.append(lse_g)
    y_b, o_m, lse_d = _attn_merge(os_, lses, zb, dils, "attn_merge")
    x2 = _matmul(y_b, wb_out, "nn", F32, "mm_b_out", res=x1)
    loss_part, dx2, dx2b, gg_f = _final_head(x2, g_f, tgt, "final_head")
    loss = lax.psum(loss_part[0, 0], ("x", "y", "c"))

    dw_tiles = dict(tm=512, tn=1024, tk=1024)
    dy_b = _matmul(dx2b, wb_out, "nt", BF16, "mm_b_out_dx")
    dwb_out = _matmul(y_b, dx2b, "tn", BF16, "mm_b_out_dw", **dw_tiles)
    dos, dhs, dzb = _gate_bwd(dy_b, o_m, zb, dils, "gate_bwd")
    dbs, cots = [], []
    dwb_in = dwkv = None
    for g, dil in enumerate(dils):
        dq, dkv, db = _attn_bwd(qs[g], kvs[g], dos[g], lse_d[g], dhs[g], biases[g], dil, "attn_bwd_%d" % g)
        dbs.append(db.reshape(H, BLOCK * 2 * BLOCK))
        dwb_in = _matmul(hbs[g], dq, "tn", BF16, "mm_q_dw_%d" % g, out_off=g * A, out_cols=(n_g + 1) * A,
                         out_alias=dwb_in, **dw_tiles)
        dwkv = _matmul(hks[g], dkv, "tn", BF16, "mm_kv_dw_%d" % g, b_slab=True, out_off=2 * g * A,
                       out_cols=2 * n_g * A, out_alias=dwkv, **dw_tiles)
        cots.append((_matmul(dkv, wkv, "nt", BF16, "mm_kv_dx_%d" % g, a_slab=True, b_off=2 * g * A), 0, dil))
        cots.append((_matmul(dq, wb_in, "nt", BF16, "mm_q_dx_%d" % g, b_off=g * A), 1, dil))
    dwb_in = _matmul(hbs[0], dzb, "tn", BF16, "mm_zb_dw", out_off=n_g * A, out_cols=(n_g + 1) * A,
                     out_alias=dwb_in, **dw_tiles)
    cots.append((_matmul(dzb, wb_in, "nt", BF16, "mm_zb_dx", b_off=n_g * A), 1, 1))
    g_rel_t = _bias_grad(dbs, onehots, "bias_grad")
    dx1, dx1b, gg_kvb = _rms_bwd(x1, cots, [g_kv, g_b], dx2, "rms_kv_b_bwd")
    dy_a = _matmul(dx1b, wa_out, "nt", BF16, "mm_a_out_dx")
    dwa_out = _matmul(y_a, dx1b, "tn", BF16, "mm_a_out_dw", **dw_tiles)
    dconv, dproj3, gg_ln = _ln_gate_bwd(conv, proj3, dy_a, ln_g, ln_b, "ln_gate_bwd")
    dproj3, g_conv_w = _conv_bwd(proj3, dconv, conv_w32, dproj3, "conv_bwd")
    dh0 = _matmul(dproj3, wa_in, "nt", BF16, "mm_a_in_dx", a_slab=True)
    dwa_in = _matmul(h0, dproj3, "tn", BF16, "mm_a_in_dw", b_slab=True, **dw_tiles)
    grad_x, _, gg_a = _rms_bwd(xs, [(dh0, 0, 1)], [g_a], dx1, "rms_a_bwd")

    full_grads = [dwa_in, dwa_out, dwkv, dwb_in, dwb_out]
    theirs = _exchange_halves(full_grads, kinds, "rs_exchange_halves")
    parts = [_add_half(full_grads[w], theirs[w], c_idx, kinds[w], "rs_add_half_%d" % w) for w in range(5)]
    slots = _scatter_partials(parts, kinds, "rs_scatter_partials")
    chip_c = jnp.stack([chip, cc]).astype(jnp.int32)
    halves = [_sum_into_shard(parts[w], slots[w], chip_c, kinds[w], "rs_sum_chips_%d" % w) for w in range(5)]
    big_g = _join_halves(halves, kinds, "rs_join_halves")

    g_rel_row = jnp.pad(g_rel_t[:, :N_BUCKETS].T.reshape(1, N_BUCKETS * H), ((0, 0), (0, D - N_BUCKETS * H)))
    small_g = jnp.concatenate([
        gg_a[0:1], g_conv_w[0:CONV_TAPS], gg_ln[2:3], gg_ln[0:1], gg_ln[1:2],
        gg_kvb[0:1], gg_kvb[1:2], gg_f[0:1], g_rel_row, jnp.zeros((1, D), F32)], axis=0)
    small_sum = _sum_leading(_allgather_small(small_g, "ag_small_grads"), F32, "sum_small_grads", tr=40)
    g_sharded = lax.dynamic_slice(small_sum, (0, chip * DC), (40, DC))
    g_repl = jnp.concatenate([small_sum[35:39], jnp.zeros((4, D), F32)], axis=0)

    outs_g, outs_d, outs_m, outs_v = {}, {}, {}, {}
    for w, nm in enumerate(big_names):
        d_, m_, v_ = _adamw(big_w[w], big_g[w], big_m[w], big_v[w], "adamw_" + nm)
        outs_g[nm], outs_d[nm], outs_m[nm], outs_v[nm] = big_g[w], d_, m_, v_
    sm_m = pack_sharded(m_a_norm, m_a_conv_w, m_a_conv_b, m_a_ln_g, m_a_ln_b)
    sm_v = pack_sharded(v_a_norm, v_a_conv_w, v_a_conv_b, v_a_ln_g, v_a_ln_b)
    sd, smm, svv = _adamw(small_w, g_sharded, sm_m, sm_v, "adamw_small_sharded")

    def unpack_sharded(p):
        return {"a_norm": p[0:1], "a_conv_w": p[1:32].reshape(1, CONV_TAPS, DC), "a_conv_b": p[32:33],
                "a_ln_g": p[33:34], "a_ln_b": p[34:35]}

    for src, dst in ((g_sharded, outs_g), (sd, outs_d), (smm, outs_m), (svv, outs_v)):
        dst.update(unpack_sharded(src))

    def pack_repl(kn, bn, fn, rb):
        rbrow = jnp.pad(rb.reshape(1, N_BUCKETS * H), ((0, 0), (0, D - N_BUCKETS * H)))
        return jnp.concatenate([kn.reshape(1, D), bn.reshape(1, D), fn.reshape(1, D), rbrow, jnp.zeros((4, D), F32)], axis=0)

    rp_w = pack_repl(kv_norm, b_norm, final_norm, rel_bias)
    rp_m = pack_repl(m_kv_norm, m_b_norm, m_final_norm, m_rel_bias)
    rp_v = pack_repl(v_kv_norm, v_b_norm, v_final_norm, v_rel_bias)
    rd, rmm, rvv = _adamw(rp_w, g_repl, rp_m, rp_v, "adamw_small_replicated")

    def unpack_repl(p):
        return {"kv_norm": p[0], "b_norm": p[1:2], "final_norm": p[2],
                "rel_bias": p[3, :N_BUCKETS * H].reshape(N_BUCKETS, H)}

    for src, dst in ((g_repl, outs_g), (rd, outs_d), (rmm, outs_m), (rvv, outs_v)):
        dst.update(unpack_repl(src))

    order = ["a_norm", "a_w_in", "a_conv_w", "a_conv_b", "a_ln_g", "a_ln_b", "a_w_out", "kv_norm", "w_kv",
             "b_norm", "b_w_in", "b_w_out", "rel_bias", "final_norm"]
    lead = {"a_w_in", "a_w_out", "b_w_in", "b_w_out"}

    def shaped(nm, val):
        return val[None] if nm in lead else val

    result = [loss, grad_x.reshape(1, S, D)]
    for table in (outs_g, outs_d, outs_m, outs_v):
        result.extend(shaped(nm, table[nm]) for nm in order)
    return tuple(result)
```

```python
import functools

import numpy as np
import jax
import jax.numpy as jnp
from jax import lax
from jax.experimental import pallas as pl
from jax.experimental.pallas import tpu as pltpu
from jax.experimental.pallas import tpu_sc as plsc

F32 = jnp.float32
BF16 = jnp.bfloat16
MESH = pl.DeviceIdType.MESH
ANY = pl.BlockSpec(memory_space=pl.ANY)
VMEM_SPEC = pl.BlockSpec(memory_space=pltpu.VMEM)

EPS = 1e-6
HEAD_DIM = 128
BLOCK = 128
GROUPS = ((128, 1), (512, 4), (2048, 16))
SCALE = HEAD_DIM ** -0.5
CONV_TAPS = 31
HALO = 32
N_BUCKETS = 32
MAX_EXACT = 16
MAX_DISTANCE = 2048
NEG = -1e30
N_CHIPS = 4
N_DEV = 8
LANES = 128
VMEM_LIMIT = 56 * 1024 * 1024

ADAM_LR = 0.001
ADAM_B1 = 0.9
ADAM_B2 = 0.999
ADAM_EPS = 1e-08
ADAM_WD = 0.01
ADAM_STEP = 10


def _tile(n, pref, mult=LANES):
    t = (min(pref, n) // mult) * mult
    while t >= mult:
        if n % t == 0:
            return t
        t -= mult
    return n


def _params(*sem):
    return pltpu.CompilerParams(dimension_semantics=sem, vmem_limit_bytes=VMEM_LIMIT)


def _sigmoid(v):
    return 1.0 / (1.0 + jnp.exp(-v))


def _dot(a, b, dims):
    return lax.dot_general(a, b, (dims, ((), ())), preferred_element_type=F32)


NN = ((1,), (0,))
NT = ((1,), (1,))
TN = ((0,), (0,))


def _stack_rows(rows, total):
    width = rows[0].shape[1]
    rid = lax.broadcasted_iota(jnp.int32, (total, width), 0)
    out = jnp.zeros((total, width), F32)
    for q, row in enumerate(rows):
        out = jnp.where(rid == q, jnp.broadcast_to(row, (total, width)), out)
    return out


def _lane_col(arr, h, lane):
    return jnp.sum(jnp.where(lane == h, arr, 0.0), axis=-1, keepdims=True)


def _matmul(a, b, mode, out_dtype, name, res=None, a_slab=False, b_slab=False, out_slab=0,
            b_off=0, n_cols=None, out_off=0, out_cols=None, out_alias=None, tm=512, tn=1024, tk=2048):
    if a_slab:
        na, M, W = a.shape
        K = na * W
    elif mode == "tn":
        K, M = a.shape
    else:
        M, K = a.shape
    if b_slab:
        nbs, _, Wb = b.shape
        N = nbs * Wb
    elif mode == "nt":
        N = b.shape[0]
    else:
        N = n_cols if n_cols else b.shape[1]
    tm = _tile(M, tm)
    tn = _tile(Wb if b_slab else (out_slab if out_slab else N), tn)
    tk = _tile(W if a_slab else K, tk)
    nk = K // tk
    grid = (M // tm, N // tn, nk)
    bo = b_off // (tk if mode == "nt" else tn)
    oo = out_off // tn

    if a_slab:
        per = W // tk
        a_spec = pl.BlockSpec((None, tm, tk), lambda i, j, k: (k // per, i, k % per))
    elif mode == "tn":
        a_spec = pl.BlockSpec((tk, tm), lambda i, j, k: (k, i))
    else:
        a_spec = pl.BlockSpec((tm, tk), lambda i, j, k: (i, k))
    if b_slab:
        perb = Wb // tn
        b_spec = pl.BlockSpec((None, tk, tn), lambda i, j, k: (j // perb, k, j % perb))
    elif mode == "nt":
        b_spec = pl.BlockSpec((tn, tk), lambda i, j, k: (j, k + bo))
    else:
        b_spec = pl.BlockSpec((tk, tn), lambda i, j, k: (k, j + bo))
    if out_slab:
        pero = out_slab // tn
        o_spec = pl.BlockSpec((None, tm, tn), lambda i, j, k: (j // pero, i, j % pero))
        out_shape = jax.ShapeDtypeStruct((N // out_slab, M, out_slab), out_dtype)
    else:
        o_spec = pl.BlockSpec((tm, tn), lambda i, j, k: (i, j + oo))
        out_shape = jax.ShapeDtypeStruct((M, out_cols if out_cols else N), out_dtype)
    in_specs = [a_spec, b_spec]
    operands = [a, b]
    if res is not None:
        in_specs.append(pl.BlockSpec((tm, tn), lambda i, j, k: (i, j)))
        operands.append(res)
    aliases = {}
    if out_alias is not None:
        aliases[len(operands)] = 0
        in_specs.append(ANY)
        operands.append(out_alias)
    dims = {"nn": NN, "nt": NT, "tn": TN}[mode]
    has_res = res is not None
    n_in = len(operands)

    def body(*refs):
        a_ref, b_ref = refs[0], refs[1]
        r_ref = refs[2] if has_res else None
        o_ref = refs[n_in]
        prod = _dot(a_ref[...], b_ref[...], dims)

        def finish(val):
            if has_res:
                val = val + r_ref[...]
            o_ref[...] = val.astype(out_dtype)

        if nk == 1:
            finish(prod)
        else:
            acc_ref = refs[n_in + 1]
            k = pl.program_id(2)

            @pl.when(k == 0)
            def _():
                acc_ref[...] = prod

            @pl.when(k > 0)
            def _():
                acc_ref[...] += prod

            @pl.when(k == nk - 1)
            def _():
                finish(acc_ref[...])

    scratch = [pltpu.VMEM((tm, tn), F32)] if nk > 1 else []
    return pl.pallas_call(
        body, name=name, grid=grid, in_specs=in_specs, out_specs=o_spec, out_shape=out_shape,
        scratch_shapes=scratch, input_output_aliases=aliases,
        compiler_params=_params("parallel", "parallel", "arbitrary"),
    )(*operands)


def _group_spec(d, ts, width):
    if d == 1:
        return pl.BlockSpec((ts, width), lambda i: (i, 0))
    return pl.BlockSpec((d, ts // d, width), lambda i: (0, i, 0))


def _group_shape(d, S, width, dtype):
    return jax.ShapeDtypeStruct((S, width) if d == 1 else (d, S // d, width), dtype)


def _chunk_buf(ts, width):
    return pltpu.VMEM((width // LANES, ts, LANES), F32)


def _fill_chunks(buf, val):
    for c in range(buf.shape[0]):
        buf[c] = val[:, c * LANES:(c + 1) * LANES]


def _read_chunks(buf):
    return jnp.concatenate([buf[c] for c in range(buf.shape[0])], axis=1)


def _emit_group_order(o_ref, buf, d, dtype):
    n = buf.shape[1] // d
    for r in range(d):
        for c in range(buf.shape[0]):
            o_ref[r, :, c * LANES:(c + 1) * LANES] = buf[c, pl.ds(r, n, stride=d), :].astype(dtype)


def _store_token_order(buf, i_ref, d):
    n = buf.shape[1] // d
    for r in range(d):
        for c in range(buf.shape[0]):
            buf[c, pl.ds(r, n, stride=d), :] = i_ref[r, :, c * LANES:(c + 1) * LANES].astype(F32)


def _rms_fwd(x, gains, dils, name, ts=256):
    S, D = x.shape
    ts = _tile(S, ts, 16 * max(dils))
    n = len(gains)
    nd = len(dils)

    def body(*refs):
        buf = refs[1 + n + n * nd]
        xv = refs[0][...]
        nrm = xv * lax.rsqrt(jnp.mean(xv * xv, axis=-1, keepdims=True) + EPS)
        for q in range(n):
            val = nrm * refs[1 + q][...]
            if max(dils) > 1:
                _fill_chunks(buf, val)
            for e, d in enumerate(dils):
                if d == 1:
                    refs[1 + n + q * nd + e][...] = val.astype(BF16)
                else:
                    _emit_group_order(refs[1 + n + q * nd + e], buf, d, BF16)

    row = pl.BlockSpec((ts, D), lambda i: (i, 0))
    vec = pl.BlockSpec((1, D), lambda i: (0, 0))
    outs = pl.pallas_call(
        body, name=name, grid=(S // ts,), in_specs=[row] + [vec] * n,
        out_specs=[_group_spec(d, ts, D) for _ in range(n) for d in dils],
        out_shape=[_group_shape(d, S, D, BF16) for _ in range(n) for d in dils],
        scratch_shapes=[_chunk_buf(ts, D)],
        compiler_params=_params("parallel"),
    )(x, *gains)
    return [[outs[q * nd + e].reshape(S, D) for e in range(nd)] for q in range(n)]


def _rms_bwd(x, cots, gains, dres, name, ts=256):
    S, D = x.shape
    ts = _tile(S, ts, 16 * max(d for _, _, d in cots))
    n = len(cots)
    ng = len(gains)

    def body(*refs):
        x_ref = refs[0]
        dh_refs = refs[1:1 + n]
        g_refs = refs[1 + n:1 + n + ng]
        dres_ref = refs[1 + n + ng]
        dx_ref, dxb_ref, gg_ref, buf = refs[2 + n + ng:6 + n + ng]
        i = pl.program_id(0)
        xv = x_ref[...]
        r = lax.rsqrt(jnp.mean(xv * xv, axis=-1, keepdims=True) + EPS)
        nrm = xv * r
        dn = jnp.zeros_like(xv)
        rows = [jnp.zeros((1, D), F32) for _ in range(ng)]
        for q, (_, gi, d) in enumerate(cots):
            if d == 1:
                dh = dh_refs[q][...].astype(F32)
            else:
                _store_token_order(buf, dh_refs[q], d)
                dh = _read_chunks(buf)
            dn = dn + dh * g_refs[gi][...]
            rows[gi] = rows[gi] + jnp.sum(dh * nrm, axis=0, keepdims=True)
        dx = dres_ref[...] + r * (dn - nrm * jnp.mean(dn * nrm, axis=-1, keepdims=True))
        dx_ref[...] = dx
        dxb_ref[...] = dx.astype(BF16)
        upd = _stack_rows(rows, 8)

        @pl.when(i == 0)
        def _():
            gg_ref[...] = upd

        @pl.when(i > 0)
        def _():
            gg_ref[...] += upd

    row = pl.BlockSpec((ts, D), lambda i: (i, 0))
    vec = pl.BlockSpec((1, D), lambda i: (0, 0))
    acc = pl.BlockSpec((8, D), lambda i: (0, 0))
    return pl.pallas_call(
        body, name=name, grid=(S // ts,),
        in_specs=[row] + [_group_spec(d, ts, D) for _, _, d in cots] + [vec] * ng + [row],
        out_specs=[row, row, acc],
        out_shape=[jax.ShapeDtypeStruct((S, D), F32), jax.ShapeDtypeStruct((S, D), BF16),
                   jax.ShapeDtypeStruct((8, D), F32)],
        scratch_shapes=[_chunk_buf(ts, D)],
        compiler_params=_params("arbitrary"),
    )(x, *[a if d == 1 else a.reshape(d, S // d, D) for a, _, d in cots], *gains, dres)


def _final_head(x2, gain, target, name, ts=256):
    S, D = x2.shape
    ts = _tile(S, ts, 16)

    def body(x_ref, g_ref, t_ref, loss_ref, dx_ref, dxb_ref, gg_ref):
        i = pl.program_id(0)
        xv = x_ref[...]
        g = g_ref[...]
        r = lax.rsqrt(jnp.mean(xv * xv, axis=-1, keepdims=True) + EPS)
        nrm = xv * r
        err = nrm * g - t_ref[...]
        part = 0.5 * jnp.sum(jnp.mean(err * err, axis=-1, keepdims=True), axis=0, keepdims=True)
        dout = err * (1.0 / D)
        dn = dout * g
        dx = r * (dn - nrm * jnp.mean(dn * nrm, axis=-1, keepdims=True))
        dx_ref[...] = dx
        dxb_ref[...] = dx.astype(BF16)
        upd = _stack_rows([jnp.sum(dout * nrm, axis=0, keepdims=True)], 8)
        lpart = jnp.broadcast_to(part, (1, LANES))

        @pl.when(i == 0)
        def _():
            gg_ref[...] = upd
            loss_ref[...] = lpart

        @pl.when(i > 0)
        def _():
            gg_ref[...] += upd
            loss_ref[...] += lpart

    row = pl.BlockSpec((ts, D), lambda i: (i, 0))
    vec = pl.BlockSpec((1, D), lambda i: (0, 0))
    return pl.pallas_call(
        body, name=name, grid=(S // ts,), in_specs=[row, vec, row],
        out_specs=[pl.BlockSpec((1, LANES), lambda i: (0, 0)), row, row, pl.BlockSpec((8, D), lambda i: (0, 0))],
        out_shape=[jax.ShapeDtypeStruct((1, LANES), F32), jax.ShapeDtypeStruct((S, D), F32),
                   jax.ShapeDtypeStruct((S, D), BF16), jax.ShapeDtypeStruct((8, D), F32)],
        compiler_params=_params("arbitrary"),
    )(x2, gain, target)


CONV_ROWS = 64


def _conv_fwd(proj3, conv_w32, conv_b, name, ts=256, cw=256):
    _, S, E = proj3.shape
    ts = _tile(S, ts, HALO)
    cw = _tile(E, cw)
    per = ts // HALO
    rc = min(CONV_ROWS, ts)

    def body(a_ref, b_ref, ap_ref, bp_ref, w_ref, cb_ref, c_ref, ubuf):
        i = pl.program_id(0)
        up = ap_ref[...].astype(F32) * _sigmoid(bp_ref[...].astype(F32))
        ubuf[0:HALO, :] = jnp.where(i > 0, up, 0.0)
        ubuf[HALO:HALO + ts, :] = a_ref[...].astype(F32) * _sigmoid(b_ref[...].astype(F32))
        for r0 in range(0, ts, rc):
            acc = jnp.broadcast_to(cb_ref[...], (rc, cw))
            for k in range(CONV_TAPS):
                off = r0 + HALO - (CONV_TAPS - 1) + k
                acc = acc + ubuf[off:off + rc, :] * w_ref[k:k + 1, :]
            c_ref[r0:r0 + rc, :] = acc

    return pl.pallas_call(
        body, name=name, grid=(S // ts, E // cw),
        in_specs=[
            pl.BlockSpec((None, ts, cw), lambda i, j: (0, i, j)),
            pl.BlockSpec((None, ts, cw), lambda i, j: (1, i, j)),
            pl.BlockSpec((None, HALO, cw), lambda i, j: (0, jnp.maximum(i * per - 1, 0), j)),
            pl.BlockSpec((None, HALO, cw), lambda i, j: (1, jnp.maximum(i * per - 1, 0), j)),
            pl.BlockSpec((HALO, cw), lambda i, j: (0, j)),
            pl.BlockSpec((1, cw), lambda i, j: (0, j)),
        ],
        out_specs=pl.BlockSpec((ts, cw), lambda i, j: (i, j)),
        out_shape=jax.ShapeDtypeStruct((S, E), F32),
        scratch_shapes=[pltpu.VMEM((HALO + ts, cw), F32)],
        compiler_params=_params("parallel", "parallel"),
    )(proj3, proj3, proj3, proj3, conv_w32, conv_b)


def _ln_gate_fwd(c, proj3, ln_g, ln_b, name, ts=256):
    S, E = c.shape
    ts = _tile(S, ts, 16)

    def body(c_ref, z_ref, g_ref, b_ref, y_ref):
        cv = c_ref[...]
        mu = jnp.mean(cv, axis=-1, keepdims=True)
        d = cv - mu
        var = jnp.mean(d * d, axis=-1, keepdims=True)
        cn = d * lax.rsqrt(var + EPS) * g_ref[...] + b_ref[...]
        z = z_ref[...].astype(F32)
        y_ref[...] = ((cn * _sigmoid(cn)).astype(F32) * (z * _sigmoid(z))).astype(BF16)

    row = pl.BlockSpec((ts, E), lambda i: (i, 0))
    vec = pl.BlockSpec((1, E), lambda i: (0, 0))
    return pl.pallas_call(
        body, name=name, grid=(S // ts,),
        in_specs=[row, pl.BlockSpec((None, ts, E), lambda i: (2, i, 0)), vec, vec],
        out_specs=row, out_shape=jax.ShapeDtypeStruct((S, E), BF16),
        compiler_params=_params("parallel"),
    )(c, proj3, ln_g, ln_b)


def _ln_gate_bwd(c, proj3, dy, ln_g, ln_b, name, ts=256):
    S, E = c.shape
    ts = _tile(S, ts, 16)

    def body(c_ref, z_ref, dy_ref, g_ref, b_ref, dc_ref, dz_ref, acc_ref):
        i = pl.program_id(0)
        cv = c_ref[...]
        g = g_ref[...]
        mu = jnp.mean(cv, axis=-1, keepdims=True)
        d = cv - mu
        var = jnp.mean(d * d, axis=-1, keepdims=True)
        rstd = lax.rsqrt(var + EPS)
        chat = d * rstd
        cn = chat * g + b_ref[...]
        z = z_ref[...].astype(F32)
        dyv = dy_ref[...].astype(F32)
        sc = _sigmoid(cn)
        sz = _sigmoid(z)
        dcn = dyv * (z * sz) * (sc * (1.0 + cn * (1.0 - sc)))
        dz_ref[...] = (dyv * (cn * sc) * (sz * (1.0 + z * (1.0 - sz)))).astype(BF16)
        dchat = dcn * g
        dcv = rstd * (dchat - jnp.mean(dchat, axis=-1, keepdims=True)
                      - chat * jnp.mean(dchat * chat, axis=-1, keepdims=True))
        dc_ref[...] = dcv
        upd = _stack_rows([jnp.sum(dcn * chat, axis=0, keepdims=True),
                           jnp.sum(dcn, axis=0, keepdims=True),
                           jnp.sum(dcv, axis=0, keepdims=True)], 8)

        @pl.when(i == 0)
        def _():
            acc_ref[...] = upd

        @pl.when(i > 0)
        def _():
            acc_ref[...] += upd

    row = pl.BlockSpec((ts, E), lambda i: (i, 0))
    vec = pl.BlockSpec((1, E), lambda i: (0, 0))
    return pl.pallas_call(
        body, name=name, grid=(S // ts,),
        in_specs=[row, pl.BlockSpec((None, ts, E), lambda i: (2, i, 0)), row, vec, vec],
        out_specs=[row, pl.BlockSpec((None, ts, E), lambda i: (2, i, 0)), pl.BlockSpec((8, E), lambda i: (0, 0))],
        out_shape=[jax.ShapeDtypeStruct((S, E), F32), jax.ShapeDtypeStruct((3, S, E), BF16),
                   jax.ShapeDtypeStruct((8, E), F32)],
        compiler_params=_params("arbitrary"),
    )(c, proj3, dy, ln_g, ln_b)


def _conv_bwd(proj3, dc, conv_w32, dproj3, name, ts=256, cw=256):
    _, S, E = proj3.shape
    ts = _tile(S, ts, HALO)
    cw = _tile(E, cw)
    per = ts // HALO
    n_i = S // ts
    last_halo = S // HALO - 1
    rc = min(CONV_ROWS, ts)

    def body(a_ref, b_ref, dc_ref, dcn_ref, w_ref, dp_in, dab_ref, dw_ref, dcbuf, ubuf, dwacc):
        del dp_in
        i = pl.program_id(1)
        dcbuf[0:ts, :] = dc_ref[...]
        dcbuf[ts:ts + HALO, :] = jnp.where(i < n_i - 1, dcn_ref[...], 0.0)
        av = a_ref[...].astype(F32)
        sb = _sigmoid(b_ref[...].astype(F32))
        ubuf[...] = av * sb

        @pl.when(i == 0)
        def _():
            dwacc[...] = jnp.zeros_like(dwacc)

        for r0 in range(0, ts, rc):
            uv = ubuf[r0:r0 + rc, :]
            du = jnp.zeros((rc, cw), F32)
            for d in range(CONV_TAPS):
                k = CONV_TAPS - 1 - d
                win = dcbuf[r0 + d:r0 + d + rc, :]
                du = du + win * w_ref[k:k + 1, :]
                dwacc[k:k + 1, :] += jnp.sum(uv * win, axis=0, keepdims=True)
            a_c = a_ref[r0:r0 + rc, :].astype(F32)
            s_c = _sigmoid(b_ref[r0:r0 + rc, :].astype(F32))
            dab_ref[0, r0:r0 + rc, :] = (du * s_c).astype(BF16)
            dab_ref[1, r0:r0 + rc, :] = (du * a_c * s_c * (1.0 - s_c)).astype(BF16)

        @pl.when(i == n_i - 1)
        def _():
            dw_ref[...] = dwacc[...]

    return pl.pallas_call(
        body, name=name, grid=(E // cw, n_i),
        in_specs=[
            pl.BlockSpec((None, ts, cw), lambda j, i: (0, i, j)),
            pl.BlockSpec((None, ts, cw), lambda j, i: (1, i, j)),
            pl.BlockSpec((ts, cw), lambda j, i: (i, j)),
            pl.BlockSpec((HALO, cw), lambda j, i: (jnp.minimum((i + 1) * per, last_halo), j)),
            pl.BlockSpec((HALO, cw), lambda j, i: (0, j)),
            ANY,
        ],
        out_specs=[pl.BlockSpec((2, ts, cw), lambda j, i: (0, i, j)),
                   pl.BlockSpec((HALO, cw), lambda j, i: (0, j))],
        out_shape=[jax.ShapeDtypeStruct((3, S, E), BF16), jax.ShapeDtypeStruct((HALO, E), F32)],
        scratch_shapes=[pltpu.VMEM((ts + HALO, cw), F32), pltpu.VMEM((ts, cw), F32), pltpu.VMEM((HALO, cw), F32)],
        input_output_aliases={5: 0},
        compiler_params=_params("parallel", "arbitrary"),
    )(proj3, proj3, dc, dc, conv_w32, dproj3)


def _bucket_table(dil):
    delta = (np.arange(BLOCK)[:, None] + BLOCK) - np.arange(2 * BLOCK)[None, :]
    dist = np.clip(delta, 0, None) * dil
    large = MAX_EXACT + (np.log(np.maximum(dist, 1).astype(np.float32) / MAX_EXACT)
                         / np.log(MAX_DISTANCE / MAX_EXACT) * (N_BUCKETS - MAX_EXACT)).astype(np.int32)
    large = np.minimum(large, N_BUCKETS - 1)
    return np.where(dist < MAX_EXACT, dist, large).astype(np.int32).reshape(-1)


def _onehot(dil):
    tbl = jnp.asarray(_bucket_table(dil))
    return (tbl[None, :] == jnp.arange(LANES, dtype=jnp.int32)[:, None]).astype(BF16)


def _split3(v):
    hi = v.astype(BF16)
    r1 = v - hi.astype(F32)
    mid = r1.astype(BF16)
    lo = (r1 - mid.astype(F32)).astype(BF16)
    return hi, mid, lo


def _bias_table(rb_t, onehot, name):
    H = rb_t.shape[0]
    N = onehot.shape[1]

    def body(r_ref, oh_ref, o_ref):
        oh = oh_ref[...]
        hi, mid, lo = _split3(r_ref[...])
        o_ref[...] = (_dot(lo, oh, NN) + _dot(mid, oh, NN)) + _dot(hi, oh, NN)

    return pl.pallas_call(
        body, name=name, in_specs=[VMEM_SPEC, VMEM_SPEC], out_specs=VMEM_SPEC,
        out_shape=jax.ShapeDtypeStruct((H, N), F32),
        compiler_params=pltpu.CompilerParams(vmem_limit_bytes=VMEM_LIMIT),
    )(rb_t, onehot)


def _bias_grad(dbs, onehots, name):
    H = dbs[0].shape[0]
    n = len(dbs)

    def body(*refs):
        acc = jnp.zeros((H, LANES), F32)
        for q in range(n):
            oh = refs[n + q][...]
            hi, mid, lo = _split3(refs[q][...])
            acc = acc + ((_dot(lo, oh, NT) + _dot(mid, oh, NT)) + _dot(hi, oh, NT))
        refs[2 * n][...] = acc

    return pl.pallas_call(
        body, name=name, in_specs=[VMEM_SPEC] * (2 * n), out_specs=VMEM_SPEC,
        out_shape=jax.ShapeDtypeStruct((H, LANES), F32),
        compiler_params=pltpu.CompilerParams(vmem_limit_bytes=VMEM_LIMIT),
    )(*dbs, *onehots)


def _attn_fwd(q, kv, bias, dil, name):
    S, A = q.shape
    H = A // HEAD_DIM
    L = S // dil
    nb = L // BLOCK
    qv = q.reshape(dil, L, A)
    kvv = kv.reshape(2, dil, L, A)

    def body(q_ref, kp_ref, kc_ref, vp_ref, vc_ref, b_ref, o_ref, lse_ref):
        i = pl.program_id(1)
        qi = lax.broadcasted_iota(jnp.int32, (BLOCK, BLOCK), 0)
        ki = lax.broadcasted_iota(jnp.int32, (BLOCK, BLOCK), 1)
        mask_c = ki <= qi
        mask_p = jnp.logical_and(ki >= qi, i > 0)
        lane = lax.broadcasted_iota(jnp.int32, (BLOCK, LANES), 1)
        lse_acc = jnp.zeros((BLOCK, LANES), F32)
        for h in range(H):
            sl = slice(h * HEAD_DIM, (h + 1) * HEAD_DIM)
            qh = q_ref[:, sl]
            s_c = jnp.where(mask_c, _dot(qh, kc_ref[:, sl], NT) * SCALE + b_ref[h, :, BLOCK:], NEG)
            s_p = jnp.where(mask_p, _dot(qh, kp_ref[:, sl], NT) * SCALE + b_ref[h, :, :BLOCK], NEG)
            m = jnp.maximum(jnp.max(s_c, axis=-1, keepdims=True), jnp.max(s_p, axis=-1, keepdims=True))
            p_c = jnp.exp(s_c - m)
            p_p = jnp.exp(s_p - m)
            den = jnp.sum(p_c, axis=-1, keepdims=True) + jnp.sum(p_p, axis=-1, keepdims=True)
            acc = _dot(p_c.astype(BF16), vc_ref[:, sl], NN) + _dot(p_p.astype(BF16), vp_ref[:, sl], NN)
            o_ref[:, sl] = acc / den
            lse_acc = jnp.where(lane == h, m + jnp.log(den), lse_acc)
        lse_ref[...] = lse_acc

    def blk(slab, prev):
        if prev:
            return pl.BlockSpec((None, None, BLOCK, A), lambda r, i: (slab, r, jnp.maximum(i - 1, 0), 0))
        return pl.BlockSpec((None, None, BLOCK, A), lambda r, i: (slab, r, i, 0))

    o, lse = pl.pallas_call(
        body, name=name, grid=(dil, nb),
        in_specs=[pl.BlockSpec((None, BLOCK, A), lambda r, i: (r, i, 0)),
                  blk(0, True), blk(0, False), blk(1, True), blk(1, False),
                  pl.BlockSpec((H, BLOCK, 2 * BLOCK), lambda r, i: (0, 0, 0))],
        out_specs=[pl.BlockSpec((None, BLOCK, A), lambda r, i: (r, i, 0)),
                   pl.BlockSpec((None, BLOCK, LANES), lambda r, i: (r, i, 0))],
        out_shape=[jax.ShapeDtypeStruct((dil, L, A), F32), jax.ShapeDtypeStruct((dil, L, LANES), F32)],
        compiler_params=_params("parallel", "parallel"),
    )(qv, kvv, kvv, kvv, kvv, bias)
    return o.reshape(S, A), lse.reshape(S, LANES)


def _attn_merge(os_, lses, z, dils, name, ts=256):
    S, A = z.shape
    H = A // HEAD_DIM
    ts = _tile(S, ts, 16 * max(dils))
    n = len(os_)

    def body(*refs):
        z_ref = refs[2 * n]
        y_ref, om_ref = refs[2 * n + 1:2 * n + 3]
        lse_refs = refs[2 * n + 3:3 * n + 3]
        o_refs = refs[3 * n + 3:4 * n + 3]
        l_bufs = refs[4 * n + 3:5 * n + 3]
        lse_buf = refs[5 * n + 3]
        ls = []
        for q, d in enumerate(dils):
            if d == 1:
                ls.append(refs[n + q][...])
            else:
                _store_token_order(o_refs[q], refs[q], d)
                _store_token_order(l_bufs[q], refs[n + q], d)
                ls.append(l_bufs[q][0])
        m = ls[0]
        for q in range(1, n):
            m = jnp.maximum(m, ls[q])
        es = [jnp.exp(v - m) for v in ls]
        den = es[0]
        for q in range(1, n):
            den = den + es[q]
        alphas = [e / den for e in es]
        lse = m + jnp.log(den)
        lse_buf[0] = lse
        for q, d in enumerate(dils):
            if d == 1:
                lse_refs[q][...] = lse
            else:
                _emit_group_order(lse_refs[q], lse_buf, d, F32)
        lane = lax.broadcasted_iota(jnp.int32, (ts, LANES), 1)
        for h in range(H):
            sl = slice(h * HEAD_DIM, (h + 1) * HEAD_DIM)
            om = jnp.zeros((ts, HEAD_DIM), F32)
            for q, d in enumerate(dils):
                o_h = refs[q][:, sl] if d == 1 else o_refs[q][h]
                om = om + _lane_col(alphas[q], h, lane) * o_h
            z = z_ref[:, sl].astype(F32)
            y_ref[:, sl] = (om * (z * _sigmoid(z))).astype(BF16)
            om_ref[:, sl] = om.astype(BF16)

    row = pl.BlockSpec((ts, A), lambda i: (i, 0))
    outs = pl.pallas_call(
        body, name=name, grid=(S // ts,),
        in_specs=[_group_spec(d, ts, A) for d in dils] + [_group_spec(d, ts, LANES) for d in dils] + [row],
        out_specs=[row, row] + [_group_spec(d, ts, LANES) for d in dils],
        out_shape=[jax.ShapeDtypeStruct((S, A), BF16), jax.ShapeDtypeStruct((S, A), BF16)]
        + [_group_shape(d, S, LANES, F32) for d in dils],
        scratch_shapes=[_chunk_buf(ts, A)] * n + [_chunk_buf(ts, LANES)] * (n + 1),
        compiler_params=_params("parallel"),
    )(*[o if d == 1 else o.reshape(d, S // d, A) for o, d in zip(os_, dils)],
      *[v if d == 1 else v.reshape(d, S // d, LANES) for v, d in zip(lses, dils)], z)
    return outs[0], outs[1], [v.reshape(S, LANES) for v in outs[2:]]


def _gate_bwd(dy, om, z, dils, name, ts=256):
    S, A = dy.shape
    H = A // HEAD_DIM
    ts = _tile(S, ts, 16 * max(dils))
    n = len(dils)

    def body(*refs):
        dy_ref, om_ref, z_ref = refs[:3]
        do_refs = refs[3:3 + n]
        dh_refs = refs[3 + n:3 + 2 * n]
        dz_ref = refs[3 + 2 * n]
        do_buf, dh_buf = refs[4 + 2 * n:6 + 2 * n]
        lane = lax.broadcasted_iota(jnp.int32, (ts, LANES), 1)
        acc = jnp.zeros((ts, LANES), F32)
        for h in range(H):
            sl = slice(h * HEAD_DIM, (h + 1) * HEAD_DIM)
            dyv = dy_ref[:, sl].astype(F32)
            omv = om_ref[:, sl].astype(F32)
            zv = z_ref[:, sl].astype(F32)
            sz = _sigmoid(zv)
            dob = (dyv * (zv * sz)).astype(BF16)
            do_buf[h] = dob.astype(F32)
            for q, d in enumerate(dils):
                if d == 1:
                    do_refs[q][:, sl] = dob
            dz_ref[:, sl] = (dyv * omv * (sz * (1.0 + zv * (1.0 - sz)))).astype(BF16)
            acc = jnp.where(lane == h, jnp.sum(dob.astype(F32) * omv, axis=-1, keepdims=True), acc)
        dh_buf[0] = acc
        for q, d in enumerate(dils):
            if d == 1:
                dh_refs[q][...] = acc
            else:
                _emit_group_order(do_refs[q], do_buf, d, BF16)
                _emit_group_order(dh_refs[q], dh_buf, d, F32)

    row = pl.BlockSpec((ts, A), lambda i: (i, 0))
    outs = pl.pallas_call(
        body, name=name, grid=(S // ts,), in_specs=[row, row, row],
        out_specs=[_group_spec(d, ts, A) for d in dils] + [_group_spec(d, ts, LANES) for d in dils] + [row],
        out_shape=[_group_shape(d, S, A, BF16) for d in dils] + [_group_shape(d, S, LANES, F32) for d in dils]
        + [jax.ShapeDtypeStruct((S, A), BF16)],
        scratch_shapes=[_chunk_buf(ts, A), _chunk_buf(ts, LANES)],
        compiler_params=_params("parallel"),
    )(dy, om, z)
    return ([v.reshape(S, A) for v in outs[:n]], [v.reshape(S, LANES) for v in outs[n:2 * n]], outs[2 * n])


def _attn_bwd(q, kv, do, lse, dh, bias, dil, name):
    S, A = q.shape
    H = A // HEAD_DIM
    L = S // dil
    nb = L // BLOCK
    qv = q.reshape(dil, L, A)
    kvv = kv.reshape(2, dil, L, A)
    dov = do.reshape(dil, L, A)
    lsev = lse.reshape(dil, L, LANES)
    dhv = dh.reshape(dil, L, LANES)

    def body(*refs):
        (q_ref, qn_ref, kp_ref, kc_ref, vp_ref, vc_ref, do_ref, don_ref, l_ref, ln_ref, d_ref, dn_ref,
         b_ref) = refs[:13]
        dq_ref, dkv_ref, db_ref = refs[13:16]
        r = pl.program_id(0)
        i = pl.program_id(1)
        qi = lax.broadcasted_iota(jnp.int32, (BLOCK, BLOCK), 0)
        ki = lax.broadcasted_iota(jnp.int32, (BLOCK, BLOCK), 1)
        mask_c = ki <= qi
        band = ki >= qi
        mask_p = jnp.logical_and(band, i > 0)
        mask_n = jnp.logical_and(band, i < nb - 1)
        lane = lax.broadcasted_iota(jnp.int32, (BLOCK, LANES), 1)

        @pl.when(jnp.logical_and(r == 0, i == 0))
        def _():
            db_ref[...] = jnp.zeros_like(db_ref)

        for h in range(H):
            sl = slice(h * HEAD_DIM, (h + 1) * HEAD_DIM)
            q_i, q_n = q_ref[:, sl], qn_ref[:, sl]
            k_p, k_c = kp_ref[:, sl], kc_ref[:, sl]
            v_p, v_c = vp_ref[:, sl], vc_ref[:, sl]
            do_i, do_n = do_ref[:, sl], don_ref[:, sl]
            l_i, l_n = _lane_col(l_ref[...], h, lane), _lane_col(ln_ref[...], h, lane)
            d_i, d_n = _lane_col(d_ref[...], h, lane), _lane_col(dn_ref[...], h, lane)
            b_c = b_ref[h, :, BLOCK:]
            b_p = b_ref[h, :, :BLOCK]
            s = jnp.where(mask_c, _dot(q_i, k_c, NT) * SCALE + b_c, NEG)
            p1 = jnp.exp(s - l_i)
            ds1 = p1 * (_dot(do_i, v_c, NT) - d_i)
            ds1b = ds1.astype(BF16)
            p1b = p1.astype(BF16)
            s = jnp.where(mask_p, _dot(q_i, k_p, NT) * SCALE + b_p, NEG)
            p2 = jnp.exp(s - l_i)
            ds2 = p2 * (_dot(do_i, v_p, NT) - d_i)
            ds2b = ds2.astype(BF16)
            s = jnp.where(mask_n, _dot(q_n, k_c, NT) * SCALE + b_p, NEG)
            p3 = jnp.exp(s - l_n)
            ds3b = (p3 * (_dot(do_n, v_c, NT) - d_n)).astype(BF16)
            p3b = p3.astype(BF16)
            dq = _dot(ds1b, k_c, NN) + _dot(ds2b, k_p, NN)
            dk = _dot(ds1b, q_i, TN) + _dot(ds3b, q_n, TN)
            dv = _dot(p1b, do_i, TN) + _dot(p3b, do_n, TN)
            dq_ref[:, sl] = (dq * SCALE).astype(BF16)
            dkv_ref[0, :, sl] = (dk * SCALE).astype(BF16)
            dkv_ref[1, :, sl] = dv.astype(BF16)
            db_ref[h, :, BLOCK:] += ds1
            db_ref[h, :, :BLOCK] += ds2

    def blk(slab, shift):
        if shift < 0:
            return pl.BlockSpec((None, None, BLOCK, A), lambda r, i: (slab, r, jnp.maximum(i - 1, 0), 0))
        return pl.BlockSpec((None, None, BLOCK, A), lambda r, i: (slab, r, i, 0))

    def row(width, shift):
        if shift > 0:
            return pl.BlockSpec((None, BLOCK, width), lambda r, i: (r, jnp.minimum(i + 1, nb - 1), 0))
        return pl.BlockSpec((None, BLOCK, width), lambda r, i: (r, i, 0))

    in_specs = [row(A, 0), row(A, 1), blk(0, -1), blk(0, 0), blk(1, -1), blk(1, 0),
                row(A, 0), row(A, 1), row(LANES, 0), row(LANES, 1), row(LANES, 0), row(LANES, 1),
                pl.BlockSpec((H, BLOCK, 2 * BLOCK), lambda r, i: (0, 0, 0))]
    dq, dkv, db = pl.pallas_call(
        body, name=name, grid=(dil, nb), in_specs=in_specs,
        out_specs=[pl.BlockSpec((None, BLOCK, A), lambda r, i: (r, i, 0)),
                   pl.BlockSpec((2, None, BLOCK, A), lambda r, i: (0, r, i, 0)),
                   pl.BlockSpec((H, BLOCK, 2 * BLOCK), lambda r, i: (0, 0, 0))],
        out_shape=[jax.ShapeDtypeStruct((dil, L, A), BF16), jax.ShapeDtypeStruct((2, dil, L, A), BF16),
                   jax.ShapeDtypeStruct((H, BLOCK, 2 * BLOCK), F32)],
        compiler_params=_params("arbitrary", "arbitrary"),
    )(qv, qv, kvv, kvv, kvv, kvv, dov, dov, lsev, lsev, dhv, dhv, bias)
    return dq.reshape(S, A), dkv.reshape(2, S, A), db


def _sum_leading(stack, out_dtype, name, tr=256, tc=2048):
    n, R, C = stack.shape
    tr = _tile(R, tr, 16)
    tc = _tile(C, tc)

    def body(s_ref, o_ref):
        acc = s_ref[0].astype(F32)
        for q in range(1, n):
            acc = acc + s_ref[q].astype(F32)
        o_ref[...] = acc.astype(out_dtype)

    return pl.pallas_call(
        body, name=name, grid=(R // tr, C // tc),
        in_specs=[pl.BlockSpec((n, tr, tc), lambda i, j: (0, i, j))],
        out_specs=pl.BlockSpec((tr, tc), lambda i, j: (i, j)),
        out_shape=jax.ShapeDtypeStruct((R, C), out_dtype),
        compiler_params=_params("parallel", "parallel"),
    )(stack)


def _add_half(g, t, c_idx, kind, name, tr=256, tc=2048):
    R, C = t.shape
    tr = _tile(R, tr, 16)
    tc = _tile(C, tc)
    nrb, ncb = R // tr, C // tc

    def body(c_ref, g_ref, t_ref, o_ref):
        del c_ref
        o_ref[...] = (g_ref[...].astype(F32) + t_ref[...].astype(F32)).astype(BF16)

    if kind == "col":
        g_map = lambda i, j, c_ref: (c_ref[0] * nrb + i, j)
    else:
        g_map = lambda i, j, c_ref: (i, c_ref[0] * ncb + j)
    same = lambda i, j, c_ref: (i, j)
    return pl.pallas_call(
        body, name=name,
        grid_spec=pltpu.PrefetchScalarGridSpec(
            num_scalar_prefetch=1, grid=(nrb, ncb),
            in_specs=[pl.BlockSpec((tr, tc), g_map), pl.BlockSpec((tr, tc), same)],
            out_specs=pl.BlockSpec((tr, tc), same)),
        out_shape=jax.ShapeDtypeStruct((R, C), BF16),
        compiler_params=_params("parallel", "parallel"),
    )(c_idx, g, t)


def _cast_into_full(w, kind, chip_idx, name, tr=256, tc=2048):
    R, C = w.shape
    tr = _tile(R, tr, 16)
    tc = _tile(C, tc)
    nrb, ncb = R // tr, C // tc

    def body(k_ref, w_ref, o_ref):
        del k_ref
        o_ref[...] = w_ref[...].astype(BF16)

    if kind == "col":
        o_map = lambda i, j, k_ref: (i, k_ref[0] * ncb + j)
        full = (R, N_CHIPS * C)
    else:
        o_map = lambda i, j, k_ref: (k_ref[0] * nrb + i, j)
        full = (N_CHIPS * R, C)
    return pl.pallas_call(
        body, name=name,
        grid_spec=pltpu.PrefetchScalarGridSpec(
            num_scalar_prefetch=1, grid=(nrb, ncb),
            in_specs=[pl.BlockSpec((tr, tc), lambda i, j, k_ref: (i, j))],
            out_specs=pl.BlockSpec((tr, tc), o_map)),
        out_shape=jax.ShapeDtypeStruct(full, BF16),
        compiler_params=_params("parallel", "parallel"),
    )(chip_idx, w)


def _sum_into_shard(p, u, idx, kind, name, tr=256, tc=2048):
    _, R, C = u.shape
    tr = _tile(R, tr, 16)
    tc = _tile(C, tc)
    nrb, ncb = R // tr, C // tc

    def body(i_ref, p_ref, u_ref, o_ref):
        del i_ref
        acc = p_ref[...].astype(F32)
        for q in range(N_CHIPS - 1):
            acc = acc + u_ref[q].astype(F32)
        o_ref[...] = acc

    if kind == "col":
        p_map = lambda i, j, r: (i, r[0] * ncb + j)
        o_map = lambda i, j, r: (r[1] * nrb + i, j)
        full = (2 * R, C)
    else:
        p_map = lambda i, j, r: (r[0] * nrb + i, j)
        o_map = lambda i, j, r: (i, r[1] * ncb + j)
        full = (R, 2 * C)
    return pl.pallas_call(
        body, name=name,
        grid_spec=pltpu.PrefetchScalarGridSpec(
            num_scalar_prefetch=1, grid=(nrb, ncb),
            in_specs=[pl.BlockSpec((tr, tc), p_map), pl.BlockSpec((N_CHIPS - 1, tr, tc), lambda i, j, r: (0, i, j))],
            out_specs=pl.BlockSpec((tr, tc), o_map)),
        out_shape=jax.ShapeDtypeStruct(full, F32),
        compiler_params=_params("parallel", "parallel"),
    )(idx, p, u)


def _adamw(w, g, m, v, name, tr=256, tc=2048):
    R, C = w.shape
    tr = _tile(R, tr, 8)
    tc = _tile(C, tc)
    c1 = 1.0 - ADAM_B1 ** ADAM_STEP
    c2 = 1.0 - ADAM_B2 ** ADAM_STEP

    def body(w_ref, g_ref, m_ref, v_ref, d_ref, nm_ref, nv_ref):
        gv = g_ref[...]
        nm = ADAM_B1 * m_ref[...] + (1.0 - ADAM_B1) * gv
        nv = ADAM_B2 * v_ref[...] + (1.0 - ADAM_B2) * (gv * gv)
        d_ref[...] = -ADAM_LR * ((nm / c1) / (jnp.sqrt(nv / c2) + ADAM_EPS) + ADAM_WD * w_ref[...])
        nm_ref[...] = nm
        nv_ref[...] = nv

    blk = pl.BlockSpec((tr, tc), lambda i, j: (i, j))
    sh = jax.ShapeDtypeStruct((R, C), F32)
    return pl.pallas_call(
        body, name=name, grid=(R // tr, C // tc), in_specs=[blk] * 4, out_specs=[blk] * 3,
        out_shape=[sh, sh, sh], compiler_params=_params("parallel", "parallel"),
    )(w, g, m, v)


def _xyc():
    return lax.axis_index("x"), lax.axis_index("y"), lax.axis_index("c")


def _drain(copies):
    for cp in copies:
        if cp.is_remote:
            cp.wait_send()
        else:
            cp.wait()


def _other_chips(x, y):
    return [(1 - x, y), (x, 1 - y), (1 - x, 1 - y)]


def _allgather_small(blk, name):
    R, C = blk.shape

    def body(x_ref, out_ref, send_sems, recv_sems, local_sem):
        x, y, c = _xyc()
        me = 4 * x + 2 * y + c
        mine = pltpu.make_async_copy(x_ref, out_ref.at[me], local_sem)
        mine.start()
        peers = []
        for k in range(1, N_DEV):
            px = 1 - x if (k >> 2) & 1 else x
            py = 1 - y if (k >> 1) & 1 else y
            pc = 1 - c if k & 1 else c
            peers.append((px, py, pc))
        sends = []
        for k, peer in enumerate(peers):
            cp = pltpu.make_async_remote_copy(
                src_ref=x_ref, dst_ref=out_ref.at[me], send_sem=send_sems.at[k], recv_sem=recv_sems.at[k],
                device_id=peer, device_id_type=MESH)
            cp.start()
            sends.append(cp)
        for k, (px, py, pc) in enumerate(peers):
            pltpu.make_async_remote_copy(
                src_ref=x_ref, dst_ref=out_ref.at[4 * px + 2 * py + pc], send_sem=send_sems.at[k],
                recv_sem=recv_sems.at[k], device_id=(px, py, pc), device_id_type=MESH).wait_recv()
        for cp in sends:
            cp.wait_send()
        mine.wait()

    return pl.pallas_call(
        body, name=name, in_specs=[VMEM_SPEC], out_specs=VMEM_SPEC,
        out_shape=jax.ShapeDtypeStruct((N_DEV, R, C), blk.dtype),
        scratch_shapes=[pltpu.SemaphoreType.DMA((N_DEV - 1,)), pltpu.SemaphoreType.DMA((N_DEV - 1,)),
                        pltpu.SemaphoreType.DMA],
        compiler_params=pltpu.CompilerParams(vmem_limit_bytes=VMEM_LIMIT),
    )(blk)


def _full_region(ref, kind, chip, half, shard_shape):
    r, cn = shard_shape
    hr = r // 2
    if kind == "col":
        rows = pl.ds(0, r) if half is None else pl.ds(pl.multiple_of(half * hr, 16), hr)
        return ref.at[rows, pl.ds(pl.multiple_of(chip * cn, LANES), cn)]
    if half is None:
        return ref.at[pl.ds(pl.multiple_of(chip * r, 16), r), :]
    return ref.at[pl.ds(pl.multiple_of(chip * r + half * hr, 16), hr), :]


def _allgather_weights(fulls, kinds, shapes, name):
    n = len(fulls)

    def body(*refs):
        outs = refs[n:2 * n]
        send_sems, recv_sems = refs[2 * n:]
        x, y, c = _xyc()
        chip = 2 * x + y
        sib = (x, y, 1 - c)
        others = _other_chips(x, y)
        started = []
        for w in range(n):
            mine = _full_region(outs[w], kinds[w], chip, c, shapes[w])
            for j, (ox, oy) in enumerate(others):
                cp = pltpu.make_async_remote_copy(
                    src_ref=mine, dst_ref=mine, send_sem=send_sems.at[6 * w + j], recv_sem=recv_sems.at[6 * w + j],
                    device_id=(ox, oy, c), device_id_type=MESH)
                cp.start()
                started.append(cp)
        for w in range(n):
            for j, (ox, oy) in enumerate(others):
                landed = _full_region(outs[w], kinds[w], 2 * ox + oy, c, shapes[w])
                pltpu.make_async_remote_copy(
                    src_ref=landed, dst_ref=landed, send_sem=send_sems.at[6 * w + j], recv_sem=recv_sems.at[6 * w + j],
                    device_id=(ox, oy, c), device_id_type=MESH).wait_recv()
                cp = pltpu.make_async_remote_copy(
                    src_ref=landed, dst_ref=landed, send_sem=send_sems.at[6 * w + 3 + j],
                    recv_sem=recv_sems.at[6 * w + 3 + j], device_id=sib, device_id_type=MESH)
                cp.start()
                started.append(cp)
        for w in range(n):
            for j, (ox, oy) in enumerate(others):
                theirs = _full_region(outs[w], kinds[w], 2 * ox + oy, 1 - c, shapes[w])
                pltpu.make_async_remote_copy(
                    src_ref=theirs, dst_ref=theirs, send_sem=send_sems.at[6 * w + 3 + j],
                    recv_sem=recv_sems.at[6 * w + 3 + j], device_id=sib, device_id_type=MESH).wait_recv()
        _drain(started)

    return pl.pallas_call(
        body, name=name, in_specs=[ANY] * n, out_specs=[ANY] * n,
        out_shape=[jax.ShapeDtypeStruct(f.shape, f.dtype) for f in fulls],
        input_output_aliases={w: w for w in range(n)},
        scratch_shapes=[pltpu.SemaphoreType.DMA((6 * n,)), pltpu.SemaphoreType.DMA((6 * n,))],
    )(*fulls)


def _region_of_size(ref, kind, shard_shape, count):
    r, cn = shard_shape
    if kind == "col":
        return ref.at[pl.ds(0, r // 2), pl.ds(0, count * cn)]
    return ref.at[pl.ds(0, count * (r // 2)), :]


def _allgather_weights_seq(fulls, kinds, shapes, after, name, collective_id):
    n = len(fulls)
    refs = [jax.new_ref(f, memory_space=pltpu.MemorySpace.HBM) for f in fulls]

    def body(after_ref, send_sems, recv_sems):
        del after_ref
        x, y, c = _xyc()
        chip = 2 * x + y
        sib = (x, y, 1 - c)
        others = _other_chips(x, y)
        peers = [(ox, oy, c) for ox, oy in others] + [sib]
        barrier = pltpu.get_barrier_semaphore()
        for peer in peers:
            pl.semaphore_signal(barrier, inc=1, device_id=peer, device_id_type=MESH)
        pl.semaphore_wait(barrier, len(peers))

        def copy(w, region, sem, to):
            return pltpu.make_async_remote_copy(src_ref=region, dst_ref=region, send_sem=send_sems.at[sem],
                                                recv_sem=recv_sems.at[sem], device_id=to, device_id_type=MESH)

        for w in range(n):
            mine = _full_region(refs[w], kinds[w], chip, c, shapes[w])
            for ox, oy in others:
                copy(w, mine, 2 * w, (ox, oy, c)).start()
        for w in range(n):
            three = _region_of_size(refs[w], kinds[w], shapes[w], 3)
            copy(w, three, 2 * w, sib).wait_recv()
            for ox, oy in others:
                copy(w, _full_region(refs[w], kinds[w], 2 * ox + oy, c, shapes[w]), 2 * w + 1, sib).start()
        for w in range(n):
            three = _region_of_size(refs[w], kinds[w], shapes[w], 3)
            copy(w, three, 2 * w + 1, sib).wait_recv()
            copy(w, three, 2 * w, sib).wait_send()
            copy(w, three, 2 * w + 1, sib).wait_send()

    pl.kernel(
        body, out_type=(), mesh=plsc.ScalarSubcoreMesh(axis_name="seq", num_cores=1), name=name,
        scratch_types=[pltpu.SemaphoreType.DMA((2 * n,)), pltpu.SemaphoreType.DMA((2 * n,))],
        compiler_params=pltpu.CompilerParams(collective_id=collective_id),
    )(after)
    return [r[...] for r in refs]


def _half_of(ref, kind, half):
    r, cn = ref.shape
    if kind == "col":
        return ref.at[pl.ds(pl.multiple_of(half * (r // 2), 16), r // 2), :]
    return ref.at[:, pl.ds(pl.multiple_of(half * (cn // 2), LANES), cn // 2)]


def _shard_of(ref, kind, chip):
    r, cn = ref.shape
    if kind == "col":
        return ref.at[:, pl.ds(pl.multiple_of(chip * (cn // N_CHIPS), LANES), cn // N_CHIPS)]
    return ref.at[pl.ds(pl.multiple_of(chip * (r // N_CHIPS), 16), r // N_CHIPS), :]


def _exchange_halves(grads, kinds, name):
    n = len(grads)

    def body(*refs):
        gs = refs[:n]
        ts = refs[n:2 * n]
        send_sems, recv_sems = refs[2 * n:]
        x, y, c = _xyc()
        cps = []
        for w in range(n):
            cp = pltpu.make_async_remote_copy(
                src_ref=_half_of(gs[w], kinds[w], 1 - c), dst_ref=ts[w], send_sem=send_sems.at[w],
                recv_sem=recv_sems.at[w], device_id=(x, y, 1 - c), device_id_type=MESH)
            cp.start()
            cps.append(cp)
        for cp in cps:
            cp.wait()

    out_shape = []
    for gr, kind in zip(grads, kinds):
        r, cn = gr.shape
        out_shape.append(jax.ShapeDtypeStruct((r // 2, cn) if kind == "col" else (r, cn // 2), gr.dtype))
    return pl.pallas_call(
        body, name=name, in_specs=[ANY] * n, out_specs=[ANY] * n, out_shape=out_shape,
        scratch_shapes=[pltpu.SemaphoreType.DMA((n,)), pltpu.SemaphoreType.DMA((n,))],
    )(*grads)


def _scatter_partials(parts, kinds, name):
    n = len(parts)

    def body(*refs):
        ps = refs[:n]
        us = refs[n:2 * n]
        send_sems, recv_sems = refs[2 * n:]
        x, y, c = _xyc()
        others = _other_chips(x, y)
        cps = []
        for w in range(n):
            for j, (ox, oy) in enumerate(others):
                cp = pltpu.make_async_remote_copy(
                    src_ref=_shard_of(ps[w], kinds[w], 2 * ox + oy), dst_ref=us[w].at[j],
                    send_sem=send_sems.at[3 * w + j], recv_sem=recv_sems.at[3 * w + j],
                    device_id=(ox, oy, c), device_id_type=MESH)
                cp.start()
                cps.append(cp)
        for cp in cps:
            cp.wait()

    out_shape = []
    for p, kind in zip(parts, kinds):
        r, cn = p.shape
        hs = (r, cn // N_CHIPS) if kind == "col" else (r // N_CHIPS, cn)
        out_shape.append(jax.ShapeDtypeStruct((N_CHIPS - 1,) + hs, p.dtype))
    return pl.pallas_call(
        body, name=name, in_specs=[ANY] * n, out_specs=[ANY] * n, out_shape=out_shape,
        scratch_shapes=[pltpu.SemaphoreType.DMA((3 * n,)), pltpu.SemaphoreType.DMA((3 * n,))],
    )(*parts)


def _join_halves(halves, kinds, name):
    n = len(halves)

    def body(*refs):
        outs = refs[n:2 * n]
        send_sems, recv_sems = refs[2 * n:]
        x, y, c = _xyc()
        cps = []
        for w in range(n):
            mine = _half_of(outs[w], kinds[w], c)
            cp = pltpu.make_async_remote_copy(
                src_ref=mine, dst_ref=mine, send_sem=send_sems.at[w], recv_sem=recv_sems.at[w],
                device_id=(x, y, 1 - c), device_id_type=MESH)
            cp.start()
            cps.append(cp)
        for w in range(n):
            theirs = _half_of(outs[w], kinds[w], 1 - c)
            pltpu.make_async_remote_copy(
                src_ref=theirs, dst_ref=theirs, send_sem=send_sems.at[w], recv_sem=recv_sems.at[w],
                device_id=(x, y, 1 - c), device_id_type=MESH).wait_recv()
        _drain(cps)

    return pl.pallas_call(
        body, name=name, in_specs=[ANY] * n, out_specs=[ANY] * n,
        out_shape=[jax.ShapeDtypeStruct(h.shape, h.dtype) for h in halves],
        input_output_aliases={w: w for w in range(n)},
        scratch_shapes=[pltpu.SemaphoreType.DMA((n,)), pltpu.SemaphoreType.DMA((n,))],
    )(*halves)


def kernel(x, a_norm, a_w_in, a_conv_w, a_conv_b, a_ln_g, a_ln_b, a_w_out, kv_norm, w_kv, b_norm, b_w_in, b_w_out, rel_bias, final_norm, loss_target, m_a_norm, m_a_w_in, m_a_conv_w, m_a_conv_b, m_a_ln_g, m_a_ln_b, m_a_w_out, m_kv_norm, m_w_kv, m_b_norm, m_b_w_in, m_b_w_out, m_rel_bias, m_final_norm, v_a_norm, v_a_w_in, v_a_conv_w, v_a_conv_b, v_a_ln_g, v_a_ln_b, v_a_w_out, v_kv_norm, v_w_kv, v_b_norm, v_b_w_in, v_b_w_out, v_rel_bias, v_final_norm):
    S, D = x.shape[1], x.shape[2]
    E = a_w_out.shape[1] * N_CHIPS
    A = b_w_out.shape[1] * N_CHIPS
    H = A // HEAD_DIM
    DC = D // N_CHIPS
    xs = x.reshape(S, D)
    tgt = loss_target.reshape(S, D)
    cx, cy, cc = _xyc()
    chip = 2 * cx + cy
    c_idx = jnp.reshape(cc, (1,)).astype(jnp.int32)

    big_names = ["a_w_in", "a_w_out", "w_kv", "b_w_in", "b_w_out"]
    kinds = ["col", "row", "col", "col", "row"]
    big_w = [a_w_in[0], a_w_out[0], w_kv, b_w_in[0], b_w_out[0]]
    big_m = [m_a_w_in[0], m_a_w_out[0], m_w_kv, m_b_w_in[0], m_b_w_out[0]]
    big_v = [v_a_w_in[0], v_a_w_out[0], v_w_kv, v_b_w_in[0], v_b_w_out[0]]
    chip_idx = jnp.reshape(chip, (1,)).astype(jnp.int32)
    placed = [_cast_into_full(big_w[w], kinds[w], chip_idx, "cast_" + big_names[w]) for w in range(5)]
    shard_shapes = [w.shape for w in big_w]
    (wa_in,) = _allgather_weights(placed[0:1], kinds[0:1], shard_shapes[0:1], "ag_a_w_in")
    wa_out, wkv = _allgather_weights_seq(placed[1:3], kinds[1:3], shard_shapes[1:3], wa_in, "ag_seq_a_out_kv", 1)
    wb_in, wb_out = _allgather_weights_seq(placed[3:5], kinds[3:5], shard_shapes[3:5], wkv, "ag_seq_b", 2)

    def row_at(vec, q):
        return jnp.pad(vec, ((q, 7 - q), (0, 0)))

    def pack_sharded(an, cw, cb, lg, lb):
        return jnp.concatenate([row_at(an, 0), jnp.pad(cw[0], ((0, 1), (0, 0))),
                                row_at(lg, 0) + row_at(lb, 1) + row_at(cb, 2)], axis=0)

    small_w = pack_sharded(a_norm, a_conv_w, a_conv_b, a_ln_g, a_ln_b)
    gathered = _allgather_small(small_w, "ag_small_params")
    small_full = jnp.concatenate([gathered[2 * k] for k in range(N_CHIPS)], axis=1)
    g_a = small_full[0:1]
    conv_w32 = small_full[8:8 + HALO]
    ln_g = small_full[40:41]
    ln_b = small_full[41:42]
    conv_b = small_full[42:43]
    g_kv = kv_norm.reshape(1, D)
    g_b = b_norm.reshape(1, D)
    g_f = final_norm.reshape(1, D)

    rb_t = jnp.pad(rel_bias.T, ((0, 0), (0, LANES - N_BUCKETS)))
    onehots = [_onehot(dil) for _, dil in GROUPS]
    biases = [_bias_table(rb_t, onehots[g], "bias_table_%d" % g).reshape(H, BLOCK, 2 * BLOCK)
              for g in range(len(GROUPS))]

    dils = tuple(dil for _, dil in GROUPS)
    assert dils[0] == 1
    n_g = len(GROUPS)
    ((h0,),) = _rms_fwd(xs, [g_a], (1,), "rms_a")
    proj3 = _matmul(h0, wa_in, "nn", BF16, "mm_a_in", out_slab=E)
    conv = _conv_fwd(proj3, conv_w32, conv_b, "conv_fwd")
    y_a = _ln_gate_fwd(conv, proj3, ln_g, ln_b, "ln_gate_fwd")
    x1 = _matmul(y_a, wa_out, "nn", F32, "mm_a_out", res=xs)
    hks, hbs = _rms_fwd(x1, [g_kv, g_b], dils, "rms_kv_b")
    kvs = [_matmul(hks[g], wkv, "nn", BF16, "mm_kv_%d" % g, out_slab=A, b_off=2 * g * A, n_cols=2 * A)
           for g in range(n_g)]
    qs = [_matmul(hbs[g], wb_in, "nn", BF16, "mm_q_%d" % g, b_off=g * A, n_cols=A) for g in range(n_g)]
    zb = _matmul(hbs[0], wb_in, "nn", BF16, "mm_zb", b_off=n_g * A, n_cols=A)
    os_, lses = [], []
    for g, dil in enumerate(dils):
        o_g, lse_g = _attn_fwd(qs[g], kvs[g], biases[g], dil, "attn_fwd_%d" % g)
        os_.append(o_g)
        lses.append(lse_g)
    y_b, o_m, lse_d = _attn_merge(os_, lses, zb, dils, "attn_merge")
    x2 = _matmul(y_b, wb_out, "nn", F32, "mm_b_out", res=x1)
    loss_part, dx2, dx2b, gg_f = _final_head(x2, g_f, tgt, "final_head")
    loss = lax.psum(loss_part[0, 0], ("x", "y", "c"))

    dw_tiles = dict(tm=512, tn=1024, tk=4096)
    dy_b = _matmul(dx2b, wb_out, "nt", BF16, "mm_b_out_dx")
    dwb_out = _matmul(y_b, dx2b, "tn", BF16, "mm_b_out_dw", **dw_tiles)
    dos, dhs, dzb = _gate_bwd(dy_b, o_m, zb, dils, "gate_bwd")
    dbs, cots = [], []
    dwb_in = dwkv = None
    for g, dil in enumerate(dils):
        dq, dkv, db = _attn_bwd(qs[g], kvs[g], dos[g], lse_d[g], dhs[g], biases[g], dil, "attn_bwd_%d" % g)
        dbs.append(db.reshape(H, BLOCK * 2 * BLOCK))
        dwb_in = _matmul(hbs[g], dq, "tn", BF16, "mm_q_dw_%d" % g, out_off=g * A, out_cols=(n_g + 1) * A,
                         out_alias=dwb_in, **dw_tiles)
        dwkv = _matmul(hks[g], dkv, "tn", BF16, "mm_kv_dw_%d" % g, b_slab=True, out_off=2 * g * A,
                       out_cols=2 * n_g * A, out_alias=dwkv, **dw_tiles)
        cots.append((_matmul(dkv, wkv, "nt", BF16, "mm_kv_dx_%d" % g, a_slab=True, b_off=2 * g * A), 0, dil))
        cots.append((_matmul(dq, wb_in, "nt", BF16, "mm_q_dx_%d" % g, b_off=g * A), 1, dil))
    dwb_in = _matmul(hbs[0], dzb, "tn", BF16, "mm_zb_dw", out_off=n_g * A, out_cols=(n_g + 1) * A,
                     out_alias=dwb_in, **dw_tiles)
    cots.append((_matmul(dzb, wb_in, "nt", BF16, "mm_zb_dx", b_off=n_g * A), 1, 1))
    g_rel_t = _bias_grad(dbs, onehots, "bias_grad")
    dx1, dx1b, gg_kvb = _rms_bwd(x1, cots, [g_kv, g_b], dx2, "rms_kv_b_bwd")
    dy_a = _matmul(dx1b, wa_out, "nt", BF16, "mm_a_out_dx")
    dwa_out = _matmul(y_a, dx1b, "tn", BF16, "mm_a_out_dw", **dw_tiles)
    dconv, dproj3, gg_ln = _ln_gate_bwd(conv, proj3, dy_a, ln_g, ln_b, "ln_gate_bwd")
    dproj3, g_conv_w = _conv_bwd(proj3, dconv, conv_w32, dproj3, "conv_bwd")
    dh0 = _matmul(dproj3, wa_in, "nt", BF16, "mm_a_in_dx", a_slab=True)
    dwa_in = _matmul(h0, dproj3, "tn", BF16, "mm_a_in_dw", b_slab=True, **dw_tiles)
    grad_x, _, gg_a = _rms_bwd(xs, [(dh0, 0, 1)], [g_a], dx1, "rms_a_bwd")

    full_grads = [dwa_in, dwa_out, dwkv, dwb_in, dwb_out]
    theirs = _exchange_halves(full_grads, kinds, "rs_exchange_halves")
    parts = [_add_half(full_grads[w], theirs[w], c_idx, kinds[w], "rs_add_half_%d" % w) for w in range(5)]
    slots = _scatter_partials(parts, kinds, "rs_scatter_partials")
    chip_c = jnp.stack([chip, cc]).astype(jnp.int32)
    halves = [_sum_into_shard(parts[w], slots[w], chip_c, kinds[w], "rs_sum_chips_%d" % w) for w in range(5)]
    big_g = _join_halves(halves, kinds, "rs_join_halves")

    def rel_rows(rb):
        return jnp.pad(rb.reshape(1, N_BUCKETS * H), ((0, 7), (0, D - N_BUCKETS * H)))

    small_g = jnp.concatenate([gg_a, g_conv_w, gg_ln, gg_kvb, gg_f, rel_rows(g_rel_t[:, :N_BUCKETS].T)], axis=0)
    small_sum = _sum_leading(_allgather_small(small_g, "ag_small_grads"), F32, "sum_small_grads", tr=72)
    g_sharded = lax.dynamic_slice(small_sum, (0, chip * DC), (48, DC))
    g_repl = small_sum[48:72]

    outs_g, outs_d, outs_m, outs_v = {}, {}, {}, {}
    for w, nm in enumerate(big_names):
        d_, m_, v_ = _adamw(big_w[w], big_g[w], big_m[w], big_v[w], "adamw_" + nm)
        outs_g[nm], outs_d[nm], outs_m[nm], outs_v[nm] = big_g[w], d_, m_, v_
    sm_m = pack_sharded(m_a_norm, m_a_conv_w, m_a_conv_b, m_a_ln_g, m_a_ln_b)
    sm_v = pack_sharded(v_a_norm, v_a_conv_w, v_a_conv_b, v_a_ln_g, v_a_ln_b)
    sd, smm, svv = _adamw(small_w, g_sharded, sm_m, sm_v, "adamw_small_sharded")

    def unpack_sharded(p):
        return {"a_norm": p[0:1], "a_conv_w": p[8:8 + CONV_TAPS].reshape(1, CONV_TAPS, DC), "a_ln_g": p[40:41],
                "a_ln_b": p[41:42], "a_conv_b": p[42:43]}

    for src, dst in ((g_sharded, outs_g), (sd, outs_d), (smm, outs_m), (svv, outs_v)):
        dst.update(unpack_sharded(src))

    def pack_repl(kn, bn, fn, rb):
        return jnp.concatenate([row_at(kn.reshape(1, D), 0) + row_at(bn.reshape(1, D), 1),
                                row_at(fn.reshape(1, D), 0), rel_rows(rb)], axis=0)

    rp_w = pack_repl(kv_norm, b_norm, final_norm, rel_bias)
    rp_m = pack_repl(m_kv_norm, m_b_norm, m_final_norm, m_rel_bias)
    rp_v = pack_repl(v_kv_norm, v_b_norm, v_final_norm, v_rel_bias)
    rd, rmm, rvv = _adamw(rp_w, g_repl, rp_m, rp_v, "adamw_small_replicated")

    def unpack_repl(p):
        return {"kv_norm": p[0], "b_norm": p[1:2], "final_norm": p[8],
                "rel_bias": p[16, :N_BUCKETS * H].reshape(N_BUCKETS, H)}

    for src, dst in ((g_repl, outs_g), (rd, outs_d), (rmm, outs_m), (rvv, outs_v)):
        dst.update(unpack_repl(src))

    order = ["a_norm", "a_w_in", "a_conv_w", "a_conv_b", "a_ln_g", "a_ln_b", "a_w_out", "kv_norm", "w_kv",
             "b_norm", "b_w_in", "b_w_out", "rel_bias", "final_norm"]
    lead = {"a_w_in", "a_w_out", "b_w_in", "b_w_out"}

    def shaped(nm, val):
        return val[None] if nm in lead else val

    result = [loss, grad_x.reshape(1, S, D)]
    for table in (outs_g, outs_d, outs_m, outs_v):
        result.extend(shaped(nm, table[nm]) for nm in order)
    return tuple(result)
```

```python
import functools

import numpy as np
import jax
import jax.numpy as jnp
from jax import lax
from jax.experimental import pallas as pl
from jax.experimental.pallas import tpu as pltpu
from jax.experimental.pallas import tpu_sc as plsc

F32 = jnp.float32
BF16 = jnp.bfloat16
MESH = pl.DeviceIdType.MESH
ANY = pl.BlockSpec(memory_space=pl.ANY)
VMEM_SPEC = pl.BlockSpec(memory_space=pltpu.VMEM)

EPS = 1e-6
HEAD_DIM = 128
BLOCK = 128
GROUPS = ((128, 1), (512, 4), (2048, 16))
SCALE = HEAD_DIM ** -0.5
CONV_TAPS = 31
HALO = 32
N_BUCKETS = 32
MAX_EXACT = 16
MAX_DISTANCE = 2048
NEG = -1e30
N_CHIPS = 4
N_DEV = 8
LANES = 128
VMEM_LIMIT = 56 * 1024 * 1024

ADAM_LR = 0.001
ADAM_B1 = 0.9
ADAM_B2 = 0.999
ADAM_EPS = 1e-08
ADAM_WD = 0.01
ADAM_STEP = 10


def _tile(n, pref, mult=LANES):
    t = (min(pref, n) // mult) * mult
    while t >= mult:
        if n % t == 0:
            return t
        t -= mult
    return n


def _params(*sem):
    return pltpu.CompilerParams(dimension_semantics=sem, vmem_limit_bytes=VMEM_LIMIT)


def _sigmoid(v):
    return 1.0 / (1.0 + jnp.exp(-v))


def _dot(a, b, dims):
    return lax.dot_general(a, b, (dims, ((), ())), preferred_element_type=F32)


NN = ((1,), (0,))
NT = ((1,), (1,))
TN = ((0,), (0,))


def _stack_rows(rows, total):
    width = rows[0].shape[1]
    rid = lax.broadcasted_iota(jnp.int32, (total, width), 0)
    out = jnp.zeros((total, width), F32)
    for q, row in enumerate(rows):
        out = jnp.where(rid == q, jnp.broadcast_to(row, (total, width)), out)
    return out


def _lane_col(arr, h, lane):
    return jnp.sum(jnp.where(lane == h, arr, 0.0), axis=-1, keepdims=True)


def _matmul(a, b, mode, out_dtype, name, res=None, a_slab=False, b_slab=False, out_slab=0,
            b_off=0, n_cols=None, out_off=0, out_cols=None, out_alias=None, tm=512, tn=1024, tk=2048):
    if a_slab:
        na, M, W = a.shape
        K = na * W
    elif mode == "tn":
        K, M = a.shape
    else:
        M, K = a.shape
    if b_slab:
        nbs, _, Wb = b.shape
        N = nbs * Wb
    elif mode == "nt":
        N = b.shape[0]
    else:
        N = n_cols if n_cols else b.shape[1]
    tm = _tile(M, tm)
    tn = _tile(Wb if b_slab else (out_slab if out_slab else N), tn)
    tk = _tile(W if a_slab else K, tk)
    nk = K // tk
    grid = (M // tm, N // tn, nk)
    bo = b_off // (tk if mode == "nt" else tn)
    oo = out_off // tn

    if a_slab:
        per = W // tk
        a_spec = pl.BlockSpec((None, tm, tk), lambda i, j, k: (k // per, i, k % per))
    elif mode == "tn":
        a_spec = pl.BlockSpec((tk, tm), lambda i, j, k: (k, i))
    else:
        a_spec = pl.BlockSpec((tm, tk), lambda i, j, k: (i, k))
    if b_slab:
        perb = Wb // tn
        b_spec = pl.BlockSpec((None, tk, tn), lambda i, j, k: (j // perb, k, j % perb))
    elif mode == "nt":
        b_spec = pl.BlockSpec((tn, tk), lambda i, j, k: (j, k + bo))
    else:
        b_spec = pl.BlockSpec((tk, tn), lambda i, j, k: (k, j + bo))
    if out_slab:
        pero = out_slab // tn
        o_spec = pl.BlockSpec((None, tm, tn), lambda i, j, k: (j // pero, i, j % pero))
        out_shape = jax.ShapeDtypeStruct((N // out_slab, M, out_slab), out_dtype)
    else:
        o_spec = pl.BlockSpec((tm, tn), lambda i, j, k: (i, j + oo))
        out_shape = jax.ShapeDtypeStruct((M, out_cols if out_cols else N), out_dtype)
    in_specs = [a_spec, b_spec]
    operands = [a, b]
    if res is not None:
        in_specs.append(pl.BlockSpec((tm, tn), lambda i, j, k: (i, j)))
        operands.append(res)
    aliases = {}
    if out_alias is not None:
        aliases[len(operands)] = 0
        in_specs.append(ANY)
        operands.append(out_alias)
    dims = {"nn": NN, "nt": NT, "tn": TN}[mode]
    has_res = res is not None
    n_in = len(operands)

    def body(*refs):
        a_ref, b_ref = refs[0], refs[1]
        r_ref = refs[2] if has_res else None
        o_ref = refs[n_in]
        prod = _dot(a_ref[...], b_ref[...], dims)

        def finish(val):
            if has_res:
                val = val + r_ref[...]
            o_ref[...] = val.astype(out_dtype)

        if nk == 1:
            finish(prod)
        else:
            acc_ref = refs[n_in + 1]
            k = pl.program_id(2)

            @pl.when(k == 0)
            def _():
                acc_ref[...] = prod

            @pl.when(k > 0)
            def _():
                acc_ref[...] += prod

            @pl.when(k == nk - 1)
            def _():
                finish(acc_ref[...])

    scratch = [pltpu.VMEM((tm, tn), F32)] if nk > 1 else []
    return pl.pallas_call(
        body, name=name, grid=grid, in_specs=in_specs, out_specs=o_spec, out_shape=out_shape,
        scratch_shapes=scratch, input_output_aliases=aliases,
        compiler_params=_params("parallel", "parallel", "arbitrary"),
    )(*operands)


def _group_spec(d, ts, width):
    if d == 1:
        return pl.BlockSpec((ts, width), lambda i: (i, 0))
    return pl.BlockSpec((d, ts // d, width), lambda i: (0, i, 0))


def _group_shape(d, S, width, dtype):
    return jax.ShapeDtypeStruct((S, width) if d == 1 else (d, S // d, width), dtype)


def _chunk_buf(ts, width):
    return pltpu.VMEM((width // LANES, ts, LANES), F32)


def _fill_chunks(buf, val):
    for c in range(buf.shape[0]):
        buf[c] = val[:, c * LANES:(c + 1) * LANES]


def _read_chunks(buf):
    return jnp.concatenate([buf[c] for c in range(buf.shape[0])], axis=1)


def _emit_group_order(o_ref, buf, d, dtype):
    n = buf.shape[1] // d
    for r in range(d):
        for c in range(buf.shape[0]):
            o_ref[r, :, c * LANES:(c + 1) * LANES] = buf[c, pl.ds(r, n, stride=d), :].astype(dtype)


def _store_token_order(buf, i_ref, d):
    n = buf.shape[1] // d
    for r in range(d):
        for c in range(buf.shape[0]):
            buf[c, pl.ds(r, n, stride=d), :] = i_ref[r, :, c * LANES:(c + 1) * LANES].astype(F32)


def _rms_fwd(x, gains, dils, name, ts=256):
    S, D = x.shape
    ts = _tile(S, ts, 16 * max(dils))
    n = len(gains)
    nd = len(dils)

    def body(*refs):
        buf = refs[1 + n + n * nd]
        xv = refs[0][...]
        nrm = xv * lax.rsqrt(jnp.mean(xv * xv, axis=-1, keepdims=True) + EPS)
        for q in range(n):
            val = nrm * refs[1 + q][...]
            if max(dils) > 1:
                _fill_chunks(buf, val)
            for e, d in enumerate(dils):
                if d == 1:
                    refs[1 + n + q * nd + e][...] = val.astype(BF16)
                else:
                    _emit_group_order(refs[1 + n + q * nd + e], buf, d, BF16)

    row = pl.BlockSpec((ts, D), lambda i: (i, 0))
    vec = pl.BlockSpec((1, D), lambda i: (0, 0))
    outs = pl.pallas_call(
        body, name=name, grid=(S // ts,), in_specs=[row] + [vec] * n,
        out_specs=[_group_spec(d, ts, D) for _ in range(n) for d in dils],
        out_shape=[_group_shape(d, S, D, BF16) for _ in range(n) for d in dils],
        scratch_shapes=[_chunk_buf(ts, D)],
        compiler_params=_params("parallel"),
    )(x, *gains)
    return [[outs[q * nd + e].reshape(S, D) for e in range(nd)] for q in range(n)]


def _rms_bwd(x, cots, gains, dres, name, ts=256):
    S, D = x.shape
    ts = _tile(S, ts, 16 * max(d for _, _, d in cots))
    n = len(cots)
    ng = len(gains)

    def body(*refs):
        x_ref = refs[0]
        dh_refs = refs[1:1 + n]
        g_refs = refs[1 + n:1 + n + ng]
        dres_ref = refs[1 + n + ng]
        dx_ref, dxb_ref, gg_ref, buf = refs[2 + n + ng:6 + n + ng]
        i = pl.program_id(0)
        xv = x_ref[...]
        r = lax.rsqrt(jnp.mean(xv * xv, axis=-1, keepdims=True) + EPS)
        nrm = xv * r
        dn = jnp.zeros_like(xv)
        rows = [jnp.zeros((1, D), F32) for _ in range(ng)]
        for q, (_, gi, d) in enumerate(cots):
            if d == 1:
                dh = dh_refs[q][...].astype(F32)
            else:
                _store_token_order(buf, dh_refs[q], d)
                dh = _read_chunks(buf)
            dn = dn + dh * g_refs[gi][...]
            rows[gi] = rows[gi] + jnp.sum(dh * nrm, axis=0, keepdims=True)
        dx = dres_ref[...] + r * (dn - nrm * jnp.mean(dn * nrm, axis=-1, keepdims=True))
        dx_ref[...] = dx
        dxb_ref[...] = dx.astype(BF16)
        upd = _stack_rows(rows, 8)

        @pl.when(i == 0)
        def _():
            gg_ref[...] = upd

        @pl.when(i > 0)
        def _():
            gg_ref[...] += upd

    row = pl.BlockSpec((ts, D), lambda i: (i, 0))
    vec = pl.BlockSpec((1, D), lambda i: (0, 0))
    acc = pl.BlockSpec((8, D), lambda i: (0, 0))
    return pl.pallas_call(
        body, name=name, grid=(S // ts,),
        in_specs=[row] + [_group_spec(d, ts, D) for _, _, d in cots] + [vec] * ng + [row],
        out_specs=[row, row, acc],
        out_shape=[jax.ShapeDtypeStruct((S, D), F32), jax.ShapeDtypeStruct((S, D), BF16),
                   jax.ShapeDtypeStruct((8, D), F32)],
        scratch_shapes=[_chunk_buf(ts, D)],
        compiler_params=_params("arbitrary"),
    )(x, *[a if d == 1 else a.reshape(d, S // d, D) for a, _, d in cots], *gains, dres)


def _final_head(x2, gain, target, name, ts=256):
    S, D = x2.shape
    ts = _tile(S, ts, 16)

    def body(x_ref, g_ref, t_ref, loss_ref, dx_ref, dxb_ref, gg_ref):
        i = pl.program_id(0)
        xv = x_ref[...]
        g = g_ref[...]
        r = lax.rsqrt(jnp.mean(xv * xv, axis=-1, keepdims=True) + EPS)
        nrm = xv * r
        err = nrm * g - t_ref[...]
        part = 0.5 * jnp.sum(jnp.mean(err * err, axis=-1, keepdims=True), axis=0, keepdims=True)
        dout = err * (1.0 / D)
        dn = dout * g
        dx = r * (dn - nrm * jnp.mean(dn * nrm, axis=-1, keepdims=True))
        dx_ref[...] = dx
        dxb_ref[...] = dx.astype(BF16)
        upd = _stack_rows([jnp.sum(dout * nrm, axis=0, keepdims=True)], 8)
        lpart = jnp.broadcast_to(part, (1, LANES))

        @pl.when(i == 0)
        def _():
            gg_ref[...] = upd
            loss_ref[...] = lpart

        @pl.when(i > 0)
        def _():
            gg_ref[...] += upd
            loss_ref[...] += lpart

    row = pl.BlockSpec((ts, D), lambda i: (i, 0))
    vec = pl.BlockSpec((1, D), lambda i: (0, 0))
    return pl.pallas_call(
        body, name=name, grid=(S // ts,), in_specs=[row, vec, row],
        out_specs=[pl.BlockSpec((1, LANES), lambda i: (0, 0)), row, row, pl.BlockSpec((8, D), lambda i: (0, 0))],
        out_shape=[jax.ShapeDtypeStruct((1, LANES), F32), jax.ShapeDtypeStruct((S, D), F32),
                   jax.ShapeDtypeStruct((S, D), BF16), jax.ShapeDtypeStruct((8, D), F32)],
        compiler_params=_params("arbitrary"),
    )(x2, gain, target)


CONV_ROWS = 64


def _conv_fwd(proj3, conv_w32, conv_b, name, ts=256, cw=256):
    _, S, E = proj3.shape
    ts = _tile(S, ts, HALO)
    cw = _tile(E, cw)
    per = ts // HALO
    rc = min(CONV_ROWS, ts)

    def body(a_ref, b_ref, ap_ref, bp_ref, w_ref, cb_ref, c_ref, ubuf):
        i = pl.program_id(0)
        up = ap_ref[...].astype(F32) * _sigmoid(bp_ref[...].astype(F32))
        ubuf[0:HALO, :] = jnp.where(i > 0, up, 0.0)
        ubuf[HALO:HALO + ts, :] = a_ref[...].astype(F32) * _sigmoid(b_ref[...].astype(F32))
        for r0 in range(0, ts, rc):
            acc = jnp.broadcast_to(cb_ref[...], (rc, cw))
            for k in range(CONV_TAPS):
                off = r0 + HALO - (CONV_TAPS - 1) + k
                acc = acc + ubuf[off:off + rc, :] * w_ref[k:k + 1, :]
            c_ref[r0:r0 + rc, :] = acc

    return pl.pallas_call(
        body, name=name, grid=(S // ts, E // cw),
        in_specs=[
            pl.BlockSpec((None, ts, cw), lambda i, j: (0, i, j)),
            pl.BlockSpec((None, ts, cw), lambda i, j: (1, i, j)),
            pl.BlockSpec((None, HALO, cw), lambda i, j: (0, jnp.maximum(i * per - 1, 0), j)),
            pl.BlockSpec((None, HALO, cw), lambda i, j: (1, jnp.maximum(i * per - 1, 0), j)),
            pl.BlockSpec((HALO, cw), lambda i, j: (0, j)),
            pl.BlockSpec((1, cw), lambda i, j: (0, j)),
        ],
        out_specs=pl.BlockSpec((ts, cw), lambda i, j: (i, j)),
        out_shape=jax.ShapeDtypeStruct((S, E), F32),
        scratch_shapes=[pltpu.VMEM((HALO + ts, cw), F32)],
        compiler_params=_params("parallel", "parallel"),
    )(proj3, proj3, proj3, proj3, conv_w32, conv_b)


def _ln_gate_fwd(c, proj3, ln_g, ln_b, name, ts=256):
    S, E = c.shape
    ts = _tile(S, ts, 16)

    def body(c_ref, z_ref, g_ref, b_ref, y_ref):
        cv = c_ref[...]
        mu = jnp.mean(cv, axis=-1, keepdims=True)
        d = cv - mu
        var = jnp.mean(d * d, axis=-1, keepdims=True)
        cn = d * lax.rsqrt(var + EPS) * g_ref[...] + b_ref[...]
        z = z_ref[...].astype(F32)
        y_ref[...] = ((cn * _sigmoid(cn)).astype(F32) * (z * _sigmoid(z))).astype(BF16)

    row = pl.BlockSpec((ts, E), lambda i: (i, 0))
    vec = pl.BlockSpec((1, E), lambda i: (0, 0))
    return pl.pallas_call(
        body, name=name, grid=(S // ts,),
        in_specs=[row, pl.BlockSpec((None, ts, E), lambda i: (2, i, 0)), vec, vec],
        out_specs=row, out_shape=jax.ShapeDtypeStruct((S, E), BF16),
        compiler_params=_params("parallel"),
    )(c, proj3, ln_g, ln_b)


def _ln_gate_bwd(c, proj3, dy, ln_g, ln_b, name, ts=256):
    S, E = c.shape
    ts = _tile(S, ts, 16)

    def body(c_ref, z_ref, dy_ref, g_ref, b_ref, dc_ref, dz_ref, acc_ref):
        i = pl.program_id(0)
        cv = c_ref[...]
        g = g_ref[...]
        mu = jnp.mean(cv, axis=-1, keepdims=True)
        d = cv - mu
        var = jnp.mean(d * d, axis=-1, keepdims=True)
        rstd = lax.rsqrt(var + EPS)
        chat = d * rstd
        cn = chat * g + b_ref[...]
        z = z_ref[...].astype(F32)
        dyv = dy_ref[...].astype(F32)
        sc = _sigmoid(cn)
        sz = _sigmoid(z)
        dcn = dyv * (z * sz) * (sc * (1.0 + cn * (1.0 - sc)))
        dz_ref[...] = (dyv * (cn * sc) * (sz * (1.0 + z * (1.0 - sz)))).astype(BF16)
        dchat = dcn * g
        dcv = rstd * (dchat - jnp.mean(dchat, axis=-1, keepdims=True)
                      - chat * jnp.mean(dchat * chat, axis=-1, keepdims=True))
        dc_ref[...] = dcv
        upd = _stack_rows([jnp.sum(dcn * chat, axis=0, keepdims=True),
                           jnp.sum(dcn, axis=0, keepdims=True),
                           jnp.sum(dcv, axis=0, keepdims=True)], 8)

        @pl.when(i == 0)
        def _():
            acc_ref[...] = upd

        @pl.when(i > 0)
        def _():
            acc_ref[...] += upd

    row = pl.BlockSpec((ts, E), lambda i: (i, 0))
    vec = pl.BlockSpec((1, E), lambda i: (0, 0))
    return pl.pallas_call(
        body, name=name, grid=(S // ts,),
        in_specs=[row, pl.BlockSpec((None, ts, E), lambda i: (2, i, 0)), row, vec, vec],
        out_specs=[row, pl.BlockSpec((None, ts, E), lambda i: (2, i, 0)), pl.BlockSpec((8, E), lambda i: (0, 0))],
        out_shape=[jax.ShapeDtypeStruct((S, E), F32), jax.ShapeDtypeStruct((3, S, E), BF16),
                   jax.ShapeDtypeStruct((8, E), F32)],
        compiler_params=_params("arbitrary"),
    )(c, proj3, dy, ln_g, ln_b)


def _conv_bwd(proj3, dc, conv_w32, dproj3, name, ts=256, cw=256):
    _, S, E = proj3.shape
    ts = _tile(S, ts, HALO)
    cw = _tile(E, cw)
    per = ts // HALO
    n_i = S // ts
    last_halo = S // HALO - 1
    rc = min(CONV_ROWS, ts)

    def body(a_ref, b_ref, dc_ref, dcn_ref, w_ref, dp_in, dab_ref, dw_ref, dcbuf, ubuf, dwacc):
        del dp_in
        i = pl.program_id(1)
        dcbuf[0:ts, :] = dc_ref[...]
        dcbuf[ts:ts + HALO, :] = jnp.where(i < n_i - 1, dcn_ref[...], 0.0)
        av = a_ref[...].astype(F32)
        sb = _sigmoid(b_ref[...].astype(F32))
        ubuf[...] = av * sb

        @pl.when(i == 0)
        def _():
            dwacc[...] = jnp.zeros_like(dwacc)

        for r0 in range(0, ts, rc):
            uv = ubuf[r0:r0 + rc, :]
            du = jnp.zeros((rc, cw), F32)
            for d in range(CONV_TAPS):
                k = CONV_TAPS - 1 - d
                win = dcbuf[r0 + d:r0 + d + rc, :]
                du = du + win * w_ref[k:k + 1, :]
                dwacc[k:k + 1, :] += jnp.sum(uv * win, axis=0, keepdims=True)
            a_c = a_ref[r0:r0 + rc, :].astype(F32)
            s_c = _sigmoid(b_ref[r0:r0 + rc, :].astype(F32))
            dab_ref[0, r0:r0 + rc, :] = (du * s_c).astype(BF16)
            dab_ref[1, r0:r0 + rc, :] = (du * a_c * s_c * (1.0 - s_c)).astype(BF16)

        @pl.when(i == n_i - 1)
        def _():
            dw_ref[...] = dwacc[...]

    return pl.pallas_call(
        body, name=name, grid=(E // cw, n_i),
        in_specs=[
            pl.BlockSpec((None, ts, cw), lambda j, i: (0, i, j)),
            pl.BlockSpec((None, ts, cw), lambda j, i: (1, i, j)),
            pl.BlockSpec((ts, cw), lambda j, i: (i, j)),
            pl.BlockSpec((HALO, cw), lambda j, i: (jnp.minimum((i + 1) * per, last_halo), j)),
            pl.BlockSpec((HALO, cw), lambda j, i: (0, j)),
            ANY,
        ],
        out_specs=[pl.BlockSpec((2, ts, cw), lambda j, i: (0, i, j)),
                   pl.BlockSpec((HALO, cw), lambda j, i: (0, j))],
        out_shape=[jax.ShapeDtypeStruct((3, S, E), BF16), jax.ShapeDtypeStruct((HALO, E), F32)],
        scratch_shapes=[pltpu.VMEM((ts + HALO, cw), F32), pltpu.VMEM((ts, cw), F32), pltpu.VMEM((HALO, cw), F32)],
        input_output_aliases={5: 0},
        compiler_params=_params("parallel", "arbitrary"),
    )(proj3, proj3, dc, dc, conv_w32, dproj3)


def _bucket_table(dil):
    delta = (np.arange(BLOCK)[:, None] + BLOCK) - np.arange(2 * BLOCK)[None, :]
    dist = np.clip(delta, 0, None) * dil
    large = MAX_EXACT + (np.log(np.maximum(dist, 1).astype(np.float32) / MAX_EXACT)
                         / np.log(MAX_DISTANCE / MAX_EXACT) * (N_BUCKETS - MAX_EXACT)).astype(np.int32)
    large = np.minimum(large, N_BUCKETS - 1)
    return np.where(dist < MAX_EXACT, dist, large).astype(np.int32).reshape(-1)


def _onehot(dil):
    tbl = jnp.asarray(_bucket_table(dil))
    return (tbl[None, :] == jnp.arange(LANES, dtype=jnp.int32)[:, None]).astype(BF16)


def _split3(v):
    hi = v.astype(BF16)
    r1 = v - hi.astype(F32)
    mid = r1.astype(BF16)
    lo = (r1 - mid.astype(F32)).astype(BF16)
    return hi, mid, lo


def _bias_table(rb_t, onehot, name):
    H = rb_t.shape[0]
    N = onehot.shape[1]

    def body(r_ref, oh_ref, o_ref):
        oh = oh_ref[...]
        hi, mid, lo = _split3(r_ref[...])
        o_ref[...] = (_dot(lo, oh, NN) + _dot(mid, oh, NN)) + _dot(hi, oh, NN)

    return pl.pallas_call(
        body, name=name, in_specs=[VMEM_SPEC, VMEM_SPEC], out_specs=VMEM_SPEC,
        out_shape=jax.ShapeDtypeStruct((H, N), F32),
        compiler_params=pltpu.CompilerParams(vmem_limit_bytes=VMEM_LIMIT),
    )(rb_t, onehot)


def _bias_grad(dbs, onehots, name):
    H = dbs[0].shape[0]
    n = len(dbs)

    def body(*refs):
        acc = jnp.zeros((H, LANES), F32)
        for q in range(n):
            oh = refs[n + q][...]
            hi, mid, lo = _split3(refs[q][...])
            acc = acc + ((_dot(lo, oh, NT) + _dot(mid, oh, NT)) + _dot(hi, oh, NT))
        refs[2 * n][...] = acc

    return pl.pallas_call(
        body, name=name, in_specs=[VMEM_SPEC] * (2 * n), out_specs=VMEM_SPEC,
        out_shape=jax.ShapeDtypeStruct((H, LANES), F32),
        compiler_params=pltpu.CompilerParams(vmem_limit_bytes=VMEM_LIMIT),
    )(*dbs, *onehots)


def _attn_fwd(q, kv, bias, dil, name):
    S, A = q.shape
    H = A // HEAD_DIM
    L = S // dil
    nb = L // BLOCK
    qv = q.reshape(dil, L, A)
    kvv = kv.reshape(2, dil, L, A)

    def body(q_ref, kp_ref, kc_ref, vp_ref, vc_ref, b_ref, o_ref, lse_ref):
        i = pl.program_id(1)
        qi = lax.broadcasted_iota(jnp.int32, (BLOCK, BLOCK), 0)
        ki = lax.broadcasted_iota(jnp.int32, (BLOCK, BLOCK), 1)
        mask_c = ki <= qi
        mask_p = jnp.logical_and(ki >= qi, i > 0)
        lane = lax.broadcasted_iota(jnp.int32, (BLOCK, LANES), 1)
        lse_acc = jnp.zeros((BLOCK, LANES), F32)
        for h in range(H):
            sl = slice(h * HEAD_DIM, (h + 1) * HEAD_DIM)
            qh = q_ref[:, sl]
            s_c = jnp.where(mask_c, _dot(qh, kc_ref[:, sl], NT) * SCALE + b_ref[h, :, BLOCK:], NEG)
            s_p = jnp.where(mask_p, _dot(qh, kp_ref[:, sl], NT) * SCALE + b_ref[h, :, :BLOCK], NEG)
            m = jnp.maximum(jnp.max(s_c, axis=-1, keepdims=True), jnp.max(s_p, axis=-1, keepdims=True))
            p_c = jnp.exp(s_c - m)
            p_p = jnp.exp(s_p - m)
            den = jnp.sum(p_c, axis=-1, keepdims=True) + jnp.sum(p_p, axis=-1, keepdims=True)
            acc = _dot(p_c.astype(BF16), vc_ref[:, sl], NN) + _dot(p_p.astype(BF16), vp_ref[:, sl], NN)
            o_ref[:, sl] = acc / den
            lse_acc = jnp.where(lane == h, m + jnp.log(den), lse_acc)
        lse_ref[...] = lse_acc

    def blk(slab, prev):
        if prev:
            return pl.BlockSpec((None, None, BLOCK, A), lambda r, i: (slab, r, jnp.maximum(i - 1, 0), 0))
        return pl.BlockSpec((None, None, BLOCK, A), lambda r, i: (slab, r, i, 0))

    o, lse = pl.pallas_call(
        body, name=name, grid=(dil, nb),
        in_specs=[pl.BlockSpec((None, BLOCK, A), lambda r, i: (r, i, 0)),
                  blk(0, True), blk(0, False), blk(1, True), blk(1, False),
                  pl.BlockSpec((H, BLOCK, 2 * BLOCK), lambda r, i: (0, 0, 0))],
        out_specs=[pl.BlockSpec((None, BLOCK, A), lambda r, i: (r, i, 0)),
                   pl.BlockSpec((None, BLOCK, LANES), lambda r, i: (r, i, 0))],
        out_shape=[jax.ShapeDtypeStruct((dil, L, A), F32), jax.ShapeDtypeStruct((dil, L, LANES), F32)],
        compiler_params=_params("parallel", "parallel"),
    )(qv, kvv, kvv, kvv, kvv, bias)
    return o.reshape(S, A), lse.reshape(S, LANES)


def _attn_merge(os_, lses, z, dils, name, ts=256):
    S, A = z.shape
    H = A // HEAD_DIM
    ts = _tile(S, ts, 16 * max(dils))
    n = len(os_)

    def body(*refs):
        z_ref = refs[2 * n]
        y_ref, om_ref = refs[2 * n + 1:2 * n + 3]
        lse_refs = refs[2 * n + 3:3 * n + 3]
        o_refs = refs[3 * n + 3:4 * n + 3]
        l_bufs = refs[4 * n + 3:5 * n + 3]
        lse_buf = refs[5 * n + 3]
        ls = []
        for q, d in enumerate(dils):
            if d == 1:
                ls.append(refs[n + q][...])
            else:
                _store_token_order(o_refs[q], refs[q], d)
                _store_token_order(l_bufs[q], refs[n + q], d)
                ls.append(l_bufs[q][0])
        m = ls[0]
        for q in range(1, n):
            m = jnp.maximum(m, ls[q])
        es = [jnp.exp(v - m) for v in ls]
        den = es[0]
        for q in range(1, n):
            den = den + es[q]
        alphas = [e / den for e in es]
        lse = m + jnp.log(den)
        lse_buf[0] = lse
        for q, d in enumerate(dils):
            if d == 1:
                lse_refs[q][...] = lse
            else:
                _emit_group_order(lse_refs[q], lse_buf, d, F32)
        lane = lax.broadcasted_iota(jnp.int32, (ts, LANES), 1)
        for h in range(H):
            sl = slice(h * HEAD_DIM, (h + 1) * HEAD_DIM)
            om = jnp.zeros((ts, HEAD_DIM), F32)
            for q, d in enumerate(dils):
                o_h = refs[q][:, sl] if d == 1 else o_refs[q][h]
                om = om + _lane_col(alphas[q], h, lane) * o_h
            z = z_ref[:, sl].astype(F32)
            y_ref[:, sl] = (om * (z * _sigmoid(z))).astype(BF16)
            om_ref[:, sl] = om.astype(BF16)

    row = pl.BlockSpec((ts, A), lambda i: (i, 0))
    outs = pl.pallas_call(
        body, name=name, grid=(S // ts,),
        in_specs=[_group_spec(d, ts, A) for d in dils] + [_group_spec(d, ts, LANES) for d in dils] + [row],
        out_specs=[row, row] + [_group_spec(d, ts, LANES) for d in dils],
        out_shape=[jax.ShapeDtypeStruct((S, A), BF16), jax.ShapeDtypeStruct((S, A), BF16)]
        + [_group_shape(d, S, LANES, F32) for d in dils],
        scratch_shapes=[_chunk_buf(ts, A)] * n + [_chunk_buf(ts, LANES)] * (n + 1),
        compiler_params=_params("parallel"),
    )(*[o if d == 1 else o.reshape(d, S // d, A) for o, d in zip(os_, dils)],
      *[v if d == 1 else v.reshape(d, S // d, LANES) for v, d in zip(lses, dils)], z)
    return outs[0], outs[1], [v.reshape(S, LANES) for v in outs[2:]]


def _gate_bwd(dy, om, z, dils, name, ts=256):
    S, A = dy.shape
    H = A // HEAD_DIM
    ts = _tile(S, ts, 16 * max(dils))
    n = len(dils)

    def body(*refs):
        dy_ref, om_ref, z_ref = refs[:3]
        do_refs = refs[3:3 + n]
        dh_refs = refs[3 + n:3 + 2 * n]
        dz_ref = refs[3 + 2 * n]
        do_buf, dh_buf = refs[4 + 2 * n:6 + 2 * n]
        lane = lax.broadcasted_iota(jnp.int32, (ts, LANES), 1)
        acc = jnp.zeros((ts, LANES), F32)
        for h in range(H):
            sl = slice(h * HEAD_DIM, (h + 1) * HEAD_DIM)
            dyv = dy_ref[:, sl].astype(F32)
            omv = om_ref[:, sl].astype(F32)
            zv = z_ref[:, sl].astype(F32)
            sz = _sigmoid(zv)
            dob = (dyv * (zv * sz)).astype(BF16)
            do_buf[h] = dob.astype(F32)
            for q, d in enumerate(dils):
                if d == 1:
                    do_refs[q][:, sl] = dob
            dz_ref[:, sl] = (dyv * omv * (sz * (1.0 + zv * (1.0 - sz)))).astype(BF16)
            acc = jnp.where(lane == h, jnp.sum(dob.astype(F32) * omv, axis=-1, keepdims=True), acc)
        dh_buf[0] = acc
        for q, d in enumerate(dils):
            if d == 1:
                dh_refs[q][...] = acc
            else:
                _emit_group_order(do_refs[q], do_buf, d, BF16)
                _emit_group_order(dh_refs[q], dh_buf, d, F32)

    row = pl.BlockSpec((ts, A), lambda i: (i, 0))
    outs = pl.pallas_call(
        body, name=name, grid=(S // ts,), in_specs=[row, row, row],
        out_specs=[_group_spec(d, ts, A) for d in dils] + [_group_spec(d, ts, LANES) for d in dils] + [row],
        out_shape=[_group_shape(d, S, A, BF16) for d in dils] + [_group_shape(d, S, LANES, F32) for d in dils]
        + [jax.ShapeDtypeStruct((S, A), BF16)],
        scratch_shapes=[_chunk_buf(ts, A), _chunk_buf(ts, LANES)],
        compiler_params=_params("parallel"),
    )(dy, om, z)
    return ([v.reshape(S, A) for v in outs[:n]], [v.reshape(S, LANES) for v in outs[n:2 * n]], outs[2 * n])


def _attn_bwd(q, kv, do, lse, dh, bias, dil, name):
    S, A = q.shape
    H = A // HEAD_DIM
    L = S // dil
    nb = L // BLOCK
    qv = q.reshape(dil, L, A)
    kvv = kv.reshape(2, dil, L, A)
    dov = do.reshape(dil, L, A)
    lsev = lse.reshape(dil, L, LANES)
    dhv = dh.reshape(dil, L, LANES)

    def body(*refs):
        (q_ref, qn_ref, kp_ref, kc_ref, vp_ref, vc_ref, do_ref, don_ref, l_ref, ln_ref, d_ref, dn_ref,
         b_ref) = refs[:13]
        dq_ref, dkv_ref, db_ref = refs[13:16]
        r = pl.program_id(0)
        i = pl.program_id(1)
        qi = lax.broadcasted_iota(jnp.int32, (BLOCK, BLOCK), 0)
        ki = lax.broadcasted_iota(jnp.int32, (BLOCK, BLOCK), 1)
        mask_c = ki <= qi
        band = ki >= qi
        mask_p = jnp.logical_and(band, i > 0)
        mask_n = jnp.logical_and(band, i < nb - 1)
        lane = lax.broadcasted_iota(jnp.int32, (BLOCK, LANES), 1)

        @pl.when(jnp.logical_and(r == 0, i == 0))
        def _():
            db_ref[...] = jnp.zeros_like(db_ref)

        for h in range(H):
            sl = slice(h * HEAD_DIM, (h + 1) * HEAD_DIM)
            q_i, q_n = q_ref[:, sl], qn_ref[:, sl]
            k_p, k_c = kp_ref[:, sl], kc_ref[:, sl]
            v_p, v_c = vp_ref[:, sl], vc_ref[:, sl]
            do_i, do_n = do_ref[:, sl], don_ref[:, sl]
            l_i, l_n = _lane_col(l_ref[...], h, lane), _lane_col(ln_ref[...], h, lane)
            d_i, d_n = _lane_col(d_ref[...], h, lane), _lane_col(dn_ref[...], h, lane)
            b_c = b_ref[h, :, BLOCK:]
            b_p = b_ref[h, :, :BLOCK]
            s = jnp.where(mask_c, _dot(q_i, k_c, NT) * SCALE + b_c, NEG)
            p1 = jnp.exp(s - l_i)
            ds1 = p1 * (_dot(do_i, v_c, NT) - d_i)
            ds1b = ds1.astype(BF16)
            p1b = p1.astype(BF16)
            s = jnp.where(mask_p, _dot(q_i, k_p, NT) * SCALE + b_p, NEG)
            p2 = jnp.exp(s - l_i)
            ds2 = p2 * (_dot(do_i, v_p, NT) - d_i)
            ds2b = ds2.astype(BF16)
            s = jnp.where(mask_n, _dot(q_n, k_c, NT) * SCALE + b_p, NEG)
            p3 = jnp.exp(s - l_n)
            ds3b = (p3 * (_dot(do_n, v_c, NT) - d_n)).astype(BF16)
            p3b = p3.astype(BF16)
            dq = _dot(ds1b, k_c, NN) + _dot(ds2b, k_p, NN)
            dk = _dot(ds1b, q_i, TN) + _dot(ds3b, q_n, TN)
            dv = _dot(p1b, do_i, TN) + _dot(p3b, do_n, TN)
            dq_ref[:, sl] = (dq * SCALE).astype(BF16)
            dkv_ref[0, :, sl] = (dk * SCALE).astype(BF16)
            dkv_ref[1, :, sl] = dv.astype(BF16)
            db_ref[h, :, BLOCK:] += ds1
            db_ref[h, :, :BLOCK] += ds2

    def blk(slab, shift):
        if shift < 0:
            return pl.BlockSpec((None, None, BLOCK, A), lambda r, i: (slab, r, jnp.maximum(i - 1, 0), 0))
        return pl.BlockSpec((None, None, BLOCK, A), lambda r, i: (slab, r, i, 0))

    def row(width, shift):
        if shift > 0:
            return pl.BlockSpec((None, BLOCK, width), lambda r, i: (r, jnp.minimum(i + 1, nb - 1), 0))
        return pl.BlockSpec((None, BLOCK, width), lambda r, i: (r, i, 0))

    in_specs = [row(A, 0), row(A, 1), blk(0, -1), blk(0, 0), blk(1, -1), blk(1, 0),
                row(A, 0), row(A, 1), row(LANES, 0), row(LANES, 1), row(LANES, 0), row(LANES, 1),
                pl.BlockSpec((H, BLOCK, 2 * BLOCK), lambda r, i: (0, 0, 0))]
    dq, dkv, db = pl.pallas_call(
        body, name=name, grid=(dil, nb), in_specs=in_specs,
        out_specs=[pl.BlockSpec((None, BLOCK, A), lambda r, i: (r, i, 0)),
                   pl.BlockSpec((2, None, BLOCK, A), lambda r, i: (0, r, i, 0)),
                   pl.BlockSpec((H, BLOCK, 2 * BLOCK), lambda r, i: (0, 0, 0))],
        out_shape=[jax.ShapeDtypeStruct((dil, L, A), BF16), jax.ShapeDtypeStruct((2, dil, L, A), BF16),
                   jax.ShapeDtypeStruct((H, BLOCK, 2 * BLOCK), F32)],
        compiler_params=_params("arbitrary", "arbitrary"),
    )(qv, qv, kvv, kvv, kvv, kvv, dov, dov, lsev, lsev, dhv, dhv, bias)
    return dq.reshape(S, A), dkv.reshape(2, S, A), db


def _sum_leading(stack, out_dtype, name, tr=256, tc=2048):
    n, R, C = stack.shape
    tr = _tile(R, tr, 16)
    tc = _tile(C, tc)

    def body(s_ref, o_ref):
        acc = s_ref[0].astype(F32)
        for q in range(1, n):
            acc = acc + s_ref[q].astype(F32)
        o_ref[...] = acc.astype(out_dtype)

    return pl.pallas_call(
        body, name=name, grid=(R // tr, C // tc),
        in_specs=[pl.BlockSpec((n, tr, tc), lambda i, j: (0, i, j))],
        out_specs=pl.BlockSpec((tr, tc), lambda i, j: (i, j)),
        out_shape=jax.ShapeDtypeStruct((R, C), out_dtype),
        compiler_params=_params("parallel", "parallel"),
    )(stack)


def _add_half(g, t, c_idx, kind, name, tr=256, tc=2048):
    R, C = t.shape
    tr = _tile(R, tr, 16)
    tc = _tile(C, tc)
    nrb, ncb = R // tr, C // tc

    def body(c_ref, g_ref, t_ref, o_ref):
        del c_ref
        o_ref[...] = (g_ref[...].astype(F32) + t_ref[...].astype(F32)).astype(BF16)

    if kind == "col":
        g_map = lambda i, j, c_ref: (c_ref[0] * nrb + i, j)
    else:
        g_map = lambda i, j, c_ref: (i, c_ref[0] * ncb + j)
    same = lambda i, j, c_ref: (i, j)
    return pl.pallas_call(
        body, name=name,
        grid_spec=pltpu.PrefetchScalarGridSpec(
            num_scalar_prefetch=1, grid=(nrb, ncb),
            in_specs=[pl.BlockSpec((tr, tc), g_map), pl.BlockSpec((tr, tc), same)],
            out_specs=pl.BlockSpec((tr, tc), same)),
        out_shape=jax.ShapeDtypeStruct((R, C), BF16),
        compiler_params=_params("parallel", "parallel"),
    )(c_idx, g, t)


def _cast_into_full(w, kind, chip_idx, name, tr=256, tc=2048):
    R, C = w.shape
    tr = _tile(R, tr, 16)
    tc = _tile(C, tc)
    nrb, ncb = R // tr, C // tc

    def body(k_ref, w_ref, o_ref):
        del k_ref
        o_ref[...] = w_ref[...].astype(BF16)

    if kind == "col":
        o_map = lambda i, j, k_ref: (i, k_ref[0] * ncb + j)
        full = (R, N_CHIPS * C)
    else:
        o_map = lambda i, j, k_ref: (k_ref[0] * nrb + i, j)
        full = (N_CHIPS * R, C)
    return pl.pallas_call(
        body, name=name,
        grid_spec=pltpu.PrefetchScalarGridSpec(
            num_scalar_prefetch=1, grid=(nrb, ncb),
            in_specs=[pl.BlockSpec((tr, tc), lambda i, j, k_ref: (i, j))],
            out_specs=pl.BlockSpec((tr, tc), o_map)),
        out_shape=jax.ShapeDtypeStruct(full, BF16),
        compiler_params=_params("parallel", "parallel"),
    )(chip_idx, w)


def _sum_into_shard(p, u, idx, kind, name, tr=256, tc=2048):
    _, R, C = u.shape
    tr = _tile(R, tr, 16)
    tc = _tile(C, tc)
    nrb, ncb = R // tr, C // tc

    def body(i_ref, p_ref, u_ref, o_ref):
        del i_ref
        acc = p_ref[...].astype(F32)
        for q in range(N_CHIPS - 1):
            acc = acc + u_ref[q].astype(F32)
        o_ref[...] = acc

    if kind == "col":
        p_map = lambda i, j, r: (i, r[0] * ncb + j)
        o_map = lambda i, j, r: (r[1] * nrb + i, j)
        full = (2 * R, C)
    else:
        p_map = lambda i, j, r: (r[0] * nrb + i, j)
        o_map = lambda i, j, r: (i, r[1] * ncb + j)
        full = (R, 2 * C)
    return pl.pallas_call(
        body, name=name,
        grid_spec=pltpu.PrefetchScalarGridSpec(
            num_scalar_prefetch=1, grid=(nrb, ncb),
            in_specs=[pl.BlockSpec((tr, tc), p_map), pl.BlockSpec((N_CHIPS - 1, tr, tc), lambda i, j, r: (0, i, j))],
            out_specs=pl.BlockSpec((tr, tc), o_map)),
        out_shape=jax.ShapeDtypeStruct(full, F32),
        compiler_params=_params("parallel", "parallel"),
    )(idx, p, u)


def _adamw(w, g, m, v, name, tr=256, tc=2048):
    R, C = w.shape
    tr = _tile(R, tr, 8)
    tc = _tile(C, tc)
    c1 = 1.0 - ADAM_B1 ** ADAM_STEP
    c2 = 1.0 - ADAM_B2 ** ADAM_STEP

    def body(w_ref, g_ref, m_ref, v_ref, d_ref, nm_ref, nv_ref):
        gv = g_ref[...]
        nm = ADAM_B1 * m_ref[...] + (1.0 - ADAM_B1) * gv
        nv = ADAM_B2 * v_ref[...] + (1.0 - ADAM_B2) * (gv * gv)
        d_ref[...] = -ADAM_LR * ((nm / c1) / (jnp.sqrt(nv / c2) + ADAM_EPS) + ADAM_WD * w_ref[...])
        nm_ref[...] = nm
        nv_ref[...] = nv

    blk = pl.BlockSpec((tr, tc), lambda i, j: (i, j))
    sh = jax.ShapeDtypeStruct((R, C), F32)
    return pl.pallas_call(
        body, name=name, grid=(R // tr, C // tc), in_specs=[blk] * 4, out_specs=[blk] * 3,
        out_shape=[sh, sh, sh], compiler_params=_params("parallel", "parallel"),
    )(w, g, m, v)


def _xyc():
    return lax.axis_index("x"), lax.axis_index("y"), lax.axis_index("c")


def _drain(copies):
    for cp in copies:
        if cp.is_remote:
            cp.wait_send()
        else:
            cp.wait()


def _other_chips(x, y):
    return [(1 - x, y), (x, 1 - y), (1 - x, 1 - y)]


def _allgather_small(blk, name):
    R, C = blk.shape

    def body(x_ref, out_ref, send_sems, recv_sems, local_sem):
        x, y, c = _xyc()
        me = 4 * x + 2 * y + c
        mine = pltpu.make_async_copy(x_ref, out_ref.at[me], local_sem)
        mine.start()
        peers = []
        for k in range(1, N_DEV):
            px = 1 - x if (k >> 2) & 1 else x
            py = 1 - y if (k >> 1) & 1 else y
            pc = 1 - c if k & 1 else c
            peers.append((px, py, pc))
        sends = []
        for k, peer in enumerate(peers):
            cp = pltpu.make_async_remote_copy(
                src_ref=x_ref, dst_ref=out_ref.at[me], send_sem=send_sems.at[k], recv_sem=recv_sems.at[k],
                device_id=peer, device_id_type=MESH)
            cp.start()
            sends.append(cp)
        for k, (px, py, pc) in enumerate(peers):
            pltpu.make_async_remote_copy(
                src_ref=x_ref, dst_ref=out_ref.at[4 * px + 2 * py + pc], send_sem=send_sems.at[k],
                recv_sem=recv_sems.at[k], device_id=(px, py, pc), device_id_type=MESH).wait_recv()
        for cp in sends:
            cp.wait_send()
        mine.wait()

    return pl.pallas_call(
        body, name=name, in_specs=[VMEM_SPEC], out_specs=VMEM_SPEC,
        out_shape=jax.ShapeDtypeStruct((N_DEV, R, C), blk.dtype),
        scratch_shapes=[pltpu.SemaphoreType.DMA((N_DEV - 1,)), pltpu.SemaphoreType.DMA((N_DEV - 1,)),
                        pltpu.SemaphoreType.DMA],
        compiler_params=pltpu.CompilerParams(vmem_limit_bytes=VMEM_LIMIT),
    )(blk)


def _full_region(ref, kind, chip, half, shard_shape):
    r, cn = shard_shape
    hr = r // 2
    if kind == "col":
        rows = pl.ds(0, r) if half is None else pl.ds(pl.multiple_of(half * hr, 16), hr)
        return ref.at[rows, pl.ds(pl.multiple_of(chip * cn, LANES), cn)]
    if half is None:
        return ref.at[pl.ds(pl.multiple_of(chip * r, 16), r), :]
    return ref.at[pl.ds(pl.multiple_of(chip * r + half * hr, 16), hr), :]


def _allgather_weights(fulls, kinds, shapes, name):
    n = len(fulls)

    def body(*refs):
        outs = refs[n:2 * n]
        send_sems, recv_sems = refs[2 * n:]
        x, y, c = _xyc()
        chip = 2 * x + y
        sib = (x, y, 1 - c)
        others = _other_chips(x, y)
        started = []
        for w in range(n):
            mine = _full_region(outs[w], kinds[w], chip, c, shapes[w])
            for j, (ox, oy) in enumerate(others):
                cp = pltpu.make_async_remote_copy(
                    src_ref=mine, dst_ref=mine, send_sem=send_sems.at[6 * w + j], recv_sem=recv_sems.at[6 * w + j],
                    device_id=(ox, oy, c), device_id_type=MESH)
                cp.start()
                started.append(cp)
        for w in range(n):
            for j, (ox, oy) in enumerate(others):
                landed = _full_region(outs[w], kinds[w], 2 * ox + oy, c, shapes[w])
                pltpu.make_async_remote_copy(
                    src_ref=landed, dst_ref=landed, send_sem=send_sems.at[6 * w + j], recv_sem=recv_sems.at[6 * w + j],
                    device_id=(ox, oy, c), device_id_type=MESH).wait_recv()
                cp = pltpu.make_async_remote_copy(
                    src_ref=landed, dst_ref=landed, send_sem=send_sems.at[6 * w + 3 + j],
                    recv_sem=recv_sems.at[6 * w + 3 + j], device_id=sib, device_id_type=MESH)
                cp.start()
                started.append(cp)
        for w in range(n):
            for j, (ox, oy) in enumerate(others):
                theirs = _full_region(outs[w], kinds[w], 2 * ox + oy, 1 - c, shapes[w])
                pltpu.make_async_remote_copy(
                    src_ref=theirs, dst_ref=theirs, send_sem=send_sems.at[6 * w + 3 + j],
                    recv_sem=recv_sems.at[6 * w + 3 + j], device_id=sib, device_id_type=MESH).wait_recv()
        _drain(started)

    return pl.pallas_call(
        body, name=name, in_specs=[ANY] * n, out_specs=[ANY] * n,
        out_shape=[jax.ShapeDtypeStruct(f.shape, f.dtype) for f in fulls],
        input_output_aliases={w: w for w in range(n)},
        scratch_shapes=[pltpu.SemaphoreType.DMA((6 * n,)), pltpu.SemaphoreType.DMA((6 * n,))],
    )(*fulls)


def _region_of_size(ref, kind, shard_shape, count):
    r, cn = shard_shape
    if kind == "col":
        return ref.at[pl.ds(0, r // 2), pl.ds(0, count * cn)]
    return ref.at[pl.ds(0, count * (r // 2)), :]


def _allgather_weights_seq(fulls, kinds, shapes, name, collective_id):
    n = len(fulls)
    refs = [jax.new_ref(f, memory_space=pltpu.MemorySpace.HBM) for f in fulls]

    def body(send_sems, recv_sems):
        x, y, c = _xyc()
        chip = 2 * x + y
        sib = (x, y, 1 - c)
        others = _other_chips(x, y)
        peers = [(ox, oy, c) for ox, oy in others] + [sib]
        barrier = pltpu.get_barrier_semaphore()
        for peer in peers:
            pl.semaphore_signal(barrier, inc=1, device_id=peer, device_id_type=MESH)
        pl.semaphore_wait(barrier, len(peers))

        def copy(w, region, sem, to):
            return pltpu.make_async_remote_copy(src_ref=region, dst_ref=region, send_sem=send_sems.at[sem],
                                                recv_sem=recv_sems.at[sem], device_id=to, device_id_type=MESH)

        for w in range(n):
            mine = _full_region(refs[w], kinds[w], chip, c, shapes[w])
            for ox, oy in others:
                copy(w, mine, 2 * w, (ox, oy, c)).start()
        for w in range(n):
            three = _region_of_size(refs[w], kinds[w], shapes[w], 3)
            copy(w, three, 2 * w, sib).wait_recv()
            for ox, oy in others:
                copy(w, _full_region(refs[w], kinds[w], 2 * ox + oy, c, shapes[w]), 2 * w + 1, sib).start()
        for w in range(n):
            three = _region_of_size(refs[w], kinds[w], shapes[w], 3)
            copy(w, three, 2 * w + 1, sib).wait_recv()
            copy(w, three, 2 * w, sib).wait_send()
            copy(w, three, 2 * w + 1, sib).wait_send()

    pl.kernel(
        body, out_type=(), mesh=plsc.ScalarSubcoreMesh(axis_name="seq", num_cores=1), name=name,
        scratch_types=[pltpu.SemaphoreType.DMA((2 * n,)), pltpu.SemaphoreType.DMA((2 * n,))],
        compiler_params=pltpu.CompilerParams(collective_id=collective_id),
    )()
    return [r[...] for r in refs]


def _half_of(ref, kind, half):
    r, cn = ref.shape
    if kind == "col":
        return ref.at[pl.ds(pl.multiple_of(half * (r // 2), 16), r // 2), :]
    return ref.at[:, pl.ds(pl.multiple_of(half * (cn // 2), LANES), cn // 2)]


def _shard_of(ref, kind, chip):
    r, cn = ref.shape
    if kind == "col":
        return ref.at[:, pl.ds(pl.multiple_of(chip * (cn // N_CHIPS), LANES), cn // N_CHIPS)]
    return ref.at[pl.ds(pl.multiple_of(chip * (r // N_CHIPS), 16), r // N_CHIPS), :]


def _exchange_halves(grads, kinds, name):
    n = len(grads)

    def body(*refs):
        gs = refs[:n]
        ts = refs[n:2 * n]
        send_sems, recv_sems = refs[2 * n:]
        x, y, c = _xyc()
        cps = []
        for w in range(n):
            cp = pltpu.make_async_remote_copy(
                src_ref=_half_of(gs[w], kinds[w], 1 - c), dst_ref=ts[w], send_sem=send_sems.at[w],
                recv_sem=recv_sems.at[w], device_id=(x, y, 1 - c), device_id_type=MESH)
            cp.start()
            cps.append(cp)
        for cp in cps:
            cp.wait()

    out_shape = []
    for gr, kind in zip(grads, kinds):
        r, cn = gr.shape
        out_shape.append(jax.ShapeDtypeStruct((r // 2, cn) if kind == "col" else (r, cn // 2), gr.dtype))
    return pl.pallas_call(
        body, name=name, in_specs=[ANY] * n, out_specs=[ANY] * n, out_shape=out_shape,
        scratch_shapes=[pltpu.SemaphoreType.DMA((n,)), pltpu.SemaphoreType.DMA((n,))],
    )(*grads)


def _scatter_partials(parts, kinds, name):
    n = len(parts)

    def body(*refs):
        ps = refs[:n]
        us = refs[n:2 * n]
        send_sems, recv_sems = refs[2 * n:]
        x, y, c = _xyc()
        others = _other_chips(x, y)
        cps = []
        for w in range(n):
            for j, (ox, oy) in enumerate(others):
                cp = pltpu.make_async_remote_copy(
                    src_ref=_shard_of(ps[w], kinds[w], 2 * ox + oy), dst_ref=us[w].at[j],
                    send_sem=send_sems.at[3 * w + j], recv_sem=recv_sems.at[3 * w + j],
                    device_id=(ox, oy, c), device_id_type=MESH)
                cp.start()
                cps.append(cp)
        for cp in cps:
            cp.wait()

    out_shape = []
    for p, kind in zip(parts, kinds):
        r, cn = p.shape
        hs = (r, cn // N_CHIPS) if kind == "col" else (r // N_CHIPS, cn)
        out_shape.append(jax.ShapeDtypeStruct((N_CHIPS - 1,) + hs, p.dtype))
    return pl.pallas_call(
        body, name=name, in_specs=[ANY] * n, out_specs=[ANY] * n, out_shape=out_shape,
        scratch_shapes=[pltpu.SemaphoreType.DMA((3 * n,)), pltpu.SemaphoreType.DMA((3 * n,))],
    )(*parts)


def _scatter_partials_seq(parts, kinds, name, collective_id):
    n = len(parts)

    def body(*refs):
        ps = refs[:n]
        us = refs[n:2 * n]
        send_sems, recv_sems = refs[2 * n:]
        x, y, c = _xyc()
        others = _other_chips(x, y)
        barrier = pltpu.get_barrier_semaphore()
        for ox, oy in others:
            pl.semaphore_signal(barrier, inc=1, device_id=(ox, oy, c), device_id_type=MESH)
        pl.semaphore_wait(barrier, len(others))
        for w in range(n):
            for j, (ox, oy) in enumerate(others):
                pltpu.make_async_remote_copy(
                    src_ref=_shard_of(ps[w], kinds[w], 2 * ox + oy), dst_ref=us[w].at[j],
                    send_sem=send_sems.at[w], recv_sem=recv_sems.at[w],
                    device_id=(ox, oy, c), device_id_type=MESH).start()
        for w in range(n):
            pltpu.make_async_remote_copy(
                src_ref=us[w], dst_ref=us[w], send_sem=send_sems.at[w], recv_sem=recv_sems.at[w],
                device_id=(x, y, c), device_id_type=MESH).wait()

    out_type = []
    for p, kind in zip(parts, kinds):
        r, cn = p.shape
        hs = (r, cn // N_CHIPS) if kind == "col" else (r // N_CHIPS, cn)
        out_type.append(jax.ShapeDtypeStruct((N_CHIPS - 1,) + hs, p.dtype))
    return pl.kernel(
        body, out_type=out_type, mesh=plsc.ScalarSubcoreMesh(axis_name="seq", num_cores=1), name=name,
        scratch_types=[pltpu.SemaphoreType.DMA((n,)), pltpu.SemaphoreType.DMA((n,))],
        compiler_params=pltpu.CompilerParams(collective_id=collective_id),
    )(*parts)


def _join_halves(halves, kinds, name):
    n = len(halves)

    def body(*refs):
        outs = refs[n:2 * n]
        send_sems, recv_sems = refs[2 * n:]
        x, y, c = _xyc()
        cps = []
        for w in range(n):
            mine = _half_of(outs[w], kinds[w], c)
            cp = pltpu.make_async_remote_copy(
                src_ref=mine, dst_ref=mine, send_sem=send_sems.at[w], recv_sem=recv_sems.at[w],
                device_id=(x, y, 1 - c), device_id_type=MESH)
            cp.start()
            cps.append(cp)
        for w in range(n):
            theirs = _half_of(outs[w], kinds[w], 1 - c)
            pltpu.make_async_remote_copy(
                src_ref=theirs, dst_ref=theirs, send_sem=send_sems.at[w], recv_sem=recv_sems.at[w],
                device_id=(x, y, 1 - c), device_id_type=MESH).wait_recv()
        _drain(cps)

    return pl.pallas_call(
        body, name=name, in_specs=[ANY] * n, out_specs=[ANY] * n,
        out_shape=[jax.ShapeDtypeStruct(h.shape, h.dtype) for h in halves],
        input_output_aliases={w: w for w in range(n)},
        scratch_shapes=[pltpu.SemaphoreType.DMA((n,)), pltpu.SemaphoreType.DMA((n,))],
    )(*halves)


def kernel(x, a_norm, a_w_in, a_conv_w, a_conv_b, a_ln_g, a_ln_b, a_w_out, kv_norm, w_kv, b_norm, b_w_in, b_w_out, rel_bias, final_norm, loss_target, m_a_norm, m_a_w_in, m_a_conv_w, m_a_conv_b, m_a_ln_g, m_a_ln_b, m_a_w_out, m_kv_norm, m_w_kv, m_b_norm, m_b_w_in, m_b_w_out, m_rel_bias, m_final_norm, v_a_norm, v_a_w_in, v_a_conv_w, v_a_conv_b, v_a_ln_g, v_a_ln_b, v_a_w_out, v_kv_norm, v_w_kv, v_b_norm, v_b_w_in, v_b_w_out, v_rel_bias, v_final_norm):
    S, D = x.shape[1], x.shape[2]
    E = a_w_out.shape[1] * N_CHIPS
    A = b_w_out.shape[1] * N_CHIPS
    H = A // HEAD_DIM
    DC = D // N_CHIPS
    xs = x.reshape(S, D)
    tgt = loss_target.reshape(S, D)
    cx, cy, cc = _xyc()
    chip = 2 * cx + cy
    c_idx = jnp.reshape(cc, (1,)).astype(jnp.int32)

    big_names = ["a_w_in", "a_w_out", "w_kv", "b_w_in", "b_w_out"]
    kinds = ["col", "row", "col", "col", "row"]
    big_w = [a_w_in[0], a_w_out[0], w_kv, b_w_in[0], b_w_out[0]]
    big_m = [m_a_w_in[0], m_a_w_out[0], m_w_kv, m_b_w_in[0], m_b_w_out[0]]
    big_v = [v_a_w_in[0], v_a_w_out[0], v_w_kv, v_b_w_in[0], v_b_w_out[0]]
    chip_idx = jnp.reshape(chip, (1,)).astype(jnp.int32)
    placed = [_cast_into_full(big_w[w], kinds[w], chip_idx, "cast_" + big_names[w]) for w in range(5)]
    shard_shapes = [w.shape for w in big_w]
    (wa_in,) = _allgather_weights_seq(placed[0:1], kinds[0:1], shard_shapes[0:1], "ag_seq_a_in", 0)
    wa_out, wkv = _allgather_weights_seq(placed[1:3], kinds[1:3], shard_shapes[1:3], "ag_seq_a_out_kv", 1)
    wb_in, wb_out = _allgather_weights_seq(placed[3:5], kinds[3:5], shard_shapes[3:5], "ag_seq_b", 2)

    def row_at(vec, q):
        return jnp.pad(vec, ((q, 7 - q), (0, 0)))

    def pack_sharded(an, cw, cb, lg, lb):
        return jnp.concatenate([row_at(an, 0), jnp.pad(cw[0], ((0, 1), (0, 0))),
                                row_at(lg, 0) + row_at(lb, 1) + row_at(cb, 2)], axis=0)

    small_w = pack_sharded(a_norm, a_conv_w, a_conv_b, a_ln_g, a_ln_b)
    gathered = _allgather_small(small_w, "ag_small_params")
    small_full = jnp.concatenate([gathered[2 * k] for k in range(N_CHIPS)], axis=1)
    g_a = small_full[0:1]
    conv_w32 = small_full[8:8 + HALO]
    ln_g = small_full[40:41]
    ln_b = small_full[41:42]
    conv_b = small_full[42:43]
    g_kv = kv_norm.reshape(1, D)
    g_b = b_norm.reshape(1, D)
    g_f = final_norm.reshape(1, D)

    rb_t = jnp.pad(rel_bias.T, ((0, 0), (0, LANES - N_BUCKETS)))
    onehots = [_onehot(dil) for _, dil in GROUPS]
    biases = [_bias_table(rb_t, onehots[g], "bias_table_%d" % g).reshape(H, BLOCK, 2 * BLOCK)
              for g in range(len(GROUPS))]

    dils = tuple(dil for _, dil in GROUPS)
    assert dils[0] == 1
    n_g = len(GROUPS)
    ((h0,),) = _rms_fwd(xs, [g_a], (1,), "rms_a")
    proj3 = _matmul(h0, wa_in, "nn", BF16, "mm_a_in", out_slab=E)
    conv = _conv_fwd(proj3, conv_w32, conv_b, "conv_fwd")
    y_a = _ln_gate_fwd(conv, proj3, ln_g, ln_b, "ln_gate_fwd")
    x1 = _matmul(y_a, wa_out, "nn", F32, "mm_a_out", res=xs)
    hks, hbs = _rms_fwd(x1, [g_kv, g_b], dils, "rms_kv_b")
    kvs = [_matmul(hks[g], wkv, "nn", BF16, "mm_kv_%d" % g, out_slab=A, b_off=2 * g * A, n_cols=2 * A)
           for g in range(n_g)]
    qs = [_matmul(hbs[g], wb_in, "nn", BF16, "mm_q_%d" % g, b_off=g * A, n_cols=A) for g in range(n_g)]
    zb = _matmul(hbs[0], wb_in, "nn", BF16, "mm_zb", b_off=n_g * A, n_cols=A)
    os_, lses = [], []
    for g, dil in enumerate(dils):
        o_g, lse_g = _attn_fwd(qs[g], kvs[g], biases[g], dil, "attn_fwd_%d" % g)
        os_.append(o_g)
        lses.append(lse_g)
    y_b, o_m, lse_d = _attn_merge(os_, lses, zb, dils, "attn_merge")
    x2 = _matmul(y_b, wb_out, "nn", F32, "mm_b_out", res=x1)
    loss_part, dx2, dx2b, gg_f = _final_head(x2, g_f, tgt, "final_head")
    loss = lax.psum(loss_part[0, 0], ("x", "y", "c"))

    dw_tiles = dict(tm=512, tn=1024, tk=4096)
    dy_b = _matmul(dx2b, wb_out, "nt", BF16, "mm_b_out_dx")
    dwb_out = _matmul(y_b, dx2b, "tn", BF16, "mm_b_out_dw", **dw_tiles)
    dos, dhs, dzb = _gate_bwd(dy_b, o_m, zb, dils, "gate_bwd")
    dbs, cots = [], []
    dwb_in = dwkv = None
    for g, dil in enumerate(dils):
        dq, dkv, db = _attn_bwd(qs[g], kvs[g], dos[g], lse_d[g], dhs[g], biases[g], dil, "attn_bwd_%d" % g)
        dbs.append(db.reshape(H, BLOCK * 2 * BLOCK))
        dwb_in = _matmul(hbs[g], dq, "tn", BF16, "mm_q_dw_%d" % g, out_off=g * A, out_cols=(n_g + 1) * A,
                         out_alias=dwb_in, **dw_tiles)
        dwkv = _matmul(hks[g], dkv, "tn", BF16, "mm_kv_dw_%d" % g, b_slab=True, out_off=2 * g * A,
                       out_cols=2 * n_g * A, out_alias=dwkv, **dw_tiles)
        cots.append((_matmul(dkv, wkv, "nt", BF16, "mm_kv_dx_%d" % g, a_slab=True, b_off=2 * g * A), 0, dil))
        cots.append((_matmul(dq, wb_in, "nt", BF16, "mm_q_dx_%d" % g, b_off=g * A), 1, dil))
    dwb_in = _matmul(hbs[0], dzb, "tn", BF16, "mm_zb_dw", out_off=n_g * A, out_cols=(n_g + 1) * A,
                     out_alias=dwb_in, **dw_tiles)
    cots.append((_matmul(dzb, wb_in, "nt", BF16, "mm_zb_dx", b_off=n_g * A), 1, 1))
    chip_c = jnp.stack([chip, cc]).astype(jnp.int32)

    def scatter_group(idx, grads, tag, collective_id):
        ks = [kinds[w] for w in idx]
        theirs = _exchange_halves(grads, ks, "rs_exchange_" + tag)
        parts = [_add_half(grads[q], theirs[q], c_idx, ks[q], "rs_add_half_%d" % w) for q, w in enumerate(idx)]
        return parts, _scatter_partials_seq(parts, ks, "rs_seq_" + tag, collective_id)

    def reduce_group(idx, parts, slots, tag):
        ks = [kinds[w] for w in idx]
        halves = [_sum_into_shard(parts[q], slots[q], chip_c, ks[q], "rs_sum_chips_%d" % w)
                  for q, w in enumerate(idx)]
        return _join_halves(halves, ks, "rs_join_" + tag)

    parts_b, slots_b = scatter_group([2, 3, 4], [dwkv, dwb_in, dwb_out], "b", 3)
    g_rel_t = _bias_grad(dbs, onehots, "bias_grad")
    dx1, dx1b, gg_kvb = _rms_bwd(x1, cots, [g_kv, g_b], dx2, "rms_kv_b_bwd")
    dy_a = _matmul(dx1b, wa_out, "nt", BF16, "mm_a_out_dx")
    dwa_out = _matmul(y_a, dx1b, "tn", BF16, "mm_a_out_dw", **dw_tiles)
    dconv, dproj3, gg_ln = _ln_gate_bwd(conv, proj3, dy_a, ln_g, ln_b, "ln_gate_bwd")
    dproj3, g_conv_w = _conv_bwd(proj3, dconv, conv_w32, dproj3, "conv_bwd")
    dh0 = _matmul(dproj3, wa_in, "nt", BF16, "mm_a_in_dx", a_slab=True)
    dwa_in = _matmul(h0, dproj3, "tn", BF16, "mm_a_in_dw", b_slab=True, **dw_tiles)
    grad_x, _, gg_a = _rms_bwd(xs, [(dh0, 0, 1)], [g_a], dx1, "rms_a_bwd")

    parts_a, slots_a = scatter_group([0, 1], [dwa_in, dwa_out], "a", 4)
    big_g = [None] * 5
    big_g[2:5] = reduce_group([2, 3, 4], parts_b, slots_b, "b")
    big_g[0:2] = reduce_group([0, 1], parts_a, slots_a, "a")

    def rel_rows(rb):
        return jnp.pad(rb.reshape(1, N_BUCKETS * H), ((0, 7), (0, D - N_BUCKETS * H)))

    small_g = jnp.concatenate([gg_a, g_conv_w, gg_ln, gg_kvb, gg_f, rel_rows(g_rel_t[:, :N_BUCKETS].T)], axis=0)
    small_sum = _sum_leading(_allgather_small(small_g, "ag_small_grads"), F32, "sum_small_grads", tr=72)
    g_sharded = lax.dynamic_slice(small_sum, (0, chip * DC), (48, DC))
    g_repl = small_sum[48:72]

    outs_g, outs_d, outs_m, outs_v = {}, {}, {}, {}
    for w, nm in enumerate(big_names):
        d_, m_, v_ = _adamw(big_w[w], big_g[w], big_m[w], big_v[w], "adamw_" + nm)
        outs_g[nm], outs_d[nm], outs_m[nm], outs_v[nm] = big_g[w], d_, m_, v_
    sm_m = pack_sharded(m_a_norm, m_a_conv_w, m_a_conv_b, m_a_ln_g, m_a_ln_b)
    sm_v = pack_sharded(v_a_norm, v_a_conv_w, v_a_conv_b, v_a_ln_g, v_a_ln_b)
    sd, smm, svv = _adamw(small_w, g_sharded, sm_m, sm_v, "adamw_small_sharded")

    def unpack_sharded(p):
        return {"a_norm": p[0:1], "a_conv_w": p[8:8 + CONV_TAPS].reshape(1, CONV_TAPS, DC), "a_ln_g": p[40:41],
                "a_ln_b": p[41:42], "a_conv_b": p[42:43]}

    for src, dst in ((g_sharded, outs_g), (sd, outs_d), (smm, outs_m), (svv, outs_v)):
        dst.update(unpack_sharded(src))

    def pack_repl(kn, bn, fn, rb):
        return jnp.concatenate([row_at(kn.reshape(1, D), 0) + row_at(bn.reshape(1, D), 1),
                                row_at(fn.reshape(1, D), 0), rel_rows(rb)], axis=0)

    rp_w = pack_repl(kv_norm, b_norm, final_norm, rel_bias)
    rp_m = pack_repl(m_kv_norm, m_b_norm, m_final_norm, m_rel_bias)
    rp_v = pack_repl(v_kv_norm, v_b_norm, v_final_norm, v_rel_bias)
    rd, rmm, rvv = _adamw(rp_w, g_repl, rp_m, rp_v, "adamw_small_replicated")

    def unpack_repl(p):
        return {"kv_norm": p[0], "b_norm": p[1:2], "final_norm": p[8],
                "rel_bias": p[16, :N_BUCKETS * H].reshape(N_BUCKETS, H)}

    for src, dst in ((g_repl, outs_g), (rd, outs_d), (rmm, outs_m), (rvv, outs_v)):
        dst.update(unpack_repl(src))

    order = ["a_norm", "a_w_in", "a_conv_w", "a_conv_b", "a_ln_g", "a_ln_b", "a_w_out", "kv_norm", "w_kv",
             "b_norm", "b_w_in", "b_w_out", "rel_bias", "final_norm"]
    lead = {"a_w_in", "a_w_out", "b_w_in", "b_w_out"}

    def shaped(nm, val):
        return val[None] if nm in lead else val

    result = [loss, grad_x.reshape(1, S, D)]
    for table in (outs_g, outs_d, outs_m, outs_v):
        result.extend(shaped(nm, table[nm]) for nm in order)
    return tuple(result)
```

```python
import functools

import numpy as np
import jax
import jax.numpy as jnp
from jax import lax
from jax.experimental import pallas as pl
from jax.experimental.pallas import tpu as pltpu
from jax.experimental.pallas import tpu_sc as plsc

F32 = jnp.float32
BF16 = jnp.bfloat16
MESH = pl.DeviceIdType.MESH
ANY = pl.BlockSpec(memory_space=pl.ANY)
VMEM_SPEC = pl.BlockSpec(memory_space=pltpu.VMEM)

EPS = 1e-6
HEAD_DIM = 128
BLOCK = 128
GROUPS = ((128, 1), (512, 4), (2048, 16))
SCALE = HEAD_DIM ** -0.5
CONV_TAPS = 31
HALO = 32
N_BUCKETS = 32
MAX_EXACT = 16
MAX_DISTANCE = 2048
NEG = -1e30
N_CHIPS = 4
N_DEV = 8
LANES = 128
VMEM_LIMIT = 56 * 1024 * 1024

ADAM_LR = 0.001
ADAM_B1 = 0.9
ADAM_B2 = 0.999
ADAM_EPS = 1e-08
ADAM_WD = 0.01
ADAM_STEP = 10


def _tile(n, pref, mult=LANES):
    t = (min(pref, n) // mult) * mult
    while t >= mult:
        if n % t == 0:
            return t
        t -= mult
    return n


def _params(*sem):
    return pltpu.CompilerParams(dimension_semantics=sem, vmem_limit_bytes=VMEM_LIMIT)


def _sigmoid(v):
    return 1.0 / (1.0 + jnp.exp(-v))


def _dot(a, b, dims):
    return lax.dot_general(a, b, (dims, ((), ())), preferred_element_type=F32)


NN = ((1,), (0,))
NT = ((1,), (1,))
TN = ((0,), (0,))


def _stack_rows(rows, total):
    width = rows[0].shape[1]
    rid = lax.broadcasted_iota(jnp.int32, (total, width), 0)
    out = jnp.zeros((total, width), F32)
    for q, row in enumerate(rows):
        out = jnp.where(rid == q, jnp.broadcast_to(row, (total, width)), out)
    return out


def _lane_col(arr, h, lane):
    return jnp.sum(jnp.where(lane == h, arr, 0.0), axis=-1, keepdims=True)


def _matmul(a, b, mode, out_dtype, name, res=None, a_slab=False, b_slab=False, out_slab=0,
            b_off=0, n_cols=None, out_off=0, out_cols=None, out_alias=None, after=None,
            tm=512, tn=1024, tk=2048):
    if a_slab:
        na, M, W = a.shape
        K = na * W
    elif mode == "tn":
        K, M = a.shape
    else:
        M, K = a.shape
    if b_slab:
        nbs, _, Wb = b.shape
        N = nbs * Wb
    elif mode == "nt":
        N = b.shape[0]
    else:
        N = n_cols if n_cols else b.shape[1]
    tm = _tile(M, tm)
    tn = _tile(Wb if b_slab else (out_slab if out_slab else N), tn)
    tk = _tile(W if a_slab else K, tk)
    nk = K // tk
    grid = (M // tm, N // tn, nk)
    bo = b_off // (tk if mode == "nt" else tn)
    oo = out_off // tn

    if a_slab:
        per = W // tk
        a_spec = pl.BlockSpec((None, tm, tk), lambda i, j, k: (k // per, i, k % per))
    elif mode == "tn":
        a_spec = pl.BlockSpec((tk, tm), lambda i, j, k: (k, i))
    else:
        a_spec = pl.BlockSpec((tm, tk), lambda i, j, k: (i, k))
    if b_slab:
        perb = Wb // tn
        b_spec = pl.BlockSpec((None, tk, tn), lambda i, j, k: (j // perb, k, j % perb))
    elif mode == "nt":
        b_spec = pl.BlockSpec((tn, tk), lambda i, j, k: (j, k + bo))
    else:
        b_spec = pl.BlockSpec((tk, tn), lambda i, j, k: (k, j + bo))
    if out_slab:
        pero = out_slab // tn
        o_spec = pl.BlockSpec((None, tm, tn), lambda i, j, k: (j // pero, i, j % pero))
        out_shape = jax.ShapeDtypeStruct((N // out_slab, M, out_slab), out_dtype)
    else:
        o_spec = pl.BlockSpec((tm, tn), lambda i, j, k: (i, j + oo))
        out_shape = jax.ShapeDtypeStruct((M, out_cols if out_cols else N), out_dtype)
    in_specs = [a_spec, b_spec]
    operands = [a, b]
    if res is not None:
        in_specs.append(pl.BlockSpec((tm, tn), lambda i, j, k: (i, j)))
        operands.append(res)
    aliases = {}
    if out_alias is not None:
        aliases[len(operands)] = 0
        in_specs.append(ANY)
        operands.append(out_alias)
    if after is not None:
        in_specs.append(ANY)
        operands.append(after)
    dims = {"nn": NN, "nt": NT, "tn": TN}[mode]
    has_res = res is not None
    n_in = len(operands)

    def body(*refs):
        a_ref, b_ref = refs[0], refs[1]
        r_ref = refs[2] if has_res else None
        o_ref = refs[n_in]
        prod = _dot(a_ref[...], b_ref[...], dims)

        def finish(val):
            if has_res:
                val = val + r_ref[...]
            o_ref[...] = val.astype(out_dtype)

        if nk == 1:
            finish(prod)
        else:
            acc_ref = refs[n_in + 1]
            k = pl.program_id(2)

            @pl.when(k == 0)
            def _():
                acc_ref[...] = prod

            @pl.when(k > 0)
            def _():
                acc_ref[...] += prod

            @pl.when(k == nk - 1)
            def _():
                finish(acc_ref[...])

    scratch = [pltpu.VMEM((tm, tn), F32)] if nk > 1 else []
    return pl.pallas_call(
        body, name=name, grid=grid, in_specs=in_specs, out_specs=o_spec, out_shape=out_shape,
        scratch_shapes=scratch, input_output_aliases=aliases,
        compiler_params=_params("parallel", "parallel", "arbitrary"),
    )(*operands)


def _group_spec(d, ts, width):
    if d == 1:
        return pl.BlockSpec((ts, width), lambda i: (i, 0))
    return pl.BlockSpec((d, ts // d, width), lambda i: (0, i, 0))


def _group_shape(d, S, width, dtype):
    return jax.ShapeDtypeStruct((S, width) if d == 1 else (d, S // d, width), dtype)


def _chunk_buf(ts, width):
    return pltpu.VMEM((width // LANES, ts, LANES), F32)


def _fill_chunks(buf, val):
    for c in range(buf.shape[0]):
        buf[c] = val[:, c * LANES:(c + 1) * LANES]


def _read_chunks(buf):
    return jnp.concatenate([buf[c] for c in range(buf.shape[0])], axis=1)


def _emit_group_order(o_ref, buf, d, dtype):
    n = buf.shape[1] // d
    for r in range(d):
        for c in range(buf.shape[0]):
            o_ref[r, :, c * LANES:(c + 1) * LANES] = buf[c, pl.ds(r, n, stride=d), :].astype(dtype)


def _store_token_order(buf, i_ref, d):
    n = buf.shape[1] // d
    for r in range(d):
        for c in range(buf.shape[0]):
            buf[c, pl.ds(r, n, stride=d), :] = i_ref[r, :, c * LANES:(c + 1) * LANES].astype(F32)


def _rms_fwd(x, gains, dils, name, ts=256):
    S, D = x.shape
    ts = _tile(S, ts, 16 * max(dils))
    n = len(gains)
    nd = len(dils)

    def body(*refs):
        buf = refs[1 + n + n * nd]
        xv = refs[0][...]
        nrm = xv * lax.rsqrt(jnp.mean(xv * xv, axis=-1, keepdims=True) + EPS)
        for q in range(n):
            val = nrm * refs[1 + q][...]
            if max(dils) > 1:
                _fill_chunks(buf, val)
            for e, d in enumerate(dils):
                if d == 1:
                    refs[1 + n + q * nd + e][...] = val.astype(BF16)
                else:
                    _emit_group_order(refs[1 + n + q * nd + e], buf, d, BF16)

    row = pl.BlockSpec((ts, D), lambda i: (i, 0))
    vec = pl.BlockSpec((1, D), lambda i: (0, 0))
    outs = pl.pallas_call(
        body, name=name, grid=(S // ts,), in_specs=[row] + [vec] * n,
        out_specs=[_group_spec(d, ts, D) for _ in range(n) for d in dils],
        out_shape=[_group_shape(d, S, D, BF16) for _ in range(n) for d in dils],
        scratch_shapes=[_chunk_buf(ts, D)],
        compiler_params=_params("parallel"),
    )(x, *gains)
    return [[outs[q * nd + e].reshape(S, D) for e in range(nd)] for q in range(n)]


def _rms_bwd(x, cots, gains, dres, name, after=None, ts=256):
    S, D = x.shape
    ts = _tile(S, ts, 16 * max(d for _, _, d in cots))
    n = len(cots)
    ng = len(gains)
    extra = [] if after is None else [after]
    n_in = 2 + n + ng + len(extra)

    def body(*refs):
        x_ref = refs[0]
        dh_refs = refs[1:1 + n]
        g_refs = refs[1 + n:1 + n + ng]
        dres_ref = refs[1 + n + ng]
        dx_ref, dxb_ref, gg_ref, buf = refs[n_in:n_in + 4]
        i = pl.program_id(0)
        xv = x_ref[...]
        r = lax.rsqrt(jnp.mean(xv * xv, axis=-1, keepdims=True) + EPS)
        nrm = xv * r
        dn = jnp.zeros_like(xv)
        rows = [jnp.zeros((1, D), F32) for _ in range(ng)]
        for q, (_, gi, d) in enumerate(cots):
            if d == 1:
                dh = dh_refs[q][...].astype(F32)
            else:
                _store_token_order(buf, dh_refs[q], d)
                dh = _read_chunks(buf)
            dn = dn + dh * g_refs[gi][...]
            rows[gi] = rows[gi] + jnp.sum(dh * nrm, axis=0, keepdims=True)
        dx = dres_ref[...] + r * (dn - nrm * jnp.mean(dn * nrm, axis=-1, keepdims=True))
        dx_ref[...] = dx
        dxb_ref[...] = dx.astype(BF16)
        upd = _stack_rows(rows, 8)

        @pl.when(i == 0)
        def _():
            gg_ref[...] = upd

        @pl.when(i > 0)
        def _():
            gg_ref[...] += upd

    row = pl.BlockSpec((ts, D), lambda i: (i, 0))
    vec = pl.BlockSpec((1, D), lambda i: (0, 0))
    acc = pl.BlockSpec((8, D), lambda i: (0, 0))
    return pl.pallas_call(
        body, name=name, grid=(S // ts,),
        in_specs=[row] + [_group_spec(d, ts, D) for _, _, d in cots] + [vec] * ng + [row] + [ANY] * len(extra),
        out_specs=[row, row, acc],
        out_shape=[jax.ShapeDtypeStruct((S, D), F32), jax.ShapeDtypeStruct((S, D), BF16),
                   jax.ShapeDtypeStruct((8, D), F32)],
        scratch_shapes=[_chunk_buf(ts, D)],
        compiler_params=_params("arbitrary"),
    )(x, *[a if d == 1 else a.reshape(d, S // d, D) for a, _, d in cots], *gains, dres, *extra)


def _final_head(x2, gain, target, name, ts=256):
    S, D = x2.shape
    ts = _tile(S, ts, 16)

    def body(x_ref, g_ref, t_ref, loss_ref, dx_ref, dxb_ref, gg_ref):
        i = pl.program_id(0)
        xv = x_ref[...]
        g = g_ref[...]
        r = lax.rsqrt(jnp.mean(xv * xv, axis=-1, keepdims=True) + EPS)
        nrm = xv * r
        err = nrm * g - t_ref[...]
        part = 0.5 * jnp.sum(jnp.mean(err * err, axis=-1, keepdims=True), axis=0, keepdims=True)
        dout = err * (1.0 / D)
        dn = dout * g
        dx = r * (dn - nrm * jnp.mean(dn * nrm, axis=-1, keepdims=True))
        dx_ref[...] = dx
        dxb_ref[...] = dx.astype(BF16)
        upd = _stack_rows([jnp.sum(dout * nrm, axis=0, keepdims=True)], 8)
        lpart = jnp.broadcast_to(part, (1, LANES))

        @pl.when(i == 0)
        def _():
            gg_ref[...] = upd
            loss_ref[...] = lpart

        @pl.when(i > 0)
        def _():
            gg_ref[...] += upd
            loss_ref[...] += lpart

    row = pl.BlockSpec((ts, D), lambda i: (i, 0))
    vec = pl.BlockSpec((1, D), lambda i: (0, 0))
    return pl.pallas_call(
        body, name=name, grid=(S // ts,), in_specs=[row, vec, row],
        out_specs=[pl.BlockSpec((1, LANES), lambda i: (0, 0)), row, row, pl.BlockSpec((8, D), lambda i: (0, 0))],
        out_shape=[jax.ShapeDtypeStruct((1, LANES), F32), jax.ShapeDtypeStruct((S, D), F32),
                   jax.ShapeDtypeStruct((S, D), BF16), jax.ShapeDtypeStruct((8, D), F32)],
        compiler_params=_params("arbitrary"),
    )(x2, gain, target)


CONV_ROWS = 64


def _conv_fwd(proj3, conv_w32, conv_b, name, ts=256, cw=256):
    _, S, E = proj3.shape
    ts = _tile(S, ts, HALO)
    cw = _tile(E, cw)
    per = ts // HALO
    rc = min(CONV_ROWS, ts)

    def body(a_ref, b_ref, ap_ref, bp_ref, w_ref, cb_ref, c_ref, ubuf):
        i = pl.program_id(0)
        up = ap_ref[...].astype(F32) * _sigmoid(bp_ref[...].astype(F32))
        ubuf[0:HALO, :] = jnp.where(i > 0, up, 0.0)
        ubuf[HALO:HALO + ts, :] = a_ref[...].astype(F32) * _sigmoid(b_ref[...].astype(F32))
        for r0 in range(0, ts, rc):
            acc = jnp.broadcast_to(cb_ref[...], (rc, cw))
            for k in range(CONV_TAPS):
                off = r0 + HALO - (CONV_TAPS - 1) + k
                acc = acc + ubuf[off:off + rc, :] * w_ref[k:k + 1, :]
            c_ref[r0:r0 + rc, :] = acc

    return pl.pallas_call(
        body, name=name, grid=(S // ts, E // cw),
        in_specs=[
            pl.BlockSpec((None, ts, cw), lambda i, j: (0, i, j)),
            pl.BlockSpec((None, ts, cw), lambda i, j: (1, i, j)),
            pl.BlockSpec((None, HALO, cw), lambda i, j: (0, jnp.maximum(i * per - 1, 0), j)),
            pl.BlockSpec((None, HALO, cw), lambda i, j: (1, jnp.maximum(i * per - 1, 0), j)),
            pl.BlockSpec((HALO, cw), lambda i, j: (0, j)),
            pl.BlockSpec((1, cw), lambda i, j: (0, j)),
        ],
        out_specs=pl.BlockSpec((ts, cw), lambda i, j: (i, j)),
        out_shape=jax.ShapeDtypeStruct((S, E), F32),
        scratch_shapes=[pltpu.VMEM((HALO + ts, cw), F32)],
        compiler_params=_params("parallel", "parallel"),
    )(proj3, proj3, proj3, proj3, conv_w32, conv_b)


def _ln_gate_fwd(c, proj3, ln_g, ln_b, name, ts=256):
    S, E = c.shape
    ts = _tile(S, ts, 16)

    def body(c_ref, z_ref, g_ref, b_ref, y_ref):
        cv = c_ref[...]
        mu = jnp.mean(cv, axis=-1, keepdims=True)
        d = cv - mu
        var = jnp.mean(d * d, axis=-1, keepdims=True)
        cn = d * lax.rsqrt(var + EPS) * g_ref[...] + b_ref[...]
        z = z_ref[...].astype(F32)
        y_ref[...] = ((cn * _sigmoid(cn)).astype(F32) * (z * _sigmoid(z))).astype(BF16)

    row = pl.BlockSpec((ts, E), lambda i: (i, 0))
    vec = pl.BlockSpec((1, E), lambda i: (0, 0))
    return pl.pallas_call(
        body, name=name, grid=(S // ts,),
        in_specs=[row, pl.BlockSpec((None, ts, E), lambda i: (2, i, 0)), vec, vec],
        out_specs=row, out_shape=jax.ShapeDtypeStruct((S, E), BF16),
        compiler_params=_params("parallel"),
    )(c, proj3, ln_g, ln_b)


def _ln_gate_bwd(c, proj3, dy, ln_g, ln_b, name, ts=256):
    S, E = c.shape
    ts = _tile(S, ts, 16)

    def body(c_ref, z_ref, dy_ref, g_ref, b_ref, dc_ref, dz_ref, acc_ref):
        i = pl.program_id(0)
        cv = c_ref[...]
        g = g_ref[...]
        mu = jnp.mean(cv, axis=-1, keepdims=True)
        d = cv - mu
        var = jnp.mean(d * d, axis=-1, keepdims=True)
        rstd = lax.rsqrt(var + EPS)
        chat = d * rstd
        cn = chat * g + b_ref[...]
        z = z_ref[...].astype(F32)
        dyv = dy_ref[...].astype(F32)
        sc = _sigmoid(cn)
        sz = _sigmoid(z)
        dcn = dyv * (z * sz) * (sc * (1.0 + cn * (1.0 - sc)))
        dz_ref[...] = (dyv * (cn * sc) * (sz * (1.0 + z * (1.0 - sz)))).astype(BF16)
        dchat = dcn * g
        dcv = rstd * (dchat - jnp.mean(dchat, axis=-1, keepdims=True)
                      - chat * jnp.mean(dchat * chat, axis=-1, keepdims=True))
        dc_ref[...] = dcv
        upd = _stack_rows([jnp.sum(dcn * chat, axis=0, keepdims=True),
                           jnp.sum(dcn, axis=0, keepdims=True),
                           jnp.sum(dcv, axis=0, keepdims=True)], 8)

        @pl.when(i == 0)
        def _():
            acc_ref[...] = upd

        @pl.when(i > 0)
        def _():
            acc_ref[...] += upd

    row = pl.BlockSpec((ts, E), lambda i: (i, 0))
    vec = pl.BlockSpec((1, E), lambda i: (0, 0))
    return pl.pallas_call(
        body, name=name, grid=(S // ts,),
        in_specs=[row, pl.BlockSpec((None, ts, E), lambda i: (2, i, 0)), row, vec, vec],
        out_specs=[row, pl.BlockSpec((None, ts, E), lambda i: (2, i, 0)), pl.BlockSpec((8, E), lambda i: (0, 0))],
        out_shape=[jax.ShapeDtypeStruct((S, E), F32), jax.ShapeDtypeStruct((3, S, E), BF16),
                   jax.ShapeDtypeStruct((8, E), F32)],
        compiler_params=_params("arbitrary"),
    )(c, proj3, dy, ln_g, ln_b)


def _conv_bwd(proj3, dc, conv_w32, dproj3, name, ts=256, cw=256):
    _, S, E = proj3.shape
    ts = _tile(S, ts, HALO)
    cw = _tile(E, cw)
    per = ts // HALO
    n_i = S // ts
    last_halo = S // HALO - 1
    rc = min(CONV_ROWS, ts)

    def body(a_ref, b_ref, dc_ref, dcn_ref, w_ref, dp_in, dab_ref, dw_ref, dcbuf, ubuf, dwacc):
        del dp_in
        i = pl.program_id(1)
        dcbuf[0:ts, :] = dc_ref[...]
        dcbuf[ts:ts + HALO, :] = jnp.where(i < n_i - 1, dcn_ref[...], 0.0)
        av = a_ref[...].astype(F32)
        sb = _sigmoid(b_ref[...].astype(F32))
        ubuf[...] = av * sb

        @pl.when(i == 0)
        def _():
            dwacc[...] = jnp.zeros_like(dwacc)

        for r0 in range(0, ts, rc):
            uv = ubuf[r0:r0 + rc, :]
            du = jnp.zeros((rc, cw), F32)
            for d in range(CONV_TAPS):
                k = CONV_TAPS - 1 - d
                win = dcbuf[r0 + d:r0 + d + rc, :]
                du = du + win * w_ref[k:k + 1, :]
                dwacc[k:k + 1, :] += jnp.sum(uv * win, axis=0, keepdims=True)
            a_c = a_ref[r0:r0 + rc, :].astype(F32)
            s_c = _sigmoid(b_ref[r0:r0 + rc, :].astype(F32))
            dab_ref[0, r0:r0 + rc, :] = (du * s_c).astype(BF16)
            dab_ref[1, r0:r0 + rc, :] = (du * a_c * s_c * (1.0 - s_c)).astype(BF16)

        @pl.when(i == n_i - 1)
        def _():
            dw_ref[...] = dwacc[...]

    return pl.pallas_call(
        body, name=name, grid=(E // cw, n_i),
        in_specs=[
            pl.BlockSpec((None, ts, cw), lambda j, i: (0, i, j)),
            pl.BlockSpec((None, ts, cw), lambda j, i: (1, i, j)),
            pl.BlockSpec((ts, cw), lambda j, i: (i, j)),
            pl.BlockSpec((HALO, cw), lambda j, i: (jnp.minimum((i + 1) * per, last_halo), j)),
            pl.BlockSpec((HALO, cw), lambda j, i: (0, j)),
            ANY,
        ],
        out_specs=[pl.BlockSpec((2, ts, cw), lambda j, i: (0, i, j)),
                   pl.BlockSpec((HALO, cw), lambda j, i: (0, j))],
        out_shape=[jax.ShapeDtypeStruct((3, S, E), BF16), jax.ShapeDtypeStruct((HALO, E), F32)],
        scratch_shapes=[pltpu.VMEM((ts + HALO, cw), F32), pltpu.VMEM((ts, cw), F32), pltpu.VMEM((HALO, cw), F32)],
        input_output_aliases={5: 0},
        compiler_params=_params("parallel", "arbitrary"),
    )(proj3, proj3, dc, dc, conv_w32, dproj3)


def _bucket_table(dil):
    delta = (np.arange(BLOCK)[:, None] + BLOCK) - np.arange(2 * BLOCK)[None, :]
    dist = np.clip(delta, 0, None) * dil
    large = MAX_EXACT + (np.log(np.maximum(dist, 1).astype(np.float32) / MAX_EXACT)
                         / np.log(MAX_DISTANCE / MAX_EXACT) * (N_BUCKETS - MAX_EXACT)).astype(np.int32)
    large = np.minimum(large, N_BUCKETS - 1)
    return np.where(dist < MAX_EXACT, dist, large).astype(np.int32).reshape(-1)


def _onehot(dil):
    tbl = jnp.asarray(_bucket_table(dil))
    return (tbl[None, :] == jnp.arange(LANES, dtype=jnp.int32)[:, None]).astype(BF16)


def _split3(v):
    hi = v.astype(BF16)
    r1 = v - hi.astype(F32)
    mid = r1.astype(BF16)
    lo = (r1 - mid.astype(F32)).astype(BF16)
    return hi, mid, lo


def _bias_table(rb_t, onehot, name):
    H = rb_t.shape[0]
    N = onehot.shape[1]

    def body(r_ref, oh_ref, o_ref):
        oh = oh_ref[...]
        hi, mid, lo = _split3(r_ref[...])
        o_ref[...] = (_dot(lo, oh, NN) + _dot(mid, oh, NN)) + _dot(hi, oh, NN)

    return pl.pallas_call(
        body, name=name, in_specs=[VMEM_SPEC, VMEM_SPEC], out_specs=VMEM_SPEC,
        out_shape=jax.ShapeDtypeStruct((H, N), F32),
        compiler_params=pltpu.CompilerParams(vmem_limit_bytes=VMEM_LIMIT),
    )(rb_t, onehot)


def _bias_grad(dbs, onehots, name):
    H = dbs[0].shape[0]
    n = len(dbs)

    def body(*refs):
        acc = jnp.zeros((H, LANES), F32)
        for q in range(n):
            oh = refs[n + q][...]
            hi, mid, lo = _split3(refs[q][...])
            acc = acc + ((_dot(lo, oh, NT) + _dot(mid, oh, NT)) + _dot(hi, oh, NT))
        refs[2 * n][...] = acc

    return pl.pallas_call(
        body, name=name, in_specs=[VMEM_SPEC] * (2 * n), out_specs=VMEM_SPEC,
        out_shape=jax.ShapeDtypeStruct((H, LANES), F32),
        compiler_params=pltpu.CompilerParams(vmem_limit_bytes=VMEM_LIMIT),
    )(*dbs, *onehots)


def _attn_fwd(q, kv, bias, dil, name):
    S, A = q.shape
    H = A // HEAD_DIM
    L = S // dil
    nb = L // BLOCK
    qv = q.reshape(dil, L, A)
    kvv = kv.reshape(2, dil, L, A)

    def body(q_ref, kp_ref, kc_ref, vp_ref, vc_ref, b_ref, o_ref, lse_ref):
        i = pl.program_id(1)
        qi = lax.broadcasted_iota(jnp.int32, (BLOCK, BLOCK), 0)
        ki = lax.broadcasted_iota(jnp.int32, (BLOCK, BLOCK), 1)
        mask_c = ki <= qi
        mask_p = jnp.logical_and(ki >= qi, i > 0)
        lane = lax.broadcasted_iota(jnp.int32, (BLOCK, LANES), 1)
        lse_acc = jnp.zeros((BLOCK, LANES), F32)
        for h in range(H):
            sl = slice(h * HEAD_DIM, (h + 1) * HEAD_DIM)
            qh = q_ref[:, sl]
            s_c = jnp.where(mask_c, _dot(qh, kc_ref[:, sl], NT) * SCALE + b_ref[h, :, BLOCK:], NEG)
            s_p = jnp.where(mask_p, _dot(qh, kp_ref[:, sl], NT) * SCALE + b_ref[h, :, :BLOCK], NEG)
            m = jnp.maximum(jnp.max(s_c, axis=-1, keepdims=True), jnp.max(s_p, axis=-1, keepdims=True))
            p_c = jnp.exp(s_c - m)
            p_p = jnp.exp(s_p - m)
            den = jnp.sum(p_c, axis=-1, keepdims=True) + jnp.sum(p_p, axis=-1, keepdims=True)
            acc = _dot(p_c.astype(BF16), vc_ref[:, sl], NN) + _dot(p_p.astype(BF16), vp_ref[:, sl], NN)
            o_ref[:, sl] = acc / den
            lse_acc = jnp.where(lane == h, m + jnp.log(den), lse_acc)
        lse_ref[...] = lse_acc

    def blk(slab, prev):
        if prev:
            return pl.BlockSpec((None, None, BLOCK, A), lambda r, i: (slab, r, jnp.maximum(i - 1, 0), 0))
        return pl.BlockSpec((None, None, BLOCK, A), lambda r, i: (slab, r, i, 0))

    o, lse = pl.pallas_call(
        body, name=name, grid=(dil, nb),
        in_specs=[pl.BlockSpec((None, BLOCK, A), lambda r, i: (r, i, 0)),
                  blk(0, True), blk(0, False), blk(1, True), blk(1, False),
                  pl.BlockSpec((H, BLOCK, 2 * BLOCK), lambda r, i: (0, 0, 0))],
        out_specs=[pl.BlockSpec((None, BLOCK, A), lambda r, i: (r, i, 0)),
                   pl.BlockSpec((None, BLOCK, LANES), lambda r, i: (r, i, 0))],
        out_shape=[jax.ShapeDtypeStruct((dil, L, A), F32), jax.ShapeDtypeStruct((dil, L, LANES), F32)],
        compiler_params=_params("parallel", "parallel"),
    )(qv, kvv, kvv, kvv, kvv, bias)
    return o.reshape(S, A), lse.reshape(S, LANES)


def _attn_merge(os_, lses, z, dils, name, ts=256):
    S, A = z.shape
    H = A // HEAD_DIM
    ts = _tile(S, ts, 16 * max(dils))
    n = len(os_)

    def body(*refs):
        z_ref = refs[2 * n]
        y_ref, om_ref = refs[2 * n + 1:2 * n + 3]
        lse_refs = refs[2 * n + 3:3 * n + 3]
        o_refs = refs[3 * n + 3:4 * n + 3]
        l_bufs = refs[4 * n + 3:5 * n + 3]
        lse_buf = refs[5 * n + 3]
        ls = []
        for q, d in enumerate(dils):
            if d == 1:
                ls.append(refs[n + q][...])
            else:
                _store_token_order(o_refs[q], refs[q], d)
                _store_token_order(l_bufs[q], refs[n + q], d)
                ls.append(l_bufs[q][0])
        m = ls[0]
        for q in range(1, n):
            m = jnp.maximum(m, ls[q])
        es = [jnp.exp(v - m) for v in ls]
        den = es[0]
        for q in range(1, n):
            den = den + es[q]
        alphas = [e / den for e in es]
        lse = m + jnp.log(den)
        lse_buf[0] = lse
        for q, d in enumerate(dils):
            if d == 1:
                lse_refs[q][...] = lse
            else:
                _emit_group_order(lse_refs[q], lse_buf, d, F32)
        lane = lax.broadcasted_iota(jnp.int32, (ts, LANES), 1)
        for h in range(H):
            sl = slice(h * HEAD_DIM, (h + 1) * HEAD_DIM)
            om = jnp.zeros((ts, HEAD_DIM), F32)
            for q, d in enumerate(dils):
                o_h = refs[q][:, sl] if d == 1 else o_refs[q][h]
                om = om + _lane_col(alphas[q], h, lane) * o_h
            z = z_ref[:, sl].astype(F32)
            y_ref[:, sl] = (om * (z * _sigmoid(z))).astype(BF16)
            om_ref[:, sl] = om.astype(BF16)

    row = pl.BlockSpec((ts, A), lambda i: (i, 0))
    outs = pl.pallas_call(
        body, name=name, grid=(S // ts,),
        in_specs=[_group_spec(d, ts, A) for d in dils] + [_group_spec(d, ts, LANES) for d in dils] + [row],
        out_specs=[row, row] + [_group_spec(d, ts, LANES) for d in dils],
        out_shape=[jax.ShapeDtypeStruct((S, A), BF16), jax.ShapeDtypeStruct((S, A), BF16)]
        + [_group_shape(d, S, LANES, F32) for d in dils],
        scratch_shapes=[_chunk_buf(ts, A)] * n + [_chunk_buf(ts, LANES)] * (n + 1),
        compiler_params=_params("parallel"),
    )(*[o if d == 1 else o.reshape(d, S // d, A) for o, d in zip(os_, dils)],
      *[v if d == 1 else v.reshape(d, S // d, LANES) for v, d in zip(lses, dils)], z)
    return outs[0], outs[1], [v.reshape(S, LANES) for v in outs[2:]]


def _gate_bwd(dy, om, z, dils, name, ts=256):
    S, A = dy.shape
    H = A // HEAD_DIM
    ts = _tile(S, ts, 16 * max(dils))
    n = len(dils)

    def body(*refs):
        dy_ref, om_ref, z_ref = refs[:3]
        do_refs = refs[3:3 + n]
        dh_refs = refs[3 + n:3 + 2 * n]
        dz_ref = refs[3 + 2 * n]
        do_buf, dh_buf = refs[4 + 2 * n:6 + 2 * n]
        lane = lax.broadcasted_iota(jnp.int32, (ts, LANES), 1)
        acc = jnp.zeros((ts, LANES), F32)
        for h in range(H):
            sl = slice(h * HEAD_DIM, (h + 1) * HEAD_DIM)
            dyv = dy_ref[:, sl].astype(F32)
            omv = om_ref[:, sl].astype(F32)
            zv = z_ref[:, sl].astype(F32)
            sz = _sigmoid(zv)
            dob = (dyv * (zv * sz)).astype(BF16)
            do_buf[h] = dob.astype(F32)
            for q, d in enumerate(dils):
                if d == 1:
                    do_refs[q][:, sl] = dob
            dz_ref[:, sl] = (dyv * omv * (sz * (1.0 + zv * (1.0 - sz)))).astype(BF16)
            acc = jnp.where(lane == h, jnp.sum(dob.astype(F32) * omv, axis=-1, keepdims=True), acc)
        dh_buf[0] = acc
        for q, d in enumerate(dils):
            if d == 1:
                dh_refs[q][...] = acc
            else:
                _emit_group_order(do_refs[q], do_buf, d, BF16)
                _emit_group_order(dh_refs[q], dh_buf, d, F32)

    row = pl.BlockSpec((ts, A), lambda i: (i, 0))
    outs = pl.pallas_call(
        body, name=name, grid=(S // ts,), in_specs=[row, row, row],
        out_specs=[_group_spec(d, ts, A) for d in dils] + [_group_spec(d, ts, LANES) for d in dils] + [row],
        out_shape=[_group_shape(d, S, A, BF16) for d in dils] + [_group_shape(d, S, LANES, F32) for d in dils]
        + [jax.ShapeDtypeStruct((S, A), BF16)],
        scratch_shapes=[_chunk_buf(ts, A), _chunk_buf(ts, LANES)],
        compiler_params=_params("parallel"),
    )(dy, om, z)
    return ([v.reshape(S, A) for v in outs[:n]], [v.reshape(S, LANES) for v in outs[n:2 * n]], outs[2 * n])


def _attn_bwd(q, kv, do, lse, dh, bias, dil, name):
    S, A = q.shape
    H = A // HEAD_DIM
    L = S // dil
    nb = L // BLOCK
    qv = q.reshape(dil, L, A)
    kvv = kv.reshape(2, dil, L, A)
    dov = do.reshape(dil, L, A)
    lsev = lse.reshape(dil, L, LANES)
    dhv = dh.reshape(dil, L, LANES)

    def body(*refs):
        (q_ref, qn_ref, kp_ref, kc_ref, vp_ref, vc_ref, do_ref, don_ref, l_ref, ln_ref, d_ref, dn_ref,
         b_ref) = refs[:13]
        dq_ref, dkv_ref, db_ref = refs[13:16]
        r = pl.program_id(0)
        i = pl.program_id(1)
        qi = lax.broadcasted_iota(jnp.int32, (BLOCK, BLOCK), 0)
        ki = lax.broadcasted_iota(jnp.int32, (BLOCK, BLOCK), 1)
        mask_c = ki <= qi
        band = ki >= qi
        mask_p = jnp.logical_and(band, i > 0)
        mask_n = jnp.logical_and(band, i < nb - 1)
        lane = lax.broadcasted_iota(jnp.int32, (BLOCK, LANES), 1)

        @pl.when(jnp.logical_and(r == 0, i == 0))
        def _():
            db_ref[...] = jnp.zeros_like(db_ref)

        for h in range(H):
            sl = slice(h * HEAD_DIM, (h + 1) * HEAD_DIM)
            q_i, q_n = q_ref[:, sl], qn_ref[:, sl]
            k_p, k_c = kp_ref[:, sl], kc_ref[:, sl]
            v_p, v_c = vp_ref[:, sl], vc_ref[:, sl]
            do_i, do_n = do_ref[:, sl], don_ref[:, sl]
            l_i, l_n = _lane_col(l_ref[...], h, lane), _lane_col(ln_ref[...], h, lane)
            d_i, d_n = _lane_col(d_ref[...], h, lane), _lane_col(dn_ref[...], h, lane)
            b_c = b_ref[h, :, BLOCK:]
            b_p = b_ref[h, :, :BLOCK]
            s = jnp.where(mask_c, _dot(q_i, k_c, NT) * SCALE + b_c, NEG)
            p1 = jnp.exp(s - l_i)
            ds1 = p1 * (_dot(do_i, v_c, NT) - d_i)
            ds1b = ds1.astype(BF16)
            p1b = p1.astype(BF16)
            s = jnp.where(mask_p, _dot(q_i, k_p, NT) * SCALE + b_p, NEG)
            p2 = jnp.exp(s - l_i)
            ds2 = p2 * (_dot(do_i, v_p, NT) - d_i)
            ds2b = ds2.astype(BF16)
            s = jnp.where(mask_n, _dot(q_n, k_c, NT) * SCALE + b_p, NEG)
            p3 = jnp.exp(s - l_n)
            ds3b = (p3 * (_dot(do_n, v_c, NT) - d_n)).astype(BF16)
            p3b = p3.astype(BF16)
            dq = _dot(ds1b, k_c, NN) + _dot(ds2b, k_p, NN)
            dk = _dot(ds1b, q_i, TN) + _dot(ds3b, q_n, TN)
            dv = _dot(p1b, do_i, TN) + _dot(p3b, do_n, TN)
            dq_ref[:, sl] = (dq * SCALE).astype(BF16)
            dkv_ref[0, :, sl] = (dk * SCALE).astype(BF16)
            dkv_ref[1, :, sl] = dv.astype(BF16)
            db_ref[h, :, BLOCK:] += ds1
            db_ref[h, :, :BLOCK] += ds2

    def blk(slab, shift):
        if shift < 0:
            return pl.BlockSpec((None, None, BLOCK, A), lambda r, i: (slab, r, jnp.maximum(i - 1, 0), 0))
        return pl.BlockSpec((None, None, BLOCK, A), lambda r, i: (slab, r, i, 0))

    def row(width, shift):
        if shift > 0:
            return pl.BlockSpec((None, BLOCK, width), lambda r, i: (r, jnp.minimum(i + 1, nb - 1), 0))
        return pl.BlockSpec((None, BLOCK, width), lambda r, i: (r, i, 0))

    in_specs = [row(A, 0), row(A, 1), blk(0, -1), blk(0, 0), blk(1, -1), blk(1, 0),
                row(A, 0), row(A, 1), row(LANES, 0), row(LANES, 1), row(LANES, 0), row(LANES, 1),
                pl.BlockSpec((H, BLOCK, 2 * BLOCK), lambda r, i: (0, 0, 0))]
    dq, dkv, db = pl.pallas_call(
        body, name=name, grid=(dil, nb), in_specs=in_specs,
        out_specs=[pl.BlockSpec((None, BLOCK, A), lambda r, i: (r, i, 0)),
                   pl.BlockSpec((2, None, BLOCK, A), lambda r, i: (0, r, i, 0)),
                   pl.BlockSpec((H, BLOCK, 2 * BLOCK), lambda r, i: (0, 0, 0))],
        out_shape=[jax.ShapeDtypeStruct((dil, L, A), BF16), jax.ShapeDtypeStruct((2, dil, L, A), BF16),
                   jax.ShapeDtypeStruct((H, BLOCK, 2 * BLOCK), F32)],
        compiler_params=_params("arbitrary", "arbitrary"),
    )(qv, qv, kvv, kvv, kvv, kvv, dov, dov, lsev, lsev, dhv, dhv, bias)
    return dq.reshape(S, A), dkv.reshape(2, S, A), db


def _sum_leading(stack, out_dtype, name, tr=256, tc=2048):
    n, R, C = stack.shape
    tr = _tile(R, tr, 16)
    tc = _tile(C, tc)

    def body(s_ref, o_ref):
        acc = s_ref[0].astype(F32)
        for q in range(1, n):
            acc = acc + s_ref[q].astype(F32)
        o_ref[...] = acc.astype(out_dtype)

    return pl.pallas_call(
        body, name=name, grid=(R // tr, C // tc),
        in_specs=[pl.BlockSpec((n, tr, tc), lambda i, j: (0, i, j))],
        out_specs=pl.BlockSpec((tr, tc), lambda i, j: (i, j)),
        out_shape=jax.ShapeDtypeStruct((R, C), out_dtype),
        compiler_params=_params("parallel", "parallel"),
    )(stack)


def _add_half(g, t, c_idx, kind, name, tr=256, tc=2048):
    R, C = t.shape
    tr = _tile(R, tr, 16)
    tc = _tile(C, tc)
    nrb, ncb = R // tr, C // tc

    def body(c_ref, g_ref, t_ref, o_ref):
        del c_ref
        o_ref[...] = (g_ref[...].astype(F32) + t_ref[...].astype(F32)).astype(BF16)

    if kind == "col":
        g_map = lambda i, j, c_ref: (c_ref[0] * nrb + i, j)
    else:
        g_map = lambda i, j, c_ref: (i, c_ref[0] * ncb + j)
    same = lambda i, j, c_ref: (i, j)
    return pl.pallas_call(
        body, name=name,
        grid_spec=pltpu.PrefetchScalarGridSpec(
            num_scalar_prefetch=1, grid=(nrb, ncb),
            in_specs=[pl.BlockSpec((tr, tc), g_map), pl.BlockSpec((tr, tc), same)],
            out_specs=pl.BlockSpec((tr, tc), same)),
        out_shape=jax.ShapeDtypeStruct((R, C), BF16),
        compiler_params=_params("parallel", "parallel"),
    )(c_idx, g, t)


def _cast_into_full(w, kind, chip_idx, name, tr=256, tc=2048):
    R, C = w.shape
    tr = _tile(R, tr, 16)
    tc = _tile(C, tc)
    nrb, ncb = R // tr, C // tc

    def body(k_ref, w_ref, o_ref):
        del k_ref
        o_ref[...] = w_ref[...].astype(BF16)

    if kind == "col":
        o_map = lambda i, j, k_ref: (i, k_ref[0] * ncb + j)
        full = (R, N_CHIPS * C)
    else:
        o_map = lambda i, j, k_ref: (k_ref[0] * nrb + i, j)
        full = (N_CHIPS * R, C)
    return pl.pallas_call(
        body, name=name,
        grid_spec=pltpu.PrefetchScalarGridSpec(
            num_scalar_prefetch=1, grid=(nrb, ncb),
            in_specs=[pl.BlockSpec((tr, tc), lambda i, j, k_ref: (i, j))],
            out_specs=pl.BlockSpec((tr, tc), o_map)),
        out_shape=jax.ShapeDtypeStruct(full, BF16),
        compiler_params=_params("parallel", "parallel"),
    )(chip_idx, w)


def _sum_into_shard(p, u, idx, kind, name, tr=256, tc=2048):
    _, R, C = u.shape
    tr = _tile(R, tr, 16)
    tc = _tile(C, tc)
    nrb, ncb = R // tr, C // tc

    def body(i_ref, p_ref, u_ref, o_ref):
        del i_ref
        acc = p_ref[...].astype(F32)
        for q in range(N_CHIPS - 1):
            acc = acc + u_ref[q].astype(F32)
        o_ref[...] = acc

    if kind == "col":
        p_map = lambda i, j, r: (i, r[0] * ncb + j)
        o_map = lambda i, j, r: (r[1] * nrb + i, j)
        full = (2 * R, C)
    else:
        p_map = lambda i, j, r: (r[0] * nrb + i, j)
        o_map = lambda i, j, r: (i, r[1] * ncb + j)
        full = (R, 2 * C)
    return pl.pallas_call(
        body, name=name,
        grid_spec=pltpu.PrefetchScalarGridSpec(
            num_scalar_prefetch=1, grid=(nrb, ncb),
            in_specs=[pl.BlockSpec((tr, tc), p_map), pl.BlockSpec((N_CHIPS - 1, tr, tc), lambda i, j, r: (0, i, j))],
            out_specs=pl.BlockSpec((tr, tc), o_map)),
        out_shape=jax.ShapeDtypeStruct(full, F32),
        compiler_params=_params("parallel", "parallel"),
    )(idx, p, u)


def _adamw(w, g, m, v, name, tr=256, tc=2048):
    R, C = w.shape
    tr = _tile(R, tr, 8)
    tc = _tile(C, tc)
    c1 = 1.0 - ADAM_B1 ** ADAM_STEP
    c2 = 1.0 - ADAM_B2 ** ADAM_STEP

    def body(w_ref, g_ref, m_ref, v_ref, d_ref, nm_ref, nv_ref):
        gv = g_ref[...]
        nm = ADAM_B1 * m_ref[...] + (1.0 - ADAM_B1) * gv
        nv = ADAM_B2 * v_ref[...] + (1.0 - ADAM_B2) * (gv * gv)
        d_ref[...] = -ADAM_LR * ((nm / c1) / (jnp.sqrt(nv / c2) + ADAM_EPS) + ADAM_WD * w_ref[...])
        nm_ref[...] = nm
        nv_ref[...] = nv

    blk = pl.BlockSpec((tr, tc), lambda i, j: (i, j))
    sh = jax.ShapeDtypeStruct((R, C), F32)
    return pl.pallas_call(
        body, name=name, grid=(R // tr, C // tc), in_specs=[blk] * 4, out_specs=[blk] * 3,
        out_shape=[sh, sh, sh], compiler_params=_params("parallel", "parallel"),
    )(w, g, m, v)


def _xyc():
    return lax.axis_index("x"), lax.axis_index("y"), lax.axis_index("c")


def _drain(copies):
    for cp in copies:
        if cp.is_remote:
            cp.wait_send()
        else:
            cp.wait()


def _other_chips(x, y):
    return [(1 - x, y), (x, 1 - y), (1 - x, 1 - y)]


def _allgather_small(blk, name, after=None):
    R, C = blk.shape
    extra = [] if after is None else [after]

    def body(*refs):
        x_ref = refs[0]
        out_ref, send_sems, recv_sems, local_sem = refs[1 + len(extra):]
        x, y, c = _xyc()
        me = 4 * x + 2 * y + c
        mine = pltpu.make_async_copy(x_ref, out_ref.at[me], local_sem)
        mine.start()
        peers = []
        for k in range(1, N_DEV):
            px = 1 - x if (k >> 2) & 1 else x
            py = 1 - y if (k >> 1) & 1 else y
            pc = 1 - c if k & 1 else c
            peers.append((px, py, pc))
        sends = []
        for k, peer in enumerate(peers):
            cp = pltpu.make_async_remote_copy(
                src_ref=x_ref, dst_ref=out_ref.at[me], send_sem=send_sems.at[k], recv_sem=recv_sems.at[k],
                device_id=peer, device_id_type=MESH)
            cp.start()
            sends.append(cp)
        for k, (px, py, pc) in enumerate(peers):
            pltpu.make_async_remote_copy(
                src_ref=x_ref, dst_ref=out_ref.at[4 * px + 2 * py + pc], send_sem=send_sems.at[k],
                recv_sem=recv_sems.at[k], device_id=(px, py, pc), device_id_type=MESH).wait_recv()
        for cp in sends:
            cp.wait_send()
        mine.wait()

    return pl.pallas_call(
        body, name=name, in_specs=[VMEM_SPEC] + [ANY] * len(extra), out_specs=VMEM_SPEC,
        out_shape=jax.ShapeDtypeStruct((N_DEV, R, C), blk.dtype),
        scratch_shapes=[pltpu.SemaphoreType.DMA((N_DEV - 1,)), pltpu.SemaphoreType.DMA((N_DEV - 1,)),
                        pltpu.SemaphoreType.DMA],
        compiler_params=pltpu.CompilerParams(vmem_limit_bytes=VMEM_LIMIT),
    )(blk, *extra)


def _full_region(ref, kind, chip, half, shard_shape):
    r, cn = shard_shape
    hr = r // 2
    if kind == "col":
        rows = pl.ds(0, r) if half is None else pl.ds(pl.multiple_of(half * hr, 16), hr)
        return ref.at[rows, pl.ds(pl.multiple_of(chip * cn, LANES), cn)]
    if half is None:
        return ref.at[pl.ds(pl.multiple_of(chip * r, 16), r), :]
    return ref.at[pl.ds(pl.multiple_of(chip * r + half * hr, 16), hr), :]


def _allgather_weights(fulls, kinds, shapes, name):
    n = len(fulls)

    def body(*refs):
        outs = refs[n:2 * n]
        send_sems, recv_sems = refs[2 * n:]
        x, y, c = _xyc()
        chip = 2 * x + y
        sib = (x, y, 1 - c)
        others = _other_chips(x, y)
        started = []
        for w in range(n):
            mine = _full_region(outs[w], kinds[w], chip, c, shapes[w])
            for j, (ox, oy) in enumerate(others):
                cp = pltpu.make_async_remote_copy(
                    src_ref=mine, dst_ref=mine, send_sem=send_sems.at[6 * w + j], recv_sem=recv_sems.at[6 * w + j],
                    device_id=(ox, oy, c), device_id_type=MESH)
                cp.start()
                started.append(cp)
        for w in range(n):
            for j, (ox, oy) in enumerate(others):
                landed = _full_region(outs[w], kinds[w], 2 * ox + oy, c, shapes[w])
                pltpu.make_async_remote_copy(
                    src_ref=landed, dst_ref=landed, send_sem=send_sems.at[6 * w + j], recv_sem=recv_sems.at[6 * w + j],
                    device_id=(ox, oy, c), device_id_type=MESH).wait_recv()
                cp = pltpu.make_async_remote_copy(
                    src_ref=landed, dst_ref=landed, send_sem=send_sems.at[6 * w + 3 + j],
                    recv_sem=recv_sems.at[6 * w + 3 + j], device_id=sib, device_id_type=MESH)
                cp.start()
                started.append(cp)
        for w in range(n):
            for j, (ox, oy) in enumerate(others):
                theirs = _full_region(outs[w], kinds[w], 2 * ox + oy, 1 - c, shapes[w])
                pltpu.make_async_remote_copy(
                    src_ref=theirs, dst_ref=theirs, send_sem=send_sems.at[6 * w + 3 + j],
                    recv_sem=recv_sems.at[6 * w + 3 + j], device_id=sib, device_id_type=MESH).wait_recv()
        _drain(started)

    return pl.pallas_call(
        body, name=name, in_specs=[ANY] * n, out_specs=[ANY] * n,
        out_shape=[jax.ShapeDtypeStruct(f.shape, f.dtype) for f in fulls],
        input_output_aliases={w: w for w in range(n)},
        scratch_shapes=[pltpu.SemaphoreType.DMA((6 * n,)), pltpu.SemaphoreType.DMA((6 * n,))],
    )(*fulls)


def _region_of_size(ref, kind, shard_shape, count):
    r, cn = shard_shape
    if kind == "col":
        return ref.at[pl.ds(0, r // 2), pl.ds(0, count * cn)]
    return ref.at[pl.ds(0, count * (r // 2)), :]


def _allgather_weights_seq(fulls, kinds, shapes, name, collective_id):
    n = len(fulls)
    refs = [jax.new_ref(f, memory_space=pltpu.MemorySpace.HBM) for f in fulls]

    def body(send_sems, recv_sems):
        x, y, c = _xyc()
        chip = 2 * x + y
        sib = (x, y, 1 - c)
        others = _other_chips(x, y)
        peers = [(ox, oy, c) for ox, oy in others] + [sib]
        barrier = pltpu.get_barrier_semaphore()
        for peer in peers:
            pl.semaphore_signal(barrier, inc=1, device_id=peer, device_id_type=MESH)
        pl.semaphore_wait(barrier, len(peers))

        def copy(w, region, sem, to):
            return pltpu.make_async_remote_copy(src_ref=region, dst_ref=region, send_sem=send_sems.at[sem],
                                                recv_sem=recv_sems.at[sem], device_id=to, device_id_type=MESH)

        for w in range(n):
            mine = _full_region(refs[w], kinds[w], chip, c, shapes[w])
            for ox, oy in others:
                copy(w, mine, 2 * w, (ox, oy, c)).start()
        for w in range(n):
            three = _region_of_size(refs[w], kinds[w], shapes[w], 3)
            copy(w, three, 2 * w, sib).wait_recv()
            for ox, oy in others:
                copy(w, _full_region(refs[w], kinds[w], 2 * ox + oy, c, shapes[w]), 2 * w + 1, sib).start()
        for w in range(n):
            three = _region_of_size(refs[w], kinds[w], shapes[w], 3)
            copy(w, three, 2 * w + 1, sib).wait_recv()
            copy(w, three, 2 * w, sib).wait_send()
            copy(w, three, 2 * w + 1, sib).wait_send()

    pl.kernel(
        body, out_type=(), mesh=plsc.ScalarSubcoreMesh(axis_name="seq", num_cores=1), name=name,
        scratch_types=[pltpu.SemaphoreType.DMA((2 * n,)), pltpu.SemaphoreType.DMA((2 * n,))],
        compiler_params=pltpu.CompilerParams(collective_id=collective_id),
    )()
    return [r[...] for r in refs]


def _half_of(ref, kind, half):
    r, cn = ref.shape
    if kind == "col":
        return ref.at[pl.ds(pl.multiple_of(half * (r // 2), 16), r // 2), :]
    return ref.at[:, pl.ds(pl.multiple_of(half * (cn // 2), LANES), cn // 2)]


def _shard_of(ref, kind, chip):
    r, cn = ref.shape
    if kind == "col":
        return ref.at[:, pl.ds(pl.multiple_of(chip * (cn // N_CHIPS), LANES), cn // N_CHIPS)]
    return ref.at[pl.ds(pl.multiple_of(chip * (r // N_CHIPS), 16), r // N_CHIPS), :]


def _exchange_halves(grads, kinds, name):
    n = len(grads)

    def body(*refs):
        gs = refs[:n]
        ts = refs[n:2 * n]
        send_sems, recv_sems = refs[2 * n:]
        x, y, c = _xyc()
        cps = []
        for w in range(n):
            cp = pltpu.make_async_remote_copy(
                src_ref=_half_of(gs[w], kinds[w], 1 - c), dst_ref=ts[w], send_sem=send_sems.at[w],
                recv_sem=recv_sems.at[w], device_id=(x, y, 1 - c), device_id_type=MESH)
            cp.start()
            cps.append(cp)
        for cp in cps:
            cp.wait()

    out_shape = []
    for gr, kind in zip(grads, kinds):
        r, cn = gr.shape
        out_shape.append(jax.ShapeDtypeStruct((r // 2, cn) if kind == "col" else (r, cn // 2), gr.dtype))
    return pl.pallas_call(
        body, name=name, in_specs=[ANY] * n, out_specs=[ANY] * n, out_shape=out_shape,
        scratch_shapes=[pltpu.SemaphoreType.DMA((n,)), pltpu.SemaphoreType.DMA((n,))],
    )(*grads)


def _scatter_partials(parts, kinds, name):
    n = len(parts)

    def body(*refs):
        ps = refs[:n]
        us = refs[n:2 * n]
        send_sems, recv_sems = refs[2 * n:]
        x, y, c = _xyc()
        others = _other_chips(x, y)
        cps = []
        for w in range(n):
            for j, (ox, oy) in enumerate(others):
                cp = pltpu.make_async_remote_copy(
                    src_ref=_shard_of(ps[w], kinds[w], 2 * ox + oy), dst_ref=us[w].at[j],
                    send_sem=send_sems.at[3 * w + j], recv_sem=recv_sems.at[3 * w + j],
                    device_id=(ox, oy, c), device_id_type=MESH)
                cp.start()
                cps.append(cp)
        for cp in cps:
            cp.wait()

    out_shape = []
    for p, kind in zip(parts, kinds):
        r, cn = p.shape
        hs = (r, cn // N_CHIPS) if kind == "col" else (r // N_CHIPS, cn)
        out_shape.append(jax.ShapeDtypeStruct((N_CHIPS - 1,) + hs, p.dtype))
    return pl.pallas_call(
        body, name=name, in_specs=[ANY] * n, out_specs=[ANY] * n, out_shape=out_shape,
        scratch_shapes=[pltpu.SemaphoreType.DMA((3 * n,)), pltpu.SemaphoreType.DMA((3 * n,))],
    )(*parts)


def _scatter_partials_seq(parts, kinds, name, collective_id):
    n = len(parts)

    def body(*refs):
        ps = refs[:n]
        us = refs[n:2 * n]
        send_sems, recv_sems = refs[2 * n:]
        x, y, c = _xyc()
        others = _other_chips(x, y)
        barrier = pltpu.get_barrier_semaphore()
        for ox, oy in others:
            pl.semaphore_signal(barrier, inc=1, device_id=(ox, oy, c), device_id_type=MESH)
        pl.semaphore_wait(barrier, len(others))
        for w in range(n):
            for j, (ox, oy) in enumerate(others):
                pltpu.make_async_remote_copy(
                    src_ref=_shard_of(ps[w], kinds[w], 2 * ox + oy), dst_ref=us[w].at[j],
                    send_sem=send_sems.at[w], recv_sem=recv_sems.at[w],
                    device_id=(ox, oy, c), device_id_type=MESH).start()
        for w in range(n):
            pltpu.make_async_remote_copy(
                src_ref=us[w], dst_ref=us[w], send_sem=send_sems.at[w], recv_sem=recv_sems.at[w],
                device_id=(x, y, c), device_id_type=MESH).wait()

    out_type = []
    for p, kind in zip(parts, kinds):
        r, cn = p.shape
        hs = (r, cn // N_CHIPS) if kind == "col" else (r // N_CHIPS, cn)
        out_type.append(jax.ShapeDtypeStruct((N_CHIPS - 1,) + hs, p.dtype))
    return pl.kernel(
        body, out_type=out_type, mesh=plsc.ScalarSubcoreMesh(axis_name="seq", num_cores=1), name=name,
        scratch_types=[pltpu.SemaphoreType.DMA((n,)), pltpu.SemaphoreType.DMA((n,))],
        compiler_params=pltpu.CompilerParams(collective_id=collective_id),
    )(*parts)


def _join_halves(halves, kinds, name):
    n = len(halves)

    def body(*refs):
        outs = refs[n:2 * n]
        send_sems, recv_sems = refs[2 * n:]
        x, y, c = _xyc()
        cps = []
        for w in range(n):
            mine = _half_of(outs[w], kinds[w], c)
            cp = pltpu.make_async_remote_copy(
                src_ref=mine, dst_ref=mine, send_sem=send_sems.at[w], recv_sem=recv_sems.at[w],
                device_id=(x, y, 1 - c), device_id_type=MESH)
            cp.start()
            cps.append(cp)
        for w in range(n):
            theirs = _half_of(outs[w], kinds[w], 1 - c)
            pltpu.make_async_remote_copy(
                src_ref=theirs, dst_ref=theirs, send_sem=send_sems.at[w], recv_sem=recv_sems.at[w],
                device_id=(x, y, 1 - c), device_id_type=MESH).wait_recv()
        _drain(cps)

    return pl.pallas_call(
        body, name=name, in_specs=[ANY] * n, out_specs=[ANY] * n,
        out_shape=[jax.ShapeDtypeStruct(h.shape, h.dtype) for h in halves],
        input_output_aliases={w: w for w in range(n)},
        scratch_shapes=[pltpu.SemaphoreType.DMA((n,)), pltpu.SemaphoreType.DMA((n,))],
    )(*halves)


def kernel(x, a_norm, a_w_in, a_conv_w, a_conv_b, a_ln_g, a_ln_b, a_w_out, kv_norm, w_kv, b_norm, b_w_in, b_w_out, rel_bias, final_norm, loss_target, m_a_norm, m_a_w_in, m_a_conv_w, m_a_conv_b, m_a_ln_g, m_a_ln_b, m_a_w_out, m_kv_norm, m_w_kv, m_b_norm, m_b_w_in, m_b_w_out, m_rel_bias, m_final_norm, v_a_norm, v_a_w_in, v_a_conv_w, v_a_conv_b, v_a_ln_g, v_a_ln_b, v_a_w_out, v_kv_norm, v_w_kv, v_b_norm, v_b_w_in, v_b_w_out, v_rel_bias, v_final_norm):
    S, D = x.shape[1], x.shape[2]
    E = a_w_out.shape[1] * N_CHIPS
    A = b_w_out.shape[1] * N_CHIPS
    H = A // HEAD_DIM
    DC = D // N_CHIPS
    xs = x.reshape(S, D)
    tgt = loss_target.reshape(S, D)
    cx, cy, cc = _xyc()
    chip = 2 * cx + cy
    c_idx = jnp.reshape(cc, (1,)).astype(jnp.int32)

    big_names = ["a_w_in", "a_w_out", "w_kv", "b_w_in", "b_w_out"]
    kinds = ["col", "row", "col", "col", "row"]
    big_w = [a_w_in[0], a_w_out[0], w_kv, b_w_in[0], b_w_out[0]]
    big_m = [m_a_w_in[0], m_a_w_out[0], m_w_kv, m_b_w_in[0], m_b_w_out[0]]
    big_v = [v_a_w_in[0], v_a_w_out[0], v_w_kv, v_b_w_in[0], v_b_w_out[0]]
    chip_idx = jnp.reshape(chip, (1,)).astype(jnp.int32)
    placed = [_cast_into_full(big_w[w], kinds[w], chip_idx, "cast_" + big_names[w]) for w in range(5)]
    shard_shapes = [w.shape for w in big_w]
    (wa_in,) = _allgather_weights_seq(placed[0:1], kinds[0:1], shard_shapes[0:1], "ag_seq_a_in", 0)
    wa_out, wkv = _allgather_weights_seq(placed[1:3], kinds[1:3], shard_shapes[1:3], "ag_seq_a_out_kv", 1)
    wb_in, wb_out = _allgather_weights_seq(placed[3:5], kinds[3:5], shard_shapes[3:5], "ag_seq_b", 2)

    def row_at(vec, q):
        return jnp.pad(vec, ((q, 7 - q), (0, 0)))

    def pack_sharded(an, cw, cb, lg, lb):
        return jnp.concatenate([row_at(an, 0), jnp.pad(cw[0], ((0, 1), (0, 0))),
                                row_at(lg, 0) + row_at(lb, 1) + row_at(cb, 2)], axis=0)

    small_w = pack_sharded(a_norm, a_conv_w, a_conv_b, a_ln_g, a_ln_b)
    gathered = _allgather_small(small_w, "ag_small_params")
    small_full = jnp.concatenate([gathered[2 * k] for k in range(N_CHIPS)], axis=1)
    g_a = small_full[0:1]
    conv_w32 = small_full[8:8 + HALO]
    ln_g = small_full[40:41]
    ln_b = small_full[41:42]
    conv_b = small_full[42:43]
    g_kv = kv_norm.reshape(1, D)
    g_b = b_norm.reshape(1, D)
    g_f = final_norm.reshape(1, D)

    rb_t = jnp.pad(rel_bias.T, ((0, 0), (0, LANES - N_BUCKETS)))
    onehots = [_onehot(dil) for _, dil in GROUPS]
    biases = [_bias_table(rb_t, onehots[g], "bias_table_%d" % g).reshape(H, BLOCK, 2 * BLOCK)
              for g in range(len(GROUPS))]

    dils = tuple(dil for _, dil in GROUPS)
    assert dils[0] == 1
    n_g = len(GROUPS)
    ((h0,),) = _rms_fwd(xs, [g_a], (1,), "rms_a")
    proj3 = _matmul(h0, wa_in, "nn", BF16, "mm_a_in", out_slab=E)
    conv = _conv_fwd(proj3, conv_w32, conv_b, "conv_fwd")
    y_a = _ln_gate_fwd(conv, proj3, ln_g, ln_b, "ln_gate_fwd")
    x1 = _matmul(y_a, wa_out, "nn", F32, "mm_a_out", res=xs)
    hks, hbs = _rms_fwd(x1, [g_kv, g_b], dils, "rms_kv_b")
    kvs = [_matmul(hks[g], wkv, "nn", BF16, "mm_kv_%d" % g, out_slab=A, b_off=2 * g * A, n_cols=2 * A)
           for g in range(n_g)]
    qs = [_matmul(hbs[g], wb_in, "nn", BF16, "mm_q_%d" % g, b_off=g * A, n_cols=A, after=kvs[-1])
          for g in range(n_g)]
    zb = _matmul(hbs[0], wb_in, "nn", BF16, "mm_zb", b_off=n_g * A, n_cols=A, after=kvs[-1])
    os_, lses = [], []
    for g, dil in enumerate(dils):
        o_g, lse_g = _attn_fwd(qs[g], kvs[g], biases[g], dil, "attn_fwd_%d" % g)
        os_.append(o_g)
        lses.append(lse_g)
    y_b, o_m, lse_d = _attn_merge(os_, lses, zb, dils, "attn_merge")
    x2 = _matmul(y_b, wb_out, "nn", F32, "mm_b_out", res=x1)
    loss_part, dx2, dx2b, gg_f = _final_head(x2, g_f, tgt, "final_head")
    loss = lax.psum(loss_part[0, 0], ("x", "y", "c"))

    dw_tiles = dict(tm=512, tn=1024, tk=4096)
    dy_b = _matmul(dx2b, wb_out, "nt", BF16, "mm_b_out_dx")
    dwb_out = _matmul(y_b, dx2b, "tn", BF16, "mm_b_out_dw", **dw_tiles)
    dos, dhs, dzb = _gate_bwd(dy_b, o_m, zb, dils, "gate_bwd")
    dbs, cots = [], []
    dwb_in = dwkv = None
    for g, dil in enumerate(dils):
        dq, dkv, db = _attn_bwd(qs[g], kvs[g], dos[g], lse_d[g], dhs[g], biases[g], dil, "attn_bwd_%d" % g)
        dbs.append(db.reshape(H, BLOCK * 2 * BLOCK))
        dwb_in = _matmul(hbs[g], dq, "tn", BF16, "mm_q_dw_%d" % g, out_off=g * A, out_cols=(n_g + 1) * A,
                         out_alias=dwb_in, **dw_tiles)
        dwkv = _matmul(hks[g], dkv, "tn", BF16, "mm_kv_dw_%d" % g, b_slab=True, out_off=2 * g * A,
                       out_cols=2 * n_g * A, out_alias=dwkv, **dw_tiles)
        cots.append((_matmul(dkv, wkv, "nt", BF16, "mm_kv_dx_%d" % g, a_slab=True, b_off=2 * g * A), 0, dil))
        cots.append((_matmul(dq, wb_in, "nt", BF16, "mm_q_dx_%d" % g, b_off=g * A), 1, dil))
    dwb_in = _matmul(hbs[0], dzb, "tn", BF16, "mm_zb_dw", out_off=n_g * A, out_cols=(n_g + 1) * A,
                     out_alias=dwb_in, **dw_tiles)
    cots.append((_matmul(dzb, wb_in, "nt", BF16, "mm_zb_dx", b_off=n_g * A), 1, 1))
    chip_c = jnp.stack([chip, cc]).astype(jnp.int32)

    def scatter_group(idx, grads, tag, collective_id):
        ks = [kinds[w] for w in idx]
        theirs = _exchange_halves(grads, ks, "rs_exchange_" + tag)
        parts = [_add_half(grads[q], theirs[q], c_idx, ks[q], "rs_add_half_%d" % w) for q, w in enumerate(idx)]
        return parts, _scatter_partials_seq(parts, ks, "rs_seq_" + tag, collective_id)

    def reduce_group(idx, parts, slots, tag):
        ks = [kinds[w] for w in idx]
        halves = [_sum_into_shard(parts[q], slots[q], chip_c, ks[q], "rs_sum_chips_%d" % w)
                  for q, w in enumerate(idx)]
        return _join_halves(halves, ks, "rs_join_" + tag)

    parts_b, slots_b = scatter_group([2, 3, 4], [dwkv, dwb_in, dwb_out], "b", 3)
    g_rel_t = _bias_grad(dbs, onehots, "bias_grad")
    dx1, dx1b, gg_kvb = _rms_bwd(x1, cots, [g_kv, g_b], dx2, "rms_kv_b_bwd", after=parts_b[0])
    dy_a = _matmul(dx1b, wa_out, "nt", BF16, "mm_a_out_dx")
    dwa_out = _matmul(y_a, dx1b, "tn", BF16, "mm_a_out_dw", **dw_tiles)
    dconv, dproj3, gg_ln = _ln_gate_bwd(conv, proj3, dy_a, ln_g, ln_b, "ln_gate_bwd")
    dproj3, g_conv_w = _conv_bwd(proj3, dconv, conv_w32, dproj3, "conv_bwd")
    dwa_in = _matmul(h0, dproj3, "tn", BF16, "mm_a_in_dw", b_slab=True, **dw_tiles)
    parts_a, slots_a = scatter_group([0, 1], [dwa_in, dwa_out], "a", 4)
    dh0 = _matmul(dproj3, wa_in, "nt", BF16, "mm_a_in_dx", a_slab=True, after=parts_a[0])
    grad_x, _, gg_a = _rms_bwd(xs, [(dh0, 0, 1)], [g_a], dx1, "rms_a_bwd")

    big_g = [None] * 5
    big_g[2:5] = reduce_group([2, 3, 4], parts_b, slots_b, "b")
    big_g[0:2] = reduce_group([0, 1], parts_a, slots_a, "a")

    def rel_rows(rb):
        return jnp.pad(rb.reshape(1, N_BUCKETS * H), ((0, 7), (0, D - N_BUCKETS * H)))

    small_g = jnp.concatenate([gg_a, g_conv_w, gg_ln, gg_kvb, gg_f, rel_rows(g_rel_t[:, :N_BUCKETS].T)], axis=0)
    small_sum = _sum_leading(_allgather_small(small_g, "ag_small_grads", after=slots_a[0]), F32,
                             "sum_small_grads", tr=72)
    g_sharded = lax.dynamic_slice(small_sum, (0, chip * DC), (48, DC))
    g_repl = small_sum[48:72]

    outs_g, outs_d, outs_m, outs_v = {}, {}, {}, {}
    for w, nm in enumerate(big_names):
        d_, m_, v_ = _adamw(big_w[w], big_g[w], big_m[w], big_v[w], "adamw_" + nm)
        outs_g[nm], outs_d[nm], outs_m[nm], outs_v[nm] = big_g[w], d_, m_, v_
    sm_m = pack_sharded(m_a_norm, m_a_conv_w, m_a_conv_b, m_a_ln_g, m_a_ln_b)
    sm_v = pack_sharded(v_a_norm, v_a_conv_w, v_a_conv_b, v_a_ln_g, v_a_ln_b)
    sd, smm, svv = _adamw(small_w, g_sharded, sm_m, sm_v, "adamw_small_sharded")

    def unpack_sharded(p):
        return {"a_norm": p[0:1], "a_conv_w": p[8:8 + CONV_TAPS].reshape(1, CONV_TAPS, DC), "a_ln_g": p[40:41],
                "a_ln_b": p[41:42], "a_conv_b": p[42:43]}

    for src, dst in ((g_sharded, outs_g), (sd, outs_d), (smm, outs_m), (svv, outs_v)):
        dst.update(unpack_sharded(src))

    def pack_repl(kn, bn, fn, rb):
        return jnp.concatenate([row_at(kn.reshape(1, D), 0) + row_at(bn.reshape(1, D), 1),
                                row_at(fn.reshape(1, D), 0), rel_rows(rb)], axis=0)

    rp_w = pack_repl(kv_norm, b_norm, final_norm, rel_bias)
    rp_m = pack_repl(m_kv_norm, m_b_norm, m_final_norm, m_rel_bias)
    rp_v = pack_repl(v_kv_norm, v_b_norm, v_final_norm, v_rel_bias)
    rd, rmm, rvv = _adamw(rp_w, g_repl, rp_m, rp_v, "adamw_small_replicated")

    def unpack_repl(p):
        return {"kv_norm": p[0], "b_norm": p[1:2], "final_norm": p[8],
                "rel_bias": p[16, :N_BUCKETS * H].reshape(N_BUCKETS, H)}

    for src, dst in ((g_repl, outs_g), (rd, outs_d), (rmm, outs_m), (rvv, outs_v)):
        dst.update(unpack_repl(src))

    order = ["a_norm", "a_w_in", "a_conv_w", "a_conv_b", "a_ln_g", "a_ln_b", "a_w_out", "kv_norm", "w_kv",
             "b_norm", "b_w_in", "b_w_out", "rel_bias", "final_norm"]
    lead = {"a_w_in", "a_w_out", "b_w_in", "b_w_out"}

    def shaped(nm, val):
        return val[None] if nm in lead else val

    result = [loss, grad_x.reshape(1, S, D)]
    for table in (outs_g, outs_d, outs_m, outs_v):
        result.extend(shaped(nm, table[nm]) for nm in order)
    return tuple(result)
```

```python
import functools

import numpy as np
import jax
import jax.numpy as jnp
from jax import lax
from jax.experimental import pallas as pl
from jax.experimental.pallas import tpu as pltpu
from jax.experimental.pallas import tpu_sc as plsc

F32 = jnp.float32
BF16 = jnp.bfloat16
MESH = pl.DeviceIdType.MESH
ANY = pl.BlockSpec(memory_space=pl.ANY)
VMEM_SPEC = pl.BlockSpec(memory_space=pltpu.VMEM)

EPS = 1e-6
HEAD_DIM = 128
BLOCK = 128
GROUPS = ((128, 1), (512, 4), (2048, 16))
SCALE = HEAD_DIM ** -0.5
CONV_TAPS = 31
HALO = 32
N_BUCKETS = 32
MAX_EXACT = 16
MAX_DISTANCE = 2048
NEG = -1e30
PRODUCTS_AHEAD = 2
SCORES_AHEAD = 3
N_CHIPS = 4
N_DEV = 8
LANES = 128
VMEM_LIMIT = 56 * 1024 * 1024

ADAM_LR = 0.001
ADAM_B1 = 0.9
ADAM_B2 = 0.999
ADAM_EPS = 1e-08
ADAM_WD = 0.01
ADAM_STEP = 10


def _tile(n, pref, mult=LANES):
    t = (min(pref, n) // mult) * mult
    while t >= mult:
        if n % t == 0:
            return t
        t -= mult
    return n


def _params(*sem):
    return pltpu.CompilerParams(dimension_semantics=sem, vmem_limit_bytes=VMEM_LIMIT)


def _sigmoid(v):
    return 1.0 / (1.0 + jnp.exp(-v))


def _dot(a, b, dims):
    return lax.dot_general(a, b, (dims, ((), ())), preferred_element_type=F32)


NN = ((1,), (0,))
NT = ((1,), (1,))
TN = ((0,), (0,))


def _as_list(after):
    if after is None:
        return []
    return list(after) if isinstance(after, (list, tuple)) else [after]


def _stack_rows(rows, total):
    width = rows[0].shape[1]
    rid = lax.broadcasted_iota(jnp.int32, (total, width), 0)
    out = jnp.zeros((total, width), F32)
    for q, row in enumerate(rows):
        out = jnp.where(rid == q, jnp.broadcast_to(row, (total, width)), out)
    return out


def _lane_col(arr, h, lane):
    return jnp.sum(jnp.where(lane == h, arr, 0.0), axis=-1, keepdims=True)


def _matmul(a, b, mode, out_dtype, name, res=None, a_slab=False, b_slab=False, out_slab=0,
            b_off=0, n_cols=None, out_off=0, out_cols=None, out_alias=None, after=None,
            tm=512, tn=1024, tk=2048):
    if a_slab:
        na, M, W = a.shape
        K = na * W
    elif mode == "tn":
        K, M = a.shape
    else:
        M, K = a.shape
    if b_slab:
        nbs, _, Wb = b.shape
        N = nbs * Wb
    elif mode == "nt":
        N = b.shape[0]
    else:
        N = n_cols if n_cols else b.shape[1]
    tm = _tile(M, tm)
    tn = _tile(Wb if b_slab else (out_slab if out_slab else N), tn)
    tk = _tile(W if a_slab else K, tk)
    nk = K // tk
    grid = (M // tm, N // tn, nk)
    bo = b_off // (tk if mode == "nt" else tn)
    oo = out_off // tn

    if a_slab:
        per = W // tk
        a_spec = pl.BlockSpec((None, tm, tk), lambda i, j, k: (k // per, i, k % per))
    elif mode == "tn":
        a_spec = pl.BlockSpec((tk, tm), lambda i, j, k: (k, i))
    else:
        a_spec = pl.BlockSpec((tm, tk), lambda i, j, k: (i, k))
    if b_slab:
        perb = Wb // tn
        b_spec = pl.BlockSpec((None, tk, tn), lambda i, j, k: (j // perb, k, j % perb))
    elif mode == "nt":
        b_spec = pl.BlockSpec((tn, tk), lambda i, j, k: (j, k + bo))
    else:
        b_spec = pl.BlockSpec((tk, tn), lambda i, j, k: (k, j + bo))
    if out_slab:
        pero = out_slab // tn
        o_spec = pl.BlockSpec((None, tm, tn), lambda i, j, k: (j // pero, i, j % pero))
        out_shape = jax.ShapeDtypeStruct((N // out_slab, M, out_slab), out_dtype)
    else:
        o_spec = pl.BlockSpec((tm, tn), lambda i, j, k: (i, j + oo))
        out_shape = jax.ShapeDtypeStruct((M, out_cols if out_cols else N), out_dtype)
    in_specs = [a_spec, b_spec]
    operands = [a, b]
    if res is not None:
        in_specs.append(pl.BlockSpec((tm, tn), lambda i, j, k: (i, j)))
        operands.append(res)
    aliases = {}
    if out_alias is not None:
        aliases[len(operands)] = 0
        in_specs.append(ANY)
        operands.append(out_alias)
    for arr in _as_list(after):
        in_specs.append(ANY)
        operands.append(arr)
    dims = {"nn": NN, "nt": NT, "tn": TN}[mode]
    has_res = res is not None
    n_in = len(operands)

    def body(*refs):
        a_ref, b_ref = refs[0], refs[1]
        r_ref = refs[2] if has_res else None
        o_ref = refs[n_in]
        prod = _dot(a_ref[...], b_ref[...], dims)

        def finish(val):
            if has_res:
                val = val + r_ref[...]
            o_ref[...] = val.astype(out_dtype)

        if nk == 1:
            finish(prod)
        else:
            acc_ref = refs[n_in + 1]
            k = pl.program_id(2)

            @pl.when(k == 0)
            def _():
                acc_ref[...] = prod

            @pl.when(k > 0)
            def _():
                acc_ref[...] += prod

            @pl.when(k == nk - 1)
            def _():
                finish(acc_ref[...])

    scratch = [pltpu.VMEM((tm, tn), F32)] if nk > 1 else []
    return pl.pallas_call(
        body, name=name, grid=grid, in_specs=in_specs, out_specs=o_spec, out_shape=out_shape,
        scratch_shapes=scratch, input_output_aliases=aliases,
        compiler_params=_params("parallel", "parallel", "arbitrary"),
    )(*operands)


def _group_spec(d, ts, width):
    if d == 1:
        return pl.BlockSpec((ts, width), lambda i: (i, 0))
    return pl.BlockSpec((d, ts // d, width), lambda i: (0, i, 0))


def _group_shape(d, S, width, dtype):
    return jax.ShapeDtypeStruct((S, width) if d == 1 else (d, S // d, width), dtype)


def _chunk_buf(ts, width):
    return pltpu.VMEM((width // LANES, ts, LANES), F32)


def _fill_chunks(buf, val):
    for c in range(buf.shape[0]):
        buf[c] = val[:, c * LANES:(c + 1) * LANES]


def _read_chunks(buf):
    return jnp.concatenate([buf[c] for c in range(buf.shape[0])], axis=1)


def _emit_group_order(o_ref, buf, d, dtype):
    n = buf.shape[1] // d
    for r in range(d):
        for c in range(buf.shape[0]):
            o_ref[r, :, c * LANES:(c + 1) * LANES] = buf[c, pl.ds(r, n, stride=d), :].astype(dtype)


def _store_token_order(buf, i_ref, d):
    n = buf.shape[1] // d
    for r in range(d):
        for c in range(buf.shape[0]):
            buf[c, pl.ds(r, n, stride=d), :] = i_ref[r, :, c * LANES:(c + 1) * LANES].astype(F32)


def _rms_fwd(x, gains, dils, name, ts=256):
    S, D = x.shape
    ts = _tile(S, ts, 16 * max(dils))
    n = len(gains)
    nd = len(dils)

    def body(*refs):
        buf = refs[1 + n + n * nd]
        xv = refs[0][...]
        nrm = xv * lax.rsqrt(jnp.mean(xv * xv, axis=-1, keepdims=True) + EPS)
        for q in range(n):
            val = nrm * refs[1 + q][...]
            if max(dils) > 1:
                _fill_chunks(buf, val)
            for e, d in enumerate(dils):
                if d == 1:
                    refs[1 + n + q * nd + e][...] = val.astype(BF16)
                else:
                    _emit_group_order(refs[1 + n + q * nd + e], buf, d, BF16)

    row = pl.BlockSpec((ts, D), lambda i: (i, 0))
    vec = pl.BlockSpec((1, D), lambda i: (0, 0))
    outs = pl.pallas_call(
        body, name=name, grid=(S // ts,), in_specs=[row] + [vec] * n,
        out_specs=[_group_spec(d, ts, D) for _ in range(n) for d in dils],
        out_shape=[_group_shape(d, S, D, BF16) for _ in range(n) for d in dils],
        scratch_shapes=[_chunk_buf(ts, D)],
        compiler_params=_params("parallel"),
    )(x, *gains)
    return [[outs[q * nd + e].reshape(S, D) for e in range(nd)] for q in range(n)]


def _rms_bwd(x, cots, gains, dres, name, after=None, ts=256):
    S, D = x.shape
    ts = _tile(S, ts, 16 * max(d for _, _, d in cots))
    n = len(cots)
    ng = len(gains)
    extra = _as_list(after)
    n_in = 2 + n + ng + len(extra)

    def body(*refs):
        x_ref = refs[0]
        dh_refs = refs[1:1 + n]
        g_refs = refs[1 + n:1 + n + ng]
        dres_ref = refs[1 + n + ng]
        dx_ref, dxb_ref, gg_ref, buf = refs[n_in:n_in + 4]
        i = pl.program_id(0)
        xv = x_ref[...]
        r = lax.rsqrt(jnp.mean(xv * xv, axis=-1, keepdims=True) + EPS)
        nrm = xv * r
        dn = jnp.zeros_like(xv)
        rows = [jnp.zeros((1, D), F32) for _ in range(ng)]
        for q, (_, gi, d) in enumerate(cots):
            if d == 1:
                dh = dh_refs[q][...].astype(F32)
            else:
                _store_token_order(buf, dh_refs[q], d)
                dh = _read_chunks(buf)
            dn = dn + dh * g_refs[gi][...]
            rows[gi] = rows[gi] + jnp.sum(dh * nrm, axis=0, keepdims=True)
        dx = dres_ref[...] + r * (dn - nrm * jnp.mean(dn * nrm, axis=-1, keepdims=True))
        dx_ref[...] = dx
        dxb_ref[...] = dx.astype(BF16)
        upd = _stack_rows(rows, 8)

        @pl.when(i == 0)
        def _():
            gg_ref[...] = upd

        @pl.when(i > 0)
        def _():
            gg_ref[...] += upd

    row = pl.BlockSpec((ts, D), lambda i: (i, 0))
    vec = pl.BlockSpec((1, D), lambda i: (0, 0))
    acc = pl.BlockSpec((8, D), lambda i: (0, 0))
    return pl.pallas_call(
        body, name=name, grid=(S // ts,),
        in_specs=[row] + [_group_spec(d, ts, D) for _, _, d in cots] + [vec] * ng + [row] + [ANY] * len(extra),
        out_specs=[row, row, acc],
        out_shape=[jax.ShapeDtypeStruct((S, D), F32), jax.ShapeDtypeStruct((S, D), BF16),
                   jax.ShapeDtypeStruct((8, D), F32)],
        scratch_shapes=[_chunk_buf(ts, D)],
        compiler_params=_params("arbitrary"),
    )(x, *[a if d == 1 else a.reshape(d, S // d, D) for a, _, d in cots], *gains, dres, *extra)


def _final_head(x2, gain, target, name, ts=256):
    S, D = x2.shape
    ts = _tile(S, ts, 16)

    def body(x_ref, g_ref, t_ref, loss_ref, dx_ref, dxb_ref, gg_ref):
        i = pl.program_id(0)
        xv = x_ref[...]
        g = g_ref[...]
        r = lax.rsqrt(jnp.mean(xv * xv, axis=-1, keepdims=True) + EPS)
        nrm = xv * r
        err = nrm * g - t_ref[...]
        part = 0.5 * jnp.sum(jnp.mean(err * err, axis=-1, keepdims=True), axis=0, keepdims=True)
        dout = err * (1.0 / D)
        dn = dout * g
        dx = r * (dn - nrm * jnp.mean(dn * nrm, axis=-1, keepdims=True))
        dx_ref[...] = dx
        dxb_ref[...] = dx.astype(BF16)
        upd = _stack_rows([jnp.sum(dout * nrm, axis=0, keepdims=True)], 8)
        lpart = jnp.broadcast_to(part, (1, LANES))

        @pl.when(i == 0)
        def _():
            gg_ref[...] = upd
            loss_ref[...] = lpart

        @pl.when(i > 0)
        def _():
            gg_ref[...] += upd
            loss_ref[...] += lpart

    row = pl.BlockSpec((ts, D), lambda i: (i, 0))
    vec = pl.BlockSpec((1, D), lambda i: (0, 0))
    return pl.pallas_call(
        body, name=name, grid=(S // ts,), in_specs=[row, vec, row],
        out_specs=[pl.BlockSpec((1, LANES), lambda i: (0, 0)), row, row, pl.BlockSpec((8, D), lambda i: (0, 0))],
        out_shape=[jax.ShapeDtypeStruct((1, LANES), F32), jax.ShapeDtypeStruct((S, D), F32),
                   jax.ShapeDtypeStruct((S, D), BF16), jax.ShapeDtypeStruct((8, D), F32)],
        compiler_params=_params("arbitrary"),
    )(x2, gain, target)


CONV_ROWS = 64


def _conv_fwd(proj3, conv_w32, conv_b, name, ts=256, cw=256):
    _, S, E = proj3.shape
    ts = _tile(S, ts, HALO)
    cw = _tile(E, cw)
    per = ts // HALO
    rc = min(CONV_ROWS, ts)

    def body(a_ref, b_ref, ap_ref, bp_ref, w_ref, cb_ref, c_ref, ubuf):
        i = pl.program_id(0)
        up = ap_ref[...].astype(F32) * _sigmoid(bp_ref[...].astype(F32))
        ubuf[0:HALO, :] = jnp.where(i > 0, up, 0.0)
        ubuf[HALO:HALO + ts, :] = a_ref[...].astype(F32) * _sigmoid(b_ref[...].astype(F32))
        for r0 in range(0, ts, rc):
            acc = jnp.broadcast_to(cb_ref[...], (rc, cw))
            for k in range(CONV_TAPS):
                off = r0 + HALO - (CONV_TAPS - 1) + k
                acc = acc + ubuf[off:off + rc, :] * w_ref[k:k + 1, :]
            c_ref[r0:r0 + rc, :] = acc

    return pl.pallas_call(
        body, name=name, grid=(S // ts, E // cw),
        in_specs=[
            pl.BlockSpec((None, ts, cw), lambda i, j: (0, i, j)),
            pl.BlockSpec((None, ts, cw), lambda i, j: (1, i, j)),
            pl.BlockSpec((None, HALO, cw), lambda i, j: (0, jnp.maximum(i * per - 1, 0), j)),
            pl.BlockSpec((None, HALO, cw), lambda i, j: (1, jnp.maximum(i * per - 1, 0), j)),
            pl.BlockSpec((HALO, cw), lambda i, j: (0, j)),
            pl.BlockSpec((1, cw), lambda i, j: (0, j)),
        ],
        out_specs=pl.BlockSpec((ts, cw), lambda i, j: (i, j)),
        out_shape=jax.ShapeDtypeStruct((S, E), F32),
        scratch_shapes=[pltpu.VMEM((HALO + ts, cw), F32)],
        compiler_params=_params("parallel", "parallel"),
    )(proj3, proj3, proj3, proj3, conv_w32, conv_b)


def _ln_gate_fwd(c, proj3, ln_g, ln_b, name, ts=256):
    S, E = c.shape
    ts = _tile(S, ts, 16)

    def body(c_ref, z_ref, g_ref, b_ref, y_ref):
        cv = c_ref[...]
        mu = jnp.mean(cv, axis=-1, keepdims=True)
        d = cv - mu
        var = jnp.mean(d * d, axis=-1, keepdims=True)
        cn = d * lax.rsqrt(var + EPS) * g_ref[...] + b_ref[...]
        z = z_ref[...].astype(F32)
        y_ref[...] = ((cn * _sigmoid(cn)).astype(F32) * (z * _sigmoid(z))).astype(BF16)

    row = pl.BlockSpec((ts, E), lambda i: (i, 0))
    vec = pl.BlockSpec((1, E), lambda i: (0, 0))
    return pl.pallas_call(
        body, name=name, grid=(S // ts,),
        in_specs=[row, pl.BlockSpec((None, ts, E), lambda i: (2, i, 0)), vec, vec],
        out_specs=row, out_shape=jax.ShapeDtypeStruct((S, E), BF16),
        compiler_params=_params("parallel"),
    )(c, proj3, ln_g, ln_b)


def _ln_gate_bwd(c, proj3, dy, ln_g, ln_b, name, ts=256):
    S, E = c.shape
    ts = _tile(S, ts, 16)

    def body(c_ref, z_ref, dy_ref, g_ref, b_ref, dc_ref, dz_ref, acc_ref):
        i = pl.program_id(0)
        cv = c_ref[...]
        g = g_ref[...]
        mu = jnp.mean(cv, axis=-1, keepdims=True)
        d = cv - mu
        var = jnp.mean(d * d, axis=-1, keepdims=True)
        rstd = lax.rsqrt(var + EPS)
        chat = d * rstd
        cn = chat * g + b_ref[...]
        z = z_ref[...].astype(F32)
        dyv = dy_ref[...].astype(F32)
        sc = _sigmoid(cn)
        sz = _sigmoid(z)
        dcn = dyv * (z * sz) * (sc * (1.0 + cn * (1.0 - sc)))
        dz_ref[...] = (dyv * (cn * sc) * (sz * (1.0 + z * (1.0 - sz)))).astype(BF16)
        dchat = dcn * g
        dcv = rstd * (dchat - jnp.mean(dchat, axis=-1, keepdims=True)
                      - chat * jnp.mean(dchat * chat, axis=-1, keepdims=True))
        dc_ref[...] = dcv
        upd = _stack_rows([jnp.sum(dcn * chat, axis=0, keepdims=True),
                           jnp.sum(dcn, axis=0, keepdims=True),
                           jnp.sum(dcv, axis=0, keepdims=True)], 8)

        @pl.when(i == 0)
        def _():
            acc_ref[...] = upd

        @pl.when(i > 0)
        def _():
            acc_ref[...] += upd

    row = pl.BlockSpec((ts, E), lambda i: (i, 0))
    vec = pl.BlockSpec((1, E), lambda i: (0, 0))
    return pl.pallas_call(
        body, name=name, grid=(S // ts,),
        in_specs=[row, pl.BlockSpec((None, ts, E), lambda i: (2, i, 0)), row, vec, vec],
        out_specs=[row, pl.BlockSpec((None, ts, E), lambda i: (2, i, 0)), pl.BlockSpec((8, E), lambda i: (0, 0))],
        out_shape=[jax.ShapeDtypeStruct((S, E), F32), jax.ShapeDtypeStruct((3, S, E), BF16),
                   jax.ShapeDtypeStruct((8, E), F32)],
        compiler_params=_params("arbitrary"),
    )(c, proj3, dy, ln_g, ln_b)


def _conv_bwd(proj3, dc, conv_w32, dproj3, name, ts=256, cw=256):
    _, S, E = proj3.shape
    ts = _tile(S, ts, HALO)
    cw = _tile(E, cw)
    per = ts // HALO
    n_i = S // ts
    last_halo = S // HALO - 1
    rc = min(CONV_ROWS, ts)

    def body(a_ref, b_ref, dc_ref, dcn_ref, w_ref, dp_in, dab_ref, dw_ref, dcbuf, ubuf, dwacc):
        del dp_in
        i = pl.program_id(1)
        dcbuf[0:ts, :] = dc_ref[...]
        dcbuf[ts:ts + HALO, :] = jnp.where(i < n_i - 1, dcn_ref[...], 0.0)
        av = a_ref[...].astype(F32)
        sb = _sigmoid(b_ref[...].astype(F32))
        ubuf[...] = av * sb

        @pl.when(i == 0)
        def _():
            dwacc[...] = jnp.zeros_like(dwacc)

        for r0 in range(0, ts, rc):
            uv = ubuf[r0:r0 + rc, :]
            du = jnp.zeros((rc, cw), F32)
            for d in range(CONV_TAPS):
                k = CONV_TAPS - 1 - d
                win = dcbuf[r0 + d:r0 + d + rc, :]
                du = du + win * w_ref[k:k + 1, :]
                dwacc[k:k + 1, :] += jnp.sum(uv * win, axis=0, keepdims=True)
            a_c = a_ref[r0:r0 + rc, :].astype(F32)
            s_c = _sigmoid(b_ref[r0:r0 + rc, :].astype(F32))
            dab_ref[0, r0:r0 + rc, :] = (du * s_c).astype(BF16)
            dab_ref[1, r0:r0 + rc, :] = (du * a_c * s_c * (1.0 - s_c)).astype(BF16)

        @pl.when(i == n_i - 1)
        def _():
            dw_ref[...] = dwacc[...]

    return pl.pallas_call(
        body, name=name, grid=(E // cw, n_i),
        in_specs=[
            pl.BlockSpec((None, ts, cw), lambda j, i: (0, i, j)),
            pl.BlockSpec((None, ts, cw), lambda j, i: (1, i, j)),
            pl.BlockSpec((ts, cw), lambda j, i: (i, j)),
            pl.BlockSpec((HALO, cw), lambda j, i: (jnp.minimum((i + 1) * per, last_halo), j)),
            pl.BlockSpec((HALO, cw), lambda j, i: (0, j)),
            ANY,
        ],
        out_specs=[pl.BlockSpec((2, ts, cw), lambda j, i: (0, i, j)),
                   pl.BlockSpec((HALO, cw), lambda j, i: (0, j))],
        out_shape=[jax.ShapeDtypeStruct((3, S, E), BF16), jax.ShapeDtypeStruct((HALO, E), F32)],
        scratch_shapes=[pltpu.VMEM((ts + HALO, cw), F32), pltpu.VMEM((ts, cw), F32), pltpu.VMEM((HALO, cw), F32)],
        input_output_aliases={5: 0},
        compiler_params=_params("parallel", "arbitrary"),
    )(proj3, proj3, dc, dc, conv_w32, dproj3)


def _bucket_table(dil):
    delta = (np.arange(BLOCK)[:, None] + BLOCK) - np.arange(2 * BLOCK)[None, :]
    dist = np.clip(delta, 0, None) * dil
    large = MAX_EXACT + (np.log(np.maximum(dist, 1).astype(np.float32) / MAX_EXACT)
                         / np.log(MAX_DISTANCE / MAX_EXACT) * (N_BUCKETS - MAX_EXACT)).astype(np.int32)
    large = np.minimum(large, N_BUCKETS - 1)
    return np.where(dist < MAX_EXACT, dist, large).astype(np.int32).reshape(-1)


def _onehot(dil):
    tbl = jnp.asarray(_bucket_table(dil))
    return (tbl[None, :] == jnp.arange(LANES, dtype=jnp.int32)[:, None]).astype(BF16)


def _split3(v):
    hi = v.astype(BF16)
    r1 = v - hi.astype(F32)
    mid = r1.astype(BF16)
    lo = (r1 - mid.astype(F32)).astype(BF16)
    return hi, mid, lo


def _bias_table(rb_t, onehot, name):
    H = rb_t.shape[0]
    N = onehot.shape[1]

    def body(r_ref, oh_ref, o_ref):
        oh = oh_ref[...]
        hi, mid, lo = _split3(r_ref[...])
        o_ref[...] = (_dot(lo, oh, NN) + _dot(mid, oh, NN)) + _dot(hi, oh, NN)

    return pl.pallas_call(
        body, name=name, in_specs=[VMEM_SPEC, VMEM_SPEC], out_specs=VMEM_SPEC,
        out_shape=jax.ShapeDtypeStruct((H, N), F32),
        compiler_params=pltpu.CompilerParams(vmem_limit_bytes=VMEM_LIMIT),
    )(rb_t, onehot)


def _bias_grad(dbs, onehots, name):
    H = dbs[0].shape[0]
    n = len(dbs)

    def body(*refs):
        acc = jnp.zeros((H, LANES), F32)
        for q in range(n):
            oh = refs[n + q][...]
            hi, mid, lo = _split3(refs[q][...])
            acc = acc + ((_dot(lo, oh, NT) + _dot(mid, oh, NT)) + _dot(hi, oh, NT))
        refs[2 * n][...] = acc

    return pl.pallas_call(
        body, name=name, in_specs=[VMEM_SPEC] * (2 * n), out_specs=VMEM_SPEC,
        out_shape=jax.ShapeDtypeStruct((H, LANES), F32),
        compiler_params=pltpu.CompilerParams(vmem_limit_bytes=VMEM_LIMIT),
    )(*dbs, *onehots)


def _attn_fwd(q, kv, bias, dil, name):
    S, A = q.shape
    H = A // HEAD_DIM
    L = S // dil
    nb = L // BLOCK
    qv = q.reshape(dil, L, A)
    kvv = kv.reshape(2, dil, L, A)

    def body(q_ref, kp_ref, kc_ref, vp_ref, vc_ref, b_ref, o_ref, lse_ref):
        i = pl.program_id(1)
        qi = lax.broadcasted_iota(jnp.int32, (BLOCK, BLOCK), 0)
        ki = lax.broadcasted_iota(jnp.int32, (BLOCK, BLOCK), 1)
        mask_c = ki <= qi
        mask_p = jnp.logical_and(ki >= qi, i > 0)
        lane = lax.broadcasted_iota(jnp.int32, (BLOCK, LANES), 1)
        lse_acc = jnp.zeros((BLOCK, LANES), F32)

        def scores(h):
            sl = slice(h * HEAD_DIM, (h + 1) * HEAD_DIM)
            qh = q_ref[:, sl]
            return (_dot(qh, kc_ref[:, sl], NT), _dot(qh, kp_ref[:, sl], NT))

        ahead = [scores(h) for h in range(min(SCORES_AHEAD, H))]
        for h in range(H):
            sl = slice(h * HEAD_DIM, (h + 1) * HEAD_DIM)
            raw_c, raw_p = ahead.pop(0)
            if h + SCORES_AHEAD < H:
                ahead.append(scores(h + SCORES_AHEAD))
            s_c = jnp.where(mask_c, raw_c * SCALE + b_ref[h, :, BLOCK:], NEG)
            s_p = jnp.where(mask_p, raw_p * SCALE + b_ref[h, :, :BLOCK], NEG)
            m = jnp.max(jnp.maximum(s_c, s_p), axis=-1, keepdims=True)
            p_c = jnp.exp(s_c - m)
            p_p = jnp.exp(s_p - m)
            den = jnp.sum(p_c + p_p, axis=-1, keepdims=True)
            acc = _dot(p_c.astype(BF16), vc_ref[:, sl], NN) + _dot(p_p.astype(BF16), vp_ref[:, sl], NN)
            o_ref[:, sl] = acc / den
            lse_acc = jnp.where(lane == h, m + jnp.log(den), lse_acc)
        lse_ref[...] = lse_acc

    def blk(slab, prev):
        if prev:
            return pl.BlockSpec((None, None, BLOCK, A), lambda r, i: (slab, r, jnp.maximum(i - 1, 0), 0))
        return pl.BlockSpec((None, None, BLOCK, A), lambda r, i: (slab, r, i, 0))

    o, lse = pl.pallas_call(
        body, name=name, grid=(dil, nb),
        in_specs=[pl.BlockSpec((None, BLOCK, A), lambda r, i: (r, i, 0)),
                  blk(0, True), blk(0, False), blk(1, True), blk(1, False),
                  pl.BlockSpec((H, BLOCK, 2 * BLOCK), lambda r, i: (0, 0, 0))],
        out_specs=[pl.BlockSpec((None, BLOCK, A), lambda r, i: (r, i, 0)),
                   pl.BlockSpec((None, BLOCK, LANES), lambda r, i: (r, i, 0))],
        out_shape=[jax.ShapeDtypeStruct((dil, L, A), F32), jax.ShapeDtypeStruct((dil, L, LANES), F32)],
        compiler_params=_params("parallel", "parallel"),
    )(qv, kvv, kvv, kvv, kvv, bias)
    return o.reshape(S, A), lse.reshape(S, LANES)


def _attn_merge(os_, lses, z, dils, name, ts=256):
    S, A = z.shape
    H = A // HEAD_DIM
    ts = _tile(S, ts, 16 * max(dils))
    n = len(os_)

    def body(*refs):
        z_ref = refs[2 * n]
        y_ref, om_ref = refs[2 * n + 1:2 * n + 3]
        lse_refs = refs[2 * n + 3:3 * n + 3]
        o_refs = refs[3 * n + 3:4 * n + 3]
        l_bufs = refs[4 * n + 3:5 * n + 3]
        lse_buf = refs[5 * n + 3]
        ls = []
        for q, d in enumerate(dils):
            if d == 1:
                ls.append(refs[n + q][...])
            else:
                _store_token_order(o_refs[q], refs[q], d)
                _store_token_order(l_bufs[q], refs[n + q], d)
                ls.append(l_bufs[q][0])
        m = ls[0]
        for q in range(1, n):
            m = jnp.maximum(m, ls[q])
        es = [jnp.exp(v - m) for v in ls]
        den = es[0]
        for q in range(1, n):
            den = den + es[q]
        alphas = [e / den for e in es]
        lse = m + jnp.log(den)
        lse_buf[0] = lse
        for q, d in enumerate(dils):
            if d == 1:
                lse_refs[q][...] = lse
            else:
                _emit_group_order(lse_refs[q], lse_buf, d, F32)
        lane = lax.broadcasted_iota(jnp.int32, (ts, LANES), 1)
        for h in range(H):
            sl = slice(h * HEAD_DIM, (h + 1) * HEAD_DIM)
            om = jnp.zeros((ts, HEAD_DIM), F32)
            for q, d in enumerate(dils):
                o_h = refs[q][:, sl] if d == 1 else o_refs[q][h]
                om = om + _lane_col(alphas[q], h, lane) * o_h
            z = z_ref[:, sl].astype(F32)
            y_ref[:, sl] = (om * (z * _sigmoid(z))).astype(BF16)
            om_ref[:, sl] = om.astype(BF16)

    row = pl.BlockSpec((ts, A), lambda i: (i, 0))
    outs = pl.pallas_call(
        body, name=name, grid=(S // ts,),
        in_specs=[_group_spec(d, ts, A) for d in dils] + [_group_spec(d, ts, LANES) for d in dils] + [row],
        out_specs=[row, row] + [_group_spec(d, ts, LANES) for d in dils],
        out_shape=[jax.ShapeDtypeStruct((S, A), BF16), jax.ShapeDtypeStruct((S, A), BF16)]
        + [_group_shape(d, S, LANES, F32) for d in dils],
        scratch_shapes=[_chunk_buf(ts, A)] * n + [_chunk_buf(ts, LANES)] * (n + 1),
        compiler_params=_params("parallel"),
    )(*[o if d == 1 else o.reshape(d, S // d, A) for o, d in zip(os_, dils)],
      *[v if d == 1 else v.reshape(d, S // d, LANES) for v, d in zip(lses, dils)], z)
    return outs[0], outs[1], [v.reshape(S, LANES) for v in outs[2:]]


def _gate_bwd(dy, om, z, dils, name, ts=256):
    S, A = dy.shape
    H = A // HEAD_DIM
    ts = _tile(S, ts, 16 * max(dils))
    n = len(dils)

    def body(*refs):
        dy_ref, om_ref, z_ref = refs[:3]
        do_refs = refs[3:3 + n]
        dh_refs = refs[3 + n:3 + 2 * n]
        dz_ref = refs[3 + 2 * n]
        do_buf, dh_buf = refs[4 + 2 * n:6 + 2 * n]
        lane = lax.broadcasted_iota(jnp.int32, (ts, LANES), 1)
        acc = jnp.zeros((ts, LANES), F32)
        for h in range(H):
            sl = slice(h * HEAD_DIM, (h + 1) * HEAD_DIM)
            dyv = dy_ref[:, sl].astype(F32)
            omv = om_ref[:, sl].astype(F32)
            zv = z_ref[:, sl].astype(F32)
            sz = _sigmoid(zv)
            dob = (dyv * (zv * sz)).astype(BF16)
            do_buf[h] = dob.astype(F32)
            for q, d in enumerate(dils):
                if d == 1:
                    do_refs[q][:, sl] = dob
            dz_ref[:, sl] = (dyv * omv * (sz * (1.0 + zv * (1.0 - sz)))).astype(BF16)
            acc = jnp.where(lane == h, jnp.sum(dob.astype(F32) * omv, axis=-1, keepdims=True), acc)
        dh_buf[0] = acc
        for q, d in enumerate(dils):
            if d == 1:
                dh_refs[q][...] = acc
            else:
                _emit_group_order(do_refs[q], do_buf, d, BF16)
                _emit_group_order(dh_refs[q], dh_buf, d, F32)

    row = pl.BlockSpec((ts, A), lambda i: (i, 0))
    outs = pl.pallas_call(
        body, name=name, grid=(S // ts,), in_specs=[row, row, row],
        out_specs=[_group_spec(d, ts, A) for d in dils] + [_group_spec(d, ts, LANES) for d in dils] + [row],
        out_shape=[_group_shape(d, S, A, BF16) for d in dils] + [_group_shape(d, S, LANES, F32) for d in dils]
        + [jax.ShapeDtypeStruct((S, A), BF16)],
        scratch_shapes=[_chunk_buf(ts, A), _chunk_buf(ts, LANES)],
        compiler_params=_params("parallel"),
    )(dy, om, z)
    return ([v.reshape(S, A) for v in outs[:n]], [v.reshape(S, LANES) for v in outs[n:2 * n]], outs[2 * n])


def _attn_bwd(q, kv, do, lse, dh, bias, dil, name):
    S, A = q.shape
    H = A // HEAD_DIM
    L = S // dil
    nb = L // BLOCK
    qv = q.reshape(dil, L, A)
    kvv = kv.reshape(2, dil, L, A)
    dov = do.reshape(dil, L, A)
    lsev = lse.reshape(dil, L, LANES)
    dhv = dh.reshape(dil, L, LANES)

    def body(*refs):
        (q_ref, qn_ref, kp_ref, kc_ref, vp_ref, vc_ref, do_ref, don_ref, l_ref, ln_ref, d_ref, dn_ref,
         b_ref) = refs[:13]
        dq_ref, dkv_ref, db_ref = refs[13:16]
        r = pl.program_id(0)
        i = pl.program_id(1)
        qi = lax.broadcasted_iota(jnp.int32, (BLOCK, BLOCK), 0)
        ki = lax.broadcasted_iota(jnp.int32, (BLOCK, BLOCK), 1)
        mask_c = ki <= qi
        band = ki >= qi
        mask_p = jnp.logical_and(band, i > 0)
        mask_n = jnp.logical_and(band, i < nb - 1)
        lane = lax.broadcasted_iota(jnp.int32, (BLOCK, LANES), 1)

        @pl.when(jnp.logical_and(r == 0, i == 0))
        def _():
            db_ref[...] = jnp.zeros_like(db_ref)

        def products(h):
            sl = slice(h * HEAD_DIM, (h + 1) * HEAD_DIM)
            q_i, q_n = q_ref[:, sl], qn_ref[:, sl]
            k_p, k_c = kp_ref[:, sl], kc_ref[:, sl]
            v_p, v_c = vp_ref[:, sl], vc_ref[:, sl]
            do_i, do_n = do_ref[:, sl], don_ref[:, sl]
            return (_dot(q_i, k_c, NT), _dot(do_i, v_c, NT), _dot(q_i, k_p, NT), _dot(do_i, v_p, NT),
                    _dot(q_n, k_c, NT), _dot(do_n, v_c, NT))

        ahead = [products(h) for h in range(min(PRODUCTS_AHEAD, H))]
        for h in range(H):
            sl = slice(h * HEAD_DIM, (h + 1) * HEAD_DIM)
            s1, dp1, s2, dp2, s3, dp3 = ahead.pop(0)
            if h + PRODUCTS_AHEAD < H:
                ahead.append(products(h + PRODUCTS_AHEAD))
            q_i, q_n = q_ref[:, sl], qn_ref[:, sl]
            k_p, k_c = kp_ref[:, sl], kc_ref[:, sl]
            do_i, do_n = do_ref[:, sl], don_ref[:, sl]
            l_i, l_n = _lane_col(l_ref[...], h, lane), _lane_col(ln_ref[...], h, lane)
            d_i, d_n = _lane_col(d_ref[...], h, lane), _lane_col(dn_ref[...], h, lane)
            b_c = b_ref[h, :, BLOCK:]
            b_p = b_ref[h, :, :BLOCK]
            p1 = jnp.exp(jnp.where(mask_c, s1 * SCALE + b_c, NEG) - l_i)
            ds1 = p1 * (dp1 - d_i)
            ds1b = ds1.astype(BF16)
            p1b = p1.astype(BF16)
            p2 = jnp.exp(jnp.where(mask_p, s2 * SCALE + b_p, NEG) - l_i)
            ds2 = p2 * (dp2 - d_i)
            ds2b = ds2.astype(BF16)
            p3 = jnp.exp(jnp.where(mask_n, s3 * SCALE + b_p, NEG) - l_n)
            ds3b = (p3 * (dp3 - d_n)).astype(BF16)
            p3b = p3.astype(BF16)
            dq = _dot(ds1b, k_c, NN) + _dot(ds2b, k_p, NN)
            dk = _dot(ds1b, q_i, TN) + _dot(ds3b, q_n, TN)
            dv = _dot(p1b, do_i, TN) + _dot(p3b, do_n, TN)
            dq_ref[:, sl] = (dq * SCALE).astype(BF16)
            dkv_ref[0, :, sl] = (dk * SCALE).astype(BF16)
            dkv_ref[1, :, sl] = dv.astype(BF16)
            db_ref[h, :, BLOCK:] += ds1
            db_ref[h, :, :BLOCK] += ds2

    def blk(slab, shift):
        if shift < 0:
            return pl.BlockSpec((None, None, BLOCK, A), lambda r, i: (slab, r, jnp.maximum(i - 1, 0), 0))
        return pl.BlockSpec((None, None, BLOCK, A), lambda r, i: (slab, r, i, 0))

    def row(width, shift):
        if shift > 0:
            return pl.BlockSpec((None, BLOCK, width), lambda r, i: (r, jnp.minimum(i + 1, nb - 1), 0))
        return pl.BlockSpec((None, BLOCK, width), lambda r, i: (r, i, 0))

    in_specs = [row(A, 0), row(A, 1), blk(0, -1), blk(0, 0), blk(1, -1), blk(1, 0),
                row(A, 0), row(A, 1), row(LANES, 0), row(LANES, 1), row(LANES, 0), row(LANES, 1),
                pl.BlockSpec((H, BLOCK, 2 * BLOCK), lambda r, i: (0, 0, 0))]
    dq, dkv, db = pl.pallas_call(
        body, name=name, grid=(dil, nb), in_specs=in_specs,
        out_specs=[pl.BlockSpec((None, BLOCK, A), lambda r, i: (r, i, 0)),
                   pl.BlockSpec((2, None, BLOCK, A), lambda r, i: (0, r, i, 0)),
                   pl.BlockSpec((H, BLOCK, 2 * BLOCK), lambda r, i: (0, 0, 0))],
        out_shape=[jax.ShapeDtypeStruct((dil, L, A), BF16), jax.ShapeDtypeStruct((2, dil, L, A), BF16),
                   jax.ShapeDtypeStruct((H, BLOCK, 2 * BLOCK), F32)],
        compiler_params=_params("arbitrary", "arbitrary"),
    )(qv, qv, kvv, kvv, kvv, kvv, dov, dov, lsev, lsev, dhv, dhv, bias)
    return dq.reshape(S, A), dkv.reshape(2, S, A), db


def _sum_leading(stack, out_dtype, name, tr=256, tc=2048):
    n, R, C = stack.shape
    tr = _tile(R, tr, 16)
    tc = _tile(C, tc)

    def body(s_ref, o_ref):
        acc = s_ref[0].astype(F32)
        for q in range(1, n):
            acc = acc + s_ref[q].astype(F32)
        o_ref[...] = acc.astype(out_dtype)

    return pl.pallas_call(
        body, name=name, grid=(R // tr, C // tc),
        in_specs=[pl.BlockSpec((n, tr, tc), lambda i, j: (0, i, j))],
        out_specs=pl.BlockSpec((tr, tc), lambda i, j: (i, j)),
        out_shape=jax.ShapeDtypeStruct((R, C), out_dtype),
        compiler_params=_params("parallel", "parallel"),
    )(stack)


def _add_half(g, t, c_idx, kind, name, tr=256, tc=2048):
    R, C = t.shape
    tr = _tile(R, tr, 16)
    tc = _tile(C, tc)
    nrb, ncb = R // tr, C // tc

    def body(c_ref, g_ref, t_ref, o_ref):
        del c_ref
        o_ref[...] = (g_ref[...].astype(F32) + t_ref[...].astype(F32)).astype(BF16)

    if kind == "col":
        g_map = lambda i, j, c_ref: (c_ref[0] * nrb + i, j)
    else:
        g_map = lambda i, j, c_ref: (i, c_ref[0] * ncb + j)
    same = lambda i, j, c_ref: (i, j)
    return pl.pallas_call(
        body, name=name,
        grid_spec=pltpu.PrefetchScalarGridSpec(
            num_scalar_prefetch=1, grid=(nrb, ncb),
            in_specs=[pl.BlockSpec((tr, tc), g_map), pl.BlockSpec((tr, tc), same)],
            out_specs=pl.BlockSpec((tr, tc), same)),
        out_shape=jax.ShapeDtypeStruct((R, C), BF16),
        compiler_params=_params("parallel", "parallel"),
    )(c_idx, g, t)


def _cast_into_full(w, kind, chip_idx, name, tr=256, tc=2048):
    R, C = w.shape
    tr = _tile(R, tr, 16)
    tc = _tile(C, tc)
    nrb, ncb = R // tr, C // tc

    def body(k_ref, w_ref, o_ref):
        del k_ref
        o_ref[...] = w_ref[...].astype(BF16)

    if kind == "col":
        o_map = lambda i, j, k_ref: (i, k_ref[0] * ncb + j)
        full = (R, N_CHIPS * C)
    else:
        o_map = lambda i, j, k_ref: (k_ref[0] * nrb + i, j)
        full = (N_CHIPS * R, C)
    return pl.pallas_call(
        body, name=name,
        grid_spec=pltpu.PrefetchScalarGridSpec(
            num_scalar_prefetch=1, grid=(nrb, ncb),
            in_specs=[pl.BlockSpec((tr, tc), lambda i, j, k_ref: (i, j))],
            out_specs=pl.BlockSpec((tr, tc), o_map)),
        out_shape=jax.ShapeDtypeStruct(full, BF16),
        compiler_params=_params("parallel", "parallel"),
    )(chip_idx, w)


def _sum_into_shard(p, u, idx, kind, name, tr=256, tc=2048):
    _, R, C = u.shape
    tr = _tile(R, tr, 16)
    tc = _tile(C, tc)
    nrb, ncb = R // tr, C // tc

    def body(i_ref, p_ref, u_ref, o_ref):
        del i_ref
        acc = p_ref[...].astype(F32)
        for q in range(N_CHIPS - 1):
            acc = acc + u_ref[q].astype(F32)
        o_ref[...] = acc

    if kind == "col":
        p_map = lambda i, j, r: (i, r[0] * ncb + j)
        o_map = lambda i, j, r: (r[1] * nrb + i, j)
        full = (2 * R, C)
    else:
        p_map = lambda i, j, r: (r[0] * nrb + i, j)
        o_map = lambda i, j, r: (i, r[1] * ncb + j)
        full = (R, 2 * C)
    return pl.pallas_call(
        body, name=name,
        grid_spec=pltpu.PrefetchScalarGridSpec(
            num_scalar_prefetch=1, grid=(nrb, ncb),
            in_specs=[pl.BlockSpec((tr, tc), p_map), pl.BlockSpec((N_CHIPS - 1, tr, tc), lambda i, j, r: (0, i, j))],
            out_specs=pl.BlockSpec((tr, tc), o_map)),
        out_shape=jax.ShapeDtypeStruct(full, F32),
        compiler_params=_params("parallel", "parallel"),
    )(idx, p, u)


def _adamw(w, g, m, v, name, tr=256, tc=2048):
    R, C = w.shape
    tr = _tile(R, tr, 8)
    tc = _tile(C, tc)
    c1 = 1.0 - ADAM_B1 ** ADAM_STEP
    c2 = 1.0 - ADAM_B2 ** ADAM_STEP

    def body(w_ref, g_ref, m_ref, v_ref, d_ref, nm_ref, nv_ref):
        gv = g_ref[...]
        nm = ADAM_B1 * m_ref[...] + (1.0 - ADAM_B1) * gv
        nv = ADAM_B2 * v_ref[...] + (1.0 - ADAM_B2) * (gv * gv)
        d_ref[...] = -ADAM_LR * ((nm / c1) / (jnp.sqrt(nv / c2) + ADAM_EPS) + ADAM_WD * w_ref[...])
        nm_ref[...] = nm
        nv_ref[...] = nv

    blk = pl.BlockSpec((tr, tc), lambda i, j: (i, j))
    sh = jax.ShapeDtypeStruct((R, C), F32)
    return pl.pallas_call(
        body, name=name, grid=(R // tr, C // tc), in_specs=[blk] * 4, out_specs=[blk] * 3,
        out_shape=[sh, sh, sh], compiler_params=_params("parallel", "parallel"),
    )(w, g, m, v)


def _xyc():
    return lax.axis_index("x"), lax.axis_index("y"), lax.axis_index("c")


def _drain(copies):
    for cp in copies:
        if cp.is_remote:
            cp.wait_send()
        else:
            cp.wait()


def _other_chips(x, y):
    return [(1 - x, y), (x, 1 - y), (1 - x, 1 - y)]


def _allgather_small(blk, name, after=None):
    R, C = blk.shape
    extra = _as_list(after)

    def body(*refs):
        x_ref = refs[0]
        out_ref, send_sems, recv_sems, local_sem = refs[1 + len(extra):]
        x, y, c = _xyc()
        me = 4 * x + 2 * y + c
        mine = pltpu.make_async_copy(x_ref, out_ref.at[me], local_sem)
        mine.start()
        peers = []
        for k in range(1, N_DEV):
            px = 1 - x if (k >> 2) & 1 else x
            py = 1 - y if (k >> 1) & 1 else y
            pc = 1 - c if k & 1 else c
            peers.append((px, py, pc))
        sends = []
        for k, peer in enumerate(peers):
            cp = pltpu.make_async_remote_copy(
                src_ref=x_ref, dst_ref=out_ref.at[me], send_sem=send_sems.at[k], recv_sem=recv_sems.at[k],
                device_id=peer, device_id_type=MESH)
            cp.start()
            sends.append(cp)
        for k, (px, py, pc) in enumerate(peers):
            pltpu.make_async_remote_copy(
                src_ref=x_ref, dst_ref=out_ref.at[4 * px + 2 * py + pc], send_sem=send_sems.at[k],
                recv_sem=recv_sems.at[k], device_id=(px, py, pc), device_id_type=MESH).wait_recv()
        for cp in sends:
            cp.wait_send()
        mine.wait()

    return pl.pallas_call(
        body, name=name, in_specs=[VMEM_SPEC] + [ANY] * len(extra), out_specs=VMEM_SPEC,
        out_shape=jax.ShapeDtypeStruct((N_DEV, R, C), blk.dtype),
        scratch_shapes=[pltpu.SemaphoreType.DMA((N_DEV - 1,)), pltpu.SemaphoreType.DMA((N_DEV - 1,)),
                        pltpu.SemaphoreType.DMA],
        compiler_params=pltpu.CompilerParams(vmem_limit_bytes=VMEM_LIMIT),
    )(blk, *extra)


def _full_region(ref, kind, chip, half, shard_shape):
    r, cn = shard_shape
    hr = r // 2
    if kind == "col":
        rows = pl.ds(0, r) if half is None else pl.ds(pl.multiple_of(half * hr, 16), hr)
        return ref.at[rows, pl.ds(pl.multiple_of(chip * cn, LANES), cn)]
    if half is None:
        return ref.at[pl.ds(pl.multiple_of(chip * r, 16), r), :]
    return ref.at[pl.ds(pl.multiple_of(chip * r + half * hr, 16), hr), :]


def _allgather_weights(fulls, kinds, shapes, name):
    n = len(fulls)

    def body(*refs):
        outs = refs[n:2 * n]
        send_sems, recv_sems = refs[2 * n:]
        x, y, c = _xyc()
        chip = 2 * x + y
        sib = (x, y, 1 - c)
        others = _other_chips(x, y)
        started = []
        for w in range(n):
            mine = _full_region(outs[w], kinds[w], chip, c, shapes[w])
            for j, (ox, oy) in enumerate(others):
                cp = pltpu.make_async_remote_copy(
                    src_ref=mine, dst_ref=mine, send_sem=send_sems.at[6 * w + j], recv_sem=recv_sems.at[6 * w + j],
                    device_id=(ox, oy, c), device_id_type=MESH)
                cp.start()
                started.append(cp)
        for w in range(n):
            for j, (ox, oy) in enumerate(others):
                landed = _full_region(outs[w], kinds[w], 2 * ox + oy, c, shapes[w])
                pltpu.make_async_remote_copy(
                    src_ref=landed, dst_ref=landed, send_sem=send_sems.at[6 * w + j], recv_sem=recv_sems.at[6 * w + j],
                    device_id=(ox, oy, c), device_id_type=MESH).wait_recv()
                cp = pltpu.make_async_remote_copy(
                    src_ref=landed, dst_ref=landed, send_sem=send_sems.at[6 * w + 3 + j],
                    recv_sem=recv_sems.at[6 * w + 3 + j], device_id=sib, device_id_type=MESH)
                cp.start()
                started.append(cp)
        for w in range(n):
            for j, (ox, oy) in enumerate(others):
                theirs = _full_region(outs[w], kinds[w], 2 * ox + oy, 1 - c, shapes[w])
                pltpu.make_async_remote_copy(
                    src_ref=theirs, dst_ref=theirs, send_sem=send_sems.at[6 * w + 3 + j],
                    recv_sem=recv_sems.at[6 * w + 3 + j], device_id=sib, device_id_type=MESH).wait_recv()
        _drain(started)

    return pl.pallas_call(
        body, name=name, in_specs=[ANY] * n, out_specs=[ANY] * n,
        out_shape=[jax.ShapeDtypeStruct(f.shape, f.dtype) for f in fulls],
        input_output_aliases={w: w for w in range(n)},
        scratch_shapes=[pltpu.SemaphoreType.DMA((6 * n,)), pltpu.SemaphoreType.DMA((6 * n,))],
    )(*fulls)


def _region_of_size(ref, kind, shard_shape, count):
    r, cn = shard_shape
    if kind == "col":
        return ref.at[pl.ds(0, r // 2), pl.ds(0, count * cn)]
    return ref.at[pl.ds(0, count * (r // 2)), :]


def _allgather_weights_seq(fulls, kinds, shapes, name, collective_id):
    n = len(fulls)
    refs = [jax.new_ref(f, memory_space=pltpu.MemorySpace.HBM) for f in fulls]

    def body(send_sems, recv_sems):
        x, y, c = _xyc()
        chip = 2 * x + y
        sib = (x, y, 1 - c)
        others = _other_chips(x, y)
        peers = [(ox, oy, c) for ox, oy in others] + [sib]
        barrier = pltpu.get_barrier_semaphore()
        for peer in peers:
            pl.semaphore_signal(barrier, inc=1, device_id=peer, device_id_type=MESH)
        pl.semaphore_wait(barrier, len(peers))

        def copy(w, region, sem, to):
            return pltpu.make_async_remote_copy(src_ref=region, dst_ref=region, send_sem=send_sems.at[sem],
                                                recv_sem=recv_sems.at[sem], device_id=to, device_id_type=MESH)

        for w in range(n):
            mine = _full_region(refs[w], kinds[w], chip, c, shapes[w])
            for ox, oy in others:
                copy(w, mine, 2 * w, (ox, oy, c)).start()
        for w in range(n):
            three = _region_of_size(refs[w], kinds[w], shapes[w], 3)
            copy(w, three, 2 * w, sib).wait_recv()
            for ox, oy in others:
                copy(w, _full_region(refs[w], kinds[w], 2 * ox + oy, c, shapes[w]), 2 * w + 1, sib).start()
        for w in range(n):
            three = _region_of_size(refs[w], kinds[w], shapes[w], 3)
            copy(w, three, 2 * w + 1, sib).wait_recv()
            copy(w, three, 2 * w, sib).wait_send()
            copy(w, three, 2 * w + 1, sib).wait_send()

    pl.kernel(
        body, out_type=(), mesh=plsc.ScalarSubcoreMesh(axis_name="seq", num_cores=1), name=name,
        scratch_types=[pltpu.SemaphoreType.DMA((2 * n,)), pltpu.SemaphoreType.DMA((2 * n,))],
        compiler_params=pltpu.CompilerParams(collective_id=collective_id),
    )()
    return [r[...] for r in refs]


def _half_of(ref, kind, half):
    r, cn = ref.shape
    if kind == "col":
        return ref.at[pl.ds(pl.multiple_of(half * (r // 2), 16), r // 2), :]
    return ref.at[:, pl.ds(pl.multiple_of(half * (cn // 2), LANES), cn // 2)]


def _shard_of(ref, kind, chip):
    r, cn = ref.shape
    if kind == "col":
        return ref.at[:, pl.ds(pl.multiple_of(chip * (cn // N_CHIPS), LANES), cn // N_CHIPS)]
    return ref.at[pl.ds(pl.multiple_of(chip * (r // N_CHIPS), 16), r // N_CHIPS), :]


def _exchange_halves(grads, kinds, name):
    n = len(grads)

    def body(*refs):
        gs = refs[:n]
        ts = refs[n:2 * n]
        send_sems, recv_sems = refs[2 * n:]
        x, y, c = _xyc()
        cps = []
        for w in range(n):
            cp = pltpu.make_async_remote_copy(
                src_ref=_half_of(gs[w], kinds[w], 1 - c), dst_ref=ts[w], send_sem=send_sems.at[w],
                recv_sem=recv_sems.at[w], device_id=(x, y, 1 - c), device_id_type=MESH)
            cp.start()
            cps.append(cp)
        for cp in cps:
            cp.wait()

    out_shape = []
    for gr, kind in zip(grads, kinds):
        r, cn = gr.shape
        out_shape.append(jax.ShapeDtypeStruct((r // 2, cn) if kind == "col" else (r, cn // 2), gr.dtype))
    return pl.pallas_call(
        body, name=name, in_specs=[ANY] * n, out_specs=[ANY] * n, out_shape=out_shape,
        scratch_shapes=[pltpu.SemaphoreType.DMA((n,)), pltpu.SemaphoreType.DMA((n,))],
    )(*grads)


def _scatter_partials(parts, kinds, name):
    n = len(parts)

    def body(*refs):
        ps = refs[:n]
        us = refs[n:2 * n]
        send_sems, recv_sems = refs[2 * n:]
        x, y, c = _xyc()
        others = _other_chips(x, y)
        cps = []
        for w in range(n):
            for j, (ox, oy) in enumerate(others):
                cp = pltpu.make_async_remote_copy(
                    src_ref=_shard_of(ps[w], kinds[w], 2 * ox + oy), dst_ref=us[w].at[j],
                    send_sem=send_sems.at[3 * w + j], recv_sem=recv_sems.at[3 * w + j],
                    device_id=(ox, oy, c), device_id_type=MESH)
                cp.start()
                cps.append(cp)
        for cp in cps:
            cp.wait()

    out_shape = []
    for p, kind in zip(parts, kinds):
        r, cn = p.shape
        hs = (r, cn // N_CHIPS) if kind == "col" else (r // N_CHIPS, cn)
        out_shape.append(jax.ShapeDtypeStruct((N_CHIPS - 1,) + hs, p.dtype))
    return pl.pallas_call(
        body, name=name, in_specs=[ANY] * n, out_specs=[ANY] * n, out_shape=out_shape,
        scratch_shapes=[pltpu.SemaphoreType.DMA((3 * n,)), pltpu.SemaphoreType.DMA((3 * n,))],
    )(*parts)


def _scatter_partials_seq(parts, kinds, name, collective_id):
    n = len(parts)

    def body(*refs):
        ps = refs[:n]
        us = refs[n:2 * n]
        send_sems, recv_sems = refs[2 * n:]
        x, y, c = _xyc()
        others = _other_chips(x, y)
        barrier = pltpu.get_barrier_semaphore()
        for ox, oy in others:
            pl.semaphore_signal(barrier, inc=1, device_id=(ox, oy, c), device_id_type=MESH)
        pl.semaphore_wait(barrier, len(others))
        for w in range(n):
            for j, (ox, oy) in enumerate(others):
                pltpu.make_async_remote_copy(
                    src_ref=_shard_of(ps[w], kinds[w], 2 * ox + oy), dst_ref=us[w].at[j],
                    send_sem=send_sems.at[w], recv_sem=recv_sems.at[w],
                    device_id=(ox, oy, c), device_id_type=MESH).start()
        for w in range(n):
            pltpu.make_async_remote_copy(
                src_ref=us[w], dst_ref=us[w], send_sem=send_sems.at[w], recv_sem=recv_sems.at[w],
                device_id=(x, y, c), device_id_type=MESH).wait()

    out_type = []
    for p, kind in zip(parts, kinds):
        r, cn = p.shape
        hs = (r, cn // N_CHIPS) if kind == "col" else (r // N_CHIPS, cn)
        out_type.append(jax.ShapeDtypeStruct((N_CHIPS - 1,) + hs, p.dtype))
    return pl.kernel(
        body, out_type=out_type, mesh=plsc.ScalarSubcoreMesh(axis_name="seq", num_cores=1), name=name,
        scratch_types=[pltpu.SemaphoreType.DMA((n,)), pltpu.SemaphoreType.DMA((n,))],
        compiler_params=pltpu.CompilerParams(collective_id=collective_id),
    )(*parts)


def _join_halves(halves, kinds, name):
    n = len(halves)

    def body(*refs):
        outs = refs[n:2 * n]
        send_sems, recv_sems = refs[2 * n:]
        x, y, c = _xyc()
        cps = []
        for w in range(n):
            mine = _half_of(outs[w], kinds[w], c)
            cp = pltpu.make_async_remote_copy(
                src_ref=mine, dst_ref=mine, send_sem=send_sems.at[w], recv_sem=recv_sems.at[w],
                device_id=(x, y, 1 - c), device_id_type=MESH)
            cp.start()
            cps.append(cp)
        for w in range(n):
            theirs = _half_of(outs[w], kinds[w], 1 - c)
            pltpu.make_async_remote_copy(
                src_ref=theirs, dst_ref=theirs, send_sem=send_sems.at[w], recv_sem=recv_sems.at[w],
                device_id=(x, y, 1 - c), device_id_type=MESH).wait_recv()
        _drain(cps)

    return pl.pallas_call(
        body, name=name, in_specs=[ANY] * n, out_specs=[ANY] * n,
        out_shape=[jax.ShapeDtypeStruct(h.shape, h.dtype) for h in halves],
        input_output_aliases={w: w for w in range(n)},
        scratch_shapes=[pltpu.SemaphoreType.DMA((n,)), pltpu.SemaphoreType.DMA((n,))],
    )(*halves)


def kernel(x, a_norm, a_w_in, a_conv_w, a_conv_b, a_ln_g, a_ln_b, a_w_out, kv_norm, w_kv, b_norm, b_w_in, b_w_out, rel_bias, final_norm, loss_target, m_a_norm, m_a_w_in, m_a_conv_w, m_a_conv_b, m_a_ln_g, m_a_ln_b, m_a_w_out, m_kv_norm, m_w_kv, m_b_norm, m_b_w_in, m_b_w_out, m_rel_bias, m_final_norm, v_a_norm, v_a_w_in, v_a_conv_w, v_a_conv_b, v_a_ln_g, v_a_ln_b, v_a_w_out, v_kv_norm, v_w_kv, v_b_norm, v_b_w_in, v_b_w_out, v_rel_bias, v_final_norm):
    S, D = x.shape[1], x.shape[2]
    E = a_w_out.shape[1] * N_CHIPS
    A = b_w_out.shape[1] * N_CHIPS
    H = A // HEAD_DIM
    DC = D // N_CHIPS
    xs = x.reshape(S, D)
    tgt = loss_target.reshape(S, D)
    cx, cy, cc = _xyc()
    chip = 2 * cx + cy
    c_idx = jnp.reshape(cc, (1,)).astype(jnp.int32)

    big_names = ["a_w_in", "a_w_out", "w_kv", "b_w_in", "b_w_out"]
    kinds = ["col", "row", "col", "col", "row"]
    big_w = [a_w_in[0], a_w_out[0], w_kv, b_w_in[0], b_w_out[0]]
    big_m = [m_a_w_in[0], m_a_w_out[0], m_w_kv, m_b_w_in[0], m_b_w_out[0]]
    big_v = [v_a_w_in[0], v_a_w_out[0], v_w_kv, v_b_w_in[0], v_b_w_out[0]]
    chip_idx = jnp.reshape(chip, (1,)).astype(jnp.int32)
    placed = [_cast_into_full(big_w[w], kinds[w], chip_idx, "cast_" + big_names[w]) for w in range(5)]
    shard_shapes = [w.shape for w in big_w]
    (wa_in,) = _allgather_weights_seq(placed[0:1], kinds[0:1], shard_shapes[0:1], "ag_seq_a_in", 0)
    wa_out, wkv = _allgather_weights_seq(placed[1:3], kinds[1:3], shard_shapes[1:3], "ag_seq_a_out_kv", 1)
    wb_in, wb_out = _allgather_weights_seq(placed[3:5], kinds[3:5], shard_shapes[3:5], "ag_seq_b", 2)

    def row_at(vec, q):
        return jnp.pad(vec, ((q, 7 - q), (0, 0)))

    def pack_sharded(an, cw, cb, lg, lb):
        return jnp.concatenate([row_at(an, 0), jnp.pad(cw[0], ((0, 1), (0, 0))),
                                row_at(lg, 0) + row_at(lb, 1) + row_at(cb, 2)], axis=0)

    small_w = pack_sharded(a_norm, a_conv_w, a_conv_b, a_ln_g, a_ln_b)
    gathered = _allgather_small(small_w, "ag_small_params")
    small_full = jnp.concatenate([gathered[2 * k] for k in range(N_CHIPS)], axis=1)
    g_a = small_full[0:1]
    conv_w32 = small_full[8:8 + HALO]
    ln_g = small_full[40:41]
    ln_b = small_full[41:42]
    conv_b = small_full[42:43]
    g_kv = kv_norm.reshape(1, D)
    g_b = b_norm.reshape(1, D)
    g_f = final_norm.reshape(1, D)

    rb_t = jnp.pad(rel_bias.T, ((0, 0), (0, LANES - N_BUCKETS)))
    onehots = [_onehot(dil) for _, dil in GROUPS]
    biases = [_bias_table(rb_t, onehots[g], "bias_table_%d" % g).reshape(H, BLOCK, 2 * BLOCK)
              for g in range(len(GROUPS))]

    dils = tuple(dil for _, dil in GROUPS)
    assert dils[0] == 1
    n_g = len(GROUPS)
    ((h0,),) = _rms_fwd(xs, [g_a], (1,), "rms_a")
    proj3 = _matmul(h0, wa_in, "nn", BF16, "mm_a_in", out_slab=E)
    conv = _conv_fwd(proj3, conv_w32, conv_b, "conv_fwd")
    y_a = _ln_gate_fwd(conv, proj3, ln_g, ln_b, "ln_gate_fwd")
    x1 = _matmul(y_a, wa_out, "nn", F32, "mm_a_out", res=xs)
    hks, hbs = _rms_fwd(x1, [g_kv, g_b], dils, "rms_kv_b")
    kvs = [_matmul(hks[g], wkv, "nn", BF16, "mm_kv_%d" % g, out_slab=A, b_off=2 * g * A, n_cols=2 * A)
           for g in range(n_g)]
    qs = [_matmul(hbs[g], wb_in, "nn", BF16, "mm_q_%d" % g, b_off=g * A, n_cols=A, after=kvs[-1])
          for g in range(n_g)]
    zb = _matmul(hbs[0], wb_in, "nn", BF16, "mm_zb", b_off=n_g * A, n_cols=A, after=kvs[-1])
    os_, lses = [], []
    for g, dil in enumerate(dils):
        o_g, lse_g = _attn_fwd(qs[g], kvs[g], biases[g], dil, "attn_fwd_%d" % g)
        os_.append(o_g)
        lses.append(lse_g)
    y_b, o_m, lse_d = _attn_merge(os_, lses, zb, dils, "attn_merge")
    x2 = _matmul(y_b, wb_out, "nn", F32, "mm_b_out", res=x1)
    loss_part, dx2, dx2b, gg_f = _final_head(x2, g_f, tgt, "final_head")
    loss = lax.psum(loss_part[0, 0], ("x", "y", "c"))

    dw_tiles = dict(tm=512, tn=1024, tk=4096)
    dy_b = _matmul(dx2b, wb_out, "nt", BF16, "mm_b_out_dx")
    dwb_out = _matmul(y_b, dx2b, "tn", BF16, "mm_b_out_dw", **dw_tiles)
    dos, dhs, dzb = _gate_bwd(dy_b, o_m, zb, dils, "gate_bwd")
    dbs, cots = [], []
    dwb_in = dwkv = None
    for g, dil in enumerate(dils):
        dq, dkv, db = _attn_bwd(qs[g], kvs[g], dos[g], lse_d[g], dhs[g], biases[g], dil, "attn_bwd_%d" % g)
        dbs.append(db.reshape(H, BLOCK * 2 * BLOCK))
        dwb_in = _matmul(hbs[g], dq, "tn", BF16, "mm_q_dw_%d" % g, out_off=g * A, out_cols=(n_g + 1) * A,
                         out_alias=dwb_in, **dw_tiles)
        dwkv = _matmul(hks[g], dkv, "tn", BF16, "mm_kv_dw_%d" % g, b_slab=True, out_off=2 * g * A,
                       out_cols=2 * n_g * A, out_alias=dwkv, **dw_tiles)
        cots.append((_matmul(dkv, wkv, "nt", BF16, "mm_kv_dx_%d" % g, a_slab=True, b_off=2 * g * A), 0, dil))
        cots.append((_matmul(dq, wb_in, "nt", BF16, "mm_q_dx_%d" % g, b_off=g * A), 1, dil))
    dwb_in = _matmul(hbs[0], dzb, "tn", BF16, "mm_zb_dw", out_off=n_g * A, out_cols=(n_g + 1) * A,
                     out_alias=dwb_in, **dw_tiles)
    cots.append((_matmul(dzb, wb_in, "nt", BF16, "mm_zb_dx", b_off=n_g * A), 1, 1))
    chip_c = jnp.stack([chip, cc]).astype(jnp.int32)

    def scatter_group(idx, grads, tag, collective_id):
        ks = [kinds[w] for w in idx]
        theirs = _exchange_halves(grads, ks, "rs_exchange_" + tag)
        parts = [_add_half(grads[q], theirs[q], c_idx, ks[q], "rs_add_half_%d" % w) for q, w in enumerate(idx)]
        return parts, _scatter_partials_seq(parts, ks, "rs_seq_" + tag, collective_id)

    def reduce_group(idx, parts, slots, tag):
        ks = [kinds[w] for w in idx]
        halves = [_sum_into_shard(parts[q], slots[q], chip_c, ks[q], "rs_sum_chips_%d" % w)
                  for q, w in enumerate(idx)]
        return _join_halves(halves, ks, "rs_join_" + tag)

    parts_b, slots_b = scatter_group([2, 3, 4], [dwkv, dwb_in, dwb_out], "b", 3)
    g_rel_t = _bias_grad(dbs, onehots, "bias_grad")
    dx1, dx1b, gg_kvb = _rms_bwd(x1, cots, [g_kv, g_b], dx2, "rms_kv_b_bwd", after=parts_b)
    dy_a = _matmul(dx1b, wa_out, "nt", BF16, "mm_a_out_dx")
    dwa_out = _matmul(y_a, dx1b, "tn", BF16, "mm_a_out_dw", **dw_tiles)
    dconv, dproj3, gg_ln = _ln_gate_bwd(conv, proj3, dy_a, ln_g, ln_b, "ln_gate_bwd")
    dproj3, g_conv_w = _conv_bwd(proj3, dconv, conv_w32, dproj3, "conv_bwd")
    dwa_in = _matmul(h0, dproj3, "tn", BF16, "mm_a_in_dw", b_slab=True, **dw_tiles)
    parts_a, slots_a = scatter_group([0, 1], [dwa_in, dwa_out], "a", 4)
    dh0 = _matmul(dproj3, wa_in, "nt", BF16, "mm_a_in_dx", a_slab=True, after=parts_a)
    grad_x, _, gg_a = _rms_bwd(xs, [(dh0, 0, 1)], [g_a], dx1, "rms_a_bwd")

    big_g = [None] * 5
    big_g[2:5] = reduce_group([2, 3, 4], parts_b, slots_b, "b")
    big_g[0:2] = reduce_group([0, 1], parts_a, slots_a, "a")

    def rel_rows(rb):
        return jnp.pad(rb.reshape(1, N_BUCKETS * H), ((0, 7), (0, D - N_BUCKETS * H)))

    small_g = jnp.concatenate([gg_a, g_conv_w, gg_ln, gg_kvb, gg_f, rel_rows(g_rel_t[:, :N_BUCKETS].T)], axis=0)
    small_sum = _sum_leading(_allgather_small(small_g, "ag_small_grads", after=[slots_a[0], slots_b[0]]), F32,
                             "sum_small_grads", tr=72)
    g_sharded = lax.dynamic_slice(small_sum, (0, chip * DC), (48, DC))
    g_repl = small_sum[48:72]

    outs_g, outs_d, outs_m, outs_v = {}, {}, {}, {}
    for w, nm in enumerate(big_names):
        d_, m_, v_ = _adamw(big_w[w], big_g[w], big_m[w], big_v[w], "adamw_" + nm)
        outs_g[nm], outs_d[nm], outs_m[nm], outs_v[nm] = big_g[w], d_, m_, v_
    sm_m = pack_sharded(m_a_norm, m_a_conv_w, m_a_conv_b, m_a_ln_g, m_a_ln_b)
    sm_v = pack_sharded(v_a_norm, v_a_conv_w, v_a_conv_b, v_a_ln_g, v_a_ln_b)
    sd, smm, svv = _adamw(small_w, g_sharded, sm_m, sm_v, "adamw_small_sharded")

    def unpack_sharded(p):
        return {"a_norm": p[0:1], "a_conv_w": p[8:8 + CONV_TAPS].reshape(1, CONV_TAPS, DC), "a_ln_g": p[40:41],
                "a_ln_b": p[41:42], "a_conv_b": p[42:43]}

    for src, dst in ((g_sharded, outs_g), (sd, outs_d), (smm, outs_m), (svv, outs_v)):
        dst.update(unpack_sharded(src))

    def pack_repl(kn, bn, fn, rb):
        return jnp.concatenate([row_at(kn.reshape(1, D), 0) + row_at(bn.reshape(1, D), 1),
                                row_at(fn.reshape(1, D), 0), rel_rows(rb)], axis=0)

    rp_w = pack_repl(kv_norm, b_norm, final_norm, rel_bias)
    rp_m = pack_repl(m_kv_norm, m_b_norm, m_final_norm, m_rel_bias)
    rp_v = pack_repl(v_kv_norm, v_b_norm, v_final_norm, v_rel_bias)
    rd, rmm, rvv = _adamw(rp_w, g_repl, rp_m, rp_v, "adamw_small_replicated")

    def unpack_repl(p):
        return {"kv_norm": p[0], "b_norm": p[1:2], "final_norm": p[8],
                "rel_bias": p[16, :N_BUCKETS * H].reshape(N_BUCKETS, H)}

    for src, dst in ((g_repl, outs_g), (rd, outs_d), (rmm, outs_m), (rvv, outs_v)):
        dst.update(unpack_repl(src))

    order = ["a_norm", "a_w_in", "a_conv_w", "a_conv_b", "a_ln_g", "a_ln_b", "a_w_out", "kv_norm", "w_kv",
             "b_norm", "b_w_in", "b_w_out", "rel_bias", "final_norm"]
    lead = {"a_w_in", "a_w_out", "b_w_in", "b_w_out"}

    def shaped(nm, val):
        return val[None] if nm in lead else val

    result = [loss, grad_x.reshape(1, S, D)]
    for table in (outs_g, outs_d, outs_m, outs_v):
        result.extend(shaped(nm, table[nm]) for nm in order)
    return tuple(result)
```

```python
import functools

import numpy as np
import jax
import jax.numpy as jnp
from jax import lax
from jax.experimental import pallas as pl
from jax.experimental.pallas import tpu as pltpu
from jax.experimental.pallas import tpu_sc as plsc

F32 = jnp.float32
BF16 = jnp.bfloat16
MESH = pl.DeviceIdType.MESH
ANY = pl.BlockSpec(memory_space=pl.ANY)
VMEM_SPEC = pl.BlockSpec(memory_space=pltpu.VMEM)

EPS = 1e-6
HEAD_DIM = 128
BLOCK = 128
GROUPS = ((128, 1), (512, 4), (2048, 16))
SCALE = HEAD_DIM ** -0.5
CONV_TAPS = 31
HALO = 32
N_BUCKETS = 32
MAX_EXACT = 16
MAX_DISTANCE = 2048
NEG = -1e30
PRODUCTS_AHEAD = 2
SCORES_AHEAD = 3
N_CHIPS = 4
N_DEV = 8
LANES = 128
VMEM_LIMIT = 56 * 1024 * 1024

ADAM_LR = 0.001
ADAM_B1 = 0.9
ADAM_B2 = 0.999
ADAM_EPS = 1e-08
ADAM_WD = 0.01
ADAM_STEP = 10


def _tile(n, pref, mult=LANES):
    t = (min(pref, n) // mult) * mult
    while t >= mult:
        if n % t == 0:
            return t
        t -= mult
    return n


def _params(*sem):
    return pltpu.CompilerParams(dimension_semantics=sem, vmem_limit_bytes=VMEM_LIMIT)


def _sigmoid(v):
    return 1.0 / (1.0 + jnp.exp(-v))


def _dot(a, b, dims):
    return lax.dot_general(a, b, (dims, ((), ())), preferred_element_type=F32)


NN = ((1,), (0,))
NT = ((1,), (1,))
TN = ((0,), (0,))


def _as_list(after):
    if after is None:
        return []
    return list(after) if isinstance(after, (list, tuple)) else [after]


def _stack_rows(rows, total):
    width = rows[0].shape[1]
    rid = lax.broadcasted_iota(jnp.int32, (total, width), 0)
    out = jnp.zeros((total, width), F32)
    for q, row in enumerate(rows):
        out = jnp.where(rid == q, jnp.broadcast_to(row, (total, width)), out)
    return out


def _lane_col(arr, h, lane):
    return jnp.sum(jnp.where(lane == h, arr, 0.0), axis=-1, keepdims=True)


def _matmul(a, b, mode, out_dtype, name, res=None, a_slab=False, b_slab=False, out_slab=0,
            b_off=0, n_cols=None, out_off=0, out_cols=None, out_alias=None, after=None,
            tm=512, tn=1024, tk=2048):
    if a_slab:
        na, M, W = a.shape
        K = na * W
    elif mode == "tn":
        K, M = a.shape
    else:
        M, K = a.shape
    if b_slab:
        nbs, _, Wb = b.shape
        N = nbs * Wb
    elif mode == "nt":
        N = b.shape[0]
    else:
        N = n_cols if n_cols else b.shape[1]
    tm = _tile(M, tm)
    tn = _tile(Wb if b_slab else (out_slab if out_slab else N), tn)
    tk = _tile(W if a_slab else K, tk)
    all_slabs = a_slab and mode == "nt" and tk == W
    if all_slabs:
        tk = K
    nk = K // tk
    grid = (M // tm, N // tn, nk)
    bo = b_off // (tk if mode == "nt" else tn)
    oo = out_off // tn

    if all_slabs:
        a_spec = pl.BlockSpec((na, tm, W), lambda i, j, k: (0, i, 0))
    elif a_slab:
        per = W // tk
        a_spec = pl.BlockSpec((None, tm, tk), lambda i, j, k: (k // per, i, k % per))
    elif mode == "tn":
        a_spec = pl.BlockSpec((tk, tm), lambda i, j, k: (k, i))
    else:
        a_spec = pl.BlockSpec((tm, tk), lambda i, j, k: (i, k))
    if b_slab:
        perb = Wb // tn
        b_spec = pl.BlockSpec((None, tk, tn), lambda i, j, k: (j // perb, k, j % perb))
    elif mode == "nt":
        b_spec = pl.BlockSpec((tn, tk), lambda i, j, k: (j, k + bo))
    else:
        b_spec = pl.BlockSpec((tk, tn), lambda i, j, k: (k, j + bo))
    if out_slab:
        pero = out_slab // tn
        o_spec = pl.BlockSpec((None, tm, tn), lambda i, j, k: (j // pero, i, j % pero))
        out_shape = jax.ShapeDtypeStruct((N // out_slab, M, out_slab), out_dtype)
    else:
        o_spec = pl.BlockSpec((tm, tn), lambda i, j, k: (i, j + oo))
        out_shape = jax.ShapeDtypeStruct((M, out_cols if out_cols else N), out_dtype)
    in_specs = [a_spec, b_spec]
    operands = [a, b]
    if res is not None:
        in_specs.append(pl.BlockSpec((tm, tn), lambda i, j, k: (i, j)))
        operands.append(res)
    aliases = {}
    if out_alias is not None:
        aliases[len(operands)] = 0
        in_specs.append(ANY)
        operands.append(out_alias)
    for arr in _as_list(after):
        in_specs.append(ANY)
        operands.append(arr)
    dims = {"nn": NN, "nt": NT, "tn": TN}[mode]
    has_res = res is not None
    n_in = len(operands)

    def body(*refs):
        a_ref, b_ref = refs[0], refs[1]
        r_ref = refs[2] if has_res else None
        o_ref = refs[n_in]
        if all_slabs:
            prod = _dot(a_ref[0], b_ref[:, 0:W], dims)
            for q in range(1, na):
                prod = prod + _dot(a_ref[q], b_ref[:, q * W:(q + 1) * W], dims)
        else:
            prod = _dot(a_ref[...], b_ref[...], dims)

        def finish(val):
            if has_res:
                val = val + r_ref[...]
            o_ref[...] = val.astype(out_dtype)

        if nk == 1:
            finish(prod)
        else:
            acc_ref = refs[n_in + 1]
            k = pl.program_id(2)

            @pl.when(k == 0)
            def _():
                acc_ref[...] = prod

            @pl.when(k > 0)
            def _():
                acc_ref[...] += prod

            @pl.when(k == nk - 1)
            def _():
                finish(acc_ref[...])

    scratch = [pltpu.VMEM((tm, tn), F32)] if nk > 1 else []
    return pl.pallas_call(
        body, name=name, grid=grid, in_specs=in_specs, out_specs=o_spec, out_shape=out_shape,
        scratch_shapes=scratch, input_output_aliases=aliases,
        compiler_params=_params("parallel", "parallel", "arbitrary"),
    )(*operands)


def _group_spec(d, ts, width):
    if d == 1:
        return pl.BlockSpec((ts, width), lambda i: (i, 0))
    return pl.BlockSpec((d, ts // d, width), lambda i: (0, i, 0))


def _group_shape(d, S, width, dtype):
    return jax.ShapeDtypeStruct((S, width) if d == 1 else (d, S // d, width), dtype)


def _chunk_buf(ts, width):
    return pltpu.VMEM((width // LANES, ts, LANES), F32)


def _fill_chunks(buf, val):
    for c in range(buf.shape[0]):
        buf[c] = val[:, c * LANES:(c + 1) * LANES]


def _read_chunks(buf):
    return jnp.concatenate([buf[c] for c in range(buf.shape[0])], axis=1)


def _emit_group_order(o_ref, buf, d, dtype):
    n = buf.shape[1] // d
    for r in range(d):
        for c in range(buf.shape[0]):
            o_ref[r, :, c * LANES:(c + 1) * LANES] = buf[c, pl.ds(r, n, stride=d), :].astype(dtype)


def _store_token_order(buf, i_ref, d):
    n = buf.shape[1] // d
    for r in range(d):
        for c in range(buf.shape[0]):
            buf[c, pl.ds(r, n, stride=d), :] = i_ref[r, :, c * LANES:(c + 1) * LANES].astype(F32)


def _rms_fwd(x, gains, dils, name, ts=256):
    S, D = x.shape
    ts = _tile(S, ts, 16 * max(dils))
    n = len(gains)
    nd = len(dils)

    def body(*refs):
        buf = refs[1 + n + n * nd]
        xv = refs[0][...]
        nrm = xv * lax.rsqrt(jnp.mean(xv * xv, axis=-1, keepdims=True) + EPS)
        for q in range(n):
            val = nrm * refs[1 + q][...]
            if max(dils) > 1:
                _fill_chunks(buf, val)
            for e, d in enumerate(dils):
                if d == 1:
                    refs[1 + n + q * nd + e][...] = val.astype(BF16)
                else:
                    _emit_group_order(refs[1 + n + q * nd + e], buf, d, BF16)

    row = pl.BlockSpec((ts, D), lambda i: (i, 0))
    vec = pl.BlockSpec((1, D), lambda i: (0, 0))
    outs = pl.pallas_call(
        body, name=name, grid=(S // ts,), in_specs=[row] + [vec] * n,
        out_specs=[_group_spec(d, ts, D) for _ in range(n) for d in dils],
        out_shape=[_group_shape(d, S, D, BF16) for _ in range(n) for d in dils],
        scratch_shapes=[_chunk_buf(ts, D)],
        compiler_params=_params("parallel"),
    )(x, *gains)
    return [[outs[q * nd + e].reshape(S, D) for e in range(nd)] for q in range(n)]


def _rms_bwd(x, cots, gains, dres, name, after=None, ts=256):
    S, D = x.shape
    ts = _tile(S, ts, 16 * max(d for _, _, d in cots))
    n = len(cots)
    ng = len(gains)
    extra = _as_list(after)
    n_in = 2 + n + ng + len(extra)

    def body(*refs):
        x_ref = refs[0]
        dh_refs = refs[1:1 + n]
        g_refs = refs[1 + n:1 + n + ng]
        dres_ref = refs[1 + n + ng]
        dx_ref, dxb_ref, gg_ref, buf = refs[n_in:n_in + 4]
        i = pl.program_id(0)
        xv = x_ref[...]
        r = lax.rsqrt(jnp.mean(xv * xv, axis=-1, keepdims=True) + EPS)
        nrm = xv * r
        dn = jnp.zeros_like(xv)
        rows = [jnp.zeros((1, D), F32) for _ in range(ng)]
        for q, (_, gi, d) in enumerate(cots):
            if d == 1:
                dh = dh_refs[q][...].astype(F32)
            else:
                _store_token_order(buf, dh_refs[q], d)
                dh = _read_chunks(buf)
            dn = dn + dh * g_refs[gi][...]
            rows[gi] = rows[gi] + jnp.sum(dh * nrm, axis=0, keepdims=True)
        dx = dres_ref[...] + r * (dn - nrm * jnp.mean(dn * nrm, axis=-1, keepdims=True))
        dx_ref[...] = dx
        dxb_ref[...] = dx.astype(BF16)
        upd = _stack_rows(rows, 8)

        @pl.when(i == 0)
        def _():
            gg_ref[...] = upd

        @pl.when(i > 0)
        def _():
            gg_ref[...] += upd

    row = pl.BlockSpec((ts, D), lambda i: (i, 0))
    vec = pl.BlockSpec((1, D), lambda i: (0, 0))
    acc = pl.BlockSpec((8, D), lambda i: (0, 0))
    return pl.pallas_call(
        body, name=name, grid=(S // ts,),
        in_specs=[row] + [_group_spec(d, ts, D) for _, _, d in cots] + [vec] * ng + [row] + [ANY] * len(extra),
        out_specs=[row, row, acc],
        out_shape=[jax.ShapeDtypeStruct((S, D), F32), jax.ShapeDtypeStruct((S, D), BF16),
                   jax.ShapeDtypeStruct((8, D), F32)],
        scratch_shapes=[_chunk_buf(ts, D)],
        compiler_params=_params("arbitrary"),
    )(x, *[a if d == 1 else a.reshape(d, S // d, D) for a, _, d in cots], *gains, dres, *extra)


def _final_head(x2, gain, target, name, ts=256):
    S, D = x2.shape
    ts = _tile(S, ts, 16)

    def body(x_ref, g_ref, t_ref, loss_ref, dx_ref, dxb_ref, gg_ref):
        i = pl.program_id(0)
        xv = x_ref[...]
        g = g_ref[...]
        r = lax.rsqrt(jnp.mean(xv * xv, axis=-1, keepdims=True) + EPS)
        nrm = xv * r
        err = nrm * g - t_ref[...]
        part = 0.5 * jnp.sum(jnp.mean(err * err, axis=-1, keepdims=True), axis=0, keepdims=True)
        dout = err * (1.0 / D)
        dn = dout * g
        dx = r * (dn - nrm * jnp.mean(dn * nrm, axis=-1, keepdims=True))
        dx_ref[...] = dx
        dxb_ref[...] = dx.astype(BF16)
        upd = _stack_rows([jnp.sum(dout * nrm, axis=0, keepdims=True)], 8)
        lpart = jnp.broadcast_to(part, (1, LANES))

        @pl.when(i == 0)
        def _():
            gg_ref[...] = upd
            loss_ref[...] = lpart

        @pl.when(i > 0)
        def _():
            gg_ref[...] += upd
            loss_ref[...] += lpart

    row = pl.BlockSpec((ts, D), lambda i: (i, 0))
    vec = pl.BlockSpec((1, D), lambda i: (0, 0))
    return pl.pallas_call(
        body, name=name, grid=(S // ts,), in_specs=[row, vec, row],
        out_specs=[pl.BlockSpec((1, LANES), lambda i: (0, 0)), row, row, pl.BlockSpec((8, D), lambda i: (0, 0))],
        out_shape=[jax.ShapeDtypeStruct((1, LANES), F32), jax.ShapeDtypeStruct((S, D), F32),
                   jax.ShapeDtypeStruct((S, D), BF16), jax.ShapeDtypeStruct((8, D), F32)],
        compiler_params=_params("arbitrary"),
    )(x2, gain, target)


CONV_ROWS = 64


SUBLANES = 8


def _shifted_buf(ts, cw):
    return pltpu.VMEM((SUBLANES - 1, ts + HALO - SUBLANES, cw), F32)


def _fill_shifted(shifted, buf):
    rows = shifted.shape[1]
    for s in range(1, SUBLANES):
        shifted[s - 1] = buf[s:s + rows, :]


def _window(buf, shifted, off, rows):
    s = off % SUBLANES
    base = off - s
    if s == 0:
        return buf[base:base + rows, :]
    return shifted[s - 1, base:base + rows, :]


def _conv_fwd(proj3, conv_w32, conv_b, name, ts=256, cw=256):
    _, S, E = proj3.shape
    ts = _tile(S, ts, HALO)
    cw = _tile(E, cw)
    per = ts // HALO
    rc = min(CONV_ROWS, ts)

    def body(a_ref, b_ref, ap_ref, bp_ref, w_ref, cb_ref, c_ref, ubuf, shifted):
        i = pl.program_id(0)
        up = ap_ref[...].astype(F32) * _sigmoid(bp_ref[...].astype(F32))
        ubuf[0:HALO, :] = jnp.where(i > 0, up, 0.0)
        ubuf[HALO:HALO + ts, :] = a_ref[...].astype(F32) * _sigmoid(b_ref[...].astype(F32))
        _fill_shifted(shifted, ubuf)
        for r0 in range(0, ts, rc):
            acc = jnp.broadcast_to(cb_ref[...], (rc, cw))
            for k in range(CONV_TAPS):
                off = r0 + HALO - (CONV_TAPS - 1) + k
                acc = acc + _window(ubuf, shifted, off, rc) * w_ref[k:k + 1, :]
            c_ref[r0:r0 + rc, :] = acc

    return pl.pallas_call(
        body, name=name, grid=(S // ts, E // cw),
        in_specs=[
            pl.BlockSpec((None, ts, cw), lambda i, j: (0, i, j)),
            pl.BlockSpec((None, ts, cw), lambda i, j: (1, i, j)),
            pl.BlockSpec((None, HALO, cw), lambda i, j: (0, jnp.maximum(i * per - 1, 0), j)),
            pl.BlockSpec((None, HALO, cw), lambda i, j: (1, jnp.maximum(i * per - 1, 0), j)),
            pl.BlockSpec((HALO, cw), lambda i, j: (0, j)),
            pl.BlockSpec((1, cw), lambda i, j: (0, j)),
        ],
        out_specs=pl.BlockSpec((ts, cw), lambda i, j: (i, j)),
        out_shape=jax.ShapeDtypeStruct((S, E), F32),
        scratch_shapes=[pltpu.VMEM((HALO + ts, cw), F32), _shifted_buf(ts, cw)],
        compiler_params=_params("parallel", "parallel"),
    )(proj3, proj3, proj3, proj3, conv_w32, conv_b)


def _ln_gate_fwd(c, proj3, ln_g, ln_b, name, ts=256):
    S, E = c.shape
    ts = _tile(S, ts, 16)

    def body(c_ref, z_ref, g_ref, b_ref, y_ref):
        cv = c_ref[...]
        mu = jnp.mean(cv, axis=-1, keepdims=True)
        d = cv - mu
        var = jnp.mean(d * d, axis=-1, keepdims=True)
        cn = d * lax.rsqrt(var + EPS) * g_ref[...] + b_ref[...]
        z = z_ref[...].astype(F32)
        y_ref[...] = ((cn * _sigmoid(cn)).astype(F32) * (z * _sigmoid(z))).astype(BF16)

    row = pl.BlockSpec((ts, E), lambda i: (i, 0))
    vec = pl.BlockSpec((1, E), lambda i: (0, 0))
    return pl.pallas_call(
        body, name=name, grid=(S // ts,),
        in_specs=[row, pl.BlockSpec((None, ts, E), lambda i: (2, i, 0)), vec, vec],
        out_specs=row, out_shape=jax.ShapeDtypeStruct((S, E), BF16),
        compiler_params=_params("parallel"),
    )(c, proj3, ln_g, ln_b)


def _ln_gate_bwd(c, proj3, dy, ln_g, ln_b, name, ts=256):
    S, E = c.shape
    ts = _tile(S, ts, 16)

    def body(c_ref, z_ref, dy_ref, g_ref, b_ref, dc_ref, dz_ref, acc_ref):
        i = pl.program_id(0)
        cv = c_ref[...]
        g = g_ref[...]
        mu = jnp.mean(cv, axis=-1, keepdims=True)
        d = cv - mu
        var = jnp.mean(d * d, axis=-1, keepdims=True)
        rstd = lax.rsqrt(var + EPS)
        chat = d * rstd
        cn = chat * g + b_ref[...]
        z = z_ref[...].astype(F32)
        dyv = dy_ref[...].astype(F32)
        sc = _sigmoid(cn)
        sz = _sigmoid(z)
        dcn = dyv * (z * sz) * (sc * (1.0 + cn * (1.0 - sc)))
        dz_ref[...] = (dyv * (cn * sc) * (sz * (1.0 + z * (1.0 - sz)))).astype(BF16)
        dchat = dcn * g
        dcv = rstd * (dchat - jnp.mean(dchat, axis=-1, keepdims=True)
                      - chat * jnp.mean(dchat * chat, axis=-1, keepdims=True))
        dc_ref[...] = dcv
        upd = _stack_rows([jnp.sum(dcn * chat, axis=0, keepdims=True),
                           jnp.sum(dcn, axis=0, keepdims=True),
                           jnp.sum(dcv, axis=0, keepdims=True)], 8)

        @pl.when(i == 0)
        def _():
            acc_ref[...] = upd

        @pl.when(i > 0)
        def _():
            acc_ref[...] += upd

    row = pl.BlockSpec((ts, E), lambda i: (i, 0))
    vec = pl.BlockSpec((1, E), lambda i: (0, 0))
    return pl.pallas_call(
        body, name=name, grid=(S // ts,),
        in_specs=[row, pl.BlockSpec((None, ts, E), lambda i: (2, i, 0)), row, vec, vec],
        out_specs=[row, pl.BlockSpec((None, ts, E), lambda i: (2, i, 0)), pl.BlockSpec((8, E), lambda i: (0, 0))],
        out_shape=[jax.ShapeDtypeStruct((S, E), F32), jax.ShapeDtypeStruct((3, S, E), BF16),
                   jax.ShapeDtypeStruct((8, E), F32)],
        compiler_params=_params("arbitrary"),
    )(c, proj3, dy, ln_g, ln_b)


def _conv_bwd(proj3, dc, conv_w32, dproj3, name, ts=256, cw=256):
    _, S, E = proj3.shape
    ts = _tile(S, ts, HALO)
    cw = _tile(E, cw)
    per = ts // HALO
    n_i = S // ts
    last_halo = S // HALO - 1
    rc = min(CONV_ROWS, ts)

    def body(a_ref, b_ref, dc_ref, dcn_ref, w_ref, dp_in, dab_ref, dw_ref, dcbuf, ubuf, dwacc, shifted):
        del dp_in
        i = pl.program_id(1)
        dcbuf[0:ts, :] = dc_ref[...]
        dcbuf[ts:ts + HALO, :] = jnp.where(i < n_i - 1, dcn_ref[...], 0.0)
        _fill_shifted(shifted, dcbuf)
        av = a_ref[...].astype(F32)
        sb = _sigmoid(b_ref[...].astype(F32))
        ubuf[...] = av * sb

        @pl.when(i == 0)
        def _():
            dwacc[...] = jnp.zeros_like(dwacc)

        for r0 in range(0, ts, rc):
            uv = ubuf[r0:r0 + rc, :]
            du = jnp.zeros((rc, cw), F32)
            for d in range(CONV_TAPS):
                k = CONV_TAPS - 1 - d
                win = _window(dcbuf, shifted, r0 + d, rc)
                du = du + win * w_ref[k:k + 1, :]
                dwacc[k:k + 1, :] += jnp.sum(uv * win, axis=0, keepdims=True)
            a_c = a_ref[r0:r0 + rc, :].astype(F32)
            s_c = _sigmoid(b_ref[r0:r0 + rc, :].astype(F32))
            dab_ref[0, r0:r0 + rc, :] = (du * s_c).astype(BF16)
            dab_ref[1, r0:r0 + rc, :] = (du * a_c * s_c * (1.0 - s_c)).astype(BF16)

        @pl.when(i == n_i - 1)
        def _():
            dw_ref[...] = dwacc[...]

    return pl.pallas_call(
        body, name=name, grid=(E // cw, n_i),
        in_specs=[
            pl.BlockSpec((None, ts, cw), lambda j, i: (0, i, j)),
            pl.BlockSpec((None, ts, cw), lambda j, i: (1, i, j)),
            pl.BlockSpec((ts, cw), lambda j, i: (i, j)),
            pl.BlockSpec((HALO, cw), lambda j, i: (jnp.minimum((i + 1) * per, last_halo), j)),
            pl.BlockSpec((HALO, cw), lambda j, i: (0, j)),
            ANY,
        ],
        out_specs=[pl.BlockSpec((2, ts, cw), lambda j, i: (0, i, j)),
                   pl.BlockSpec((HALO, cw), lambda j, i: (0, j))],
        out_shape=[jax.ShapeDtypeStruct((3, S, E), BF16), jax.ShapeDtypeStruct((HALO, E), F32)],
        scratch_shapes=[pltpu.VMEM((ts + HALO, cw), F32), pltpu.VMEM((ts, cw), F32), pltpu.VMEM((HALO, cw), F32),
                        _shifted_buf(ts, cw)],
        input_output_aliases={5: 0},
        compiler_params=_params("parallel", "arbitrary"),
    )(proj3, proj3, dc, dc, conv_w32, dproj3)


def _bucket_table(dil):
    delta = (np.arange(BLOCK)[:, None] + BLOCK) - np.arange(2 * BLOCK)[None, :]
    dist = np.clip(delta, 0, None) * dil
    large = MAX_EXACT + (np.log(np.maximum(dist, 1).astype(np.float32) / MAX_EXACT)
                         / np.log(MAX_DISTANCE / MAX_EXACT) * (N_BUCKETS - MAX_EXACT)).astype(np.int32)
    large = np.minimum(large, N_BUCKETS - 1)
    return np.where(dist < MAX_EXACT, dist, large).astype(np.int32).reshape(-1)


def _onehot(dil):
    tbl = jnp.asarray(_bucket_table(dil))
    return (tbl[None, :] == jnp.arange(LANES, dtype=jnp.int32)[:, None]).astype(BF16)


def _split3(v):
    hi = v.astype(BF16)
    r1 = v - hi.astype(F32)
    mid = r1.astype(BF16)
    lo = (r1 - mid.astype(F32)).astype(BF16)
    return hi, mid, lo


def _bias_table(rb_t, onehot, name):
    H = rb_t.shape[0]
    N = onehot.shape[1]

    def body(r_ref, oh_ref, o_ref):
        oh = oh_ref[...]
        hi, mid, lo = _split3(r_ref[...])
        o_ref[...] = (_dot(lo, oh, NN) + _dot(mid, oh, NN)) + _dot(hi, oh, NN)

    return pl.pallas_call(
        body, name=name, in_specs=[VMEM_SPEC, VMEM_SPEC], out_specs=VMEM_SPEC,
        out_shape=jax.ShapeDtypeStruct((H, N), F32),
        compiler_params=pltpu.CompilerParams(vmem_limit_bytes=VMEM_LIMIT),
    )(rb_t, onehot)


def _bias_grad(dbs, onehots, name):
    H = dbs[0].shape[0]
    n = len(dbs)

    def body(*refs):
        acc = jnp.zeros((H, LANES), F32)
        for q in range(n):
            oh = refs[n + q][...]
            hi, mid, lo = _split3(refs[q][...])
            acc = acc + ((_dot(lo, oh, NT) + _dot(mid, oh, NT)) + _dot(hi, oh, NT))
        refs[2 * n][...] = acc

    return pl.pallas_call(
        body, name=name, in_specs=[VMEM_SPEC] * (2 * n), out_specs=VMEM_SPEC,
        out_shape=jax.ShapeDtypeStruct((H, LANES), F32),
        compiler_params=pltpu.CompilerParams(vmem_limit_bytes=VMEM_LIMIT),
    )(*dbs, *onehots)


def _attn_fwd(q, kv, bias, dil, name):
    S, A = q.shape
    H = A // HEAD_DIM
    L = S // dil
    nb = L // BLOCK
    qv = q.reshape(dil, L, A)
    kvv = kv.reshape(2, dil, L, A)

    def body(q_ref, kp_ref, kc_ref, vp_ref, vc_ref, b_ref, o_ref, lse_ref):
        i = pl.program_id(1)
        qi = lax.broadcasted_iota(jnp.int32, (BLOCK, BLOCK), 0)
        ki = lax.broadcasted_iota(jnp.int32, (BLOCK, BLOCK), 1)
        mask_c = ki <= qi
        mask_p = jnp.logical_and(ki >= qi, i > 0)
        lane = lax.broadcasted_iota(jnp.int32, (BLOCK, LANES), 1)
        lse_acc = jnp.zeros((BLOCK, LANES), F32)

        def scores(h):
            sl = slice(h * HEAD_DIM, (h + 1) * HEAD_DIM)
            qh = q_ref[:, sl]
            return (_dot(qh, kc_ref[:, sl], NT), _dot(qh, kp_ref[:, sl], NT))

        ahead = [scores(h) for h in range(min(SCORES_AHEAD, H))]
        for h in range(H):
            sl = slice(h * HEAD_DIM, (h + 1) * HEAD_DIM)
            raw_c, raw_p = ahead.pop(0)
            if h + SCORES_AHEAD < H:
                ahead.append(scores(h + SCORES_AHEAD))
            s_c = jnp.where(mask_c, raw_c * SCALE + b_ref[h, :, BLOCK:], NEG)
            s_p = jnp.where(mask_p, raw_p * SCALE + b_ref[h, :, :BLOCK], NEG)
            m = jnp.max(jnp.maximum(s_c, s_p), axis=-1, keepdims=True)
            p_c = jnp.exp(s_c - m)
            p_p = jnp.exp(s_p - m)
            den = jnp.sum(p_c + p_p, axis=-1, keepdims=True)
            acc = _dot(p_c.astype(BF16), vc_ref[:, sl], NN) + _dot(p_p.astype(BF16), vp_ref[:, sl], NN)
            o_ref[:, sl] = acc / den
            lse_acc = jnp.where(lane == h, m + jnp.log(den), lse_acc)
        lse_ref[...] = lse_acc

    def blk(slab, prev):
        if prev:
            return pl.BlockSpec((None, None, BLOCK, A), lambda r, i: (slab, r, jnp.maximum(i - 1, 0), 0))
        return pl.BlockSpec((None, None, BLOCK, A), lambda r, i: (slab, r, i, 0))

    o, lse = pl.pallas_call(
        body, name=name, grid=(dil, nb),
        in_specs=[pl.BlockSpec((None, BLOCK, A), lambda r, i: (r, i, 0)),
                  blk(0, True), blk(0, False), blk(1, True), blk(1, False),
                  pl.BlockSpec((H, BLOCK, 2 * BLOCK), lambda r, i: (0, 0, 0))],
        out_specs=[pl.BlockSpec((None, BLOCK, A), lambda r, i: (r, i, 0)),
                   pl.BlockSpec((None, BLOCK, LANES), lambda r, i: (r, i, 0))],
        out_shape=[jax.ShapeDtypeStruct((dil, L, A), F32), jax.ShapeDtypeStruct((dil, L, LANES), F32)],
        compiler_params=_params("parallel", "parallel"),
    )(qv, kvv, kvv, kvv, kvv, bias)
    return o.reshape(S, A), lse.reshape(S, LANES)


def _attn_merge(os_, lses, z, dils, name, ts=256):
    S, A = z.shape
    H = A // HEAD_DIM
    ts = _tile(S, ts, 16 * max(dils))
    n = len(os_)

    def body(*refs):
        z_ref = refs[2 * n]
        y_ref, om_ref = refs[2 * n + 1:2 * n + 3]
        lse_refs = refs[2 * n + 3:3 * n + 3]
        o_refs = refs[3 * n + 3:4 * n + 3]
        l_bufs = refs[4 * n + 3:5 * n + 3]
        lse_buf = refs[5 * n + 3]
        ls = []
        for q, d in enumerate(dils):
            if d == 1:
                ls.append(refs[n + q][...])
            else:
                _store_token_order(o_refs[q], refs[q], d)
                _store_token_order(l_bufs[q], refs[n + q], d)
                ls.append(l_bufs[q][0])
        m = ls[0]
        for q in range(1, n):
            m = jnp.maximum(m, ls[q])
        es = [jnp.exp(v - m) for v in ls]
        den = es[0]
        for q in range(1, n):
            den = den + es[q]
        alphas = [e / den for e in es]
        lse = m + jnp.log(den)
        lse_buf[0] = lse
        for q, d in enumerate(dils):
            if d == 1:
                lse_refs[q][...] = lse
            else:
                _emit_group_order(lse_refs[q], lse_buf, d, F32)
        lane = lax.broadcasted_iota(jnp.int32, (ts, LANES), 1)
        for h in range(H):
            sl = slice(h * HEAD_DIM, (h + 1) * HEAD_DIM)
            om = jnp.zeros((ts, HEAD_DIM), F32)
            for q, d in enumerate(dils):
                o_h = refs[q][:, sl] if d == 1 else o_refs[q][h]
                om = om + _lane_col(alphas[q], h, lane) * o_h
            z = z_ref[:, sl].astype(F32)
            y_ref[:, sl] = (om * (z * _sigmoid(z))).astype(BF16)
            om_ref[:, sl] = om.astype(BF16)

    row = pl.BlockSpec((ts, A), lambda i: (i, 0))
    outs = pl.pallas_call(
        body, name=name, grid=(S // ts,),
        in_specs=[_group_spec(d, ts, A) for d in dils] + [_group_spec(d, ts, LANES) for d in dils] + [row],
        out_specs=[row, row] + [_group_spec(d, ts, LANES) for d in dils],
        out_shape=[jax.ShapeDtypeStruct((S, A), BF16), jax.ShapeDtypeStruct((S, A), BF16)]
        + [_group_shape(d, S, LANES, F32) for d in dils],
        scratch_shapes=[_chunk_buf(ts, A)] * n + [_chunk_buf(ts, LANES)] * (n + 1),
        compiler_params=_params("parallel"),
    )(*[o if d == 1 else o.reshape(d, S // d, A) for o, d in zip(os_, dils)],
      *[v if d == 1 else v.reshape(d, S // d, LANES) for v, d in zip(lses, dils)], z)
    return outs[0], outs[1], [v.reshape(S, LANES) for v in outs[2:]]


def _gate_bwd(dy, om, z, dils, name, ts=256):
    S, A = dy.shape
    H = A // HEAD_DIM
    ts = _tile(S, ts, 16 * max(dils))
    n = len(dils)

    def body(*refs):
        dy_ref, om_ref, z_ref = refs[:3]
        do_refs = refs[3:3 + n]
        dh_refs = refs[3 + n:3 + 2 * n]
        dz_ref = refs[3 + 2 * n]
        do_buf, dh_buf = refs[4 + 2 * n:6 + 2 * n]
        lane = lax.broadcasted_iota(jnp.int32, (ts, LANES), 1)
        acc = jnp.zeros((ts, LANES), F32)
        for h in range(H):
            sl = slice(h * HEAD_DIM, (h + 1) * HEAD_DIM)
            dyv = dy_ref[:, sl].astype(F32)
            omv = om_ref[:, sl].astype(F32)
            zv = z_ref[:, sl].astype(F32)
            sz = _sigmoid(zv)
            dob = (dyv * (zv * sz)).astype(BF16)
            do_buf[h] = dob.astype(F32)
            for q, d in enumerate(dils):
                if d == 1:
                    do_refs[q][:, sl] = dob
            dz_ref[:, sl] = (dyv * omv * (sz * (1.0 + zv * (1.0 - sz)))).astype(BF16)
            acc = jnp.where(lane == h, jnp.sum(dob.astype(F32) * omv, axis=-1, keepdims=True), acc)
        dh_buf[0] = acc
        for q, d in enumerate(dils):
            if d == 1:
                dh_refs[q][...] = acc
            else:
                _emit_group_order(do_refs[q], do_buf, d, BF16)
                _emit_group_order(dh_refs[q], dh_buf, d, F32)

    row = pl.BlockSpec((ts, A), lambda i: (i, 0))
    outs = pl.pallas_call(
        body, name=name, grid=(S // ts,), in_specs=[row, row, row],
        out_specs=[_group_spec(d, ts, A) for d in dils] + [_group_spec(d, ts, LANES) for d in dils] + [row],
        out_shape=[_group_shape(d, S, A, BF16) for d in dils] + [_group_shape(d, S, LANES, F32) for d in dils]
        + [jax.ShapeDtypeStruct((S, A), BF16)],
        scratch_shapes=[_chunk_buf(ts, A), _chunk_buf(ts, LANES)],
        compiler_params=_params("parallel"),
    )(dy, om, z)
    return ([v.reshape(S, A) for v in outs[:n]], [v.reshape(S, LANES) for v in outs[n:2 * n]], outs[2 * n])


def _attn_bwd(q, kv, do, lse, dh, bias, dil, name):
    S, A = q.shape
    H = A // HEAD_DIM
    L = S // dil
    nb = L // BLOCK
    qv = q.reshape(dil, L, A)
    kvv = kv.reshape(2, dil, L, A)
    dov = do.reshape(dil, L, A)
    lsev = lse.reshape(dil, L, LANES)
    dhv = dh.reshape(dil, L, LANES)

    def body(*refs):
        (q_ref, qn_ref, kp_ref, kc_ref, vp_ref, vc_ref, do_ref, don_ref, l_ref, ln_ref, d_ref, dn_ref,
         b_ref) = refs[:13]
        dq_ref, dkv_ref, db_ref = refs[13:16]
        r = pl.program_id(0)
        i = pl.program_id(1)
        qi = lax.broadcasted_iota(jnp.int32, (BLOCK, BLOCK), 0)
        ki = lax.broadcasted_iota(jnp.int32, (BLOCK, BLOCK), 1)
        mask_c = ki <= qi
        band = ki >= qi
        mask_p = jnp.logical_and(band, i > 0)
        mask_n = jnp.logical_and(band, i < nb - 1)
        lane = lax.broadcasted_iota(jnp.int32, (BLOCK, LANES), 1)

        @pl.when(jnp.logical_and(r == 0, i == 0))
        def _():
            db_ref[...] = jnp.zeros_like(db_ref)

        def products(h):
            sl = slice(h * HEAD_DIM, (h + 1) * HEAD_DIM)
            q_i, q_n = q_ref[:, sl], qn_ref[:, sl]
            k_p, k_c = kp_ref[:, sl], kc_ref[:, sl]
            v_p, v_c = vp_ref[:, sl], vc_ref[:, sl]
            do_i, do_n = do_ref[:, sl], don_ref[:, sl]
            return (_dot(q_i, k_c, NT), _dot(do_i, v_c, NT), _dot(q_i, k_p, NT), _dot(do_i, v_p, NT),
                    _dot(q_n, k_c, NT), _dot(do_n, v_c, NT))

        ahead = [products(h) for h in range(min(PRODUCTS_AHEAD, H))]
        for h in range(H):
            sl = slice(h * HEAD_DIM, (h + 1) * HEAD_DIM)
            s1, dp1, s2, dp2, s3, dp3 = ahead.pop(0)
            if h + PRODUCTS_AHEAD < H:
                ahead.append(products(h + PRODUCTS_AHEAD))
            q_i, q_n = q_ref[:, sl], qn_ref[:, sl]
            k_p, k_c = kp_ref[:, sl], kc_ref[:, sl]
            do_i, do_n = do_ref[:, sl], don_ref[:, sl]
            l_i, l_n = _lane_col(l_ref[...], h, lane), _lane_col(ln_ref[...], h, lane)
            d_i, d_n = _lane_col(d_ref[...], h, lane), _lane_col(dn_ref[...], h, lane)
            b_c = b_ref[h, :, BLOCK:]
            b_p = b_ref[h, :, :BLOCK]
            p1 = jnp.exp(jnp.where(mask_c, s1 * SCALE + b_c, NEG) - l_i)
            ds1 = p1 * (dp1 - d_i)
            ds1b = ds1.astype(BF16)
            p1b = p1.astype(BF16)
            p2 = jnp.exp(jnp.where(mask_p, s2 * SCALE + b_p, NEG) - l_i)
            ds2 = p2 * (dp2 - d_i)
            ds2b = ds2.astype(BF16)
            p3 = jnp.exp(jnp.where(mask_n, s3 * SCALE + b_p, NEG) - l_n)
            ds3b = (p3 * (dp3 - d_n)).astype(BF16)
            p3b = p3.astype(BF16)
            dq = _dot(ds1b, k_c, NN) + _dot(ds2b, k_p, NN)
            dk = _dot(ds1b, q_i, TN) + _dot(ds3b, q_n, TN)
            dv = _dot(p1b, do_i, TN) + _dot(p3b, do_n, TN)
            dq_ref[:, sl] = (dq * SCALE).astype(BF16)
            dkv_ref[0, :, sl] = (dk * SCALE).astype(BF16)
            dkv_ref[1, :, sl] = dv.astype(BF16)
            db_ref[h, :, BLOCK:] += ds1
            db_ref[h, :, :BLOCK] += ds2

    def blk(slab, shift):
        if shift < 0:
            return pl.BlockSpec((None, None, BLOCK, A), lambda r, i: (slab, r, jnp.maximum(i - 1, 0), 0))
        return pl.BlockSpec((None, None, BLOCK, A), lambda r, i: (slab, r, i, 0))

    def row(width, shift):
        if shift > 0:
            return pl.BlockSpec((None, BLOCK, width), lambda r, i: (r, jnp.minimum(i + 1, nb - 1), 0))
        return pl.BlockSpec((None, BLOCK, width), lambda r, i: (r, i, 0))

    in_specs = [row(A, 0), row(A, 1), blk(0, -1), blk(0, 0), blk(1, -1), blk(1, 0),
                row(A, 0), row(A, 1), row(LANES, 0), row(LANES, 1), row(LANES, 0), row(LANES, 1),
                pl.BlockSpec((H, BLOCK, 2 * BLOCK), lambda r, i: (0, 0, 0))]
    dq, dkv, db = pl.pallas_call(
        body, name=name, grid=(dil, nb), in_specs=in_specs,
        out_specs=[pl.BlockSpec((None, BLOCK, A), lambda r, i: (r, i, 0)),
                   pl.BlockSpec((2, None, BLOCK, A), lambda r, i: (0, r, i, 0)),
                   pl.BlockSpec((H, BLOCK, 2 * BLOCK), lambda r, i: (0, 0, 0))],
        out_shape=[jax.ShapeDtypeStruct((dil, L, A), BF16), jax.ShapeDtypeStruct((2, dil, L, A), BF16),
                   jax.ShapeDtypeStruct((H, BLOCK, 2 * BLOCK), F32)],
        compiler_params=_params("arbitrary", "arbitrary"),
    )(qv, qv, kvv, kvv, kvv, kvv, dov, dov, lsev, lsev, dhv, dhv, bias)
    return dq.reshape(S, A), dkv.reshape(2, S, A), db


def _sum_leading(stack, out_dtype, name, tr=256, tc=2048):
    n, R, C = stack.shape
    tr = _tile(R, tr, 16)
    tc = _tile(C, tc)

    def body(s_ref, o_ref):
        acc = s_ref[0].astype(F32)
        for q in range(1, n):
            acc = acc + s_ref[q].astype(F32)
        o_ref[...] = acc.astype(out_dtype)

    return pl.pallas_call(
        body, name=name, grid=(R // tr, C // tc),
        in_specs=[pl.BlockSpec((n, tr, tc), lambda i, j: (0, i, j))],
        out_specs=pl.BlockSpec((tr, tc), lambda i, j: (i, j)),
        out_shape=jax.ShapeDtypeStruct((R, C), out_dtype),
        compiler_params=_params("parallel", "parallel"),
    )(stack)


def _add_half(g, t, c_idx, kind, name, tr=256, tc=2048):
    R, C = t.shape
    tr = _tile(R, tr, 16)
    tc = _tile(C, tc)
    nrb, ncb = R // tr, C // tc

    def body(c_ref, g_ref, t_ref, o_ref):
        del c_ref
        o_ref[...] = (g_ref[...].astype(F32) + t_ref[...].astype(F32)).astype(BF16)

    if kind == "col":
        g_map = lambda i, j, c_ref: (c_ref[0] * nrb + i, j)
    else:
        g_map = lambda i, j, c_ref: (i, c_ref[0] * ncb + j)
    same = lambda i, j, c_ref: (i, j)
    return pl.pallas_call(
        body, name=name,
        grid_spec=pltpu.PrefetchScalarGridSpec(
            num_scalar_prefetch=1, grid=(nrb, ncb),
            in_specs=[pl.BlockSpec((tr, tc), g_map), pl.BlockSpec((tr, tc), same)],
            out_specs=pl.BlockSpec((tr, tc), same)),
        out_shape=jax.ShapeDtypeStruct((R, C), BF16),
        compiler_params=_params("parallel", "parallel"),
    )(c_idx, g, t)


def _cast_into_full(w, kind, chip_idx, name, tr=256, tc=2048):
    R, C = w.shape
    tr = _tile(R, tr, 16)
    tc = _tile(C, tc)
    nrb, ncb = R // tr, C // tc

    def body(k_ref, w_ref, o_ref):
        del k_ref
        o_ref[...] = w_ref[...].astype(BF16)

    if kind == "col":
        o_map = lambda i, j, k_ref: (i, k_ref[0] * ncb + j)
        full = (R, N_CHIPS * C)
    else:
        o_map = lambda i, j, k_ref: (k_ref[0] * nrb + i, j)
        full = (N_CHIPS * R, C)
    return pl.pallas_call(
        body, name=name,
        grid_spec=pltpu.PrefetchScalarGridSpec(
            num_scalar_prefetch=1, grid=(nrb, ncb),
            in_specs=[pl.BlockSpec((tr, tc), lambda i, j, k_ref: (i, j))],
            out_specs=pl.BlockSpec((tr, tc), o_map)),
        out_shape=jax.ShapeDtypeStruct(full, BF16),
        compiler_params=_params("parallel", "parallel"),
    )(chip_idx, w)


def _sum_into_shard(p, u, idx, kind, name, tr=256, tc=2048):
    _, R, C = u.shape
    tr = _tile(R, tr, 16)
    tc = _tile(C, tc)
    nrb, ncb = R // tr, C // tc

    def body(i_ref, p_ref, u_ref, o_ref):
        del i_ref
        acc = p_ref[...].astype(F32)
        for q in range(N_CHIPS - 1):
            acc = acc + u_ref[q].astype(F32)
        o_ref[...] = acc

    if kind == "col":
        p_map = lambda i, j, r: (i, r[0] * ncb + j)
        o_map = lambda i, j, r: (r[1] * nrb + i, j)
        full = (2 * R, C)
    else:
        p_map = lambda i, j, r: (r[0] * nrb + i, j)
        o_map = lambda i, j, r: (i, r[1] * ncb + j)
        full = (R, 2 * C)
    return pl.pallas_call(
        body, name=name,
        grid_spec=pltpu.PrefetchScalarGridSpec(
            num_scalar_prefetch=1, grid=(nrb, ncb),
            in_specs=[pl.BlockSpec((tr, tc), p_map), pl.BlockSpec((N_CHIPS - 1, tr, tc), lambda i, j, r: (0, i, j))],
            out_specs=pl.BlockSpec((tr, tc), o_map)),
        out_shape=jax.ShapeDtypeStruct(full, F32),
        compiler_params=_params("parallel", "parallel"),
    )(idx, p, u)


def _adamw(w, g, m, v, name, tr=256, tc=2048):
    R, C = w.shape
    tr = _tile(R, tr, 8)
    tc = _tile(C, tc)
    c1 = 1.0 - ADAM_B1 ** ADAM_STEP
    c2 = 1.0 - ADAM_B2 ** ADAM_STEP

    def body(w_ref, g_ref, m_ref, v_ref, d_ref, nm_ref, nv_ref):
        gv = g_ref[...]
        nm = ADAM_B1 * m_ref[...] + (1.0 - ADAM_B1) * gv
        nv = ADAM_B2 * v_ref[...] + (1.0 - ADAM_B2) * (gv * gv)
        d_ref[...] = -ADAM_LR * ((nm / c1) / (jnp.sqrt(nv / c2) + ADAM_EPS) + ADAM_WD * w_ref[...])
        nm_ref[...] = nm
        nv_ref[...] = nv

    blk = pl.BlockSpec((tr, tc), lambda i, j: (i, j))
    sh = jax.ShapeDtypeStruct((R, C), F32)
    return pl.pallas_call(
        body, name=name, grid=(R // tr, C // tc), in_specs=[blk] * 4, out_specs=[blk] * 3,
        out_shape=[sh, sh, sh], compiler_params=_params("parallel", "parallel"),
    )(w, g, m, v)


def _xyc():
    return lax.axis_index("x"), lax.axis_index("y"), lax.axis_index("c")


def _drain(copies):
    for cp in copies:
        if cp.is_remote:
            cp.wait_send()
        else:
            cp.wait()


def _other_chips(x, y):
    return [(1 - x, y), (x, 1 - y), (1 - x, 1 - y)]


def _allgather_small(blk, name, after=None):
    R, C = blk.shape
    extra = _as_list(after)

    def body(*refs):
        x_ref = refs[0]
        out_ref, send_sems, recv_sems, local_sem = refs[1 + len(extra):]
        x, y, c = _xyc()
        me = 4 * x + 2 * y + c
        mine = pltpu.make_async_copy(x_ref, out_ref.at[me], local_sem)
        mine.start()
        peers = []
        for k in range(1, N_DEV):
            px = 1 - x if (k >> 2) & 1 else x
            py = 1 - y if (k >> 1) & 1 else y
            pc = 1 - c if k & 1 else c
            peers.append((px, py, pc))
        sends = []
        for k, peer in enumerate(peers):
            cp = pltpu.make_async_remote_copy(
                src_ref=x_ref, dst_ref=out_ref.at[me], send_sem=send_sems.at[k], recv_sem=recv_sems.at[k],
                device_id=peer, device_id_type=MESH)
            cp.start()
            sends.append(cp)
        for k, (px, py, pc) in enumerate(peers):
            pltpu.make_async_remote_copy(
                src_ref=x_ref, dst_ref=out_ref.at[4 * px + 2 * py + pc], send_sem=send_sems.at[k],
                recv_sem=recv_sems.at[k], device_id=(px, py, pc), device_id_type=MESH).wait_recv()
        for cp in sends:
            cp.wait_send()
        mine.wait()

    return pl.pallas_call(
        body, name=name, in_specs=[VMEM_SPEC] + [ANY] * len(extra), out_specs=VMEM_SPEC,
        out_shape=jax.ShapeDtypeStruct((N_DEV, R, C), blk.dtype),
        scratch_shapes=[pltpu.SemaphoreType.DMA((N_DEV - 1,)), pltpu.SemaphoreType.DMA((N_DEV - 1,)),
                        pltpu.SemaphoreType.DMA],
        compiler_params=pltpu.CompilerParams(vmem_limit_bytes=VMEM_LIMIT),
    )(blk, *extra)


def _full_region(ref, kind, chip, half, shard_shape):
    r, cn = shard_shape
    hr = r // 2
    if kind == "col":
        rows = pl.ds(0, r) if half is None else pl.ds(pl.multiple_of(half * hr, 16), hr)
        return ref.at[rows, pl.ds(pl.multiple_of(chip * cn, LANES), cn)]
    if half is None:
        return ref.at[pl.ds(pl.multiple_of(chip * r, 16), r), :]
    return ref.at[pl.ds(pl.multiple_of(chip * r + half * hr, 16), hr), :]


def _allgather_weights(fulls, kinds, shapes, name):
    n = len(fulls)

    def body(*refs):
        outs = refs[n:2 * n]
        send_sems, recv_sems = refs[2 * n:]
        x, y, c = _xyc()
        chip = 2 * x + y
        sib = (x, y, 1 - c)
        others = _other_chips(x, y)
        started = []
        for w in range(n):
            mine = _full_region(outs[w], kinds[w], chip, c, shapes[w])
            for j, (ox, oy) in enumerate(others):
                cp = pltpu.make_async_remote_copy(
                    src_ref=mine, dst_ref=mine, send_sem=send_sems.at[6 * w + j], recv_sem=recv_sems.at[6 * w + j],
                    device_id=(ox, oy, c), device_id_type=MESH)
                cp.start()
                started.append(cp)
        for w in range(n):
            for j, (ox, oy) in enumerate(others):
                landed = _full_region(outs[w], kinds[w], 2 * ox + oy, c, shapes[w])
                pltpu.make_async_remote_copy(
                    src_ref=landed, dst_ref=landed, send_sem=send_sems.at[6 * w + j], recv_sem=recv_sems.at[6 * w + j],
                    device_id=(ox, oy, c), device_id_type=MESH).wait_recv()
                cp = pltpu.make_async_remote_copy(
                    src_ref=landed, dst_ref=landed, send_sem=send_sems.at[6 * w + 3 + j],
                    recv_sem=recv_sems.at[6 * w + 3 + j], device_id=sib, device_id_type=MESH)
                cp.start()
                started.append(cp)
        for w in range(n):
            for j, (ox, oy) in enumerate(others):
                theirs = _full_region(outs[w], kinds[w], 2 * ox + oy, 1 - c, shapes[w])
                pltpu.make_async_remote_copy(
                    src_ref=theirs, dst_ref=theirs, send_sem=send_sems.at[6 * w + 3 + j],
                    recv_sem=recv_sems.at[6 * w + 3 + j], device_id=sib, device_id_type=MESH).wait_recv()
        _drain(started)

    return pl.pallas_call(
        body, name=name, in_specs=[ANY] * n, out_specs=[ANY] * n,
        out_shape=[jax.ShapeDtypeStruct(f.shape, f.dtype) for f in fulls],
        input_output_aliases={w: w for w in range(n)},
        scratch_shapes=[pltpu.SemaphoreType.DMA((6 * n,)), pltpu.SemaphoreType.DMA((6 * n,))],
    )(*fulls)


def _region_of_size(ref, kind, shard_shape, count):
    r, cn = shard_shape
    if kind == "col":
        return ref.at[pl.ds(0, r // 2), pl.ds(0, count * cn)]
    return ref.at[pl.ds(0, count * (r // 2)), :]


def _allgather_weights_seq(fulls, kinds, shapes, name, collective_id):
    n = len(fulls)
    refs = [jax.new_ref(f, memory_space=pltpu.MemorySpace.HBM) for f in fulls]

    def body(send_sems, recv_sems):
        x, y, c = _xyc()
        chip = 2 * x + y
        sib = (x, y, 1 - c)
        others = _other_chips(x, y)
        peers = [(ox, oy, c) for ox, oy in others] + [sib]
        barrier = pltpu.get_barrier_semaphore()
        for peer in peers:
            pl.semaphore_signal(barrier, inc=1, device_id=peer, device_id_type=MESH)
        pl.semaphore_wait(barrier, len(peers))

        def copy(w, region, sem, to):
            return pltpu.make_async_remote_copy(src_ref=region, dst_ref=region, send_sem=send_sems.at[sem],
                                                recv_sem=recv_sems.at[sem], device_id=to, device_id_type=MESH)

        for w in range(n):
            mine = _full_region(refs[w], kinds[w], chip, c, shapes[w])
            for ox, oy in others:
                copy(w, mine, 2 * w, (ox, oy, c)).start()
        for w in range(n):
            three = _region_of_size(refs[w], kinds[w], shapes[w], 3)
            copy(w, three, 2 * w, sib).wait_recv()
            for ox, oy in others:
                copy(w, _full_region(refs[w], kinds[w], 2 * ox + oy, c, shapes[w]), 2 * w + 1, sib).start()
        for w in range(n):
            three = _region_of_size(refs[w], kinds[w], shapes[w], 3)
            copy(w, three, 2 * w + 1, sib).wait_recv()
            copy(w, three, 2 * w, sib).wait_send()
            copy(w, three, 2 * w + 1, sib).wait_send()

    pl.kernel(
        body, out_type=(), mesh=plsc.ScalarSubcoreMesh(axis_name="seq", num_cores=1), name=name,
        scratch_types=[pltpu.SemaphoreType.DMA((2 * n,)), pltpu.SemaphoreType.DMA((2 * n,))],
        compiler_params=pltpu.CompilerParams(collective_id=collective_id),
    )()
    return [r[...] for r in refs]


def _half_of(ref, kind, half):
    r, cn = ref.shape
    if kind == "col":
        return ref.at[pl.ds(pl.multiple_of(half * (r // 2), 16), r // 2), :]
    return ref.at[:, pl.ds(pl.multiple_of(half * (cn // 2), LANES), cn // 2)]


def _shard_of(ref, kind, chip):
    r, cn = ref.shape
    if kind == "col":
        return ref.at[:, pl.ds(pl.multiple_of(chip * (cn // N_CHIPS), LANES), cn // N_CHIPS)]
    return ref.at[pl.ds(pl.multiple_of(chip * (r // N_CHIPS), 16), r // N_CHIPS), :]


def _exchange_halves(grads, kinds, name):
    n = len(grads)

    def body(*refs):
        gs = refs[:n]
        ts = refs[n:2 * n]
        send_sems, recv_sems = refs[2 * n:]
        x, y, c = _xyc()
        cps = []
        for w in range(n):
            cp = pltpu.make_async_remote_copy(
                src_ref=_half_of(gs[w], kinds[w], 1 - c), dst_ref=ts[w], send_sem=send_sems.at[w],
                recv_sem=recv_sems.at[w], device_id=(x, y, 1 - c), device_id_type=MESH)
            cp.start()
            cps.append(cp)
        for cp in cps:
            cp.wait()

    out_shape = []
    for gr, kind in zip(grads, kinds):
        r, cn = gr.shape
        out_shape.append(jax.ShapeDtypeStruct((r // 2, cn) if kind == "col" else (r, cn // 2), gr.dtype))
    return pl.pallas_call(
        body, name=name, in_specs=[ANY] * n, out_specs=[ANY] * n, out_shape=out_shape,
        scratch_shapes=[pltpu.SemaphoreType.DMA((n,)), pltpu.SemaphoreType.DMA((n,))],
    )(*grads)


def _scatter_partials(parts, kinds, name):
    n = len(parts)

    def body(*refs):
        ps = refs[:n]
        us = refs[n:2 * n]
        send_sems, recv_sems = refs[2 * n:]
        x, y, c = _xyc()
        others = _other_chips(x, y)
        cps = []
        for w in range(n):
            for j, (ox, oy) in enumerate(others):
                cp = pltpu.make_async_remote_copy(
                    src_ref=_shard_of(ps[w], kinds[w], 2 * ox + oy), dst_ref=us[w].at[j],
                    send_sem=send_sems.at[3 * w + j], recv_sem=recv_sems.at[3 * w + j],
                    device_id=(ox, oy, c), device_id_type=MESH)
                cp.start()
                cps.append(cp)
        for cp in cps:
            cp.wait()

    out_shape = []
    for p, kind in zip(parts, kinds):
        r, cn = p.shape
        hs = (r, cn // N_CHIPS) if kind == "col" else (r // N_CHIPS, cn)
        out_shape.append(jax.ShapeDtypeStruct((N_CHIPS - 1,) + hs, p.dtype))
    return pl.pallas_call(
        body, name=name, in_specs=[ANY] * n, out_specs=[ANY] * n, out_shape=out_shape,
        scratch_shapes=[pltpu.SemaphoreType.DMA((3 * n,)), pltpu.SemaphoreType.DMA((3 * n,))],
    )(*parts)


def _scatter_partials_seq(parts, kinds, name, collective_id):
    n = len(parts)

    def body(*refs):
        ps = refs[:n]
        us = refs[n:2 * n]
        send_sems, recv_sems = refs[2 * n:]
        x, y, c = _xyc()
        others = _other_chips(x, y)
        barrier = pltpu.get_barrier_semaphore()
        for ox, oy in others:
            pl.semaphore_signal(barrier, inc=1, device_id=(ox, oy, c), device_id_type=MESH)
        pl.semaphore_wait(barrier, len(others))
        for w in range(n):
            for j, (ox, oy) in enumerate(others):
                pltpu.make_async_remote_copy(
                    src_ref=_shard_of(ps[w], kinds[w], 2 * ox + oy), dst_ref=us[w].at[j],
                    send_sem=send_sems.at[w], recv_sem=recv_sems.at[w],
                    device_id=(ox, oy, c), device_id_type=MESH).start()
        for w in range(n):
            pltpu.make_async_remote_copy(
                src_ref=us[w], dst_ref=us[w], send_sem=send_sems.at[w], recv_sem=recv_sems.at[w],
                device_id=(x, y, c), device_id_type=MESH).wait()

    out_type = []
    for p, kind in zip(parts, kinds):
        r, cn = p.shape
        hs = (r, cn // N_CHIPS) if kind == "col" else (r // N_CHIPS, cn)
        out_type.append(jax.ShapeDtypeStruct((N_CHIPS - 1,) + hs, p.dtype))
    return pl.kernel(
        body, out_type=out_type, mesh=plsc.ScalarSubcoreMesh(axis_name="seq", num_cores=1), name=name,
        scratch_types=[pltpu.SemaphoreType.DMA((n,)), pltpu.SemaphoreType.DMA((n,))],
        compiler_params=pltpu.CompilerParams(collective_id=collective_id),
    )(*parts)


def _join_halves(halves, kinds, name):
    n = len(halves)

    def body(*refs):
        outs = refs[n:2 * n]
        send_sems, recv_sems = refs[2 * n:]
        x, y, c = _xyc()
        cps = []
        for w in range(n):
            mine = _half_of(outs[w], kinds[w], c)
            cp = pltpu.make_async_remote_copy(
                src_ref=mine, dst_ref=mine, send_sem=send_sems.at[w], recv_sem=recv_sems.at[w],
                device_id=(x, y, 1 - c), device_id_type=MESH)
            cp.start()
            cps.append(cp)
        for w in range(n):
            theirs = _half_of(outs[w], kinds[w], 1 - c)
            pltpu.make_async_remote_copy(
                src_ref=theirs, dst_ref=theirs, send_sem=send_sems.at[w], recv_sem=recv_sems.at[w],
                device_id=(x, y, 1 - c), device_id_type=MESH).wait_recv()
        _drain(cps)

    return pl.pallas_call(
        body, name=name, in_specs=[ANY] * n, out_specs=[ANY] * n,
        out_shape=[jax.ShapeDtypeStruct(h.shape, h.dtype) for h in halves],
        input_output_aliases={w: w for w in range(n)},
        scratch_shapes=[pltpu.SemaphoreType.DMA((n,)), pltpu.SemaphoreType.DMA((n,))],
    )(*halves)


def kernel(x, a_norm, a_w_in, a_conv_w, a_conv_b, a_ln_g, a_ln_b, a_w_out, kv_norm, w_kv, b_norm, b_w_in, b_w_out, rel_bias, final_norm, loss_target, m_a_norm, m_a_w_in, m_a_conv_w, m_a_conv_b, m_a_ln_g, m_a_ln_b, m_a_w_out, m_kv_norm, m_w_kv, m_b_norm, m_b_w_in, m_b_w_out, m_rel_bias, m_final_norm, v_a_norm, v_a_w_in, v_a_conv_w, v_a_conv_b, v_a_ln_g, v_a_ln_b, v_a_w_out, v_kv_norm, v_w_kv, v_b_norm, v_b_w_in, v_b_w_out, v_rel_bias, v_final_norm):
    S, D = x.shape[1], x.shape[2]
    E = a_w_out.shape[1] * N_CHIPS
    A = b_w_out.shape[1] * N_CHIPS
    H = A // HEAD_DIM
    DC = D // N_CHIPS
    xs = x.reshape(S, D)
    tgt = loss_target.reshape(S, D)
    cx, cy, cc = _xyc()
    chip = 2 * cx + cy
    c_idx = jnp.reshape(cc, (1,)).astype(jnp.int32)

    big_names = ["a_w_in", "a_w_out", "w_kv", "b_w_in", "b_w_out"]
    kinds = ["col", "row", "col", "col", "row"]
    big_w = [a_w_in[0], a_w_out[0], w_kv, b_w_in[0], b_w_out[0]]
    big_m = [m_a_w_in[0], m_a_w_out[0], m_w_kv, m_b_w_in[0], m_b_w_out[0]]
    big_v = [v_a_w_in[0], v_a_w_out[0], v_w_kv, v_b_w_in[0], v_b_w_out[0]]
    chip_idx = jnp.reshape(chip, (1,)).astype(jnp.int32)
    placed = [_cast_into_full(big_w[w], kinds[w], chip_idx, "cast_" + big_names[w]) for w in range(5)]
    shard_shapes = [w.shape for w in big_w]
    (wa_in,) = _allgather_weights_seq(placed[0:1], kinds[0:1], shard_shapes[0:1], "ag_seq_a_in", 0)
    wa_out, wkv = _allgather_weights_seq(placed[1:3], kinds[1:3], shard_shapes[1:3], "ag_seq_a_out_kv", 1)
    wb_in, wb_out = _allgather_weights_seq(placed[3:5], kinds[3:5], shard_shapes[3:5], "ag_seq_b", 2)

    def row_at(vec, q):
        return jnp.pad(vec, ((q, 7 - q), (0, 0)))

    def pack_sharded(an, cw, cb, lg, lb):
        return jnp.concatenate([row_at(an, 0), jnp.pad(cw[0], ((0, 1), (0, 0))),
                                row_at(lg, 0) + row_at(lb, 1) + row_at(cb, 2)], axis=0)

    small_w = pack_sharded(a_norm, a_conv_w, a_conv_b, a_ln_g, a_ln_b)
    gathered = _allgather_small(small_w, "ag_small_params")
    small_full = jnp.concatenate([gathered[2 * k] for k in range(N_CHIPS)], axis=1)
    g_a = small_full[0:1]
    conv_w32 = small_full[8:8 + HALO]
    ln_g = small_full[40:41]
    ln_b = small_full[41:42]
    conv_b = small_full[42:43]
    g_kv = kv_norm.reshape(1, D)
    g_b = b_norm.reshape(1, D)
    g_f = final_norm.reshape(1, D)

    rb_t = jnp.pad(rel_bias.T, ((0, 0), (0, LANES - N_BUCKETS)))
    onehots = [_onehot(dil) for _, dil in GROUPS]
    biases = [_bias_table(rb_t, onehots[g], "bias_table_%d" % g).reshape(H, BLOCK, 2 * BLOCK)
              for g in range(len(GROUPS))]

    dils = tuple(dil for _, dil in GROUPS)
    assert dils[0] == 1
    n_g = len(GROUPS)
    ((h0,),) = _rms_fwd(xs, [g_a], (1,), "rms_a")
    proj3 = _matmul(h0, wa_in, "nn", BF16, "mm_a_in", out_slab=E)
    conv = _conv_fwd(proj3, conv_w32, conv_b, "conv_fwd")
    y_a = _ln_gate_fwd(conv, proj3, ln_g, ln_b, "ln_gate_fwd")
    x1 = _matmul(y_a, wa_out, "nn", F32, "mm_a_out", res=xs)
    hks, hbs = _rms_fwd(x1, [g_kv, g_b], dils, "rms_kv_b")
    kvs = [_matmul(hks[g], wkv, "nn", BF16, "mm_kv_%d" % g, out_slab=A, b_off=2 * g * A, n_cols=2 * A)
           for g in range(n_g)]
    qs = [_matmul(hbs[g], wb_in, "nn", BF16, "mm_q_%d" % g, b_off=g * A, n_cols=A, after=kvs[-1])
          for g in range(n_g)]
    zb = _matmul(hbs[0], wb_in, "nn", BF16, "mm_zb", b_off=n_g * A, n_cols=A, after=kvs[-1])
    os_, lses = [], []
    for g, dil in enumerate(dils):
        o_g, lse_g = _attn_fwd(qs[g], kvs[g], biases[g], dil, "attn_fwd_%d" % g)
        os_.append(o_g)
        lses.append(lse_g)
    y_b, o_m, lse_d = _attn_merge(os_, lses, zb, dils, "attn_merge")
    x2 = _matmul(y_b, wb_out, "nn", F32, "mm_b_out", res=x1)
    loss_part, dx2, dx2b, gg_f = _final_head(x2, g_f, tgt, "final_head")
    loss = lax.psum(loss_part[0, 0], ("x", "y", "c"))

    dw_tiles = dict(tm=512, tn=1024, tk=4096)
    dy_b = _matmul(dx2b, wb_out, "nt", BF16, "mm_b_out_dx")
    dwb_out = _matmul(y_b, dx2b, "tn", BF16, "mm_b_out_dw", **dw_tiles)
    dos, dhs, dzb = _gate_bwd(dy_b, o_m, zb, dils, "gate_bwd")
    dbs, cots = [], []
    dwb_in = dwkv = None
    for g, dil in enumerate(dils):
        dq, dkv, db = _attn_bwd(qs[g], kvs[g], dos[g], lse_d[g], dhs[g], biases[g], dil, "attn_bwd_%d" % g)
        dbs.append(db.reshape(H, BLOCK * 2 * BLOCK))
        dwb_in = _matmul(hbs[g], dq, "tn", BF16, "mm_q_dw_%d" % g, out_off=g * A, out_cols=(n_g + 1) * A,
                         out_alias=dwb_in, **dw_tiles)
        dwkv = _matmul(hks[g], dkv, "tn", BF16, "mm_kv_dw_%d" % g, b_slab=True, out_off=2 * g * A,
                       out_cols=2 * n_g * A, out_alias=dwkv, **dw_tiles)
        cots.append((_matmul(dkv, wkv, "nt", BF16, "mm_kv_dx_%d" % g, a_slab=True, b_off=2 * g * A), 0, dil))
        cots.append((_matmul(dq, wb_in, "nt", BF16, "mm_q_dx_%d" % g, b_off=g * A), 1, dil))
    dwb_in = _matmul(hbs[0], dzb, "tn", BF16, "mm_zb_dw", out_off=n_g * A, out_cols=(n_g + 1) * A,
                     out_alias=dwb_in, **dw_tiles)
    cots.append((_matmul(dzb, wb_in, "nt", BF16, "mm_zb_dx", b_off=n_g * A), 1, 1))
    chip_c = jnp.stack([chip, cc]).astype(jnp.int32)

    def scatter_group(idx, grads, tag, collective_id):
        ks = [kinds[w] for w in idx]
        theirs = _exchange_halves(grads, ks, "rs_exchange_" + tag)
        parts = [_add_half(grads[q], theirs[q], c_idx, ks[q], "rs_add_half_%d" % w) for q, w in enumerate(idx)]
        return parts, _scatter_partials_seq(parts, ks, "rs_seq_" + tag, collective_id)

    def reduce_group(idx, parts, slots, tag):
        ks = [kinds[w] for w in idx]
        halves = [_sum_into_shard(parts[q], slots[q], chip_c, ks[q], "rs_sum_chips_%d" % w)
                  for q, w in enumerate(idx)]
        return _join_halves(halves, ks, "rs_join_" + tag)

    parts_b, slots_b = scatter_group([2, 3, 4], [dwkv, dwb_in, dwb_out], "b", 3)
    g_rel_t = _bias_grad(dbs, onehots, "bias_grad")
    dx1, dx1b, gg_kvb = _rms_bwd(x1, cots, [g_kv, g_b], dx2, "rms_kv_b_bwd", after=parts_b)
    dy_a = _matmul(dx1b, wa_out, "nt", BF16, "mm_a_out_dx")
    dwa_out = _matmul(y_a, dx1b, "tn", BF16, "mm_a_out_dw", **dw_tiles)
    dconv, dproj3, gg_ln = _ln_gate_bwd(conv, proj3, dy_a, ln_g, ln_b, "ln_gate_bwd")
    dproj3, g_conv_w = _conv_bwd(proj3, dconv, conv_w32, dproj3, "conv_bwd")
    dwa_in = _matmul(h0, dproj3, "tn", BF16, "mm_a_in_dw", b_slab=True, **dw_tiles)
    parts_a, slots_a = scatter_group([0, 1], [dwa_in, dwa_out], "a", 4)
    dh0 = _matmul(dproj3, wa_in, "nt", BF16, "mm_a_in_dx", a_slab=True, after=parts_a, tn=512)
    grad_x, _, gg_a = _rms_bwd(xs, [(dh0, 0, 1)], [g_a], dx1, "rms_a_bwd")

    big_g = [None] * 5
    big_g[2:5] = reduce_group([2, 3, 4], parts_b, slots_b, "b")
    big_g[0:2] = reduce_group([0, 1], parts_a, slots_a, "a")

    def rel_rows(rb):
        return jnp.pad(rb.reshape(1, N_BUCKETS * H), ((0, 7), (0, D - N_BUCKETS * H)))

    small_g = jnp.concatenate([gg_a, g_conv_w, gg_ln, gg_kvb, gg_f, rel_rows(g_rel_t[:, :N_BUCKETS].T)], axis=0)
    small_sum = _sum_leading(_allgather_small(small_g, "ag_small_grads", after=[slots_a[0], slots_b[0]]), F32,
                             "sum_small_grads", tr=72)
    g_sharded = lax.dynamic_slice(small_sum, (0, chip * DC), (48, DC))
    g_repl = small_sum[48:72]

    outs_g, outs_d, outs_m, outs_v = {}, {}, {}, {}
    for w, nm in enumerate(big_names):
        d_, m_, v_ = _adamw(big_w[w], big_g[w], big_m[w], big_v[w], "adamw_" + nm)
        outs_g[nm], outs_d[nm], outs_m[nm], outs_v[nm] = big_g[w], d_, m_, v_
    sm_m = pack_sharded(m_a_norm, m_a_conv_w, m_a_conv_b, m_a_ln_g, m_a_ln_b)
    sm_v = pack_sharded(v_a_norm, v_a_conv_w, v_a_conv_b, v_a_ln_g, v_a_ln_b)
    sd, smm, svv = _adamw(small_w, g_sharded, sm_m, sm_v, "adamw_small_sharded")

    def unpack_sharded(p):
        return {"a_norm": p[0:1], "a_conv_w": p[8:8 + CONV_TAPS].reshape(1, CONV_TAPS, DC), "a_ln_g": p[40:41],
                "a_ln_b": p[41:42], "a_conv_b": p[42:43]}

    for src, dst in ((g_sharded, outs_g), (sd, outs_d), (smm, outs_m), (svv, outs_v)):
        dst.update(unpack_sharded(src))

    def pack_repl(kn, bn, fn, rb):
        return jnp.concatenate([row_at(kn.reshape(1, D), 0) + row_at(bn.reshape(1, D), 1),
                                row_at(fn.reshape(1, D), 0), rel_rows(rb)], axis=0)

    rp_w = pack_repl(kv_norm, b_norm, final_norm, rel_bias)
    rp_m = pack_repl(m_kv_norm, m_b_norm, m_final_norm, m_rel_bias)
    rp_v = pack_repl(v_kv_norm, v_b_norm, v_final_norm, v_rel_bias)
    rd, rmm, rvv = _adamw(rp_w, g_repl, rp_m, rp_v, "adamw_small_replicated")

    def unpack_repl(p):
        return {"kv_norm": p[0], "b_norm": p[1:2], "final_norm": p[8],
                "rel_bias": p[16, :N_BUCKETS * H].reshape(N_BUCKETS, H)}

    for src, dst in ((g_repl, outs_g), (rd, outs_d), (rmm, outs_m), (rvv, outs_v)):
        dst.update(unpack_repl(src))

    order = ["a_norm", "a_w_in", "a_conv_w", "a_conv_b", "a_ln_g", "a_ln_b", "a_w_out", "kv_norm", "w_kv",
             "b_norm", "b_w_in", "b_w_out", "rel_bias", "final_norm"]
    lead = {"a_w_in", "a_w_out", "b_w_in", "b_w_out"}

    def shaped(nm, val):
        return val[None] if nm in lead else val

    result = [loss, grad_x.reshape(1, S, D)]
    for table in (outs_g, outs_d, outs_m, outs_v):
        result.extend(shaped(nm, table[nm]) for nm in order)
    return tuple(result)
```

```python
import functools

import numpy as np
import jax
import jax.numpy as jnp
from jax import lax
from jax.experimental import pallas as pl
from jax.experimental.pallas import tpu as pltpu
from jax.experimental.pallas import tpu_sc as plsc

F32 = jnp.float32
BF16 = jnp.bfloat16
MESH = pl.DeviceIdType.MESH
ANY = pl.BlockSpec(memory_space=pl.ANY)
VMEM_SPEC = pl.BlockSpec(memory_space=pltpu.VMEM)

EPS = 1e-6
HEAD_DIM = 128
BLOCK = 128
GROUPS = ((128, 1), (512, 4), (2048, 16))
SCALE = HEAD_DIM ** -0.5
CONV_TAPS = 31
HALO = 32
N_BUCKETS = 32
MAX_EXACT = 16
MAX_DISTANCE = 2048
NEG = -1e30
PRODUCTS_AHEAD = 2
SCORES_AHEAD = 3
N_CHIPS = 4
N_DEV = 8
LANES = 128
VMEM_LIMIT = 56 * 1024 * 1024

ADAM_LR = 0.001
ADAM_B1 = 0.9
ADAM_B2 = 0.999
ADAM_EPS = 1e-08
ADAM_WD = 0.01
ADAM_STEP = 10


def _tile(n, pref, mult=LANES):
    t = (min(pref, n) // mult) * mult
    while t >= mult:
        if n % t == 0:
            return t
        t -= mult
    return n


def _params(*sem):
    return pltpu.CompilerParams(dimension_semantics=sem, vmem_limit_bytes=VMEM_LIMIT)


def _sigmoid(v):
    return 1.0 / (1.0 + jnp.exp(-v))


def _dot(a, b, dims):
    return lax.dot_general(a, b, (dims, ((), ())), preferred_element_type=F32)


NN = ((1,), (0,))
NT = ((1,), (1,))
TN = ((0,), (0,))


def _as_list(after):
    if after is None:
        return []
    return list(after) if isinstance(after, (list, tuple)) else [after]


def _stack_rows(rows, total):
    width = rows[0].shape[1]
    rid = lax.broadcasted_iota(jnp.int32, (total, width), 0)
    out = jnp.zeros((total, width), F32)
    for q, row in enumerate(rows):
        out = jnp.where(rid == q, jnp.broadcast_to(row, (total, width)), out)
    return out


def _lane_col(arr, h, lane):
    return jnp.sum(jnp.where(lane == h, arr, 0.0), axis=-1, keepdims=True)


def _matmul(a, b, mode, out_dtype, name, res=None, a_slab=False, b_slab=False, out_slab=0,
            b_off=0, n_cols=None, out_off=0, out_cols=None, out_alias=None, after=None,
            tm=512, tn=1024, tk=2048):
    if a_slab:
        na, M, W = a.shape
        K = na * W
    elif mode == "tn":
        K, M = a.shape
    else:
        M, K = a.shape
    if b_slab:
        nbs, _, Wb = b.shape
        N = nbs * Wb
    elif mode == "nt":
        N = b.shape[0]
    else:
        N = n_cols if n_cols else b.shape[1]
    tm = _tile(M, tm)
    tn = _tile(Wb if b_slab else (out_slab if out_slab else N), tn)
    tk = _tile(W if a_slab else K, tk)
    all_slabs = a_slab and mode == "nt" and tk == W
    if all_slabs:
        tk = K
    nk = K // tk
    grid = (M // tm, N // tn, nk)
    bo = b_off // (tk if mode == "nt" else tn)
    oo = out_off // tn

    if all_slabs:
        a_spec = pl.BlockSpec((na, tm, W), lambda i, j, k: (0, i, 0))
    elif a_slab:
        per = W // tk
        a_spec = pl.BlockSpec((None, tm, tk), lambda i, j, k: (k // per, i, k % per))
    elif mode == "tn":
        a_spec = pl.BlockSpec((tk, tm), lambda i, j, k: (k, i))
    else:
        a_spec = pl.BlockSpec((tm, tk), lambda i, j, k: (i, k))
    if b_slab:
        perb = Wb // tn
        b_spec = pl.BlockSpec((None, tk, tn), lambda i, j, k: (j // perb, k, j % perb))
    elif mode == "nt":
        b_spec = pl.BlockSpec((tn, tk), lambda i, j, k: (j, k + bo))
    else:
        b_spec = pl.BlockSpec((tk, tn), lambda i, j, k: (k, j + bo))
    if out_slab:
        pero = out_slab // tn
        o_spec = pl.BlockSpec((None, tm, tn), lambda i, j, k: (j // pero, i, j % pero))
        out_shape = jax.ShapeDtypeStruct((N // out_slab, M, out_slab), out_dtype)
    else:
        o_spec = pl.BlockSpec((tm, tn), lambda i, j, k: (i, j + oo))
        out_shape = jax.ShapeDtypeStruct((M, out_cols if out_cols else N), out_dtype)
    in_specs = [a_spec, b_spec]
    operands = [a, b]
    if res is not None:
        in_specs.append(pl.BlockSpec((tm, tn), lambda i, j, k: (i, j)))
        operands.append(res)
    aliases = {}
    if out_alias is not None:
        aliases[len(operands)] = 0
        in_specs.append(ANY)
        operands.append(out_alias)
    for arr in _as_list(after):
        in_specs.append(ANY)
        operands.append(arr)
    dims = {"nn": NN, "nt": NT, "tn": TN}[mode]
    has_res = res is not None
    n_in = len(operands)

    def body(*refs):
        a_ref, b_ref = refs[0], refs[1]
        r_ref = refs[2] if has_res else None
        o_ref = refs[n_in]
        if all_slabs:
            prod = _dot(a_ref[0], b_ref[:, 0:W], dims)
            for q in range(1, na):
                prod = prod + _dot(a_ref[q], b_ref[:, q * W:(q + 1) * W], dims)
        else:
            prod = _dot(a_ref[...], b_ref[...], dims)

        def finish(val):
            if has_res:
                val = val + r_ref[...]
            o_ref[...] = val.astype(out_dtype)

        if nk == 1:
            finish(prod)
        else:
            acc_ref = refs[n_in + 1]
            k = pl.program_id(2)

            @pl.when(k == 0)
            def _():
                acc_ref[...] = prod

            @pl.when(k > 0)
            def _():
                acc_ref[...] += prod

            @pl.when(k == nk - 1)
            def _():
                finish(acc_ref[...])

    scratch = [pltpu.VMEM((tm, tn), F32)] if nk > 1 else []
    return pl.pallas_call(
        body, name=name, grid=grid, in_specs=in_specs, out_specs=o_spec, out_shape=out_shape,
        scratch_shapes=scratch, input_output_aliases=aliases,
        compiler_params=_params("parallel", "parallel", "arbitrary"),
    )(*operands)


def _group_spec(d, ts, width):
    if d == 1:
        return pl.BlockSpec((ts, width), lambda i: (i, 0))
    return pl.BlockSpec((d, ts // d, width), lambda i: (0, i, 0))


def _group_shape(d, S, width, dtype):
    return jax.ShapeDtypeStruct((S, width) if d == 1 else (d, S // d, width), dtype)


def _chunk_buf(ts, width):
    return pltpu.VMEM((width // LANES, ts, LANES), F32)


def _fill_chunks(buf, val):
    for c in range(buf.shape[0]):
        buf[c] = val[:, c * LANES:(c + 1) * LANES]


def _read_chunks(buf):
    return jnp.concatenate([buf[c] for c in range(buf.shape[0])], axis=1)


def _emit_group_order(o_ref, buf, d, dtype):
    n = buf.shape[1] // d
    for r in range(d):
        for c in range(buf.shape[0]):
            o_ref[r, :, c * LANES:(c + 1) * LANES] = buf[c, pl.ds(r, n, stride=d), :].astype(dtype)


def _store_token_order(buf, i_ref, d):
    n = buf.shape[1] // d
    for r in range(d):
        for c in range(buf.shape[0]):
            buf[c, pl.ds(r, n, stride=d), :] = i_ref[r, :, c * LANES:(c + 1) * LANES].astype(F32)


def _rms_fwd(x, gains, dils, name, ts=256):
    S, D = x.shape
    ts = _tile(S, ts, 16 * max(dils))
    n = len(gains)
    nd = len(dils)

    def body(*refs):
        buf = refs[1 + n + n * nd]
        xv = refs[0][...]
        nrm = xv * lax.rsqrt(jnp.mean(xv * xv, axis=-1, keepdims=True) + EPS)
        for q in range(n):
            val = nrm * refs[1 + q][...]
            if max(dils) > 1:
                _fill_chunks(buf, val)
            for e, d in enumerate(dils):
                if d == 1:
                    refs[1 + n + q * nd + e][...] = val.astype(BF16)
                else:
                    _emit_group_order(refs[1 + n + q * nd + e], buf, d, BF16)

    row = pl.BlockSpec((ts, D), lambda i: (i, 0))
    vec = pl.BlockSpec((1, D), lambda i: (0, 0))
    outs = pl.pallas_call(
        body, name=name, grid=(S // ts,), in_specs=[row] + [vec] * n,
        out_specs=[_group_spec(d, ts, D) for _ in range(n) for d in dils],
        out_shape=[_group_shape(d, S, D, BF16) for _ in range(n) for d in dils],
        scratch_shapes=[_chunk_buf(ts, D)],
        compiler_params=_params("parallel"),
    )(x, *gains)
    return [[outs[q * nd + e].reshape(S, D) for e in range(nd)] for q in range(n)]


def _rms_bwd(x, cots, gains, dres, name, after=None, ts=256):
    S, D = x.shape
    ts = _tile(S, ts, 16 * max(d for _, _, d in cots))
    n = len(cots)
    ng = len(gains)
    extra = _as_list(after)
    n_in = 2 + n + ng + len(extra)

    def body(*refs):
        x_ref = refs[0]
        dh_refs = refs[1:1 + n]
        g_refs = refs[1 + n:1 + n + ng]
        dres_ref = refs[1 + n + ng]
        dx_ref, dxb_ref, gg_ref, buf = refs[n_in:n_in + 4]
        i = pl.program_id(0)
        xv = x_ref[...]
        r = lax.rsqrt(jnp.mean(xv * xv, axis=-1, keepdims=True) + EPS)
        nrm = xv * r
        dn = jnp.zeros_like(xv)
        rows = [jnp.zeros((1, D), F32) for _ in range(ng)]
        for q, (_, gi, d) in enumerate(cots):
            if d == 1:
                dh = dh_refs[q][...].astype(F32)
            else:
                _store_token_order(buf, dh_refs[q], d)
                dh = _read_chunks(buf)
            dn = dn + dh * g_refs[gi][...]
            rows[gi] = rows[gi] + jnp.sum(dh * nrm, axis=0, keepdims=True)
        dx = dres_ref[...] + r * (dn - nrm * jnp.mean(dn * nrm, axis=-1, keepdims=True))
        dx_ref[...] = dx
        dxb_ref[...] = dx.astype(BF16)
        upd = _stack_rows(rows, 8)

        @pl.when(i == 0)
        def _():
            gg_ref[...] = upd

        @pl.when(i > 0)
        def _():
            gg_ref[...] += upd

    row = pl.BlockSpec((ts, D), lambda i: (i, 0))
    vec = pl.BlockSpec((1, D), lambda i: (0, 0))
    acc = pl.BlockSpec((8, D), lambda i: (0, 0))
    return pl.pallas_call(
        body, name=name, grid=(S // ts,),
        in_specs=[row] + [_group_spec(d, ts, D) for _, _, d in cots] + [vec] * ng + [row] + [ANY] * len(extra),
        out_specs=[row, row, acc],
        out_shape=[jax.ShapeDtypeStruct((S, D), F32), jax.ShapeDtypeStruct((S, D), BF16),
                   jax.ShapeDtypeStruct((8, D), F32)],
        scratch_shapes=[_chunk_buf(ts, D)],
        compiler_params=_params("arbitrary"),
    )(x, *[a if d == 1 else a.reshape(d, S // d, D) for a, _, d in cots], *gains, dres, *extra)


def _final_head(x2, gain, target, name, ts=256):
    S, D = x2.shape
    ts = _tile(S, ts, 16)

    def body(x_ref, g_ref, t_ref, loss_ref, dx_ref, dxb_ref, gg_ref):
        i = pl.program_id(0)
        xv = x_ref[...]
        g = g_ref[...]
        r = lax.rsqrt(jnp.mean(xv * xv, axis=-1, keepdims=True) + EPS)
        nrm = xv * r
        err = nrm * g - t_ref[...]
        part = 0.5 * jnp.sum(jnp.mean(err * err, axis=-1, keepdims=True), axis=0, keepdims=True)
        dout = err * (1.0 / D)
        dn = dout * g
        dx = r * (dn - nrm * jnp.mean(dn * nrm, axis=-1, keepdims=True))
        dx_ref[...] = dx
        dxb_ref[...] = dx.astype(BF16)
        upd = _stack_rows([jnp.sum(dout * nrm, axis=0, keepdims=True)], 8)
        lpart = jnp.broadcast_to(part, (1, LANES))

        @pl.when(i == 0)
        def _():
            gg_ref[...] = upd
            loss_ref[...] = lpart

        @pl.when(i > 0)
        def _():
            gg_ref[...] += upd
            loss_ref[...] += lpart

    row = pl.BlockSpec((ts, D), lambda i: (i, 0))
    vec = pl.BlockSpec((1, D), lambda i: (0, 0))
    return pl.pallas_call(
        body, name=name, grid=(S // ts,), in_specs=[row, vec, row],
        out_specs=[pl.BlockSpec((1, LANES), lambda i: (0, 0)), row, row, pl.BlockSpec((8, D), lambda i: (0, 0))],
        out_shape=[jax.ShapeDtypeStruct((1, LANES), F32), jax.ShapeDtypeStruct((S, D), F32),
                   jax.ShapeDtypeStruct((S, D), BF16), jax.ShapeDtypeStruct((8, D), F32)],
        compiler_params=_params("arbitrary"),
    )(x2, gain, target)


CONV_ROWS = 64


SUBLANES = 8


def _shifted_buf(ts, cw):
    return pltpu.VMEM((SUBLANES - 1, ts + HALO - SUBLANES, cw), F32)


def _fill_shifted(shifted, buf):
    rows = shifted.shape[1]
    for s in range(1, SUBLANES):
        shifted[s - 1] = buf[s:s + rows, :]


def _window(buf, shifted, off, rows):
    s = off % SUBLANES
    base = off - s
    if s == 0:
        return buf[base:base + rows, :]
    return shifted[s - 1, base:base + rows, :]


def _conv_fwd(proj3, conv_w32, conv_b, name, ts=256, cw=256):
    _, S, E = proj3.shape
    ts = _tile(S, ts, HALO)
    cw = _tile(E, cw)
    per = ts // HALO
    rc = min(CONV_ROWS, ts)

    def body(a_ref, b_ref, ap_ref, bp_ref, w_ref, cb_ref, c_ref, ubuf, shifted):
        i = pl.program_id(0)
        up = ap_ref[...].astype(F32) * _sigmoid(bp_ref[...].astype(F32))
        ubuf[0:HALO, :] = jnp.where(i > 0, up, 0.0)
        ubuf[HALO:HALO + ts, :] = a_ref[...].astype(F32) * _sigmoid(b_ref[...].astype(F32))
        _fill_shifted(shifted, ubuf)
        for r0 in range(0, ts, rc):
            acc = jnp.broadcast_to(cb_ref[...], (rc, cw))
            for k in range(CONV_TAPS):
                off = r0 + HALO - (CONV_TAPS - 1) + k
                acc = acc + _window(ubuf, shifted, off, rc) * w_ref[k:k + 1, :]
            c_ref[r0:r0 + rc, :] = acc

    return pl.pallas_call(
        body, name=name, grid=(S // ts, E // cw),
        in_specs=[
            pl.BlockSpec((None, ts, cw), lambda i, j: (0, i, j)),
            pl.BlockSpec((None, ts, cw), lambda i, j: (1, i, j)),
            pl.BlockSpec((None, HALO, cw), lambda i, j: (0, jnp.maximum(i * per - 1, 0), j)),
            pl.BlockSpec((None, HALO, cw), lambda i, j: (1, jnp.maximum(i * per - 1, 0), j)),
            pl.BlockSpec((HALO, cw), lambda i, j: (0, j)),
            pl.BlockSpec((1, cw), lambda i, j: (0, j)),
        ],
        out_specs=pl.BlockSpec((ts, cw), lambda i, j: (i, j)),
        out_shape=jax.ShapeDtypeStruct((S, E), F32),
        scratch_shapes=[pltpu.VMEM((HALO + ts, cw), F32), _shifted_buf(ts, cw)],
        compiler_params=_params("parallel", "parallel"),
    )(proj3, proj3, proj3, proj3, conv_w32, conv_b)


def _ln_gate_fwd(c, proj3, ln_g, ln_b, name, ts=256):
    S, E = c.shape
    ts = _tile(S, ts, 16)

    def body(c_ref, z_ref, g_ref, b_ref, y_ref):
        cv = c_ref[...]
        mu = jnp.mean(cv, axis=-1, keepdims=True)
        d = cv - mu
        var = jnp.mean(d * d, axis=-1, keepdims=True)
        cn = d * lax.rsqrt(var + EPS) * g_ref[...] + b_ref[...]
        z = z_ref[...].astype(F32)
        y_ref[...] = ((cn * _sigmoid(cn)).astype(F32) * (z * _sigmoid(z))).astype(BF16)

    row = pl.BlockSpec((ts, E), lambda i: (i, 0))
    vec = pl.BlockSpec((1, E), lambda i: (0, 0))
    return pl.pallas_call(
        body, name=name, grid=(S // ts,),
        in_specs=[row, pl.BlockSpec((None, ts, E), lambda i: (2, i, 0)), vec, vec],
        out_specs=row, out_shape=jax.ShapeDtypeStruct((S, E), BF16),
        compiler_params=_params("parallel"),
    )(c, proj3, ln_g, ln_b)


def _ln_gate_bwd(c, proj3, dy, ln_g, ln_b, name, ts=256):
    S, E = c.shape
    ts = _tile(S, ts, 16)

    def body(c_ref, z_ref, dy_ref, g_ref, b_ref, dc_ref, dz_ref, acc_ref):
        i = pl.program_id(0)
        cv = c_ref[...]
        g = g_ref[...]
        mu = jnp.mean(cv, axis=-1, keepdims=True)
        d = cv - mu
        var = jnp.mean(d * d, axis=-1, keepdims=True)
        rstd = lax.rsqrt(var + EPS)
        chat = d * rstd
        cn = chat * g + b_ref[...]
        z = z_ref[...].astype(F32)
        dyv = dy_ref[...].astype(F32)
        sc = _sigmoid(cn)
        sz = _sigmoid(z)
        dcn = dyv * (z * sz) * (sc * (1.0 + cn * (1.0 - sc)))
        dz_ref[...] = (dyv * (cn * sc) * (sz * (1.0 + z * (1.0 - sz)))).astype(BF16)
        dchat = dcn * g
        dcv = rstd * (dchat - jnp.mean(dchat, axis=-1, keepdims=True)
                      - chat * jnp.mean(dchat * chat, axis=-1, keepdims=True))
        dc_ref[...] = dcv
        upd = _stack_rows([jnp.sum(dcn * chat, axis=0, keepdims=True),
                           jnp.sum(dcn, axis=0, keepdims=True),
                           jnp.sum(dcv, axis=0, keepdims=True)], 8)

        @pl.when(i == 0)
        def _():
            acc_ref[...] = upd

        @pl.when(i > 0)
        def _():
            acc_ref[...] += upd

    row = pl.BlockSpec((ts, E), lambda i: (i, 0))
    vec = pl.BlockSpec((1, E), lambda i: (0, 0))
    return pl.pallas_call(
        body, name=name, grid=(S // ts,),
        in_specs=[row, pl.BlockSpec((None, ts, E), lambda i: (2, i, 0)), row, vec, vec],
        out_specs=[row, pl.BlockSpec((None, ts, E), lambda i: (2, i, 0)), pl.BlockSpec((8, E), lambda i: (0, 0))],
        out_shape=[jax.ShapeDtypeStruct((S, E), F32), jax.ShapeDtypeStruct((3, S, E), BF16),
                   jax.ShapeDtypeStruct((8, E), F32)],
        compiler_params=_params("arbitrary"),
    )(c, proj3, dy, ln_g, ln_b)


def _conv_bwd(proj3, dc, conv_w32, dproj3, name, ts=256, cw=256):
    _, S, E = proj3.shape
    ts = _tile(S, ts, HALO)
    cw = _tile(E, cw)
    per = ts // HALO
    n_i = S // ts
    last_halo = S // HALO - 1
    rc = min(CONV_ROWS, ts)

    def body(a_ref, b_ref, dc_ref, dcn_ref, w_ref, dp_in, dab_ref, dw_ref, dcbuf, ubuf, dwacc, shifted):
        del dp_in
        i = pl.program_id(1)
        dcbuf[0:ts, :] = dc_ref[...]
        dcbuf[ts:ts + HALO, :] = jnp.where(i < n_i - 1, dcn_ref[...], 0.0)
        _fill_shifted(shifted, dcbuf)
        av = a_ref[...].astype(F32)
        sb = _sigmoid(b_ref[...].astype(F32))
        ubuf[...] = av * sb

        @pl.when(i == 0)
        def _():
            dwacc[...] = jnp.zeros_like(dwacc)

        for r0 in range(0, ts, rc):
            uv = ubuf[r0:r0 + rc, :]
            du = jnp.zeros((rc, cw), F32)
            for d in range(CONV_TAPS):
                k = CONV_TAPS - 1 - d
                win = _window(dcbuf, shifted, r0 + d, rc)
                du = du + win * w_ref[k:k + 1, :]
                dwacc[k:k + 1, :] += jnp.sum(uv * win, axis=0, keepdims=True)
            a_c = a_ref[r0:r0 + rc, :].astype(F32)
            s_c = _sigmoid(b_ref[r0:r0 + rc, :].astype(F32))
            dab_ref[0, r0:r0 + rc, :] = (du * s_c).astype(BF16)
            dab_ref[1, r0:r0 + rc, :] = (du * a_c * s_c * (1.0 - s_c)).astype(BF16)

        @pl.when(i == n_i - 1)
        def _():
            dw_ref[...] = dwacc[...]

    return pl.pallas_call(
        body, name=name, grid=(E // cw, n_i),
        in_specs=[
            pl.BlockSpec((None, ts, cw), lambda j, i: (0, i, j)),
            pl.BlockSpec((None, ts, cw), lambda j, i: (1, i, j)),
            pl.BlockSpec((ts, cw), lambda j, i: (i, j)),
            pl.BlockSpec((HALO, cw), lambda j, i: (jnp.minimum((i + 1) * per, last_halo), j)),
            pl.BlockSpec((HALO, cw), lambda j, i: (0, j)),
            ANY,
        ],
        out_specs=[pl.BlockSpec((2, ts, cw), lambda j, i: (0, i, j)),
                   pl.BlockSpec((HALO, cw), lambda j, i: (0, j))],
        out_shape=[jax.ShapeDtypeStruct((3, S, E), BF16), jax.ShapeDtypeStruct((HALO, E), F32)],
        scratch_shapes=[pltpu.VMEM((ts + HALO, cw), F32), pltpu.VMEM((ts, cw), F32), pltpu.VMEM((HALO, cw), F32),
                        _shifted_buf(ts, cw)],
        input_output_aliases={5: 0},
        compiler_params=_params("parallel", "arbitrary"),
    )(proj3, proj3, dc, dc, conv_w32, dproj3)


def _bucket_table(dil):
    delta = (np.arange(BLOCK)[:, None] + BLOCK) - np.arange(2 * BLOCK)[None, :]
    dist = np.clip(delta, 0, None) * dil
    large = MAX_EXACT + (np.log(np.maximum(dist, 1).astype(np.float32) / MAX_EXACT)
                         / np.log(MAX_DISTANCE / MAX_EXACT) * (N_BUCKETS - MAX_EXACT)).astype(np.int32)
    large = np.minimum(large, N_BUCKETS - 1)
    return np.where(dist < MAX_EXACT, dist, large).astype(np.int32).reshape(-1)


def _onehot(dil):
    tbl = jnp.asarray(_bucket_table(dil))
    return (tbl[None, :] == jnp.arange(LANES, dtype=jnp.int32)[:, None]).astype(BF16)


def _split3(v):
    hi = v.astype(BF16)
    r1 = v - hi.astype(F32)
    mid = r1.astype(BF16)
    lo = (r1 - mid.astype(F32)).astype(BF16)
    return hi, mid, lo


def _bias_table(rb_t, onehot, name):
    H = rb_t.shape[0]
    N = onehot.shape[1]

    def body(r_ref, oh_ref, o_ref):
        oh = oh_ref[...]
        hi, mid, lo = _split3(r_ref[...])
        o_ref[...] = (_dot(lo, oh, NN) + _dot(mid, oh, NN)) + _dot(hi, oh, NN)

    return pl.pallas_call(
        body, name=name, in_specs=[VMEM_SPEC, VMEM_SPEC], out_specs=VMEM_SPEC,
        out_shape=jax.ShapeDtypeStruct((H, N), F32),
        compiler_params=pltpu.CompilerParams(vmem_limit_bytes=VMEM_LIMIT),
    )(rb_t, onehot)


def _bias_grad(dbs, onehots, name):
    H = dbs[0].shape[0]
    n = len(dbs)

    def body(*refs):
        acc = jnp.zeros((H, LANES), F32)
        for q in range(n):
            oh = refs[n + q][...]
            hi, mid, lo = _split3(refs[q][...])
            acc = acc + ((_dot(lo, oh, NT) + _dot(mid, oh, NT)) + _dot(hi, oh, NT))
        refs[2 * n][...] = acc

    return pl.pallas_call(
        body, name=name, in_specs=[VMEM_SPEC] * (2 * n), out_specs=VMEM_SPEC,
        out_shape=jax.ShapeDtypeStruct((H, LANES), F32),
        compiler_params=pltpu.CompilerParams(vmem_limit_bytes=VMEM_LIMIT),
    )(*dbs, *onehots)


def _attn_fwd(q, kv, bias, dil, name):
    S, A = q.shape
    H = A // HEAD_DIM
    L = S // dil
    nb = L // BLOCK
    qv = q.reshape(dil, L, A)
    kvv = kv.reshape(2, dil, L, A)

    def body(q_ref, kp_ref, kc_ref, vp_ref, vc_ref, b_ref, o_ref, lse_ref):
        i = pl.program_id(1)
        qi = lax.broadcasted_iota(jnp.int32, (BLOCK, BLOCK), 0)
        ki = lax.broadcasted_iota(jnp.int32, (BLOCK, BLOCK), 1)
        mask_c = ki <= qi
        mask_p = jnp.logical_and(ki >= qi, i > 0)
        lane = lax.broadcasted_iota(jnp.int32, (BLOCK, LANES), 1)
        lse_acc = jnp.zeros((BLOCK, LANES), F32)

        def scores(h):
            sl = slice(h * HEAD_DIM, (h + 1) * HEAD_DIM)
            qh = q_ref[:, sl]
            return (_dot(qh, kc_ref[:, sl], NT), _dot(qh, kp_ref[:, sl], NT))

        ahead = [scores(h) for h in range(min(SCORES_AHEAD, H))]
        for h in range(H):
            sl = slice(h * HEAD_DIM, (h + 1) * HEAD_DIM)
            raw_c, raw_p = ahead.pop(0)
            if h + SCORES_AHEAD < H:
                ahead.append(scores(h + SCORES_AHEAD))
            s_c = jnp.where(mask_c, raw_c * SCALE + b_ref[h, :, BLOCK:], NEG)
            s_p = jnp.where(mask_p, raw_p * SCALE + b_ref[h, :, :BLOCK], NEG)
            m = jnp.max(jnp.maximum(s_c, s_p), axis=-1, keepdims=True)
            p_c = jnp.exp(s_c - m)
            p_p = jnp.exp(s_p - m)
            den = jnp.sum(p_c + p_p, axis=-1, keepdims=True)
            acc = _dot(p_c.astype(BF16), vc_ref[:, sl], NN) + _dot(p_p.astype(BF16), vp_ref[:, sl], NN)
            o_ref[:, sl] = acc / den
            lse_acc = jnp.where(lane == h, m + jnp.log(den), lse_acc)
        lse_ref[...] = lse_acc

    def blk(slab, prev):
        if prev:
            return pl.BlockSpec((None, None, BLOCK, A), lambda r, i: (slab, r, jnp.maximum(i - 1, 0), 0))
        return pl.BlockSpec((None, None, BLOCK, A), lambda r, i: (slab, r, i, 0))

    o, lse = pl.pallas_call(
        body, name=name, grid=(dil, nb),
        in_specs=[pl.BlockSpec((None, BLOCK, A), lambda r, i: (r, i, 0)),
                  blk(0, True), blk(0, False), blk(1, True), blk(1, False),
                  pl.BlockSpec((H, BLOCK, 2 * BLOCK), lambda r, i: (0, 0, 0))],
        out_specs=[pl.BlockSpec((None, BLOCK, A), lambda r, i: (r, i, 0)),
                   pl.BlockSpec((None, BLOCK, LANES), lambda r, i: (r, i, 0))],
        out_shape=[jax.ShapeDtypeStruct((dil, L, A), F32), jax.ShapeDtypeStruct((dil, L, LANES), F32)],
        compiler_params=_params("parallel", "parallel"),
    )(qv, kvv, kvv, kvv, kvv, bias)
    return o.reshape(S, A), lse.reshape(S, LANES)


def _attn_merge(os_, lses, z, dils, name, ts=256):
    S, A = z.shape
    H = A // HEAD_DIM
    ts = _tile(S, ts, 16 * max(dils))
    n = len(os_)

    def body(*refs):
        z_ref = refs[2 * n]
        y_ref, om_ref = refs[2 * n + 1:2 * n + 3]
        lse_refs = refs[2 * n + 3:3 * n + 3]
        o_refs = refs[3 * n + 3:4 * n + 3]
        l_bufs = refs[4 * n + 3:5 * n + 3]
        lse_buf = refs[5 * n + 3]
        ls = []
        for q, d in enumerate(dils):
            if d == 1:
                ls.append(refs[n + q][...])
            else:
                _store_token_order(o_refs[q], refs[q], d)
                _store_token_order(l_bufs[q], refs[n + q], d)
                ls.append(l_bufs[q][0])
        m = ls[0]
        for q in range(1, n):
            m = jnp.maximum(m, ls[q])
        es = [jnp.exp(v - m) for v in ls]
        den = es[0]
        for q in range(1, n):
            den = den + es[q]
        alphas = [e / den for e in es]
        lse = m + jnp.log(den)
        lse_buf[0] = lse
        for q, d in enumerate(dils):
            if d == 1:
                lse_refs[q][...] = lse
            else:
                _emit_group_order(lse_refs[q], lse_buf, d, F32)
        lane = lax.broadcasted_iota(jnp.int32, (ts, LANES), 1)
        for h in range(H):
            sl = slice(h * HEAD_DIM, (h + 1) * HEAD_DIM)
            om = jnp.zeros((ts, HEAD_DIM), F32)
            for q, d in enumerate(dils):
                o_h = refs[q][:, sl] if d == 1 else o_refs[q][h]
                om = om + _lane_col(alphas[q], h, lane) * o_h
            z = z_ref[:, sl].astype(F32)
            y_ref[:, sl] = (om * (z * _sigmoid(z))).astype(BF16)
            om_ref[:, sl] = om.astype(BF16)

    row = pl.BlockSpec((ts, A), lambda i: (i, 0))
    outs = pl.pallas_call(
        body, name=name, grid=(S // ts,),
        in_specs=[_group_spec(d, ts, A) for d in dils] + [_group_spec(d, ts, LANES) for d in dils] + [row],
        out_specs=[row, row] + [_group_spec(d, ts, LANES) for d in dils],
        out_shape=[jax.ShapeDtypeStruct((S, A), BF16), jax.ShapeDtypeStruct((S, A), BF16)]
        + [_group_shape(d, S, LANES, F32) for d in dils],
        scratch_shapes=[_chunk_buf(ts, A)] * n + [_chunk_buf(ts, LANES)] * (n + 1),
        compiler_params=_params("parallel"),
    )(*[o if d == 1 else o.reshape(d, S // d, A) for o, d in zip(os_, dils)],
      *[v if d == 1 else v.reshape(d, S // d, LANES) for v, d in zip(lses, dils)], z)
    return outs[0], outs[1], [v.reshape(S, LANES) for v in outs[2:]]


def _gate_bwd(dy, om, z, dils, name, ts=256):
    S, A = dy.shape
    H = A // HEAD_DIM
    ts = _tile(S, ts, 16 * max(dils))
    n = len(dils)

    def body(*refs):
        dy_ref, om_ref, z_ref = refs[:3]
        do_refs = refs[3:3 + n]
        dh_refs = refs[3 + n:3 + 2 * n]
        dz_ref = refs[3 + 2 * n]
        do_buf, dh_buf = refs[4 + 2 * n:6 + 2 * n]
        lane = lax.broadcasted_iota(jnp.int32, (ts, LANES), 1)
        acc = jnp.zeros((ts, LANES), F32)
        for h in range(H):
            sl = slice(h * HEAD_DIM, (h + 1) * HEAD_DIM)
            dyv = dy_ref[:, sl].astype(F32)
            omv = om_ref[:, sl].astype(F32)
            zv = z_ref[:, sl].astype(F32)
            sz = _sigmoid(zv)
            dob = (dyv * (zv * sz)).astype(BF16)
            do_buf[h] = dob.astype(F32)
            for q, d in enumerate(dils):
                if d == 1:
                    do_refs[q][:, sl] = dob
            dz_ref[:, sl] = (dyv * omv * (sz * (1.0 + zv * (1.0 - sz)))).astype(BF16)
            acc = jnp.where(lane == h, jnp.sum(dob.astype(F32) * omv, axis=-1, keepdims=True), acc)
        dh_buf[0] = acc
        for q, d in enumerate(dils):
            if d == 1:
                dh_refs[q][...] = acc
            else:
                _emit_group_order(do_refs[q], do_buf, d, BF16)
                _emit_group_order(dh_refs[q], dh_buf, d, F32)

    row = pl.BlockSpec((ts, A), lambda i: (i, 0))
    outs = pl.pallas_call(
        body, name=name, grid=(S // ts,), in_specs=[row, row, row],
        out_specs=[_group_spec(d, ts, A) for d in dils] + [_group_spec(d, ts, LANES) for d in dils] + [row],
        out_shape=[_group_shape(d, S, A, BF16) for d in dils] + [_group_shape(d, S, LANES, F32) for d in dils]
        + [jax.ShapeDtypeStruct((S, A), BF16)],
        scratch_shapes=[_chunk_buf(ts, A), _chunk_buf(ts, LANES)],
        compiler_params=_params("parallel"),
    )(dy, om, z)
    return ([v.reshape(S, A) for v in outs[:n]], [v.reshape(S, LANES) for v in outs[n:2 * n]], outs[2 * n])


def _attn_bwd(q, kv, do, lse, dh, bias, dil, name):
    S, A = q.shape
    H = A // HEAD_DIM
    L = S // dil
    nb = L // BLOCK
    qv = q.reshape(dil, L, A)
    kvv = kv.reshape(2, dil, L, A)
    dov = do.reshape(dil, L, A)
    lsev = lse.reshape(dil, L, LANES)
    dhv = dh.reshape(dil, L, LANES)

    def body(*refs):
        (q_ref, qn_ref, kp_ref, kc_ref, vp_ref, vc_ref, do_ref, don_ref, l_ref, ln_ref, d_ref, dn_ref,
         b_ref) = refs[:13]
        dq_ref, dkv_ref, db_ref = refs[13:16]
        r = pl.program_id(0)
        i = pl.program_id(1)
        qi = lax.broadcasted_iota(jnp.int32, (BLOCK, BLOCK), 0)
        ki = lax.broadcasted_iota(jnp.int32, (BLOCK, BLOCK), 1)
        mask_c = ki <= qi
        band = ki >= qi
        mask_p = jnp.logical_and(band, i > 0)
        mask_n = jnp.logical_and(band, i < nb - 1)
        lane = lax.broadcasted_iota(jnp.int32, (BLOCK, LANES), 1)

        @pl.when(jnp.logical_and(r == 0, i == 0))
        def _():
            db_ref[...] = jnp.zeros_like(db_ref)

        def products(h):
            sl = slice(h * HEAD_DIM, (h + 1) * HEAD_DIM)
            q_i, q_n = q_ref[:, sl], qn_ref[:, sl]
            k_p, k_c = kp_ref[:, sl], kc_ref[:, sl]
            v_p, v_c = vp_ref[:, sl], vc_ref[:, sl]
            do_i, do_n = do_ref[:, sl], don_ref[:, sl]
            return (_dot(q_i, k_c, NT), _dot(do_i, v_c, NT), _dot(q_i, k_p, NT), _dot(do_i, v_p, NT),
                    _dot(q_n, k_c, NT), _dot(do_n, v_c, NT))

        ahead = [products(h) for h in range(min(PRODUCTS_AHEAD, H))]
        for h in range(H):
            sl = slice(h * HEAD_DIM, (h + 1) * HEAD_DIM)
            s1, dp1, s2, dp2, s3, dp3 = ahead.pop(0)
            if h + PRODUCTS_AHEAD < H:
                ahead.append(products(h + PRODUCTS_AHEAD))
            q_i, q_n = q_ref[:, sl], qn_ref[:, sl]
            k_p, k_c = kp_ref[:, sl], kc_ref[:, sl]
            do_i, do_n = do_ref[:, sl], don_ref[:, sl]
            l_i, l_n = _lane_col(l_ref[...], h, lane), _lane_col(ln_ref[...], h, lane)
            d_i, d_n = _lane_col(d_ref[...], h, lane), _lane_col(dn_ref[...], h, lane)
            b_c = b_ref[h, :, BLOCK:]
            b_p = b_ref[h, :, :BLOCK]
            p1 = jnp.exp(jnp.where(mask_c, s1 * SCALE + b_c, NEG) - l_i)
            ds1 = p1 * (dp1 - d_i)
            ds1b = ds1.astype(BF16)
            p1b = p1.astype(BF16)
            p2 = jnp.exp(jnp.where(mask_p, s2 * SCALE + b_p, NEG) - l_i)
            ds2 = p2 * (dp2 - d_i)
            ds2b = ds2.astype(BF16)
            p3 = jnp.exp(jnp.where(mask_n, s3 * SCALE + b_p, NEG) - l_n)
            ds3b = (p3 * (dp3 - d_n)).astype(BF16)
            p3b = p3.astype(BF16)
            dq = _dot(ds1b, k_c, NN) + _dot(ds2b, k_p, NN)
            dk = _dot(ds1b, q_i, TN) + _dot(ds3b, q_n, TN)
            dv = _dot(p1b, do_i, TN) + _dot(p3b, do_n, TN)
            dq_ref[:, sl] = (dq * SCALE).astype(BF16)
            dkv_ref[0, :, sl] = (dk * SCALE).astype(BF16)
            dkv_ref[1, :, sl] = dv.astype(BF16)
            db_ref[h, :, BLOCK:] += ds1
            db_ref[h, :, :BLOCK] += ds2

    def blk(slab, shift):
        if shift < 0:
            return pl.BlockSpec((None, None, BLOCK, A), lambda r, i: (slab, r, jnp.maximum(i - 1, 0), 0))
        return pl.BlockSpec((None, None, BLOCK, A), lambda r, i: (slab, r, i, 0))

    def row(width, shift):
        if shift > 0:
            return pl.BlockSpec((None, BLOCK, width), lambda r, i: (r, jnp.minimum(i + 1, nb - 1), 0))
        return pl.BlockSpec((None, BLOCK, width), lambda r, i: (r, i, 0))

    in_specs = [row(A, 0), row(A, 1), blk(0, -1), blk(0, 0), blk(1, -1), blk(1, 0),
                row(A, 0), row(A, 1), row(LANES, 0), row(LANES, 1), row(LANES, 0), row(LANES, 1),
                pl.BlockSpec((H, BLOCK, 2 * BLOCK), lambda r, i: (0, 0, 0))]
    dq, dkv, db = pl.pallas_call(
        body, name=name, grid=(dil, nb), in_specs=in_specs,
        out_specs=[pl.BlockSpec((None, BLOCK, A), lambda r, i: (r, i, 0)),
                   pl.BlockSpec((2, None, BLOCK, A), lambda r, i: (0, r, i, 0)),
                   pl.BlockSpec((H, BLOCK, 2 * BLOCK), lambda r, i: (0, 0, 0))],
        out_shape=[jax.ShapeDtypeStruct((dil, L, A), BF16), jax.ShapeDtypeStruct((2, dil, L, A), BF16),
                   jax.ShapeDtypeStruct((H, BLOCK, 2 * BLOCK), F32)],
        compiler_params=_params("arbitrary", "arbitrary"),
    )(qv, qv, kvv, kvv, kvv, kvv, dov, dov, lsev, lsev, dhv, dhv, bias)
    return dq.reshape(S, A), dkv.reshape(2, S, A), db


def _sum_leading(stack, out_dtype, name, tr=256, tc=2048):
    n, R, C = stack.shape
    tr = _tile(R, tr, 16)
    tc = _tile(C, tc)

    def body(s_ref, o_ref):
        acc = s_ref[0].astype(F32)
        for q in range(1, n):
            acc = acc + s_ref[q].astype(F32)
        o_ref[...] = acc.astype(out_dtype)

    return pl.pallas_call(
        body, name=name, grid=(R // tr, C // tc),
        in_specs=[pl.BlockSpec((n, tr, tc), lambda i, j: (0, i, j))],
        out_specs=pl.BlockSpec((tr, tc), lambda i, j: (i, j)),
        out_shape=jax.ShapeDtypeStruct((R, C), out_dtype),
        compiler_params=_params("parallel", "parallel"),
    )(stack)


def _add_half(g, t, c_idx, kind, name, after=None, tr=256, tc=2048):
    R, C = t.shape
    tr = _tile(R, tr, 16)
    tc = _tile(C, tc)
    nrb, ncb = R // tr, C // tc
    extra = _as_list(after)

    def body(c_ref, g_ref, t_ref, *rest):
        del c_ref
        o_ref = rest[len(extra)]
        o_ref[...] = (g_ref[...].astype(F32) + t_ref[...].astype(F32)).astype(BF16)

    if kind == "col":
        g_map = lambda i, j, c_ref: (c_ref[0] * nrb + i, j)
    else:
        g_map = lambda i, j, c_ref: (i, c_ref[0] * ncb + j)
    same = lambda i, j, c_ref: (i, j)
    return pl.pallas_call(
        body, name=name,
        grid_spec=pltpu.PrefetchScalarGridSpec(
            num_scalar_prefetch=1, grid=(nrb, ncb),
            in_specs=[pl.BlockSpec((tr, tc), g_map), pl.BlockSpec((tr, tc), same)] + [ANY] * len(extra),
            out_specs=pl.BlockSpec((tr, tc), same)),
        out_shape=jax.ShapeDtypeStruct((R, C), BF16),
        compiler_params=_params("parallel", "parallel"),
    )(c_idx, g, t, *extra)


def _cast_into_full(w, kind, chip_idx, name, tr=256, tc=2048):
    R, C = w.shape
    tr = _tile(R, tr, 16)
    tc = _tile(C, tc)
    nrb, ncb = R // tr, C // tc

    def body(k_ref, w_ref, o_ref):
        del k_ref
        o_ref[...] = w_ref[...].astype(BF16)

    if kind == "col":
        o_map = lambda i, j, k_ref: (i, k_ref[0] * ncb + j)
        full = (R, N_CHIPS * C)
    else:
        o_map = lambda i, j, k_ref: (k_ref[0] * nrb + i, j)
        full = (N_CHIPS * R, C)
    return pl.pallas_call(
        body, name=name,
        grid_spec=pltpu.PrefetchScalarGridSpec(
            num_scalar_prefetch=1, grid=(nrb, ncb),
            in_specs=[pl.BlockSpec((tr, tc), lambda i, j, k_ref: (i, j))],
            out_specs=pl.BlockSpec((tr, tc), o_map)),
        out_shape=jax.ShapeDtypeStruct(full, BF16),
        compiler_params=_params("parallel", "parallel"),
    )(chip_idx, w)


def _sum_into_shard(p, u, idx, kind, name, tr=256, tc=2048):
    _, R, C = u.shape
    tr = _tile(R, tr, 16)
    tc = _tile(C, tc)
    nrb, ncb = R // tr, C // tc

    def body(i_ref, p_ref, u_ref, o_ref):
        del i_ref
        acc = p_ref[...].astype(F32)
        for q in range(N_CHIPS - 1):
            acc = acc + u_ref[q].astype(F32)
        o_ref[...] = acc

    if kind == "col":
        p_map = lambda i, j, r: (i, r[0] * ncb + j)
        o_map = lambda i, j, r: (r[1] * nrb + i, j)
        full = (2 * R, C)
    else:
        p_map = lambda i, j, r: (r[0] * nrb + i, j)
        o_map = lambda i, j, r: (i, r[1] * ncb + j)
        full = (R, 2 * C)
    return pl.pallas_call(
        body, name=name,
        grid_spec=pltpu.PrefetchScalarGridSpec(
            num_scalar_prefetch=1, grid=(nrb, ncb),
            in_specs=[pl.BlockSpec((tr, tc), p_map), pl.BlockSpec((N_CHIPS - 1, tr, tc), lambda i, j, r: (0, i, j))],
            out_specs=pl.BlockSpec((tr, tc), o_map)),
        out_shape=jax.ShapeDtypeStruct(full, F32),
        compiler_params=_params("parallel", "parallel"),
    )(idx, p, u)


def _adamw(w, g, m, v, name, tr=256, tc=2048):
    R, C = w.shape
    tr = _tile(R, tr, 8)
    tc = _tile(C, tc)
    c1 = 1.0 - ADAM_B1 ** ADAM_STEP
    c2 = 1.0 - ADAM_B2 ** ADAM_STEP

    def body(w_ref, g_ref, m_ref, v_ref, d_ref, nm_ref, nv_ref):
        gv = g_ref[...]
        nm = ADAM_B1 * m_ref[...] + (1.0 - ADAM_B1) * gv
        nv = ADAM_B2 * v_ref[...] + (1.0 - ADAM_B2) * (gv * gv)
        d_ref[...] = -ADAM_LR * ((nm / c1) / (jnp.sqrt(nv / c2) + ADAM_EPS) + ADAM_WD * w_ref[...])
        nm_ref[...] = nm
        nv_ref[...] = nv

    blk = pl.BlockSpec((tr, tc), lambda i, j: (i, j))
    sh = jax.ShapeDtypeStruct((R, C), F32)
    return pl.pallas_call(
        body, name=name, grid=(R // tr, C // tc), in_specs=[blk] * 4, out_specs=[blk] * 3,
        out_shape=[sh, sh, sh], compiler_params=_params("parallel", "parallel"),
    )(w, g, m, v)


def _xyc():
    return lax.axis_index("x"), lax.axis_index("y"), lax.axis_index("c")


def _drain(copies):
    for cp in copies:
        if cp.is_remote:
            cp.wait_send()
        else:
            cp.wait()


def _other_chips(x, y):
    return [(1 - x, y), (x, 1 - y), (1 - x, 1 - y)]


def _allgather_small(blk, name, after=None):
    R, C = blk.shape
    extra = _as_list(after)

    def body(*refs):
        x_ref = refs[0]
        out_ref, send_sems, recv_sems, local_sem = refs[1 + len(extra):]
        x, y, c = _xyc()
        me = 4 * x + 2 * y + c
        mine = pltpu.make_async_copy(x_ref, out_ref.at[me], local_sem)
        mine.start()
        peers = []
        for k in range(1, N_DEV):
            px = 1 - x if (k >> 2) & 1 else x
            py = 1 - y if (k >> 1) & 1 else y
            pc = 1 - c if k & 1 else c
            peers.append((px, py, pc))
        sends = []
        for k, peer in enumerate(peers):
            cp = pltpu.make_async_remote_copy(
                src_ref=x_ref, dst_ref=out_ref.at[me], send_sem=send_sems.at[k], recv_sem=recv_sems.at[k],
                device_id=peer, device_id_type=MESH)
            cp.start()
            sends.append(cp)
        for k, (px, py, pc) in enumerate(peers):
            pltpu.make_async_remote_copy(
                src_ref=x_ref, dst_ref=out_ref.at[4 * px + 2 * py + pc], send_sem=send_sems.at[k],
                recv_sem=recv_sems.at[k], device_id=(px, py, pc), device_id_type=MESH).wait_recv()
        for cp in sends:
            cp.wait_send()
        mine.wait()

    return pl.pallas_call(
        body, name=name, in_specs=[VMEM_SPEC] + [ANY] * len(extra), out_specs=VMEM_SPEC,
        out_shape=jax.ShapeDtypeStruct((N_DEV, R, C), blk.dtype),
        scratch_shapes=[pltpu.SemaphoreType.DMA((N_DEV - 1,)), pltpu.SemaphoreType.DMA((N_DEV - 1,)),
                        pltpu.SemaphoreType.DMA],
        compiler_params=pltpu.CompilerParams(vmem_limit_bytes=VMEM_LIMIT),
    )(blk, *extra)


def _full_region(ref, kind, chip, half, shard_shape):
    r, cn = shard_shape
    hr = r // 2
    if kind == "col":
        rows = pl.ds(0, r) if half is None else pl.ds(pl.multiple_of(half * hr, 16), hr)
        return ref.at[rows, pl.ds(pl.multiple_of(chip * cn, LANES), cn)]
    if half is None:
        return ref.at[pl.ds(pl.multiple_of(chip * r, 16), r), :]
    return ref.at[pl.ds(pl.multiple_of(chip * r + half * hr, 16), hr), :]


def _allgather_weights(fulls, kinds, shapes, name):
    n = len(fulls)

    def body(*refs):
        outs = refs[n:2 * n]
        send_sems, recv_sems = refs[2 * n:]
        x, y, c = _xyc()
        chip = 2 * x + y
        sib = (x, y, 1 - c)
        others = _other_chips(x, y)
        started = []
        for w in range(n):
            mine = _full_region(outs[w], kinds[w], chip, c, shapes[w])
            for j, (ox, oy) in enumerate(others):
                cp = pltpu.make_async_remote_copy(
                    src_ref=mine, dst_ref=mine, send_sem=send_sems.at[6 * w + j], recv_sem=recv_sems.at[6 * w + j],
                    device_id=(ox, oy, c), device_id_type=MESH)
                cp.start()
                started.append(cp)
        for w in range(n):
            for j, (ox, oy) in enumerate(others):
                landed = _full_region(outs[w], kinds[w], 2 * ox + oy, c, shapes[w])
                pltpu.make_async_remote_copy(
                    src_ref=landed, dst_ref=landed, send_sem=send_sems.at[6 * w + j], recv_sem=recv_sems.at[6 * w + j],
                    device_id=(ox, oy, c), device_id_type=MESH).wait_recv()
                cp = pltpu.make_async_remote_copy(
                    src_ref=landed, dst_ref=landed, send_sem=send_sems.at[6 * w + 3 + j],
                    recv_sem=recv_sems.at[6 * w + 3 + j], device_id=sib, device_id_type=MESH)
                cp.start()
                started.append(cp)
        for w in range(n):
            for j, (ox, oy) in enumerate(others):
                theirs = _full_region(outs[w], kinds[w], 2 * ox + oy, 1 - c, shapes[w])
                pltpu.make_async_remote_copy(
                    src_ref=theirs, dst_ref=theirs, send_sem=send_sems.at[6 * w + 3 + j],
                    recv_sem=recv_sems.at[6 * w + 3 + j], device_id=sib, device_id_type=MESH).wait_recv()
        _drain(started)

    return pl.pallas_call(
        body, name=name, in_specs=[ANY] * n, out_specs=[ANY] * n,
        out_shape=[jax.ShapeDtypeStruct(f.shape, f.dtype) for f in fulls],
        input_output_aliases={w: w for w in range(n)},
        scratch_shapes=[pltpu.SemaphoreType.DMA((6 * n,)), pltpu.SemaphoreType.DMA((6 * n,))],
    )(*fulls)


def _region_of_size(ref, kind, shard_shape, count):
    r, cn = shard_shape
    if kind == "col":
        return ref.at[pl.ds(0, r // 2), pl.ds(0, count * cn)]
    return ref.at[pl.ds(0, count * (r // 2)), :]


def _allgather_weights_seq(fulls, kinds, shapes, name, collective_id):
    n = len(fulls)
    refs = [jax.new_ref(f, memory_space=pltpu.MemorySpace.HBM) for f in fulls]

    def body(send_sems, recv_sems):
        x, y, c = _xyc()
        chip = 2 * x + y
        sib = (x, y, 1 - c)
        others = _other_chips(x, y)
        peers = [(ox, oy, c) for ox, oy in others] + [sib]
        barrier = pltpu.get_barrier_semaphore()
        for peer in peers:
            pl.semaphore_signal(barrier, inc=1, device_id=peer, device_id_type=MESH)
        pl.semaphore_wait(barrier, len(peers))

        def copy(w, region, sem, to):
            return pltpu.make_async_remote_copy(src_ref=region, dst_ref=region, send_sem=send_sems.at[sem],
                                                recv_sem=recv_sems.at[sem], device_id=to, device_id_type=MESH)

        for w in range(n):
            mine = _full_region(refs[w], kinds[w], chip, c, shapes[w])
            for ox, oy in others:
                copy(w, mine, 2 * w, (ox, oy, c)).start()
        for w in range(n):
            three = _region_of_size(refs[w], kinds[w], shapes[w], 3)
            copy(w, three, 2 * w, sib).wait_recv()
            for ox, oy in others:
                copy(w, _full_region(refs[w], kinds[w], 2 * ox + oy, c, shapes[w]), 2 * w + 1, sib).start()
        for w in range(n):
            three = _region_of_size(refs[w], kinds[w], shapes[w], 3)
            copy(w, three, 2 * w + 1, sib).wait_recv()
            copy(w, three, 2 * w, sib).wait_send()
            copy(w, three, 2 * w + 1, sib).wait_send()

    pl.kernel(
        body, out_type=(), mesh=plsc.ScalarSubcoreMesh(axis_name="seq", num_cores=1), name=name,
        scratch_types=[pltpu.SemaphoreType.DMA((2 * n,)), pltpu.SemaphoreType.DMA((2 * n,))],
        compiler_params=pltpu.CompilerParams(collective_id=collective_id),
    )()
    return [r[...] for r in refs]


def _half_of(ref, kind, half):
    r, cn = ref.shape
    if kind == "col":
        return ref.at[pl.ds(pl.multiple_of(half * (r // 2), 16), r // 2), :]
    return ref.at[:, pl.ds(pl.multiple_of(half * (cn // 2), LANES), cn // 2)]


def _shard_of(ref, kind, chip):
    r, cn = ref.shape
    if kind == "col":
        return ref.at[:, pl.ds(pl.multiple_of(chip * (cn // N_CHIPS), LANES), cn // N_CHIPS)]
    return ref.at[pl.ds(pl.multiple_of(chip * (r // N_CHIPS), 16), r // N_CHIPS), :]


def _exchange_halves(grads, kinds, name):
    n = len(grads)

    def body(*refs):
        gs = refs[:n]
        ts = refs[n:2 * n]
        send_sems, recv_sems = refs[2 * n:]
        x, y, c = _xyc()
        cps = []
        for w in range(n):
            cp = pltpu.make_async_remote_copy(
                src_ref=_half_of(gs[w], kinds[w], 1 - c), dst_ref=ts[w], send_sem=send_sems.at[w],
                recv_sem=recv_sems.at[w], device_id=(x, y, 1 - c), device_id_type=MESH)
            cp.start()
            cps.append(cp)
        for cp in cps:
            cp.wait()

    out_shape = []
    for gr, kind in zip(grads, kinds):
        r, cn = gr.shape
        out_shape.append(jax.ShapeDtypeStruct((r // 2, cn) if kind == "col" else (r, cn // 2), gr.dtype))
    return pl.pallas_call(
        body, name=name, in_specs=[ANY] * n, out_specs=[ANY] * n, out_shape=out_shape,
        scratch_shapes=[pltpu.SemaphoreType.DMA((n,)), pltpu.SemaphoreType.DMA((n,))],
    )(*grads)


def _exchange_halves_seq(grads, kinds, name, collective_id):
    n = len(grads)

    def body(*refs):
        gs = refs[:n]
        ts = refs[n:2 * n]
        send_sems, recv_sems = refs[2 * n:]
        x, y, c = _xyc()
        sib = (x, y, 1 - c)
        barrier = pltpu.get_barrier_semaphore()
        pl.semaphore_signal(barrier, inc=1, device_id=sib, device_id_type=MESH)
        pl.semaphore_wait(barrier, 1)
        cps = []
        for w in range(n):
            cp = pltpu.make_async_remote_copy(
                src_ref=_half_of(gs[w], kinds[w], 1 - c), dst_ref=ts[w], send_sem=send_sems.at[w],
                recv_sem=recv_sems.at[w], device_id=sib, device_id_type=MESH)
            cp.start()
            cps.append(cp)
        for cp in cps:
            cp.wait()

    out_type = []
    for gr, kind in zip(grads, kinds):
        r, cn = gr.shape
        out_type.append(jax.ShapeDtypeStruct((r // 2, cn) if kind == "col" else (r, cn // 2), gr.dtype))
    return pl.kernel(
        body, out_type=out_type, mesh=plsc.ScalarSubcoreMesh(axis_name="seq", num_cores=1), name=name,
        scratch_types=[pltpu.SemaphoreType.DMA((n,)), pltpu.SemaphoreType.DMA((n,))],
        compiler_params=pltpu.CompilerParams(collective_id=collective_id),
    )(*grads)


def _scatter_partials(parts, kinds, name):
    n = len(parts)

    def body(*refs):
        ps = refs[:n]
        us = refs[n:2 * n]
        send_sems, recv_sems = refs[2 * n:]
        x, y, c = _xyc()
        others = _other_chips(x, y)
        cps = []
        for w in range(n):
            for j, (ox, oy) in enumerate(others):
                cp = pltpu.make_async_remote_copy(
                    src_ref=_shard_of(ps[w], kinds[w], 2 * ox + oy), dst_ref=us[w].at[j],
                    send_sem=send_sems.at[3 * w + j], recv_sem=recv_sems.at[3 * w + j],
                    device_id=(ox, oy, c), device_id_type=MESH)
                cp.start()
                cps.append(cp)
        for cp in cps:
            cp.wait()

    out_shape = []
    for p, kind in zip(parts, kinds):
        r, cn = p.shape
        hs = (r, cn // N_CHIPS) if kind == "col" else (r // N_CHIPS, cn)
        out_shape.append(jax.ShapeDtypeStruct((N_CHIPS - 1,) + hs, p.dtype))
    return pl.pallas_call(
        body, name=name, in_specs=[ANY] * n, out_specs=[ANY] * n, out_shape=out_shape,
        scratch_shapes=[pltpu.SemaphoreType.DMA((3 * n,)), pltpu.SemaphoreType.DMA((3 * n,))],
    )(*parts)


def _scatter_partials_seq(parts, kinds, name, collective_id):
    n = len(parts)

    def body(*refs):
        ps = refs[:n]
        us = refs[n:2 * n]
        send_sems, recv_sems = refs[2 * n:]
        x, y, c = _xyc()
        others = _other_chips(x, y)
        barrier = pltpu.get_barrier_semaphore()
        for ox, oy in others:
            pl.semaphore_signal(barrier, inc=1, device_id=(ox, oy, c), device_id_type=MESH)
        pl.semaphore_wait(barrier, len(others))
        for w in range(n):
            for j, (ox, oy) in enumerate(others):
                pltpu.make_async_remote_copy(
                    src_ref=_shard_of(ps[w], kinds[w], 2 * ox + oy), dst_ref=us[w].at[j],
                    send_sem=send_sems.at[w], recv_sem=recv_sems.at[w],
                    device_id=(ox, oy, c), device_id_type=MESH).start()
        for w in range(n):
            pltpu.make_async_remote_copy(
                src_ref=us[w], dst_ref=us[w], send_sem=send_sems.at[w], recv_sem=recv_sems.at[w],
                device_id=(x, y, c), device_id_type=MESH).wait()

    out_type = []
    for p, kind in zip(parts, kinds):
        r, cn = p.shape
        hs = (r, cn // N_CHIPS) if kind == "col" else (r // N_CHIPS, cn)
        out_type.append(jax.ShapeDtypeStruct((N_CHIPS - 1,) + hs, p.dtype))
    return pl.kernel(
        body, out_type=out_type, mesh=plsc.ScalarSubcoreMesh(axis_name="seq", num_cores=1), name=name,
        scratch_types=[pltpu.SemaphoreType.DMA((n,)), pltpu.SemaphoreType.DMA((n,))],
        compiler_params=pltpu.CompilerParams(collective_id=collective_id),
    )(*parts)


def _join_halves(halves, kinds, name):
    n = len(halves)

    def body(*refs):
        outs = refs[n:2 * n]
        send_sems, recv_sems = refs[2 * n:]
        x, y, c = _xyc()
        cps = []
        for w in range(n):
            mine = _half_of(outs[w], kinds[w], c)
            cp = pltpu.make_async_remote_copy(
                src_ref=mine, dst_ref=mine, send_sem=send_sems.at[w], recv_sem=recv_sems.at[w],
                device_id=(x, y, 1 - c), device_id_type=MESH)
            cp.start()
            cps.append(cp)
        for w in range(n):
            theirs = _half_of(outs[w], kinds[w], 1 - c)
            pltpu.make_async_remote_copy(
                src_ref=theirs, dst_ref=theirs, send_sem=send_sems.at[w], recv_sem=recv_sems.at[w],
                device_id=(x, y, 1 - c), device_id_type=MESH).wait_recv()
        _drain(cps)

    return pl.pallas_call(
        body, name=name, in_specs=[ANY] * n, out_specs=[ANY] * n,
        out_shape=[jax.ShapeDtypeStruct(h.shape, h.dtype) for h in halves],
        input_output_aliases={w: w for w in range(n)},
        scratch_shapes=[pltpu.SemaphoreType.DMA((n,)), pltpu.SemaphoreType.DMA((n,))],
    )(*halves)


def kernel(x, a_norm, a_w_in, a_conv_w, a_conv_b, a_ln_g, a_ln_b, a_w_out, kv_norm, w_kv, b_norm, b_w_in, b_w_out, rel_bias, final_norm, loss_target, m_a_norm, m_a_w_in, m_a_conv_w, m_a_conv_b, m_a_ln_g, m_a_ln_b, m_a_w_out, m_kv_norm, m_w_kv, m_b_norm, m_b_w_in, m_b_w_out, m_rel_bias, m_final_norm, v_a_norm, v_a_w_in, v_a_conv_w, v_a_conv_b, v_a_ln_g, v_a_ln_b, v_a_w_out, v_kv_norm, v_w_kv, v_b_norm, v_b_w_in, v_b_w_out, v_rel_bias, v_final_norm):
    S, D = x.shape[1], x.shape[2]
    E = a_w_out.shape[1] * N_CHIPS
    A = b_w_out.shape[1] * N_CHIPS
    H = A // HEAD_DIM
    DC = D // N_CHIPS
    xs = x.reshape(S, D)
    tgt = loss_target.reshape(S, D)
    cx, cy, cc = _xyc()
    chip = 2 * cx + cy
    c_idx = jnp.reshape(cc, (1,)).astype(jnp.int32)

    big_names = ["a_w_in", "a_w_out", "w_kv", "b_w_in", "b_w_out"]
    kinds = ["col", "row", "col", "col", "row"]
    big_w = [a_w_in[0], a_w_out[0], w_kv, b_w_in[0], b_w_out[0]]
    big_m = [m_a_w_in[0], m_a_w_out[0], m_w_kv, m_b_w_in[0], m_b_w_out[0]]
    big_v = [v_a_w_in[0], v_a_w_out[0], v_w_kv, v_b_w_in[0], v_b_w_out[0]]
    chip_idx = jnp.reshape(chip, (1,)).astype(jnp.int32)
    placed = [_cast_into_full(big_w[w], kinds[w], chip_idx, "cast_" + big_names[w]) for w in range(5)]
    shard_shapes = [w.shape for w in big_w]
    (wa_in,) = _allgather_weights_seq(placed[0:1], kinds[0:1], shard_shapes[0:1], "ag_seq_a_in", 0)
    wa_out, wkv = _allgather_weights_seq(placed[1:3], kinds[1:3], shard_shapes[1:3], "ag_seq_a_out_kv", 1)
    wb_in, wb_out = _allgather_weights_seq(placed[3:5], kinds[3:5], shard_shapes[3:5], "ag_seq_b", 2)

    def row_at(vec, q):
        return jnp.pad(vec, ((q, 7 - q), (0, 0)))

    def pack_sharded(an, cw, cb, lg, lb):
        return jnp.concatenate([row_at(an, 0), jnp.pad(cw[0], ((0, 1), (0, 0))),
                                row_at(lg, 0) + row_at(lb, 1) + row_at(cb, 2)], axis=0)

    small_w = pack_sharded(a_norm, a_conv_w, a_conv_b, a_ln_g, a_ln_b)
    gathered = _allgather_small(small_w, "ag_small_params")
    small_full = jnp.concatenate([gathered[2 * k] for k in range(N_CHIPS)], axis=1)
    g_a = small_full[0:1]
    conv_w32 = small_full[8:8 + HALO]
    ln_g = small_full[40:41]
    ln_b = small_full[41:42]
    conv_b = small_full[42:43]
    g_kv = kv_norm.reshape(1, D)
    g_b = b_norm.reshape(1, D)
    g_f = final_norm.reshape(1, D)

    rb_t = jnp.pad(rel_bias.T, ((0, 0), (0, LANES - N_BUCKETS)))
    onehots = [_onehot(dil) for _, dil in GROUPS]
    biases = [_bias_table(rb_t, onehots[g], "bias_table_%d" % g).reshape(H, BLOCK, 2 * BLOCK)
              for g in range(len(GROUPS))]

    dils = tuple(dil for _, dil in GROUPS)
    assert dils[0] == 1
    n_g = len(GROUPS)
    ((h0,),) = _rms_fwd(xs, [g_a], (1,), "rms_a")
    proj3 = _matmul(h0, wa_in, "nn", BF16, "mm_a_in", out_slab=E)
    conv = _conv_fwd(proj3, conv_w32, conv_b, "conv_fwd")
    y_a = _ln_gate_fwd(conv, proj3, ln_g, ln_b, "ln_gate_fwd")
    x1 = _matmul(y_a, wa_out, "nn", F32, "mm_a_out", res=xs)
    hks, hbs = _rms_fwd(x1, [g_kv, g_b], dils, "rms_kv_b")
    kvs = [_matmul(hks[g], wkv, "nn", BF16, "mm_kv_%d" % g, out_slab=A, b_off=2 * g * A, n_cols=2 * A)
           for g in range(n_g)]
    qs = [_matmul(hbs[g], wb_in, "nn", BF16, "mm_q_%d" % g, b_off=g * A, n_cols=A, after=kvs[-1])
          for g in range(n_g)]
    zb = _matmul(hbs[0], wb_in, "nn", BF16, "mm_zb", b_off=n_g * A, n_cols=A, after=kvs[-1])
    os_, lses = [], []
    for g, dil in enumerate(dils):
        o_g, lse_g = _attn_fwd(qs[g], kvs[g], biases[g], dil, "attn_fwd_%d" % g)
        os_.append(o_g)
        lses.append(lse_g)
    y_b, o_m, lse_d = _attn_merge(os_, lses, zb, dils, "attn_merge")
    x2 = _matmul(y_b, wb_out, "nn", F32, "mm_b_out", res=x1)
    loss_part, dx2, dx2b, gg_f = _final_head(x2, g_f, tgt, "final_head")
    loss = lax.psum(loss_part[0, 0], ("x", "y", "c"))

    dw_tiles = dict(tm=512, tn=1024, tk=4096)
    dy_b = _matmul(dx2b, wb_out, "nt", BF16, "mm_b_out_dx")
    dwb_out = _matmul(y_b, dx2b, "tn", BF16, "mm_b_out_dw", **dw_tiles)
    dos, dhs, dzb = _gate_bwd(dy_b, o_m, zb, dils, "gate_bwd")
    dbs, cots = [], []
    dwb_in = dwkv = None
    for g, dil in enumerate(dils):
        dq, dkv, db = _attn_bwd(qs[g], kvs[g], dos[g], lse_d[g], dhs[g], biases[g], dil, "attn_bwd_%d" % g)
        dbs.append(db.reshape(H, BLOCK * 2 * BLOCK))
        dwb_in = _matmul(hbs[g], dq, "tn", BF16, "mm_q_dw_%d" % g, out_off=g * A, out_cols=(n_g + 1) * A,
                         out_alias=dwb_in, **dw_tiles)
        dwkv = _matmul(hks[g], dkv, "tn", BF16, "mm_kv_dw_%d" % g, b_slab=True, out_off=2 * g * A,
                       out_cols=2 * n_g * A, out_alias=dwkv, **dw_tiles)
        cots.append((_matmul(dkv, wkv, "nt", BF16, "mm_kv_dx_%d" % g, a_slab=True, b_off=2 * g * A), 0, dil))
        cots.append((_matmul(dq, wb_in, "nt", BF16, "mm_q_dx_%d" % g, b_off=g * A), 1, dil))
    dwb_in = _matmul(hbs[0], dzb, "tn", BF16, "mm_zb_dw", out_off=n_g * A, out_cols=(n_g + 1) * A,
                     out_alias=dwb_in, **dw_tiles)
    cots.append((_matmul(dzb, wb_in, "nt", BF16, "mm_zb_dx", b_off=n_g * A), 1, 1))
    chip_c = jnp.stack([chip, cc]).astype(jnp.int32)

    def scatter_group(idx, grads, tag, ids, behind):
        ks = [kinds[w] for w in idx]
        theirs = _exchange_halves_seq(grads, ks, "rs_exchange_seq_" + tag, ids[0])
        parts = [_add_half(grads[q], theirs[q], c_idx, ks[q], "rs_add_half_%d" % w, after=behind)
                 for q, w in enumerate(idx)]
        return parts, _scatter_partials_seq(parts, ks, "rs_seq_" + tag, ids[1])

    def reduce_group(idx, parts, slots, tag):
        ks = [kinds[w] for w in idx]
        halves = [_sum_into_shard(parts[q], slots[q], chip_c, ks[q], "rs_sum_chips_%d" % w)
                  for q, w in enumerate(idx)]
        return _join_halves(halves, ks, "rs_join_" + tag)

    parts_b, slots_b = scatter_group([2, 3, 4], [dwkv, dwb_in, dwb_out], "b", (5, 3), [ct[0] for ct in cots])
    g_rel_t = _bias_grad(dbs, onehots, "bias_grad")
    dx1, dx1b, gg_kvb = _rms_bwd(x1, cots, [g_kv, g_b], dx2, "rms_kv_b_bwd", after=parts_b)
    dy_a = _matmul(dx1b, wa_out, "nt", BF16, "mm_a_out_dx")
    dwa_out = _matmul(y_a, dx1b, "tn", BF16, "mm_a_out_dw", **dw_tiles)
    dconv, dproj3, gg_ln = _ln_gate_bwd(conv, proj3, dy_a, ln_g, ln_b, "ln_gate_bwd")
    dproj3, g_conv_w = _conv_bwd(proj3, dconv, conv_w32, dproj3, "conv_bwd")
    dwa_in = _matmul(h0, dproj3, "tn", BF16, "mm_a_in_dw", b_slab=True, **dw_tiles)
    dh0 = _matmul(dproj3, wa_in, "nt", BF16, "mm_a_in_dx", a_slab=True, after=dwa_in, tn=512)
    parts_a, slots_a = scatter_group([0, 1], [dwa_in, dwa_out], "a", (6, 4), [dh0])
    grad_x, _, gg_a = _rms_bwd(xs, [(dh0, 0, 1)], [g_a], dx1, "rms_a_bwd", after=parts_a)

    big_g = [None] * 5
    big_g[2:5] = reduce_group([2, 3, 4], parts_b, slots_b, "b")
    big_g[0:2] = reduce_group([0, 1], parts_a, slots_a, "a")

    def rel_rows(rb):
        return jnp.pad(rb.reshape(1, N_BUCKETS * H), ((0, 7), (0, D - N_BUCKETS * H)))

    small_g = jnp.concatenate([gg_a, g_conv_w, gg_ln, gg_kvb, gg_f, rel_rows(g_rel_t[:, :N_BUCKETS].T)], axis=0)
    small_sum = _sum_leading(_allgather_small(small_g, "ag_small_grads", after=[slots_a[0], slots_b[0]]), F32,
                             "sum_small_grads", tr=72)
    g_sharded = lax.dynamic_slice(small_sum, (0, chip * DC), (48, DC))
    g_repl = small_sum[48:72]

    outs_g, outs_d, outs_m, outs_v = {}, {}, {}, {}
    for w, nm in enumerate(big_names):
        d_, m_, v_ = _adamw(big_w[w], big_g[w], big_m[w], big_v[w], "adamw_" + nm)
        outs_g[nm], outs_d[nm], outs_m[nm], outs_v[nm] = big_g[w], d_, m_, v_
    sm_m = pack_sharded(m_a_norm, m_a_conv_w, m_a_conv_b, m_a_ln_g, m_a_ln_b)
    sm_v = pack_sharded(v_a_norm, v_a_conv_w, v_a_conv_b, v_a_ln_g, v_a_ln_b)
    sd, smm, svv = _adamw(small_w, g_sharded, sm_m, sm_v, "adamw_small_sharded")

    def unpack_sharded(p):
        return {"a_norm": p[0:1], "a_conv_w": p[8:8 + CONV_TAPS].reshape(1, CONV_TAPS, DC), "a_ln_g": p[40:41],
                "a_ln_b": p[41:42], "a_conv_b": p[42:43]}

    for src, dst in ((g_sharded, outs_g), (sd, outs_d), (smm, outs_m), (svv, outs_v)):
        dst.update(unpack_sharded(src))

    def pack_repl(kn, bn, fn, rb):
        return jnp.concatenate([row_at(kn.reshape(1, D), 0) + row_at(bn.reshape(1, D), 1),
                                row_at(fn.reshape(1, D), 0), rel_rows(rb)], axis=0)

    rp_w = pack_repl(kv_norm, b_norm, final_norm, rel_bias)
    rp_m = pack_repl(m_kv_norm, m_b_norm, m_final_norm, m_rel_bias)
    rp_v = pack_repl(v_kv_norm, v_b_norm, v_final_norm, v_rel_bias)
    rd, rmm, rvv = _adamw(rp_w, g_repl, rp_m, rp_v, "adamw_small_replicated")

    def unpack_repl(p):
        return {"kv_norm": p[0], "b_norm": p[1:2], "final_norm": p[8],
                "rel_bias": p[16, :N_BUCKETS * H].reshape(N_BUCKETS, H)}

    for src, dst in ((g_repl, outs_g), (rd, outs_d), (rmm, outs_m), (rvv, outs_v)):
        dst.update(unpack_repl(src))

    order = ["a_norm", "a_w_in", "a_conv_w", "a_conv_b", "a_ln_g", "a_ln_b", "a_w_out", "kv_norm", "w_kv",
             "b_norm", "b_w_in", "b_w_out", "rel_bias", "final_norm"]
    lead = {"a_w_in", "a_w_out", "b_w_in", "b_w_out"}

    def shaped(nm, val):
        return val[None] if nm in lead else val

    result = [loss, grad_x.reshape(1, S, D)]
    for table in (outs_g, outs_d, outs_m, outs_v):
        result.extend(shaped(nm, table[nm]) for nm in order)
    return tuple(result)
```

```python
import functools

import numpy as np
import jax
import jax.numpy as jnp
from jax import lax
from jax.experimental import pallas as pl
from jax.experimental.pallas import tpu as pltpu
from jax.experimental.pallas import tpu_sc as plsc

F32 = jnp.float32
BF16 = jnp.bfloat16
MESH = pl.DeviceIdType.MESH
ANY = pl.BlockSpec(memory_space=pl.ANY)
VMEM_SPEC = pl.BlockSpec(memory_space=pltpu.VMEM)

EPS = 1e-6
HEAD_DIM = 128
BLOCK = 128
GROUPS = ((128, 1), (512, 4), (2048, 16))
SCALE = HEAD_DIM ** -0.5
CONV_TAPS = 31
HALO = 32
N_BUCKETS = 32
MAX_EXACT = 16
MAX_DISTANCE = 2048
NEG = -1e30
PRODUCTS_AHEAD = 2
SCORES_AHEAD = 4
N_CHIPS = 4
N_DEV = 8
LANES = 128
VMEM_LIMIT = 56 * 1024 * 1024

ADAM_LR = 0.001
ADAM_B1 = 0.9
ADAM_B2 = 0.999
ADAM_EPS = 1e-08
ADAM_WD = 0.01
ADAM_STEP = 10


def _tile(n, pref, mult=LANES):
    t = (min(pref, n) // mult) * mult
    while t >= mult:
        if n % t == 0:
            return t
        t -= mult
    return n


def _params(*sem):
    return pltpu.CompilerParams(dimension_semantics=sem, vmem_limit_bytes=VMEM_LIMIT)


def _sigmoid(v):
    return 1.0 / (1.0 + jnp.exp(-v))


def _dot(a, b, dims):
    return lax.dot_general(a, b, (dims, ((), ())), preferred_element_type=F32)


NN = ((1,), (0,))
NT = ((1,), (1,))
TN = ((0,), (0,))


def _as_list(after):
    if after is None:
        return []
    return list(after) if isinstance(after, (list, tuple)) else [after]


def _stack_rows(rows, total):
    width = rows[0].shape[1]
    rid = lax.broadcasted_iota(jnp.int32, (total, width), 0)
    out = jnp.zeros((total, width), F32)
    for q, row in enumerate(rows):
        out = jnp.where(rid == q, jnp.broadcast_to(row, (total, width)), out)
    return out


def _lane_col(arr, h, lane):
    return jnp.sum(jnp.where(lane == h, arr, 0.0), axis=-1, keepdims=True)


def _matmul(a, b, mode, out_dtype, name, res=None, a_slab=False, b_slab=False, out_slab=0,
            b_off=0, n_cols=None, out_off=0, out_cols=None, out_alias=None, after=None,
            tm=1024, tn=1024, tk=2048):
    if a_slab:
        na, M, W = a.shape
        K = na * W
    elif mode == "tn":
        K, M = a.shape
    else:
        M, K = a.shape
    if b_slab:
        nbs, _, Wb = b.shape
        N = nbs * Wb
    elif mode == "nt":
        N = b.shape[0]
    else:
        N = n_cols if n_cols else b.shape[1]
    tm = _tile(M, tm)
    tn = _tile(Wb if b_slab else (out_slab if out_slab else N), tn)
    tk = _tile(W if a_slab else K, tk)
    all_slabs = a_slab and mode == "nt" and tk == W
    if all_slabs:
        tk = K
    nk = K // tk
    grid = (M // tm, N // tn, nk)
    bo = b_off // (tk if mode == "nt" else tn)
    oo = out_off // tn

    if all_slabs:
        a_spec = pl.BlockSpec((na, tm, W), lambda i, j, k: (0, i, 0))
    elif a_slab:
        per = W // tk
        a_spec = pl.BlockSpec((None, tm, tk), lambda i, j, k: (k // per, i, k % per))
    elif mode == "tn":
        a_spec = pl.BlockSpec((tk, tm), lambda i, j, k: (k, i))
    else:
        a_spec = pl.BlockSpec((tm, tk), lambda i, j, k: (i, k))
    if b_slab:
        perb = Wb // tn
        b_spec = pl.BlockSpec((None, tk, tn), lambda i, j, k: (j // perb, k, j % perb))
    elif mode == "nt":
        b_spec = pl.BlockSpec((tn, tk), lambda i, j, k: (j, k + bo))
    else:
        b_spec = pl.BlockSpec((tk, tn), lambda i, j, k: (k, j + bo))
    if out_slab:
        pero = out_slab // tn
        o_spec = pl.BlockSpec((None, tm, tn), lambda i, j, k: (j // pero, i, j % pero))
        out_shape = jax.ShapeDtypeStruct((N // out_slab, M, out_slab), out_dtype)
    else:
        o_spec = pl.BlockSpec((tm, tn), lambda i, j, k: (i, j + oo))
        out_shape = jax.ShapeDtypeStruct((M, out_cols if out_cols else N), out_dtype)
    in_specs = [a_spec, b_spec]
    operands = [a, b]
    if res is not None:
        in_specs.append(pl.BlockSpec((tm, tn), lambda i, j, k: (i, j)))
        operands.append(res)
    aliases = {}
    if out_alias is not None:
        aliases[len(operands)] = 0
        in_specs.append(ANY)
        operands.append(out_alias)
    for arr in _as_list(after):
        in_specs.append(ANY)
        operands.append(arr)
    dims = {"nn": NN, "nt": NT, "tn": TN}[mode]
    has_res = res is not None
    n_in = len(operands)

    def body(*refs):
        a_ref, b_ref = refs[0], refs[1]
        r_ref = refs[2] if has_res else None
        o_ref = refs[n_in]
        if all_slabs:
            prod = _dot(a_ref[0], b_ref[:, 0:W], dims)
            for q in range(1, na):
                prod = prod + _dot(a_ref[q], b_ref[:, q * W:(q + 1) * W], dims)
        else:
            prod = _dot(a_ref[...], b_ref[...], dims)

        def finish(val):
            if has_res:
                val = val + r_ref[...]
            o_ref[...] = val.astype(out_dtype)

        if nk == 1:
            finish(prod)
        else:
            acc_ref = refs[n_in + 1]
            k = pl.program_id(2)

            @pl.when(k == 0)
            def _():
                acc_ref[...] = prod

            @pl.when(k > 0)
            def _():
                acc_ref[...] += prod

            @pl.when(k == nk - 1)
            def _():
                finish(acc_ref[...])

    scratch = [pltpu.VMEM((tm, tn), F32)] if nk > 1 else []
    return pl.pallas_call(
        body, name=name, grid=grid, in_specs=in_specs, out_specs=o_spec, out_shape=out_shape,
        scratch_shapes=scratch, input_output_aliases=aliases,
        compiler_params=_params("parallel", "parallel", "arbitrary"),
    )(*operands)


def _group_spec(d, ts, width):
    if d == 1:
        return pl.BlockSpec((ts, width), lambda i: (i, 0))
    return pl.BlockSpec((d, ts // d, width), lambda i: (0, i, 0))


def _group_shape(d, S, width, dtype):
    return jax.ShapeDtypeStruct((S, width) if d == 1 else (d, S // d, width), dtype)


def _chunk_buf(ts, width):
    return pltpu.VMEM((width // LANES, ts, LANES), F32)


def _fill_chunks(buf, val):
    for c in range(buf.shape[0]):
        buf[c] = val[:, c * LANES:(c + 1) * LANES]


def _read_chunks(buf):
    return jnp.concatenate([buf[c] for c in range(buf.shape[0])], axis=1)


def _emit_group_order(o_ref, buf, d, dtype):
    n = buf.shape[1] // d
    for r in range(d):
        for c in range(buf.shape[0]):
            o_ref[r, :, c * LANES:(c + 1) * LANES] = buf[c, pl.ds(r, n, stride=d), :].astype(dtype)


def _store_token_order(buf, i_ref, d):
    n = buf.shape[1] // d
    for r in range(d):
        for c in range(buf.shape[0]):
            buf[c, pl.ds(r, n, stride=d), :] = i_ref[r, :, c * LANES:(c + 1) * LANES].astype(F32)


def _rms_fwd(x, gains, dils, name, ts=256):
    S, D = x.shape
    ts = _tile(S, ts, 16 * max(dils))
    n = len(gains)
    nd = len(dils)

    def body(*refs):
        buf = refs[1 + n + n * nd]
        xv = refs[0][...]
        nrm = xv * lax.rsqrt(jnp.mean(xv * xv, axis=-1, keepdims=True) + EPS)
        for q in range(n):
            val = nrm * refs[1 + q][...]
            if max(dils) > 1:
                _fill_chunks(buf, val)
            for e, d in enumerate(dils):
                if d == 1:
                    refs[1 + n + q * nd + e][...] = val.astype(BF16)
                else:
                    _emit_group_order(refs[1 + n + q * nd + e], buf, d, BF16)

    row = pl.BlockSpec((ts, D), lambda i: (i, 0))
    vec = pl.BlockSpec((1, D), lambda i: (0, 0))
    outs = pl.pallas_call(
        body, name=name, grid=(S // ts,), in_specs=[row] + [vec] * n,
        out_specs=[_group_spec(d, ts, D) for _ in range(n) for d in dils],
        out_shape=[_group_shape(d, S, D, BF16) for _ in range(n) for d in dils],
        scratch_shapes=[_chunk_buf(ts, D)],
        compiler_params=_params("parallel"),
    )(x, *gains)
    return [[outs[q * nd + e].reshape(S, D) for e in range(nd)] for q in range(n)]


def _rms_bwd(x, cots, gains, dres, name, after=None, ts=256):
    S, D = x.shape
    ts = _tile(S, ts, 16 * max(d for _, _, d in cots))
    n = len(cots)
    ng = len(gains)
    extra = _as_list(after)
    n_in = 2 + n + ng + len(extra)

    def body(*refs):
        x_ref = refs[0]
        dh_refs = refs[1:1 + n]
        g_refs = refs[1 + n:1 + n + ng]
        dres_ref = refs[1 + n + ng]
        dx_ref, dxb_ref, gg_ref, buf = refs[n_in:n_in + 4]
        i = pl.program_id(0)
        xv = x_ref[...]
        r = lax.rsqrt(jnp.mean(xv * xv, axis=-1, keepdims=True) + EPS)
        nrm = xv * r
        dn = jnp.zeros_like(xv)
        rows = [jnp.zeros((1, D), F32) for _ in range(ng)]
        for q, (_, gi, d) in enumerate(cots):
            if d == 1:
                dh = dh_refs[q][...].astype(F32)
            else:
                _store_token_order(buf, dh_refs[q], d)
                dh = _read_chunks(buf)
            dn = dn + dh * g_refs[gi][...]
            rows[gi] = rows[gi] + jnp.sum(dh * nrm, axis=0, keepdims=True)
        dx = dres_ref[...] + r * (dn - nrm * jnp.mean(dn * nrm, axis=-1, keepdims=True))
        dx_ref[...] = dx
        dxb_ref[...] = dx.astype(BF16)
        upd = _stack_rows(rows, 8)

        @pl.when(i == 0)
        def _():
            gg_ref[...] = upd

        @pl.when(i > 0)
        def _():
            gg_ref[...] += upd

    row = pl.BlockSpec((ts, D), lambda i: (i, 0))
    vec = pl.BlockSpec((1, D), lambda i: (0, 0))
    acc = pl.BlockSpec((8, D), lambda i: (0, 0))
    return pl.pallas_call(
        body, name=name, grid=(S // ts,),
        in_specs=[row] + [_group_spec(d, ts, D) for _, _, d in cots] + [vec] * ng + [row] + [ANY] * len(extra),
        out_specs=[row, row, acc],
        out_shape=[jax.ShapeDtypeStruct((S, D), F32), jax.ShapeDtypeStruct((S, D), BF16),
                   jax.ShapeDtypeStruct((8, D), F32)],
        scratch_shapes=[_chunk_buf(ts, D)],
        compiler_params=_params("arbitrary"),
    )(x, *[a if d == 1 else a.reshape(d, S // d, D) for a, _, d in cots], *gains, dres, *extra)


def _final_head(x2, gain, target, name, ts=256):
    S, D = x2.shape
    ts = _tile(S, ts, 16)

    def body(x_ref, g_ref, t_ref, loss_ref, dx_ref, dxb_ref, gg_ref):
        i = pl.program_id(0)
        xv = x_ref[...]
        g = g_ref[...]
        r = lax.rsqrt(jnp.mean(xv * xv, axis=-1, keepdims=True) + EPS)
        nrm = xv * r
        err = nrm * g - t_ref[...]
        part = 0.5 * jnp.sum(jnp.mean(err * err, axis=-1, keepdims=True), axis=0, keepdims=True)
        dout = err * (1.0 / D)
        dn = dout * g
        dx = r * (dn - nrm * jnp.mean(dn * nrm, axis=-1, keepdims=True))
        dx_ref[...] = dx
        dxb_ref[...] = dx.astype(BF16)
        upd = _stack_rows([jnp.sum(dout * nrm, axis=0, keepdims=True)], 8)
        lpart = jnp.broadcast_to(part, (1, LANES))

        @pl.when(i == 0)
        def _():
            gg_ref[...] = upd
            loss_ref[...] = lpart

        @pl.when(i > 0)
        def _():
            gg_ref[...] += upd
            loss_ref[...] += lpart

    row = pl.BlockSpec((ts, D), lambda i: (i, 0))
    vec = pl.BlockSpec((1, D), lambda i: (0, 0))
    return pl.pallas_call(
        body, name=name, grid=(S // ts,), in_specs=[row, vec, row],
        out_specs=[pl.BlockSpec((1, LANES), lambda i: (0, 0)), row, row, pl.BlockSpec((8, D), lambda i: (0, 0))],
        out_shape=[jax.ShapeDtypeStruct((1, LANES), F32), jax.ShapeDtypeStruct((S, D), F32),
                   jax.ShapeDtypeStruct((S, D), BF16), jax.ShapeDtypeStruct((8, D), F32)],
        compiler_params=_params("arbitrary"),
    )(x2, gain, target)


CONV_ROWS = 64


SUBLANES = 8


def _shifted_buf(ts, cw):
    return pltpu.VMEM((SUBLANES - 1, ts + HALO - SUBLANES, cw), F32)


def _fill_shifted(shifted, buf):
    rows = shifted.shape[1]
    for s in range(1, SUBLANES):
        shifted[s - 1] = buf[s:s + rows, :]


def _window(buf, shifted, off, rows):
    s = off % SUBLANES
    base = off - s
    if s == 0:
        return buf[base:base + rows, :]
    return shifted[s - 1, base:base + rows, :]


def _conv_fwd(proj3, conv_w32, conv_b, name, ts=256, cw=256):
    _, S, E = proj3.shape
    ts = _tile(S, ts, HALO)
    cw = _tile(E, cw)
    per = ts // HALO
    rc = min(CONV_ROWS, ts)

    def body(a_ref, b_ref, ap_ref, bp_ref, w_ref, cb_ref, c_ref, ubuf, shifted):
        i = pl.program_id(0)
        up = ap_ref[...].astype(F32) * _sigmoid(bp_ref[...].astype(F32))
        ubuf[0:HALO, :] = jnp.where(i > 0, up, 0.0)
        ubuf[HALO:HALO + ts, :] = a_ref[...].astype(F32) * _sigmoid(b_ref[...].astype(F32))
        _fill_shifted(shifted, ubuf)
        for r0 in range(0, ts, rc):
            acc = jnp.broadcast_to(cb_ref[...], (rc, cw))
            for k in range(CONV_TAPS):
                off = r0 + HALO - (CONV_TAPS - 1) + k
                acc = acc + _window(ubuf, shifted, off, rc) * w_ref[k:k + 1, :]
            c_ref[r0:r0 + rc, :] = acc

    return pl.pallas_call(
        body, name=name, grid=(S // ts, E // cw),
        in_specs=[
            pl.BlockSpec((None, ts, cw), lambda i, j: (0, i, j)),
            pl.BlockSpec((None, ts, cw), lambda i, j: (1, i, j)),
            pl.BlockSpec((None, HALO, cw), lambda i, j: (0, jnp.maximum(i * per - 1, 0), j)),
            pl.BlockSpec((None, HALO, cw), lambda i, j: (1, jnp.maximum(i * per - 1, 0), j)),
            pl.BlockSpec((HALO, cw), lambda i, j: (0, j)),
            pl.BlockSpec((1, cw), lambda i, j: (0, j)),
        ],
        out_specs=pl.BlockSpec((ts, cw), lambda i, j: (i, j)),
        out_shape=jax.ShapeDtypeStruct((S, E), F32),
        scratch_shapes=[pltpu.VMEM((HALO + ts, cw), F32), _shifted_buf(ts, cw)],
        compiler_params=_params("parallel", "parallel"),
    )(proj3, proj3, proj3, proj3, conv_w32, conv_b)


def _ln_gate_fwd(c, proj3, ln_g, ln_b, name, ts=256):
    S, E = c.shape
    ts = _tile(S, ts, 16)

    def body(c_ref, z_ref, g_ref, b_ref, y_ref):
        cv = c_ref[...]
        mu = jnp.mean(cv, axis=-1, keepdims=True)
        d = cv - mu
        var = jnp.mean(d * d, axis=-1, keepdims=True)
        cn = d * lax.rsqrt(var + EPS) * g_ref[...] + b_ref[...]
        z = z_ref[...].astype(F32)
        y_ref[...] = ((cn * _sigmoid(cn)).astype(F32) * (z * _sigmoid(z))).astype(BF16)

    row = pl.BlockSpec((ts, E), lambda i: (i, 0))
    vec = pl.BlockSpec((1, E), lambda i: (0, 0))
    return pl.pallas_call(
        body, name=name, grid=(S // ts,),
        in_specs=[row, pl.BlockSpec((None, ts, E), lambda i: (2, i, 0)), vec, vec],
        out_specs=row, out_shape=jax.ShapeDtypeStruct((S, E), BF16),
        compiler_params=_params("parallel"),
    )(c, proj3, ln_g, ln_b)


def _ln_gate_bwd(c, proj3, dy, ln_g, ln_b, name, ts=256):
    S, E = c.shape
    ts = _tile(S, ts, 16)

    def body(c_ref, z_ref, dy_ref, g_ref, b_ref, dc_ref, dz_ref, acc_ref):
        i = pl.program_id(0)
        cv = c_ref[...]
        g = g_ref[...]
        mu = jnp.mean(cv, axis=-1, keepdims=True)
        d = cv - mu
        var = jnp.mean(d * d, axis=-1, keepdims=True)
        rstd = lax.rsqrt(var + EPS)
        chat = d * rstd
        cn = chat * g + b_ref[...]
        z = z_ref[...].astype(F32)
        dyv = dy_ref[...].astype(F32)
        sc = _sigmoid(cn)
        sz = _sigmoid(z)
        dcn = dyv * (z * sz) * (sc * (1.0 + cn * (1.0 - sc)))
        dz_ref[...] = (dyv * (cn * sc) * (sz * (1.0 + z * (1.0 - sz)))).astype(BF16)
        dchat = dcn * g
        dcv = rstd * (dchat - jnp.mean(dchat, axis=-1, keepdims=True)
                      - chat * jnp.mean(dchat * chat, axis=-1, keepdims=True))
        dc_ref[...] = dcv
        upd = _stack_rows([jnp.sum(dcn * chat, axis=0, keepdims=True),
                           jnp.sum(dcn, axis=0, keepdims=True),
                           jnp.sum(dcv, axis=0, keepdims=True)], 8)

        @pl.when(i == 0)
        def _():
            acc_ref[...] = upd

        @pl.when(i > 0)
        def _():
            acc_ref[...] += upd

    row = pl.BlockSpec((ts, E), lambda i: (i, 0))
    vec = pl.BlockSpec((1, E), lambda i: (0, 0))
    return pl.pallas_call(
        body, name=name, grid=(S // ts,),
        in_specs=[row, pl.BlockSpec((None, ts, E), lambda i: (2, i, 0)), row, vec, vec],
        out_specs=[row, pl.BlockSpec((None, ts, E), lambda i: (2, i, 0)), pl.BlockSpec((8, E), lambda i: (0, 0))],
        out_shape=[jax.ShapeDtypeStruct((S, E), F32), jax.ShapeDtypeStruct((3, S, E), BF16),
                   jax.ShapeDtypeStruct((8, E), F32)],
        compiler_params=_params("arbitrary"),
    )(c, proj3, dy, ln_g, ln_b)


def _conv_bwd(proj3, dc, conv_w32, dproj3, name, ts=256, cw=256):
    _, S, E = proj3.shape
    ts = _tile(S, ts, HALO)
    cw = _tile(E, cw)
    per = ts // HALO
    n_i = S // ts
    last_halo = S // HALO - 1
    rc = min(CONV_ROWS, ts)

    def body(a_ref, b_ref, dc_ref, dcn_ref, w_ref, dp_in, dab_ref, dw_ref, dcbuf, ubuf, dwacc, shifted):
        del dp_in
        i = pl.program_id(1)
        dcbuf[0:ts, :] = dc_ref[...]
        dcbuf[ts:ts + HALO, :] = jnp.where(i < n_i - 1, dcn_ref[...], 0.0)
        _fill_shifted(shifted, dcbuf)
        av = a_ref[...].astype(F32)
        sb = _sigmoid(b_ref[...].astype(F32))
        ubuf[...] = av * sb

        @pl.when(i == 0)
        def _():
            dwacc[...] = jnp.zeros_like(dwacc)

        for r0 in range(0, ts, rc):
            uv = ubuf[r0:r0 + rc, :]
            du = jnp.zeros((rc, cw), F32)
            for d in range(CONV_TAPS):
                k = CONV_TAPS - 1 - d
                win = _window(dcbuf, shifted, r0 + d, rc)
                du = du + win * w_ref[k:k + 1, :]
                dwacc[k:k + 1, :] += jnp.sum(uv * win, axis=0, keepdims=True)
            a_c = a_ref[r0:r0 + rc, :].astype(F32)
            s_c = _sigmoid(b_ref[r0:r0 + rc, :].astype(F32))
            dab_ref[0, r0:r0 + rc, :] = (du * s_c).astype(BF16)
            dab_ref[1, r0:r0 + rc, :] = (du * a_c * s_c * (1.0 - s_c)).astype(BF16)

        @pl.when(i == n_i - 1)
        def _():
            dw_ref[...] = dwacc[...]

    return pl.pallas_call(
        body, name=name, grid=(E // cw, n_i),
        in_specs=[
            pl.BlockSpec((None, ts, cw), lambda j, i: (0, i, j)),
            pl.BlockSpec((None, ts, cw), lambda j, i: (1, i, j)),
            pl.BlockSpec((ts, cw), lambda j, i: (i, j)),
            pl.BlockSpec((HALO, cw), lambda j, i: (jnp.minimum((i + 1) * per, last_halo), j)),
            pl.BlockSpec((HALO, cw), lambda j, i: (0, j)),
            ANY,
        ],
        out_specs=[pl.BlockSpec((2, ts, cw), lambda j, i: (0, i, j)),
                   pl.BlockSpec((HALO, cw), lambda j, i: (0, j))],
        out_shape=[jax.ShapeDtypeStruct((3, S, E), BF16), jax.ShapeDtypeStruct((HALO, E), F32)],
        scratch_shapes=[pltpu.VMEM((ts + HALO, cw), F32), pltpu.VMEM((ts, cw), F32), pltpu.VMEM((HALO, cw), F32),
                        _shifted_buf(ts, cw)],
        input_output_aliases={5: 0},
        compiler_params=_params("parallel", "arbitrary"),
    )(proj3, proj3, dc, dc, conv_w32, dproj3)


def _bucket_table(dil):
    delta = (np.arange(BLOCK)[:, None] + BLOCK) - np.arange(2 * BLOCK)[None, :]
    dist = np.clip(delta, 0, None) * dil
    large = MAX_EXACT + (np.log(np.maximum(dist, 1).astype(np.float32) / MAX_EXACT)
                         / np.log(MAX_DISTANCE / MAX_EXACT) * (N_BUCKETS - MAX_EXACT)).astype(np.int32)
    large = np.minimum(large, N_BUCKETS - 1)
    return np.where(dist < MAX_EXACT, dist, large).astype(np.int32).reshape(-1)


def _onehot(dil):
    tbl = jnp.asarray(_bucket_table(dil))
    return (tbl[None, :] == jnp.arange(LANES, dtype=jnp.int32)[:, None]).astype(BF16)


def _split3(v):
    hi = v.astype(BF16)
    r1 = v - hi.astype(F32)
    mid = r1.astype(BF16)
    lo = (r1 - mid.astype(F32)).astype(BF16)
    return hi, mid, lo


def _bias_table(rb_t, onehot, name):
    H = rb_t.shape[0]
    N = onehot.shape[1]

    def body(r_ref, oh_ref, o_ref):
        oh = oh_ref[...]
        hi, mid, lo = _split3(r_ref[...])
        o_ref[...] = (_dot(lo, oh, NN) + _dot(mid, oh, NN)) + _dot(hi, oh, NN)

    return pl.pallas_call(
        body, name=name, in_specs=[VMEM_SPEC, VMEM_SPEC], out_specs=VMEM_SPEC,
        out_shape=jax.ShapeDtypeStruct((H, N), F32),
        compiler_params=pltpu.CompilerParams(vmem_limit_bytes=VMEM_LIMIT),
    )(rb_t, onehot)


def _bias_grad(dbs, onehots, name):
    H = dbs[0].shape[0]
    n = len(dbs)

    def body(*refs):
        acc = jnp.zeros((H, LANES), F32)
        for q in range(n):
            oh = refs[n + q][...]
            hi, mid, lo = _split3(refs[q][...])
            acc = acc + ((_dot(lo, oh, NT) + _dot(mid, oh, NT)) + _dot(hi, oh, NT))
        refs[2 * n][...] = acc

    return pl.pallas_call(
        body, name=name, in_specs=[VMEM_SPEC] * (2 * n), out_specs=VMEM_SPEC,
        out_shape=jax.ShapeDtypeStruct((H, LANES), F32),
        compiler_params=pltpu.CompilerParams(vmem_limit_bytes=VMEM_LIMIT),
    )(*dbs, *onehots)


def _pair_mask(has_prev):
    qi = lax.broadcasted_iota(jnp.int32, (BLOCK, 2 * BLOCK), 0)
    ki = lax.broadcasted_iota(jnp.int32, (BLOCK, 2 * BLOCK), 1)
    prev = jnp.logical_and(jnp.logical_and(ki < BLOCK, ki >= qi), has_prev)
    return jnp.logical_or(prev, jnp.logical_and(ki >= BLOCK, ki - BLOCK <= qi))


def _attn_fwd(q, kv, bias, dil, name):
    S, A = q.shape
    H = A // HEAD_DIM
    L = S // dil
    nb = L // BLOCK
    qv = q.reshape(dil, L, A)
    kvv = kv.reshape(2, dil, L, A)

    def body(q_ref, kp_ref, kc_ref, vp_ref, vc_ref, b_ref, o_ref, lse_ref):
        i = pl.program_id(1)
        qi = lax.broadcasted_iota(jnp.int32, (BLOCK, BLOCK), 0)
        ki = lax.broadcasted_iota(jnp.int32, (BLOCK, BLOCK), 1)
        mask = _pair_mask(i > 0)
        lane = lax.broadcasted_iota(jnp.int32, (BLOCK, LANES), 1)
        lse_acc = jnp.zeros((BLOCK, LANES), F32)

        def scores(h):
            sl = slice(h * HEAD_DIM, (h + 1) * HEAD_DIM)
            return _dot(q_ref[:, sl], jnp.concatenate([kp_ref[:, sl], kc_ref[:, sl]], axis=0), NT)

        ahead = [scores(h) for h in range(min(SCORES_AHEAD, H))]
        for h in range(H):
            sl = slice(h * HEAD_DIM, (h + 1) * HEAD_DIM)
            raw = ahead.pop(0)
            if h + SCORES_AHEAD < H:
                ahead.append(scores(h + SCORES_AHEAD))
            s = jnp.where(mask, raw * SCALE + b_ref[h], NEG)
            m = jnp.max(s, axis=-1, keepdims=True)
            p = jnp.exp(s - m)
            den = jnp.sum(p, axis=-1, keepdims=True)
            acc = _dot(p.astype(BF16), jnp.concatenate([vp_ref[:, sl], vc_ref[:, sl]], axis=0), NN)
            o_ref[:, sl] = acc / den
            lse_acc = jnp.where(lane == h, m + jnp.log(den), lse_acc)
        lse_ref[...] = lse_acc

    def blk(slab, prev):
        if prev:
            return pl.BlockSpec((None, None, BLOCK, A), lambda r, i: (slab, r, jnp.maximum(i - 1, 0), 0))
        return pl.BlockSpec((None, None, BLOCK, A), lambda r, i: (slab, r, i, 0))

    o, lse = pl.pallas_call(
        body, name=name, grid=(dil, nb),
        in_specs=[pl.BlockSpec((None, BLOCK, A), lambda r, i: (r, i, 0)),
                  blk(0, True), blk(0, False), blk(1, True), blk(1, False),
                  pl.BlockSpec((H, BLOCK, 2 * BLOCK), lambda r, i: (0, 0, 0))],
        out_specs=[pl.BlockSpec((None, BLOCK, A), lambda r, i: (r, i, 0)),
                   pl.BlockSpec((None, BLOCK, LANES), lambda r, i: (r, i, 0))],
        out_shape=[jax.ShapeDtypeStruct((dil, L, A), F32), jax.ShapeDtypeStruct((dil, L, LANES), F32)],
        compiler_params=_params("parallel", "parallel"),
    )(qv, kvv, kvv, kvv, kvv, bias)
    return o.reshape(S, A), lse.reshape(S, LANES)


def _attn_merge(os_, lses, z, dils, name, ts=256):
    S, A = z.shape
    H = A // HEAD_DIM
    ts = _tile(S, ts, 16 * max(dils))
    n = len(os_)

    def body(*refs):
        z_ref = refs[2 * n]
        y_ref, om_ref = refs[2 * n + 1:2 * n + 3]
        lse_refs = refs[2 * n + 3:3 * n + 3]
        o_refs = refs[3 * n + 3:4 * n + 3]
        l_bufs = refs[4 * n + 3:5 * n + 3]
        lse_buf = refs[5 * n + 3]
        ls = []
        for q, d in enumerate(dils):
            if d == 1:
                ls.append(refs[n + q][...])
            else:
                _store_token_order(o_refs[q], refs[q], d)
                _store_token_order(l_bufs[q], refs[n + q], d)
                ls.append(l_bufs[q][0])
        m = ls[0]
        for q in range(1, n):
            m = jnp.maximum(m, ls[q])
        es = [jnp.exp(v - m) for v in ls]
        den = es[0]
        for q in range(1, n):
            den = den + es[q]
        alphas = [e / den for e in es]
        lse = m + jnp.log(den)
        lse_buf[0] = lse
        for q, d in enumerate(dils):
            if d == 1:
                lse_refs[q][...] = lse
            else:
                _emit_group_order(lse_refs[q], lse_buf, d, F32)
        lane = lax.broadcasted_iota(jnp.int32, (ts, LANES), 1)
        for h in range(H):
            sl = slice(h * HEAD_DIM, (h + 1) * HEAD_DIM)
            om = jnp.zeros((ts, HEAD_DIM), F32)
            for q, d in enumerate(dils):
                o_h = refs[q][:, sl] if d == 1 else o_refs[q][h]
                om = om + _lane_col(alphas[q], h, lane) * o_h
            z = z_ref[:, sl].astype(F32)
            y_ref[:, sl] = (om * (z * _sigmoid(z))).astype(BF16)
            om_ref[:, sl] = om.astype(BF16)

    row = pl.BlockSpec((ts, A), lambda i: (i, 0))
    outs = pl.pallas_call(
        body, name=name, grid=(S // ts,),
        in_specs=[_group_spec(d, ts, A) for d in dils] + [_group_spec(d, ts, LANES) for d in dils] + [row],
        out_specs=[row, row] + [_group_spec(d, ts, LANES) for d in dils],
        out_shape=[jax.ShapeDtypeStruct((S, A), BF16), jax.ShapeDtypeStruct((S, A), BF16)]
        + [_group_shape(d, S, LANES, F32) for d in dils],
        scratch_shapes=[_chunk_buf(ts, A)] * n + [_chunk_buf(ts, LANES)] * (n + 1),
        compiler_params=_params("parallel"),
    )(*[o if d == 1 else o.reshape(d, S // d, A) for o, d in zip(os_, dils)],
      *[v if d == 1 else v.reshape(d, S // d, LANES) for v, d in zip(lses, dils)], z)
    return outs[0], outs[1], [v.reshape(S, LANES) for v in outs[2:]]


def _gate_bwd(dy, om, z, dils, name, ts=256):
    S, A = dy.shape
    H = A // HEAD_DIM
    ts = _tile(S, ts, 16 * max(dils))
    n = len(dils)

    def body(*refs):
        dy_ref, om_ref, z_ref = refs[:3]
        do_refs = refs[3:3 + n]
        dh_refs = refs[3 + n:3 + 2 * n]
        dz_ref = refs[3 + 2 * n]
        do_buf, dh_buf = refs[4 + 2 * n:6 + 2 * n]
        lane = lax.broadcasted_iota(jnp.int32, (ts, LANES), 1)
        acc = jnp.zeros((ts, LANES), F32)
        for h in range(H):
            sl = slice(h * HEAD_DIM, (h + 1) * HEAD_DIM)
            dyv = dy_ref[:, sl].astype(F32)
            omv = om_ref[:, sl].astype(F32)
            zv = z_ref[:, sl].astype(F32)
            sz = _sigmoid(zv)
            dob = (dyv * (zv * sz)).astype(BF16)
            do_buf[h] = dob.astype(F32)
            for q, d in enumerate(dils):
                if d == 1:
                    do_refs[q][:, sl] = dob
            dz_ref[:, sl] = (dyv * omv * (sz * (1.0 + zv * (1.0 - sz)))).astype(BF16)
            acc = jnp.where(lane == h, jnp.sum(dob.astype(F32) * omv, axis=-1, keepdims=True), acc)
        dh_buf[0] = acc
        for q, d in enumerate(dils):
            if d == 1:
                dh_refs[q][...] = acc
            else:
                _emit_group_order(do_refs[q], do_buf, d, BF16)
                _emit_group_order(dh_refs[q], dh_buf, d, F32)

    row = pl.BlockSpec((ts, A), lambda i: (i, 0))
    outs = pl.pallas_call(
        body, name=name, grid=(S // ts,), in_specs=[row, row, row],
        out_specs=[_group_spec(d, ts, A) for d in dils] + [_group_spec(d, ts, LANES) for d in dils] + [row],
        out_shape=[_group_shape(d, S, A, BF16) for d in dils] + [_group_shape(d, S, LANES, F32) for d in dils]
        + [jax.ShapeDtypeStruct((S, A), BF16)],
        scratch_shapes=[_chunk_buf(ts, A), _chunk_buf(ts, LANES)],
        compiler_params=_params("parallel"),
    )(dy, om, z)
    return ([v.reshape(S, A) for v in outs[:n]], [v.reshape(S, LANES) for v in outs[n:2 * n]], outs[2 * n])


def _attn_bwd(q, kv, do, lse, dh, bias, dil, name):
    S, A = q.shape
    H = A // HEAD_DIM
    L = S // dil
    nb = L // BLOCK
    qv = q.reshape(dil, L, A)
    kvv = kv.reshape(2, dil, L, A)
    dov = do.reshape(dil, L, A)
    lsev = lse.reshape(dil, L, LANES)
    dhv = dh.reshape(dil, L, LANES)

    def body(*refs):
        (q_ref, qn_ref, kp_ref, kc_ref, vp_ref, vc_ref, do_ref, don_ref, l_ref, ln_ref, d_ref, dn_ref,
         b_ref) = refs[:13]
        dq_ref, dkv_ref, db_ref = refs[13:16]
        r = pl.program_id(0)
        i = pl.program_id(1)
        qi = lax.broadcasted_iota(jnp.int32, (BLOCK, BLOCK), 0)
        ki = lax.broadcasted_iota(jnp.int32, (BLOCK, BLOCK), 1)
        mask_c = ki <= qi
        band = ki >= qi
        mask_p = jnp.logical_and(band, i > 0)
        mask_n = jnp.logical_and(band, i < nb - 1)
        lane = lax.broadcasted_iota(jnp.int32, (BLOCK, LANES), 1)

        @pl.when(jnp.logical_and(r == 0, i == 0))
        def _():
            db_ref[...] = jnp.zeros_like(db_ref)

        def products(h):
            sl = slice(h * HEAD_DIM, (h + 1) * HEAD_DIM)
            q_i, q_n = q_ref[:, sl], qn_ref[:, sl]
            k_p, k_c = kp_ref[:, sl], kc_ref[:, sl]
            v_p, v_c = vp_ref[:, sl], vc_ref[:, sl]
            do_i, do_n = do_ref[:, sl], don_ref[:, sl]
            return (_dot(q_i, k_c, NT), _dot(do_i, v_c, NT), _dot(q_i, k_p, NT), _dot(do_i, v_p, NT),
                    _dot(q_n, k_c, NT), _dot(do_n, v_c, NT))

        ahead = [products(h) for h in range(min(PRODUCTS_AHEAD, H))]
        for h in range(H):
            sl = slice(h * HEAD_DIM, (h + 1) * HEAD_DIM)
            s1, dp1, s2, dp2, s3, dp3 = ahead.pop(0)
            if h + PRODUCTS_AHEAD < H:
                ahead.append(products(h + PRODUCTS_AHEAD))
            q_i, q_n = q_ref[:, sl], qn_ref[:, sl]
            k_p, k_c = kp_ref[:, sl], kc_ref[:, sl]
            do_i, do_n = do_ref[:, sl], don_ref[:, sl]
            l_i, l_n = _lane_col(l_ref[...], h, lane), _lane_col(ln_ref[...], h, lane)
            d_i, d_n = _lane_col(d_ref[...], h, lane), _lane_col(dn_ref[...], h, lane)
            b_c = b_ref[h, :, BLOCK:]
            b_p = b_ref[h, :, :BLOCK]
            p1 = jnp.exp(jnp.where(mask_c, s1 * SCALE + b_c, NEG) - l_i)
            ds1 = p1 * (dp1 - d_i)
            ds1b = ds1.astype(BF16)
            p1b = p1.astype(BF16)
            p2 = jnp.exp(jnp.where(mask_p, s2 * SCALE + b_p, NEG) - l_i)
            ds2 = p2 * (dp2 - d_i)
            ds2b = ds2.astype(BF16)
            p3 = jnp.exp(jnp.where(mask_n, s3 * SCALE + b_p, NEG) - l_n)
            ds3b = (p3 * (dp3 - d_n)).astype(BF16)
            p3b = p3.astype(BF16)
            dq = _dot(ds1b, k_c, NN) + _dot(ds2b, k_p, NN)
            dk = _dot(ds1b, q_i, TN) + _dot(ds3b, q_n, TN)
            dv = _dot(p1b, do_i, TN) + _dot(p3b, do_n, TN)
            dq_ref[:, sl] = (dq * SCALE).astype(BF16)
            dkv_ref[0, :, sl] = (dk * SCALE).astype(BF16)
            dkv_ref[1, :, sl] = dv.astype(BF16)
            db_ref[h, :, BLOCK:] += ds1
            db_ref[h, :, :BLOCK] += ds2

    def blk(slab, shift):
        if shift < 0:
            return pl.BlockSpec((None, None, BLOCK, A), lambda r, i: (slab, r, jnp.maximum(i - 1, 0), 0))
        return pl.BlockSpec((None, None, BLOCK, A), lambda r, i: (slab, r, i, 0))

    def row(width, shift):
        if shift > 0:
            return pl.BlockSpec((None, BLOCK, width), lambda r, i: (r, jnp.minimum(i + 1, nb - 1), 0))
        return pl.BlockSpec((None, BLOCK, width), lambda r, i: (r, i, 0))

    in_specs = [row(A, 0), row(A, 1), blk(0, -1), blk(0, 0), blk(1, -1), blk(1, 0),
                row(A, 0), row(A, 1), row(LANES, 0), row(LANES, 1), row(LANES, 0), row(LANES, 1),
                pl.BlockSpec((H, BLOCK, 2 * BLOCK), lambda r, i: (0, 0, 0))]
    dq, dkv, db = pl.pallas_call(
        body, name=name, grid=(dil, nb), in_specs=in_specs,
        out_specs=[pl.BlockSpec((None, BLOCK, A), lambda r, i: (r, i, 0)),
                   pl.BlockSpec((2, None, BLOCK, A), lambda r, i: (0, r, i, 0)),
                   pl.BlockSpec((H, BLOCK, 2 * BLOCK), lambda r, i: (0, 0, 0))],
        out_shape=[jax.ShapeDtypeStruct((dil, L, A), BF16), jax.ShapeDtypeStruct((2, dil, L, A), BF16),
                   jax.ShapeDtypeStruct((H, BLOCK, 2 * BLOCK), F32)],
        compiler_params=_params("arbitrary", "arbitrary"),
    )(qv, qv, kvv, kvv, kvv, kvv, dov, dov, lsev, lsev, dhv, dhv, bias)
    return dq.reshape(S, A), dkv.reshape(2, S, A), db


def _sum_leading(stack, out_dtype, name, tr=256, tc=2048):
    n, R, C = stack.shape
    tr = _tile(R, tr, 16)
    tc = _tile(C, tc)

    def body(s_ref, o_ref):
        acc = s_ref[0].astype(F32)
        for q in range(1, n):
            acc = acc + s_ref[q].astype(F32)
        o_ref[...] = acc.astype(out_dtype)

    return pl.pallas_call(
        body, name=name, grid=(R // tr, C // tc),
        in_specs=[pl.BlockSpec((n, tr, tc), lambda i, j: (0, i, j))],
        out_specs=pl.BlockSpec((tr, tc), lambda i, j: (i, j)),
        out_shape=jax.ShapeDtypeStruct((R, C), out_dtype),
        compiler_params=_params("parallel", "parallel"),
    )(stack)


def _add_half(g, t, c_idx, kind, name, after=None, tr=256, tc=2048):
    R, C = t.shape
    tr = _tile(R, tr, 16)
    tc = _tile(C, tc)
    nrb, ncb = R // tr, C // tc
    extra = _as_list(after)

    def body(c_ref, g_ref, t_ref, *rest):
        del c_ref
        o_ref = rest[len(extra)]
        o_ref[...] = (g_ref[...].astype(F32) + t_ref[...].astype(F32)).astype(BF16)

    if kind == "col":
        g_map = lambda i, j, c_ref: (c_ref[0] * nrb + i, j)
    else:
        g_map = lambda i, j, c_ref: (i, c_ref[0] * ncb + j)
    same = lambda i, j, c_ref: (i, j)
    return pl.pallas_call(
        body, name=name,
        grid_spec=pltpu.PrefetchScalarGridSpec(
            num_scalar_prefetch=1, grid=(nrb, ncb),
            in_specs=[pl.BlockSpec((tr, tc), g_map), pl.BlockSpec((tr, tc), same)] + [ANY] * len(extra),
            out_specs=pl.BlockSpec((tr, tc), same)),
        out_shape=jax.ShapeDtypeStruct((R, C), BF16),
        compiler_params=_params("parallel", "parallel"),
    )(c_idx, g, t, *extra)


def _cast_into_full(w, kind, chip_idx, name, tr=256, tc=2048):
    R, C = w.shape
    tr = _tile(R, tr, 16)
    tc = _tile(C, tc)
    nrb, ncb = R // tr, C // tc

    def body(k_ref, w_ref, o_ref):
        del k_ref
        o_ref[...] = w_ref[...].astype(BF16)

    if kind == "col":
        o_map = lambda i, j, k_ref: (i, k_ref[0] * ncb + j)
        full = (R, N_CHIPS * C)
    else:
        o_map = lambda i, j, k_ref: (k_ref[0] * nrb + i, j)
        full = (N_CHIPS * R, C)
    return pl.pallas_call(
        body, name=name,
        grid_spec=pltpu.PrefetchScalarGridSpec(
            num_scalar_prefetch=1, grid=(nrb, ncb),
            in_specs=[pl.BlockSpec((tr, tc), lambda i, j, k_ref: (i, j))],
            out_specs=pl.BlockSpec((tr, tc), o_map)),
        out_shape=jax.ShapeDtypeStruct(full, BF16),
        compiler_params=_params("parallel", "parallel"),
    )(chip_idx, w)


def _sum_into_shard(p, u, idx, kind, name, tr=256, tc=2048):
    _, R, C = u.shape
    tr = _tile(R, tr, 16)
    tc = _tile(C, tc)
    nrb, ncb = R // tr, C // tc

    def body(i_ref, p_ref, u_ref, o_ref):
        del i_ref
        acc = p_ref[...].astype(F32)
        for q in range(N_CHIPS - 1):
            acc = acc + u_ref[q].astype(F32)
        o_ref[...] = acc

    if kind == "col":
        p_map = lambda i, j, r: (i, r[0] * ncb + j)
        o_map = lambda i, j, r: (r[1] * nrb + i, j)
        full = (2 * R, C)
    else:
        p_map = lambda i, j, r: (r[0] * nrb + i, j)
        o_map = lambda i, j, r: (i, r[1] * ncb + j)
        full = (R, 2 * C)
    return pl.pallas_call(
        body, name=name,
        grid_spec=pltpu.PrefetchScalarGridSpec(
            num_scalar_prefetch=1, grid=(nrb, ncb),
            in_specs=[pl.BlockSpec((tr, tc), p_map), pl.BlockSpec((N_CHIPS - 1, tr, tc), lambda i, j, r: (0, i, j))],
            out_specs=pl.BlockSpec((tr, tc), o_map)),
        out_shape=jax.ShapeDtypeStruct(full, F32),
        compiler_params=_params("parallel", "parallel"),
    )(idx, p, u)


def _adamw(w, g, m, v, name, tr=256, tc=2048):
    R, C = w.shape
    tr = _tile(R, tr, 8)
    tc = _tile(C, tc)
    c1 = 1.0 - ADAM_B1 ** ADAM_STEP
    c2 = 1.0 - ADAM_B2 ** ADAM_STEP

    def body(w_ref, g_ref, m_ref, v_ref, d_ref, nm_ref, nv_ref):
        gv = g_ref[...]
        nm = ADAM_B1 * m_ref[...] + (1.0 - ADAM_B1) * gv
        nv = ADAM_B2 * v_ref[...] + (1.0 - ADAM_B2) * (gv * gv)
        d_ref[...] = -ADAM_LR * ((nm / c1) / (jnp.sqrt(nv / c2) + ADAM_EPS) + ADAM_WD * w_ref[...])
        nm_ref[...] = nm
        nv_ref[...] = nv

    blk = pl.BlockSpec((tr, tc), lambda i, j: (i, j))
    sh = jax.ShapeDtypeStruct((R, C), F32)
    return pl.pallas_call(
        body, name=name, grid=(R // tr, C // tc), in_specs=[blk] * 4, out_specs=[blk] * 3,
        out_shape=[sh, sh, sh], compiler_params=_params("parallel", "parallel"),
    )(w, g, m, v)


def _xyc():
    return lax.axis_index("x"), lax.axis_index("y"), lax.axis_index("c")


def _drain(copies):
    for cp in copies:
        if cp.is_remote:
            cp.wait_send()
        else:
            cp.wait()


def _other_chips(x, y):
    return [(1 - x, y), (x, 1 - y), (1 - x, 1 - y)]


def _allgather_small(blk, name, after=None):
    R, C = blk.shape
    extra = _as_list(after)

    def body(*refs):
        x_ref = refs[0]
        out_ref, send_sems, recv_sems, local_sem = refs[1 + len(extra):]
        x, y, c = _xyc()
        me = 4 * x + 2 * y + c
        mine = pltpu.make_async_copy(x_ref, out_ref.at[me], local_sem)
        mine.start()
        peers = []
        for k in range(1, N_DEV):
            px = 1 - x if (k >> 2) & 1 else x
            py = 1 - y if (k >> 1) & 1 else y
            pc = 1 - c if k & 1 else c
            peers.append((px, py, pc))
        sends = []
        for k, peer in enumerate(peers):
            cp = pltpu.make_async_remote_copy(
                src_ref=x_ref, dst_ref=out_ref.at[me], send_sem=send_sems.at[k], recv_sem=recv_sems.at[k],
                device_id=peer, device_id_type=MESH)
            cp.start()
            sends.append(cp)
        for k, (px, py, pc) in enumerate(peers):
            pltpu.make_async_remote_copy(
                src_ref=x_ref, dst_ref=out_ref.at[4 * px + 2 * py + pc], send_sem=send_sems.at[k],
                recv_sem=recv_sems.at[k], device_id=(px, py, pc), device_id_type=MESH).wait_recv()
        for cp in sends:
            cp.wait_send()
        mine.wait()

    return pl.pallas_call(
        body, name=name, in_specs=[VMEM_SPEC] + [ANY] * len(extra), out_specs=VMEM_SPEC,
        out_shape=jax.ShapeDtypeStruct((N_DEV, R, C), blk.dtype),
        scratch_shapes=[pltpu.SemaphoreType.DMA((N_DEV - 1,)), pltpu.SemaphoreType.DMA((N_DEV - 1,)),
                        pltpu.SemaphoreType.DMA],
        compiler_params=pltpu.CompilerParams(vmem_limit_bytes=VMEM_LIMIT),
    )(blk, *extra)


def _full_region(ref, kind, chip, half, shard_shape):
    r, cn = shard_shape
    hr = r // 2
    if kind == "col":
        rows = pl.ds(0, r) if half is None else pl.ds(pl.multiple_of(half * hr, 16), hr)
        return ref.at[rows, pl.ds(pl.multiple_of(chip * cn, LANES), cn)]
    if half is None:
        return ref.at[pl.ds(pl.multiple_of(chip * r, 16), r), :]
    return ref.at[pl.ds(pl.multiple_of(chip * r + half * hr, 16), hr), :]


def _allgather_weights(fulls, kinds, shapes, name):
    n = len(fulls)

    def body(*refs):
        outs = refs[n:2 * n]
        send_sems, recv_sems = refs[2 * n:]
        x, y, c = _xyc()
        chip = 2 * x + y
        sib = (x, y, 1 - c)
        others = _other_chips(x, y)
        started = []
        for w in range(n):
            mine = _full_region(outs[w], kinds[w], chip, c, shapes[w])
            for j, (ox, oy) in enumerate(others):
                cp = pltpu.make_async_remote_copy(
                    src_ref=mine, dst_ref=mine, send_sem=send_sems.at[6 * w + j], recv_sem=recv_sems.at[6 * w + j],
                    device_id=(ox, oy, c), device_id_type=MESH)
                cp.start()
                started.append(cp)
        for w in range(n):
            for j, (ox, oy) in enumerate(others):
                landed = _full_region(outs[w], kinds[w], 2 * ox + oy, c, shapes[w])
                pltpu.make_async_remote_copy(
                    src_ref=landed, dst_ref=landed, send_sem=send_sems.at[6 * w + j], recv_sem=recv_sems.at[6 * w + j],
                    device_id=(ox, oy, c), device_id_type=MESH).wait_recv()
                cp = pltpu.make_async_remote_copy(
                    src_ref=landed, dst_ref=landed, send_sem=send_sems.at[6 * w + 3 + j],
                    recv_sem=recv_sems.at[6 * w + 3 + j], device_id=sib, device_id_type=MESH)
                cp.start()
                started.append(cp)
        for w in range(n):
            for j, (ox, oy) in enumerate(others):
                theirs = _full_region(outs[w], kinds[w], 2 * ox + oy, 1 - c, shapes[w])
                pltpu.make_async_remote_copy(
                    src_ref=theirs, dst_ref=theirs, send_sem=send_sems.at[6 * w + 3 + j],
                    recv_sem=recv_sems.at[6 * w + 3 + j], device_id=sib, device_id_type=MESH).wait_recv()
        _drain(started)

    return pl.pallas_call(
        body, name=name, in_specs=[ANY] * n, out_specs=[ANY] * n,
        out_shape=[jax.ShapeDtypeStruct(f.shape, f.dtype) for f in fulls],
        input_output_aliases={w: w for w in range(n)},
        scratch_shapes=[pltpu.SemaphoreType.DMA((6 * n,)), pltpu.SemaphoreType.DMA((6 * n,))],
    )(*fulls)


def _region_of_size(ref, kind, shard_shape, count):
    r, cn = shard_shape
    if kind == "col":
        return ref.at[pl.ds(0, r // 2), pl.ds(0, count * cn)]
    return ref.at[pl.ds(0, count * (r // 2)), :]


def _allgather_weights_seq(fulls, kinds, shapes, name, collective_id):
    n = len(fulls)
    refs = [jax.new_ref(f, memory_space=pltpu.MemorySpace.HBM) for f in fulls]

    def body(send_sems, recv_sems):
        x, y, c = _xyc()
        chip = 2 * x + y
        sib = (x, y, 1 - c)
        others = _other_chips(x, y)
        peers = [(ox, oy, c) for ox, oy in others] + [sib]
        barrier = pltpu.get_barrier_semaphore()
        for peer in peers:
            pl.semaphore_signal(barrier, inc=1, device_id=peer, device_id_type=MESH)
        pl.semaphore_wait(barrier, len(peers))

        def copy(w, region, sem, to):
            return pltpu.make_async_remote_copy(src_ref=region, dst_ref=region, send_sem=send_sems.at[sem],
                                                recv_sem=recv_sems.at[sem], device_id=to, device_id_type=MESH)

        for w in range(n):
            mine = _full_region(refs[w], kinds[w], chip, c, shapes[w])
            for ox, oy in others:
                copy(w, mine, 2 * w, (ox, oy, c)).start()
        for w in range(n):
            three = _region_of_size(refs[w], kinds[w], shapes[w], 3)
            copy(w, three, 2 * w, sib).wait_recv()
            for ox, oy in others:
                copy(w, _full_region(refs[w], kinds[w], 2 * ox + oy, c, shapes[w]), 2 * w + 1, sib).start()
        for w in range(n):
            three = _region_of_size(refs[w], kinds[w], shapes[w], 3)
            copy(w, three, 2 * w + 1, sib).wait_recv()
            copy(w, three, 2 * w, sib).wait_send()
            copy(w, three, 2 * w + 1, sib).wait_send()

    pl.kernel(
        body, out_type=(), mesh=plsc.ScalarSubcoreMesh(axis_name="seq", num_cores=1), name=name,
        scratch_types=[pltpu.SemaphoreType.DMA((2 * n,)), pltpu.SemaphoreType.DMA((2 * n,))],
        compiler_params=pltpu.CompilerParams(collective_id=collective_id),
    )()
    return [r[...] for r in refs]


def _half_of(ref, kind, half):
    r, cn = ref.shape
    if kind == "col":
        return ref.at[pl.ds(pl.multiple_of(half * (r // 2), 16), r // 2), :]
    return ref.at[:, pl.ds(pl.multiple_of(half * (cn // 2), LANES), cn // 2)]


def _shard_of(ref, kind, chip):
    r, cn = ref.shape
    if kind == "col":
        return ref.at[:, pl.ds(pl.multiple_of(chip * (cn // N_CHIPS), LANES), cn // N_CHIPS)]
    return ref.at[pl.ds(pl.multiple_of(chip * (r // N_CHIPS), 16), r // N_CHIPS), :]


def _exchange_halves(grads, kinds, name):
    n = len(grads)

    def body(*refs):
        gs = refs[:n]
        ts = refs[n:2 * n]
        send_sems, recv_sems = refs[2 * n:]
        x, y, c = _xyc()
        cps = []
        for w in range(n):
            cp = pltpu.make_async_remote_copy(
                src_ref=_half_of(gs[w], kinds[w], 1 - c), dst_ref=ts[w], send_sem=send_sems.at[w],
                recv_sem=recv_sems.at[w], device_id=(x, y, 1 - c), device_id_type=MESH)
            cp.start()
            cps.append(cp)
        for cp in cps:
            cp.wait()

    out_shape = []
    for gr, kind in zip(grads, kinds):
        r, cn = gr.shape
        out_shape.append(jax.ShapeDtypeStruct((r // 2, cn) if kind == "col" else (r, cn // 2), gr.dtype))
    return pl.pallas_call(
        body, name=name, in_specs=[ANY] * n, out_specs=[ANY] * n, out_shape=out_shape,
        scratch_shapes=[pltpu.SemaphoreType.DMA((n,)), pltpu.SemaphoreType.DMA((n,))],
    )(*grads)


def _scatter_partials(parts, kinds, name):
    n = len(parts)

    def body(*refs):
        ps = refs[:n]
        us = refs[n:2 * n]
        send_sems, recv_sems = refs[2 * n:]
        x, y, c = _xyc()
        others = _other_chips(x, y)
        cps = []
        for w in range(n):
            for j, (ox, oy) in enumerate(others):
                cp = pltpu.make_async_remote_copy(
                    src_ref=_shard_of(ps[w], kinds[w], 2 * ox + oy), dst_ref=us[w].at[j],
                    send_sem=send_sems.at[3 * w + j], recv_sem=recv_sems.at[3 * w + j],
                    device_id=(ox, oy, c), device_id_type=MESH)
                cp.start()
                cps.append(cp)
        for cp in cps:
            cp.wait()

    out_shape = []
    for p, kind in zip(parts, kinds):
        r, cn = p.shape
        hs = (r, cn // N_CHIPS) if kind == "col" else (r // N_CHIPS, cn)
        out_shape.append(jax.ShapeDtypeStruct((N_CHIPS - 1,) + hs, p.dtype))
    return pl.pallas_call(
        body, name=name, in_specs=[ANY] * n, out_specs=[ANY] * n, out_shape=out_shape,
        scratch_shapes=[pltpu.SemaphoreType.DMA((3 * n,)), pltpu.SemaphoreType.DMA((3 * n,))],
    )(*parts)


def _scatter_partials_seq(parts, kinds, name, collective_id):
    n = len(parts)

    def body(*refs):
        ps = refs[:n]
        us = refs[n:2 * n]
        send_sems, recv_sems = refs[2 * n:]
        x, y, c = _xyc()
        others = _other_chips(x, y)
        barrier = pltpu.get_barrier_semaphore()
        for ox, oy in others:
            pl.semaphore_signal(barrier, inc=1, device_id=(ox, oy, c), device_id_type=MESH)
        pl.semaphore_wait(barrier, len(others))
        for w in range(n):
            for j, (ox, oy) in enumerate(others):
                pltpu.make_async_remote_copy(
                    src_ref=_shard_of(ps[w], kinds[w], 2 * ox + oy), dst_ref=us[w].at[j],
                    send_sem=send_sems.at[w], recv_sem=recv_sems.at[w],
                    device_id=(ox, oy, c), device_id_type=MESH).start()
        for w in range(n):
            pltpu.make_async_remote_copy(
                src_ref=us[w], dst_ref=us[w], send_sem=send_sems.at[w], recv_sem=recv_sems.at[w],
                device_id=(x, y, c), device_id_type=MESH).wait()

    out_type = []
    for p, kind in zip(parts, kinds):
        r, cn = p.shape
        hs = (r, cn // N_CHIPS) if kind == "col" else (r // N_CHIPS, cn)
        out_type.append(jax.ShapeDtypeStruct((N_CHIPS - 1,) + hs, p.dtype))
    return pl.kernel(
        body, out_type=out_type, mesh=plsc.ScalarSubcoreMesh(axis_name="seq", num_cores=1), name=name,
        scratch_types=[pltpu.SemaphoreType.DMA((n,)), pltpu.SemaphoreType.DMA((n,))],
        compiler_params=pltpu.CompilerParams(collective_id=collective_id),
    )(*parts)


def _join_halves(halves, kinds, name):
    n = len(halves)

    def body(*refs):
        outs = refs[n:2 * n]
        send_sems, recv_sems = refs[2 * n:]
        x, y, c = _xyc()
        cps = []
        for w in range(n):
            mine = _half_of(outs[w], kinds[w], c)
            cp = pltpu.make_async_remote_copy(
                src_ref=mine, dst_ref=mine, send_sem=send_sems.at[w], recv_sem=recv_sems.at[w],
                device_id=(x, y, 1 - c), device_id_type=MESH)
            cp.start()
            cps.append(cp)
        for w in range(n):
            theirs = _half_of(outs[w], kinds[w], 1 - c)
            pltpu.make_async_remote_copy(
                src_ref=theirs, dst_ref=theirs, send_sem=send_sems.at[w], recv_sem=recv_sems.at[w],
                device_id=(x, y, 1 - c), device_id_type=MESH).wait_recv()
        _drain(cps)

    return pl.pallas_call(
        body, name=name, in_specs=[ANY] * n, out_specs=[ANY] * n,
        out_shape=[jax.ShapeDtypeStruct(h.shape, h.dtype) for h in halves],
        input_output_aliases={w: w for w in range(n)},
        scratch_shapes=[pltpu.SemaphoreType.DMA((n,)), pltpu.SemaphoreType.DMA((n,))],
    )(*halves)


def kernel(x, a_norm, a_w_in, a_conv_w, a_conv_b, a_ln_g, a_ln_b, a_w_out, kv_norm, w_kv, b_norm, b_w_in, b_w_out, rel_bias, final_norm, loss_target, m_a_norm, m_a_w_in, m_a_conv_w, m_a_conv_b, m_a_ln_g, m_a_ln_b, m_a_w_out, m_kv_norm, m_w_kv, m_b_norm, m_b_w_in, m_b_w_out, m_rel_bias, m_final_norm, v_a_norm, v_a_w_in, v_a_conv_w, v_a_conv_b, v_a_ln_g, v_a_ln_b, v_a_w_out, v_kv_norm, v_w_kv, v_b_norm, v_b_w_in, v_b_w_out, v_rel_bias, v_final_norm):
    S, D = x.shape[1], x.shape[2]
    E = a_w_out.shape[1] * N_CHIPS
    A = b_w_out.shape[1] * N_CHIPS
    H = A // HEAD_DIM
    DC = D // N_CHIPS
    xs = x.reshape(S, D)
    tgt = loss_target.reshape(S, D)
    cx, cy, cc = _xyc()
    chip = 2 * cx + cy
    c_idx = jnp.reshape(cc, (1,)).astype(jnp.int32)

    big_names = ["a_w_in", "a_w_out", "w_kv", "b_w_in", "b_w_out"]
    kinds = ["col", "row", "col", "col", "row"]
    big_w = [a_w_in[0], a_w_out[0], w_kv, b_w_in[0], b_w_out[0]]
    big_m = [m_a_w_in[0], m_a_w_out[0], m_w_kv, m_b_w_in[0], m_b_w_out[0]]
    big_v = [v_a_w_in[0], v_a_w_out[0], v_w_kv, v_b_w_in[0], v_b_w_out[0]]
    chip_idx = jnp.reshape(chip, (1,)).astype(jnp.int32)
    placed = [_cast_into_full(big_w[w], kinds[w], chip_idx, "cast_" + big_names[w]) for w in range(5)]
    shard_shapes = [w.shape for w in big_w]
    (wa_in,) = _allgather_weights_seq(placed[0:1], kinds[0:1], shard_shapes[0:1], "ag_seq_a_in", 0)
    wa_out, wkv = _allgather_weights_seq(placed[1:3], kinds[1:3], shard_shapes[1:3], "ag_seq_a_out_kv", 1)
    wb_in, wb_out = _allgather_weights_seq(placed[3:5], kinds[3:5], shard_shapes[3:5], "ag_seq_b", 2)

    def row_at(vec, q):
        return jnp.pad(vec, ((q, 7 - q), (0, 0)))

    def pack_sharded(an, cw, cb, lg, lb):
        return jnp.concatenate([row_at(an, 0), jnp.pad(cw[0], ((0, 1), (0, 0))),
                                row_at(lg, 0) + row_at(lb, 1) + row_at(cb, 2)], axis=0)

    small_w = pack_sharded(a_norm, a_conv_w, a_conv_b, a_ln_g, a_ln_b)
    gathered = _allgather_small(small_w, "ag_small_params")
    small_full = jnp.concatenate([gathered[2 * k] for k in range(N_CHIPS)], axis=1)
    g_a = small_full[0:1]
    conv_w32 = small_full[8:8 + HALO]
    ln_g = small_full[40:41]
    ln_b = small_full[41:42]
    conv_b = small_full[42:43]
    g_kv = kv_norm.reshape(1, D)
    g_b = b_norm.reshape(1, D)
    g_f = final_norm.reshape(1, D)

    rb_t = jnp.pad(rel_bias.T, ((0, 0), (0, LANES - N_BUCKETS)))
    onehots = [_onehot(dil) for _, dil in GROUPS]
    biases = [_bias_table(rb_t, onehots[g], "bias_table_%d" % g).reshape(H, BLOCK, 2 * BLOCK)
              for g in range(len(GROUPS))]

    dils = tuple(dil for _, dil in GROUPS)
    assert dils[0] == 1
    n_g = len(GROUPS)
    ((h0,),) = _rms_fwd(xs, [g_a], (1,), "rms_a")
    proj3 = _matmul(h0, wa_in, "nn", BF16, "mm_a_in", out_slab=E)
    conv = _conv_fwd(proj3, conv_w32, conv_b, "conv_fwd")
    y_a = _ln_gate_fwd(conv, proj3, ln_g, ln_b, "ln_gate_fwd")
    x1 = _matmul(y_a, wa_out, "nn", F32, "mm_a_out", res=xs)
    hks, hbs = _rms_fwd(x1, [g_kv, g_b], dils, "rms_kv_b")
    kvs = [_matmul(hks[g], wkv, "nn", BF16, "mm_kv_%d" % g, out_slab=A, b_off=2 * g * A, n_cols=2 * A)
           for g in range(n_g)]
    qs = [_matmul(hbs[g], wb_in, "nn", BF16, "mm_q_%d" % g, b_off=g * A, n_cols=A, after=kvs[-1])
          for g in range(n_g)]
    zb = _matmul(hbs[0], wb_in, "nn", BF16, "mm_zb", b_off=n_g * A, n_cols=A, after=kvs[-1])
    os_, lses = [], []
    for g, dil in enumerate(dils):
        o_g, lse_g = _attn_fwd(qs[g], kvs[g], biases[g], dil, "attn_fwd_%d" % g)
        os_.append(o_g)
        lses.append(lse_g)
    y_b, o_m, lse_d = _attn_merge(os_, lses, zb, dils, "attn_merge")
    x2 = _matmul(y_b, wb_out, "nn", F32, "mm_b_out", res=x1)
    loss_part, dx2, dx2b, gg_f = _final_head(x2, g_f, tgt, "final_head")
    loss = lax.psum(loss_part[0, 0], ("x", "y", "c"))

    dw_tiles = dict(tm=512, tn=1024, tk=4096)
    dy_b = _matmul(dx2b, wb_out, "nt", BF16, "mm_b_out_dx", after=loss.reshape(1, 1))
    dwb_out = _matmul(y_b, dx2b, "tn", BF16, "mm_b_out_dw", **dw_tiles)
    dos, dhs, dzb = _gate_bwd(dy_b, o_m, zb, dils, "gate_bwd")
    dbs, cots = [], []
    dwb_in = dwkv = None
    for g, dil in enumerate(dils):
        dq, dkv, db = _attn_bwd(qs[g], kvs[g], dos[g], lse_d[g], dhs[g], biases[g], dil, "attn_bwd_%d" % g)
        dbs.append(db.reshape(H, BLOCK * 2 * BLOCK))
        dwb_in = _matmul(hbs[g], dq, "tn", BF16, "mm_q_dw_%d" % g, out_off=g * A, out_cols=(n_g + 1) * A,
                         out_alias=dwb_in, **dw_tiles)
        dwkv = _matmul(hks[g], dkv, "tn", BF16, "mm_kv_dw_%d" % g, b_slab=True, out_off=2 * g * A,
                       out_cols=2 * n_g * A, out_alias=dwkv, **dw_tiles)
        cots.append((_matmul(dkv, wkv, "nt", BF16, "mm_kv_dx_%d" % g, a_slab=True, b_off=2 * g * A), 0, dil))
        cots.append((_matmul(dq, wb_in, "nt", BF16, "mm_q_dx_%d" % g, b_off=g * A), 1, dil))
    dwb_in = _matmul(hbs[0], dzb, "tn", BF16, "mm_zb_dw", out_off=n_g * A, out_cols=(n_g + 1) * A,
                     out_alias=dwb_in, **dw_tiles)
    cots.append((_matmul(dzb, wb_in, "nt", BF16, "mm_zb_dx", b_off=n_g * A), 1, 1))
    chip_c = jnp.stack([chip, cc]).astype(jnp.int32)

    def scatter_group(idx, grads, tag, collective_id):
        ks = [kinds[w] for w in idx]
        theirs = _exchange_halves(grads, ks, "rs_exchange_" + tag)
        parts = [_add_half(grads[q], theirs[q], c_idx, ks[q], "rs_add_half_%d" % w) for q, w in enumerate(idx)]
        return parts, _scatter_partials_seq(parts, ks, "rs_seq_" + tag, collective_id)

    def reduce_group(idx, parts, slots, tag):
        ks = [kinds[w] for w in idx]
        halves = [_sum_into_shard(parts[q], slots[q], chip_c, ks[q], "rs_sum_chips_%d" % w)
                  for q, w in enumerate(idx)]
        return _join_halves(halves, ks, "rs_join_" + tag)

    parts_b, slots_b = scatter_group([2, 3, 4], [dwkv, dwb_in, dwb_out], "b", 3)
    g_rel_t = _bias_grad(dbs, onehots, "bias_grad")
    dx1, dx1b, gg_kvb = _rms_bwd(x1, cots, [g_kv, g_b], dx2, "rms_kv_b_bwd", after=parts_b)
    dy_a = _matmul(dx1b, wa_out, "nt", BF16, "mm_a_out_dx")
    dwa_out = _matmul(y_a, dx1b, "tn", BF16, "mm_a_out_dw", **dw_tiles)
    dconv, dproj3, gg_ln = _ln_gate_bwd(conv, proj3, dy_a, ln_g, ln_b, "ln_gate_bwd")
    dproj3, g_conv_w = _conv_bwd(proj3, dconv, conv_w32, dproj3, "conv_bwd")
    dwa_in = _matmul(h0, dproj3, "tn", BF16, "mm_a_in_dw", b_slab=True, **dw_tiles)
    parts_a, slots_a = scatter_group([0, 1], [dwa_in, dwa_out], "a", 4)
    dh0 = _matmul(dproj3, wa_in, "nt", BF16, "mm_a_in_dx", a_slab=True, after=parts_a, tn=512)
    grad_x, _, gg_a = _rms_bwd(xs, [(dh0, 0, 1)], [g_a], dx1, "rms_a_bwd")

    big_g = [None] * 5
    big_g[2:5] = reduce_group([2, 3, 4], parts_b, slots_b, "b")
    big_g[0:2] = reduce_group([0, 1], parts_a, slots_a, "a")

    def rel_rows(rb):
        return jnp.pad(rb.reshape(1, N_BUCKETS * H), ((0, 7), (0, D - N_BUCKETS * H)))

    small_g = jnp.concatenate([gg_a, g_conv_w, gg_ln, gg_kvb, gg_f, rel_rows(g_rel_t[:, :N_BUCKETS].T)], axis=0)
    small_sum = _sum_leading(_allgather_small(small_g, "ag_small_grads", after=[slots_a[0], slots_b[0]]), F32,
                             "sum_small_grads", tr=72)
    g_sharded = lax.dynamic_slice(small_sum, (0, chip * DC), (48, DC))
    g_repl = small_sum[48:72]

    outs_g, outs_d, outs_m, outs_v = {}, {}, {}, {}
    for w, nm in enumerate(big_names):
        d_, m_, v_ = _adamw(big_w[w], big_g[w], big_m[w], big_v[w], "adamw_" + nm)
        outs_g[nm], outs_d[nm], outs_m[nm], outs_v[nm] = big_g[w], d_, m_, v_
    sm_m = pack_sharded(m_a_norm, m_a_conv_w, m_a_conv_b, m_a_ln_g, m_a_ln_b)
    sm_v = pack_sharded(v_a_norm, v_a_conv_w, v_a_conv_b, v_a_ln_g, v_a_ln_b)
    sd, smm, svv = _adamw(small_w, g_sharded, sm_m, sm_v, "adamw_small_sharded")

    def unpack_sharded(p):
        return {"a_norm": p[0:1], "a_conv_w": p[8:8 + CONV_TAPS].reshape(1, CONV_TAPS, DC), "a_ln_g": p[40:41],
                "a_ln_b": p[41:42], "a_conv_b": p[42:43]}

    for src, dst in ((g_sharded, outs_g), (sd, outs_d), (smm, outs_m), (svv, outs_v)):
        dst.update(unpack_sharded(src))

    def pack_repl(kn, bn, fn, rb):
        return jnp.concatenate([row_at(kn.reshape(1, D), 0) + row_at(bn.reshape(1, D), 1),
                                row_at(fn.reshape(1, D), 0), rel_rows(rb)], axis=0)

    rp_w = pack_repl(kv_norm, b_norm, final_norm, rel_bias)
    rp_m = pack_repl(m_kv_norm, m_b_norm, m_final_norm, m_rel_bias)
    rp_v = pack_repl(v_kv_norm, v_b_norm, v_final_norm, v_rel_bias)
    rd, rmm, rvv = _adamw(rp_w, g_repl, rp_m, rp_v, "adamw_small_replicated")

    def unpack_repl(p):
        return {"kv_norm": p[0], "b_norm": p[1:2], "final_norm": p[8],
                "rel_bias": p[16, :N_BUCKETS * H].reshape(N_BUCKETS, H)}

    for src, dst in ((g_repl, outs_g), (rd, outs_d), (rmm, outs_m), (rvv, outs_v)):
        dst.update(unpack_repl(src))

    order = ["a_norm", "a_w_in", "a_conv_w", "a_conv_b", "a_ln_g", "a_ln_b", "a_w_out", "kv_norm", "w_kv",
             "b_norm", "b_w_in", "b_w_out", "rel_bias", "final_norm"]
    lead = {"a_w_in", "a_w_out", "b_w_in", "b_w_out"}

    def shaped(nm, val):
        return val[None] if nm in lead else val

    result = [loss, grad_x.reshape(1, S, D)]
    for table in (outs_g, outs_d, outs_m, outs_v):
        result.extend(shaped(nm, table[nm]) for nm in order)
    return tuple(result)
```

```python
import functools

import numpy as np
import jax
import jax.numpy as jnp
from jax import lax
from jax.experimental import pallas as pl
from jax.experimental.pallas import tpu as pltpu
from jax.experimental.pallas import tpu_sc as plsc

F32 = jnp.float32
BF16 = jnp.bfloat16
MESH = pl.DeviceIdType.MESH
ANY = pl.BlockSpec(memory_space=pl.ANY)
VMEM_SPEC = pl.BlockSpec(memory_space=pltpu.VMEM)

EPS = 1e-6
HEAD_DIM = 128
BLOCK = 128
GROUPS = ((128, 1), (512, 4), (2048, 16))
SCALE = HEAD_DIM ** -0.5
CONV_TAPS = 31
HALO = 32
N_BUCKETS = 32
MAX_EXACT = 16
MAX_DISTANCE = 2048
NEG = -1e30
PRODUCTS_AHEAD = 2
SCORES_AHEAD = 4
N_CHIPS = 4
N_DEV = 8
LANES = 128
VMEM_LIMIT = 56 * 1024 * 1024

ADAM_LR = 0.001
ADAM_B1 = 0.9
ADAM_B2 = 0.999
ADAM_EPS = 1e-08
ADAM_WD = 0.01
ADAM_STEP = 10


def _tile(n, pref, mult=LANES):
    t = (min(pref, n) // mult) * mult
    while t >= mult:
        if n % t == 0:
            return t
        t -= mult
    return n


def _params(*sem):
    return pltpu.CompilerParams(dimension_semantics=sem, vmem_limit_bytes=VMEM_LIMIT)


def _sigmoid(v):
    return 1.0 / (1.0 + jnp.exp(-v))


def _dot(a, b, dims):
    return lax.dot_general(a, b, (dims, ((), ())), preferred_element_type=F32)


NN = ((1,), (0,))
NT = ((1,), (1,))
TN = ((0,), (0,))


def _as_list(after):
    if after is None:
        return []
    return list(after) if isinstance(after, (list, tuple)) else [after]


def _stack_rows(rows, total):
    width = rows[0].shape[1]
    rid = lax.broadcasted_iota(jnp.int32, (total, width), 0)
    out = jnp.zeros((total, width), F32)
    for q, row in enumerate(rows):
        out = jnp.where(rid == q, jnp.broadcast_to(row, (total, width)), out)
    return out


def _lane_col(arr, h, lane):
    return jnp.sum(jnp.where(lane == h, arr, 0.0), axis=-1, keepdims=True)


def _matmul(a, b, mode, out_dtype, name, res=None, a_slab=False, b_slab=False, out_slab=0,
            b_off=0, n_cols=None, out_off=0, out_cols=None, out_alias=None, after=None,
            tm=1024, tn=1024, tk=2048):
    if a_slab:
        na, M, W = a.shape
        K = na * W
    elif mode == "tn":
        K, M = a.shape
    else:
        M, K = a.shape
    if b_slab:
        nbs, _, Wb = b.shape
        N = nbs * Wb
    elif mode == "nt":
        N = b.shape[0]
    else:
        N = n_cols if n_cols else b.shape[1]
    tm = _tile(M, tm)
    tn = _tile(Wb if b_slab else (out_slab if out_slab else N), tn)
    tk = _tile(W if a_slab else K, tk)
    all_slabs = a_slab and mode == "nt" and tk == W
    if all_slabs:
        tk = K
    nk = K // tk
    grid = (M // tm, N // tn, nk)
    bo = b_off // (tk if mode == "nt" else tn)
    oo = out_off // tn

    if all_slabs:
        a_spec = pl.BlockSpec((na, tm, W), lambda i, j, k: (0, i, 0))
    elif a_slab:
        per = W // tk
        a_spec = pl.BlockSpec((None, tm, tk), lambda i, j, k: (k // per, i, k % per))
    elif mode == "tn":
        a_spec = pl.BlockSpec((tk, tm), lambda i, j, k: (k, i))
    else:
        a_spec = pl.BlockSpec((tm, tk), lambda i, j, k: (i, k))
    if b_slab:
        perb = Wb // tn
        b_spec = pl.BlockSpec((None, tk, tn), lambda i, j, k: (j // perb, k, j % perb))
    elif mode == "nt":
        b_spec = pl.BlockSpec((tn, tk), lambda i, j, k: (j, k + bo))
    else:
        b_spec = pl.BlockSpec((tk, tn), lambda i, j, k: (k, j + bo))
    if out_slab:
        pero = out_slab // tn
        o_spec = pl.BlockSpec((None, tm, tn), lambda i, j, k: (j // pero, i, j % pero))
        out_shape = jax.ShapeDtypeStruct((N // out_slab, M, out_slab), out_dtype)
    else:
        o_spec = pl.BlockSpec((tm, tn), lambda i, j, k: (i, j + oo))
        out_shape = jax.ShapeDtypeStruct((M, out_cols if out_cols else N), out_dtype)
    in_specs = [a_spec, b_spec]
    operands = [a, b]
    if res is not None:
        in_specs.append(pl.BlockSpec((tm, tn), lambda i, j, k: (i, j)))
        operands.append(res)
    aliases = {}
    if out_alias is not None:
        aliases[len(operands)] = 0
        in_specs.append(ANY)
        operands.append(out_alias)
    for arr in _as_list(after):
        in_specs.append(ANY)
        operands.append(arr)
    dims = {"nn": NN, "nt": NT, "tn": TN}[mode]
    has_res = res is not None
    n_in = len(operands)

    def body(*refs):
        a_ref, b_ref = refs[0], refs[1]
        r_ref = refs[2] if has_res else None
        o_ref = refs[n_in]
        if all_slabs:
            prod = _dot(a_ref[0], b_ref[:, 0:W], dims)
            for q in range(1, na):
                prod = prod + _dot(a_ref[q], b_ref[:, q * W:(q + 1) * W], dims)
        else:
            prod = _dot(a_ref[...], b_ref[...], dims)

        def finish(val):
            if has_res:
                val = val + r_ref[...]
            o_ref[...] = val.astype(out_dtype)

        if nk == 1:
            finish(prod)
        else:
            acc_ref = refs[n_in + 1]
            k = pl.program_id(2)

            @pl.when(k == 0)
            def _():
                acc_ref[...] = prod

            @pl.when(k > 0)
            def _():
                acc_ref[...] += prod

            @pl.when(k == nk - 1)
            def _():
                finish(acc_ref[...])

    scratch = [pltpu.VMEM((tm, tn), F32)] if nk > 1 else []
    return pl.pallas_call(
        body, name=name, grid=grid, in_specs=in_specs, out_specs=o_spec, out_shape=out_shape,
        scratch_shapes=scratch, input_output_aliases=aliases,
        compiler_params=_params("parallel", "parallel", "arbitrary"),
    )(*operands)


def _group_spec(d, ts, width):
    if d == 1:
        return pl.BlockSpec((ts, width), lambda i: (i, 0))
    return pl.BlockSpec((d, ts // d, width), lambda i: (0, i, 0))


def _group_shape(d, S, width, dtype):
    return jax.ShapeDtypeStruct((S, width) if d == 1 else (d, S // d, width), dtype)


def _chunk_buf(ts, width):
    return pltpu.VMEM((width // LANES, ts, LANES), F32)


def _fill_chunks(buf, val):
    for c in range(buf.shape[0]):
        buf[c] = val[:, c * LANES:(c + 1) * LANES]


def _read_chunks(buf):
    return jnp.concatenate([buf[c] for c in range(buf.shape[0])], axis=1)


def _emit_group_order(o_ref, buf, d, dtype):
    n = buf.shape[1] // d
    for r in range(d):
        for c in range(buf.shape[0]):
            o_ref[r, :, c * LANES:(c + 1) * LANES] = buf[c, pl.ds(r, n, stride=d), :].astype(dtype)


def _store_token_order(buf, i_ref, d):
    n = buf.shape[1] // d
    for r in range(d):
        for c in range(buf.shape[0]):
            buf[c, pl.ds(r, n, stride=d), :] = i_ref[r, :, c * LANES:(c + 1) * LANES].astype(F32)


def _rms_fwd(x, gains, dils, name, ts=256):
    S, D = x.shape
    ts = _tile(S, ts, 16 * max(dils))
    n = len(gains)
    nd = len(dils)

    def body(*refs):
        buf = refs[1 + n + n * nd]
        xv = refs[0][...]
        nrm = xv * lax.rsqrt(jnp.mean(xv * xv, axis=-1, keepdims=True) + EPS)
        for q in range(n):
            val = nrm * refs[1 + q][...]
            if max(dils) > 1:
                _fill_chunks(buf, val)
            for e, d in enumerate(dils):
                if d == 1:
                    refs[1 + n + q * nd + e][...] = val.astype(BF16)
                else:
                    _emit_group_order(refs[1 + n + q * nd + e], buf, d, BF16)

    row = pl.BlockSpec((ts, D), lambda i: (i, 0))
    vec = pl.BlockSpec((1, D), lambda i: (0, 0))
    outs = pl.pallas_call(
        body, name=name, grid=(S // ts,), in_specs=[row] + [vec] * n,
        out_specs=[_group_spec(d, ts, D) for _ in range(n) for d in dils],
        out_shape=[_group_shape(d, S, D, BF16) for _ in range(n) for d in dils],
        scratch_shapes=[_chunk_buf(ts, D)],
        compiler_params=_params("parallel"),
    )(x, *gains)
    return [[outs[q * nd + e].reshape(S, D) for e in range(nd)] for q in range(n)]


def _rms_bwd(x, cots, gains, dres, name, after=None, ts=256):
    S, D = x.shape
    ts = _tile(S, ts, 16 * max(d for _, _, d in cots))
    n = len(cots)
    ng = len(gains)
    extra = _as_list(after)
    n_in = 2 + n + ng + len(extra)

    def body(*refs):
        x_ref = refs[0]
        dh_refs = refs[1:1 + n]
        g_refs = refs[1 + n:1 + n + ng]
        dres_ref = refs[1 + n + ng]
        dx_ref, dxb_ref, gg_ref, buf = refs[n_in:n_in + 4]
        i = pl.program_id(0)
        xv = x_ref[...]
        r = lax.rsqrt(jnp.mean(xv * xv, axis=-1, keepdims=True) + EPS)
        nrm = xv * r
        dn = jnp.zeros_like(xv)
        rows = [jnp.zeros((1, D), F32) for _ in range(ng)]
        for q, (_, gi, d) in enumerate(cots):
            if d == 1:
                dh = dh_refs[q][...].astype(F32)
            else:
                _store_token_order(buf, dh_refs[q], d)
                dh = _read_chunks(buf)
            dn = dn + dh * g_refs[gi][...]
            rows[gi] = rows[gi] + jnp.sum(dh * nrm, axis=0, keepdims=True)
        dx = dres_ref[...] + r * (dn - nrm * jnp.mean(dn * nrm, axis=-1, keepdims=True))
        dx_ref[...] = dx
        dxb_ref[...] = dx.astype(BF16)
        upd = _stack_rows(rows, 8)

        @pl.when(i == 0)
        def _():
            gg_ref[...] = upd

        @pl.when(i > 0)
        def _():
            gg_ref[...] += upd

    row = pl.BlockSpec((ts, D), lambda i: (i, 0))
    vec = pl.BlockSpec((1, D), lambda i: (0, 0))
    acc = pl.BlockSpec((8, D), lambda i: (0, 0))
    return pl.pallas_call(
        body, name=name, grid=(S // ts,),
        in_specs=[row] + [_group_spec(d, ts, D) for _, _, d in cots] + [vec] * ng + [row] + [ANY] * len(extra),
        out_specs=[row, row, acc],
        out_shape=[jax.ShapeDtypeStruct((S, D), F32), jax.ShapeDtypeStruct((S, D), BF16),
                   jax.ShapeDtypeStruct((8, D), F32)],
        scratch_shapes=[_chunk_buf(ts, D)],
        compiler_params=_params("arbitrary"),
    )(x, *[a if d == 1 else a.reshape(d, S // d, D) for a, _, d in cots], *gains, dres, *extra)


def _final_head(x2, gain, target, name, ts=256):
    S, D = x2.shape
    ts = _tile(S, ts, 16)

    def body(x_ref, g_ref, t_ref, loss_ref, dx_ref, dxb_ref, gg_ref):
        i = pl.program_id(0)
        xv = x_ref[...]
        g = g_ref[...]
        r = lax.rsqrt(jnp.mean(xv * xv, axis=-1, keepdims=True) + EPS)
        nrm = xv * r
        err = nrm * g - t_ref[...]
        part = 0.5 * jnp.sum(jnp.mean(err * err, axis=-1, keepdims=True), axis=0, keepdims=True)
        dout = err * (1.0 / D)
        dn = dout * g
        dx = r * (dn - nrm * jnp.mean(dn * nrm, axis=-1, keepdims=True))
        dx_ref[...] = dx
        dxb_ref[...] = dx.astype(BF16)
        upd = _stack_rows([jnp.sum(dout * nrm, axis=0, keepdims=True)], 8)
        lpart = jnp.broadcast_to(part, (1, LANES))

        @pl.when(i == 0)
        def _():
            gg_ref[...] = upd
            loss_ref[...] = lpart

        @pl.when(i > 0)
        def _():
            gg_ref[...] += upd
            loss_ref[...] += lpart

    row = pl.BlockSpec((ts, D), lambda i: (i, 0))
    vec = pl.BlockSpec((1, D), lambda i: (0, 0))
    return pl.pallas_call(
        body, name=name, grid=(S // ts,), in_specs=[row, vec, row],
        out_specs=[pl.BlockSpec((1, LANES), lambda i: (0, 0)), row, row, pl.BlockSpec((8, D), lambda i: (0, 0))],
        out_shape=[jax.ShapeDtypeStruct((1, LANES), F32), jax.ShapeDtypeStruct((S, D), F32),
                   jax.ShapeDtypeStruct((S, D), BF16), jax.ShapeDtypeStruct((8, D), F32)],
        compiler_params=_params("arbitrary"),
    )(x2, gain, target)


CONV_ROWS = 64


SUBLANES = 8


def _shifted_buf(ts, cw):
    return pltpu.VMEM((SUBLANES - 1, ts + HALO - SUBLANES, cw), F32)


def _fill_shifted(shifted, buf):
    rows = shifted.shape[1]
    for s in range(1, SUBLANES):
        shifted[s - 1] = buf[s:s + rows, :]


def _window(buf, shifted, off, rows):
    s = off % SUBLANES
    base = off - s
    if s == 0:
        return buf[base:base + rows, :]
    return shifted[s - 1, base:base + rows, :]


def _conv_fwd(proj3, conv_w32, conv_b, name, ts=256, cw=256):
    _, S, E = proj3.shape
    ts = _tile(S, ts, HALO)
    cw = _tile(E, cw)
    per = ts // HALO
    rc = min(CONV_ROWS, ts)

    def body(a_ref, b_ref, ap_ref, bp_ref, w_ref, cb_ref, c_ref, ubuf, shifted):
        i = pl.program_id(0)
        up = ap_ref[...].astype(F32) * _sigmoid(bp_ref[...].astype(F32))
        ubuf[0:HALO, :] = jnp.where(i > 0, up, 0.0)
        ubuf[HALO:HALO + ts, :] = a_ref[...].astype(F32) * _sigmoid(b_ref[...].astype(F32))
        _fill_shifted(shifted, ubuf)
        for r0 in range(0, ts, rc):
            acc = jnp.broadcast_to(cb_ref[...], (rc, cw))
            for k in range(CONV_TAPS):
                off = r0 + HALO - (CONV_TAPS - 1) + k
                acc = acc + _window(ubuf, shifted, off, rc) * w_ref[k:k + 1, :]
            c_ref[r0:r0 + rc, :] = acc

    return pl.pallas_call(
        body, name=name, grid=(S // ts, E // cw),
        in_specs=[
            pl.BlockSpec((None, ts, cw), lambda i, j: (0, i, j)),
            pl.BlockSpec((None, ts, cw), lambda i, j: (1, i, j)),
            pl.BlockSpec((None, HALO, cw), lambda i, j: (0, jnp.maximum(i * per - 1, 0), j)),
            pl.BlockSpec((None, HALO, cw), lambda i, j: (1, jnp.maximum(i * per - 1, 0), j)),
            pl.BlockSpec((HALO, cw), lambda i, j: (0, j)),
            pl.BlockSpec((1, cw), lambda i, j: (0, j)),
        ],
        out_specs=pl.BlockSpec((ts, cw), lambda i, j: (i, j)),
        out_shape=jax.ShapeDtypeStruct((S, E), F32),
        scratch_shapes=[pltpu.VMEM((HALO + ts, cw), F32), _shifted_buf(ts, cw)],
        compiler_params=_params("parallel", "parallel"),
    )(proj3, proj3, proj3, proj3, conv_w32, conv_b)


def _ln_gate_fwd(c, proj3, ln_g, ln_b, name, ts=256):
    S, E = c.shape
    ts = _tile(S, ts, 16)

    def body(c_ref, z_ref, g_ref, b_ref, y_ref):
        cv = c_ref[...]
        mu = jnp.mean(cv, axis=-1, keepdims=True)
        d = cv - mu
        var = jnp.mean(d * d, axis=-1, keepdims=True)
        cn = d * lax.rsqrt(var + EPS) * g_ref[...] + b_ref[...]
        z = z_ref[...].astype(F32)
        y_ref[...] = ((cn * _sigmoid(cn)).astype(F32) * (z * _sigmoid(z))).astype(BF16)

    row = pl.BlockSpec((ts, E), lambda i: (i, 0))
    vec = pl.BlockSpec((1, E), lambda i: (0, 0))
    return pl.pallas_call(
        body, name=name, grid=(S // ts,),
        in_specs=[row, pl.BlockSpec((None, ts, E), lambda i: (2, i, 0)), vec, vec],
        out_specs=row, out_shape=jax.ShapeDtypeStruct((S, E), BF16),
        compiler_params=_params("parallel"),
    )(c, proj3, ln_g, ln_b)


def _ln_gate_bwd(c, proj3, dy, ln_g, ln_b, name, ts=256):
    S, E = c.shape
    ts = _tile(S, ts, 16)

    def body(c_ref, z_ref, dy_ref, g_ref, b_ref, dc_ref, dz_ref, acc_ref):
        i = pl.program_id(0)
        cv = c_ref[...]
        g = g_ref[...]
        mu = jnp.mean(cv, axis=-1, keepdims=True)
        d = cv - mu
        var = jnp.mean(d * d, axis=-1, keepdims=True)
        rstd = lax.rsqrt(var + EPS)
        chat = d * rstd
        cn = chat * g + b_ref[...]
        z = z_ref[...].astype(F32)
        dyv = dy_ref[...].astype(F32)
        sc = _sigmoid(cn)
        sz = _sigmoid(z)
        dcn = dyv * (z * sz) * (sc * (1.0 + cn * (1.0 - sc)))
        dz_ref[...] = (dyv * (cn * sc) * (sz * (1.0 + z * (1.0 - sz)))).astype(BF16)
        dchat = dcn * g
        dcv = rstd * (dchat - jnp.mean(dchat, axis=-1, keepdims=True)
                      - chat * jnp.mean(dchat * chat, axis=-1, keepdims=True))
        dc_ref[...] = dcv
        upd = _stack_rows([jnp.sum(dcn * chat, axis=0, keepdims=True),
                           jnp.sum(dcn, axis=0, keepdims=True),
                           jnp.sum(dcv, axis=0, keepdims=True)], 8)

        @pl.when(i == 0)
        def _():
            acc_ref[...] = upd

        @pl.when(i > 0)
        def _():
            acc_ref[...] += upd

    row = pl.BlockSpec((ts, E), lambda i: (i, 0))
    vec = pl.BlockSpec((1, E), lambda i: (0, 0))
    return pl.pallas_call(
        body, name=name, grid=(S // ts,),
        in_specs=[row, pl.BlockSpec((None, ts, E), lambda i: (2, i, 0)), row, vec, vec],
        out_specs=[row, pl.BlockSpec((None, ts, E), lambda i: (2, i, 0)), pl.BlockSpec((8, E), lambda i: (0, 0))],
        out_shape=[jax.ShapeDtypeStruct((S, E), F32), jax.ShapeDtypeStruct((3, S, E), BF16),
                   jax.ShapeDtypeStruct((8, E), F32)],
        compiler_params=_params("arbitrary"),
    )(c, proj3, dy, ln_g, ln_b)


def _conv_bwd(proj3, dc, conv_w32, dproj3, name, ts=256, cw=256):
    _, S, E = proj3.shape
    ts = _tile(S, ts, HALO)
    cw = _tile(E, cw)
    per = ts // HALO
    n_i = S // ts
    last_halo = S // HALO - 1
    rc = min(CONV_ROWS, ts)

    def body(a_ref, b_ref, dc_ref, dcn_ref, w_ref, dp_in, dab_ref, dw_ref, dcbuf, ubuf, dwacc, shifted):
        del dp_in
        i = pl.program_id(1)
        dcbuf[0:ts, :] = dc_ref[...]
        dcbuf[ts:ts + HALO, :] = jnp.where(i < n_i - 1, dcn_ref[...], 0.0)
        _fill_shifted(shifted, dcbuf)
        av = a_ref[...].astype(F32)
        sb = _sigmoid(b_ref[...].astype(F32))
        ubuf[...] = av * sb

        @pl.when(i == 0)
        def _():
            dwacc[...] = jnp.zeros_like(dwacc)

        for r0 in range(0, ts, rc):
            uv = ubuf[r0:r0 + rc, :]
            du = jnp.zeros((rc, cw), F32)
            for d in range(CONV_TAPS):
                k = CONV_TAPS - 1 - d
                win = _window(dcbuf, shifted, r0 + d, rc)
                du = du + win * w_ref[k:k + 1, :]
                dwacc[k:k + 1, :] += jnp.sum(uv * win, axis=0, keepdims=True)
            a_c = a_ref[r0:r0 + rc, :].astype(F32)
            s_c = _sigmoid(b_ref[r0:r0 + rc, :].astype(F32))
            dab_ref[0, r0:r0 + rc, :] = (du * s_c).astype(BF16)
            dab_ref[1, r0:r0 + rc, :] = (du * a_c * s_c * (1.0 - s_c)).astype(BF16)

        @pl.when(i == n_i - 1)
        def _():
            dw_ref[...] = dwacc[...]

    return pl.pallas_call(
        body, name=name, grid=(E // cw, n_i),
        in_specs=[
            pl.BlockSpec((None, ts, cw), lambda j, i: (0, i, j)),
            pl.BlockSpec((None, ts, cw), lambda j, i: (1, i, j)),
            pl.BlockSpec((ts, cw), lambda j, i: (i, j)),
            pl.BlockSpec((HALO, cw), lambda j, i: (jnp.minimum((i + 1) * per, last_halo), j)),
            pl.BlockSpec((HALO, cw), lambda j, i: (0, j)),
            ANY,
        ],
        out_specs=[pl.BlockSpec((2, ts, cw), lambda j, i: (0, i, j)),
                   pl.BlockSpec((HALO, cw), lambda j, i: (0, j))],
        out_shape=[jax.ShapeDtypeStruct((3, S, E), BF16), jax.ShapeDtypeStruct((HALO, E), F32)],
        scratch_shapes=[pltpu.VMEM((ts + HALO, cw), F32), pltpu.VMEM((ts, cw), F32), pltpu.VMEM((HALO, cw), F32),
                        _shifted_buf(ts, cw)],
        input_output_aliases={5: 0},
        compiler_params=_params("parallel", "arbitrary"),
    )(proj3, proj3, dc, dc, conv_w32, dproj3)


def _bucket_table(dil):
    delta = (np.arange(BLOCK)[:, None] + BLOCK) - np.arange(2 * BLOCK)[None, :]
    dist = np.clip(delta, 0, None) * dil
    large = MAX_EXACT + (np.log(np.maximum(dist, 1).astype(np.float32) / MAX_EXACT)
                         / np.log(MAX_DISTANCE / MAX_EXACT) * (N_BUCKETS - MAX_EXACT)).astype(np.int32)
    large = np.minimum(large, N_BUCKETS - 1)
    return np.where(dist < MAX_EXACT, dist, large).astype(np.int32).reshape(-1)


def _onehot(dil):
    tbl = jnp.asarray(_bucket_table(dil))
    return (tbl[None, :] == jnp.arange(LANES, dtype=jnp.int32)[:, None]).astype(BF16)


def _split3(v):
    hi = v.astype(BF16)
    r1 = v - hi.astype(F32)
    mid = r1.astype(BF16)
    lo = (r1 - mid.astype(F32)).astype(BF16)
    return hi, mid, lo


def _bias_table(rb_t, onehot, name):
    H = rb_t.shape[0]
    N = onehot.shape[1]

    def body(r_ref, oh_ref, o_ref):
        oh = oh_ref[...]
        hi, mid, lo = _split3(r_ref[...])
        o_ref[...] = (_dot(lo, oh, NN) + _dot(mid, oh, NN)) + _dot(hi, oh, NN)

    return pl.pallas_call(
        body, name=name, in_specs=[VMEM_SPEC, VMEM_SPEC], out_specs=VMEM_SPEC,
        out_shape=jax.ShapeDtypeStruct((H, N), F32),
        compiler_params=pltpu.CompilerParams(vmem_limit_bytes=VMEM_LIMIT),
    )(rb_t, onehot)


def _bias_grad(dbs, onehots, name):
    H = dbs[0].shape[0]
    n = len(dbs)

    def body(*refs):
        acc = jnp.zeros((H, LANES), F32)
        for q in range(n):
            oh = refs[n + q][...]
            hi, mid, lo = _split3(refs[q][...])
            acc = acc + ((_dot(lo, oh, NT) + _dot(mid, oh, NT)) + _dot(hi, oh, NT))
        refs[2 * n][...] = acc

    return pl.pallas_call(
        body, name=name, in_specs=[VMEM_SPEC] * (2 * n), out_specs=VMEM_SPEC,
        out_shape=jax.ShapeDtypeStruct((H, LANES), F32),
        compiler_params=pltpu.CompilerParams(vmem_limit_bytes=VMEM_LIMIT),
    )(*dbs, *onehots)


def _pair_mask(has_prev):
    qi = lax.broadcasted_iota(jnp.int32, (BLOCK, 2 * BLOCK), 0)
    ki = lax.broadcasted_iota(jnp.int32, (BLOCK, 2 * BLOCK), 1)
    prev = jnp.logical_and(jnp.logical_and(ki < BLOCK, ki >= qi), has_prev)
    return jnp.logical_or(prev, jnp.logical_and(ki >= BLOCK, ki - BLOCK <= qi))


def _attn_fwd(q, kv, bias, dil, name):
    S, A = q.shape
    H = A // HEAD_DIM
    L = S // dil
    nb = L // BLOCK
    qv = q.reshape(dil, L, A)
    kvv = kv.reshape(2, dil, L, A)

    def body(q_ref, kp_ref, kc_ref, vp_ref, vc_ref, b_ref, o_ref, lse_ref):
        i = pl.program_id(1)
        qi = lax.broadcasted_iota(jnp.int32, (BLOCK, BLOCK), 0)
        ki = lax.broadcasted_iota(jnp.int32, (BLOCK, BLOCK), 1)
        mask = _pair_mask(i > 0)
        lane = lax.broadcasted_iota(jnp.int32, (BLOCK, LANES), 1)
        lse_acc = jnp.zeros((BLOCK, LANES), F32)

        def scores(h):
            sl = slice(h * HEAD_DIM, (h + 1) * HEAD_DIM)
            return _dot(q_ref[:, sl], jnp.concatenate([kp_ref[:, sl], kc_ref[:, sl]], axis=0), NT)

        ahead = [scores(h) for h in range(min(SCORES_AHEAD, H))]
        for h in range(H):
            sl = slice(h * HEAD_DIM, (h + 1) * HEAD_DIM)
            raw = ahead.pop(0)
            if h + SCORES_AHEAD < H:
                ahead.append(scores(h + SCORES_AHEAD))
            s = jnp.where(mask, raw * SCALE + b_ref[h], NEG)
            m = jnp.max(s, axis=-1, keepdims=True)
            p = jnp.exp(s - m)
            den = jnp.sum(p, axis=-1, keepdims=True)
            acc = _dot(p.astype(BF16), jnp.concatenate([vp_ref[:, sl], vc_ref[:, sl]], axis=0), NN)
            o_ref[:, sl] = acc / den
            lse_acc = jnp.where(lane == h, m + jnp.log(den), lse_acc)
        lse_ref[...] = lse_acc

    def blk(slab, prev):
        if prev:
            return pl.BlockSpec((None, None, BLOCK, A), lambda r, i: (slab, r, jnp.maximum(i - 1, 0), 0))
        return pl.BlockSpec((None, None, BLOCK, A), lambda r, i: (slab, r, i, 0))

    o, lse = pl.pallas_call(
        body, name=name, grid=(dil, nb),
        in_specs=[pl.BlockSpec((None, BLOCK, A), lambda r, i: (r, i, 0)),
                  blk(0, True), blk(0, False), blk(1, True), blk(1, False),
                  pl.BlockSpec((H, BLOCK, 2 * BLOCK), lambda r, i: (0, 0, 0))],
        out_specs=[pl.BlockSpec((None, BLOCK, A), lambda r, i: (r, i, 0)),
                   pl.BlockSpec((None, BLOCK, LANES), lambda r, i: (r, i, 0))],
        out_shape=[jax.ShapeDtypeStruct((dil, L, A), F32), jax.ShapeDtypeStruct((dil, L, LANES), F32)],
        compiler_params=_params("parallel", "parallel"),
    )(qv, kvv, kvv, kvv, kvv, bias)
    return o.reshape(S, A), lse.reshape(S, LANES)


def _attn_merge(os_, lses, z, dils, name, ts=256):
    S, A = z.shape
    H = A // HEAD_DIM
    ts = _tile(S, ts, 16 * max(dils))
    n = len(os_)

    def body(*refs):
        z_ref = refs[2 * n]
        y_ref, om_ref = refs[2 * n + 1:2 * n + 3]
        lse_refs = refs[2 * n + 3:3 * n + 3]
        o_refs = refs[3 * n + 3:4 * n + 3]
        l_bufs = refs[4 * n + 3:5 * n + 3]
        lse_buf = refs[5 * n + 3]
        ls = []
        for q, d in enumerate(dils):
            if d == 1:
                ls.append(refs[n + q][...])
            else:
                _store_token_order(o_refs[q], refs[q], d)
                _store_token_order(l_bufs[q], refs[n + q], d)
                ls.append(l_bufs[q][0])
        m = ls[0]
        for q in range(1, n):
            m = jnp.maximum(m, ls[q])
        es = [jnp.exp(v - m) for v in ls]
        den = es[0]
        for q in range(1, n):
            den = den + es[q]
        alphas = [e / den for e in es]
        lse = m + jnp.log(den)
        lse_buf[0] = lse
        for q, d in enumerate(dils):
            if d == 1:
                lse_refs[q][...] = lse
            else:
                _emit_group_order(lse_refs[q], lse_buf, d, F32)
        lane = lax.broadcasted_iota(jnp.int32, (ts, LANES), 1)
        for h in range(H):
            sl = slice(h * HEAD_DIM, (h + 1) * HEAD_DIM)
            om = jnp.zeros((ts, HEAD_DIM), F32)
            for q, d in enumerate(dils):
                o_h = refs[q][:, sl] if d == 1 else o_refs[q][h]
                om = om + _lane_col(alphas[q], h, lane) * o_h
            z = z_ref[:, sl].astype(F32)
            y_ref[:, sl] = (om * (z * _sigmoid(z))).astype(BF16)
            om_ref[:, sl] = om.astype(BF16)

    row = pl.BlockSpec((ts, A), lambda i: (i, 0))
    outs = pl.pallas_call(
        body, name=name, grid=(S // ts,),
        in_specs=[_group_spec(d, ts, A) for d in dils] + [_group_spec(d, ts, LANES) for d in dils] + [row],
        out_specs=[row, row] + [_group_spec(d, ts, LANES) for d in dils],
        out_shape=[jax.ShapeDtypeStruct((S, A), BF16), jax.ShapeDtypeStruct((S, A), BF16)]
        + [_group_shape(d, S, LANES, F32) for d in dils],
        scratch_shapes=[_chunk_buf(ts, A)] * n + [_chunk_buf(ts, LANES)] * (n + 1),
        compiler_params=_params("parallel"),
    )(*[o if d == 1 else o.reshape(d, S // d, A) for o, d in zip(os_, dils)],
      *[v if d == 1 else v.reshape(d, S // d, LANES) for v, d in zip(lses, dils)], z)
    return outs[0], outs[1], [v.reshape(S, LANES) for v in outs[2:]]


def _gate_bwd(dy, om, z, dils, name, ts=256):
    S, A = dy.shape
    H = A // HEAD_DIM
    ts = _tile(S, ts, 16 * max(dils))
    n = len(dils)

    def body(*refs):
        dy_ref, om_ref, z_ref = refs[:3]
        do_refs = refs[3:3 + n]
        dh_refs = refs[3 + n:3 + 2 * n]
        dz_ref = refs[3 + 2 * n]
        do_buf, dh_buf = refs[4 + 2 * n:6 + 2 * n]
        lane = lax.broadcasted_iota(jnp.int32, (ts, LANES), 1)
        acc = jnp.zeros((ts, LANES), F32)
        for h in range(H):
            sl = slice(h * HEAD_DIM, (h + 1) * HEAD_DIM)
            dyv = dy_ref[:, sl].astype(F32)
            omv = om_ref[:, sl].astype(F32)
            zv = z_ref[:, sl].astype(F32)
            sz = _sigmoid(zv)
            dob = (dyv * (zv * sz)).astype(BF16)
            do_buf[h] = dob.astype(F32)
            for q, d in enumerate(dils):
                if d == 1:
                    do_refs[q][:, sl] = dob
            dz_ref[:, sl] = (dyv * omv * (sz * (1.0 + zv * (1.0 - sz)))).astype(BF16)
            acc = jnp.where(lane == h, jnp.sum(dob.astype(F32) * omv, axis=-1, keepdims=True), acc)
        dh_buf[0] = acc
        for q, d in enumerate(dils):
            if d == 1:
                dh_refs[q][...] = acc
            else:
                _emit_group_order(do_refs[q], do_buf, d, BF16)
                _emit_group_order(dh_refs[q], dh_buf, d, F32)

    row = pl.BlockSpec((ts, A), lambda i: (i, 0))
    outs = pl.pallas_call(
        body, name=name, grid=(S // ts,), in_specs=[row, row, row],
        out_specs=[_group_spec(d, ts, A) for d in dils] + [_group_spec(d, ts, LANES) for d in dils] + [row],
        out_shape=[_group_shape(d, S, A, BF16) for d in dils] + [_group_shape(d, S, LANES, F32) for d in dils]
        + [jax.ShapeDtypeStruct((S, A), BF16)],
        scratch_shapes=[_chunk_buf(ts, A), _chunk_buf(ts, LANES)],
        compiler_params=_params("parallel"),
    )(dy, om, z)
    return ([v.reshape(S, A) for v in outs[:n]], [v.reshape(S, LANES) for v in outs[n:2 * n]], outs[2 * n])


def _attn_bwd(q, kv, do, lse, dh, bias, dil, name):
    S, A = q.shape
    H = A // HEAD_DIM
    L = S // dil
    nb = L // BLOCK
    qv = q.reshape(dil, L, A)
    kvv = kv.reshape(2, dil, L, A)
    dov = do.reshape(dil, L, A)
    lsev = lse.reshape(dil, L, LANES)
    dhv = dh.reshape(dil, L, LANES)

    def body(*refs):
        (q_ref, qn_ref, kp_ref, kc_ref, vp_ref, vc_ref, do_ref, don_ref, l_ref, ln_ref, d_ref, dn_ref,
         b_ref) = refs[:13]
        dq_ref, dkv_ref, db_ref = refs[13:16]
        r = pl.program_id(0)
        i = pl.program_id(1)
        qi = lax.broadcasted_iota(jnp.int32, (BLOCK, BLOCK), 0)
        ki = lax.broadcasted_iota(jnp.int32, (BLOCK, BLOCK), 1)
        mask_c = ki <= qi
        band = ki >= qi
        mask_p = jnp.logical_and(band, i > 0)
        mask_n = jnp.logical_and(band, i < nb - 1)
        lane = lax.broadcasted_iota(jnp.int32, (BLOCK, LANES), 1)

        @pl.when(jnp.logical_and(r == 0, i == 0))
        def _():
            db_ref[...] = jnp.zeros_like(db_ref)

        def products(h):
            sl = slice(h * HEAD_DIM, (h + 1) * HEAD_DIM)
            q_i, q_n = q_ref[:, sl], qn_ref[:, sl]
            k_p, k_c = kp_ref[:, sl], kc_ref[:, sl]
            v_p, v_c = vp_ref[:, sl], vc_ref[:, sl]
            do_i, do_n = do_ref[:, sl], don_ref[:, sl]
            return (_dot(q_i, k_c, NT), _dot(do_i, v_c, NT), _dot(q_i, k_p, NT), _dot(do_i, v_p, NT),
                    _dot(q_n, k_c, NT), _dot(do_n, v_c, NT))

        ahead = [products(h) for h in range(min(PRODUCTS_AHEAD, H))]
        for h in range(H):
            sl = slice(h * HEAD_DIM, (h + 1) * HEAD_DIM)
            s1, dp1, s2, dp2, s3, dp3 = ahead.pop(0)
            if h + PRODUCTS_AHEAD < H:
                ahead.append(products(h + PRODUCTS_AHEAD))
            q_i, q_n = q_ref[:, sl], qn_ref[:, sl]
            k_p, k_c = kp_ref[:, sl], kc_ref[:, sl]
            do_i, do_n = do_ref[:, sl], don_ref[:, sl]
            l_i, l_n = _lane_col(l_ref[...], h, lane), _lane_col(ln_ref[...], h, lane)
            d_i, d_n = _lane_col(d_ref[...], h, lane), _lane_col(dn_ref[...], h, lane)
            b_c = b_ref[h, :, BLOCK:]
            b_p = b_ref[h, :, :BLOCK]
            p1 = jnp.exp(jnp.where(mask_c, s1 * SCALE + b_c, NEG) - l_i)
            ds1 = p1 * (dp1 - d_i)
            ds1b = ds1.astype(BF16)
            p1b = p1.astype(BF16)
            p2 = jnp.exp(jnp.where(mask_p, s2 * SCALE + b_p, NEG) - l_i)
            ds2 = p2 * (dp2 - d_i)
            ds2b = ds2.astype(BF16)
            p3 = jnp.exp(jnp.where(mask_n, s3 * SCALE + b_p, NEG) - l_n)
            ds3b = (p3 * (dp3 - d_n)).astype(BF16)
            p3b = p3.astype(BF16)
            dq = _dot(ds1b, k_c, NN) + _dot(ds2b, k_p, NN)
            dk = _dot(ds1b, q_i, TN) + _dot(ds3b, q_n, TN)
            dv = _dot(p1b, do_i, TN) + _dot(p3b, do_n, TN)
            dq_ref[:, sl] = (dq * SCALE).astype(BF16)
            dkv_ref[0, :, sl] = (dk * SCALE).astype(BF16)
            dkv_ref[1, :, sl] = dv.astype(BF16)
            db_ref[h, :, BLOCK:] += ds1
            db_ref[h, :, :BLOCK] += ds2

    def blk(slab, shift):
        if shift < 0:
            return pl.BlockSpec((None, None, BLOCK, A), lambda r, i: (slab, r, jnp.maximum(i - 1, 0), 0))
        return pl.BlockSpec((None, None, BLOCK, A), lambda r, i: (slab, r, i, 0))

    def row(width, shift):
        if shift > 0:
            return pl.BlockSpec((None, BLOCK, width), lambda r, i: (r, jnp.minimum(i + 1, nb - 1), 0))
        return pl.BlockSpec((None, BLOCK, width), lambda r, i: (r, i, 0))

    in_specs = [row(A, 0), row(A, 1), blk(0, -1), blk(0, 0), blk(1, -1), blk(1, 0),
                row(A, 0), row(A, 1), row(LANES, 0), row(LANES, 1), row(LANES, 0), row(LANES, 1),
                pl.BlockSpec((H, BLOCK, 2 * BLOCK), lambda r, i: (0, 0, 0))]
    dq, dkv, db = pl.pallas_call(
        body, name=name, grid=(dil, nb), in_specs=in_specs,
        out_specs=[pl.BlockSpec((None, BLOCK, A), lambda r, i: (r, i, 0)),
                   pl.BlockSpec((2, None, BLOCK, A), lambda r, i: (0, r, i, 0)),
                   pl.BlockSpec((H, BLOCK, 2 * BLOCK), lambda r, i: (0, 0, 0))],
        out_shape=[jax.ShapeDtypeStruct((dil, L, A), BF16), jax.ShapeDtypeStruct((2, dil, L, A), BF16),
                   jax.ShapeDtypeStruct((H, BLOCK, 2 * BLOCK), F32)],
        compiler_params=_params("arbitrary", "arbitrary"),
    )(qv, qv, kvv, kvv, kvv, kvv, dov, dov, lsev, lsev, dhv, dhv, bias)
    return dq.reshape(S, A), dkv.reshape(2, S, A), db


def _sum_leading(stack, out_dtype, name, tr=256, tc=2048):
    n, R, C = stack.shape
    tr = _tile(R, tr, 16)
    tc = _tile(C, tc)

    def body(s_ref, o_ref):
        acc = s_ref[0].astype(F32)
        for q in range(1, n):
            acc = acc + s_ref[q].astype(F32)
        o_ref[...] = acc.astype(out_dtype)

    return pl.pallas_call(
        body, name=name, grid=(R // tr, C // tc),
        in_specs=[pl.BlockSpec((n, tr, tc), lambda i, j: (0, i, j))],
        out_specs=pl.BlockSpec((tr, tc), lambda i, j: (i, j)),
        out_shape=jax.ShapeDtypeStruct((R, C), out_dtype),
        compiler_params=_params("parallel", "parallel"),
    )(stack)


def _add_half(g, t, c_idx, kind, name, after=None, tr=256, tc=2048):
    R, C = t.shape
    tr = _tile(R, tr, 16)
    tc = _tile(C, tc)
    nrb, ncb = R // tr, C // tc
    extra = _as_list(after)

    def body(c_ref, g_ref, t_ref, *rest):
        del c_ref
        o_ref = rest[len(extra)]
        o_ref[...] = (g_ref[...].astype(F32) + t_ref[...].astype(F32)).astype(BF16)

    if kind == "col":
        g_map = lambda i, j, c_ref: (c_ref[0] * nrb + i, j)
    else:
        g_map = lambda i, j, c_ref: (i, c_ref[0] * ncb + j)
    same = lambda i, j, c_ref: (i, j)
    return pl.pallas_call(
        body, name=name,
        grid_spec=pltpu.PrefetchScalarGridSpec(
            num_scalar_prefetch=1, grid=(nrb, ncb),
            in_specs=[pl.BlockSpec((tr, tc), g_map), pl.BlockSpec((tr, tc), same)] + [ANY] * len(extra),
            out_specs=pl.BlockSpec((tr, tc), same)),
        out_shape=jax.ShapeDtypeStruct((R, C), BF16),
        compiler_params=_params("parallel", "parallel"),
    )(c_idx, g, t, *extra)


def _cast_into_full(w, kind, chip_idx, name, tr=256, tc=2048):
    R, C = w.shape
    tr = _tile(R, tr, 16)
    tc = _tile(C, tc)
    nrb, ncb = R // tr, C // tc

    def body(k_ref, w_ref, o_ref):
        del k_ref
        o_ref[...] = w_ref[...].astype(BF16)

    if kind == "col":
        o_map = lambda i, j, k_ref: (i, k_ref[0] * ncb + j)
        full = (R, N_CHIPS * C)
    else:
        o_map = lambda i, j, k_ref: (k_ref[0] * nrb + i, j)
        full = (N_CHIPS * R, C)
    return pl.pallas_call(
        body, name=name,
        grid_spec=pltpu.PrefetchScalarGridSpec(
            num_scalar_prefetch=1, grid=(nrb, ncb),
            in_specs=[pl.BlockSpec((tr, tc), lambda i, j, k_ref: (i, j))],
            out_specs=pl.BlockSpec((tr, tc), o_map)),
        out_shape=jax.ShapeDtypeStruct(full, BF16),
        compiler_params=_params("parallel", "parallel"),
    )(chip_idx, w)


def _sum_into_shard(p, u, idx, kind, name, tr=256, tc=2048):
    _, R, C = u.shape
    tr = _tile(R, tr, 16)
    tc = _tile(C, tc)
    nrb, ncb = R // tr, C // tc

    def body(i_ref, p_ref, u_ref, o_ref):
        del i_ref
        acc = p_ref[...].astype(F32)
        for q in range(N_CHIPS - 1):
            acc = acc + u_ref[q].astype(F32)
        o_ref[...] = acc

    if kind == "col":
        p_map = lambda i, j, r: (i, r[0] * ncb + j)
        o_map = lambda i, j, r: (r[1] * nrb + i, j)
        full = (2 * R, C)
    else:
        p_map = lambda i, j, r: (r[0] * nrb + i, j)
        o_map = lambda i, j, r: (i, r[1] * ncb + j)
        full = (R, 2 * C)
    return pl.pallas_call(
        body, name=name,
        grid_spec=pltpu.PrefetchScalarGridSpec(
            num_scalar_prefetch=1, grid=(nrb, ncb),
            in_specs=[pl.BlockSpec((tr, tc), p_map), pl.BlockSpec((N_CHIPS - 1, tr, tc), lambda i, j, r: (0, i, j))],
            out_specs=pl.BlockSpec((tr, tc), o_map)),
        out_shape=jax.ShapeDtypeStruct(full, F32),
        compiler_params=_params("parallel", "parallel"),
    )(idx, p, u)


def _adamw(w, g, m, v, name, tr=256, tc=2048):
    R, C = w.shape
    tr = _tile(R, tr, 8)
    tc = _tile(C, tc)
    c1 = 1.0 - ADAM_B1 ** ADAM_STEP
    c2 = 1.0 - ADAM_B2 ** ADAM_STEP

    def body(w_ref, g_ref, m_ref, v_ref, d_ref, nm_ref, nv_ref):
        gv = g_ref[...]
        nm = ADAM_B1 * m_ref[...] + (1.0 - ADAM_B1) * gv
        nv = ADAM_B2 * v_ref[...] + (1.0 - ADAM_B2) * (gv * gv)
        d_ref[...] = -ADAM_LR * ((nm / c1) / (jnp.sqrt(nv / c2) + ADAM_EPS) + ADAM_WD * w_ref[...])
        nm_ref[...] = nm
        nv_ref[...] = nv

    blk = pl.BlockSpec((tr, tc), lambda i, j: (i, j))
    sh = jax.ShapeDtypeStruct((R, C), F32)
    return pl.pallas_call(
        body, name=name, grid=(R // tr, C // tc), in_specs=[blk] * 4, out_specs=[blk] * 3,
        out_shape=[sh, sh, sh], compiler_params=_params("parallel", "parallel"),
    )(w, g, m, v)


def _xyc():
    return lax.axis_index("x"), lax.axis_index("y"), lax.axis_index("c")


def _drain(copies):
    for cp in copies:
        if cp.is_remote:
            cp.wait_send()
        else:
            cp.wait()


def _other_chips(x, y):
    return [(1 - x, y), (x, 1 - y), (1 - x, 1 - y)]


def _allgather_small(blk, name, after=None):
    R, C = blk.shape
    extra = _as_list(after)

    def body(*refs):
        x_ref = refs[0]
        out_ref, send_sems, recv_sems, local_sem = refs[1 + len(extra):]
        x, y, c = _xyc()
        me = 4 * x + 2 * y + c
        mine = pltpu.make_async_copy(x_ref, out_ref.at[me], local_sem)
        mine.start()
        peers = []
        for k in range(1, N_DEV):
            px = 1 - x if (k >> 2) & 1 else x
            py = 1 - y if (k >> 1) & 1 else y
            pc = 1 - c if k & 1 else c
            peers.append((px, py, pc))
        sends = []
        for k, peer in enumerate(peers):
            cp = pltpu.make_async_remote_copy(
                src_ref=x_ref, dst_ref=out_ref.at[me], send_sem=send_sems.at[k], recv_sem=recv_sems.at[k],
                device_id=peer, device_id_type=MESH)
            cp.start()
            sends.append(cp)
        for k, (px, py, pc) in enumerate(peers):
            pltpu.make_async_remote_copy(
                src_ref=x_ref, dst_ref=out_ref.at[4 * px + 2 * py + pc], send_sem=send_sems.at[k],
                recv_sem=recv_sems.at[k], device_id=(px, py, pc), device_id_type=MESH).wait_recv()
        for cp in sends:
            cp.wait_send()
        mine.wait()

    return pl.pallas_call(
        body, name=name, in_specs=[VMEM_SPEC] + [ANY] * len(extra), out_specs=VMEM_SPEC,
        out_shape=jax.ShapeDtypeStruct((N_DEV, R, C), blk.dtype),
        scratch_shapes=[pltpu.SemaphoreType.DMA((N_DEV - 1,)), pltpu.SemaphoreType.DMA((N_DEV - 1,)),
                        pltpu.SemaphoreType.DMA],
        compiler_params=pltpu.CompilerParams(vmem_limit_bytes=VMEM_LIMIT),
    )(blk, *extra)


def _full_region(ref, kind, chip, half, shard_shape):
    r, cn = shard_shape
    hr = r // 2
    if kind == "col":
        rows = pl.ds(0, r) if half is None else pl.ds(pl.multiple_of(half * hr, 16), hr)
        return ref.at[rows, pl.ds(pl.multiple_of(chip * cn, LANES), cn)]
    if half is None:
        return ref.at[pl.ds(pl.multiple_of(chip * r, 16), r), :]
    return ref.at[pl.ds(pl.multiple_of(chip * r + half * hr, 16), hr), :]


def _allgather_weights(fulls, kinds, shapes, name):
    n = len(fulls)

    def body(*refs):
        outs = refs[n:2 * n]
        send_sems, recv_sems = refs[2 * n:]
        x, y, c = _xyc()
        chip = 2 * x + y
        sib = (x, y, 1 - c)
        others = _other_chips(x, y)
        started = []
        for w in range(n):
            mine = _full_region(outs[w], kinds[w], chip, c, shapes[w])
            for j, (ox, oy) in enumerate(others):
                cp = pltpu.make_async_remote_copy(
                    src_ref=mine, dst_ref=mine, send_sem=send_sems.at[6 * w + j], recv_sem=recv_sems.at[6 * w + j],
                    device_id=(ox, oy, c), device_id_type=MESH)
                cp.start()
                started.append(cp)
        for w in range(n):
            for j, (ox, oy) in enumerate(others):
                landed = _full_region(outs[w], kinds[w], 2 * ox + oy, c, shapes[w])
                pltpu.make_async_remote_copy(
                    src_ref=landed, dst_ref=landed, send_sem=send_sems.at[6 * w + j], recv_sem=recv_sems.at[6 * w + j],
                    device_id=(ox, oy, c), device_id_type=MESH).wait_recv()
                cp = pltpu.make_async_remote_copy(
                    src_ref=landed, dst_ref=landed, send_sem=send_sems.at[6 * w + 3 + j],
                    recv_sem=recv_sems.at[6 * w + 3 + j], device_id=sib, device_id_type=MESH)
                cp.start()
                started.append(cp)
        for w in range(n):
            for j, (ox, oy) in enumerate(others):
                theirs = _full_region(outs[w], kinds[w], 2 * ox + oy, 1 - c, shapes[w])
                pltpu.make_async_remote_copy(
                    src_ref=theirs, dst_ref=theirs, send_sem=send_sems.at[6 * w + 3 + j],
                    recv_sem=recv_sems.at[6 * w + 3 + j], device_id=sib, device_id_type=MESH).wait_recv()
        _drain(started)

    return pl.pallas_call(
        body, name=name, in_specs=[ANY] * n, out_specs=[ANY] * n,
        out_shape=[jax.ShapeDtypeStruct(f.shape, f.dtype) for f in fulls],
        input_output_aliases={w: w for w in range(n)},
        scratch_shapes=[pltpu.SemaphoreType.DMA((6 * n,)), pltpu.SemaphoreType.DMA((6 * n,))],
    )(*fulls)


def _region_of_size(ref, kind, shard_shape, count):
    r, cn = shard_shape
    if kind == "col":
        return ref.at[pl.ds(0, r // 2), pl.ds(0, count * cn)]
    return ref.at[pl.ds(0, count * (r // 2)), :]


def _allgather_weights_seq(fulls, kinds, shapes, name, collective_id):
    n = len(fulls)
    refs = [jax.new_ref(f, memory_space=pltpu.MemorySpace.HBM) for f in fulls]

    def body(send_sems, recv_sems):
        x, y, c = _xyc()
        chip = 2 * x + y
        sib = (x, y, 1 - c)
        others = _other_chips(x, y)
        peers = [(ox, oy, c) for ox, oy in others] + [sib]
        barrier = pltpu.get_barrier_semaphore()
        for peer in peers:
            pl.semaphore_signal(barrier, inc=1, device_id=peer, device_id_type=MESH)
        pl.semaphore_wait(barrier, len(peers))

        def copy(w, region, sem, to):
            return pltpu.make_async_remote_copy(src_ref=region, dst_ref=region, send_sem=send_sems.at[sem],
                                                recv_sem=recv_sems.at[sem], device_id=to, device_id_type=MESH)

        for w in range(n):
            mine = _full_region(refs[w], kinds[w], chip, c, shapes[w])
            for ox, oy in others:
                copy(w, mine, 2 * w, (ox, oy, c)).start()
        for w in range(n):
            three = _region_of_size(refs[w], kinds[w], shapes[w], 3)
            copy(w, three, 2 * w, sib).wait_recv()
            for ox, oy in others:
                copy(w, _full_region(refs[w], kinds[w], 2 * ox + oy, c, shapes[w]), 2 * w + 1, sib).start()
        for w in range(n):
            three = _region_of_size(refs[w], kinds[w], shapes[w], 3)
            copy(w, three, 2 * w + 1, sib).wait_recv()
            copy(w, three, 2 * w, sib).wait_send()
            copy(w, three, 2 * w + 1, sib).wait_send()

    pl.kernel(
        body, out_type=(), mesh=plsc.ScalarSubcoreMesh(axis_name="seq", num_cores=1), name=name,
        scratch_types=[pltpu.SemaphoreType.DMA((2 * n,)), pltpu.SemaphoreType.DMA((2 * n,))],
        compiler_params=pltpu.CompilerParams(collective_id=collective_id),
    )()
    return [r[...] for r in refs]


def _half_of(ref, kind, half):
    r, cn = ref.shape
    if kind == "col":
        return ref.at[pl.ds(pl.multiple_of(half * (r // 2), 16), r // 2), :]
    return ref.at[:, pl.ds(pl.multiple_of(half * (cn // 2), LANES), cn // 2)]


def _shard_of(ref, kind, chip):
    r, cn = ref.shape
    if kind == "col":
        return ref.at[:, pl.ds(pl.multiple_of(chip * (cn // N_CHIPS), LANES), cn // N_CHIPS)]
    return ref.at[pl.ds(pl.multiple_of(chip * (r // N_CHIPS), 16), r // N_CHIPS), :]


def _exchange_halves(grads, kinds, name):
    n = len(grads)

    def body(*refs):
        gs = refs[:n]
        ts = refs[n:2 * n]
        send_sems, recv_sems = refs[2 * n:]
        x, y, c = _xyc()
        cps = []
        for w in range(n):
            cp = pltpu.make_async_remote_copy(
                src_ref=_half_of(gs[w], kinds[w], 1 - c), dst_ref=ts[w], send_sem=send_sems.at[w],
                recv_sem=recv_sems.at[w], device_id=(x, y, 1 - c), device_id_type=MESH)
            cp.start()
            cps.append(cp)
        for cp in cps:
            cp.wait()

    out_shape = []
    for gr, kind in zip(grads, kinds):
        r, cn = gr.shape
        out_shape.append(jax.ShapeDtypeStruct((r // 2, cn) if kind == "col" else (r, cn // 2), gr.dtype))
    return pl.pallas_call(
        body, name=name, in_specs=[ANY] * n, out_specs=[ANY] * n, out_shape=out_shape,
        scratch_shapes=[pltpu.SemaphoreType.DMA((n,)), pltpu.SemaphoreType.DMA((n,))],
    )(*grads)


def _exchange_halves_seq(grads, kinds, name, collective_id):
    n = len(grads)

    def body(*refs):
        gs = refs[:n]
        ts = refs[n:2 * n]
        send_sems, recv_sems = refs[2 * n:]
        x, y, c = _xyc()
        sib = (x, y, 1 - c)
        barrier = pltpu.get_barrier_semaphore()
        pl.semaphore_signal(barrier, inc=1, device_id=sib, device_id_type=MESH)
        pl.semaphore_wait(barrier, 1)
        cps = []
        for w in range(n):
            cp = pltpu.make_async_remote_copy(
                src_ref=_half_of(gs[w], kinds[w], 1 - c), dst_ref=ts[w], send_sem=send_sems.at[w],
                recv_sem=recv_sems.at[w], device_id=sib, device_id_type=MESH)
            cp.start()
            cps.append(cp)
        for cp in cps:
            cp.wait()

    out_type = []
    for gr, kind in zip(grads, kinds):
        r, cn = gr.shape
        out_type.append(jax.ShapeDtypeStruct((r // 2, cn) if kind == "col" else (r, cn // 2), gr.dtype))
    return pl.kernel(
        body, out_type=out_type, mesh=plsc.ScalarSubcoreMesh(axis_name="seq", num_cores=1), name=name,
        scratch_types=[pltpu.SemaphoreType.DMA((n,)), pltpu.SemaphoreType.DMA((n,))],
        compiler_params=pltpu.CompilerParams(collective_id=collective_id),
    )(*grads)


def _scatter_partials(parts, kinds, name):
    n = len(parts)

    def body(*refs):
        ps = refs[:n]
        us = refs[n:2 * n]
        send_sems, recv_sems = refs[2 * n:]
        x, y, c = _xyc()
        others = _other_chips(x, y)
        cps = []
        for w in range(n):
            for j, (ox, oy) in enumerate(others):
                cp = pltpu.make_async_remote_copy(
                    src_ref=_shard_of(ps[w], kinds[w], 2 * ox + oy), dst_ref=us[w].at[j],
                    send_sem=send_sems.at[3 * w + j], recv_sem=recv_sems.at[3 * w + j],
                    device_id=(ox, oy, c), device_id_type=MESH)
                cp.start()
                cps.append(cp)
        for cp in cps:
            cp.wait()

    out_shape = []
    for p, kind in zip(parts, kinds):
        r, cn = p.shape
        hs = (r, cn // N_CHIPS) if kind == "col" else (r // N_CHIPS, cn)
        out_shape.append(jax.ShapeDtypeStruct((N_CHIPS - 1,) + hs, p.dtype))
    return pl.pallas_call(
        body, name=name, in_specs=[ANY] * n, out_specs=[ANY] * n, out_shape=out_shape,
        scratch_shapes=[pltpu.SemaphoreType.DMA((3 * n,)), pltpu.SemaphoreType.DMA((3 * n,))],
    )(*parts)


def _scatter_partials_seq(parts, kinds, name, collective_id):
    n = len(parts)

    def body(*refs):
        ps = refs[:n]
        us = refs[n:2 * n]
        send_sems, recv_sems = refs[2 * n:]
        x, y, c = _xyc()
        others = _other_chips(x, y)
        barrier = pltpu.get_barrier_semaphore()
        for ox, oy in others:
            pl.semaphore_signal(barrier, inc=1, device_id=(ox, oy, c), device_id_type=MESH)
        pl.semaphore_wait(barrier, len(others))
        for w in range(n):
            for j, (ox, oy) in enumerate(others):
                pltpu.make_async_remote_copy(
                    src_ref=_shard_of(ps[w], kinds[w], 2 * ox + oy), dst_ref=us[w].at[j],
                    send_sem=send_sems.at[w], recv_sem=recv_sems.at[w],
                    device_id=(ox, oy, c), device_id_type=MESH).start()
        for w in range(n):
            pltpu.make_async_remote_copy(
                src_ref=us[w], dst_ref=us[w], send_sem=send_sems.at[w], recv_sem=recv_sems.at[w],
                device_id=(x, y, c), device_id_type=MESH).wait()

    out_type = []
    for p, kind in zip(parts, kinds):
        r, cn = p.shape
        hs = (r, cn // N_CHIPS) if kind == "col" else (r // N_CHIPS, cn)
        out_type.append(jax.ShapeDtypeStruct((N_CHIPS - 1,) + hs, p.dtype))
    return pl.kernel(
        body, out_type=out_type, mesh=plsc.ScalarSubcoreMesh(axis_name="seq", num_cores=1), name=name,
        scratch_types=[pltpu.SemaphoreType.DMA((n,)), pltpu.SemaphoreType.DMA((n,))],
        compiler_params=pltpu.CompilerParams(collective_id=collective_id),
    )(*parts)


def _join_halves(halves, kinds, name):
    n = len(halves)

    def body(*refs):
        outs = refs[n:2 * n]
        send_sems, recv_sems = refs[2 * n:]
        x, y, c = _xyc()
        cps = []
        for w in range(n):
            mine = _half_of(outs[w], kinds[w], c)
            cp = pltpu.make_async_remote_copy(
                src_ref=mine, dst_ref=mine, send_sem=send_sems.at[w], recv_sem=recv_sems.at[w],
                device_id=(x, y, 1 - c), device_id_type=MESH)
            cp.start()
            cps.append(cp)
        for w in range(n):
            theirs = _half_of(outs[w], kinds[w], 1 - c)
            pltpu.make_async_remote_copy(
                src_ref=theirs, dst_ref=theirs, send_sem=send_sems.at[w], recv_sem=recv_sems.at[w],
                device_id=(x, y, 1 - c), device_id_type=MESH).wait_recv()
        _drain(cps)

    return pl.pallas_call(
        body, name=name, in_specs=[ANY] * n, out_specs=[ANY] * n,
        out_shape=[jax.ShapeDtypeStruct(h.shape, h.dtype) for h in halves],
        input_output_aliases={w: w for w in range(n)},
        scratch_shapes=[pltpu.SemaphoreType.DMA((n,)), pltpu.SemaphoreType.DMA((n,))],
    )(*halves)


def kernel(x, a_norm, a_w_in, a_conv_w, a_conv_b, a_ln_g, a_ln_b, a_w_out, kv_norm, w_kv, b_norm, b_w_in, b_w_out, rel_bias, final_norm, loss_target, m_a_norm, m_a_w_in, m_a_conv_w, m_a_conv_b, m_a_ln_g, m_a_ln_b, m_a_w_out, m_kv_norm, m_w_kv, m_b_norm, m_b_w_in, m_b_w_out, m_rel_bias, m_final_norm, v_a_norm, v_a_w_in, v_a_conv_w, v_a_conv_b, v_a_ln_g, v_a_ln_b, v_a_w_out, v_kv_norm, v_w_kv, v_b_norm, v_b_w_in, v_b_w_out, v_rel_bias, v_final_norm):
    S, D = x.shape[1], x.shape[2]
    E = a_w_out.shape[1] * N_CHIPS
    A = b_w_out.shape[1] * N_CHIPS
    H = A // HEAD_DIM
    DC = D // N_CHIPS
    xs = x.reshape(S, D)
    tgt = loss_target.reshape(S, D)
    cx, cy, cc = _xyc()
    chip = 2 * cx + cy
    c_idx = jnp.reshape(cc, (1,)).astype(jnp.int32)

    big_names = ["a_w_in", "a_w_out", "w_kv", "b_w_in", "b_w_out"]
    kinds = ["col", "row", "col", "col", "row"]
    big_w = [a_w_in[0], a_w_out[0], w_kv, b_w_in[0], b_w_out[0]]
    big_m = [m_a_w_in[0], m_a_w_out[0], m_w_kv, m_b_w_in[0], m_b_w_out[0]]
    big_v = [v_a_w_in[0], v_a_w_out[0], v_w_kv, v_b_w_in[0], v_b_w_out[0]]
    chip_idx = jnp.reshape(chip, (1,)).astype(jnp.int32)
    placed = [_cast_into_full(big_w[w], kinds[w], chip_idx, "cast_" + big_names[w]) for w in range(5)]
    shard_shapes = [w.shape for w in big_w]
    (wa_in,) = _allgather_weights_seq(placed[0:1], kinds[0:1], shard_shapes[0:1], "ag_seq_a_in", 0)
    (wa_out,) = _allgather_weights_seq(placed[1:2], kinds[1:2], shard_shapes[1:2], "ag_seq_a_out", 1)
    (wkv,) = _allgather_weights_seq(placed[2:3], kinds[2:3], shard_shapes[2:3], "ag_seq_kv", 5)
    wb_in, wb_out = _allgather_weights_seq(placed[3:5], kinds[3:5], shard_shapes[3:5], "ag_seq_b", 2)

    def row_at(vec, q):
        return jnp.pad(vec, ((q, 7 - q), (0, 0)))

    def pack_sharded(an, cw, cb, lg, lb):
        return jnp.concatenate([row_at(an, 0), jnp.pad(cw[0], ((0, 1), (0, 0))),
                                row_at(lg, 0) + row_at(lb, 1) + row_at(cb, 2)], axis=0)

    small_w = pack_sharded(a_norm, a_conv_w, a_conv_b, a_ln_g, a_ln_b)
    gathered = _allgather_small(small_w, "ag_small_params")
    small_full = jnp.concatenate([gathered[2 * k] for k in range(N_CHIPS)], axis=1)
    g_a = small_full[0:1]
    conv_w32 = small_full[8:8 + HALO]
    ln_g = small_full[40:41]
    ln_b = small_full[41:42]
    conv_b = small_full[42:43]
    g_kv = kv_norm.reshape(1, D)
    g_b = b_norm.reshape(1, D)
    g_f = final_norm.reshape(1, D)

    rb_t = jnp.pad(rel_bias.T, ((0, 0), (0, LANES - N_BUCKETS)))
    onehots = [_onehot(dil) for _, dil in GROUPS]
    biases = [_bias_table(rb_t, onehots[g], "bias_table_%d" % g).reshape(H, BLOCK, 2 * BLOCK)
              for g in range(len(GROUPS))]

    dils = tuple(dil for _, dil in GROUPS)
    assert dils[0] == 1
    n_g = len(GROUPS)
    ((h0,),) = _rms_fwd(xs, [g_a], (1,), "rms_a")
    proj3 = _matmul(h0, wa_in, "nn", BF16, "mm_a_in", out_slab=E)
    conv = _conv_fwd(proj3, conv_w32, conv_b, "conv_fwd")
    y_a = _ln_gate_fwd(conv, proj3, ln_g, ln_b, "ln_gate_fwd")
    x1 = _matmul(y_a, wa_out, "nn", F32, "mm_a_out", res=xs)
    hks, hbs = _rms_fwd(x1, [g_kv, g_b], dils, "rms_kv_b")
    kvs = [_matmul(hks[g], wkv, "nn", BF16, "mm_kv_%d" % g, out_slab=A, b_off=2 * g * A, n_cols=2 * A)
           for g in range(n_g)]
    qs = [_matmul(hbs[g], wb_in, "nn", BF16, "mm_q_%d" % g, b_off=g * A, n_cols=A, after=kvs[-1])
          for g in range(n_g)]
    zb = _matmul(hbs[0], wb_in, "nn", BF16, "mm_zb", b_off=n_g * A, n_cols=A, after=kvs[-1])
    os_, lses = [], []
    for g, dil in enumerate(dils):
        o_g, lse_g = _attn_fwd(qs[g], kvs[g], biases[g], dil, "attn_fwd_%d" % g)
        os_.append(o_g)
        lses.append(lse_g)
    y_b, o_m, lse_d = _attn_merge(os_, lses, zb, dils, "attn_merge")
    x2 = _matmul(y_b, wb_out, "nn", F32, "mm_b_out", res=x1)
    loss_part, dx2, dx2b, gg_f = _final_head(x2, g_f, tgt, "final_head")
    loss = lax.psum(loss_part[0, 0], ("x", "y", "c"))

    dw_tiles = dict(tm=512, tn=1024, tk=4096)
    dy_b = _matmul(dx2b, wb_out, "nt", BF16, "mm_b_out_dx", after=loss.reshape(1, 1))
    dwb_out = _matmul(y_b, dx2b, "tn", BF16, "mm_b_out_dw", **dw_tiles)
    dos, dhs, dzb = _gate_bwd(dy_b, o_m, zb, dils, "gate_bwd")
    dbs, cots = [], []
    dwb_in = dwkv = None
    for g, dil in enumerate(dils):
        dq, dkv, db = _attn_bwd(qs[g], kvs[g], dos[g], lse_d[g], dhs[g], biases[g], dil, "attn_bwd_%d" % g)
        dbs.append(db.reshape(H, BLOCK * 2 * BLOCK))
        dwb_in = _matmul(hbs[g], dq, "tn", BF16, "mm_q_dw_%d" % g, out_off=g * A, out_cols=(n_g + 1) * A,
                         out_alias=dwb_in, **dw_tiles)
        dwkv = _matmul(hks[g], dkv, "tn", BF16, "mm_kv_dw_%d" % g, b_slab=True, out_off=2 * g * A,
                       out_cols=2 * n_g * A, out_alias=dwkv, **dw_tiles)
        cots.append((_matmul(dkv, wkv, "nt", BF16, "mm_kv_dx_%d" % g, a_slab=True, b_off=2 * g * A), 0, dil))
        cots.append((_matmul(dq, wb_in, "nt", BF16, "mm_q_dx_%d" % g, b_off=g * A), 1, dil))
    dwb_in = _matmul(hbs[0], dzb, "tn", BF16, "mm_zb_dw", out_off=n_g * A, out_cols=(n_g + 1) * A,
                     out_alias=dwb_in, **dw_tiles)
    cots.append((_matmul(dzb, wb_in, "nt", BF16, "mm_zb_dx", b_off=n_g * A), 1, 1))
    chip_c = jnp.stack([chip, cc]).astype(jnp.int32)

    def scatter_group(idx, grads, tag, collective_id, exchange_id=None, behind=None):
        ks = [kinds[w] for w in idx]
        if exchange_id is None:
            theirs = _exchange_halves(grads, ks, "rs_exchange_" + tag)
        else:
            theirs = _exchange_halves_seq(grads, ks, "rs_exchange_seq_" + tag, exchange_id)
        parts = [_add_half(grads[q], theirs[q], c_idx, ks[q], "rs_add_half_%d" % w, after=behind)
                 for q, w in enumerate(idx)]
        return parts, _scatter_partials_seq(parts, ks, "rs_seq_" + tag, collective_id)

    def reduce_group(idx, parts, slots, tag):
        ks = [kinds[w] for w in idx]
        halves = [_sum_into_shard(parts[q], slots[q], chip_c, ks[q], "rs_sum_chips_%d" % w)
                  for q, w in enumerate(idx)]
        return _join_halves(halves, ks, "rs_join_" + tag)

    parts_b, slots_b = scatter_group([2, 3, 4], [dwkv, dwb_in, dwb_out], "b", 3, exchange_id=6,
                                     behind=[ct[0] for ct in cots])
    g_rel_t = _bias_grad(dbs, onehots, "bias_grad")
    dx1, dx1b, gg_kvb = _rms_bwd(x1, cots, [g_kv, g_b], dx2, "rms_kv_b_bwd", after=parts_b)
    dy_a = _matmul(dx1b, wa_out, "nt", BF16, "mm_a_out_dx")
    dwa_out = _matmul(y_a, dx1b, "tn", BF16, "mm_a_out_dw", **dw_tiles)
    dconv, dproj3, gg_ln = _ln_gate_bwd(conv, proj3, dy_a, ln_g, ln_b, "ln_gate_bwd")
    dproj3, g_conv_w = _conv_bwd(proj3, dconv, conv_w32, dproj3, "conv_bwd")
    dwa_in = _matmul(h0, dproj3, "tn", BF16, "mm_a_in_dw", b_slab=True, **dw_tiles)
    parts_a, slots_a = scatter_group([0, 1], [dwa_in, dwa_out], "a", 4)
    dh0 = _matmul(dproj3, wa_in, "nt", BF16, "mm_a_in_dx", a_slab=True, after=parts_a, tn=512)
    grad_x, _, gg_a = _rms_bwd(xs, [(dh0, 0, 1)], [g_a], dx1, "rms_a_bwd")

    big_g = [None] * 5
    big_g[2:5] = reduce_group([2, 3, 4], parts_b, slots_b, "b")
    big_g[0:2] = reduce_group([0, 1], parts_a, slots_a, "a")

    def rel_rows(rb):
        return jnp.pad(rb.reshape(1, N_BUCKETS * H), ((0, 7), (0, D - N_BUCKETS * H)))

    small_g = jnp.concatenate([gg_a, g_conv_w, gg_ln, gg_kvb, gg_f, rel_rows(g_rel_t[:, :N_BUCKETS].T)], axis=0)
    small_sum = _sum_leading(_allgather_small(small_g, "ag_small_grads", after=[slots_a[0], slots_b[0]]), F32,
                             "sum_small_grads", tr=72)
    g_sharded = lax.dynamic_slice(small_sum, (0, chip * DC), (48, DC))
    g_repl = small_sum[48:72]

    outs_g, outs_d, outs_m, outs_v = {}, {}, {}, {}
    for w, nm in enumerate(big_names):
        d_, m_, v_ = _adamw(big_w[w], big_g[w], big_m[w], big_v[w], "adamw_" + nm)
        outs_g[nm], outs_d[nm], outs_m[nm], outs_v[nm] = big_g[w], d_, m_, v_
    sm_m = pack_sharded(m_a_norm, m_a_conv_w, m_a_conv_b, m_a_ln_g, m_a_ln_b)
    sm_v = pack_sharded(v_a_norm, v_a_conv_w, v_a_conv_b, v_a_ln_g, v_a_ln_b)
    sd, smm, svv = _adamw(small_w, g_sharded, sm_m, sm_v, "adamw_small_sharded")

    def unpack_sharded(p):
        return {"a_norm": p[0:1], "a_conv_w": p[8:8 + CONV_TAPS].reshape(1, CONV_TAPS, DC), "a_ln_g": p[40:41],
                "a_ln_b": p[41:42], "a_conv_b": p[42:43]}

    for src, dst in ((g_sharded, outs_g), (sd, outs_d), (smm, outs_m), (svv, outs_v)):
        dst.update(unpack_sharded(src))

    def pack_repl(kn, bn, fn, rb):
        return jnp.concatenate([row_at(kn.reshape(1, D), 0) + row_at(bn.reshape(1, D), 1),
                                row_at(fn.reshape(1, D), 0), rel_rows(rb)], axis=0)

    rp_w = pack_repl(kv_norm, b_norm, final_norm, rel_bias)
    rp_m = pack_repl(m_kv_norm, m_b_norm, m_final_norm, m_rel_bias)
    rp_v = pack_repl(v_kv_norm, v_b_norm, v_final_norm, v_rel_bias)
    rd, rmm, rvv = _adamw(rp_w, g_repl, rp_m, rp_v, "adamw_small_replicated")

    def unpack_repl(p):
        return {"kv_norm": p[0], "b_norm": p[1:2], "final_norm": p[8],
                "rel_bias": p[16, :N_BUCKETS * H].reshape(N_BUCKETS, H)}

    for src, dst in ((g_repl, outs_g), (rd, outs_d), (rmm, outs_m), (rvv, outs_v)):
        dst.update(unpack_repl(src))

    order = ["a_norm", "a_w_in", "a_conv_w", "a_conv_b", "a_ln_g", "a_ln_b", "a_w_out", "kv_norm", "w_kv",
             "b_norm", "b_w_in", "b_w_out", "rel_bias", "final_norm"]
    lead = {"a_w_in", "a_w_out", "b_w_in", "b_w_out"}

    def shaped(nm, val):
        return val[None] if nm in lead else val

    result = [loss, grad_x.reshape(1, S, D)]
    for table in (outs_g, outs_d, outs_m, outs_v):
        result.extend(shaped(nm, table[nm]) for nm in order)
    return tuple(result)
```

```python
import functools

import numpy as np
import jax
import jax.numpy as jnp
from jax import lax
from jax.experimental import pallas as pl
from jax.experimental.pallas import tpu as pltpu
from jax.experimental.pallas import tpu_sc as plsc

F32 = jnp.float32
BF16 = jnp.bfloat16
MESH = pl.DeviceIdType.MESH
ANY = pl.BlockSpec(memory_space=pl.ANY)
VMEM_SPEC = pl.BlockSpec(memory_space=pltpu.VMEM)

EPS = 1e-6
HEAD_DIM = 128
BLOCK = 128
GROUPS = ((128, 1), (512, 4), (2048, 16))
SCALE = HEAD_DIM ** -0.5
CONV_TAPS = 31
HALO = 32
N_BUCKETS = 32
MAX_EXACT = 16
MAX_DISTANCE = 2048
NEG = -1e30
PRODUCTS_AHEAD = 2
SCORES_AHEAD = 4
N_CHIPS = 4
N_DEV = 8
LANES = 128
VMEM_LIMIT = 56 * 1024 * 1024

ADAM_LR = 0.001
ADAM_B1 = 0.9
ADAM_B2 = 0.999
ADAM_EPS = 1e-08
ADAM_WD = 0.01
ADAM_STEP = 10


def _tile(n, pref, mult=LANES):
    t = (min(pref, n) // mult) * mult
    while t >= mult:
        if n % t == 0:
            return t
        t -= mult
    return n


def _params(*sem):
    return pltpu.CompilerParams(dimension_semantics=sem, vmem_limit_bytes=VMEM_LIMIT)


def _sigmoid(v):
    return 1.0 / (1.0 + jnp.exp(-v))


def _dot(a, b, dims):
    return lax.dot_general(a, b, (dims, ((), ())), preferred_element_type=F32)


NN = ((1,), (0,))
NT = ((1,), (1,))
TN = ((0,), (0,))


def _as_list(after):
    if after is None:
        return []
    return list(after) if isinstance(after, (list, tuple)) else [after]


def _stack_rows(rows, total):
    width = rows[0].shape[1]
    rid = lax.broadcasted_iota(jnp.int32, (total, width), 0)
    out = jnp.zeros((total, width), F32)
    for q, row in enumerate(rows):
        out = jnp.where(rid == q, jnp.broadcast_to(row, (total, width)), out)
    return out


def _lane_col(arr, h, lane):
    return jnp.sum(jnp.where(lane == h, arr, 0.0), axis=-1, keepdims=True)


def _matmul(a, b, mode, out_dtype, name, res=None, a_slab=False, b_slab=False, out_slab=0,
            b_off=0, n_cols=None, out_off=0, out_cols=None, out_alias=None, after=None,
            tm=1024, tn=1024, tk=2048):
    if a_slab:
        na, M, W = a.shape
        K = na * W
    elif mode == "tn":
        K, M = a.shape
    else:
        M, K = a.shape
    if b_slab:
        nbs, _, Wb = b.shape
        N = nbs * Wb
    elif mode == "nt":
        N = b.shape[0]
    else:
        N = n_cols if n_cols else b.shape[1]
    tm = _tile(M, tm)
    tn = _tile(Wb if b_slab else (out_slab if out_slab else N), tn)
    tk = _tile(W if a_slab else K, tk)
    all_slabs = a_slab and mode == "nt" and tk == W
    if all_slabs:
        tk = K
    nk = K // tk
    grid = (M // tm, N // tn, nk)
    bo = b_off // (tk if mode == "nt" else tn)
    oo = out_off // tn

    if all_slabs:
        a_spec = pl.BlockSpec((na, tm, W), lambda i, j, k: (0, i, 0))
    elif a_slab:
        per = W // tk
        a_spec = pl.BlockSpec((None, tm, tk), lambda i, j, k: (k // per, i, k % per))
    elif mode == "tn":
        a_spec = pl.BlockSpec((tk, tm), lambda i, j, k: (k, i))
    else:
        a_spec = pl.BlockSpec((tm, tk), lambda i, j, k: (i, k))
    if b_slab:
        perb = Wb // tn
        b_spec = pl.BlockSpec((None, tk, tn), lambda i, j, k: (j // perb, k, j % perb))
    elif mode == "nt":
        b_spec = pl.BlockSpec((tn, tk), lambda i, j, k: (j, k + bo))
    else:
        b_spec = pl.BlockSpec((tk, tn), lambda i, j, k: (k, j + bo))
    if out_slab:
        pero = out_slab // tn
        o_spec = pl.BlockSpec((None, tm, tn), lambda i, j, k: (j // pero, i, j % pero))
        out_shape = jax.ShapeDtypeStruct((N // out_slab, M, out_slab), out_dtype)
    else:
        o_spec = pl.BlockSpec((tm, tn), lambda i, j, k: (i, j + oo))
        out_shape = jax.ShapeDtypeStruct((M, out_cols if out_cols else N), out_dtype)
    in_specs = [a_spec, b_spec]
    operands = [a, b]
    if res is not None:
        in_specs.append(pl.BlockSpec((tm, tn), lambda i, j, k: (i, j)))
        operands.append(res)
    aliases = {}
    if out_alias is not None:
        aliases[len(operands)] = 0
        in_specs.append(ANY)
        operands.append(out_alias)
    for arr in _as_list(after):
        in_specs.append(ANY)
        operands.append(arr)
    dims = {"nn": NN, "nt": NT, "tn": TN}[mode]
    has_res = res is not None
    n_in = len(operands)

    def body(*refs):
        a_ref, b_ref = refs[0], refs[1]
        r_ref = refs[2] if has_res else None
        o_ref = refs[n_in]
        if all_slabs:
            prod = _dot(a_ref[0], b_ref[:, 0:W], dims)
            for q in range(1, na):
                prod = prod + _dot(a_ref[q], b_ref[:, q * W:(q + 1) * W], dims)
        else:
            prod = _dot(a_ref[...], b_ref[...], dims)

        def finish(val):
            if has_res:
                val = val + r_ref[...]
            o_ref[...] = val.astype(out_dtype)

        if nk == 1:
            finish(prod)
        else:
            acc_ref = refs[n_in + 1]
            k = pl.program_id(2)

            @pl.when(k == 0)
            def _():
                acc_ref[...] = prod

            @pl.when(k > 0)
            def _():
                acc_ref[...] += prod

            @pl.when(k == nk - 1)
            def _():
                finish(acc_ref[...])

    scratch = [pltpu.VMEM((tm, tn), F32)] if nk > 1 else []
    return pl.pallas_call(
        body, name=name, grid=grid, in_specs=in_specs, out_specs=o_spec, out_shape=out_shape,
        scratch_shapes=scratch, input_output_aliases=aliases,
        compiler_params=_params("parallel", "parallel", "arbitrary"),
    )(*operands)


def _group_spec(d, ts, width):
    if d == 1:
        return pl.BlockSpec((ts, width), lambda i: (i, 0))
    return pl.BlockSpec((d, ts // d, width), lambda i: (0, i, 0))


def _group_shape(d, S, width, dtype):
    return jax.ShapeDtypeStruct((S, width) if d == 1 else (d, S // d, width), dtype)


def _chunk_buf(ts, width):
    return pltpu.VMEM((width // LANES, ts, LANES), F32)


def _fill_chunks(buf, val):
    for c in range(buf.shape[0]):
        buf[c] = val[:, c * LANES:(c + 1) * LANES]


def _read_chunks(buf):
    return jnp.concatenate([buf[c] for c in range(buf.shape[0])], axis=1)


def _emit_group_order(o_ref, buf, d, dtype):
    n = buf.shape[1] // d
    for r in range(d):
        for c in range(buf.shape[0]):
            o_ref[r, :, c * LANES:(c + 1) * LANES] = buf[c, pl.ds(r, n, stride=d), :].astype(dtype)


def _store_token_order(buf, i_ref, d):
    n = buf.shape[1] // d
    for r in range(d):
        for c in range(buf.shape[0]):
            buf[c, pl.ds(r, n, stride=d), :] = i_ref[r, :, c * LANES:(c + 1) * LANES].astype(F32)


def _rms_fwd(x, gains, dils, name, ts=256):
    S, D = x.shape
    ts = _tile(S, ts, 16 * max(dils))
    n = len(gains)
    nd = len(dils)

    def body(*refs):
        buf = refs[1 + n + n * nd]
        xv = refs[0][...]
        nrm = xv * lax.rsqrt(jnp.mean(xv * xv, axis=-1, keepdims=True) + EPS)
        for q in range(n):
            val = nrm * refs[1 + q][...]
            if max(dils) > 1:
                _fill_chunks(buf, val)
            for e, d in enumerate(dils):
                if d == 1:
                    refs[1 + n + q * nd + e][...] = val.astype(BF16)
                else:
                    _emit_group_order(refs[1 + n + q * nd + e], buf, d, BF16)

    row = pl.BlockSpec((ts, D), lambda i: (i, 0))
    vec = pl.BlockSpec((1, D), lambda i: (0, 0))
    outs = pl.pallas_call(
        body, name=name, grid=(S // ts,), in_specs=[row] + [vec] * n,
        out_specs=[_group_spec(d, ts, D) for _ in range(n) for d in dils],
        out_shape=[_group_shape(d, S, D, BF16) for _ in range(n) for d in dils],
        scratch_shapes=[_chunk_buf(ts, D)],
        compiler_params=_params("parallel"),
    )(x, *gains)
    return [[outs[q * nd + e].reshape(S, D) for e in range(nd)] for q in range(n)]


def _rms_bwd(x, cots, gains, dres, name, after=None, ts=256):
    S, D = x.shape
    ts = _tile(S, ts, 16 * max(d for _, _, d in cots))
    n = len(cots)
    ng = len(gains)
    extra = _as_list(after)
    n_in = 2 + n + ng + len(extra)

    def body(*refs):
        x_ref = refs[0]
        dh_refs = refs[1:1 + n]
        g_refs = refs[1 + n:1 + n + ng]
        dres_ref = refs[1 + n + ng]
        dx_ref, dxb_ref, gg_ref, buf = refs[n_in:n_in + 4]
        i = pl.program_id(0)
        xv = x_ref[...]
        r = lax.rsqrt(jnp.mean(xv * xv, axis=-1, keepdims=True) + EPS)
        nrm = xv * r
        dn = jnp.zeros_like(xv)
        rows = [jnp.zeros((1, D), F32) for _ in range(ng)]
        for q, (_, gi, d) in enumerate(cots):
            if d == 1:
                dh = dh_refs[q][...].astype(F32)
            else:
                _store_token_order(buf, dh_refs[q], d)
                dh = _read_chunks(buf)
            dn = dn + dh * g_refs[gi][...]
            rows[gi] = rows[gi] + jnp.sum(dh * nrm, axis=0, keepdims=True)
        dx = dres_ref[...] + r * (dn - nrm * jnp.mean(dn * nrm, axis=-1, keepdims=True))
        dx_ref[...] = dx
        dxb_ref[...] = dx.astype(BF16)
        upd = _stack_rows(rows, 8)

        @pl.when(i == 0)
        def _():
            gg_ref[...] = upd

        @pl.when(i > 0)
        def _():
            gg_ref[...] += upd

    row = pl.BlockSpec((ts, D), lambda i: (i, 0))
    vec = pl.BlockSpec((1, D), lambda i: (0, 0))
    acc = pl.BlockSpec((8, D), lambda i: (0, 0))
    return pl.pallas_call(
        body, name=name, grid=(S // ts,),
        in_specs=[row] + [_group_spec(d, ts, D) for _, _, d in cots] + [vec] * ng + [row] + [ANY] * len(extra),
        out_specs=[row, row, acc],
        out_shape=[jax.ShapeDtypeStruct((S, D), F32), jax.ShapeDtypeStruct((S, D), BF16),
                   jax.ShapeDtypeStruct((8, D), F32)],
        scratch_shapes=[_chunk_buf(ts, D)],
        compiler_params=_params("arbitrary"),
    )(x, *[a if d == 1 else a.reshape(d, S // d, D) for a, _, d in cots], *gains, dres, *extra)


def _final_head(x2, gain, target, name, ts=256):
    S, D = x2.shape
    ts = _tile(S, ts, 16)

    def body(x_ref, g_ref, t_ref, loss_ref, dx_ref, dxb_ref, gg_ref):
        i = pl.program_id(0)
        xv = x_ref[...]
        g = g_ref[...]
        r = lax.rsqrt(jnp.mean(xv * xv, axis=-1, keepdims=True) + EPS)
        nrm = xv * r
        err = nrm * g - t_ref[...]
        part = 0.5 * jnp.sum(jnp.mean(err * err, axis=-1, keepdims=True), axis=0, keepdims=True)
        dout = err * (1.0 / D)
        dn = dout * g
        dx = r * (dn - nrm * jnp.mean(dn * nrm, axis=-1, keepdims=True))
        dx_ref[...] = dx
        dxb_ref[...] = dx.astype(BF16)
        upd = _stack_rows([jnp.sum(dout * nrm, axis=0, keepdims=True)], 8)
        lpart = jnp.broadcast_to(part, (1, LANES))

        @pl.when(i == 0)
        def _():
            gg_ref[...] = upd
            loss_ref[...] = lpart

        @pl.when(i > 0)
        def _():
            gg_ref[...] += upd
            loss_ref[...] += lpart

    row = pl.BlockSpec((ts, D), lambda i: (i, 0))
    vec = pl.BlockSpec((1, D), lambda i: (0, 0))
    return pl.pallas_call(
        body, name=name, grid=(S // ts,), in_specs=[row, vec, row],
        out_specs=[pl.BlockSpec((1, LANES), lambda i: (0, 0)), row, row, pl.BlockSpec((8, D), lambda i: (0, 0))],
        out_shape=[jax.ShapeDtypeStruct((1, LANES), F32), jax.ShapeDtypeStruct((S, D), F32),
                   jax.ShapeDtypeStruct((S, D), BF16), jax.ShapeDtypeStruct((8, D), F32)],
        compiler_params=_params("arbitrary"),
    )(x2, gain, target)


CONV_ROWS = 64


SUBLANES = 8


def _shifted_buf(ts, cw):
    return pltpu.VMEM((SUBLANES - 1, ts + HALO - SUBLANES, cw), F32)


def _fill_shifted(shifted, buf):
    rows = shifted.shape[1]
    for s in range(1, SUBLANES):
        shifted[s - 1] = buf[s:s + rows, :]


def _window(buf, shifted, off, rows):
    s = off % SUBLANES
    base = off - s
    if s == 0:
        return buf[base:base + rows, :]
    return shifted[s - 1, base:base + rows, :]


def _conv_fwd(proj3, conv_w32, conv_b, name, ts=256, cw=256):
    _, S, E = proj3.shape
    ts = _tile(S, ts, HALO)
    cw = _tile(E, cw)
    per = ts // HALO
    rc = min(CONV_ROWS, ts)

    def body(a_ref, b_ref, ap_ref, bp_ref, w_ref, cb_ref, c_ref, ubuf, shifted):
        i = pl.program_id(0)
        up = ap_ref[...].astype(F32) * _sigmoid(bp_ref[...].astype(F32))
        ubuf[0:HALO, :] = jnp.where(i > 0, up, 0.0)
        ubuf[HALO:HALO + ts, :] = a_ref[...].astype(F32) * _sigmoid(b_ref[...].astype(F32))
        _fill_shifted(shifted, ubuf)
        for r0 in range(0, ts, rc):
            acc = jnp.broadcast_to(cb_ref[...], (rc, cw))
            for k in range(CONV_TAPS):
                off = r0 + HALO - (CONV_TAPS - 1) + k
                acc = acc + _window(ubuf, shifted, off, rc) * w_ref[k:k + 1, :]
            c_ref[r0:r0 + rc, :] = acc

    return pl.pallas_call(
        body, name=name, grid=(S // ts, E // cw),
        in_specs=[
            pl.BlockSpec((None, ts, cw), lambda i, j: (0, i, j)),
            pl.BlockSpec((None, ts, cw), lambda i, j: (1, i, j)),
            pl.BlockSpec((None, HALO, cw), lambda i, j: (0, jnp.maximum(i * per - 1, 0), j)),
            pl.BlockSpec((None, HALO, cw), lambda i, j: (1, jnp.maximum(i * per - 1, 0), j)),
            pl.BlockSpec((HALO, cw), lambda i, j: (0, j)),
            pl.BlockSpec((1, cw), lambda i, j: (0, j)),
        ],
        out_specs=pl.BlockSpec((ts, cw), lambda i, j: (i, j)),
        out_shape=jax.ShapeDtypeStruct((S, E), F32),
        scratch_shapes=[pltpu.VMEM((HALO + ts, cw), F32), _shifted_buf(ts, cw)],
        compiler_params=_params("parallel", "parallel"),
    )(proj3, proj3, proj3, proj3, conv_w32, conv_b)


def _ln_gate_fwd(c, proj3, ln_g, ln_b, name, ts=256):
    S, E = c.shape
    ts = _tile(S, ts, 16)

    def body(c_ref, z_ref, g_ref, b_ref, y_ref):
        cv = c_ref[...]
        mu = jnp.mean(cv, axis=-1, keepdims=True)
        d = cv - mu
        var = jnp.mean(d * d, axis=-1, keepdims=True)
        cn = d * lax.rsqrt(var + EPS) * g_ref[...] + b_ref[...]
        z = z_ref[...].astype(F32)
        y_ref[...] = ((cn * _sigmoid(cn)).astype(F32) * (z * _sigmoid(z))).astype(BF16)

    row = pl.BlockSpec((ts, E), lambda i: (i, 0))
    vec = pl.BlockSpec((1, E), lambda i: (0, 0))
    return pl.pallas_call(
        body, name=name, grid=(S // ts,),
        in_specs=[row, pl.BlockSpec((None, ts, E), lambda i: (2, i, 0)), vec, vec],
        out_specs=row, out_shape=jax.ShapeDtypeStruct((S, E), BF16),
        compiler_params=_params("parallel"),
    )(c, proj3, ln_g, ln_b)


def _ln_gate_bwd(c, proj3, dy, ln_g, ln_b, name, ts=256):
    S, E = c.shape
    ts = _tile(S, ts, 16)

    def body(c_ref, z_ref, dy_ref, g_ref, b_ref, dc_ref, dz_ref, acc_ref):
        i = pl.program_id(0)
        cv = c_ref[...]
        g = g_ref[...]
        mu = jnp.mean(cv, axis=-1, keepdims=True)
        d = cv - mu
        var = jnp.mean(d * d, axis=-1, keepdims=True)
        rstd = lax.rsqrt(var + EPS)
        chat = d * rstd
        cn = chat * g + b_ref[...]
        z = z_ref[...].astype(F32)
        dyv = dy_ref[...].astype(F32)
        sc = _sigmoid(cn)
        sz = _sigmoid(z)
        dcn = dyv * (z * sz) * (sc * (1.0 + cn * (1.0 - sc)))
        dz_ref[...] = (dyv * (cn * sc) * (sz * (1.0 + z * (1.0 - sz)))).astype(BF16)
        dchat = dcn * g
        dcv = rstd * (dchat - jnp.mean(dchat, axis=-1, keepdims=True)
                      - chat * jnp.mean(dchat * chat, axis=-1, keepdims=True))
        dc_ref[...] = dcv
        upd = _stack_rows([jnp.sum(dcn * chat, axis=0, keepdims=True),
                           jnp.sum(dcn, axis=0, keepdims=True),
                           jnp.sum(dcv, axis=0, keepdims=True)], 8)

        @pl.when(i == 0)
        def _():
            acc_ref[...] = upd

        @pl.when(i > 0)
        def _():
            acc_ref[...] += upd

    row = pl.BlockSpec((ts, E), lambda i: (i, 0))
    vec = pl.BlockSpec((1, E), lambda i: (0, 0))
    return pl.pallas_call(
        body, name=name, grid=(S // ts,),
        in_specs=[row, pl.BlockSpec((None, ts, E), lambda i: (2, i, 0)), row, vec, vec],
        out_specs=[row, pl.BlockSpec((None, ts, E), lambda i: (2, i, 0)), pl.BlockSpec((8, E), lambda i: (0, 0))],
        out_shape=[jax.ShapeDtypeStruct((S, E), F32), jax.ShapeDtypeStruct((3, S, E), BF16),
                   jax.ShapeDtypeStruct((8, E), F32)],
        compiler_params=_params("arbitrary"),
    )(c, proj3, dy, ln_g, ln_b)


def _conv_bwd(proj3, dc, conv_w32, dproj3, name, ts=256, cw=256):
    _, S, E = proj3.shape
    ts = _tile(S, ts, HALO)
    cw = _tile(E, cw)
    per = ts // HALO
    n_i = S // ts
    last_halo = S // HALO - 1
    rc = min(CONV_ROWS, ts)

    def body(a_ref, b_ref, dc_ref, dcn_ref, w_ref, dp_in, dab_ref, dw_ref, dcbuf, ubuf, dwacc, shifted):
        del dp_in
        i = pl.program_id(1)
        dcbuf[0:ts, :] = dc_ref[...]
        dcbuf[ts:ts + HALO, :] = jnp.where(i < n_i - 1, dcn_ref[...], 0.0)
        _fill_shifted(shifted, dcbuf)
        av = a_ref[...].astype(F32)
        sb = _sigmoid(b_ref[...].astype(F32))
        ubuf[...] = av * sb

        @pl.when(i == 0)
        def _():
            dwacc[...] = jnp.zeros_like(dwacc)

        for r0 in range(0, ts, rc):
            uv = ubuf[r0:r0 + rc, :]
            du = jnp.zeros((rc, cw), F32)
            for d in range(CONV_TAPS):
                k = CONV_TAPS - 1 - d
                win = _window(dcbuf, shifted, r0 + d, rc)
                du = du + win * w_ref[k:k + 1, :]
                dwacc[k:k + 1, :] += jnp.sum(uv * win, axis=0, keepdims=True)
            a_c = a_ref[r0:r0 + rc, :].astype(F32)
            s_c = _sigmoid(b_ref[r0:r0 + rc, :].astype(F32))
            dab_ref[0, r0:r0 + rc, :] = (du * s_c).astype(BF16)
            dab_ref[1, r0:r0 + rc, :] = (du * a_c * s_c * (1.0 - s_c)).astype(BF16)

        @pl.when(i == n_i - 1)
        def _():
            dw_ref[...] = dwacc[...]

    return pl.pallas_call(
        body, name=name, grid=(E // cw, n_i),
        in_specs=[
            pl.BlockSpec((None, ts, cw), lambda j, i: (0, i, j)),
            pl.BlockSpec((None, ts, cw), lambda j, i: (1, i, j)),
            pl.BlockSpec((ts, cw), lambda j, i: (i, j)),
            pl.BlockSpec((HALO, cw), lambda j, i: (jnp.minimum((i + 1) * per, last_halo), j)),
            pl.BlockSpec((HALO, cw), lambda j, i: (0, j)),
            ANY,
        ],
        out_specs=[pl.BlockSpec((2, ts, cw), lambda j, i: (0, i, j)),
                   pl.BlockSpec((HALO, cw), lambda j, i: (0, j))],
        out_shape=[jax.ShapeDtypeStruct((3, S, E), BF16), jax.ShapeDtypeStruct((HALO, E), F32)],
        scratch_shapes=[pltpu.VMEM((ts + HALO, cw), F32), pltpu.VMEM((ts, cw), F32), pltpu.VMEM((HALO, cw), F32),
                        _shifted_buf(ts, cw)],
        input_output_aliases={5: 0},
        compiler_params=_params("parallel", "arbitrary"),
    )(proj3, proj3, dc, dc, conv_w32, dproj3)


def _bucket_table(dil):
    delta = (np.arange(BLOCK)[:, None] + BLOCK) - np.arange(2 * BLOCK)[None, :]
    dist = np.clip(delta, 0, None) * dil
    large = MAX_EXACT + (np.log(np.maximum(dist, 1).astype(np.float32) / MAX_EXACT)
                         / np.log(MAX_DISTANCE / MAX_EXACT) * (N_BUCKETS - MAX_EXACT)).astype(np.int32)
    large = np.minimum(large, N_BUCKETS - 1)
    return np.where(dist < MAX_EXACT, dist, large).astype(np.int32).reshape(-1)


def _onehot(dil):
    tbl = jnp.asarray(_bucket_table(dil))
    return (tbl[None, :] == jnp.arange(LANES, dtype=jnp.int32)[:, None]).astype(BF16)


def _split3(v):
    hi = v.astype(BF16)
    r1 = v - hi.astype(F32)
    mid = r1.astype(BF16)
    lo = (r1 - mid.astype(F32)).astype(BF16)
    return hi, mid, lo


def _bias_table(rb_t, onehot, name):
    H = rb_t.shape[0]
    N = onehot.shape[1]

    def body(r_ref, oh_ref, o_ref):
        oh = oh_ref[...]
        hi, mid, lo = _split3(r_ref[...])
        o_ref[...] = (_dot(lo, oh, NN) + _dot(mid, oh, NN)) + _dot(hi, oh, NN)

    return pl.pallas_call(
        body, name=name, in_specs=[VMEM_SPEC, VMEM_SPEC], out_specs=VMEM_SPEC,
        out_shape=jax.ShapeDtypeStruct((H, N), F32),
        compiler_params=pltpu.CompilerParams(vmem_limit_bytes=VMEM_LIMIT),
    )(rb_t, onehot)


def _bias_grad(dbs, onehots, name):
    H = dbs[0].shape[0]
    n = len(dbs)

    def body(*refs):
        acc = jnp.zeros((H, LANES), F32)
        for q in range(n):
            oh = refs[n + q][...]
            hi, mid, lo = _split3(refs[q][...])
            acc = acc + ((_dot(lo, oh, NT) + _dot(mid, oh, NT)) + _dot(hi, oh, NT))
        refs[2 * n][...] = acc

    return pl.pallas_call(
        body, name=name, in_specs=[VMEM_SPEC] * (2 * n), out_specs=VMEM_SPEC,
        out_shape=jax.ShapeDtypeStruct((H, LANES), F32),
        compiler_params=pltpu.CompilerParams(vmem_limit_bytes=VMEM_LIMIT),
    )(*dbs, *onehots)


def _pair_mask(has_prev):
    qi = lax.broadcasted_iota(jnp.int32, (BLOCK, 2 * BLOCK), 0)
    ki = lax.broadcasted_iota(jnp.int32, (BLOCK, 2 * BLOCK), 1)
    prev = jnp.logical_and(jnp.logical_and(ki < BLOCK, ki >= qi), has_prev)
    return jnp.logical_or(prev, jnp.logical_and(ki >= BLOCK, ki - BLOCK <= qi))


def _attn_fwd(q, kv, bias, dil, name):
    S, A = q.shape
    H = A // HEAD_DIM
    L = S // dil
    nb = L // BLOCK
    qv = q.reshape(dil, L, A)
    kvv = kv.reshape(2, dil, L, A)

    def body(q_ref, kp_ref, kc_ref, vp_ref, vc_ref, b_ref, o_ref, lse_ref):
        i = pl.program_id(1)
        qi = lax.broadcasted_iota(jnp.int32, (BLOCK, BLOCK), 0)
        ki = lax.broadcasted_iota(jnp.int32, (BLOCK, BLOCK), 1)
        mask = _pair_mask(i > 0)
        lane = lax.broadcasted_iota(jnp.int32, (BLOCK, LANES), 1)
        lse_acc = jnp.zeros((BLOCK, LANES), F32)

        def scores(h):
            sl = slice(h * HEAD_DIM, (h + 1) * HEAD_DIM)
            return _dot(q_ref[:, sl], jnp.concatenate([kp_ref[:, sl], kc_ref[:, sl]], axis=0), NT)

        ahead = [scores(h) for h in range(min(SCORES_AHEAD, H))]
        for h in range(H):
            sl = slice(h * HEAD_DIM, (h + 1) * HEAD_DIM)
            raw = ahead.pop(0)
            if h + SCORES_AHEAD < H:
                ahead.append(scores(h + SCORES_AHEAD))
            s = jnp.where(mask, raw * SCALE + b_ref[h], NEG)
            m = jnp.max(s, axis=-1, keepdims=True)
            p = jnp.exp(s - m)
            den = jnp.sum(p, axis=-1, keepdims=True)
            acc = _dot(p.astype(BF16), jnp.concatenate([vp_ref[:, sl], vc_ref[:, sl]], axis=0), NN)
            o_ref[:, sl] = acc / den
            lse_acc = jnp.where(lane == h, m + jnp.log(den), lse_acc)
        lse_ref[...] = lse_acc

    def blk(slab, prev):
        if prev:
            return pl.BlockSpec((None, None, BLOCK, A), lambda r, i: (slab, r, jnp.maximum(i - 1, 0), 0))
        return pl.BlockSpec((None, None, BLOCK, A), lambda r, i: (slab, r, i, 0))

    o, lse = pl.pallas_call(
        body, name=name, grid=(dil, nb),
        in_specs=[pl.BlockSpec((None, BLOCK, A), lambda r, i: (r, i, 0)),
                  blk(0, True), blk(0, False), blk(1, True), blk(1, False),
                  pl.BlockSpec((H, BLOCK, 2 * BLOCK), lambda r, i: (0, 0, 0))],
        out_specs=[pl.BlockSpec((None, BLOCK, A), lambda r, i: (r, i, 0)),
                   pl.BlockSpec((None, BLOCK, LANES), lambda r, i: (r, i, 0))],
        out_shape=[jax.ShapeDtypeStruct((dil, L, A), F32), jax.ShapeDtypeStruct((dil, L, LANES), F32)],
        compiler_params=_params("parallel", "parallel"),
    )(qv, kvv, kvv, kvv, kvv, bias)
    return o.reshape(S, A), lse.reshape(S, LANES)


def _attn_merge(os_, lses, z, dils, name, ts=256):
    S, A = z.shape
    H = A // HEAD_DIM
    ts = _tile(S, ts, 16 * max(dils))
    n = len(os_)

    def body(*refs):
        z_ref = refs[2 * n]
        y_ref, om_ref = refs[2 * n + 1:2 * n + 3]
        lse_refs = refs[2 * n + 3:3 * n + 3]
        o_refs = refs[3 * n + 3:4 * n + 3]
        l_bufs = refs[4 * n + 3:5 * n + 3]
        lse_buf = refs[5 * n + 3]
        ls = []
        for q, d in enumerate(dils):
            if d == 1:
                ls.append(refs[n + q][...])
            else:
                _store_token_order(o_refs[q], refs[q], d)
                _store_token_order(l_bufs[q], refs[n + q], d)
                ls.append(l_bufs[q][0])
        m = ls[0]
        for q in range(1, n):
            m = jnp.maximum(m, ls[q])
        es = [jnp.exp(v - m) for v in ls]
        den = es[0]
        for q in range(1, n):
            den = den + es[q]
        alphas = [e / den for e in es]
        lse = m + jnp.log(den)
        lse_buf[0] = lse
        for q, d in enumerate(dils):
            if d == 1:
                lse_refs[q][...] = lse
            else:
                _emit_group_order(lse_refs[q], lse_buf, d, F32)
        lane = lax.broadcasted_iota(jnp.int32, (ts, LANES), 1)
        for h in range(H):
            sl = slice(h * HEAD_DIM, (h + 1) * HEAD_DIM)
            om = jnp.zeros((ts, HEAD_DIM), F32)
            for q, d in enumerate(dils):
                o_h = refs[q][:, sl] if d == 1 else o_refs[q][h]
                om = om + _lane_col(alphas[q], h, lane) * o_h
            z = z_ref[:, sl].astype(F32)
            y_ref[:, sl] = (om * (z * _sigmoid(z))).astype(BF16)
            om_ref[:, sl] = om.astype(BF16)

    row = pl.BlockSpec((ts, A), lambda i: (i, 0))
    outs = pl.pallas_call(
        body, name=name, grid=(S // ts,),
        in_specs=[_group_spec(d, ts, A) for d in dils] + [_group_spec(d, ts, LANES) for d in dils] + [row],
        out_specs=[row, row] + [_group_spec(d, ts, LANES) for d in dils],
        out_shape=[jax.ShapeDtypeStruct((S, A), BF16), jax.ShapeDtypeStruct((S, A), BF16)]
        + [_group_shape(d, S, LANES, F32) for d in dils],
        scratch_shapes=[_chunk_buf(ts, A)] * n + [_chunk_buf(ts, LANES)] * (n + 1),
        compiler_params=_params("parallel"),
    )(*[o if d == 1 else o.reshape(d, S // d, A) for o, d in zip(os_, dils)],
      *[v if d == 1 else v.reshape(d, S // d, LANES) for v, d in zip(lses, dils)], z)
    return outs[0], outs[1], [v.reshape(S, LANES) for v in outs[2:]]


def _gate_bwd(dy, om, z, dils, name, ts=256):
    S, A = dy.shape
    H = A // HEAD_DIM
    ts = _tile(S, ts, 16 * max(dils))
    n = len(dils)

    def body(*refs):
        dy_ref, om_ref, z_ref = refs[:3]
        do_refs = refs[3:3 + n]
        dh_refs = refs[3 + n:3 + 2 * n]
        dz_ref = refs[3 + 2 * n]
        do_buf, dh_buf = refs[4 + 2 * n:6 + 2 * n]
        lane = lax.broadcasted_iota(jnp.int32, (ts, LANES), 1)
        acc = jnp.zeros((ts, LANES), F32)
        for h in range(H):
            sl = slice(h * HEAD_DIM, (h + 1) * HEAD_DIM)
            dyv = dy_ref[:, sl].astype(F32)
            omv = om_ref[:, sl].astype(F32)
            zv = z_ref[:, sl].astype(F32)
            sz = _sigmoid(zv)
            dob = (dyv * (zv * sz)).astype(BF16)
            do_buf[h] = dob.astype(F32)
            for q, d in enumerate(dils):
                if d == 1:
                    do_refs[q][:, sl] = dob
            dz_ref[:, sl] = (dyv * omv * (sz * (1.0 + zv * (1.0 - sz)))).astype(BF16)
            acc = jnp.where(lane == h, jnp.sum(dob.astype(F32) * omv, axis=-1, keepdims=True), acc)
        dh_buf[0] = acc
        for q, d in enumerate(dils):
            if d == 1:
                dh_refs[q][...] = acc
            else:
                _emit_group_order(do_refs[q], do_buf, d, BF16)
                _emit_group_order(dh_refs[q], dh_buf, d, F32)

    row = pl.BlockSpec((ts, A), lambda i: (i, 0))
    outs = pl.pallas_call(
        body, name=name, grid=(S // ts,), in_specs=[row, row, row],
        out_specs=[_group_spec(d, ts, A) for d in dils] + [_group_spec(d, ts, LANES) for d in dils] + [row],
        out_shape=[_group_shape(d, S, A, BF16) for d in dils] + [_group_shape(d, S, LANES, F32) for d in dils]
        + [jax.ShapeDtypeStruct((S, A), BF16)],
        scratch_shapes=[_chunk_buf(ts, A), _chunk_buf(ts, LANES)],
        compiler_params=_params("parallel"),
    )(dy, om, z)
    return ([v.reshape(S, A) for v in outs[:n]], [v.reshape(S, LANES) for v in outs[n:2 * n]], outs[2 * n])


def _attn_bwd(q, kv, do, lse, dh, bias, dil, name):
    S, A = q.shape
    H = A // HEAD_DIM
    L = S // dil
    nb = L // BLOCK
    qv = q.reshape(dil, L, A)
    kvv = kv.reshape(2, dil, L, A)
    dov = do.reshape(dil, L, A)
    lsev = lse.reshape(dil, L, LANES)
    dhv = dh.reshape(dil, L, LANES)

    def body(*refs):
        (q_ref, qn_ref, kp_ref, kc_ref, vp_ref, vc_ref, do_ref, don_ref, l_ref, ln_ref, d_ref, dn_ref,
         b_ref) = refs[:13]
        dq_ref, dkv_ref, db_ref = refs[13:16]
        r = pl.program_id(0)
        i = pl.program_id(1)
        qi = lax.broadcasted_iota(jnp.int32, (BLOCK, BLOCK), 0)
        ki = lax.broadcasted_iota(jnp.int32, (BLOCK, BLOCK), 1)
        mask_c = ki <= qi
        band = ki >= qi
        mask_p = jnp.logical_and(band, i > 0)
        mask_n = jnp.logical_and(band, i < nb - 1)
        lane = lax.broadcasted_iota(jnp.int32, (BLOCK, LANES), 1)

        @pl.when(jnp.logical_and(r == 0, i == 0))
        def _():
            db_ref[...] = jnp.zeros_like(db_ref)

        def products(h):
            sl = slice(h * HEAD_DIM, (h + 1) * HEAD_DIM)
            q_i, q_n = q_ref[:, sl], qn_ref[:, sl]
            k_p, k_c = kp_ref[:, sl], kc_ref[:, sl]
            v_p, v_c = vp_ref[:, sl], vc_ref[:, sl]
            do_i, do_n = do_ref[:, sl], don_ref[:, sl]
            return (_dot(q_i, k_c, NT), _dot(do_i, v_c, NT), _dot(q_i, k_p, NT), _dot(do_i, v_p, NT),
                    _dot(q_n, k_c, NT), _dot(do_n, v_c, NT))

        ahead = [products(h) for h in range(min(PRODUCTS_AHEAD, H))]
        for h in range(H):
            sl = slice(h * HEAD_DIM, (h + 1) * HEAD_DIM)
            s1, dp1, s2, dp2, s3, dp3 = ahead.pop(0)
            if h + PRODUCTS_AHEAD < H:
                ahead.append(products(h + PRODUCTS_AHEAD))
            q_i, q_n = q_ref[:, sl], qn_ref[:, sl]
            k_p, k_c = kp_ref[:, sl], kc_ref[:, sl]
            do_i, do_n = do_ref[:, sl], don_ref[:, sl]
            l_i, l_n = _lane_col(l_ref[...], h, lane), _lane_col(ln_ref[...], h, lane)
            d_i, d_n = _lane_col(d_ref[...], h, lane), _lane_col(dn_ref[...], h, lane)
            b_c = b_ref[h, :, BLOCK:]
            b_p = b_ref[h, :, :BLOCK]
            p1 = jnp.exp(jnp.where(mask_c, s1 * SCALE + b_c, NEG) - l_i)
            ds1 = p1 * (dp1 - d_i)
            ds1b = ds1.astype(BF16)
            p1b = p1.astype(BF16)
            p2 = jnp.exp(jnp.where(mask_p, s2 * SCALE + b_p, NEG) - l_i)
            ds2 = p2 * (dp2 - d_i)
            ds2b = ds2.astype(BF16)
            p3 = jnp.exp(jnp.where(mask_n, s3 * SCALE + b_p, NEG) - l_n)
            ds3b = (p3 * (dp3 - d_n)).astype(BF16)
            p3b = p3.astype(BF16)
            dq = _dot(ds1b, k_c, NN) + _dot(ds2b, k_p, NN)
            dk = _dot(ds1b, q_i, TN) + _dot(ds3b, q_n, TN)
            dv = _dot(p1b, do_i, TN) + _dot(p3b, do_n, TN)
            dq_ref[:, sl] = (dq * SCALE).astype(BF16)
            dkv_ref[0, :, sl] = (dk * SCALE).astype(BF16)
            dkv_ref[1, :, sl] = dv.astype(BF16)
            db_ref[h, :, BLOCK:] += ds1
            db_ref[h, :, :BLOCK] += ds2

    def blk(slab, shift):
        if shift < 0:
            return pl.BlockSpec((None, None, BLOCK, A), lambda r, i: (slab, r, jnp.maximum(i - 1, 0), 0))
        return pl.BlockSpec((None, None, BLOCK, A), lambda r, i: (slab, r, i, 0))

    def row(width, shift):
        if shift > 0:
            return pl.BlockSpec((None, BLOCK, width), lambda r, i: (r, jnp.minimum(i + 1, nb - 1), 0))
        return pl.BlockSpec((None, BLOCK, width), lambda r, i: (r, i, 0))

    in_specs = [row(A, 0), row(A, 1), blk(0, -1), blk(0, 0), blk(1, -1), blk(1, 0),
                row(A, 0), row(A, 1), row(LANES, 0), row(LANES, 1), row(LANES, 0), row(LANES, 1),
                pl.BlockSpec((H, BLOCK, 2 * BLOCK), lambda r, i: (0, 0, 0))]
    dq, dkv, db = pl.pallas_call(
        body, name=name, grid=(dil, nb), in_specs=in_specs,
        out_specs=[pl.BlockSpec((None, BLOCK, A), lambda r, i: (r, i, 0)),
                   pl.BlockSpec((2, None, BLOCK, A), lambda r, i: (0, r, i, 0)),
                   pl.BlockSpec((H, BLOCK, 2 * BLOCK), lambda r, i: (0, 0, 0))],
        out_shape=[jax.ShapeDtypeStruct((dil, L, A), BF16), jax.ShapeDtypeStruct((2, dil, L, A), BF16),
                   jax.ShapeDtypeStruct((H, BLOCK, 2 * BLOCK), F32)],
        compiler_params=_params("arbitrary", "arbitrary"),
    )(qv, qv, kvv, kvv, kvv, kvv, dov, dov, lsev, lsev, dhv, dhv, bias)
    return dq.reshape(S, A), dkv.reshape(2, S, A), db


def _sum_leading(stack, out_dtype, name, tr=256, tc=2048):
    n, R, C = stack.shape
    tr = _tile(R, tr, 16)
    tc = _tile(C, tc)

    def body(s_ref, o_ref):
        acc = s_ref[0].astype(F32)
        for q in range(1, n):
            acc = acc + s_ref[q].astype(F32)
        o_ref[...] = acc.astype(out_dtype)

    return pl.pallas_call(
        body, name=name, grid=(R // tr, C // tc),
        in_specs=[pl.BlockSpec((n, tr, tc), lambda i, j: (0, i, j))],
        out_specs=pl.BlockSpec((tr, tc), lambda i, j: (i, j)),
        out_shape=jax.ShapeDtypeStruct((R, C), out_dtype),
        compiler_params=_params("parallel", "parallel"),
    )(stack)


def _add_half(g, t, c_idx, kind, name, after=None, tr=256, tc=2048):
    R, C = t.shape
    tr = _tile(R, tr, 16)
    tc = _tile(C, tc)
    nrb, ncb = R // tr, C // tc
    extra = _as_list(after)

    def body(c_ref, g_ref, t_ref, *rest):
        del c_ref
        o_ref = rest[len(extra)]
        o_ref[...] = (g_ref[...].astype(F32) + t_ref[...].astype(F32)).astype(BF16)

    if kind == "col":
        g_map = lambda i, j, c_ref: (c_ref[0] * nrb + i, j)
    else:
        g_map = lambda i, j, c_ref: (i, c_ref[0] * ncb + j)
    same = lambda i, j, c_ref: (i, j)
    return pl.pallas_call(
        body, name=name,
        grid_spec=pltpu.PrefetchScalarGridSpec(
            num_scalar_prefetch=1, grid=(nrb, ncb),
            in_specs=[pl.BlockSpec((tr, tc), g_map), pl.BlockSpec((tr, tc), same)] + [ANY] * len(extra),
            out_specs=pl.BlockSpec((tr, tc), same)),
        out_shape=jax.ShapeDtypeStruct((R, C), BF16),
        compiler_params=_params("parallel", "parallel"),
    )(c_idx, g, t, *extra)


def _cast_into_full(w, kind, chip_idx, name, tr=256, tc=2048):
    R, C = w.shape
    tr = _tile(R, tr, 16)
    tc = _tile(C, tc)
    nrb, ncb = R // tr, C // tc

    def body(k_ref, w_ref, o_ref):
        del k_ref
        o_ref[...] = w_ref[...].astype(BF16)

    if kind == "col":
        o_map = lambda i, j, k_ref: (i, k_ref[0] * ncb + j)
        full = (R, N_CHIPS * C)
    else:
        o_map = lambda i, j, k_ref: (k_ref[0] * nrb + i, j)
        full = (N_CHIPS * R, C)
    return pl.pallas_call(
        body, name=name,
        grid_spec=pltpu.PrefetchScalarGridSpec(
            num_scalar_prefetch=1, grid=(nrb, ncb),
            in_specs=[pl.BlockSpec((tr, tc), lambda i, j, k_ref: (i, j))],
            out_specs=pl.BlockSpec((tr, tc), o_map)),
        out_shape=jax.ShapeDtypeStruct(full, BF16),
        compiler_params=_params("parallel", "parallel"),
    )(chip_idx, w)


def _sum_into_shard(p, u, idx, kind, name, tr=256, tc=2048):
    _, R, C = u.shape
    tr = _tile(R, tr, 16)
    tc = _tile(C, tc)
    nrb, ncb = R // tr, C // tc

    def body(i_ref, p_ref, u_ref, o_ref):
        del i_ref
        acc = p_ref[...].astype(F32)
        for q in range(N_CHIPS - 1):
            acc = acc + u_ref[q].astype(F32)
        o_ref[...] = acc

    if kind == "col":
        p_map = lambda i, j, r: (i, r[0] * ncb + j)
        o_map = lambda i, j, r: (r[1] * nrb + i, j)
        full = (2 * R, C)
    else:
        p_map = lambda i, j, r: (r[0] * nrb + i, j)
        o_map = lambda i, j, r: (i, r[1] * ncb + j)
        full = (R, 2 * C)
    return pl.pallas_call(
        body, name=name,
        grid_spec=pltpu.PrefetchScalarGridSpec(
            num_scalar_prefetch=1, grid=(nrb, ncb),
            in_specs=[pl.BlockSpec((tr, tc), p_map), pl.BlockSpec((N_CHIPS - 1, tr, tc), lambda i, j, r: (0, i, j))],
            out_specs=pl.BlockSpec((tr, tc), o_map)),
        out_shape=jax.ShapeDtypeStruct(full, F32),
        compiler_params=_params("parallel", "parallel"),
    )(idx, p, u)


def _adamw(w, g, m, v, name, tr=256, tc=2048):
    R, C = w.shape
    tr = _tile(R, tr, 8)
    tc = _tile(C, tc)
    c1 = 1.0 - ADAM_B1 ** ADAM_STEP
    c2 = 1.0 - ADAM_B2 ** ADAM_STEP

    def body(w_ref, g_ref, m_ref, v_ref, d_ref, nm_ref, nv_ref):
        gv = g_ref[...]
        nm = ADAM_B1 * m_ref[...] + (1.0 - ADAM_B1) * gv
        nv = ADAM_B2 * v_ref[...] + (1.0 - ADAM_B2) * (gv * gv)
        d_ref[...] = -ADAM_LR * ((nm / c1) / (jnp.sqrt(nv / c2) + ADAM_EPS) + ADAM_WD * w_ref[...])
        nm_ref[...] = nm
        nv_ref[...] = nv

    blk = pl.BlockSpec((tr, tc), lambda i, j: (i, j))
    sh = jax.ShapeDtypeStruct((R, C), F32)
    return pl.pallas_call(
        body, name=name, grid=(R // tr, C // tc), in_specs=[blk] * 4, out_specs=[blk] * 3,
        out_shape=[sh, sh, sh], compiler_params=_params("parallel", "parallel"),
    )(w, g, m, v)


def _xyc():
    return lax.axis_index("x"), lax.axis_index("y"), lax.axis_index("c")


def _drain(copies):
    for cp in copies:
        if cp.is_remote:
            cp.wait_send()
        else:
            cp.wait()


def _other_chips(x, y):
    return [(1 - x, y), (x, 1 - y), (1 - x, 1 - y)]


def _allgather_small(blk, name, after=None):
    R, C = blk.shape
    extra = _as_list(after)

    def body(*refs):
        x_ref = refs[0]
        out_ref, send_sems, recv_sems, local_sem = refs[1 + len(extra):]
        x, y, c = _xyc()
        me = 4 * x + 2 * y + c
        mine = pltpu.make_async_copy(x_ref, out_ref.at[me], local_sem)
        mine.start()
        peers = []
        for k in range(1, N_DEV):
            px = 1 - x if (k >> 2) & 1 else x
            py = 1 - y if (k >> 1) & 1 else y
            pc = 1 - c if k & 1 else c
            peers.append((px, py, pc))
        sends = []
        for k, peer in enumerate(peers):
            cp = pltpu.make_async_remote_copy(
                src_ref=x_ref, dst_ref=out_ref.at[me], send_sem=send_sems.at[k], recv_sem=recv_sems.at[k],
                device_id=peer, device_id_type=MESH)
            cp.start()
            sends.append(cp)
        for k, (px, py, pc) in enumerate(peers):
            pltpu.make_async_remote_copy(
                src_ref=x_ref, dst_ref=out_ref.at[4 * px + 2 * py + pc], send_sem=send_sems.at[k],
                recv_sem=recv_sems.at[k], device_id=(px, py, pc), device_id_type=MESH).wait_recv()
        for cp in sends:
            cp.wait_send()
        mine.wait()

    return pl.pallas_call(
        body, name=name, in_specs=[VMEM_SPEC] + [ANY] * len(extra), out_specs=VMEM_SPEC,
        out_shape=jax.ShapeDtypeStruct((N_DEV, R, C), blk.dtype),
        scratch_shapes=[pltpu.SemaphoreType.DMA((N_DEV - 1,)), pltpu.SemaphoreType.DMA((N_DEV - 1,)),
                        pltpu.SemaphoreType.DMA],
        compiler_params=pltpu.CompilerParams(vmem_limit_bytes=VMEM_LIMIT),
    )(blk, *extra)


def _full_region(ref, kind, chip, half, shard_shape):
    r, cn = shard_shape
    hr = r // 2
    if kind == "col":
        rows = pl.ds(0, r) if half is None else pl.ds(pl.multiple_of(half * hr, 16), hr)
        return ref.at[rows, pl.ds(pl.multiple_of(chip * cn, LANES), cn)]
    if half is None:
        return ref.at[pl.ds(pl.multiple_of(chip * r, 16), r), :]
    return ref.at[pl.ds(pl.multiple_of(chip * r + half * hr, 16), hr), :]


def _allgather_weights(fulls, kinds, shapes, name):
    n = len(fulls)

    def body(*refs):
        outs = refs[n:2 * n]
        send_sems, recv_sems = refs[2 * n:]
        x, y, c = _xyc()
        chip = 2 * x + y
        sib = (x, y, 1 - c)
        others = _other_chips(x, y)
        started = []
        for w in range(n):
            mine = _full_region(outs[w], kinds[w], chip, c, shapes[w])
            for j, (ox, oy) in enumerate(others):
                cp = pltpu.make_async_remote_copy(
                    src_ref=mine, dst_ref=mine, send_sem=send_sems.at[6 * w + j], recv_sem=recv_sems.at[6 * w + j],
                    device_id=(ox, oy, c), device_id_type=MESH)
                cp.start()
                started.append(cp)
        for w in range(n):
            for j, (ox, oy) in enumerate(others):
                landed = _full_region(outs[w], kinds[w], 2 * ox + oy, c, shapes[w])
                pltpu.make_async_remote_copy(
                    src_ref=landed, dst_ref=landed, send_sem=send_sems.at[6 * w + j], recv_sem=recv_sems.at[6 * w + j],
                    device_id=(ox, oy, c), device_id_type=MESH).wait_recv()
                cp = pltpu.make_async_remote_copy(
                    src_ref=landed, dst_ref=landed, send_sem=send_sems.at[6 * w + 3 + j],
                    recv_sem=recv_sems.at[6 * w + 3 + j], device_id=sib, device_id_type=MESH)
                cp.start()
                started.append(cp)
        for w in range(n):
            for j, (ox, oy) in enumerate(others):
                theirs = _full_region(outs[w], kinds[w], 2 * ox + oy, 1 - c, shapes[w])
                pltpu.make_async_remote_copy(
                    src_ref=theirs, dst_ref=theirs, send_sem=send_sems.at[6 * w + 3 + j],
                    recv_sem=recv_sems.at[6 * w + 3 + j], device_id=sib, device_id_type=MESH).wait_recv()
        _drain(started)

    return pl.pallas_call(
        body, name=name, in_specs=[ANY] * n, out_specs=[ANY] * n,
        out_shape=[jax.ShapeDtypeStruct(f.shape, f.dtype) for f in fulls],
        input_output_aliases={w: w for w in range(n)},
        scratch_shapes=[pltpu.SemaphoreType.DMA((6 * n,)), pltpu.SemaphoreType.DMA((6 * n,))],
    )(*fulls)


def _region_of_size(ref, kind, shard_shape, count):
    r, cn = shard_shape
    if kind == "col":
        return ref.at[pl.ds(0, r // 2), pl.ds(0, count * cn)]
    return ref.at[pl.ds(0, count * (r // 2)), :]


def _allgather_weights_seq(fulls, kinds, shapes, name, collective_id):
    n = len(fulls)
    refs = [jax.new_ref(f, memory_space=pltpu.MemorySpace.HBM) for f in fulls]

    def body(send_sems, recv_sems):
        x, y, c = _xyc()
        chip = 2 * x + y
        sib = (x, y, 1 - c)
        others = _other_chips(x, y)
        peers = [(ox, oy, c) for ox, oy in others] + [sib]
        barrier = pltpu.get_barrier_semaphore()
        for peer in peers:
            pl.semaphore_signal(barrier, inc=1, device_id=peer, device_id_type=MESH)
        pl.semaphore_wait(barrier, len(peers))

        def copy(w, region, sem, to):
            return pltpu.make_async_remote_copy(src_ref=region, dst_ref=region, send_sem=send_sems.at[sem],
                                                recv_sem=recv_sems.at[sem], device_id=to, device_id_type=MESH)

        for w in range(n):
            mine = _full_region(refs[w], kinds[w], chip, c, shapes[w])
            for ox, oy in others:
                copy(w, mine, 2 * w, (ox, oy, c)).start()
        for w in range(n):
            three = _region_of_size(refs[w], kinds[w], shapes[w], 3)
            copy(w, three, 2 * w, sib).wait_recv()
            for ox, oy in others:
                copy(w, _full_region(refs[w], kinds[w], 2 * ox + oy, c, shapes[w]), 2 * w + 1, sib).start()
        for w in range(n):
            three = _region_of_size(refs[w], kinds[w], shapes[w], 3)
            copy(w, three, 2 * w + 1, sib).wait_recv()
            copy(w, three, 2 * w, sib).wait_send()
            copy(w, three, 2 * w + 1, sib).wait_send()

    pl.kernel(
        body, out_type=(), mesh=plsc.ScalarSubcoreMesh(axis_name="seq", num_cores=1), name=name,
        scratch_types=[pltpu.SemaphoreType.DMA((2 * n,)), pltpu.SemaphoreType.DMA((2 * n,))],
        compiler_params=pltpu.CompilerParams(collective_id=collective_id),
    )()
    return [r[...] for r in refs]


def _half_of(ref, kind, half):
    r, cn = ref.shape
    if kind == "col":
        return ref.at[pl.ds(pl.multiple_of(half * (r // 2), 16), r // 2), :]
    return ref.at[:, pl.ds(pl.multiple_of(half * (cn // 2), LANES), cn // 2)]


def _shard_of(ref, kind, chip):
    r, cn = ref.shape
    if kind == "col":
        return ref.at[:, pl.ds(pl.multiple_of(chip * (cn // N_CHIPS), LANES), cn // N_CHIPS)]
    return ref.at[pl.ds(pl.multiple_of(chip * (r // N_CHIPS), 16), r // N_CHIPS), :]


def _exchange_halves(grads, kinds, name):
    n = len(grads)

    def body(*refs):
        gs = refs[:n]
        ts = refs[n:2 * n]
        send_sems, recv_sems = refs[2 * n:]
        x, y, c = _xyc()
        cps = []
        for w in range(n):
            cp = pltpu.make_async_remote_copy(
                src_ref=_half_of(gs[w], kinds[w], 1 - c), dst_ref=ts[w], send_sem=send_sems.at[w],
                recv_sem=recv_sems.at[w], device_id=(x, y, 1 - c), device_id_type=MESH)
            cp.start()
            cps.append(cp)
        for cp in cps:
            cp.wait()

    out_shape = []
    for gr, kind in zip(grads, kinds):
        r, cn = gr.shape
        out_shape.append(jax.ShapeDtypeStruct((r // 2, cn) if kind == "col" else (r, cn // 2), gr.dtype))
    return pl.pallas_call(
        body, name=name, in_specs=[ANY] * n, out_specs=[ANY] * n, out_shape=out_shape,
        scratch_shapes=[pltpu.SemaphoreType.DMA((n,)), pltpu.SemaphoreType.DMA((n,))],
    )(*grads)


def _exchange_halves_seq(grads, kinds, name, collective_id):
    n = len(grads)

    def body(*refs):
        gs = refs[:n]
        ts = refs[n:2 * n]
        send_sems, recv_sems = refs[2 * n:]
        x, y, c = _xyc()
        sib = (x, y, 1 - c)
        barrier = pltpu.get_barrier_semaphore()
        pl.semaphore_signal(barrier, inc=1, device_id=sib, device_id_type=MESH)
        pl.semaphore_wait(barrier, 1)
        cps = []
        for w in range(n):
            cp = pltpu.make_async_remote_copy(
                src_ref=_half_of(gs[w], kinds[w], 1 - c), dst_ref=ts[w], send_sem=send_sems.at[w],
                recv_sem=recv_sems.at[w], device_id=sib, device_id_type=MESH)
            cp.start()
            cps.append(cp)
        for cp in cps:
            cp.wait()

    out_type = []
    for gr, kind in zip(grads, kinds):
        r, cn = gr.shape
        out_type.append(jax.ShapeDtypeStruct((r // 2, cn) if kind == "col" else (r, cn // 2), gr.dtype))
    return pl.kernel(
        body, out_type=out_type, mesh=plsc.ScalarSubcoreMesh(axis_name="seq", num_cores=1), name=name,
        scratch_types=[pltpu.SemaphoreType.DMA((n,)), pltpu.SemaphoreType.DMA((n,))],
        compiler_params=pltpu.CompilerParams(collective_id=collective_id),
    )(*grads)


def _scatter_partials(parts, kinds, name):
    n = len(parts)

    def body(*refs):
        ps = refs[:n]
        us = refs[n:2 * n]
        send_sems, recv_sems = refs[2 * n:]
        x, y, c = _xyc()
        others = _other_chips(x, y)
        cps = []
        for w in range(n):
            for j, (ox, oy) in enumerate(others):
                cp = pltpu.make_async_remote_copy(
                    src_ref=_shard_of(ps[w], kinds[w], 2 * ox + oy), dst_ref=us[w].at[j],
                    send_sem=send_sems.at[3 * w + j], recv_sem=recv_sems.at[3 * w + j],
                    device_id=(ox, oy, c), device_id_type=MESH)
                cp.start()
                cps.append(cp)
        for cp in cps:
            cp.wait()

    out_shape = []
    for p, kind in zip(parts, kinds):
        r, cn = p.shape
        hs = (r, cn // N_CHIPS) if kind == "col" else (r // N_CHIPS, cn)
        out_shape.append(jax.ShapeDtypeStruct((N_CHIPS - 1,) + hs, p.dtype))
    return pl.pallas_call(
        body, name=name, in_specs=[ANY] * n, out_specs=[ANY] * n, out_shape=out_shape,
        scratch_shapes=[pltpu.SemaphoreType.DMA((3 * n,)), pltpu.SemaphoreType.DMA((3 * n,))],
    )(*parts)


def _scatter_partials_seq(parts, kinds, name, collective_id):
    n = len(parts)

    def body(*refs):
        ps = refs[:n]
        us = refs[n:2 * n]
        send_sems, recv_sems = refs[2 * n:]
        x, y, c = _xyc()
        others = _other_chips(x, y)
        barrier = pltpu.get_barrier_semaphore()
        for ox, oy in others:
            pl.semaphore_signal(barrier, inc=1, device_id=(ox, oy, c), device_id_type=MESH)
        pl.semaphore_wait(barrier, len(others))
        for w in range(n):
            for j, (ox, oy) in enumerate(others):
                pltpu.make_async_remote_copy(
                    src_ref=_shard_of(ps[w], kinds[w], 2 * ox + oy), dst_ref=us[w].at[j],
                    send_sem=send_sems.at[w], recv_sem=recv_sems.at[w],
                    device_id=(ox, oy, c), device_id_type=MESH).start()
        for w in range(n):
            pltpu.make_async_remote_copy(
                src_ref=us[w], dst_ref=us[w], send_sem=send_sems.at[w], recv_sem=recv_sems.at[w],
                device_id=(x, y, c), device_id_type=MESH).wait()

    out_type = []
    for p, kind in zip(parts, kinds):
        r, cn = p.shape
        hs = (r, cn // N_CHIPS) if kind == "col" else (r // N_CHIPS, cn)
        out_type.append(jax.ShapeDtypeStruct((N_CHIPS - 1,) + hs, p.dtype))
    return pl.kernel(
        body, out_type=out_type, mesh=plsc.ScalarSubcoreMesh(axis_name="seq", num_cores=1), name=name,
        scratch_types=[pltpu.SemaphoreType.DMA((n,)), pltpu.SemaphoreType.DMA((n,))],
        compiler_params=pltpu.CompilerParams(collective_id=collective_id),
    )(*parts)


def _join_halves(halves, kinds, name):
    n = len(halves)

    def body(*refs):
        outs = refs[n:2 * n]
        send_sems, recv_sems = refs[2 * n:]
        x, y, c = _xyc()
        cps = []
        for w in range(n):
            mine = _half_of(outs[w], kinds[w], c)
            cp = pltpu.make_async_remote_copy(
                src_ref=mine, dst_ref=mine, send_sem=send_sems.at[w], recv_sem=recv_sems.at[w],
                device_id=(x, y, 1 - c), device_id_type=MESH)
            cp.start()
            cps.append(cp)
        for w in range(n):
            theirs = _half_of(outs[w], kinds[w], 1 - c)
            pltpu.make_async_remote_copy(
                src_ref=theirs, dst_ref=theirs, send_sem=send_sems.at[w], recv_sem=recv_sems.at[w],
                device_id=(x, y, 1 - c), device_id_type=MESH).wait_recv()
        _drain(cps)

    return pl.pallas_call(
        body, name=name, in_specs=[ANY] * n, out_specs=[ANY] * n,
        out_shape=[jax.ShapeDtypeStruct(h.shape, h.dtype) for h in halves],
        input_output_aliases={w: w for w in range(n)},
        scratch_shapes=[pltpu.SemaphoreType.DMA((n,)), pltpu.SemaphoreType.DMA((n,))],
    )(*halves)


def kernel(x, a_norm, a_w_in, a_conv_w, a_conv_b, a_ln_g, a_ln_b, a_w_out, kv_norm, w_kv, b_norm, b_w_in, b_w_out, rel_bias, final_norm, loss_target, m_a_norm, m_a_w_in, m_a_conv_w, m_a_conv_b, m_a_ln_g, m_a_ln_b, m_a_w_out, m_kv_norm, m_w_kv, m_b_norm, m_b_w_in, m_b_w_out, m_rel_bias, m_final_norm, v_a_norm, v_a_w_in, v_a_conv_w, v_a_conv_b, v_a_ln_g, v_a_ln_b, v_a_w_out, v_kv_norm, v_w_kv, v_b_norm, v_b_w_in, v_b_w_out, v_rel_bias, v_final_norm):
    S, D = x.shape[1], x.shape[2]
    E = a_w_out.shape[1] * N_CHIPS
    A = b_w_out.shape[1] * N_CHIPS
    H = A // HEAD_DIM
    DC = D // N_CHIPS
    xs = x.reshape(S, D)
    tgt = loss_target.reshape(S, D)
    cx, cy, cc = _xyc()
    chip = 2 * cx + cy
    c_idx = jnp.reshape(cc, (1,)).astype(jnp.int32)

    big_names = ["a_w_in", "a_w_out", "w_kv", "b_w_in", "b_w_out"]
    kinds = ["col", "row", "col", "col", "row"]
    big_w = [a_w_in[0], a_w_out[0], w_kv, b_w_in[0], b_w_out[0]]
    big_m = [m_a_w_in[0], m_a_w_out[0], m_w_kv, m_b_w_in[0], m_b_w_out[0]]
    big_v = [v_a_w_in[0], v_a_w_out[0], v_w_kv, v_b_w_in[0], v_b_w_out[0]]
    chip_idx = jnp.reshape(chip, (1,)).astype(jnp.int32)
    placed = [_cast_into_full(big_w[w], kinds[w], chip_idx, "cast_" + big_names[w]) for w in range(5)]
    shard_shapes = [w.shape for w in big_w]
    (wa_in,) = _allgather_weights_seq(placed[0:1], kinds[0:1], shard_shapes[0:1], "ag_seq_a_in", 0)
    (wa_out,) = _allgather_weights_seq(placed[1:2], kinds[1:2], shard_shapes[1:2], "ag_seq_a_out", 1)
    (wkv,) = _allgather_weights_seq(placed[2:3], kinds[2:3], shard_shapes[2:3], "ag_seq_kv", 5)
    wb_in, wb_out = _allgather_weights_seq(placed[3:5], kinds[3:5], shard_shapes[3:5], "ag_seq_b", 2)

    def row_at(vec, q):
        return jnp.pad(vec, ((q, 7 - q), (0, 0)))

    def pack_sharded(an, cw, cb, lg, lb):
        return jnp.concatenate([row_at(an, 0), jnp.pad(cw[0], ((0, 1), (0, 0))),
                                row_at(lg, 0) + row_at(lb, 1) + row_at(cb, 2)], axis=0)

    small_w = pack_sharded(a_norm, a_conv_w, a_conv_b, a_ln_g, a_ln_b)
    gathered = _allgather_small(small_w, "ag_small_params")
    small_full = jnp.concatenate([gathered[2 * k] for k in range(N_CHIPS)], axis=1)
    g_a = small_full[0:1]
    conv_w32 = small_full[8:8 + HALO]
    ln_g = small_full[40:41]
    ln_b = small_full[41:42]
    conv_b = small_full[42:43]
    g_kv = kv_norm.reshape(1, D)
    g_b = b_norm.reshape(1, D)
    g_f = final_norm.reshape(1, D)

    rb_t = jnp.pad(rel_bias.T, ((0, 0), (0, LANES - N_BUCKETS)))
    onehots = [_onehot(dil) for _, dil in GROUPS]
    biases = [_bias_table(rb_t, onehots[g], "bias_table_%d" % g).reshape(H, BLOCK, 2 * BLOCK)
              for g in range(len(GROUPS))]

    dils = tuple(dil for _, dil in GROUPS)
    assert dils[0] == 1
    n_g = len(GROUPS)
    ((h0,),) = _rms_fwd(xs, [g_a], (1,), "rms_a")
    proj3 = _matmul(h0, wa_in, "nn", BF16, "mm_a_in", out_slab=E)
    conv = _conv_fwd(proj3, conv_w32, conv_b, "conv_fwd")
    y_a = _ln_gate_fwd(conv, proj3, ln_g, ln_b, "ln_gate_fwd")
    x1 = _matmul(y_a, wa_out, "nn", F32, "mm_a_out", res=xs)
    hks, hbs = _rms_fwd(x1, [g_kv, g_b], dils, "rms_kv_b")
    kvs = [_matmul(hks[g], wkv, "nn", BF16, "mm_kv_%d" % g, out_slab=A, b_off=2 * g * A, n_cols=2 * A)
           for g in range(n_g)]
    qs = [_matmul(hbs[g], wb_in, "nn", BF16, "mm_q_%d" % g, b_off=g * A, n_cols=A, after=kvs)
          for g in range(n_g)]
    zb = _matmul(hbs[0], wb_in, "nn", BF16, "mm_zb", b_off=n_g * A, n_cols=A, after=kvs)
    os_, lses = [], []
    for g, dil in enumerate(dils):
        o_g, lse_g = _attn_fwd(qs[g], kvs[g], biases[g], dil, "attn_fwd_%d" % g)
        os_.append(o_g)
        lses.append(lse_g)
    y_b, o_m, lse_d = _attn_merge(os_, lses, zb, dils, "attn_merge")
    x2 = _matmul(y_b, wb_out, "nn", F32, "mm_b_out", res=x1)
    loss_part, dx2, dx2b, gg_f = _final_head(x2, g_f, tgt, "final_head")
    loss = lax.psum(loss_part[0, 0], ("x", "y", "c"))

    dw_tiles = dict(tm=512, tn=1024, tk=4096)
    dy_b = _matmul(dx2b, wb_out, "nt", BF16, "mm_b_out_dx", after=loss.reshape(1, 1))
    dwb_out = _matmul(y_b, dx2b, "tn", BF16, "mm_b_out_dw", **dw_tiles)
    dos, dhs, dzb = _gate_bwd(dy_b, o_m, zb, dils, "gate_bwd")
    dbs, cots = [], []
    dwb_in = dwkv = None
    for g, dil in enumerate(dils):
        dq, dkv, db = _attn_bwd(qs[g], kvs[g], dos[g], lse_d[g], dhs[g], biases[g], dil, "attn_bwd_%d" % g)
        dbs.append(db.reshape(H, BLOCK * 2 * BLOCK))
        dwb_in = _matmul(hbs[g], dq, "tn", BF16, "mm_q_dw_%d" % g, out_off=g * A, out_cols=(n_g + 1) * A,
                         out_alias=dwb_in, **dw_tiles)
        dwkv = _matmul(hks[g], dkv, "tn", BF16, "mm_kv_dw_%d" % g, b_slab=True, out_off=2 * g * A,
                       out_cols=2 * n_g * A, out_alias=dwkv, **dw_tiles)
        cots.append((_matmul(dkv, wkv, "nt", BF16, "mm_kv_dx_%d" % g, a_slab=True, b_off=2 * g * A), 0, dil))
        cots.append((_matmul(dq, wb_in, "nt", BF16, "mm_q_dx_%d" % g, b_off=g * A), 1, dil))
    dwb_in = _matmul(hbs[0], dzb, "tn", BF16, "mm_zb_dw", out_off=n_g * A, out_cols=(n_g + 1) * A,
                     out_alias=dwb_in, **dw_tiles)
    cots.append((_matmul(dzb, wb_in, "nt", BF16, "mm_zb_dx", b_off=n_g * A), 1, 1))
    chip_c = jnp.stack([chip, cc]).astype(jnp.int32)

    def scatter_group(idx, grads, tag, collective_id, exchange_id=None, behind=None):
        ks = [kinds[w] for w in idx]
        if exchange_id is None:
            theirs = _exchange_halves(grads, ks, "rs_exchange_" + tag)
        else:
            theirs = _exchange_halves_seq(grads, ks, "rs_exchange_seq_" + tag, exchange_id)
        parts = [_add_half(grads[q], theirs[q], c_idx, ks[q], "rs_add_half_%d" % w, after=behind)
                 for q, w in enumerate(idx)]
        return parts, _scatter_partials_seq(parts, ks, "rs_seq_" + tag, collective_id)

    def reduce_group(idx, parts, slots, tag):
        ks = [kinds[w] for w in idx]
        halves = [_sum_into_shard(parts[q], slots[q], chip_c, ks[q], "rs_sum_chips_%d" % w)
                  for q, w in enumerate(idx)]
        return _join_halves(halves, ks, "rs_join_" + tag)

    parts_b, slots_b = scatter_group([2, 3, 4], [dwkv, dwb_in, dwb_out], "b", 3, exchange_id=6,
                                     behind=[ct[0] for ct in cots])
    g_rel_t = _bias_grad(dbs, onehots, "bias_grad")
    dx1, dx1b, gg_kvb = _rms_bwd(x1, cots, [g_kv, g_b], dx2, "rms_kv_b_bwd", after=parts_b)
    dy_a = _matmul(dx1b, wa_out, "nt", BF16, "mm_a_out_dx")
    dwa_out = _matmul(y_a, dx1b, "tn", BF16, "mm_a_out_dw", **dw_tiles)
    dconv, dproj3, gg_ln = _ln_gate_bwd(conv, proj3, dy_a, ln_g, ln_b, "ln_gate_bwd")
    dproj3, g_conv_w = _conv_bwd(proj3, dconv, conv_w32, dproj3, "conv_bwd")
    dwa_in = _matmul(h0, dproj3, "tn", BF16, "mm_a_in_dw", b_slab=True, **dw_tiles)
    parts_a, slots_a = scatter_group([0, 1], [dwa_in, dwa_out], "a", 4)
    dh0 = _matmul(dproj3, wa_in, "nt", BF16, "mm_a_in_dx", a_slab=True, after=parts_a, tn=512)
    grad_x, _, gg_a = _rms_bwd(xs, [(dh0, 0, 1)], [g_a], dx1, "rms_a_bwd")

    big_g = [None] * 5
    big_g[2:5] = reduce_group([2, 3, 4], parts_b, slots_b, "b")
    big_g[0:2] = reduce_group([0, 1], parts_a, slots_a, "a")

    def rel_rows(rb):
        return jnp.pad(rb.reshape(1, N_BUCKETS * H), ((0, 7), (0, D - N_BUCKETS * H)))

    small_g = jnp.concatenate([gg_a, g_conv_w, gg_ln, gg_kvb, gg_f, rel_rows(g_rel_t[:, :N_BUCKETS].T)], axis=0)
    small_sum = _sum_leading(_allgather_small(small_g, "ag_small_grads", after=[slots_a[0], slots_b[0]]), F32,
                             "sum_small_grads", tr=72)
    g_sharded = lax.dynamic_slice(small_sum, (0, chip * DC), (48, DC))
    g_repl = small_sum[48:72]

    outs_g, outs_d, outs_m, outs_v = {}, {}, {}, {}
    for w, nm in enumerate(big_names):
        d_, m_, v_ = _adamw(big_w[w], big_g[w], big_m[w], big_v[w], "adamw_" + nm)
        outs_g[nm], outs_d[nm], outs_m[nm], outs_v[nm] = big_g[w], d_, m_, v_
    sm_m = pack_sharded(m_a_norm, m_a_conv_w, m_a_conv_b, m_a_ln_g, m_a_ln_b)
    sm_v = pack_sharded(v_a_norm, v_a_conv_w, v_a_conv_b, v_a_ln_g, v_a_ln_b)
    sd, smm, svv = _adamw(small_w, g_sharded, sm_m, sm_v, "adamw_small_sharded")

    def unpack_sharded(p):
        return {"a_norm": p[0:1], "a_conv_w": p[8:8 + CONV_TAPS].reshape(1, CONV_TAPS, DC), "a_ln_g": p[40:41],
                "a_ln_b": p[41:42], "a_conv_b": p[42:43]}

    for src, dst in ((g_sharded, outs_g), (sd, outs_d), (smm, outs_m), (svv, outs_v)):
        dst.update(unpack_sharded(src))

    def pack_repl(kn, bn, fn, rb):
        return jnp.concatenate([row_at(kn.reshape(1, D), 0) + row_at(bn.reshape(1, D), 1),
                                row_at(fn.reshape(1, D), 0), rel_rows(rb)], axis=0)

    rp_w = pack_repl(kv_norm, b_norm, final_norm, rel_bias)
    rp_m = pack_repl(m_kv_norm, m_b_norm, m_final_norm, m_rel_bias)
    rp_v = pack_repl(v_kv_norm, v_b_norm, v_final_norm, v_rel_bias)
    rd, rmm, rvv = _adamw(rp_w, g_repl, rp_m, rp_v, "adamw_small_replicated")

    def unpack_repl(p):
        return {"kv_norm": p[0], "b_norm": p[1:2], "final_norm": p[8],
                "rel_bias": p[16, :N_BUCKETS * H].reshape(N_BUCKETS, H)}

    for src, dst in ((g_repl, outs_g), (rd, outs_d), (rmm, outs_m), (rvv, outs_v)):
        dst.update(unpack_repl(src))

    order = ["a_norm", "a_w_in", "a_conv_w", "a_conv_b", "a_ln_g", "a_ln_b", "a_w_out", "kv_norm", "w_kv",
             "b_norm", "b_w_in", "b_w_out", "rel_bias", "final_norm"]
    lead = {"a_w_in", "a_w_out", "b_w_in", "b_w_out"}

    def shaped(nm, val):
        return val[None] if nm in lead else val

    result = [loss, grad_x.reshape(1, S, D)]
    for table in (outs_g, outs_d, outs_m, outs_v):
        result.extend(shaped(nm, table[nm]) for nm in order)
    return tuple(result)
```

```python
import functools

import numpy as np
import jax
import jax.numpy as jnp
from jax import lax
from jax.experimental import pallas as pl
from jax.experimental.pallas import tpu as pltpu
from jax.experimental.pallas import tpu_sc as plsc

F32 = jnp.float32
BF16 = jnp.bfloat16
MESH = pl.DeviceIdType.MESH
ANY = pl.BlockSpec(memory_space=pl.ANY)
VMEM_SPEC = pl.BlockSpec(memory_space=pltpu.VMEM)

EPS = 1e-6
HEAD_DIM = 128
BLOCK = 128
GROUPS = ((128, 1), (512, 4), (2048, 16))
SCALE = HEAD_DIM ** -0.5
CONV_TAPS = 31
HALO = 32
N_BUCKETS = 32
MAX_EXACT = 16
MAX_DISTANCE = 2048
NEG = -1e30
PRODUCTS_AHEAD = 2
SCORES_AHEAD = 4
N_CHIPS = 4
N_DEV = 8
LANES = 128
VMEM_LIMIT = 56 * 1024 * 1024

ADAM_LR = 0.001
ADAM_B1 = 0.9
ADAM_B2 = 0.999
ADAM_EPS = 1e-08
ADAM_WD = 0.01
ADAM_STEP = 10


def _tile(n, pref, mult=LANES):
    t = (min(pref, n) // mult) * mult
    while t >= mult:
        if n % t == 0:
            return t
        t -= mult
    return n


def _params(*sem):
    return pltpu.CompilerParams(dimension_semantics=sem, vmem_limit_bytes=VMEM_LIMIT)


def _sigmoid(v):
    return 1.0 / (1.0 + jnp.exp(-v))


def _dot(a, b, dims):
    return lax.dot_general(a, b, (dims, ((), ())), preferred_element_type=F32)


NN = ((1,), (0,))
NT = ((1,), (1,))
TN = ((0,), (0,))


def _as_list(after):
    if after is None:
        return []
    return list(after) if isinstance(after, (list, tuple)) else [after]


def _stack_rows(rows, total):
    width = rows[0].shape[1]
    rid = lax.broadcasted_iota(jnp.int32, (total, width), 0)
    out = jnp.zeros((total, width), F32)
    for q, row in enumerate(rows):
        out = jnp.where(rid == q, jnp.broadcast_to(row, (total, width)), out)
    return out


def _lane_col(arr, h, lane):
    return jnp.sum(jnp.where(lane == h, arr, 0.0), axis=-1, keepdims=True)


def _matmul(a, b, mode, out_dtype, name, res=None, a_slab=False, b_slab=False, out_slab=0,
            b_off=0, n_cols=None, out_off=0, out_cols=None, out_alias=None, after=None,
            tm=1024, tn=1024, tk=2048):
    if a_slab:
        na, M, W = a.shape
        K = na * W
    elif mode == "tn":
        K, M = a.shape
    else:
        M, K = a.shape
    if b_slab:
        nbs, _, Wb = b.shape
        N = nbs * Wb
    elif mode == "nt":
        N = b.shape[0]
    else:
        N = n_cols if n_cols else b.shape[1]
    tm = _tile(M, tm)
    tn = _tile(Wb if b_slab else (out_slab if out_slab else N), tn)
    tk = _tile(W if a_slab else K, tk)
    all_slabs = a_slab and mode == "nt" and tk == W
    if all_slabs:
        tk = K
    nk = K // tk
    grid = (M // tm, N // tn, nk)
    bo = b_off // (tk if mode == "nt" else tn)
    oo = out_off // tn

    if all_slabs:
        a_spec = pl.BlockSpec((na, tm, W), lambda i, j, k: (0, i, 0))
    elif a_slab:
        per = W // tk
        a_spec = pl.BlockSpec((None, tm, tk), lambda i, j, k: (k // per, i, k % per))
    elif mode == "tn":
        a_spec = pl.BlockSpec((tk, tm), lambda i, j, k: (k, i))
    else:
        a_spec = pl.BlockSpec((tm, tk), lambda i, j, k: (i, k))
    if b_slab:
        perb = Wb // tn
        b_spec = pl.BlockSpec((None, tk, tn), lambda i, j, k: (j // perb, k, j % perb))
    elif mode == "nt":
        b_spec = pl.BlockSpec((tn, tk), lambda i, j, k: (j, k + bo))
    else:
        b_spec = pl.BlockSpec((tk, tn), lambda i, j, k: (k, j + bo))
    if out_slab:
        pero = out_slab // tn
        o_spec = pl.BlockSpec((None, tm, tn), lambda i, j, k: (j // pero, i, j % pero))
        out_shape = jax.ShapeDtypeStruct((N // out_slab, M, out_slab), out_dtype)
    else:
        o_spec = pl.BlockSpec((tm, tn), lambda i, j, k: (i, j + oo))
        out_shape = jax.ShapeDtypeStruct((M, out_cols if out_cols else N), out_dtype)
    in_specs = [a_spec, b_spec]
    operands = [a, b]
    if res is not None:
        in_specs.append(pl.BlockSpec((tm, tn), lambda i, j, k: (i, j)))
        operands.append(res)
    aliases = {}
    if out_alias is not None:
        aliases[len(operands)] = 0
        in_specs.append(ANY)
        operands.append(out_alias)
    for arr in _as_list(after):
        in_specs.append(ANY)
        operands.append(arr)
    dims = {"nn": NN, "nt": NT, "tn": TN}[mode]
    has_res = res is not None
    n_in = len(operands)

    def body(*refs):
        a_ref, b_ref = refs[0], refs[1]
        r_ref = refs[2] if has_res else None
        o_ref = refs[n_in]
        if all_slabs:
            prod = _dot(a_ref[0], b_ref[:, 0:W], dims)
            for q in range(1, na):
                prod = prod + _dot(a_ref[q], b_ref[:, q * W:(q + 1) * W], dims)
        else:
            prod = _dot(a_ref[...], b_ref[...], dims)

        def finish(val):
            if has_res:
                val = val + r_ref[...]
            o_ref[...] = val.astype(out_dtype)

        if nk == 1:
            finish(prod)
        else:
            acc_ref = refs[n_in + 1]
            k = pl.program_id(2)

            @pl.when(k == 0)
            def _():
                acc_ref[...] = prod

            @pl.when(k > 0)
            def _():
                acc_ref[...] += prod

            @pl.when(k == nk - 1)
            def _():
                finish(acc_ref[...])

    scratch = [pltpu.VMEM((tm, tn), F32)] if nk > 1 else []
    return pl.pallas_call(
        body, name=name, grid=grid, in_specs=in_specs, out_specs=o_spec, out_shape=out_shape,
        scratch_shapes=scratch, input_output_aliases=aliases,
        compiler_params=_params("parallel", "parallel", "arbitrary"),
    )(*operands)


def _group_spec(d, ts, width):
    if d == 1:
        return pl.BlockSpec((ts, width), lambda i: (i, 0))
    return pl.BlockSpec((d, ts // d, width), lambda i: (0, i, 0))


def _group_shape(d, S, width, dtype):
    return jax.ShapeDtypeStruct((S, width) if d == 1 else (d, S // d, width), dtype)


def _chunk_buf(ts, width):
    return pltpu.VMEM((width // LANES, ts, LANES), F32)


def _fill_chunks(buf, val):
    for c in range(buf.shape[0]):
        buf[c] = val[:, c * LANES:(c + 1) * LANES]


def _read_chunks(buf):
    return jnp.concatenate([buf[c] for c in range(buf.shape[0])], axis=1)


def _emit_group_order(o_ref, buf, d, dtype):
    n = buf.shape[1] // d
    for r in range(d):
        for c in range(buf.shape[0]):
            o_ref[r, :, c * LANES:(c + 1) * LANES] = buf[c, pl.ds(r, n, stride=d), :].astype(dtype)


def _store_token_order(buf, i_ref, d):
    n = buf.shape[1] // d
    for r in range(d):
        for c in range(buf.shape[0]):
            buf[c, pl.ds(r, n, stride=d), :] = i_ref[r, :, c * LANES:(c + 1) * LANES].astype(F32)


def _rms_fwd(x, gains, dils, name, ts=256):
    S, D = x.shape
    ts = _tile(S, ts, 16 * max(dils))
    n = len(gains)
    nd = len(dils)

    def body(*refs):
        buf = refs[1 + n + n * nd]
        xv = refs[0][...]
        nrm = xv * lax.rsqrt(jnp.mean(xv * xv, axis=-1, keepdims=True) + EPS)
        for q in range(n):
            val = nrm * refs[1 + q][...]
            if max(dils) > 1:
                _fill_chunks(buf, val)
            for e, d in enumerate(dils):
                if d == 1:
                    refs[1 + n + q * nd + e][...] = val.astype(BF16)
                else:
                    _emit_group_order(refs[1 + n + q * nd + e], buf, d, BF16)

    row = pl.BlockSpec((ts, D), lambda i: (i, 0))
    vec = pl.BlockSpec((1, D), lambda i: (0, 0))
    outs = pl.pallas_call(
        body, name=name, grid=(S // ts,), in_specs=[row] + [vec] * n,
        out_specs=[_group_spec(d, ts, D) for _ in range(n) for d in dils],
        out_shape=[_group_shape(d, S, D, BF16) for _ in range(n) for d in dils],
        scratch_shapes=[_chunk_buf(ts, D)],
        compiler_params=_params("parallel"),
    )(x, *gains)
    return [[outs[q * nd + e].reshape(S, D) for e in range(nd)] for q in range(n)]


def _rms_bwd(x, cots, gains, dres, name, after=None, ts=256):
    S, D = x.shape
    ts = _tile(S, ts, 16 * max(d for _, _, d in cots))
    n = len(cots)
    ng = len(gains)
    extra = _as_list(after)
    n_in = 2 + n + ng + len(extra)

    def body(*refs):
        x_ref = refs[0]
        dh_refs = refs[1:1 + n]
        g_refs = refs[1 + n:1 + n + ng]
        dres_ref = refs[1 + n + ng]
        dx_ref, dxb_ref, gg_ref, buf = refs[n_in:n_in + 4]
        i = pl.program_id(0)
        xv = x_ref[...]
        r = lax.rsqrt(jnp.mean(xv * xv, axis=-1, keepdims=True) + EPS)
        nrm = xv * r
        dn = jnp.zeros_like(xv)
        rows = [jnp.zeros((1, D), F32) for _ in range(ng)]
        for q, (_, gi, d) in enumerate(cots):
            if d == 1:
                dh = dh_refs[q][...].astype(F32)
            else:
                _store_token_order(buf, dh_refs[q], d)
                dh = _read_chunks(buf)
            dn = dn + dh * g_refs[gi][...]
            rows[gi] = rows[gi] + jnp.sum(dh * nrm, axis=0, keepdims=True)
        dx = dres_ref[...] + r * (dn - nrm * jnp.mean(dn * nrm, axis=-1, keepdims=True))
        dx_ref[...] = dx
        dxb_ref[...] = dx.astype(BF16)
        upd = _stack_rows(rows, 8)

        @pl.when(i == 0)
        def _():
            gg_ref[...] = upd

        @pl.when(i > 0)
        def _():
            gg_ref[...] += upd

    row = pl.BlockSpec((ts, D), lambda i: (i, 0))
    vec = pl.BlockSpec((1, D), lambda i: (0, 0))
    acc = pl.BlockSpec((8, D), lambda i: (0, 0))
    return pl.pallas_call(
        body, name=name, grid=(S // ts,),
        in_specs=[row] + [_group_spec(d, ts, D) for _, _, d in cots] + [vec] * ng + [row] + [ANY] * len(extra),
        out_specs=[row, row, acc],
        out_shape=[jax.ShapeDtypeStruct((S, D), F32), jax.ShapeDtypeStruct((S, D), BF16),
                   jax.ShapeDtypeStruct((8, D), F32)],
        scratch_shapes=[_chunk_buf(ts, D)],
        compiler_params=_params("arbitrary"),
    )(x, *[a if d == 1 else a.reshape(d, S // d, D) for a, _, d in cots], *gains, dres, *extra)


def _final_head(x2, gain, target, name, ts=256):
    S, D = x2.shape
    ts = _tile(S, ts, 16)

    def body(x_ref, g_ref, t_ref, loss_ref, dx_ref, dxb_ref, gg_ref):
        i = pl.program_id(0)
        xv = x_ref[...]
        g = g_ref[...]
        r = lax.rsqrt(jnp.mean(xv * xv, axis=-1, keepdims=True) + EPS)
        nrm = xv * r
        err = nrm * g - t_ref[...]
        part = 0.5 * jnp.sum(jnp.mean(err * err, axis=-1, keepdims=True), axis=0, keepdims=True)
        dout = err * (1.0 / D)
        dn = dout * g
        dx = r * (dn - nrm * jnp.mean(dn * nrm, axis=-1, keepdims=True))
        dx_ref[...] = dx
        dxb_ref[...] = dx.astype(BF16)
        upd = _stack_rows([jnp.sum(dout * nrm, axis=0, keepdims=True)], 8)
        lpart = jnp.broadcast_to(part, (1, LANES))

        @pl.when(i == 0)
        def _():
            gg_ref[...] = upd
            loss_ref[...] = lpart

        @pl.when(i > 0)
        def _():
            gg_ref[...] += upd
            loss_ref[...] += lpart

    row = pl.BlockSpec((ts, D), lambda i: (i, 0))
    vec = pl.BlockSpec((1, D), lambda i: (0, 0))
    return pl.pallas_call(
        body, name=name, grid=(S // ts,), in_specs=[row, vec, row],
        out_specs=[pl.BlockSpec((1, LANES), lambda i: (0, 0)), row, row, pl.BlockSpec((8, D), lambda i: (0, 0))],
        out_shape=[jax.ShapeDtypeStruct((1, LANES), F32), jax.ShapeDtypeStruct((S, D), F32),
                   jax.ShapeDtypeStruct((S, D), BF16), jax.ShapeDtypeStruct((8, D), F32)],
        compiler_params=_params("arbitrary"),
    )(x2, gain, target)


CONV_ROWS = 64


SUBLANES = 8


def _shifted_buf(ts, cw):
    return pltpu.VMEM((SUBLANES - 1, ts + HALO - SUBLANES, cw), F32)


def _fill_shifted(shifted, buf):
    rows = shifted.shape[1]
    for s in range(1, SUBLANES):
        shifted[s - 1] = buf[s:s + rows, :]


def _window(buf, shifted, off, rows):
    s = off % SUBLANES
    base = off - s
    if s == 0:
        return buf[base:base + rows, :]
    return shifted[s - 1, base:base + rows, :]


def _conv_fwd(proj3, conv_w32, conv_b, name, ts=256, cw=256):
    _, S, E = proj3.shape
    ts = _tile(S, ts, HALO)
    cw = _tile(E, cw)
    per = ts // HALO
    rc = min(CONV_ROWS, ts)

    def body(a_ref, b_ref, ap_ref, bp_ref, w_ref, cb_ref, c_ref, ubuf, shifted):
        i = pl.program_id(0)
        up = ap_ref[...].astype(F32) * _sigmoid(bp_ref[...].astype(F32))
        ubuf[0:HALO, :] = jnp.where(i > 0, up, 0.0)
        ubuf[HALO:HALO + ts, :] = a_ref[...].astype(F32) * _sigmoid(b_ref[...].astype(F32))
        _fill_shifted(shifted, ubuf)
        for r0 in range(0, ts, rc):
            acc = jnp.broadcast_to(cb_ref[...], (rc, cw))
            for k in range(CONV_TAPS):
                off = r0 + HALO - (CONV_TAPS - 1) + k
                acc = acc + _window(ubuf, shifted, off, rc) * w_ref[k:k + 1, :]
            c_ref[r0:r0 + rc, :] = acc

    return pl.pallas_call(
        body, name=name, grid=(S // ts, E // cw),
        in_specs=[
            pl.BlockSpec((None, ts, cw), lambda i, j: (0, i, j)),
            pl.BlockSpec((None, ts, cw), lambda i, j: (1, i, j)),
            pl.BlockSpec((None, HALO, cw), lambda i, j: (0, jnp.maximum(i * per - 1, 0), j)),
            pl.BlockSpec((None, HALO, cw), lambda i, j: (1, jnp.maximum(i * per - 1, 0), j)),
            pl.BlockSpec((HALO, cw), lambda i, j: (0, j)),
            pl.BlockSpec((1, cw), lambda i, j: (0, j)),
        ],
        out_specs=pl.BlockSpec((ts, cw), lambda i, j: (i, j)),
        out_shape=jax.ShapeDtypeStruct((S, E), F32),
        scratch_shapes=[pltpu.VMEM((HALO + ts, cw), F32), _shifted_buf(ts, cw)],
        compiler_params=_params("parallel", "parallel"),
    )(proj3, proj3, proj3, proj3, conv_w32, conv_b)


def _ln_gate_fwd(c, proj3, ln_g, ln_b, name, ts=256):
    S, E = c.shape
    ts = _tile(S, ts, 16)

    def body(c_ref, z_ref, g_ref, b_ref, y_ref):
        cv = c_ref[...]
        mu = jnp.mean(cv, axis=-1, keepdims=True)
        d = cv - mu
        var = jnp.mean(d * d, axis=-1, keepdims=True)
        cn = d * lax.rsqrt(var + EPS) * g_ref[...] + b_ref[...]
        z = z_ref[...].astype(F32)
        y_ref[...] = ((cn * _sigmoid(cn)).astype(F32) * (z * _sigmoid(z))).astype(BF16)

    row = pl.BlockSpec((ts, E), lambda i: (i, 0))
    vec = pl.BlockSpec((1, E), lambda i: (0, 0))
    return pl.pallas_call(
        body, name=name, grid=(S // ts,),
        in_specs=[row, pl.BlockSpec((None, ts, E), lambda i: (2, i, 0)), vec, vec],
        out_specs=row, out_shape=jax.ShapeDtypeStruct((S, E), BF16),
        compiler_params=_params("parallel"),
    )(c, proj3, ln_g, ln_b)


def _ln_gate_bwd(c, proj3, dy, ln_g, ln_b, name, ts=256):
    S, E = c.shape
    ts = _tile(S, ts, 16)

    def body(c_ref, z_ref, dy_ref, g_ref, b_ref, dc_ref, dz_ref, acc_ref):
        i = pl.program_id(0)
        cv = c_ref[...]
        g = g_ref[...]
        mu = jnp.mean(cv, axis=-1, keepdims=True)
        d = cv - mu
        var = jnp.mean(d * d, axis=-1, keepdims=True)
        rstd = lax.rsqrt(var + EPS)
        chat = d * rstd
        cn = chat * g + b_ref[...]
        z = z_ref[...].astype(F32)
        dyv = dy_ref[...].astype(F32)
        sc = _sigmoid(cn)
        sz = _sigmoid(z)
        dcn = dyv * (z * sz) * (sc * (1.0 + cn * (1.0 - sc)))
        dz_ref[...] = (dyv * (cn * sc) * (sz * (1.0 + z * (1.0 - sz)))).astype(BF16)
        dchat = dcn * g
        dcv = rstd * (dchat - jnp.mean(dchat, axis=-1, keepdims=True)
                      - chat * jnp.mean(dchat * chat, axis=-1, keepdims=True))
        dc_ref[...] = dcv
        upd = _stack_rows([jnp.sum(dcn * chat, axis=0, keepdims=True),
                           jnp.sum(dcn, axis=0, keepdims=True),
                           jnp.sum(dcv, axis=0, keepdims=True)], 8)

        @pl.when(i == 0)
        def _():
            acc_ref[...] = upd

        @pl.when(i > 0)
        def _():
            acc_ref[...] += upd

    row = pl.BlockSpec((ts, E), lambda i: (i, 0))
    vec = pl.BlockSpec((1, E), lambda i: (0, 0))
    return pl.pallas_call(
        body, name=name, grid=(S // ts,),
        in_specs=[row, pl.BlockSpec((None, ts, E), lambda i: (2, i, 0)), row, vec, vec],
        out_specs=[row, pl.BlockSpec((None, ts, E), lambda i: (2, i, 0)), pl.BlockSpec((8, E), lambda i: (0, 0))],
        out_shape=[jax.ShapeDtypeStruct((S, E), F32), jax.ShapeDtypeStruct((3, S, E), BF16),
                   jax.ShapeDtypeStruct((8, E), F32)],
        compiler_params=_params("arbitrary"),
    )(c, proj3, dy, ln_g, ln_b)


def _conv_bwd(proj3, dc, conv_w32, dproj3, name, ts=256, cw=256):
    _, S, E = proj3.shape
    ts = _tile(S, ts, HALO)
    cw = _tile(E, cw)
    per = ts // HALO
    n_i = S // ts
    last_halo = S // HALO - 1
    rc = min(CONV_ROWS, ts)

    def body(a_ref, b_ref, dc_ref, dcn_ref, w_ref, dp_in, dab_ref, dw_ref, dcbuf, ubuf, dwacc, shifted):
        del dp_in
        i = pl.program_id(1)
        dcbuf[0:ts, :] = dc_ref[...]
        dcbuf[ts:ts + HALO, :] = jnp.where(i < n_i - 1, dcn_ref[...], 0.0)
        _fill_shifted(shifted, dcbuf)
        av = a_ref[...].astype(F32)
        sb = _sigmoid(b_ref[...].astype(F32))
        ubuf[...] = av * sb

        @pl.when(i == 0)
        def _():
            dwacc[...] = jnp.zeros_like(dwacc)

        for r0 in range(0, ts, rc):
            uv = ubuf[r0:r0 + rc, :]
            du = jnp.zeros((rc, cw), F32)
            for d in range(CONV_TAPS):
                k = CONV_TAPS - 1 - d
                win = _window(dcbuf, shifted, r0 + d, rc)
                du = du + win * w_ref[k:k + 1, :]
                dwacc[k:k + 1, :] += jnp.sum(uv * win, axis=0, keepdims=True)
            a_c = a_ref[r0:r0 + rc, :].astype(F32)
            s_c = _sigmoid(b_ref[r0:r0 + rc, :].astype(F32))
            dab_ref[0, r0:r0 + rc, :] = (du * s_c).astype(BF16)
            dab_ref[1, r0:r0 + rc, :] = (du * a_c * s_c * (1.0 - s_c)).astype(BF16)

        @pl.when(i == n_i - 1)
        def _():
            dw_ref[...] = dwacc[...]

    return pl.pallas_call(
        body, name=name, grid=(E // cw, n_i),
        in_specs=[
            pl.BlockSpec((None, ts, cw), lambda j, i: (0, i, j)),
            pl.BlockSpec((None, ts, cw), lambda j, i: (1, i, j)),
            pl.BlockSpec((ts, cw), lambda j, i: (i, j)),
            pl.BlockSpec((HALO, cw), lambda j, i: (jnp.minimum((i + 1) * per, last_halo), j)),
            pl.BlockSpec((HALO, cw), lambda j, i: (0, j)),
            ANY,
        ],
        out_specs=[pl.BlockSpec((2, ts, cw), lambda j, i: (0, i, j)),
                   pl.BlockSpec((HALO, cw), lambda j, i: (0, j))],
        out_shape=[jax.ShapeDtypeStruct((3, S, E), BF16), jax.ShapeDtypeStruct((HALO, E), F32)],
        scratch_shapes=[pltpu.VMEM((ts + HALO, cw), F32), pltpu.VMEM((ts, cw), F32), pltpu.VMEM((HALO, cw), F32),
                        _shifted_buf(ts, cw)],
        input_output_aliases={5: 0},
        compiler_params=_params("parallel", "arbitrary"),
    )(proj3, proj3, dc, dc, conv_w32, dproj3)


def _bucket_table(dil):
    delta = (np.arange(BLOCK)[:, None] + BLOCK) - np.arange(2 * BLOCK)[None, :]
    dist = np.clip(delta, 0, None) * dil
    large = MAX_EXACT + (np.log(np.maximum(dist, 1).astype(np.float32) / MAX_EXACT)
                         / np.log(MAX_DISTANCE / MAX_EXACT) * (N_BUCKETS - MAX_EXACT)).astype(np.int32)
    large = np.minimum(large, N_BUCKETS - 1)
    return np.where(dist < MAX_EXACT, dist, large).astype(np.int32).reshape(-1)


def _onehot(dil):
    tbl = jnp.asarray(_bucket_table(dil))
    return (tbl[None, :] == jnp.arange(LANES, dtype=jnp.int32)[:, None]).astype(BF16)


def _split3(v):
    hi = v.astype(BF16)
    r1 = v - hi.astype(F32)
    mid = r1.astype(BF16)
    lo = (r1 - mid.astype(F32)).astype(BF16)
    return hi, mid, lo


def _bias_table(rb_t, onehot, name):
    H = rb_t.shape[0]
    N = onehot.shape[1]

    def body(r_ref, oh_ref, o_ref):
        oh = oh_ref[...]
        hi, mid, lo = _split3(r_ref[...])
        o_ref[...] = (_dot(lo, oh, NN) + _dot(mid, oh, NN)) + _dot(hi, oh, NN)

    return pl.pallas_call(
        body, name=name, in_specs=[VMEM_SPEC, VMEM_SPEC], out_specs=VMEM_SPEC,
        out_shape=jax.ShapeDtypeStruct((H, N), F32),
        compiler_params=pltpu.CompilerParams(vmem_limit_bytes=VMEM_LIMIT),
    )(rb_t, onehot)


def _bias_grad(dbs, onehots, name):
    H = dbs[0].shape[0]
    n = len(dbs)

    def body(*refs):
        acc = jnp.zeros((H, LANES), F32)
        for q in range(n):
            oh = refs[n + q][...]
            hi, mid, lo = _split3(refs[q][...])
            acc = acc + ((_dot(lo, oh, NT) + _dot(mid, oh, NT)) + _dot(hi, oh, NT))
        refs[2 * n][...] = acc

    return pl.pallas_call(
        body, name=name, in_specs=[VMEM_SPEC] * (2 * n), out_specs=VMEM_SPEC,
        out_shape=jax.ShapeDtypeStruct((H, LANES), F32),
        compiler_params=pltpu.CompilerParams(vmem_limit_bytes=VMEM_LIMIT),
    )(*dbs, *onehots)


def _pair_mask(has_prev):
    qi = lax.broadcasted_iota(jnp.int32, (BLOCK, 2 * BLOCK), 0)
    ki = lax.broadcasted_iota(jnp.int32, (BLOCK, 2 * BLOCK), 1)
    prev = jnp.logical_and(jnp.logical_and(ki < BLOCK, ki >= qi), has_prev)
    return jnp.logical_or(prev, jnp.logical_and(ki >= BLOCK, ki - BLOCK <= qi))


def _attn_fwd(q, kv, bias, dil, name):
    S, A = q.shape
    H = A // HEAD_DIM
    L = S // dil
    nb = L // BLOCK
    qv = q.reshape(dil, L, A)
    kvv = kv.reshape(2, dil, L, A)

    def body(q_ref, kp_ref, kc_ref, vp_ref, vc_ref, b_ref, o_ref, lse_ref):
        i = pl.program_id(1)
        qi = lax.broadcasted_iota(jnp.int32, (BLOCK, BLOCK), 0)
        ki = lax.broadcasted_iota(jnp.int32, (BLOCK, BLOCK), 1)
        mask = _pair_mask(i > 0)
        lane = lax.broadcasted_iota(jnp.int32, (BLOCK, LANES), 1)
        lse_acc = jnp.zeros((BLOCK, LANES), F32)

        def scores(h):
            sl = slice(h * HEAD_DIM, (h + 1) * HEAD_DIM)
            return _dot(q_ref[:, sl], jnp.concatenate([kp_ref[:, sl], kc_ref[:, sl]], axis=0), NT)

        ahead = [scores(h) for h in range(min(SCORES_AHEAD, H))]
        for h in range(H):
            sl = slice(h * HEAD_DIM, (h + 1) * HEAD_DIM)
            raw = ahead.pop(0)
            if h + SCORES_AHEAD < H:
                ahead.append(scores(h + SCORES_AHEAD))
            s = jnp.where(mask, raw * SCALE + b_ref[h], NEG)
            m = jnp.max(s, axis=-1, keepdims=True)
            p = jnp.exp(s - m)
            den = jnp.sum(p, axis=-1, keepdims=True)
            acc = _dot(p.astype(BF16), jnp.concatenate([vp_ref[:, sl], vc_ref[:, sl]], axis=0), NN)
            o_ref[:, sl] = acc / den
            lse_acc = jnp.where(lane == h, m + jnp.log(den), lse_acc)
        lse_ref[...] = lse_acc

    def blk(slab, prev):
        if prev:
            return pl.BlockSpec((None, None, BLOCK, A), lambda r, i: (slab, r, jnp.maximum(i - 1, 0), 0))
        return pl.BlockSpec((None, None, BLOCK, A), lambda r, i: (slab, r, i, 0))

    o, lse = pl.pallas_call(
        body, name=name, grid=(dil, nb),
        in_specs=[pl.BlockSpec((None, BLOCK, A), lambda r, i: (r, i, 0)),
                  blk(0, True), blk(0, False), blk(1, True), blk(1, False),
                  pl.BlockSpec((H, BLOCK, 2 * BLOCK), lambda r, i: (0, 0, 0))],
        out_specs=[pl.BlockSpec((None, BLOCK, A), lambda r, i: (r, i, 0)),
                   pl.BlockSpec((None, BLOCK, LANES), lambda r, i: (r, i, 0))],
        out_shape=[jax.ShapeDtypeStruct((dil, L, A), F32), jax.ShapeDtypeStruct((dil, L, LANES), F32)],
        compiler_params=_params("parallel", "parallel"),
    )(qv, kvv, kvv, kvv, kvv, bias)
    return o.reshape(S, A), lse.reshape(S, LANES)


def _attn_merge(os_, lses, z, dils, name, ts=256):
    S, A = z.shape
    H = A // HEAD_DIM
    ts = _tile(S, ts, 16 * max(dils))
    n = len(os_)

    def body(*refs):
        z_ref = refs[2 * n]
        y_ref, om_ref = refs[2 * n + 1:2 * n + 3]
        lse_refs = refs[2 * n + 3:3 * n + 3]
        o_refs = refs[3 * n + 3:4 * n + 3]
        l_bufs = refs[4 * n + 3:5 * n + 3]
        lse_buf = refs[5 * n + 3]
        ls = []
        for q, d in enumerate(dils):
            if d == 1:
                ls.append(refs[n + q][...])
            else:
                _store_token_order(o_refs[q], refs[q], d)
                _store_token_order(l_bufs[q], refs[n + q], d)
                ls.append(l_bufs[q][0])
        m = ls[0]
        for q in range(1, n):
            m = jnp.maximum(m, ls[q])
        es = [jnp.exp(v - m) for v in ls]
        den = es[0]
        for q in range(1, n):
            den = den + es[q]
        alphas = [e / den for e in es]
        lse = m + jnp.log(den)
        lse_buf[0] = lse
        for q, d in enumerate(dils):
            if d == 1:
                lse_refs[q][...] = lse
            else:
                _emit_group_order(lse_refs[q], lse_buf, d, F32)
        lane = lax.broadcasted_iota(jnp.int32, (ts, LANES), 1)
        for h in range(H):
            sl = slice(h * HEAD_DIM, (h + 1) * HEAD_DIM)
            om = jnp.zeros((ts, HEAD_DIM), F32)
            for q, d in enumerate(dils):
                o_h = refs[q][:, sl] if d == 1 else o_refs[q][h]
                om = om + _lane_col(alphas[q], h, lane) * o_h
            z = z_ref[:, sl].astype(F32)
            y_ref[:, sl] = (om * (z * _sigmoid(z))).astype(BF16)
            om_ref[:, sl] = om.astype(BF16)

    row = pl.BlockSpec((ts, A), lambda i: (i, 0))
    outs = pl.pallas_call(
        body, name=name, grid=(S // ts,),
        in_specs=[_group_spec(d, ts, A) for d in dils] + [_group_spec(d, ts, LANES) for d in dils] + [row],
        out_specs=[row, row] + [_group_spec(d, ts, LANES) for d in dils],
        out_shape=[jax.ShapeDtypeStruct((S, A), BF16), jax.ShapeDtypeStruct((S, A), BF16)]
        + [_group_shape(d, S, LANES, F32) for d in dils],
        scratch_shapes=[_chunk_buf(ts, A)] * n + [_chunk_buf(ts, LANES)] * (n + 1),
        compiler_params=_params("parallel"),
    )(*[o if d == 1 else o.reshape(d, S // d, A) for o, d in zip(os_, dils)],
      *[v if d == 1 else v.reshape(d, S // d, LANES) for v, d in zip(lses, dils)], z)
    return outs[0], outs[1], [v.reshape(S, LANES) for v in outs[2:]]


def _gate_bwd(dy, om, z, dils, name, ts=256):
    S, A = dy.shape
    H = A // HEAD_DIM
    ts = _tile(S, ts, 16 * max(dils))
    n = len(dils)

    def body(*refs):
        dy_ref, om_ref, z_ref = refs[:3]
        do_refs = refs[3:3 + n]
        dh_refs = refs[3 + n:3 + 2 * n]
        dz_ref = refs[3 + 2 * n]
        do_buf, dh_buf = refs[4 + 2 * n:6 + 2 * n]
        lane = lax.broadcasted_iota(jnp.int32, (ts, LANES), 1)
        acc = jnp.zeros((ts, LANES), F32)
        for h in range(H):
            sl = slice(h * HEAD_DIM, (h + 1) * HEAD_DIM)
            dyv = dy_ref[:, sl].astype(F32)
            omv = om_ref[:, sl].astype(F32)
            zv = z_ref[:, sl].astype(F32)
            sz = _sigmoid(zv)
            dob = (dyv * (zv * sz)).astype(BF16)
            do_buf[h] = dob.astype(F32)
            for q, d in enumerate(dils):
                if d == 1:
                    do_refs[q][:, sl] = dob
            dz_ref[:, sl] = (dyv * omv * (sz * (1.0 + zv * (1.0 - sz)))).astype(BF16)
            acc = jnp.where(lane == h, jnp.sum(dob.astype(F32) * omv, axis=-1, keepdims=True), acc)
        dh_buf[0] = acc
        for q, d in enumerate(dils):
            if d == 1:
                dh_refs[q][...] = acc
            else:
                _emit_group_order(do_refs[q], do_buf, d, BF16)
                _emit_group_order(dh_refs[q], dh_buf, d, F32)

    row = pl.BlockSpec((ts, A), lambda i: (i, 0))
    outs = pl.pallas_call(
        body, name=name, grid=(S // ts,), in_specs=[row, row, row],
        out_specs=[_group_spec(d, ts, A) for d in dils] + [_group_spec(d, ts, LANES) for d in dils] + [row],
        out_shape=[_group_shape(d, S, A, BF16) for d in dils] + [_group_shape(d, S, LANES, F32) for d in dils]
        + [jax.ShapeDtypeStruct((S, A), BF16)],
        scratch_shapes=[_chunk_buf(ts, A), _chunk_buf(ts, LANES)],
        compiler_params=_params("parallel"),
    )(dy, om, z)
    return ([v.reshape(S, A) for v in outs[:n]], [v.reshape(S, LANES) for v in outs[n:2 * n]], outs[2 * n])


def _attn_bwd(q, kv, do, lse, dh, bias, dil, name):
    S, A = q.shape
    H = A // HEAD_DIM
    L = S // dil
    nb = L // BLOCK
    qv = q.reshape(dil, L, A)
    kvv = kv.reshape(2, dil, L, A)
    dov = do.reshape(dil, L, A)
    lsev = lse.reshape(dil, L, LANES)
    dhv = dh.reshape(dil, L, LANES)

    def body(*refs):
        (q_ref, qn_ref, kp_ref, kc_ref, vp_ref, vc_ref, do_ref, don_ref, l_ref, ln_ref, d_ref, dn_ref,
         b_ref) = refs[:13]
        dq_ref, dkv_ref, db_ref = refs[13:16]
        r = pl.program_id(0)
        i = pl.program_id(1)
        qi = lax.broadcasted_iota(jnp.int32, (BLOCK, BLOCK), 0)
        ki = lax.broadcasted_iota(jnp.int32, (BLOCK, BLOCK), 1)
        mask_c = ki <= qi
        band = ki >= qi
        mask_p = jnp.logical_and(band, i > 0)
        mask_n = jnp.logical_and(band, i < nb - 1)
        lane = lax.broadcasted_iota(jnp.int32, (BLOCK, LANES), 1)

        @pl.when(jnp.logical_and(r == 0, i == 0))
        def _():
            db_ref[...] = jnp.zeros_like(db_ref)

        def products(h):
            sl = slice(h * HEAD_DIM, (h + 1) * HEAD_DIM)
            q_i, q_n = q_ref[:, sl], qn_ref[:, sl]
            k_p, k_c = kp_ref[:, sl], kc_ref[:, sl]
            v_p, v_c = vp_ref[:, sl], vc_ref[:, sl]
            do_i, do_n = do_ref[:, sl], don_ref[:, sl]
            return (_dot(q_i, k_c, NT), _dot(do_i, v_c, NT), _dot(q_i, k_p, NT), _dot(do_i, v_p, NT),
                    _dot(q_n, k_c, NT), _dot(do_n, v_c, NT))

        ahead = [products(h) for h in range(min(PRODUCTS_AHEAD, H))]
        for h in range(H):
            sl = slice(h * HEAD_DIM, (h + 1) * HEAD_DIM)
            s1, dp1, s2, dp2, s3, dp3 = ahead.pop(0)
            if h + PRODUCTS_AHEAD < H:
                ahead.append(products(h + PRODUCTS_AHEAD))
            q_i, q_n = q_ref[:, sl], qn_ref[:, sl]
            k_p, k_c = kp_ref[:, sl], kc_ref[:, sl]
            do_i, do_n = do_ref[:, sl], don_ref[:, sl]
            l_i, l_n = _lane_col(l_ref[...], h, lane), _lane_col(ln_ref[...], h, lane)
            d_i, d_n = _lane_col(d_ref[...], h, lane), _lane_col(dn_ref[...], h, lane)
            b_c = b_ref[h, :, BLOCK:]
            b_p = b_ref[h, :, :BLOCK]
            p1 = jnp.exp(jnp.where(mask_c, s1 * SCALE + b_c, NEG) - l_i)
            ds1 = p1 * (dp1 - d_i)
            ds1b = ds1.astype(BF16)
            p1b = p1.astype(BF16)
            p2 = jnp.exp(jnp.where(mask_p, s2 * SCALE + b_p, NEG) - l_i)
            ds2 = p2 * (dp2 - d_i)
            ds2b = ds2.astype(BF16)
            p3 = jnp.exp(jnp.where(mask_n, s3 * SCALE + b_p, NEG) - l_n)
            ds3b = (p3 * (dp3 - d_n)).astype(BF16)
            p3b = p3.astype(BF16)
            dq = _dot(ds1b, k_c, NN) + _dot(ds2b, k_p, NN)
            dk = _dot(ds1b, q_i, TN) + _dot(ds3b, q_n, TN)
            dv = _dot(p1b, do_i, TN) + _dot(p3b, do_n, TN)
            dq_ref[:, sl] = (dq * SCALE).astype(BF16)
            dkv_ref[0, :, sl] = (dk * SCALE).astype(BF16)
            dkv_ref[1, :, sl] = dv.astype(BF16)
            db_ref[h, :, BLOCK:] += ds1
            db_ref[h, :, :BLOCK] += ds2

    def blk(slab, shift):
        if shift < 0:
            return pl.BlockSpec((None, None, BLOCK, A), lambda r, i: (slab, r, jnp.maximum(i - 1, 0), 0))
        return pl.BlockSpec((None, None, BLOCK, A), lambda r, i: (slab, r, i, 0))

    def row(width, shift):
        if shift > 0:
            return pl.BlockSpec((None, BLOCK, width), lambda r, i: (r, jnp.minimum(i + 1, nb - 1), 0))
        return pl.BlockSpec((None, BLOCK, width), lambda r, i: (r, i, 0))

    in_specs = [row(A, 0), row(A, 1), blk(0, -1), blk(0, 0), blk(1, -1), blk(1, 0),
                row(A, 0), row(A, 1), row(LANES, 0), row(LANES, 1), row(LANES, 0), row(LANES, 1),
                pl.BlockSpec((H, BLOCK, 2 * BLOCK), lambda r, i: (0, 0, 0))]
    dq, dkv, db = pl.pallas_call(
        body, name=name, grid=(dil, nb), in_specs=in_specs,
        out_specs=[pl.BlockSpec((None, BLOCK, A), lambda r, i: (r, i, 0)),
                   pl.BlockSpec((2, None, BLOCK, A), lambda r, i: (0, r, i, 0)),
                   pl.BlockSpec((H, BLOCK, 2 * BLOCK), lambda r, i: (0, 0, 0))],
        out_shape=[jax.ShapeDtypeStruct((dil, L, A), BF16), jax.ShapeDtypeStruct((2, dil, L, A), BF16),
                   jax.ShapeDtypeStruct((H, BLOCK, 2 * BLOCK), F32)],
        compiler_params=_params("arbitrary", "arbitrary"),
    )(qv, qv, kvv, kvv, kvv, kvv, dov, dov, lsev, lsev, dhv, dhv, bias)
    return dq.reshape(S, A), dkv.reshape(2, S, A), db


def _sum_leading(stack, out_dtype, name, tr=256, tc=2048):
    n, R, C = stack.shape
    tr = _tile(R, tr, 16)
    tc = _tile(C, tc)

    def body(s_ref, o_ref):
        acc = s_ref[0].astype(F32)
        for q in range(1, n):
            acc = acc + s_ref[q].astype(F32)
        o_ref[...] = acc.astype(out_dtype)

    return pl.pallas_call(
        body, name=name, grid=(R // tr, C // tc),
        in_specs=[pl.BlockSpec((n, tr, tc), lambda i, j: (0, i, j))],
        out_specs=pl.BlockSpec((tr, tc), lambda i, j: (i, j)),
        out_shape=jax.ShapeDtypeStruct((R, C), out_dtype),
        compiler_params=_params("parallel", "parallel"),
    )(stack)


def _add_half(g, t, c_idx, kind, name, after=None, tr=256, tc=2048):
    R, C = t.shape
    tr = _tile(R, tr, 16)
    tc = _tile(C, tc)
    nrb, ncb = R // tr, C // tc
    extra = _as_list(after)

    def body(c_ref, g_ref, t_ref, *rest):
        del c_ref
        o_ref = rest[len(extra)]
        o_ref[...] = (g_ref[...].astype(F32) + t_ref[...].astype(F32)).astype(BF16)

    if kind == "col":
        g_map = lambda i, j, c_ref: (c_ref[0] * nrb + i, j)
    else:
        g_map = lambda i, j, c_ref: (i, c_ref[0] * ncb + j)
    same = lambda i, j, c_ref: (i, j)
    return pl.pallas_call(
        body, name=name,
        grid_spec=pltpu.PrefetchScalarGridSpec(
            num_scalar_prefetch=1, grid=(nrb, ncb),
            in_specs=[pl.BlockSpec((tr, tc), g_map), pl.BlockSpec((tr, tc), same)] + [ANY] * len(extra),
            out_specs=pl.BlockSpec((tr, tc), same)),
        out_shape=jax.ShapeDtypeStruct((R, C), BF16),
        compiler_params=_params("parallel", "parallel"),
    )(c_idx, g, t, *extra)


def _cast_into_full(w, kind, chip_idx, name, tr=256, tc=2048):
    R, C = w.shape
    tr = _tile(R, tr, 16)
    tc = _tile(C, tc)
    nrb, ncb = R // tr, C // tc

    def body(k_ref, w_ref, o_ref):
        del k_ref
        o_ref[...] = w_ref[...].astype(BF16)

    if kind == "col":
        o_map = lambda i, j, k_ref: (i, k_ref[0] * ncb + j)
        full = (R, N_CHIPS * C)
    else:
        o_map = lambda i, j, k_ref: (k_ref[0] * nrb + i, j)
        full = (N_CHIPS * R, C)
    return pl.pallas_call(
        body, name=name,
        grid_spec=pltpu.PrefetchScalarGridSpec(
            num_scalar_prefetch=1, grid=(nrb, ncb),
            in_specs=[pl.BlockSpec((tr, tc), lambda i, j, k_ref: (i, j))],
            out_specs=pl.BlockSpec((tr, tc), o_map)),
        out_shape=jax.ShapeDtypeStruct(full, BF16),
        compiler_params=_params("parallel", "parallel"),
    )(chip_idx, w)


def _sum_into_shard(p, u, idx, kind, name, tr=256, tc=2048):
    _, R, C = u.shape
    tr = _tile(R, tr, 16)
    tc = _tile(C, tc)
    nrb, ncb = R // tr, C // tc

    def body(i_ref, p_ref, u_ref, o_ref):
        del i_ref
        acc = p_ref[...].astype(F32)
        for q in range(N_CHIPS - 1):
            acc = acc + u_ref[q].astype(F32)
        o_ref[...] = acc

    if kind == "col":
        p_map = lambda i, j, r: (i, r[0] * ncb + j)
        o_map = lambda i, j, r: (r[1] * nrb + i, j)
        full = (2 * R, C)
    else:
        p_map = lambda i, j, r: (r[0] * nrb + i, j)
        o_map = lambda i, j, r: (i, r[1] * ncb + j)
        full = (R, 2 * C)
    return pl.pallas_call(
        body, name=name,
        grid_spec=pltpu.PrefetchScalarGridSpec(
            num_scalar_prefetch=1, grid=(nrb, ncb),
            in_specs=[pl.BlockSpec((tr, tc), p_map), pl.BlockSpec((N_CHIPS - 1, tr, tc), lambda i, j, r: (0, i, j))],
            out_specs=pl.BlockSpec((tr, tc), o_map)),
        out_shape=jax.ShapeDtypeStruct(full, F32),
        compiler_params=_params("parallel", "parallel"),
    )(idx, p, u)


def _adamw(w, g, m, v, name, tr=256, tc=2048):
    R, C = w.shape
    tr = _tile(R, tr, 8)
    tc = _tile(C, tc)
    c1 = 1.0 - ADAM_B1 ** ADAM_STEP
    c2 = 1.0 - ADAM_B2 ** ADAM_STEP

    def body(w_ref, g_ref, m_ref, v_ref, d_ref, nm_ref, nv_ref):
        gv = g_ref[...]
        nm = ADAM_B1 * m_ref[...] + (1.0 - ADAM_B1) * gv
        nv = ADAM_B2 * v_ref[...] + (1.0 - ADAM_B2) * (gv * gv)
        d_ref[...] = -ADAM_LR * ((nm / c1) / (jnp.sqrt(nv / c2) + ADAM_EPS) + ADAM_WD * w_ref[...])
        nm_ref[...] = nm
        nv_ref[...] = nv

    blk = pl.BlockSpec((tr, tc), lambda i, j: (i, j))
    sh = jax.ShapeDtypeStruct((R, C), F32)
    return pl.pallas_call(
        body, name=name, grid=(R // tr, C // tc), in_specs=[blk] * 4, out_specs=[blk] * 3,
        out_shape=[sh, sh, sh], compiler_params=_params("parallel", "parallel"),
    )(w, g, m, v)


def _xyc():
    return lax.axis_index("x"), lax.axis_index("y"), lax.axis_index("c")


def _drain(copies):
    for cp in copies:
        if cp.is_remote:
            cp.wait_send()
        else:
            cp.wait()


def _other_chips(x, y):
    return [(1 - x, y), (x, 1 - y), (1 - x, 1 - y)]


def _allgather_small(blk, name, after=None):
    R, C = blk.shape
    extra = _as_list(after)

    def body(*refs):
        x_ref = refs[0]
        out_ref, send_sems, recv_sems, local_sem = refs[1 + len(extra):]
        x, y, c = _xyc()
        me = 4 * x + 2 * y + c
        mine = pltpu.make_async_copy(x_ref, out_ref.at[me], local_sem)
        mine.start()
        peers = []
        for k in range(1, N_DEV):
            px = 1 - x if (k >> 2) & 1 else x
            py = 1 - y if (k >> 1) & 1 else y
            pc = 1 - c if k & 1 else c
            peers.append((px, py, pc))
        sends = []
        for k, peer in enumerate(peers):
            cp = pltpu.make_async_remote_copy(
                src_ref=x_ref, dst_ref=out_ref.at[me], send_sem=send_sems.at[k], recv_sem=recv_sems.at[k],
                device_id=peer, device_id_type=MESH)
            cp.start()
            sends.append(cp)
        for k, (px, py, pc) in enumerate(peers):
            pltpu.make_async_remote_copy(
                src_ref=x_ref, dst_ref=out_ref.at[4 * px + 2 * py + pc], send_sem=send_sems.at[k],
                recv_sem=recv_sems.at[k], device_id=(px, py, pc), device_id_type=MESH).wait_recv()
        for cp in sends:
            cp.wait_send()
        mine.wait()

    return pl.pallas_call(
        body, name=name, in_specs=[VMEM_SPEC] + [ANY] * len(extra), out_specs=VMEM_SPEC,
        out_shape=jax.ShapeDtypeStruct((N_DEV, R, C), blk.dtype),
        scratch_shapes=[pltpu.SemaphoreType.DMA((N_DEV - 1,)), pltpu.SemaphoreType.DMA((N_DEV - 1,)),
                        pltpu.SemaphoreType.DMA],
        compiler_params=pltpu.CompilerParams(vmem_limit_bytes=VMEM_LIMIT),
    )(blk, *extra)


def _full_region(ref, kind, chip, half, shard_shape):
    r, cn = shard_shape
    hr = r // 2
    if kind == "col":
        rows = pl.ds(0, r) if half is None else pl.ds(pl.multiple_of(half * hr, 16), hr)
        return ref.at[rows, pl.ds(pl.multiple_of(chip * cn, LANES), cn)]
    if half is None:
        return ref.at[pl.ds(pl.multiple_of(chip * r, 16), r), :]
    return ref.at[pl.ds(pl.multiple_of(chip * r + half * hr, 16), hr), :]


def _allgather_weights(fulls, kinds, shapes, name):
    n = len(fulls)

    def body(*refs):
        outs = refs[n:2 * n]
        send_sems, recv_sems = refs[2 * n:]
        x, y, c = _xyc()
        chip = 2 * x + y
        sib = (x, y, 1 - c)
        others = _other_chips(x, y)
        started = []
        for w in range(n):
            mine = _full_region(outs[w], kinds[w], chip, c, shapes[w])
            for j, (ox, oy) in enumerate(others):
                cp = pltpu.make_async_remote_copy(
                    src_ref=mine, dst_ref=mine, send_sem=send_sems.at[6 * w + j], recv_sem=recv_sems.at[6 * w + j],
                    device_id=(ox, oy, c), device_id_type=MESH)
                cp.start()
                started.append(cp)
        for w in range(n):
            for j, (ox, oy) in enumerate(others):
                landed = _full_region(outs[w], kinds[w], 2 * ox + oy, c, shapes[w])
                pltpu.make_async_remote_copy(
                    src_ref=landed, dst_ref=landed, send_sem=send_sems.at[6 * w + j], recv_sem=recv_sems.at[6 * w + j],
                    device_id=(ox, oy, c), device_id_type=MESH).wait_recv()
                cp = pltpu.make_async_remote_copy(
                    src_ref=landed, dst_ref=landed, send_sem=send_sems.at[6 * w + 3 + j],
                    recv_sem=recv_sems.at[6 * w + 3 + j], device_id=sib, device_id_type=MESH)
                cp.start()
                started.append(cp)
        for w in range(n):
            for j, (ox, oy) in enumerate(others):
                theirs = _full_region(outs[w], kinds[w], 2 * ox + oy, 1 - c, shapes[w])
                pltpu.make_async_remote_copy(
                    src_ref=theirs, dst_ref=theirs, send_sem=send_sems.at[6 * w + 3 + j],
                    recv_sem=recv_sems.at[6 * w + 3 + j], device_id=sib, device_id_type=MESH).wait_recv()
        _drain(started)

    return pl.pallas_call(
        body, name=name, in_specs=[ANY] * n, out_specs=[ANY] * n,
        out_shape=[jax.ShapeDtypeStruct(f.shape, f.dtype) for f in fulls],
        input_output_aliases={w: w for w in range(n)},
        scratch_shapes=[pltpu.SemaphoreType.DMA((6 * n,)), pltpu.SemaphoreType.DMA((6 * n,))],
    )(*fulls)


def _region_of_size(ref, kind, shard_shape, count):
    r, cn = shard_shape
    if kind == "col":
        return ref.at[pl.ds(0, r // 2), pl.ds(0, count * cn)]
    return ref.at[pl.ds(0, count * (r // 2)), :]


def _allgather_weights_seq(fulls, kinds, shapes, name, collective_id):
    n = len(fulls)
    refs = [jax.new_ref(f, memory_space=pltpu.MemorySpace.HBM) for f in fulls]

    def body(send_sems, recv_sems):
        x, y, c = _xyc()
        chip = 2 * x + y
        sib = (x, y, 1 - c)
        others = _other_chips(x, y)
        peers = [(ox, oy, c) for ox, oy in others] + [sib]
        barrier = pltpu.get_barrier_semaphore()
        for peer in peers:
            pl.semaphore_signal(barrier, inc=1, device_id=peer, device_id_type=MESH)
        pl.semaphore_wait(barrier, len(peers))

        def copy(w, region, sem, to):
            return pltpu.make_async_remote_copy(src_ref=region, dst_ref=region, send_sem=send_sems.at[sem],
                                                recv_sem=recv_sems.at[sem], device_id=to, device_id_type=MESH)

        for w in range(n):
            mine = _full_region(refs[w], kinds[w], chip, c, shapes[w])
            for ox, oy in others:
                copy(w, mine, 2 * w, (ox, oy, c)).start()
        for w in range(n):
            three = _region_of_size(refs[w], kinds[w], shapes[w], 3)
            copy(w, three, 2 * w, sib).wait_recv()
            for ox, oy in others:
                copy(w, _full_region(refs[w], kinds[w], 2 * ox + oy, c, shapes[w]), 2 * w + 1, sib).start()
        for w in range(n):
            three = _region_of_size(refs[w], kinds[w], shapes[w], 3)
            copy(w, three, 2 * w + 1, sib).wait_recv()
            copy(w, three, 2 * w, sib).wait_send()
            copy(w, three, 2 * w + 1, sib).wait_send()

    pl.kernel(
        body, out_type=(), mesh=plsc.ScalarSubcoreMesh(axis_name="seq", num_cores=1), name=name,
        scratch_types=[pltpu.SemaphoreType.DMA((2 * n,)), pltpu.SemaphoreType.DMA((2 * n,))],
        compiler_params=pltpu.CompilerParams(collective_id=collective_id),
    )()
    return [r[...] for r in refs]


def _half_of(ref, kind, half):
    r, cn = ref.shape
    if kind == "col":
        return ref.at[pl.ds(pl.multiple_of(half * (r // 2), 16), r // 2), :]
    return ref.at[:, pl.ds(pl.multiple_of(half * (cn // 2), LANES), cn // 2)]


def _shard_of(ref, kind, chip):
    r, cn = ref.shape
    if kind == "col":
        return ref.at[:, pl.ds(pl.multiple_of(chip * (cn // N_CHIPS), LANES), cn // N_CHIPS)]
    return ref.at[pl.ds(pl.multiple_of(chip * (r // N_CHIPS), 16), r // N_CHIPS), :]


def _exchange_halves(grads, kinds, name):
    n = len(grads)

    def body(*refs):
        gs = refs[:n]
        ts = refs[n:2 * n]
        send_sems, recv_sems = refs[2 * n:]
        x, y, c = _xyc()
        cps = []
        for w in range(n):
            cp = pltpu.make_async_remote_copy(
                src_ref=_half_of(gs[w], kinds[w], 1 - c), dst_ref=ts[w], send_sem=send_sems.at[w],
                recv_sem=recv_sems.at[w], device_id=(x, y, 1 - c), device_id_type=MESH)
            cp.start()
            cps.append(cp)
        for cp in cps:
            cp.wait()

    out_shape = []
    for gr, kind in zip(grads, kinds):
        r, cn = gr.shape
        out_shape.append(jax.ShapeDtypeStruct((r // 2, cn) if kind == "col" else (r, cn // 2), gr.dtype))
    return pl.pallas_call(
        body, name=name, in_specs=[ANY] * n, out_specs=[ANY] * n, out_shape=out_shape,
        scratch_shapes=[pltpu.SemaphoreType.DMA((n,)), pltpu.SemaphoreType.DMA((n,))],
    )(*grads)


def _exchange_halves_seq(grads, kinds, name, collective_id):
    n = len(grads)

    def body(*refs):
        gs = refs[:n]
        ts = refs[n:2 * n]
        send_sems, recv_sems = refs[2 * n:]
        x, y, c = _xyc()
        sib = (x, y, 1 - c)
        barrier = pltpu.get_barrier_semaphore()
        pl.semaphore_signal(barrier, inc=1, device_id=sib, device_id_type=MESH)
        pl.semaphore_wait(barrier, 1)
        cps = []
        for w in range(n):
            cp = pltpu.make_async_remote_copy(
                src_ref=_half_of(gs[w], kinds[w], 1 - c), dst_ref=ts[w], send_sem=send_sems.at[w],
                recv_sem=recv_sems.at[w], device_id=sib, device_id_type=MESH)
            cp.start()
            cps.append(cp)
        for cp in cps:
            cp.wait()

    out_type = []
    for gr, kind in zip(grads, kinds):
        r, cn = gr.shape
        out_type.append(jax.ShapeDtypeStruct((r // 2, cn) if kind == "col" else (r, cn // 2), gr.dtype))
    return pl.kernel(
        body, out_type=out_type, mesh=plsc.ScalarSubcoreMesh(axis_name="seq", num_cores=1), name=name,
        scratch_types=[pltpu.SemaphoreType.DMA((n,)), pltpu.SemaphoreType.DMA((n,))],
        compiler_params=pltpu.CompilerParams(collective_id=collective_id),
    )(*grads)


def _scatter_partials(parts, kinds, name):
    n = len(parts)

    def body(*refs):
        ps = refs[:n]
        us = refs[n:2 * n]
        send_sems, recv_sems = refs[2 * n:]
        x, y, c = _xyc()
        others = _other_chips(x, y)
        cps = []
        for w in range(n):
            for j, (ox, oy) in enumerate(others):
                cp = pltpu.make_async_remote_copy(
                    src_ref=_shard_of(ps[w], kinds[w], 2 * ox + oy), dst_ref=us[w].at[j],
                    send_sem=send_sems.at[3 * w + j], recv_sem=recv_sems.at[3 * w + j],
                    device_id=(ox, oy, c), device_id_type=MESH)
                cp.start()
                cps.append(cp)
        for cp in cps:
            cp.wait()

    out_shape = []
    for p, kind in zip(parts, kinds):
        r, cn = p.shape
        hs = (r, cn // N_CHIPS) if kind == "col" else (r // N_CHIPS, cn)
        out_shape.append(jax.ShapeDtypeStruct((N_CHIPS - 1,) + hs, p.dtype))
    return pl.pallas_call(
        body, name=name, in_specs=[ANY] * n, out_specs=[ANY] * n, out_shape=out_shape,
        scratch_shapes=[pltpu.SemaphoreType.DMA((3 * n,)), pltpu.SemaphoreType.DMA((3 * n,))],
    )(*parts)


def _scatter_partials_seq(parts, kinds, name, collective_id):
    n = len(parts)

    def body(*refs):
        ps = refs[:n]
        us = refs[n:2 * n]
        send_sems, recv_sems = refs[2 * n:]
        x, y, c = _xyc()
        others = _other_chips(x, y)
        barrier = pltpu.get_barrier_semaphore()
        for ox, oy in others:
            pl.semaphore_signal(barrier, inc=1, device_id=(ox, oy, c), device_id_type=MESH)
        pl.semaphore_wait(barrier, len(others))
        for w in range(n):
            for j, (ox, oy) in enumerate(others):
                pltpu.make_async_remote_copy(
                    src_ref=_shard_of(ps[w], kinds[w], 2 * ox + oy), dst_ref=us[w].at[j],
                    send_sem=send_sems.at[w], recv_sem=recv_sems.at[w],
                    device_id=(ox, oy, c), device_id_type=MESH).start()
        for w in range(n):
            pltpu.make_async_remote_copy(
                src_ref=us[w], dst_ref=us[w], send_sem=send_sems.at[w], recv_sem=recv_sems.at[w],
                device_id=(x, y, c), device_id_type=MESH).wait()

    out_type = []
    for p, kind in zip(parts, kinds):
        r, cn = p.shape
        hs = (r, cn // N_CHIPS) if kind == "col" else (r // N_CHIPS, cn)
        out_type.append(jax.ShapeDtypeStruct((N_CHIPS - 1,) + hs, p.dtype))
    return pl.kernel(
        body, out_type=out_type, mesh=plsc.ScalarSubcoreMesh(axis_name="seq", num_cores=1), name=name,
        scratch_types=[pltpu.SemaphoreType.DMA((n,)), pltpu.SemaphoreType.DMA((n,))],
        compiler_params=pltpu.CompilerParams(collective_id=collective_id),
    )(*parts)


def _join_halves(halves, kinds, name):
    n = len(halves)

    def body(*refs):
        outs = refs[n:2 * n]
        send_sems, recv_sems = refs[2 * n:]
        x, y, c = _xyc()
        cps = []
        for w in range(n):
            mine = _half_of(outs[w], kinds[w], c)
            cp = pltpu.make_async_remote_copy(
                src_ref=mine, dst_ref=mine, send_sem=send_sems.at[w], recv_sem=recv_sems.at[w],
                device_id=(x, y, 1 - c), device_id_type=MESH)
            cp.start()
            cps.append(cp)
        for w in range(n):
            theirs = _half_of(outs[w], kinds[w], 1 - c)
            pltpu.make_async_remote_copy(
                src_ref=theirs, dst_ref=theirs, send_sem=send_sems.at[w], recv_sem=recv_sems.at[w],
                device_id=(x, y, 1 - c), device_id_type=MESH).wait_recv()
        _drain(cps)

    return pl.pallas_call(
        body, name=name, in_specs=[ANY] * n, out_specs=[ANY] * n,
        out_shape=[jax.ShapeDtypeStruct(h.shape, h.dtype) for h in halves],
        input_output_aliases={w: w for w in range(n)},
        scratch_shapes=[pltpu.SemaphoreType.DMA((n,)), pltpu.SemaphoreType.DMA((n,))],
    )(*halves)


def kernel(x, a_norm, a_w_in, a_conv_w, a_conv_b, a_ln_g, a_ln_b, a_w_out, kv_norm, w_kv, b_norm, b_w_in, b_w_out, rel_bias, final_norm, loss_target, m_a_norm, m_a_w_in, m_a_conv_w, m_a_conv_b, m_a_ln_g, m_a_ln_b, m_a_w_out, m_kv_norm, m_w_kv, m_b_norm, m_b_w_in, m_b_w_out, m_rel_bias, m_final_norm, v_a_norm, v_a_w_in, v_a_conv_w, v_a_conv_b, v_a_ln_g, v_a_ln_b, v_a_w_out, v_kv_norm, v_w_kv, v_b_norm, v_b_w_in, v_b_w_out, v_rel_bias, v_final_norm):
    S, D = x.shape[1], x.shape[2]
    E = a_w_out.shape[1] * N_CHIPS
    A = b_w_out.shape[1] * N_CHIPS
    H = A // HEAD_DIM
    DC = D // N_CHIPS
    xs = x.reshape(S, D)
    tgt = loss_target.reshape(S, D)
    cx, cy, cc = _xyc()
    chip = 2 * cx + cy
    c_idx = jnp.reshape(cc, (1,)).astype(jnp.int32)

    big_names = ["a_w_in", "a_w_out", "w_kv", "b_w_in", "b_w_out"]
    kinds = ["col", "row", "col", "col", "row"]
    big_w = [a_w_in[0], a_w_out[0], w_kv, b_w_in[0], b_w_out[0]]
    big_m = [m_a_w_in[0], m_a_w_out[0], m_w_kv, m_b_w_in[0], m_b_w_out[0]]
    big_v = [v_a_w_in[0], v_a_w_out[0], v_w_kv, v_b_w_in[0], v_b_w_out[0]]
    chip_idx = jnp.reshape(chip, (1,)).astype(jnp.int32)
    placed = [_cast_into_full(big_w[w], kinds[w], chip_idx, "cast_" + big_names[w]) for w in range(5)]
    shard_shapes = [w.shape for w in big_w]
    (wa_in,) = _allgather_weights_seq(placed[0:1], kinds[0:1], shard_shapes[0:1], "ag_seq_a_in", 0)
    (wa_out,) = _allgather_weights_seq(placed[1:2], kinds[1:2], shard_shapes[1:2], "ag_seq_a_out", 1)
    (wkv,) = _allgather_weights_seq(placed[2:3], kinds[2:3], shard_shapes[2:3], "ag_seq_kv", 5)
    wb_in, wb_out = _allgather_weights_seq(placed[3:5], kinds[3:5], shard_shapes[3:5], "ag_seq_b", 2)

    def row_at(vec, q):
        return jnp.pad(vec, ((q, 7 - q), (0, 0)))

    def pack_sharded(an, cw, cb, lg, lb):
        return jnp.concatenate([row_at(an, 0), jnp.pad(cw[0], ((0, 1), (0, 0))),
                                row_at(lg, 0) + row_at(lb, 1) + row_at(cb, 2)], axis=0)

    small_w = pack_sharded(a_norm, a_conv_w, a_conv_b, a_ln_g, a_ln_b)
    gathered = _allgather_small(small_w, "ag_small_params")
    small_full = jnp.concatenate([gathered[2 * k] for k in range(N_CHIPS)], axis=1)
    g_a = small_full[0:1]
    conv_w32 = small_full[8:8 + HALO]
    ln_g = small_full[40:41]
    ln_b = small_full[41:42]
    conv_b = small_full[42:43]
    g_kv = kv_norm.reshape(1, D)
    g_b = b_norm.reshape(1, D)
    g_f = final_norm.reshape(1, D)

    rb_t = jnp.pad(rel_bias.T, ((0, 0), (0, LANES - N_BUCKETS)))
    onehots = [_onehot(dil) for _, dil in GROUPS]
    biases = [_bias_table(rb_t, onehots[g], "bias_table_%d" % g).reshape(H, BLOCK, 2 * BLOCK)
              for g in range(len(GROUPS))]

    dils = tuple(dil for _, dil in GROUPS)
    assert dils[0] == 1
    n_g = len(GROUPS)
    ((h0,),) = _rms_fwd(xs, [g_a], (1,), "rms_a")
    proj3 = _matmul(h0, wa_in, "nn", BF16, "mm_a_in", out_slab=E)
    conv = _conv_fwd(proj3, conv_w32, conv_b, "conv_fwd")
    y_a = _ln_gate_fwd(conv, proj3, ln_g, ln_b, "ln_gate_fwd")
    x1 = _matmul(y_a, wa_out, "nn", F32, "mm_a_out", res=xs)
    hks, hbs = _rms_fwd(x1, [g_kv, g_b], dils, "rms_kv_b")
    kvs = [_matmul(hks[g], wkv, "nn", BF16, "mm_kv_%d" % g, out_slab=A, b_off=2 * g * A, n_cols=2 * A)
           for g in range(n_g)]
    qs = [_matmul(hbs[g], wb_in, "nn", BF16, "mm_q_%d" % g, b_off=g * A, n_cols=A, after=kvs)
          for g in range(n_g)]
    zb = _matmul(hbs[0], wb_in, "nn", BF16, "mm_zb", b_off=n_g * A, n_cols=A, after=kvs)
    os_, lses = [], []
    for g, dil in enumerate(dils):
        o_g, lse_g = _attn_fwd(qs[g], kvs[g], biases[g], dil, "attn_fwd_%d" % g)
        os_.append(o_g)
        lses.append(lse_g)
    y_b, o_m, lse_d = _attn_merge(os_, lses, zb, dils, "attn_merge")
    x2 = _matmul(y_b, wb_out, "nn", F32, "mm_b_out", res=x1)
    loss_part, dx2, dx2b, gg_f = _final_head(x2, g_f, tgt, "final_head")
    loss = lax.psum(loss_part[0, 0], ("x", "y", "c"))

    dw_tiles = dict(tm=1024, tn=1024, tk=4096)
    dy_b = _matmul(dx2b, wb_out, "nt", BF16, "mm_b_out_dx", after=loss.reshape(1, 1))
    dwb_out = _matmul(y_b, dx2b, "tn", BF16, "mm_b_out_dw", **dw_tiles)
    dos, dhs, dzb = _gate_bwd(dy_b, o_m, zb, dils, "gate_bwd")
    dbs, cots = [], []
    dwb_in = dwkv = None
    for g, dil in enumerate(dils):
        dq, dkv, db = _attn_bwd(qs[g], kvs[g], dos[g], lse_d[g], dhs[g], biases[g], dil, "attn_bwd_%d" % g)
        dbs.append(db.reshape(H, BLOCK * 2 * BLOCK))
        dwb_in = _matmul(hbs[g], dq, "tn", BF16, "mm_q_dw_%d" % g, out_off=g * A, out_cols=(n_g + 1) * A,
                         out_alias=dwb_in, **dw_tiles)
        dwkv = _matmul(hks[g], dkv, "tn", BF16, "mm_kv_dw_%d" % g, b_slab=True, out_off=2 * g * A,
                       out_cols=2 * n_g * A, out_alias=dwkv, **dw_tiles)
        cots.append((_matmul(dkv, wkv, "nt", BF16, "mm_kv_dx_%d" % g, a_slab=True, b_off=2 * g * A), 0, dil))
        cots.append((_matmul(dq, wb_in, "nt", BF16, "mm_q_dx_%d" % g, b_off=g * A), 1, dil))
    dwb_in = _matmul(hbs[0], dzb, "tn", BF16, "mm_zb_dw", out_off=n_g * A, out_cols=(n_g + 1) * A,
                     out_alias=dwb_in, **dw_tiles)
    cots.append((_matmul(dzb, wb_in, "nt", BF16, "mm_zb_dx", b_off=n_g * A), 1, 1))
    chip_c = jnp.stack([chip, cc]).astype(jnp.int32)

    def scatter_group(idx, grads, tag, collective_id, exchange_id=None, behind=None):
        ks = [kinds[w] for w in idx]
        if exchange_id is None:
            theirs = _exchange_halves(grads, ks, "rs_exchange_" + tag)
        else:
            theirs = _exchange_halves_seq(grads, ks, "rs_exchange_seq_" + tag, exchange_id)
        parts = [_add_half(grads[q], theirs[q], c_idx, ks[q], "rs_add_half_%d" % w, after=behind)
                 for q, w in enumerate(idx)]
        return parts, _scatter_partials_seq(parts, ks, "rs_seq_" + tag, collective_id)

    def reduce_group(idx, parts, slots, tag):
        ks = [kinds[w] for w in idx]
        halves = [_sum_into_shard(parts[q], slots[q], chip_c, ks[q], "rs_sum_chips_%d" % w)
                  for q, w in enumerate(idx)]
        return _join_halves(halves, ks, "rs_join_" + tag)

    parts_b, slots_b = scatter_group([2, 3, 4], [dwkv, dwb_in, dwb_out], "b", 3, exchange_id=6,
                                     behind=[ct[0] for ct in cots])
    g_rel_t = _bias_grad(dbs, onehots, "bias_grad")
    dx1, dx1b, gg_kvb = _rms_bwd(x1, cots, [g_kv, g_b], dx2, "rms_kv_b_bwd", after=parts_b)
    dy_a = _matmul(dx1b, wa_out, "nt", BF16, "mm_a_out_dx")
    dwa_out = _matmul(y_a, dx1b, "tn", BF16, "mm_a_out_dw", **dw_tiles)
    dconv, dproj3, gg_ln = _ln_gate_bwd(conv, proj3, dy_a, ln_g, ln_b, "ln_gate_bwd")
    dproj3, g_conv_w = _conv_bwd(proj3, dconv, conv_w32, dproj3, "conv_bwd")
    dwa_in = _matmul(h0, dproj3, "tn", BF16, "mm_a_in_dw", b_slab=True, **dw_tiles)
    parts_a, slots_a = scatter_group([0, 1], [dwa_in, dwa_out], "a", 4)
    dh0 = _matmul(dproj3, wa_in, "nt", BF16, "mm_a_in_dx", a_slab=True, after=parts_a, tn=512)
    grad_x, _, gg_a = _rms_bwd(xs, [(dh0, 0, 1)], [g_a], dx1, "rms_a_bwd")

    big_g = [None] * 5
    big_g[2:5] = reduce_group([2, 3, 4], parts_b, slots_b, "b")
    big_g[0:2] = reduce_group([0, 1], parts_a, slots_a, "a")

    def rel_rows(rb):
        return jnp.pad(rb.reshape(1, N_BUCKETS * H), ((0, 7), (0, D - N_BUCKETS * H)))

    small_g = jnp.concatenate([gg_a, g_conv_w, gg_ln, gg_kvb, gg_f, rel_rows(g_rel_t[:, :N_BUCKETS].T)], axis=0)
    small_sum = _sum_leading(_allgather_small(small_g, "ag_small_grads", after=[slots_a[0], slots_b[0]]), F32,
                             "sum_small_grads", tr=72)
    g_sharded = lax.dynamic_slice(small_sum, (0, chip * DC), (48, DC))
    g_repl = small_sum[48:72]

    outs_g, outs_d, outs_m, outs_v = {}, {}, {}, {}
    for w, nm in enumerate(big_names):
        d_, m_, v_ = _adamw(big_w[w], big_g[w], big_m[w], big_v[w], "adamw_" + nm)
        outs_g[nm], outs_d[nm], outs_m[nm], outs_v[nm] = big_g[w], d_, m_, v_
    sm_m = pack_sharded(m_a_norm, m_a_conv_w, m_a_conv_b, m_a_ln_g, m_a_ln_b)
    sm_v = pack_sharded(v_a_norm, v_a_conv_w, v_a_conv_b, v_a_ln_g, v_a_ln_b)
    sd, smm, svv = _adamw(small_w, g_sharded, sm_m, sm_v, "adamw_small_sharded")

    def unpack_sharded(p):
        return {"a_norm": p[0:1], "a_conv_w": p[8:8 + CONV_TAPS].reshape(1, CONV_TAPS, DC), "a_ln_g": p[40:41],
                "a_ln_b": p[41:42], "a_conv_b": p[42:43]}

    for src, dst in ((g_sharded, outs_g), (sd, outs_d), (smm, outs_m), (svv, outs_v)):
        dst.update(unpack_sharded(src))

    def pack_repl(kn, bn, fn, rb):
        return jnp.concatenate([row_at(kn.reshape(1, D), 0) + row_at(bn.reshape(1, D), 1),
                                row_at(fn.reshape(1, D), 0), rel_rows(rb)], axis=0)

    rp_w = pack_repl(kv_norm, b_norm, final_norm, rel_bias)
    rp_m = pack_repl(m_kv_norm, m_b_norm, m_final_norm, m_rel_bias)
    rp_v = pack_repl(v_kv_norm, v_b_norm, v_final_norm, v_rel_bias)
    rd, rmm, rvv = _adamw(rp_w, g_repl, rp_m, rp_v, "adamw_small_replicated")

    def unpack_repl(p):
        return {"kv_norm": p[0], "b_norm": p[1:2], "final_norm": p[8],
                "rel_bias": p[16, :N_BUCKETS * H].reshape(N_BUCKETS, H)}

    for src, dst in ((g_repl, outs_g), (rd, outs_d), (rmm, outs_m), (rvv, outs_v)):
        dst.update(unpack_repl(src))

    order = ["a_norm", "a_w_in", "a_conv_w", "a_conv_b", "a_ln_g", "a_ln_b", "a_w_out", "kv_norm", "w_kv",
             "b_norm", "b_w_in", "b_w_out", "rel_bias", "final_norm"]
    lead = {"a_w_in", "a_w_out", "b_w_in", "b_w_out"}

    def shaped(nm, val):
        return val[None] if nm in lead else val

    result = [loss, grad_x.reshape(1, S, D)]
    for table in (outs_g, outs_d, outs_m, outs_v):
        result.extend(shaped(nm, table[nm]) for nm in order)
    return tuple(result)
```

```python
import functools

import numpy as np
import jax
import jax.numpy as jnp
from jax import lax
from jax.experimental import pallas as pl
from jax.experimental.pallas import tpu as pltpu
from jax.experimental.pallas import tpu_sc as plsc

F32 = jnp.float32
BF16 = jnp.bfloat16
MESH = pl.DeviceIdType.MESH
ANY = pl.BlockSpec(memory_space=pl.ANY)
VMEM_SPEC = pl.BlockSpec(memory_space=pltpu.VMEM)

EPS = 1e-6
HEAD_DIM = 128
BLOCK = 128
GROUPS = ((128, 1), (512, 4), (2048, 16))
SCALE = HEAD_DIM ** -0.5
CONV_TAPS = 31
HALO = 32
N_BUCKETS = 32
MAX_EXACT = 16
MAX_DISTANCE = 2048
NEG = -1e30
PRODUCTS_AHEAD = 2
SCORES_AHEAD = 4
N_CHIPS = 4
N_DEV = 8
LANES = 128
VMEM_LIMIT = 56 * 1024 * 1024

ADAM_LR = 0.001
ADAM_B1 = 0.9
ADAM_B2 = 0.999
ADAM_EPS = 1e-08
ADAM_WD = 0.01
ADAM_STEP = 10


def _tile(n, pref, mult=LANES):
    t = (min(pref, n) // mult) * mult
    while t >= mult:
        if n % t == 0:
            return t
        t -= mult
    return n


def _params(*sem):
    return pltpu.CompilerParams(dimension_semantics=sem, vmem_limit_bytes=VMEM_LIMIT)


def _sigmoid(v):
    return 1.0 / (1.0 + jnp.exp(-v))


def _dot(a, b, dims):
    return lax.dot_general(a, b, (dims, ((), ())), preferred_element_type=F32)


NN = ((1,), (0,))
NT = ((1,), (1,))
TN = ((0,), (0,))


def _as_list(after):
    if after is None:
        return []
    return list(after) if isinstance(after, (list, tuple)) else [after]


def _stack_rows(rows, total):
    width = rows[0].shape[1]
    rid = lax.broadcasted_iota(jnp.int32, (total, width), 0)
    out = jnp.zeros((total, width), F32)
    for q, row in enumerate(rows):
        out = jnp.where(rid == q, jnp.broadcast_to(row, (total, width)), out)
    return out


def _lane_col(arr, h, lane):
    return jnp.sum(jnp.where(lane == h, arr, 0.0), axis=-1, keepdims=True)


def _matmul(a, b, mode, out_dtype, name, res=None, a_slab=False, b_slab=False, out_slab=0,
            b_off=0, n_cols=None, out_off=0, out_cols=None, out_alias=None, after=None,
            tm=1024, tn=1024, tk=2048):
    if a_slab:
        na, M, W = a.shape
        K = na * W
    elif mode == "tn":
        K, M = a.shape
    else:
        M, K = a.shape
    if b_slab:
        nbs, _, Wb = b.shape
        N = nbs * Wb
    elif mode == "nt":
        N = b.shape[0]
    else:
        N = n_cols if n_cols else b.shape[1]
    tm = _tile(M, tm)
    tn = _tile(Wb if b_slab else (out_slab if out_slab else N), tn)
    tk = _tile(W if a_slab else K, tk)
    all_slabs = a_slab and mode == "nt" and tk == W
    if all_slabs:
        tk = K
    nk = K // tk
    grid = (M // tm, N // tn, nk)
    bo = b_off // (tk if mode == "nt" else tn)
    oo = out_off // tn

    if all_slabs:
        a_spec = pl.BlockSpec((na, tm, W), lambda i, j, k: (0, i, 0))
    elif a_slab:
        per = W // tk
        a_spec = pl.BlockSpec((None, tm, tk), lambda i, j, k: (k // per, i, k % per))
    elif mode == "tn":
        a_spec = pl.BlockSpec((tk, tm), lambda i, j, k: (k, i))
    else:
        a_spec = pl.BlockSpec((tm, tk), lambda i, j, k: (i, k))
    if b_slab:
        perb = Wb // tn
        b_spec = pl.BlockSpec((None, tk, tn), lambda i, j, k: (j // perb, k, j % perb))
    elif mode == "nt":
        b_spec = pl.BlockSpec((tn, tk), lambda i, j, k: (j, k + bo))
    else:
        b_spec = pl.BlockSpec((tk, tn), lambda i, j, k: (k, j + bo))
    if out_slab:
        pero = out_slab // tn
        o_spec = pl.BlockSpec((None, tm, tn), lambda i, j, k: (j // pero, i, j % pero))
        out_shape = jax.ShapeDtypeStruct((N // out_slab, M, out_slab), out_dtype)
    else:
        o_spec = pl.BlockSpec((tm, tn), lambda i, j, k: (i, j + oo))
        out_shape = jax.ShapeDtypeStruct((M, out_cols if out_cols else N), out_dtype)
    in_specs = [a_spec, b_spec]
    operands = [a, b]
    if res is not None:
        in_specs.append(pl.BlockSpec((tm, tn), lambda i, j, k: (i, j)))
        operands.append(res)
    aliases = {}
    if out_alias is not None:
        aliases[len(operands)] = 0
        in_specs.append(ANY)
        operands.append(out_alias)
    for arr in _as_list(after):
        in_specs.append(ANY)
        operands.append(arr)
    dims = {"nn": NN, "nt": NT, "tn": TN}[mode]
    has_res = res is not None
    n_in = len(operands)

    def body(*refs):
        a_ref, b_ref = refs[0], refs[1]
        r_ref = refs[2] if has_res else None
        o_ref = refs[n_in]
        if all_slabs:
            prod = _dot(a_ref[0], b_ref[:, 0:W], dims)
            for q in range(1, na):
                prod = prod + _dot(a_ref[q], b_ref[:, q * W:(q + 1) * W], dims)
        else:
            prod = _dot(a_ref[...], b_ref[...], dims)

        def finish(val):
            if has_res:
                val = val + r_ref[...]
            o_ref[...] = val.astype(out_dtype)

        if nk == 1:
            finish(prod)
        else:
            acc_ref = refs[n_in + 1]
            k = pl.program_id(2)

            @pl.when(k == 0)
            def _():
                acc_ref[...] = prod

            @pl.when(k > 0)
            def _():
                acc_ref[...] += prod

            @pl.when(k == nk - 1)
            def _():
                finish(acc_ref[...])

    scratch = [pltpu.VMEM((tm, tn), F32)] if nk > 1 else []
    return pl.pallas_call(
        body, name=name, grid=grid, in_specs=in_specs, out_specs=o_spec, out_shape=out_shape,
        scratch_shapes=scratch, input_output_aliases=aliases,
        compiler_params=_params("parallel", "parallel", "arbitrary"),
    )(*operands)


def _group_spec(d, ts, width):
    if d == 1:
        return pl.BlockSpec((ts, width), lambda i: (i, 0))
    return pl.BlockSpec((d, ts // d, width), lambda i: (0, i, 0))


def _group_shape(d, S, width, dtype):
    return jax.ShapeDtypeStruct((S, width) if d == 1 else (d, S // d, width), dtype)


def _chunk_buf(ts, width):
    return pltpu.VMEM((width // LANES, ts, LANES), F32)


def _fill_chunks(buf, val):
    for c in range(buf.shape[0]):
        buf[c] = val[:, c * LANES:(c + 1) * LANES]


def _read_chunks(buf):
    return jnp.concatenate([buf[c] for c in range(buf.shape[0])], axis=1)


def _emit_group_order(o_ref, buf, d, dtype):
    n = buf.shape[1] // d
    for r in range(d):
        for c in range(buf.shape[0]):
            o_ref[r, :, c * LANES:(c + 1) * LANES] = buf[c, pl.ds(r, n, stride=d), :].astype(dtype)


def _store_token_order(buf, i_ref, d):
    n = buf.shape[1] // d
    for r in range(d):
        for c in range(buf.shape[0]):
            buf[c, pl.ds(r, n, stride=d), :] = i_ref[r, :, c * LANES:(c + 1) * LANES].astype(F32)


def _rms_fwd(x, gains, dils, name, ts=256):
    S, D = x.shape
    ts = _tile(S, ts, 16 * max(dils))
    n = len(gains)
    nd = len(dils)

    def body(*refs):
        buf = refs[1 + n + n * nd]
        xv = refs[0][...]
        nrm = xv * lax.rsqrt(jnp.mean(xv * xv, axis=-1, keepdims=True) + EPS)
        for q in range(n):
            val = nrm * refs[1 + q][...]
            if max(dils) > 1:
                _fill_chunks(buf, val)
            for e, d in enumerate(dils):
                if d == 1:
                    refs[1 + n + q * nd + e][...] = val.astype(BF16)
                else:
                    _emit_group_order(refs[1 + n + q * nd + e], buf, d, BF16)

    row = pl.BlockSpec((ts, D), lambda i: (i, 0))
    vec = pl.BlockSpec((1, D), lambda i: (0, 0))
    outs = pl.pallas_call(
        body, name=name, grid=(S // ts,), in_specs=[row] + [vec] * n,
        out_specs=[_group_spec(d, ts, D) for _ in range(n) for d in dils],
        out_shape=[_group_shape(d, S, D, BF16) for _ in range(n) for d in dils],
        scratch_shapes=[_chunk_buf(ts, D)],
        compiler_params=_params("parallel"),
    )(x, *gains)
    return [[outs[q * nd + e].reshape(S, D) for e in range(nd)] for q in range(n)]


def _rms_bwd(x, cots, gains, dres, name, after=None, ts=256):
    S, D = x.shape
    ts = _tile(S, ts, 16 * max(d for _, _, d in cots))
    n = len(cots)
    ng = len(gains)
    extra = _as_list(after)
    n_in = 2 + n + ng + len(extra)

    def body(*refs):
        x_ref = refs[0]
        dh_refs = refs[1:1 + n]
        g_refs = refs[1 + n:1 + n + ng]
        dres_ref = refs[1 + n + ng]
        dx_ref, dxb_ref, gg_ref, buf = refs[n_in:n_in + 4]
        i = pl.program_id(0)
        xv = x_ref[...]
        r = lax.rsqrt(jnp.mean(xv * xv, axis=-1, keepdims=True) + EPS)
        nrm = xv * r
        dn = jnp.zeros_like(xv)
        rows = [jnp.zeros((1, D), F32) for _ in range(ng)]
        for q, (_, gi, d) in enumerate(cots):
            if d == 1:
                dh = dh_refs[q][...].astype(F32)
            else:
                _store_token_order(buf, dh_refs[q], d)
                dh = _read_chunks(buf)
            dn = dn + dh * g_refs[gi][...]
            rows[gi] = rows[gi] + jnp.sum(dh * nrm, axis=0, keepdims=True)
        dx = dres_ref[...] + r * (dn - nrm * jnp.mean(dn * nrm, axis=-1, keepdims=True))
        dx_ref[...] = dx
        dxb_ref[...] = dx.astype(BF16)
        upd = _stack_rows(rows, 8)

        @pl.when(i == 0)
        def _():
            gg_ref[...] = upd

        @pl.when(i > 0)
        def _():
            gg_ref[...] += upd

    row = pl.BlockSpec((ts, D), lambda i: (i, 0))
    vec = pl.BlockSpec((1, D), lambda i: (0, 0))
    acc = pl.BlockSpec((8, D), lambda i: (0, 0))
    return pl.pallas_call(
        body, name=name, grid=(S // ts,),
        in_specs=[row] + [_group_spec(d, ts, D) for _, _, d in cots] + [vec] * ng + [row] + [ANY] * len(extra),
        out_specs=[row, row, acc],
        out_shape=[jax.ShapeDtypeStruct((S, D), F32), jax.ShapeDtypeStruct((S, D), BF16),
                   jax.ShapeDtypeStruct((8, D), F32)],
        scratch_shapes=[_chunk_buf(ts, D)],
        compiler_params=_params("arbitrary"),
    )(x, *[a if d == 1 else a.reshape(d, S // d, D) for a, _, d in cots], *gains, dres, *extra)


def _final_head(x2, gain, target, name, ts=256):
    S, D = x2.shape
    ts = _tile(S, ts, 16)

    def body(x_ref, g_ref, t_ref, loss_ref, dx_ref, dxb_ref, gg_ref):
        i = pl.program_id(0)
        xv = x_ref[...]
        g = g_ref[...]
        r = lax.rsqrt(jnp.mean(xv * xv, axis=-1, keepdims=True) + EPS)
        nrm = xv * r
        err = nrm * g - t_ref[...]
        part = 0.5 * jnp.sum(jnp.mean(err * err, axis=-1, keepdims=True), axis=0, keepdims=True)
        dout = err * (1.0 / D)
        dn = dout * g
        dx = r * (dn - nrm * jnp.mean(dn * nrm, axis=-1, keepdims=True))
        dx_ref[...] = dx
        dxb_ref[...] = dx.astype(BF16)
        upd = _stack_rows([jnp.sum(dout * nrm, axis=0, keepdims=True)], 8)
        lpart = jnp.broadcast_to(part, (1, LANES))

        @pl.when(i == 0)
        def _():
            gg_ref[...] = upd
            loss_ref[...] = lpart

        @pl.when(i > 0)
        def _():
            gg_ref[...] += upd
            loss_ref[...] += lpart

    row = pl.BlockSpec((ts, D), lambda i: (i, 0))
    vec = pl.BlockSpec((1, D), lambda i: (0, 0))
    return pl.pallas_call(
        body, name=name, grid=(S // ts,), in_specs=[row, vec, row],
        out_specs=[pl.BlockSpec((1, LANES), lambda i: (0, 0)), row, row, pl.BlockSpec((8, D), lambda i: (0, 0))],
        out_shape=[jax.ShapeDtypeStruct((1, LANES), F32), jax.ShapeDtypeStruct((S, D), F32),
                   jax.ShapeDtypeStruct((S, D), BF16), jax.ShapeDtypeStruct((8, D), F32)],
        compiler_params=_params("arbitrary"),
    )(x2, gain, target)


CONV_ROWS = 64


SUBLANES = 8


def _shifted_buf(ts, cw):
    return pltpu.VMEM((SUBLANES - 1, ts + HALO - SUBLANES, cw), F32)


def _fill_shifted(shifted, buf):
    rows = shifted.shape[1]
    for s in range(1, SUBLANES):
        shifted[s - 1] = buf[s:s + rows, :]


def _window(buf, shifted, off, rows):
    s = off % SUBLANES
    base = off - s
    if s == 0:
        return buf[base:base + rows, :]
    return shifted[s - 1, base:base + rows, :]


def _conv_fwd(proj3, conv_w32, conv_b, name, ts=256, cw=256):
    _, S, E = proj3.shape
    ts = _tile(S, ts, HALO)
    cw = _tile(E, cw)
    per = ts // HALO
    rc = min(CONV_ROWS, ts)

    def body(a_ref, b_ref, ap_ref, bp_ref, w_ref, cb_ref, c_ref, ubuf, shifted):
        i = pl.program_id(0)
        up = ap_ref[...].astype(F32) * _sigmoid(bp_ref[...].astype(F32))
        ubuf[0:HALO, :] = jnp.where(i > 0, up, 0.0)
        ubuf[HALO:HALO + ts, :] = a_ref[...].astype(F32) * _sigmoid(b_ref[...].astype(F32))
        _fill_shifted(shifted, ubuf)
        for r0 in range(0, ts, rc):
            acc = jnp.broadcast_to(cb_ref[...], (rc, cw))
            for k in range(CONV_TAPS):
                off = r0 + HALO - (CONV_TAPS - 1) + k
                acc = acc + _window(ubuf, shifted, off, rc) * w_ref[k:k + 1, :]
            c_ref[r0:r0 + rc, :] = acc

    return pl.pallas_call(
        body, name=name, grid=(S // ts, E // cw),
        in_specs=[
            pl.BlockSpec((None, ts, cw), lambda i, j: (0, i, j)),
            pl.BlockSpec((None, ts, cw), lambda i, j: (1, i, j)),
            pl.BlockSpec((None, HALO, cw), lambda i, j: (0, jnp.maximum(i * per - 1, 0), j)),
            pl.BlockSpec((None, HALO, cw), lambda i, j: (1, jnp.maximum(i * per - 1, 0), j)),
            pl.BlockSpec((HALO, cw), lambda i, j: (0, j)),
            pl.BlockSpec((1, cw), lambda i, j: (0, j)),
        ],
        out_specs=pl.BlockSpec((ts, cw), lambda i, j: (i, j)),
        out_shape=jax.ShapeDtypeStruct((S, E), F32),
        scratch_shapes=[pltpu.VMEM((HALO + ts, cw), F32), _shifted_buf(ts, cw)],
        compiler_params=_params("parallel", "parallel"),
    )(proj3, proj3, proj3, proj3, conv_w32, conv_b)


def _ln_gate_fwd(c, proj3, ln_g, ln_b, name, ts=256):
    S, E = c.shape
    ts = _tile(S, ts, 16)

    def body(c_ref, z_ref, g_ref, b_ref, y_ref):
        cv = c_ref[...]
        mu = jnp.mean(cv, axis=-1, keepdims=True)
        d = cv - mu
        var = jnp.mean(d * d, axis=-1, keepdims=True)
        cn = d * lax.rsqrt(var + EPS) * g_ref[...] + b_ref[...]
        z = z_ref[...].astype(F32)
        y_ref[...] = ((cn * _sigmoid(cn)).astype(F32) * (z * _sigmoid(z))).astype(BF16)

    row = pl.BlockSpec((ts, E), lambda i: (i, 0))
    vec = pl.BlockSpec((1, E), lambda i: (0, 0))
    return pl.pallas_call(
        body, name=name, grid=(S // ts,),
        in_specs=[row, pl.BlockSpec((None, ts, E), lambda i: (2, i, 0)), vec, vec],
        out_specs=row, out_shape=jax.ShapeDtypeStruct((S, E), BF16),
        compiler_params=_params("parallel"),
    )(c, proj3, ln_g, ln_b)


def _ln_gate_bwd(c, proj3, dy, ln_g, ln_b, name, ts=256):
    S, E = c.shape
    ts = _tile(S, ts, 16)

    def body(c_ref, z_ref, dy_ref, g_ref, b_ref, dc_ref, dz_ref, acc_ref):
        i = pl.program_id(0)
        cv = c_ref[...]
        g = g_ref[...]
        mu = jnp.mean(cv, axis=-1, keepdims=True)
        d = cv - mu
        var = jnp.mean(d * d, axis=-1, keepdims=True)
        rstd = lax.rsqrt(var + EPS)
        chat = d * rstd
        cn = chat * g + b_ref[...]
        z = z_ref[...].astype(F32)
        dyv = dy_ref[...].astype(F32)
        sc = _sigmoid(cn)
        sz = _sigmoid(z)
        dcn = dyv * (z * sz) * (sc * (1.0 + cn * (1.0 - sc)))
        dz_ref[...] = (dyv * (cn * sc) * (sz * (1.0 + z * (1.0 - sz)))).astype(BF16)
        dchat = dcn * g
        dcv = rstd * (dchat - jnp.mean(dchat, axis=-1, keepdims=True)
                      - chat * jnp.mean(dchat * chat, axis=-1, keepdims=True))
        dc_ref[...] = dcv
        upd = _stack_rows([jnp.sum(dcn * chat, axis=0, keepdims=True),
                           jnp.sum(dcn, axis=0, keepdims=True),
                           jnp.sum(dcv, axis=0, keepdims=True)], 8)

        @pl.when(i == 0)
        def _():
            acc_ref[...] = upd

        @pl.when(i > 0)
        def _():
            acc_ref[...] += upd

    row = pl.BlockSpec((ts, E), lambda i: (i, 0))
    vec = pl.BlockSpec((1, E), lambda i: (0, 0))
    return pl.pallas_call(
        body, name=name, grid=(S // ts,),
        in_specs=[row, pl.BlockSpec((None, ts, E), lambda i: (2, i, 0)), row, vec, vec],
        out_specs=[row, pl.BlockSpec((None, ts, E), lambda i: (2, i, 0)), pl.BlockSpec((8, E), lambda i: (0, 0))],
        out_shape=[jax.ShapeDtypeStruct((S, E), F32), jax.ShapeDtypeStruct((3, S, E), BF16),
                   jax.ShapeDtypeStruct((8, E), F32)],
        compiler_params=_params("arbitrary"),
    )(c, proj3, dy, ln_g, ln_b)


def _conv_bwd(proj3, dc, conv_w32, dproj3, name, ts=256, cw=256):
    _, S, E = proj3.shape
    ts = _tile(S, ts, HALO)
    cw = _tile(E, cw)
    per = ts // HALO
    n_i = S // ts
    last_halo = S // HALO - 1
    rc = min(CONV_ROWS, ts)

    def body(a_ref, b_ref, dc_ref, dcn_ref, w_ref, dp_in, dab_ref, dw_ref, dcbuf, ubuf, dwacc, shifted):
        del dp_in
        i = pl.program_id(1)
        dcbuf[0:ts, :] = dc_ref[...]
        dcbuf[ts:ts + HALO, :] = jnp.where(i < n_i - 1, dcn_ref[...], 0.0)
        _fill_shifted(shifted, dcbuf)
        av = a_ref[...].astype(F32)
        sb = _sigmoid(b_ref[...].astype(F32))
        ubuf[...] = av * sb

        @pl.when(i == 0)
        def _():
            dwacc[...] = jnp.zeros_like(dwacc)

        for r0 in range(0, ts, rc):
            uv = ubuf[r0:r0 + rc, :]
            du = jnp.zeros((rc, cw), F32)
            for d in range(CONV_TAPS):
                k = CONV_TAPS - 1 - d
                win = _window(dcbuf, shifted, r0 + d, rc)
                du = du + win * w_ref[k:k + 1, :]
                dwacc[k:k + 1, :] += jnp.sum(uv * win, axis=0, keepdims=True)
            a_c = a_ref[r0:r0 + rc, :].astype(F32)
            s_c = _sigmoid(b_ref[r0:r0 + rc, :].astype(F32))
            dab_ref[0, r0:r0 + rc, :] = (du * s_c).astype(BF16)
            dab_ref[1, r0:r0 + rc, :] = (du * a_c * s_c * (1.0 - s_c)).astype(BF16)

        @pl.when(i == n_i - 1)
        def _():
            dw_ref[...] = dwacc[...]

    return pl.pallas_call(
        body, name=name, grid=(E // cw, n_i),
        in_specs=[
            pl.BlockSpec((None, ts, cw), lambda j, i: (0, i, j)),
            pl.BlockSpec((None, ts, cw), lambda j, i: (1, i, j)),
            pl.BlockSpec((ts, cw), lambda j, i: (i, j)),
            pl.BlockSpec((HALO, cw), lambda j, i: (jnp.minimum((i + 1) * per, last_halo), j)),
            pl.BlockSpec((HALO, cw), lambda j, i: (0, j)),
            ANY,
        ],
        out_specs=[pl.BlockSpec((2, ts, cw), lambda j, i: (0, i, j)),
                   pl.BlockSpec((HALO, cw), lambda j, i: (0, j))],
        out_shape=[jax.ShapeDtypeStruct((3, S, E), BF16), jax.ShapeDtypeStruct((HALO, E), F32)],
        scratch_shapes=[pltpu.VMEM((ts + HALO, cw), F32), pltpu.VMEM((ts, cw), F32), pltpu.VMEM((HALO, cw), F32),
                        _shifted_buf(ts, cw)],
        input_output_aliases={5: 0},
        compiler_params=_params("parallel", "arbitrary"),
    )(proj3, proj3, dc, dc, conv_w32, dproj3)


def _bucket_table(dil):
    delta = (np.arange(BLOCK)[:, None] + BLOCK) - np.arange(2 * BLOCK)[None, :]
    dist = np.clip(delta, 0, None) * dil
    large = MAX_EXACT + (np.log(np.maximum(dist, 1).astype(np.float32) / MAX_EXACT)
                         / np.log(MAX_DISTANCE / MAX_EXACT) * (N_BUCKETS - MAX_EXACT)).astype(np.int32)
    large = np.minimum(large, N_BUCKETS - 1)
    return np.where(dist < MAX_EXACT, dist, large).astype(np.int32).reshape(-1)


def _onehot(dil):
    tbl = jnp.asarray(_bucket_table(dil))
    return (tbl[None, :] == jnp.arange(LANES, dtype=jnp.int32)[:, None]).astype(BF16)


def _split3(v):
    hi = v.astype(BF16)
    r1 = v - hi.astype(F32)
    mid = r1.astype(BF16)
    lo = (r1 - mid.astype(F32)).astype(BF16)
    return hi, mid, lo


def _bias_table(rb_t, onehot, name):
    H = rb_t.shape[0]
    N = onehot.shape[1]

    def body(r_ref, oh_ref, o_ref):
        oh = oh_ref[...]
        hi, mid, lo = _split3(r_ref[...])
        o_ref[...] = (_dot(lo, oh, NN) + _dot(mid, oh, NN)) + _dot(hi, oh, NN)

    return pl.pallas_call(
        body, name=name, in_specs=[VMEM_SPEC, VMEM_SPEC], out_specs=VMEM_SPEC,
        out_shape=jax.ShapeDtypeStruct((H, N), F32),
        compiler_params=pltpu.CompilerParams(vmem_limit_bytes=VMEM_LIMIT),
    )(rb_t, onehot)


def _bias_grad(dbs, onehots, name):
    H = dbs[0].shape[0]
    n = len(dbs)

    def body(*refs):
        acc = jnp.zeros((H, LANES), F32)
        for q in range(n):
            oh = refs[n + q][...]
            hi, mid, lo = _split3(refs[q][...])
            acc = acc + ((_dot(lo, oh, NT) + _dot(mid, oh, NT)) + _dot(hi, oh, NT))
        refs[2 * n][...] = acc

    return pl.pallas_call(
        body, name=name, in_specs=[VMEM_SPEC] * (2 * n), out_specs=VMEM_SPEC,
        out_shape=jax.ShapeDtypeStruct((H, LANES), F32),
        compiler_params=pltpu.CompilerParams(vmem_limit_bytes=VMEM_LIMIT),
    )(*dbs, *onehots)


def _pair_mask(has_prev):
    qi = lax.broadcasted_iota(jnp.int32, (BLOCK, 2 * BLOCK), 0)
    ki = lax.broadcasted_iota(jnp.int32, (BLOCK, 2 * BLOCK), 1)
    prev = jnp.logical_and(jnp.logical_and(ki < BLOCK, ki >= qi), has_prev)
    return jnp.logical_or(prev, jnp.logical_and(ki >= BLOCK, ki - BLOCK <= qi))


def _attn_fwd(q, kv, bias, dil, name):
    S, A = q.shape
    H = A // HEAD_DIM
    L = S // dil
    nb = L // BLOCK
    qv = q.reshape(dil, L, A)
    kvv = kv.reshape(2, dil, L, A)

    def body(q_ref, kp_ref, kc_ref, vp_ref, vc_ref, b_ref, o_ref, lse_ref):
        i = pl.program_id(1)
        qi = lax.broadcasted_iota(jnp.int32, (BLOCK, BLOCK), 0)
        ki = lax.broadcasted_iota(jnp.int32, (BLOCK, BLOCK), 1)
        mask = _pair_mask(i > 0)
        lane = lax.broadcasted_iota(jnp.int32, (BLOCK, LANES), 1)
        lse_acc = jnp.zeros((BLOCK, LANES), F32)

        def scores(h):
            sl = slice(h * HEAD_DIM, (h + 1) * HEAD_DIM)
            return _dot(q_ref[:, sl], jnp.concatenate([kp_ref[:, sl], kc_ref[:, sl]], axis=0), NT)

        ahead = [scores(h) for h in range(min(SCORES_AHEAD, H))]
        for h in range(H):
            sl = slice(h * HEAD_DIM, (h + 1) * HEAD_DIM)
            raw = ahead.pop(0)
            if h + SCORES_AHEAD < H:
                ahead.append(scores(h + SCORES_AHEAD))
            s = jnp.where(mask, raw * SCALE + b_ref[h], NEG)
            m = jnp.max(s, axis=-1, keepdims=True)
            p = jnp.exp(s - m)
            den = jnp.sum(p, axis=-1, keepdims=True)
            acc = _dot(p.astype(BF16), jnp.concatenate([vp_ref[:, sl], vc_ref[:, sl]], axis=0), NN)
            o_ref[:, sl] = acc / den
            lse_acc = jnp.where(lane == h, m + jnp.log(den), lse_acc)
        lse_ref[...] = lse_acc

    def blk(slab, prev):
        if prev:
            return pl.BlockSpec((None, None, BLOCK, A), lambda r, i: (slab, r, jnp.maximum(i - 1, 0), 0))
        return pl.BlockSpec((None, None, BLOCK, A), lambda r, i: (slab, r, i, 0))

    o, lse = pl.pallas_call(
        body, name=name, grid=(dil, nb),
        in_specs=[pl.BlockSpec((None, BLOCK, A), lambda r, i: (r, i, 0)),
                  blk(0, True), blk(0, False), blk(1, True), blk(1, False),
                  pl.BlockSpec((H, BLOCK, 2 * BLOCK), lambda r, i: (0, 0, 0))],
        out_specs=[pl.BlockSpec((None, BLOCK, A), lambda r, i: (r, i, 0)),
                   pl.BlockSpec((None, BLOCK, LANES), lambda r, i: (r, i, 0))],
        out_shape=[jax.ShapeDtypeStruct((dil, L, A), F32), jax.ShapeDtypeStruct((dil, L, LANES), F32)],
        compiler_params=_params("parallel", "parallel"),
    )(qv, kvv, kvv, kvv, kvv, bias)
    return o.reshape(S, A), lse.reshape(S, LANES)


def _attn_merge(os_, lses, z, dils, name, ts=256):
    S, A = z.shape
    H = A // HEAD_DIM
    ts = _tile(S, ts, 16 * max(dils))
    n = len(os_)

    def body(*refs):
        z_ref = refs[2 * n]
        y_ref, om_ref = refs[2 * n + 1:2 * n + 3]
        lse_refs = refs[2 * n + 3:3 * n + 3]
        o_refs = refs[3 * n + 3:4 * n + 3]
        l_bufs = refs[4 * n + 3:5 * n + 3]
        lse_buf = refs[5 * n + 3]
        ls = []
        for q, d in enumerate(dils):
            if d == 1:
                ls.append(refs[n + q][...])
            else:
                _store_token_order(o_refs[q], refs[q], d)
                _store_token_order(l_bufs[q], refs[n + q], d)
                ls.append(l_bufs[q][0])
        m = ls[0]
        for q in range(1, n):
            m = jnp.maximum(m, ls[q])
        es = [jnp.exp(v - m) for v in ls]
        den = es[0]
        for q in range(1, n):
            den = den + es[q]
        alphas = [e / den for e in es]
        lse = m + jnp.log(den)
        lse_buf[0] = lse
        for q, d in enumerate(dils):
            if d == 1:
                lse_refs[q][...] = lse
            else:
                _emit_group_order(lse_refs[q], lse_buf, d, F32)
        lane = lax.broadcasted_iota(jnp.int32, (ts, LANES), 1)
        for h in range(H):
            sl = slice(h * HEAD_DIM, (h + 1) * HEAD_DIM)
            om = jnp.zeros((ts, HEAD_DIM), F32)
            for q, d in enumerate(dils):
                o_h = refs[q][:, sl] if d == 1 else o_refs[q][h]
                om = om + _lane_col(alphas[q], h, lane) * o_h
            z = z_ref[:, sl].astype(F32)
            y_ref[:, sl] = (om * (z * _sigmoid(z))).astype(BF16)
            om_ref[:, sl] = om.astype(BF16)

    row = pl.BlockSpec((ts, A), lambda i: (i, 0))
    outs = pl.pallas_call(
        body, name=name, grid=(S // ts,),
        in_specs=[_group_spec(d, ts, A) for d in dils] + [_group_spec(d, ts, LANES) for d in dils] + [row],
        out_specs=[row, row] + [_group_spec(d, ts, LANES) for d in dils],
        out_shape=[jax.ShapeDtypeStruct((S, A), BF16), jax.ShapeDtypeStruct((S, A), BF16)]
        + [_group_shape(d, S, LANES, F32) for d in dils],
        scratch_shapes=[_chunk_buf(ts, A)] * n + [_chunk_buf(ts, LANES)] * (n + 1),
        compiler_params=_params("parallel"),
    )(*[o if d == 1 else o.reshape(d, S // d, A) for o, d in zip(os_, dils)],
      *[v if d == 1 else v.reshape(d, S // d, LANES) for v, d in zip(lses, dils)], z)
    return outs[0], outs[1], [v.reshape(S, LANES) for v in outs[2:]]


def _gate_bwd(dy, om, z, dils, name, ts=256):
    S, A = dy.shape
    H = A // HEAD_DIM
    ts = _tile(S, ts, 16 * max(dils))
    n = len(dils)

    def body(*refs):
        dy_ref, om_ref, z_ref = refs[:3]
        do_refs = refs[3:3 + n]
        dh_refs = refs[3 + n:3 + 2 * n]
        dz_ref = refs[3 + 2 * n]
        do_buf, dh_buf = refs[4 + 2 * n:6 + 2 * n]
        lane = lax.broadcasted_iota(jnp.int32, (ts, LANES), 1)
        acc = jnp.zeros((ts, LANES), F32)
        for h in range(H):
            sl = slice(h * HEAD_DIM, (h + 1) * HEAD_DIM)
            dyv = dy_ref[:, sl].astype(F32)
            omv = om_ref[:, sl].astype(F32)
            zv = z_ref[:, sl].astype(F32)
            sz = _sigmoid(zv)
            dob = (dyv * (zv * sz)).astype(BF16)
            do_buf[h] = dob.astype(F32)
            for q, d in enumerate(dils):
                if d == 1:
                    do_refs[q][:, sl] = dob
            dz_ref[:, sl] = (dyv * omv * (sz * (1.0 + zv * (1.0 - sz)))).astype(BF16)
            acc = jnp.where(lane == h, jnp.sum(dob.astype(F32) * omv, axis=-1, keepdims=True), acc)
        dh_buf[0] = acc
        for q, d in enumerate(dils):
            if d == 1:
                dh_refs[q][...] = acc
            else:
                _emit_group_order(do_refs[q], do_buf, d, BF16)
                _emit_group_order(dh_refs[q], dh_buf, d, F32)

    row = pl.BlockSpec((ts, A), lambda i: (i, 0))
    outs = pl.pallas_call(
        body, name=name, grid=(S // ts,), in_specs=[row, row, row],
        out_specs=[_group_spec(d, ts, A) for d in dils] + [_group_spec(d, ts, LANES) for d in dils] + [row],
        out_shape=[_group_shape(d, S, A, BF16) for d in dils] + [_group_shape(d, S, LANES, F32) for d in dils]
        + [jax.ShapeDtypeStruct((S, A), BF16)],
        scratch_shapes=[_chunk_buf(ts, A), _chunk_buf(ts, LANES)],
        compiler_params=_params("parallel"),
    )(dy, om, z)
    return ([v.reshape(S, A) for v in outs[:n]], [v.reshape(S, LANES) for v in outs[n:2 * n]], outs[2 * n])


def _attn_bwd(q, kv, do, lse, dh, bias, dil, name):
    S, A = q.shape
    H = A // HEAD_DIM
    L = S // dil
    nb = L // BLOCK
    qv = q.reshape(dil, L, A)
    kvv = kv.reshape(2, dil, L, A)
    dov = do.reshape(dil, L, A)
    lsev = lse.reshape(dil, L, LANES)
    dhv = dh.reshape(dil, L, LANES)

    def body(*refs):
        (q_ref, qn_ref, kp_ref, kc_ref, vp_ref, vc_ref, do_ref, don_ref, l_ref, ln_ref, d_ref, dn_ref,
         b_ref) = refs[:13]
        dq_ref, dkv_ref, db_ref = refs[13:16]
        r = pl.program_id(0)
        i = pl.program_id(1)
        qi = lax.broadcasted_iota(jnp.int32, (BLOCK, BLOCK), 0)
        ki = lax.broadcasted_iota(jnp.int32, (BLOCK, BLOCK), 1)
        mask_c = ki <= qi
        band = ki >= qi
        mask_p = jnp.logical_and(band, i > 0)
        mask_n = jnp.logical_and(band, i < nb - 1)
        lane = lax.broadcasted_iota(jnp.int32, (BLOCK, LANES), 1)

        @pl.when(jnp.logical_and(r == 0, i == 0))
        def _():
            db_ref[...] = jnp.zeros_like(db_ref)

        def products(h):
            sl = slice(h * HEAD_DIM, (h + 1) * HEAD_DIM)
            q_i, q_n = q_ref[:, sl], qn_ref[:, sl]
            k_p, k_c = kp_ref[:, sl], kc_ref[:, sl]
            v_p, v_c = vp_ref[:, sl], vc_ref[:, sl]
            do_i, do_n = do_ref[:, sl], don_ref[:, sl]
            return (_dot(q_i, k_c, NT), _dot(do_i, v_c, NT), _dot(q_i, k_p, NT), _dot(do_i, v_p, NT),
                    _dot(q_n, k_c, NT), _dot(do_n, v_c, NT))

        ahead = [products(h) for h in range(min(PRODUCTS_AHEAD, H))]
        for h in range(H):
            sl = slice(h * HEAD_DIM, (h + 1) * HEAD_DIM)
            s1, dp1, s2, dp2, s3, dp3 = ahead.pop(0)
            if h + PRODUCTS_AHEAD < H:
                ahead.append(products(h + PRODUCTS_AHEAD))
            q_i, q_n = q_ref[:, sl], qn_ref[:, sl]
            k_p, k_c = kp_ref[:, sl], kc_ref[:, sl]
            do_i, do_n = do_ref[:, sl], don_ref[:, sl]
            l_i, l_n = _lane_col(l_ref[...], h, lane), _lane_col(ln_ref[...], h, lane)
            d_i, d_n = _lane_col(d_ref[...], h, lane), _lane_col(dn_ref[...], h, lane)
            b_c = b_ref[h, :, BLOCK:]
            b_p = b_ref[h, :, :BLOCK]
            p1 = jnp.exp(jnp.where(mask_c, s1 * SCALE + b_c, NEG) - l_i)
            ds1 = p1 * (dp1 - d_i)
            ds1b = ds1.astype(BF16)
            p1b = p1.astype(BF16)
            p2 = jnp.exp(jnp.where(mask_p, s2 * SCALE + b_p, NEG) - l_i)
            ds2 = p2 * (dp2 - d_i)
            ds2b = ds2.astype(BF16)
            p3 = jnp.exp(jnp.where(mask_n, s3 * SCALE + b_p, NEG) - l_n)
            ds3b = (p3 * (dp3 - d_n)).astype(BF16)
            p3b = p3.astype(BF16)
            dq = _dot(ds1b, k_c, NN) + _dot(ds2b, k_p, NN)
            dk = _dot(ds1b, q_i, TN) + _dot(ds3b, q_n, TN)
            dv = _dot(p1b, do_i, TN) + _dot(p3b, do_n, TN)
            dq_ref[:, sl] = (dq * SCALE).astype(BF16)
            dkv_ref[0, :, sl] = (dk * SCALE).astype(BF16)
            dkv_ref[1, :, sl] = dv.astype(BF16)
            db_ref[h, :, BLOCK:] += ds1
            db_ref[h, :, :BLOCK] += ds2

    def blk(slab, shift):
        if shift < 0:
            return pl.BlockSpec((None, None, BLOCK, A), lambda r, i: (slab, r, jnp.maximum(i - 1, 0), 0))
        return pl.BlockSpec((None, None, BLOCK, A), lambda r, i: (slab, r, i, 0))

    def row(width, shift):
        if shift > 0:
            return pl.BlockSpec((None, BLOCK, width), lambda r, i: (r, jnp.minimum(i + 1, nb - 1), 0))
        return pl.BlockSpec((None, BLOCK, width), lambda r, i: (r, i, 0))

    in_specs = [row(A, 0), row(A, 1), blk(0, -1), blk(0, 0), blk(1, -1), blk(1, 0),
                row(A, 0), row(A, 1), row(LANES, 0), row(LANES, 1), row(LANES, 0), row(LANES, 1),
                pl.BlockSpec((H, BLOCK, 2 * BLOCK), lambda r, i: (0, 0, 0))]
    dq, dkv, db = pl.pallas_call(
        body, name=name, grid=(dil, nb), in_specs=in_specs,
        out_specs=[pl.BlockSpec((None, BLOCK, A), lambda r, i: (r, i, 0)),
                   pl.BlockSpec((2, None, BLOCK, A), lambda r, i: (0, r, i, 0)),
                   pl.BlockSpec((H, BLOCK, 2 * BLOCK), lambda r, i: (0, 0, 0))],
        out_shape=[jax.ShapeDtypeStruct((dil, L, A), BF16), jax.ShapeDtypeStruct((2, dil, L, A), BF16),
                   jax.ShapeDtypeStruct((H, BLOCK, 2 * BLOCK), F32)],
        compiler_params=_params("arbitrary", "arbitrary"),
    )(qv, qv, kvv, kvv, kvv, kvv, dov, dov, lsev, lsev, dhv, dhv, bias)
    return dq.reshape(S, A), dkv.reshape(2, S, A), db


def _sum_leading(stack, out_dtype, name, tr=256, tc=2048):
    n, R, C = stack.shape
    tr = _tile(R, tr, 16)
    tc = _tile(C, tc)

    def body(s_ref, o_ref):
        acc = s_ref[0].astype(F32)
        for q in range(1, n):
            acc = acc + s_ref[q].astype(F32)
        o_ref[...] = acc.astype(out_dtype)

    return pl.pallas_call(
        body, name=name, grid=(R // tr, C // tc),
        in_specs=[pl.BlockSpec((n, tr, tc), lambda i, j: (0, i, j))],
        out_specs=pl.BlockSpec((tr, tc), lambda i, j: (i, j)),
        out_shape=jax.ShapeDtypeStruct((R, C), out_dtype),
        compiler_params=_params("parallel", "parallel"),
    )(stack)


def _add_half(g, t, c_idx, kind, name, after=None, tr=256, tc=2048):
    R, C = t.shape
    tr = _tile(R, tr, 16)
    tc = _tile(C, tc)
    nrb, ncb = R // tr, C // tc
    extra = _as_list(after)

    def body(c_ref, g_ref, t_ref, *rest):
        del c_ref
        o_ref = rest[len(extra)]
        o_ref[...] = (g_ref[...].astype(F32) + t_ref[...].astype(F32)).astype(BF16)

    if kind == "col":
        g_map = lambda i, j, c_ref: (c_ref[0] * nrb + i, j)
    else:
        g_map = lambda i, j, c_ref: (i, c_ref[0] * ncb + j)
    same = lambda i, j, c_ref: (i, j)
    return pl.pallas_call(
        body, name=name,
        grid_spec=pltpu.PrefetchScalarGridSpec(
            num_scalar_prefetch=1, grid=(nrb, ncb),
            in_specs=[pl.BlockSpec((tr, tc), g_map), pl.BlockSpec((tr, tc), same)] + [ANY] * len(extra),
            out_specs=pl.BlockSpec((tr, tc), same)),
        out_shape=jax.ShapeDtypeStruct((R, C), BF16),
        compiler_params=_params("parallel", "parallel"),
    )(c_idx, g, t, *extra)


def _cast_into_full(w, kind, chip_idx, name, tr=256, tc=2048):
    R, C = w.shape
    tr = _tile(R, tr, 16)
    tc = _tile(C, tc)
    nrb, ncb = R // tr, C // tc

    def body(k_ref, w_ref, o_ref):
        del k_ref
        o_ref[...] = w_ref[...].astype(BF16)

    if kind == "col":
        o_map = lambda i, j, k_ref: (i, k_ref[0] * ncb + j)
        full = (R, N_CHIPS * C)
    else:
        o_map = lambda i, j, k_ref: (k_ref[0] * nrb + i, j)
        full = (N_CHIPS * R, C)
    return pl.pallas_call(
        body, name=name,
        grid_spec=pltpu.PrefetchScalarGridSpec(
            num_scalar_prefetch=1, grid=(nrb, ncb),
            in_specs=[pl.BlockSpec((tr, tc), lambda i, j, k_ref: (i, j))],
            out_specs=pl.BlockSpec((tr, tc), o_map)),
        out_shape=jax.ShapeDtypeStruct(full, BF16),
        compiler_params=_params("parallel", "parallel"),
    )(chip_idx, w)


def _sum_into_shard(p, u, idx, kind, name, tr=256, tc=2048):
    _, R, C = u.shape
    tr = _tile(R, tr, 16)
    tc = _tile(C, tc)
    nrb, ncb = R // tr, C // tc

    def body(i_ref, p_ref, u_ref, o_ref):
        del i_ref
        acc = p_ref[...].astype(F32)
        for q in range(N_CHIPS - 1):
            acc = acc + u_ref[q].astype(F32)
        o_ref[...] = acc

    if kind == "col":
        p_map = lambda i, j, r: (i, r[0] * ncb + j)
        o_map = lambda i, j, r: (r[1] * nrb + i, j)
        full = (2 * R, C)
    else:
        p_map = lambda i, j, r: (r[0] * nrb + i, j)
        o_map = lambda i, j, r: (i, r[1] * ncb + j)
        full = (R, 2 * C)
    return pl.pallas_call(
        body, name=name,
        grid_spec=pltpu.PrefetchScalarGridSpec(
            num_scalar_prefetch=1, grid=(nrb, ncb),
            in_specs=[pl.BlockSpec((tr, tc), p_map), pl.BlockSpec((N_CHIPS - 1, tr, tc), lambda i, j, r: (0, i, j))],
            out_specs=pl.BlockSpec((tr, tc), o_map)),
        out_shape=jax.ShapeDtypeStruct(full, F32),
        compiler_params=_params("parallel", "parallel"),
    )(idx, p, u)


def _adamw(w, g, m, v, name, tr=256, tc=2048):
    R, C = w.shape
    tr = _tile(R, tr, 8)
    tc = _tile(C, tc)
    c1 = 1.0 - ADAM_B1 ** ADAM_STEP
    c2 = 1.0 - ADAM_B2 ** ADAM_STEP

    def body(w_ref, g_ref, m_ref, v_ref, d_ref, nm_ref, nv_ref):
        gv = g_ref[...]
        nm = ADAM_B1 * m_ref[...] + (1.0 - ADAM_B1) * gv
        nv = ADAM_B2 * v_ref[...] + (1.0 - ADAM_B2) * (gv * gv)
        d_ref[...] = -ADAM_LR * ((nm / c1) / (jnp.sqrt(nv / c2) + ADAM_EPS) + ADAM_WD * w_ref[...])
        nm_ref[...] = nm
        nv_ref[...] = nv

    blk = pl.BlockSpec((tr, tc), lambda i, j: (i, j))
    sh = jax.ShapeDtypeStruct((R, C), F32)
    return pl.pallas_call(
        body, name=name, grid=(R // tr, C // tc), in_specs=[blk] * 4, out_specs=[blk] * 3,
        out_shape=[sh, sh, sh], compiler_params=_params("parallel", "parallel"),
    )(w, g, m, v)


def _xyc():
    return lax.axis_index("x"), lax.axis_index("y"), lax.axis_index("c")


def _drain(copies):
    for cp in copies:
        if cp.is_remote:
            cp.wait_send()
        else:
            cp.wait()


def _other_chips(x, y):
    return [(1 - x, y), (x, 1 - y), (1 - x, 1 - y)]


def _allgather_small(blk, name, after=None):
    R, C = blk.shape
    extra = _as_list(after)

    def body(*refs):
        x_ref = refs[0]
        out_ref, send_sems, recv_sems, local_sem = refs[1 + len(extra):]
        x, y, c = _xyc()
        me = 4 * x + 2 * y + c
        mine = pltpu.make_async_copy(x_ref, out_ref.at[me], local_sem)
        mine.start()
        peers = []
        for k in range(1, N_DEV):
            px = 1 - x if (k >> 2) & 1 else x
            py = 1 - y if (k >> 1) & 1 else y
            pc = 1 - c if k & 1 else c
            peers.append((px, py, pc))
        sends = []
        for k, peer in enumerate(peers):
            cp = pltpu.make_async_remote_copy(
                src_ref=x_ref, dst_ref=out_ref.at[me], send_sem=send_sems.at[k], recv_sem=recv_sems.at[k],
                device_id=peer, device_id_type=MESH)
            cp.start()
            sends.append(cp)
        for k, (px, py, pc) in enumerate(peers):
            pltpu.make_async_remote_copy(
                src_ref=x_ref, dst_ref=out_ref.at[4 * px + 2 * py + pc], send_sem=send_sems.at[k],
                recv_sem=recv_sems.at[k], device_id=(px, py, pc), device_id_type=MESH).wait_recv()
        for cp in sends:
            cp.wait_send()
        mine.wait()

    return pl.pallas_call(
        body, name=name, in_specs=[VMEM_SPEC] + [ANY] * len(extra), out_specs=VMEM_SPEC,
        out_shape=jax.ShapeDtypeStruct((N_DEV, R, C), blk.dtype),
        scratch_shapes=[pltpu.SemaphoreType.DMA((N_DEV - 1,)), pltpu.SemaphoreType.DMA((N_DEV - 1,)),
                        pltpu.SemaphoreType.DMA],
        compiler_params=pltpu.CompilerParams(vmem_limit_bytes=VMEM_LIMIT),
    )(blk, *extra)


def _full_region(ref, kind, chip, half, shard_shape):
    r, cn = shard_shape
    hr = r // 2
    if kind == "col":
        rows = pl.ds(0, r) if half is None else pl.ds(pl.multiple_of(half * hr, 16), hr)
        return ref.at[rows, pl.ds(pl.multiple_of(chip * cn, LANES), cn)]
    if half is None:
        return ref.at[pl.ds(pl.multiple_of(chip * r, 16), r), :]
    return ref.at[pl.ds(pl.multiple_of(chip * r + half * hr, 16), hr), :]


def _allgather_weights(fulls, kinds, shapes, name):
    n = len(fulls)

    def body(*refs):
        outs = refs[n:2 * n]
        send_sems, recv_sems = refs[2 * n:]
        x, y, c = _xyc()
        chip = 2 * x + y
        sib = (x, y, 1 - c)
        others = _other_chips(x, y)
        started = []
        for w in range(n):
            mine = _full_region(outs[w], kinds[w], chip, c, shapes[w])
            for j, (ox, oy) in enumerate(others):
                cp = pltpu.make_async_remote_copy(
                    src_ref=mine, dst_ref=mine, send_sem=send_sems.at[6 * w + j], recv_sem=recv_sems.at[6 * w + j],
                    device_id=(ox, oy, c), device_id_type=MESH)
                cp.start()
                started.append(cp)
        for w in range(n):
            for j, (ox, oy) in enumerate(others):
                landed = _full_region(outs[w], kinds[w], 2 * ox + oy, c, shapes[w])
                pltpu.make_async_remote_copy(
                    src_ref=landed, dst_ref=landed, send_sem=send_sems.at[6 * w + j], recv_sem=recv_sems.at[6 * w + j],
                    device_id=(ox, oy, c), device_id_type=MESH).wait_recv()
                cp = pltpu.make_async_remote_copy(
                    src_ref=landed, dst_ref=landed, send_sem=send_sems.at[6 * w + 3 + j],
                    recv_sem=recv_sems.at[6 * w + 3 + j], device_id=sib, device_id_type=MESH)
                cp.start()
                started.append(cp)
        for w in range(n):
            for j, (ox, oy) in enumerate(others):
                theirs = _full_region(outs[w], kinds[w], 2 * ox + oy, 1 - c, shapes[w])
                pltpu.make_async_remote_copy(
                    src_ref=theirs, dst_ref=theirs, send_sem=send_sems.at[6 * w + 3 + j],
                    recv_sem=recv_sems.at[6 * w + 3 + j], device_id=sib, device_id_type=MESH).wait_recv()
        _drain(started)

    return pl.pallas_call(
        body, name=name, in_specs=[ANY] * n, out_specs=[ANY] * n,
        out_shape=[jax.ShapeDtypeStruct(f.shape, f.dtype) for f in fulls],
        input_output_aliases={w: w for w in range(n)},
        scratch_shapes=[pltpu.SemaphoreType.DMA((6 * n,)), pltpu.SemaphoreType.DMA((6 * n,))],
    )(*fulls)


def _region_of_size(ref, kind, shard_shape, count):
    r, cn = shard_shape
    if kind == "col":
        return ref.at[pl.ds(0, r // 2), pl.ds(0, count * cn)]
    return ref.at[pl.ds(0, count * (r // 2)), :]


def _allgather_weights_seq(fulls, kinds, shapes, name, collective_id):
    n = len(fulls)
    refs = [jax.new_ref(f, memory_space=pltpu.MemorySpace.HBM) for f in fulls]

    def body(send_sems, recv_sems):
        x, y, c = _xyc()
        chip = 2 * x + y
        sib = (x, y, 1 - c)
        others = _other_chips(x, y)
        peers = [(ox, oy, c) for ox, oy in others] + [sib]
        barrier = pltpu.get_barrier_semaphore()
        for peer in peers:
            pl.semaphore_signal(barrier, inc=1, device_id=peer, device_id_type=MESH)
        pl.semaphore_wait(barrier, len(peers))

        def copy(w, region, sem, to):
            return pltpu.make_async_remote_copy(src_ref=region, dst_ref=region, send_sem=send_sems.at[sem],
                                                recv_sem=recv_sems.at[sem], device_id=to, device_id_type=MESH)

        for w in range(n):
            mine = _full_region(refs[w], kinds[w], chip, c, shapes[w])
            for ox, oy in others:
                copy(w, mine, 2 * w, (ox, oy, c)).start()
        for w in range(n):
            three = _region_of_size(refs[w], kinds[w], shapes[w], 3)
            copy(w, three, 2 * w, sib).wait_recv()
            for ox, oy in others:
                copy(w, _full_region(refs[w], kinds[w], 2 * ox + oy, c, shapes[w]), 2 * w + 1, sib).start()
        for w in range(n):
            three = _region_of_size(refs[w], kinds[w], shapes[w], 3)
            copy(w, three, 2 * w + 1, sib).wait_recv()
            copy(w, three, 2 * w, sib).wait_send()
            copy(w, three, 2 * w + 1, sib).wait_send()

    pl.kernel(
        body, out_type=(), mesh=plsc.ScalarSubcoreMesh(axis_name="seq", num_cores=1), name=name,
        scratch_types=[pltpu.SemaphoreType.DMA((2 * n,)), pltpu.SemaphoreType.DMA((2 * n,))],
        compiler_params=pltpu.CompilerParams(collective_id=collective_id),
    )()
    return [r[...] for r in refs]


def _half_of(ref, kind, half):
    r, cn = ref.shape
    if kind == "col":
        return ref.at[pl.ds(pl.multiple_of(half * (r // 2), 16), r // 2), :]
    return ref.at[:, pl.ds(pl.multiple_of(half * (cn // 2), LANES), cn // 2)]


def _shard_of(ref, kind, chip):
    r, cn = ref.shape
    if kind == "col":
        return ref.at[:, pl.ds(pl.multiple_of(chip * (cn // N_CHIPS), LANES), cn // N_CHIPS)]
    return ref.at[pl.ds(pl.multiple_of(chip * (r // N_CHIPS), 16), r // N_CHIPS), :]


def _exchange_halves(grads, kinds, name):
    n = len(grads)

    def body(*refs):
        gs = refs[:n]
        ts = refs[n:2 * n]
        send_sems, recv_sems = refs[2 * n:]
        x, y, c = _xyc()
        cps = []
        for w in range(n):
            cp = pltpu.make_async_remote_copy(
                src_ref=_half_of(gs[w], kinds[w], 1 - c), dst_ref=ts[w], send_sem=send_sems.at[w],
                recv_sem=recv_sems.at[w], device_id=(x, y, 1 - c), device_id_type=MESH)
            cp.start()
            cps.append(cp)
        for cp in cps:
            cp.wait()

    out_shape = []
    for gr, kind in zip(grads, kinds):
        r, cn = gr.shape
        out_shape.append(jax.ShapeDtypeStruct((r // 2, cn) if kind == "col" else (r, cn // 2), gr.dtype))
    return pl.pallas_call(
        body, name=name, in_specs=[ANY] * n, out_specs=[ANY] * n, out_shape=out_shape,
        scratch_shapes=[pltpu.SemaphoreType.DMA((n,)), pltpu.SemaphoreType.DMA((n,))],
    )(*grads)


def _exchange_halves_seq(grads, kinds, name, collective_id):
    n = len(grads)

    def body(*refs):
        gs = refs[:n]
        ts = refs[n:2 * n]
        send_sems, recv_sems = refs[2 * n:]
        x, y, c = _xyc()
        sib = (x, y, 1 - c)
        barrier = pltpu.get_barrier_semaphore()
        pl.semaphore_signal(barrier, inc=1, device_id=sib, device_id_type=MESH)
        pl.semaphore_wait(barrier, 1)
        cps = []
        for w in range(n):
            cp = pltpu.make_async_remote_copy(
                src_ref=_half_of(gs[w], kinds[w], 1 - c), dst_ref=ts[w], send_sem=send_sems.at[w],
                recv_sem=recv_sems.at[w], device_id=sib, device_id_type=MESH)
            cp.start()
            cps.append(cp)
        for cp in cps:
            cp.wait()

    out_type = []
    for gr, kind in zip(grads, kinds):
        r, cn = gr.shape
        out_type.append(jax.ShapeDtypeStruct((r // 2, cn) if kind == "col" else (r, cn // 2), gr.dtype))
    return pl.kernel(
        body, out_type=out_type, mesh=plsc.ScalarSubcoreMesh(axis_name="seq", num_cores=1), name=name,
        scratch_types=[pltpu.SemaphoreType.DMA((n,)), pltpu.SemaphoreType.DMA((n,))],
        compiler_params=pltpu.CompilerParams(collective_id=collective_id),
    )(*grads)


def _scatter_partials(parts, kinds, name):
    n = len(parts)

    def body(*refs):
        ps = refs[:n]
        us = refs[n:2 * n]
        send_sems, recv_sems = refs[2 * n:]
        x, y, c = _xyc()
        others = _other_chips(x, y)
        cps = []
        for w in range(n):
            for j, (ox, oy) in enumerate(others):
                cp = pltpu.make_async_remote_copy(
                    src_ref=_shard_of(ps[w], kinds[w], 2 * ox + oy), dst_ref=us[w].at[j],
                    send_sem=send_sems.at[3 * w + j], recv_sem=recv_sems.at[3 * w + j],
                    device_id=(ox, oy, c), device_id_type=MESH)
                cp.start()
                cps.append(cp)
        for cp in cps:
            cp.wait()

    out_shape = []
    for p, kind in zip(parts, kinds):
        r, cn = p.shape
        hs = (r, cn // N_CHIPS) if kind == "col" else (r // N_CHIPS, cn)
        out_shape.append(jax.ShapeDtypeStruct((N_CHIPS - 1,) + hs, p.dtype))
    return pl.pallas_call(
        body, name=name, in_specs=[ANY] * n, out_specs=[ANY] * n, out_shape=out_shape,
        scratch_shapes=[pltpu.SemaphoreType.DMA((3 * n,)), pltpu.SemaphoreType.DMA((3 * n,))],
    )(*parts)


def _scatter_partials_seq(parts, kinds, name, collective_id):
    n = len(parts)

    def body(*refs):
        ps = refs[:n]
        us = refs[n:2 * n]
        send_sems, recv_sems = refs[2 * n:]
        x, y, c = _xyc()
        others = _other_chips(x, y)
        barrier = pltpu.get_barrier_semaphore()
        for ox, oy in others:
            pl.semaphore_signal(barrier, inc=1, device_id=(ox, oy, c), device_id_type=MESH)
        pl.semaphore_wait(barrier, len(others))
        for w in range(n):
            for j, (ox, oy) in enumerate(others):
                pltpu.make_async_remote_copy(
                    src_ref=_shard_of(ps[w], kinds[w], 2 * ox + oy), dst_ref=us[w].at[j],
                    send_sem=send_sems.at[w], recv_sem=recv_sems.at[w],
                    device_id=(ox, oy, c), device_id_type=MESH).start()
        for w in range(n):
            pltpu.make_async_remote_copy(
                src_ref=us[w], dst_ref=us[w], send_sem=send_sems.at[w], recv_sem=recv_sems.at[w],
                device_id=(x, y, c), device_id_type=MESH).wait()

    out_type = []
    for p, kind in zip(parts, kinds):
        r, cn = p.shape
        hs = (r, cn // N_CHIPS) if kind == "col" else (r // N_CHIPS, cn)
        out_type.append(jax.ShapeDtypeStruct((N_CHIPS - 1,) + hs, p.dtype))
    return pl.kernel(
        body, out_type=out_type, mesh=plsc.ScalarSubcoreMesh(axis_name="seq", num_cores=1), name=name,
        scratch_types=[pltpu.SemaphoreType.DMA((n,)), pltpu.SemaphoreType.DMA((n,))],
        compiler_params=pltpu.CompilerParams(collective_id=collective_id),
    )(*parts)


def _join_halves(halves, kinds, name):
    n = len(halves)

    def body(*refs):
        outs = refs[n:2 * n]
        send_sems, recv_sems = refs[2 * n:]
        x, y, c = _xyc()
        cps = []
        for w in range(n):
            mine = _half_of(outs[w], kinds[w], c)
            cp = pltpu.make_async_remote_copy(
                src_ref=mine, dst_ref=mine, send_sem=send_sems.at[w], recv_sem=recv_sems.at[w],
                device_id=(x, y, 1 - c), device_id_type=MESH)
            cp.start()
            cps.append(cp)
        for w in range(n):
            theirs = _half_of(outs[w], kinds[w], 1 - c)
            pltpu.make_async_remote_copy(
                src_ref=theirs, dst_ref=theirs, send_sem=send_sems.at[w], recv_sem=recv_sems.at[w],
                device_id=(x, y, 1 - c), device_id_type=MESH).wait_recv()
        _drain(cps)

    return pl.pallas_call(
        body, name=name, in_specs=[ANY] * n, out_specs=[ANY] * n,
        out_shape=[jax.ShapeDtypeStruct(h.shape, h.dtype) for h in halves],
        input_output_aliases={w: w for w in range(n)},
        scratch_shapes=[pltpu.SemaphoreType.DMA((n,)), pltpu.SemaphoreType.DMA((n,))],
    )(*halves)


def kernel(x, a_norm, a_w_in, a_conv_w, a_conv_b, a_ln_g, a_ln_b, a_w_out, kv_norm, w_kv, b_norm, b_w_in, b_w_out, rel_bias, final_norm, loss_target, m_a_norm, m_a_w_in, m_a_conv_w, m_a_conv_b, m_a_ln_g, m_a_ln_b, m_a_w_out, m_kv_norm, m_w_kv, m_b_norm, m_b_w_in, m_b_w_out, m_rel_bias, m_final_norm, v_a_norm, v_a_w_in, v_a_conv_w, v_a_conv_b, v_a_ln_g, v_a_ln_b, v_a_w_out, v_kv_norm, v_w_kv, v_b_norm, v_b_w_in, v_b_w_out, v_rel_bias, v_final_norm):
    S, D = x.shape[1], x.shape[2]
    E = a_w_out.shape[1] * N_CHIPS
    A = b_w_out.shape[1] * N_CHIPS
    H = A // HEAD_DIM
    DC = D // N_CHIPS
    xs = x.reshape(S, D)
    tgt = loss_target.reshape(S, D)
    cx, cy, cc = _xyc()
    chip = 2 * cx + cy
    c_idx = jnp.reshape(cc, (1,)).astype(jnp.int32)

    big_names = ["a_w_in", "a_w_out", "w_kv", "b_w_in", "b_w_out"]
    kinds = ["col", "row", "col", "col", "row"]
    big_w = [a_w_in[0], a_w_out[0], w_kv, b_w_in[0], b_w_out[0]]
    big_m = [m_a_w_in[0], m_a_w_out[0], m_w_kv, m_b_w_in[0], m_b_w_out[0]]
    big_v = [v_a_w_in[0], v_a_w_out[0], v_w_kv, v_b_w_in[0], v_b_w_out[0]]
    chip_idx = jnp.reshape(chip, (1,)).astype(jnp.int32)
    placed = [_cast_into_full(big_w[w], kinds[w], chip_idx, "cast_" + big_names[w]) for w in range(5)]
    shard_shapes = [w.shape for w in big_w]
    (wa_in,) = _allgather_weights_seq(placed[0:1], kinds[0:1], shard_shapes[0:1], "ag_seq_a_in", 0)
    (wa_out,) = _allgather_weights_seq(placed[1:2], kinds[1:2], shard_shapes[1:2], "ag_seq_a_out", 1)
    (wkv,) = _allgather_weights_seq(placed[2:3], kinds[2:3], shard_shapes[2:3], "ag_seq_kv", 5)
    wb_in, wb_out = _allgather_weights_seq(placed[3:5], kinds[3:5], shard_shapes[3:5], "ag_seq_b", 2)

    def row_at(vec, q):
        return jnp.pad(vec, ((q, 7 - q), (0, 0)))

    def pack_sharded(an, cw, cb, lg, lb):
        return jnp.concatenate([row_at(an, 0), jnp.pad(cw[0], ((0, 1), (0, 0))),
                                row_at(lg, 0) + row_at(lb, 1) + row_at(cb, 2)], axis=0)

    small_w = pack_sharded(a_norm, a_conv_w, a_conv_b, a_ln_g, a_ln_b)
    gathered = _allgather_small(small_w, "ag_small_params")
    small_full = jnp.concatenate([gathered[2 * k] for k in range(N_CHIPS)], axis=1)
    g_a = small_full[0:1]
    conv_w32 = small_full[8:8 + HALO]
    ln_g = small_full[40:41]
    ln_b = small_full[41:42]
    conv_b = small_full[42:43]
    g_kv = kv_norm.reshape(1, D)
    g_b = b_norm.reshape(1, D)
    g_f = final_norm.reshape(1, D)

    rb_t = jnp.pad(rel_bias.T, ((0, 0), (0, LANES - N_BUCKETS)))
    onehots = [_onehot(dil) for _, dil in GROUPS]
    biases = [_bias_table(rb_t, onehots[g], "bias_table_%d" % g).reshape(H, BLOCK, 2 * BLOCK)
              for g in range(len(GROUPS))]

    dils = tuple(dil for _, dil in GROUPS)
    assert dils[0] == 1
    n_g = len(GROUPS)
    ((h0,),) = _rms_fwd(xs, [g_a], (1,), "rms_a")
    proj3 = _matmul(h0, wa_in, "nn", BF16, "mm_a_in", out_slab=E, tn=2048)
    conv = _conv_fwd(proj3, conv_w32, conv_b, "conv_fwd")
    y_a = _ln_gate_fwd(conv, proj3, ln_g, ln_b, "ln_gate_fwd")
    x1 = _matmul(y_a, wa_out, "nn", F32, "mm_a_out", res=xs)
    hks, hbs = _rms_fwd(x1, [g_kv, g_b], dils, "rms_kv_b")
    kvs = [_matmul(hks[g], wkv, "nn", BF16, "mm_kv_%d" % g, out_slab=A, b_off=2 * g * A, n_cols=2 * A,
                   tn=2048)
           for g in range(n_g)]
    qs = [_matmul(hbs[g], wb_in, "nn", BF16, "mm_q_%d" % g, b_off=g * A, n_cols=A, after=kvs)
          for g in range(n_g)]
    zb = _matmul(hbs[0], wb_in, "nn", BF16, "mm_zb", b_off=n_g * A, n_cols=A, after=kvs)
    os_, lses = [], []
    for g, dil in enumerate(dils):
        o_g, lse_g = _attn_fwd(qs[g], kvs[g], biases[g], dil, "attn_fwd_%d" % g)
        os_.append(o_g)
        lses.append(lse_g)
    y_b, o_m, lse_d = _attn_merge(os_, lses, zb, dils, "attn_merge")
    x2 = _matmul(y_b, wb_out, "nn", F32, "mm_b_out", res=x1)
    loss_part, dx2, dx2b, gg_f = _final_head(x2, g_f, tgt, "final_head")
    loss = lax.psum(loss_part[0, 0], ("x", "y", "c"))

    dw_tiles = dict(tm=1024, tn=1024, tk=4096)
    dy_b = _matmul(dx2b, wb_out, "nt", BF16, "mm_b_out_dx", after=loss.reshape(1, 1))
    dwb_out = _matmul(y_b, dx2b, "tn", BF16, "mm_b_out_dw", **dw_tiles)
    dos, dhs, dzb = _gate_bwd(dy_b, o_m, zb, dils, "gate_bwd")
    dbs, cots = [], []
    dwb_in = dwkv = None
    for g, dil in enumerate(dils):
        dq, dkv, db = _attn_bwd(qs[g], kvs[g], dos[g], lse_d[g], dhs[g], biases[g], dil, "attn_bwd_%d" % g)
        dbs.append(db.reshape(H, BLOCK * 2 * BLOCK))
        dwb_in = _matmul(hbs[g], dq, "tn", BF16, "mm_q_dw_%d" % g, out_off=g * A, out_cols=(n_g + 1) * A,
                         out_alias=dwb_in, **dw_tiles)
        dwkv = _matmul(hks[g], dkv, "tn", BF16, "mm_kv_dw_%d" % g, b_slab=True, out_off=2 * g * A,
                       out_cols=2 * n_g * A, out_alias=dwkv, **dw_tiles)
        cots.append((_matmul(dkv, wkv, "nt", BF16, "mm_kv_dx_%d" % g, a_slab=True, b_off=2 * g * A), 0, dil))
        cots.append((_matmul(dq, wb_in, "nt", BF16, "mm_q_dx_%d" % g, b_off=g * A), 1, dil))
    dwb_in = _matmul(hbs[0], dzb, "tn", BF16, "mm_zb_dw", out_off=n_g * A, out_cols=(n_g + 1) * A,
                     out_alias=dwb_in, **dw_tiles)
    cots.append((_matmul(dzb, wb_in, "nt", BF16, "mm_zb_dx", b_off=n_g * A), 1, 1))
    chip_c = jnp.stack([chip, cc]).astype(jnp.int32)

    def scatter_group(idx, grads, tag, collective_id, exchange_id=None, behind=None):
        ks = [kinds[w] for w in idx]
        if exchange_id is None:
            theirs = _exchange_halves(grads, ks, "rs_exchange_" + tag)
        else:
            theirs = _exchange_halves_seq(grads, ks, "rs_exchange_seq_" + tag, exchange_id)
        parts = [_add_half(grads[q], theirs[q], c_idx, ks[q], "rs_add_half_%d" % w, after=behind)
                 for q, w in enumerate(idx)]
        return parts, _scatter_partials_seq(parts, ks, "rs_seq_" + tag, collective_id)

    def reduce_group(idx, parts, slots, tag):
        ks = [kinds[w] for w in idx]
        halves = [_sum_into_shard(parts[q], slots[q], chip_c, ks[q], "rs_sum_chips_%d" % w)
                  for q, w in enumerate(idx)]
        return _join_halves(halves, ks, "rs_join_" + tag)

    parts_b, slots_b = scatter_group([2, 3, 4], [dwkv, dwb_in, dwb_out], "b", 3, exchange_id=6,
                                     behind=[ct[0] for ct in cots])
    g_rel_t = _bias_grad(dbs, onehots, "bias_grad")
    dx1, dx1b, gg_kvb = _rms_bwd(x1, cots, [g_kv, g_b], dx2, "rms_kv_b_bwd", after=parts_b)
    dy_a = _matmul(dx1b, wa_out, "nt", BF16, "mm_a_out_dx")
    dwa_out = _matmul(y_a, dx1b, "tn", BF16, "mm_a_out_dw", **dw_tiles)
    dconv, dproj3, gg_ln = _ln_gate_bwd(conv, proj3, dy_a, ln_g, ln_b, "ln_gate_bwd")
    dproj3, g_conv_w = _conv_bwd(proj3, dconv, conv_w32, dproj3, "conv_bwd")
    dwa_in = _matmul(h0, dproj3, "tn", BF16, "mm_a_in_dw", b_slab=True, **dw_tiles)
    parts_a, slots_a = scatter_group([0, 1], [dwa_in, dwa_out], "a", 4)
    dh0 = _matmul(dproj3, wa_in, "nt", BF16, "mm_a_in_dx", a_slab=True, after=parts_a, tn=512)
    grad_x, _, gg_a = _rms_bwd(xs, [(dh0, 0, 1)], [g_a], dx1, "rms_a_bwd")

    big_g = [None] * 5
    big_g[2:5] = reduce_group([2, 3, 4], parts_b, slots_b, "b")
    big_g[0:2] = reduce_group([0, 1], parts_a, slots_a, "a")

    def rel_rows(rb):
        return jnp.pad(rb.reshape(1, N_BUCKETS * H), ((0, 7), (0, D - N_BUCKETS * H)))

    small_g = jnp.concatenate([gg_a, g_conv_w, gg_ln, gg_kvb, gg_f, rel_rows(g_rel_t[:, :N_BUCKETS].T)], axis=0)
    small_sum = _sum_leading(_allgather_small(small_g, "ag_small_grads", after=[slots_a[0], slots_b[0]]), F32,
                             "sum_small_grads", tr=72)
    g_sharded = lax.dynamic_slice(small_sum, (0, chip * DC), (48, DC))
    g_repl = small_sum[48:72]

    outs_g, outs_d, outs_m, outs_v = {}, {}, {}, {}
    for w, nm in enumerate(big_names):
        d_, m_, v_ = _adamw(big_w[w], big_g[w], big_m[w], big_v[w], "adamw_" + nm)
        outs_g[nm], outs_d[nm], outs_m[nm], outs_v[nm] = big_g[w], d_, m_, v_
    sm_m = pack_sharded(m_a_norm, m_a_conv_w, m_a_conv_b, m_a_ln_g, m_a_ln_b)
    sm_v = pack_sharded(v_a_norm, v_a_conv_w, v_a_conv_b, v_a_ln_g, v_a_ln_b)
    sd, smm, svv = _adamw(small_w, g_sharded, sm_m, sm_v, "adamw_small_sharded")

    def unpack_sharded(p):
        return {"a_norm": p[0:1], "a_conv_w": p[8:8 + CONV_TAPS].reshape(1, CONV_TAPS, DC), "a_ln_g": p[40:41],
                "a_ln_b": p[41:42], "a_conv_b": p[42:43]}

    for src, dst in ((g_sharded, outs_g), (sd, outs_d), (smm, outs_m), (svv, outs_v)):
        dst.update(unpack_sharded(src))

    def pack_repl(kn, bn, fn, rb):
        return jnp.concatenate([row_at(kn.reshape(1, D), 0) + row_at(bn.reshape(1, D), 1),
                                row_at(fn.reshape(1, D), 0), rel_rows(rb)], axis=0)

    rp_w = pack_repl(kv_norm, b_norm, final_norm, rel_bias)
    rp_m = pack_repl(m_kv_norm, m_b_norm, m_final_norm, m_rel_bias)
    rp_v = pack_repl(v_kv_norm, v_b_norm, v_final_norm, v_rel_bias)
    rd, rmm, rvv = _adamw(rp_w, g_repl, rp_m, rp_v, "adamw_small_replicated")

    def unpack_repl(p):
        return {"kv_norm": p[0], "b_norm": p[1:2], "final_norm": p[8],
                "rel_bias": p[16, :N_BUCKETS * H].reshape(N_BUCKETS, H)}

    for src, dst in ((g_repl, outs_g), (rd, outs_d), (rmm, outs_m), (rvv, outs_v)):
        dst.update(unpack_repl(src))

    order = ["a_norm", "a_w_in", "a_conv_w", "a_conv_b", "a_ln_g", "a_ln_b", "a_w_out", "kv_norm", "w_kv",
             "b_norm", "b_w_in", "b_w_out", "rel_bias", "final_norm"]
    lead = {"a_w_in", "a_w_out", "b_w_in", "b_w_out"}

    def shaped(nm, val):
        return val[None] if nm in lead else val

    result = [loss, grad_x.reshape(1, S, D)]
    for table in (outs_g, outs_d, outs_m, outs_v):
        result.extend(shaped(nm, table[nm]) for nm in order)
    return tuple(result)
```

```python
import functools

import numpy as np
import jax
import jax.numpy as jnp
from jax import lax
from jax.experimental import pallas as pl
from jax.experimental.pallas import tpu as pltpu
from jax.experimental.pallas import tpu_sc as plsc

F32 = jnp.float32
BF16 = jnp.bfloat16
MESH = pl.DeviceIdType.MESH
ANY = pl.BlockSpec(memory_space=pl.ANY)
VMEM_SPEC = pl.BlockSpec(memory_space=pltpu.VMEM)

EPS = 1e-6
HEAD_DIM = 128
BLOCK = 128
GROUPS = ((128, 1), (512, 4), (2048, 16))
SCALE = HEAD_DIM ** -0.5
CONV_TAPS = 31
HALO = 32
N_BUCKETS = 32
MAX_EXACT = 16
MAX_DISTANCE = 2048
NEG = -1e30
PRODUCTS_AHEAD = 2
SCORES_AHEAD = 4
N_CHIPS = 4
N_DEV = 8
LANES = 128
VMEM_LIMIT = 56 * 1024 * 1024

ADAM_LR = 0.001
ADAM_B1 = 0.9
ADAM_B2 = 0.999
ADAM_EPS = 1e-08
ADAM_WD = 0.01
ADAM_STEP = 10


def _tile(n, pref, mult=LANES):
    t = (min(pref, n) // mult) * mult
    while t >= mult:
        if n % t == 0:
            return t
        t -= mult
    return n


def _params(*sem):
    return pltpu.CompilerParams(dimension_semantics=sem, vmem_limit_bytes=VMEM_LIMIT)


def _sigmoid(v):
    return 1.0 / (1.0 + jnp.exp(-v))


def _dot(a, b, dims):
    return lax.dot_general(a, b, (dims, ((), ())), preferred_element_type=F32)


NN = ((1,), (0,))
NT = ((1,), (1,))
TN = ((0,), (0,))


def _as_list(after):
    if after is None:
        return []
    return list(after) if isinstance(after, (list, tuple)) else [after]


def _stack_rows(rows, total):
    width = rows[0].shape[1]
    rid = lax.broadcasted_iota(jnp.int32, (total, width), 0)
    out = jnp.zeros((total, width), F32)
    for q, row in enumerate(rows):
        out = jnp.where(rid == q, jnp.broadcast_to(row, (total, width)), out)
    return out


def _lane_col(arr, h, lane):
    return jnp.sum(jnp.where(lane == h, arr, 0.0), axis=-1, keepdims=True)


def _matmul(a, b, mode, out_dtype, name, res=None, a_slab=False, b_slab=False, out_slab=0,
            b_off=0, n_cols=None, out_off=0, out_cols=None, out_alias=None, after=None,
            tm=1024, tn=1024, tk=2048):
    if a_slab:
        na, M, W = a.shape
        K = na * W
    elif mode == "tn":
        K, M = a.shape
    else:
        M, K = a.shape
    if b_slab:
        nbs, _, Wb = b.shape
        N = nbs * Wb
    elif mode == "nt":
        N = b.shape[0]
    else:
        N = n_cols if n_cols else b.shape[1]
    tm = _tile(M, tm)
    tn = _tile(Wb if b_slab else (out_slab if out_slab else N), tn)
    tk = _tile(W if a_slab else K, tk)
    all_slabs = a_slab and mode == "nt" and tk == W
    if all_slabs:
        tk = K
    nk = K // tk
    grid = (M // tm, N // tn, nk)
    bo = b_off // (tk if mode == "nt" else tn)
    oo = out_off // tn

    if all_slabs:
        a_spec = pl.BlockSpec((na, tm, W), lambda i, j, k: (0, i, 0))
    elif a_slab:
        per = W // tk
        a_spec = pl.BlockSpec((None, tm, tk), lambda i, j, k: (k // per, i, k % per))
    elif mode == "tn":
        a_spec = pl.BlockSpec((tk, tm), lambda i, j, k: (k, i))
    else:
        a_spec = pl.BlockSpec((tm, tk), lambda i, j, k: (i, k))
    if b_slab:
        perb = Wb // tn
        b_spec = pl.BlockSpec((None, tk, tn), lambda i, j, k: (j // perb, k, j % perb))
    elif mode == "nt":
        b_spec = pl.BlockSpec((tn, tk), lambda i, j, k: (j, k + bo))
    else:
        b_spec = pl.BlockSpec((tk, tn), lambda i, j, k: (k, j + bo))
    if out_slab:
        pero = out_slab // tn
        o_spec = pl.BlockSpec((None, tm, tn), lambda i, j, k: (j // pero, i, j % pero))
        out_shape = jax.ShapeDtypeStruct((N // out_slab, M, out_slab), out_dtype)
    else:
        o_spec = pl.BlockSpec((tm, tn), lambda i, j, k: (i, j + oo))
        out_shape = jax.ShapeDtypeStruct((M, out_cols if out_cols else N), out_dtype)
    in_specs = [a_spec, b_spec]
    operands = [a, b]
    if res is not None:
        in_specs.append(pl.BlockSpec((tm, tn), lambda i, j, k: (i, j)))
        operands.append(res)
    aliases = {}
    if out_alias is not None:
        aliases[len(operands)] = 0
        in_specs.append(ANY)
        operands.append(out_alias)
    for arr in _as_list(after):
        in_specs.append(ANY)
        operands.append(arr)
    dims = {"nn": NN, "nt": NT, "tn": TN}[mode]
    has_res = res is not None
    n_in = len(operands)

    def body(*refs):
        a_ref, b_ref = refs[0], refs[1]
        r_ref = refs[2] if has_res else None
        o_ref = refs[n_in]
        if all_slabs:
            prod = _dot(a_ref[0], b_ref[:, 0:W], dims)
            for q in range(1, na):
                prod = prod + _dot(a_ref[q], b_ref[:, q * W:(q + 1) * W], dims)
        else:
            prod = _dot(a_ref[...], b_ref[...], dims)

        def finish(val):
            if has_res:
                val = val + r_ref[...]
            o_ref[...] = val.astype(out_dtype)

        if nk == 1:
            finish(prod)
        else:
            acc_ref = refs[n_in + 1]
            k = pl.program_id(2)

            @pl.when(k == 0)
            def _():
                acc_ref[...] = prod

            @pl.when(k > 0)
            def _():
                acc_ref[...] += prod

            @pl.when(k == nk - 1)
            def _():
                finish(acc_ref[...])

    scratch = [pltpu.VMEM((tm, tn), F32)] if nk > 1 else []
    return pl.pallas_call(
        body, name=name, grid=grid, in_specs=in_specs, out_specs=o_spec, out_shape=out_shape,
        scratch_shapes=scratch, input_output_aliases=aliases,
        compiler_params=_params("parallel", "parallel", "arbitrary"),
    )(*operands)


def _group_spec(d, ts, width):
    if d == 1:
        return pl.BlockSpec((ts, width), lambda i: (i, 0))
    return pl.BlockSpec((d, ts // d, width), lambda i: (0, i, 0))


def _group_shape(d, S, width, dtype):
    return jax.ShapeDtypeStruct((S, width) if d == 1 else (d, S // d, width), dtype)


def _chunk_buf(ts, width):
    return pltpu.VMEM((width // LANES, ts, LANES), F32)


def _fill_chunks(buf, val):
    for c in range(buf.shape[0]):
        buf[c] = val[:, c * LANES:(c + 1) * LANES]


def _read_chunks(buf):
    return jnp.concatenate([buf[c] for c in range(buf.shape[0])], axis=1)


def _emit_group_order(o_ref, buf, d, dtype):
    n = buf.shape[1] // d
    for r in range(d):
        for c in range(buf.shape[0]):
            o_ref[r, :, c * LANES:(c + 1) * LANES] = buf[c, pl.ds(r, n, stride=d), :].astype(dtype)


def _store_token_order(buf, i_ref, d):
    n = buf.shape[1] // d
    for r in range(d):
        for c in range(buf.shape[0]):
            buf[c, pl.ds(r, n, stride=d), :] = i_ref[r, :, c * LANES:(c + 1) * LANES].astype(F32)


def _rms_fwd(x, gains, dils, name, ts=256):
    S, D = x.shape
    ts = _tile(S, ts, 16 * max(dils))
    n = len(gains)
    nd = len(dils)

    def body(*refs):
        buf = refs[1 + n + n * nd]
        xv = refs[0][...]
        nrm = xv * lax.rsqrt(jnp.mean(xv * xv, axis=-1, keepdims=True) + EPS)
        for q in range(n):
            val = nrm * refs[1 + q][...]
            if max(dils) > 1:
                _fill_chunks(buf, val)
            for e, d in enumerate(dils):
                if d == 1:
                    refs[1 + n + q * nd + e][...] = val.astype(BF16)
                else:
                    _emit_group_order(refs[1 + n + q * nd + e], buf, d, BF16)

    row = pl.BlockSpec((ts, D), lambda i: (i, 0))
    vec = pl.BlockSpec((1, D), lambda i: (0, 0))
    outs = pl.pallas_call(
        body, name=name, grid=(S // ts,), in_specs=[row] + [vec] * n,
        out_specs=[_group_spec(d, ts, D) for _ in range(n) for d in dils],
        out_shape=[_group_shape(d, S, D, BF16) for _ in range(n) for d in dils],
        scratch_shapes=[_chunk_buf(ts, D)],
        compiler_params=_params("parallel"),
    )(x, *gains)
    return [[outs[q * nd + e].reshape(S, D) for e in range(nd)] for q in range(n)]


def _rms_bwd(x, cots, gains, dres, name, after=None, ts=256):
    S, D = x.shape
    ts = _tile(S, ts, 16 * max(d for _, _, d in cots))
    n = len(cots)
    ng = len(gains)
    extra = _as_list(after)
    n_in = 2 + n + ng + len(extra)

    def body(*refs):
        x_ref = refs[0]
        dh_refs = refs[1:1 + n]
        g_refs = refs[1 + n:1 + n + ng]
        dres_ref = refs[1 + n + ng]
        dx_ref, dxb_ref, gg_ref, buf = refs[n_in:n_in + 4]
        i = pl.program_id(0)
        xv = x_ref[...]
        r = lax.rsqrt(jnp.mean(xv * xv, axis=-1, keepdims=True) + EPS)
        nrm = xv * r
        dn = jnp.zeros_like(xv)
        rows = [jnp.zeros((1, D), F32) for _ in range(ng)]
        for q, (_, gi, d) in enumerate(cots):
            if d == 1:
                dh = dh_refs[q][...].astype(F32)
            else:
                _store_token_order(buf, dh_refs[q], d)
                dh = _read_chunks(buf)
            dn = dn + dh * g_refs[gi][...]
            rows[gi] = rows[gi] + jnp.sum(dh * nrm, axis=0, keepdims=True)
        dx = dres_ref[...] + r * (dn - nrm * jnp.mean(dn * nrm, axis=-1, keepdims=True))
        dx_ref[...] = dx
        dxb_ref[...] = dx.astype(BF16)
        upd = _stack_rows(rows, 8)

        @pl.when(i == 0)
        def _():
            gg_ref[...] = upd

        @pl.when(i > 0)
        def _():
            gg_ref[...] += upd

    row = pl.BlockSpec((ts, D), lambda i: (i, 0))
    vec = pl.BlockSpec((1, D), lambda i: (0, 0))
    acc = pl.BlockSpec((8, D), lambda i: (0, 0))
    return pl.pallas_call(
        body, name=name, grid=(S // ts,),
        in_specs=[row] + [_group_spec(d, ts, D) for _, _, d in cots] + [vec] * ng + [row] + [ANY] * len(extra),
        out_specs=[row, row, acc],
        out_shape=[jax.ShapeDtypeStruct((S, D), F32), jax.ShapeDtypeStruct((S, D), BF16),
                   jax.ShapeDtypeStruct((8, D), F32)],
        scratch_shapes=[_chunk_buf(ts, D)],
        compiler_params=_params("arbitrary"),
    )(x, *[a if d == 1 else a.reshape(d, S // d, D) for a, _, d in cots], *gains, dres, *extra)


def _final_head(x2, gain, target, name, ts=256):
    S, D = x2.shape
    ts = _tile(S, ts, 16)

    def body(x_ref, g_ref, t_ref, loss_ref, dx_ref, dxb_ref, gg_ref):
        i = pl.program_id(0)
        xv = x_ref[...]
        g = g_ref[...]
        r = lax.rsqrt(jnp.mean(xv * xv, axis=-1, keepdims=True) + EPS)
        nrm = xv * r
        err = nrm * g - t_ref[...]
        part = 0.5 * jnp.sum(jnp.mean(err * err, axis=-1, keepdims=True), axis=0, keepdims=True)
        dout = err * (1.0 / D)
        dn = dout * g
        dx = r * (dn - nrm * jnp.mean(dn * nrm, axis=-1, keepdims=True))
        dx_ref[...] = dx
        dxb_ref[...] = dx.astype(BF16)
        upd = _stack_rows([jnp.sum(dout * nrm, axis=0, keepdims=True)], 8)
        lpart = jnp.broadcast_to(part, (1, LANES))

        @pl.when(i == 0)
        def _():
            gg_ref[...] = upd
            loss_ref[...] = lpart

        @pl.when(i > 0)
        def _():
            gg_ref[...] += upd
            loss_ref[...] += lpart

    row = pl.BlockSpec((ts, D), lambda i: (i, 0))
    vec = pl.BlockSpec((1, D), lambda i: (0, 0))
    return pl.pallas_call(
        body, name=name, grid=(S // ts,), in_specs=[row, vec, row],
        out_specs=[pl.BlockSpec((1, LANES), lambda i: (0, 0)), row, row, pl.BlockSpec((8, D), lambda i: (0, 0))],
        out_shape=[jax.ShapeDtypeStruct((1, LANES), F32), jax.ShapeDtypeStruct((S, D), F32),
                   jax.ShapeDtypeStruct((S, D), BF16), jax.ShapeDtypeStruct((8, D), F32)],
        compiler_params=_params("arbitrary"),
    )(x2, gain, target)


CONV_ROWS = 64


SUBLANES = 8


def _shifted_buf(ts, cw):
    return pltpu.VMEM((SUBLANES - 1, ts + HALO - SUBLANES, cw), F32)


def _fill_shifted(shifted, buf):
    rows = shifted.shape[1]
    for s in range(1, SUBLANES):
        shifted[s - 1] = buf[s:s + rows, :]


def _window(buf, shifted, off, rows):
    s = off % SUBLANES
    base = off - s
    if s == 0:
        return buf[base:base + rows, :]
    return shifted[s - 1, base:base + rows, :]


def _conv_fwd(proj3, conv_w32, conv_b, name, ts=256, cw=256):
    _, S, E = proj3.shape
    ts = _tile(S, ts, HALO)
    cw = _tile(E, cw)
    per = ts // HALO
    rc = min(CONV_ROWS, ts)

    def body(a_ref, b_ref, ap_ref, bp_ref, w_ref, cb_ref, c_ref, ubuf, shifted):
        i = pl.program_id(0)
        up = ap_ref[...].astype(F32) * _sigmoid(bp_ref[...].astype(F32))
        ubuf[0:HALO, :] = jnp.where(i > 0, up, 0.0)
        ubuf[HALO:HALO + ts, :] = a_ref[...].astype(F32) * _sigmoid(b_ref[...].astype(F32))
        _fill_shifted(shifted, ubuf)
        for r0 in range(0, ts, rc):
            acc = jnp.broadcast_to(cb_ref[...], (rc, cw))
            for k in range(CONV_TAPS):
                off = r0 + HALO - (CONV_TAPS - 1) + k
                acc = acc + _window(ubuf, shifted, off, rc) * w_ref[k:k + 1, :]
            c_ref[r0:r0 + rc, :] = acc

    return pl.pallas_call(
        body, name=name, grid=(S // ts, E // cw),
        in_specs=[
            pl.BlockSpec((None, ts, cw), lambda i, j: (0, i, j)),
            pl.BlockSpec((None, ts, cw), lambda i, j: (1, i, j)),
            pl.BlockSpec((None, HALO, cw), lambda i, j: (0, jnp.maximum(i * per - 1, 0), j)),
            pl.BlockSpec((None, HALO, cw), lambda i, j: (1, jnp.maximum(i * per - 1, 0), j)),
            pl.BlockSpec((HALO, cw), lambda i, j: (0, j)),
            pl.BlockSpec((1, cw), lambda i, j: (0, j)),
        ],
        out_specs=pl.BlockSpec((ts, cw), lambda i, j: (i, j)),
        out_shape=jax.ShapeDtypeStruct((S, E), F32),
        scratch_shapes=[pltpu.VMEM((HALO + ts, cw), F32), _shifted_buf(ts, cw)],
        compiler_params=_params("parallel", "parallel"),
    )(proj3, proj3, proj3, proj3, conv_w32, conv_b)


def _ln_gate_fwd(c, proj3, ln_g, ln_b, name, ts=256):
    S, E = c.shape
    ts = _tile(S, ts, 16)

    def body(c_ref, z_ref, g_ref, b_ref, y_ref):
        cv = c_ref[...]
        mu = jnp.mean(cv, axis=-1, keepdims=True)
        d = cv - mu
        var = jnp.mean(d * d, axis=-1, keepdims=True)
        cn = d * lax.rsqrt(var + EPS) * g_ref[...] + b_ref[...]
        z = z_ref[...].astype(F32)
        y_ref[...] = ((cn * _sigmoid(cn)).astype(F32) * (z * _sigmoid(z))).astype(BF16)

    row = pl.BlockSpec((ts, E), lambda i: (i, 0))
    vec = pl.BlockSpec((1, E), lambda i: (0, 0))
    return pl.pallas_call(
        body, name=name, grid=(S // ts,),
        in_specs=[row, pl.BlockSpec((None, ts, E), lambda i: (2, i, 0)), vec, vec],
        out_specs=row, out_shape=jax.ShapeDtypeStruct((S, E), BF16),
        compiler_params=_params("parallel"),
    )(c, proj3, ln_g, ln_b)


def _ln_gate_bwd(c, proj3, dy, ln_g, ln_b, name, ts=256):
    S, E = c.shape
    ts = _tile(S, ts, 16)

    def body(c_ref, z_ref, dy_ref, g_ref, b_ref, dc_ref, dz_ref, acc_ref):
        i = pl.program_id(0)
        cv = c_ref[...]
        g = g_ref[...]
        mu = jnp.mean(cv, axis=-1, keepdims=True)
        d = cv - mu
        var = jnp.mean(d * d, axis=-1, keepdims=True)
        rstd = lax.rsqrt(var + EPS)
        chat = d * rstd
        cn = chat * g + b_ref[...]
        z = z_ref[...].astype(F32)
        dyv = dy_ref[...].astype(F32)
        sc = _sigmoid(cn)
        sz = _sigmoid(z)
        dcn = dyv * (z * sz) * (sc * (1.0 + cn * (1.0 - sc)))
        dz_ref[...] = (dyv * (cn * sc) * (sz * (1.0 + z * (1.0 - sz)))).astype(BF16)
        dchat = dcn * g
        dcv = rstd * (dchat - jnp.mean(dchat, axis=-1, keepdims=True)
                      - chat * jnp.mean(dchat * chat, axis=-1, keepdims=True))
        dc_ref[...] = dcv
        upd = _stack_rows([jnp.sum(dcn * chat, axis=0, keepdims=True),
                           jnp.sum(dcn, axis=0, keepdims=True),
                           jnp.sum(dcv, axis=0, keepdims=True)], 8)

        @pl.when(i == 0)
        def _():
            acc_ref[...] = upd

        @pl.when(i > 0)
        def _():
            acc_ref[...] += upd

    row = pl.BlockSpec((ts, E), lambda i: (i, 0))
    vec = pl.BlockSpec((1, E), lambda i: (0, 0))
    return pl.pallas_call(
        body, name=name, grid=(S // ts,),
        in_specs=[row, pl.BlockSpec((None, ts, E), lambda i: (2, i, 0)), row, vec, vec],
        out_specs=[row, pl.BlockSpec((None, ts, E), lambda i: (2, i, 0)), pl.BlockSpec((8, E), lambda i: (0, 0))],
        out_shape=[jax.ShapeDtypeStruct((S, E), F32), jax.ShapeDtypeStruct((3, S, E), BF16),
                   jax.ShapeDtypeStruct((8, E), F32)],
        compiler_params=_params("arbitrary"),
    )(c, proj3, dy, ln_g, ln_b)


def _conv_bwd(proj3, dc, conv_w32, dproj3, name, ts=256, cw=256):
    _, S, E = proj3.shape
    ts = _tile(S, ts, HALO)
    cw = _tile(E, cw)
    per = ts // HALO
    n_i = S // ts
    last_halo = S // HALO - 1
    rc = min(CONV_ROWS, ts)

    def body(a_ref, b_ref, dc_ref, dcn_ref, w_ref, dp_in, dab_ref, dw_ref, dcbuf, ubuf, dwacc, shifted):
        del dp_in
        i = pl.program_id(1)
        dcbuf[0:ts, :] = dc_ref[...]
        dcbuf[ts:ts + HALO, :] = jnp.where(i < n_i - 1, dcn_ref[...], 0.0)
        _fill_shifted(shifted, dcbuf)
        av = a_ref[...].astype(F32)
        sb = _sigmoid(b_ref[...].astype(F32))
        ubuf[...] = av * sb

        @pl.when(i == 0)
        def _():
            dwacc[...] = jnp.zeros_like(dwacc)

        for r0 in range(0, ts, rc):
            uv = ubuf[r0:r0 + rc, :]
            du = jnp.zeros((rc, cw), F32)
            for d in range(CONV_TAPS):
                k = CONV_TAPS - 1 - d
                win = _window(dcbuf, shifted, r0 + d, rc)
                du = du + win * w_ref[k:k + 1, :]
                dwacc[k:k + 1, :] += jnp.sum(uv * win, axis=0, keepdims=True)
            a_c = a_ref[r0:r0 + rc, :].astype(F32)
            s_c = _sigmoid(b_ref[r0:r0 + rc, :].astype(F32))
            dab_ref[0, r0:r0 + rc, :] = (du * s_c).astype(BF16)
            dab_ref[1, r0:r0 + rc, :] = (du * a_c * s_c * (1.0 - s_c)).astype(BF16)

        @pl.when(i == n_i - 1)
        def _():
            dw_ref[...] = dwacc[...]

    return pl.pallas_call(
        body, name=name, grid=(E // cw, n_i),
        in_specs=[
            pl.BlockSpec((None, ts, cw), lambda j, i: (0, i, j)),
            pl.BlockSpec((None, ts, cw), lambda j, i: (1, i, j)),
            pl.BlockSpec((ts, cw), lambda j, i: (i, j)),
            pl.BlockSpec((HALO, cw), lambda j, i: (jnp.minimum((i + 1) * per, last_halo), j)),
            pl.BlockSpec((HALO, cw), lambda j, i: (0, j)),
            ANY,
        ],
        out_specs=[pl.BlockSpec((2, ts, cw), lambda j, i: (0, i, j)),
                   pl.BlockSpec((HALO, cw), lambda j, i: (0, j))],
        out_shape=[jax.ShapeDtypeStruct((3, S, E), BF16), jax.ShapeDtypeStruct((HALO, E), F32)],
        scratch_shapes=[pltpu.VMEM((ts + HALO, cw), F32), pltpu.VMEM((ts, cw), F32), pltpu.VMEM((HALO, cw), F32),
                        _shifted_buf(ts, cw)],
        input_output_aliases={5: 0},
        compiler_params=_params("parallel", "arbitrary"),
    )(proj3, proj3, dc, dc, conv_w32, dproj3)


def _bucket_table(dil):
    delta = (np.arange(BLOCK)[:, None] + BLOCK) - np.arange(2 * BLOCK)[None, :]
    dist = np.clip(delta, 0, None) * dil
    large = MAX_EXACT + (np.log(np.maximum(dist, 1).astype(np.float32) / MAX_EXACT)
                         / np.log(MAX_DISTANCE / MAX_EXACT) * (N_BUCKETS - MAX_EXACT)).astype(np.int32)
    large = np.minimum(large, N_BUCKETS - 1)
    return np.where(dist < MAX_EXACT, dist, large).astype(np.int32).reshape(-1)


def _onehot(dil):
    tbl = jnp.asarray(_bucket_table(dil))
    return (tbl[None, :] == jnp.arange(LANES, dtype=jnp.int32)[:, None]).astype(BF16)


def _split3(v):
    hi = v.astype(BF16)
    r1 = v - hi.astype(F32)
    mid = r1.astype(BF16)
    lo = (r1 - mid.astype(F32)).astype(BF16)
    return hi, mid, lo


def _bias_table(rb_t, onehot, name):
    H = rb_t.shape[0]
    N = onehot.shape[1]

    def body(r_ref, oh_ref, o_ref):
        oh = oh_ref[...]
        hi, mid, lo = _split3(r_ref[...])
        o_ref[...] = (_dot(lo, oh, NN) + _dot(mid, oh, NN)) + _dot(hi, oh, NN)

    return pl.pallas_call(
        body, name=name, in_specs=[VMEM_SPEC, VMEM_SPEC], out_specs=VMEM_SPEC,
        out_shape=jax.ShapeDtypeStruct((H, N), F32),
        compiler_params=pltpu.CompilerParams(vmem_limit_bytes=VMEM_LIMIT),
    )(rb_t, onehot)


def _bias_grad(dbs, onehots, name):
    H = dbs[0].shape[0]
    n = len(dbs)

    def body(*refs):
        acc = jnp.zeros((H, LANES), F32)
        for q in range(n):
            oh = refs[n + q][...]
            hi, mid, lo = _split3(refs[q][...])
            acc = acc + ((_dot(lo, oh, NT) + _dot(mid, oh, NT)) + _dot(hi, oh, NT))
        refs[2 * n][...] = acc

    return pl.pallas_call(
        body, name=name, in_specs=[VMEM_SPEC] * (2 * n), out_specs=VMEM_SPEC,
        out_shape=jax.ShapeDtypeStruct((H, LANES), F32),
        compiler_params=pltpu.CompilerParams(vmem_limit_bytes=VMEM_LIMIT),
    )(*dbs, *onehots)


def _pair_mask(has_prev):
    qi = lax.broadcasted_iota(jnp.int32, (BLOCK, 2 * BLOCK), 0)
    ki = lax.broadcasted_iota(jnp.int32, (BLOCK, 2 * BLOCK), 1)
    prev = jnp.logical_and(jnp.logical_and(ki < BLOCK, ki >= qi), has_prev)
    return jnp.logical_or(prev, jnp.logical_and(ki >= BLOCK, ki - BLOCK <= qi))


def _attn_fwd(q, kv, bias, dil, name):
    S, A = q.shape
    H = A // HEAD_DIM
    L = S // dil
    nb = L // BLOCK
    qv = q.reshape(dil, L, A)
    kvv = kv.reshape(2, dil, L, A)

    def body(q_ref, kp_ref, kc_ref, vp_ref, vc_ref, b_ref, o_ref, lse_ref):
        i = pl.program_id(1)
        qi = lax.broadcasted_iota(jnp.int32, (BLOCK, BLOCK), 0)
        ki = lax.broadcasted_iota(jnp.int32, (BLOCK, BLOCK), 1)
        mask = _pair_mask(i > 0)
        lane = lax.broadcasted_iota(jnp.int32, (BLOCK, LANES), 1)
        lse_acc = jnp.zeros((BLOCK, LANES), F32)

        def scores(h):
            sl = slice(h * HEAD_DIM, (h + 1) * HEAD_DIM)
            return _dot(q_ref[:, sl], jnp.concatenate([kp_ref[:, sl], kc_ref[:, sl]], axis=0), NT)

        ahead = [scores(h) for h in range(min(SCORES_AHEAD, H))]
        for h in range(H):
            sl = slice(h * HEAD_DIM, (h + 1) * HEAD_DIM)
            raw = ahead.pop(0)
            if h + SCORES_AHEAD < H:
                ahead.append(scores(h + SCORES_AHEAD))
            s = jnp.where(mask, raw * SCALE + b_ref[h], NEG)
            m = jnp.max(s, axis=-1, keepdims=True)
            p = jnp.exp(s - m)
            den = jnp.sum(p, axis=-1, keepdims=True)
            acc = _dot(p.astype(BF16), jnp.concatenate([vp_ref[:, sl], vc_ref[:, sl]], axis=0), NN)
            o_ref[:, sl] = (acc / den).astype(BF16)
            lse_acc = jnp.where(lane == h, m + jnp.log(den), lse_acc)
        lse_ref[...] = lse_acc

    def blk(slab, prev):
        if prev:
            return pl.BlockSpec((None, None, BLOCK, A), lambda r, i: (slab, r, jnp.maximum(i - 1, 0), 0))
        return pl.BlockSpec((None, None, BLOCK, A), lambda r, i: (slab, r, i, 0))

    o, lse = pl.pallas_call(
        body, name=name, grid=(dil, nb),
        in_specs=[pl.BlockSpec((None, BLOCK, A), lambda r, i: (r, i, 0)),
                  blk(0, True), blk(0, False), blk(1, True), blk(1, False),
                  pl.BlockSpec((H, BLOCK, 2 * BLOCK), lambda r, i: (0, 0, 0))],
        out_specs=[pl.BlockSpec((None, BLOCK, A), lambda r, i: (r, i, 0)),
                   pl.BlockSpec((None, BLOCK, LANES), lambda r, i: (r, i, 0))],
        out_shape=[jax.ShapeDtypeStruct((dil, L, A), BF16), jax.ShapeDtypeStruct((dil, L, LANES), F32)],
        compiler_params=_params("parallel", "parallel"),
    )(qv, kvv, kvv, kvv, kvv, bias)
    return o.reshape(S, A), lse.reshape(S, LANES)


def _attn_merge(os_, lses, z, dils, name, ts=256):
    S, A = z.shape
    H = A // HEAD_DIM
    ts = _tile(S, ts, 16 * max(dils))
    n = len(os_)

    def body(*refs):
        z_ref = refs[2 * n]
        y_ref, om_ref = refs[2 * n + 1:2 * n + 3]
        lse_refs = refs[2 * n + 3:3 * n + 3]
        o_refs = refs[3 * n + 3:4 * n + 3]
        l_bufs = refs[4 * n + 3:5 * n + 3]
        lse_buf = refs[5 * n + 3]
        ls = []
        for q, d in enumerate(dils):
            if d == 1:
                ls.append(refs[n + q][...])
            else:
                _store_token_order(o_refs[q], refs[q], d)
                _store_token_order(l_bufs[q], refs[n + q], d)
                ls.append(l_bufs[q][0])
        m = ls[0]
        for q in range(1, n):
            m = jnp.maximum(m, ls[q])
        es = [jnp.exp(v - m) for v in ls]
        den = es[0]
        for q in range(1, n):
            den = den + es[q]
        alphas = [e / den for e in es]
        lse = m + jnp.log(den)
        lse_buf[0] = lse
        for q, d in enumerate(dils):
            if d == 1:
                lse_refs[q][...] = lse
            else:
                _emit_group_order(lse_refs[q], lse_buf, d, F32)
        lane = lax.broadcasted_iota(jnp.int32, (ts, LANES), 1)
        for h in range(H):
            sl = slice(h * HEAD_DIM, (h + 1) * HEAD_DIM)
            om = jnp.zeros((ts, HEAD_DIM), F32)
            for q, d in enumerate(dils):
                o_h = refs[q][:, sl] if d == 1 else o_refs[q][h]
                om = om + _lane_col(alphas[q], h, lane) * o_h
            z = z_ref[:, sl].astype(F32)
            y_ref[:, sl] = (om * (z * _sigmoid(z))).astype(BF16)
            om_ref[:, sl] = om.astype(BF16)

    row = pl.BlockSpec((ts, A), lambda i: (i, 0))
    outs = pl.pallas_call(
        body, name=name, grid=(S // ts,),
        in_specs=[_group_spec(d, ts, A) for d in dils] + [_group_spec(d, ts, LANES) for d in dils] + [row],
        out_specs=[row, row] + [_group_spec(d, ts, LANES) for d in dils],
        out_shape=[jax.ShapeDtypeStruct((S, A), BF16), jax.ShapeDtypeStruct((S, A), BF16)]
        + [_group_shape(d, S, LANES, F32) for d in dils],
        scratch_shapes=[_chunk_buf(ts, A)] * n + [_chunk_buf(ts, LANES)] * (n + 1),
        compiler_params=_params("parallel"),
    )(*[o if d == 1 else o.reshape(d, S // d, A) for o, d in zip(os_, dils)],
      *[v if d == 1 else v.reshape(d, S // d, LANES) for v, d in zip(lses, dils)], z)
    return outs[0], outs[1], [v.reshape(S, LANES) for v in outs[2:]]


def _gate_bwd(dy, om, z, dils, name, ts=256):
    S, A = dy.shape
    H = A // HEAD_DIM
    ts = _tile(S, ts, 16 * max(dils))
    n = len(dils)

    def body(*refs):
        dy_ref, om_ref, z_ref = refs[:3]
        do_refs = refs[3:3 + n]
        dh_refs = refs[3 + n:3 + 2 * n]
        dz_ref = refs[3 + 2 * n]
        do_buf, dh_buf = refs[4 + 2 * n:6 + 2 * n]
        lane = lax.broadcasted_iota(jnp.int32, (ts, LANES), 1)
        acc = jnp.zeros((ts, LANES), F32)
        for h in range(H):
            sl = slice(h * HEAD_DIM, (h + 1) * HEAD_DIM)
            dyv = dy_ref[:, sl].astype(F32)
            omv = om_ref[:, sl].astype(F32)
            zv = z_ref[:, sl].astype(F32)
            sz = _sigmoid(zv)
            dob = (dyv * (zv * sz)).astype(BF16)
            do_buf[h] = dob.astype(F32)
            for q, d in enumerate(dils):
                if d == 1:
                    do_refs[q][:, sl] = dob
            dz_ref[:, sl] = (dyv * omv * (sz * (1.0 + zv * (1.0 - sz)))).astype(BF16)
            acc = jnp.where(lane == h, jnp.sum(dob.astype(F32) * omv, axis=-1, keepdims=True), acc)
        dh_buf[0] = acc
        for q, d in enumerate(dils):
            if d == 1:
                dh_refs[q][...] = acc
            else:
                _emit_group_order(do_refs[q], do_buf, d, BF16)
                _emit_group_order(dh_refs[q], dh_buf, d, F32)

    row = pl.BlockSpec((ts, A), lambda i: (i, 0))
    outs = pl.pallas_call(
        body, name=name, grid=(S // ts,), in_specs=[row, row, row],
        out_specs=[_group_spec(d, ts, A) for d in dils] + [_group_spec(d, ts, LANES) for d in dils] + [row],
        out_shape=[_group_shape(d, S, A, BF16) for d in dils] + [_group_shape(d, S, LANES, F32) for d in dils]
        + [jax.ShapeDtypeStruct((S, A), BF16)],
        scratch_shapes=[_chunk_buf(ts, A), _chunk_buf(ts, LANES)],
        compiler_params=_params("parallel"),
    )(dy, om, z)
    return ([v.reshape(S, A) for v in outs[:n]], [v.reshape(S, LANES) for v in outs[n:2 * n]], outs[2 * n])


def _attn_bwd(q, kv, do, lse, dh, bias, dil, name):
    S, A = q.shape
    H = A // HEAD_DIM
    L = S // dil
    nb = L // BLOCK
    qv = q.reshape(dil, L, A)
    kvv = kv.reshape(2, dil, L, A)
    dov = do.reshape(dil, L, A)
    lsev = lse.reshape(dil, L, LANES)
    dhv = dh.reshape(dil, L, LANES)

    def body(*refs):
        (q_ref, qn_ref, kp_ref, kc_ref, vp_ref, vc_ref, do_ref, don_ref, l_ref, ln_ref, d_ref, dn_ref,
         b_ref) = refs[:13]
        dq_ref, dkv_ref, db_ref = refs[13:16]
        r = pl.program_id(0)
        i = pl.program_id(1)
        qi = lax.broadcasted_iota(jnp.int32, (BLOCK, BLOCK), 0)
        ki = lax.broadcasted_iota(jnp.int32, (BLOCK, BLOCK), 1)
        mask_c = ki <= qi
        band = ki >= qi
        mask_p = jnp.logical_and(band, i > 0)
        mask_n = jnp.logical_and(band, i < nb - 1)
        lane = lax.broadcasted_iota(jnp.int32, (BLOCK, LANES), 1)

        @pl.when(jnp.logical_and(r == 0, i == 0))
        def _():
            db_ref[...] = jnp.zeros_like(db_ref)

        def products(h):
            sl = slice(h * HEAD_DIM, (h + 1) * HEAD_DIM)
            q_i, q_n = q_ref[:, sl], qn_ref[:, sl]
            k_p, k_c = kp_ref[:, sl], kc_ref[:, sl]
            v_p, v_c = vp_ref[:, sl], vc_ref[:, sl]
            do_i, do_n = do_ref[:, sl], don_ref[:, sl]
            return (_dot(q_i, k_c, NT), _dot(do_i, v_c, NT), _dot(q_i, k_p, NT), _dot(do_i, v_p, NT),
                    _dot(q_n, k_c, NT), _dot(do_n, v_c, NT))

        ahead = [products(h) for h in range(min(PRODUCTS_AHEAD, H))]
        for h in range(H):
            sl = slice(h * HEAD_DIM, (h + 1) * HEAD_DIM)
            s1, dp1, s2, dp2, s3, dp3 = ahead.pop(0)
            if h + PRODUCTS_AHEAD < H:
                ahead.append(products(h + PRODUCTS_AHEAD))
            q_i, q_n = q_ref[:, sl], qn_ref[:, sl]
            k_p, k_c = kp_ref[:, sl], kc_ref[:, sl]
            do_i, do_n = do_ref[:, sl], don_ref[:, sl]
            l_i, l_n = _lane_col(l_ref[...], h, lane), _lane_col(ln_ref[...], h, lane)
            d_i, d_n = _lane_col(d_ref[...], h, lane), _lane_col(dn_ref[...], h, lane)
            b_c = b_ref[h, :, BLOCK:]
            b_p = b_ref[h, :, :BLOCK]
            p1 = jnp.exp(jnp.where(mask_c, s1 * SCALE + b_c, NEG) - l_i)
            ds1 = p1 * (dp1 - d_i)
            ds1b = ds1.astype(BF16)
            p1b = p1.astype(BF16)
            p2 = jnp.exp(jnp.where(mask_p, s2 * SCALE + b_p, NEG) - l_i)
            ds2 = p2 * (dp2 - d_i)
            ds2b = ds2.astype(BF16)
            p3 = jnp.exp(jnp.where(mask_n, s3 * SCALE + b_p, NEG) - l_n)
            ds3b = (p3 * (dp3 - d_n)).astype(BF16)
            p3b = p3.astype(BF16)
            dq = _dot(ds1b, k_c, NN) + _dot(ds2b, k_p, NN)
            dk = _dot(ds1b, q_i, TN) + _dot(ds3b, q_n, TN)
            dv = _dot(p1b, do_i, TN) + _dot(p3b, do_n, TN)
            dq_ref[:, sl] = (dq * SCALE).astype(BF16)
            dkv_ref[0, :, sl] = (dk * SCALE).astype(BF16)
            dkv_ref[1, :, sl] = dv.astype(BF16)
            db_ref[h, :, BLOCK:] += ds1
            db_ref[h, :, :BLOCK] += ds2

    def blk(slab, shift):
        if shift < 0:
            return pl.BlockSpec((None, None, BLOCK, A), lambda r, i: (slab, r, jnp.maximum(i - 1, 0), 0))
        return pl.BlockSpec((None, None, BLOCK, A), lambda r, i: (slab, r, i, 0))

    def row(width, shift):
        if shift > 0:
            return pl.BlockSpec((None, BLOCK, width), lambda r, i: (r, jnp.minimum(i + 1, nb - 1), 0))
        return pl.BlockSpec((None, BLOCK, width), lambda r, i: (r, i, 0))

    in_specs = [row(A, 0), row(A, 1), blk(0, -1), blk(0, 0), blk(1, -1), blk(1, 0),
                row(A, 0), row(A, 1), row(LANES, 0), row(LANES, 1), row(LANES, 0), row(LANES, 1),
                pl.BlockSpec((H, BLOCK, 2 * BLOCK), lambda r, i: (0, 0, 0))]
    dq, dkv, db = pl.pallas_call(
        body, name=name, grid=(dil, nb), in_specs=in_specs,
        out_specs=[pl.BlockSpec((None, BLOCK, A), lambda r, i: (r, i, 0)),
                   pl.BlockSpec((2, None, BLOCK, A), lambda r, i: (0, r, i, 0)),
                   pl.BlockSpec((H, BLOCK, 2 * BLOCK), lambda r, i: (0, 0, 0))],
        out_shape=[jax.ShapeDtypeStruct((dil, L, A), BF16), jax.ShapeDtypeStruct((2, dil, L, A), BF16),
                   jax.ShapeDtypeStruct((H, BLOCK, 2 * BLOCK), F32)],
        compiler_params=_params("arbitrary", "arbitrary"),
    )(qv, qv, kvv, kvv, kvv, kvv, dov, dov, lsev, lsev, dhv, dhv, bias)
    return dq.reshape(S, A), dkv.reshape(2, S, A), db


def _sum_leading(stack, out_dtype, name, tr=256, tc=2048):
    n, R, C = stack.shape
    tr = _tile(R, tr, 16)
    tc = _tile(C, tc)

    def body(s_ref, o_ref):
        acc = s_ref[0].astype(F32)
        for q in range(1, n):
            acc = acc + s_ref[q].astype(F32)
        o_ref[...] = acc.astype(out_dtype)

    return pl.pallas_call(
        body, name=name, grid=(R // tr, C // tc),
        in_specs=[pl.BlockSpec((n, tr, tc), lambda i, j: (0, i, j))],
        out_specs=pl.BlockSpec((tr, tc), lambda i, j: (i, j)),
        out_shape=jax.ShapeDtypeStruct((R, C), out_dtype),
        compiler_params=_params("parallel", "parallel"),
    )(stack)


def _add_half(g, t, c_idx, kind, name, after=None, tr=256, tc=2048):
    R, C = t.shape
    tr = _tile(R, tr, 16)
    tc = _tile(C, tc)
    nrb, ncb = R // tr, C // tc
    extra = _as_list(after)

    def body(c_ref, g_ref, t_ref, *rest):
        del c_ref
        o_ref = rest[len(extra)]
        o_ref[...] = (g_ref[...].astype(F32) + t_ref[...].astype(F32)).astype(BF16)

    if kind == "col":
        g_map = lambda i, j, c_ref: (c_ref[0] * nrb + i, j)
    else:
        g_map = lambda i, j, c_ref: (i, c_ref[0] * ncb + j)
    same = lambda i, j, c_ref: (i, j)
    return pl.pallas_call(
        body, name=name,
        grid_spec=pltpu.PrefetchScalarGridSpec(
            num_scalar_prefetch=1, grid=(nrb, ncb),
            in_specs=[pl.BlockSpec((tr, tc), g_map), pl.BlockSpec((tr, tc), same)] + [ANY] * len(extra),
            out_specs=pl.BlockSpec((tr, tc), same)),
        out_shape=jax.ShapeDtypeStruct((R, C), BF16),
        compiler_params=_params("parallel", "parallel"),
    )(c_idx, g, t, *extra)


def _cast_into_full(w, kind, chip_idx, name, tr=256, tc=2048):
    R, C = w.shape
    tr = _tile(R, tr, 16)
    tc = _tile(C, tc)
    nrb, ncb = R // tr, C // tc

    def body(k_ref, w_ref, o_ref):
        del k_ref
        o_ref[...] = w_ref[...].astype(BF16)

    if kind == "col":
        o_map = lambda i, j, k_ref: (i, k_ref[0] * ncb + j)
        full = (R, N_CHIPS * C)
    else:
        o_map = lambda i, j, k_ref: (k_ref[0] * nrb + i, j)
        full = (N_CHIPS * R, C)
    return pl.pallas_call(
        body, name=name,
        grid_spec=pltpu.PrefetchScalarGridSpec(
            num_scalar_prefetch=1, grid=(nrb, ncb),
            in_specs=[pl.BlockSpec((tr, tc), lambda i, j, k_ref: (i, j))],
            out_specs=pl.BlockSpec((tr, tc), o_map)),
        out_shape=jax.ShapeDtypeStruct(full, BF16),
        compiler_params=_params("parallel", "parallel"),
    )(chip_idx, w)


def _sum_into_shard(p, u, idx, kind, name, tr=256, tc=2048):
    _, R, C = u.shape
    tr = _tile(R, tr, 16)
    tc = _tile(C, tc)
    nrb, ncb = R // tr, C // tc

    def body(i_ref, p_ref, u_ref, o_ref):
        del i_ref
        acc = p_ref[...].astype(F32)
        for q in range(N_CHIPS - 1):
            acc = acc + u_ref[q].astype(F32)
        o_ref[...] = acc

    if kind == "col":
        p_map = lambda i, j, r: (i, r[0] * ncb + j)
        o_map = lambda i, j, r: (r[1] * nrb + i, j)
        full = (2 * R, C)
    else:
        p_map = lambda i, j, r: (r[0] * nrb + i, j)
        o_map = lambda i, j, r: (i, r[1] * ncb + j)
        full = (R, 2 * C)
    return pl.pallas_call(
        body, name=name,
        grid_spec=pltpu.PrefetchScalarGridSpec(
            num_scalar_prefetch=1, grid=(nrb, ncb),
            in_specs=[pl.BlockSpec((tr, tc), p_map), pl.BlockSpec((N_CHIPS - 1, tr, tc), lambda i, j, r: (0, i, j))],
            out_specs=pl.BlockSpec((tr, tc), o_map)),
        out_shape=jax.ShapeDtypeStruct(full, F32),
        compiler_params=_params("parallel", "parallel"),
    )(idx, p, u)


def _adamw(w, g, m, v, name, tr=256, tc=2048):
    R, C = w.shape
    tr = _tile(R, tr, 8)
    tc = _tile(C, tc)
    c1 = 1.0 - ADAM_B1 ** ADAM_STEP
    c2 = 1.0 - ADAM_B2 ** ADAM_STEP

    def body(w_ref, g_ref, m_ref, v_ref, d_ref, nm_ref, nv_ref):
        gv = g_ref[...]
        nm = ADAM_B1 * m_ref[...] + (1.0 - ADAM_B1) * gv
        nv = ADAM_B2 * v_ref[...] + (1.0 - ADAM_B2) * (gv * gv)
        d_ref[...] = -ADAM_LR * ((nm / c1) / (jnp.sqrt(nv / c2) + ADAM_EPS) + ADAM_WD * w_ref[...])
        nm_ref[...] = nm
        nv_ref[...] = nv

    blk = pl.BlockSpec((tr, tc), lambda i, j: (i, j))
    sh = jax.ShapeDtypeStruct((R, C), F32)
    return pl.pallas_call(
        body, name=name, grid=(R // tr, C // tc), in_specs=[blk] * 4, out_specs=[blk] * 3,
        out_shape=[sh, sh, sh], compiler_params=_params("parallel", "parallel"),
    )(w, g, m, v)


def _xyc():
    return lax.axis_index("x"), lax.axis_index("y"), lax.axis_index("c")


def _drain(copies):
    for cp in copies:
        if cp.is_remote:
            cp.wait_send()
        else:
            cp.wait()


def _other_chips(x, y):
    return [(1 - x, y), (x, 1 - y), (1 - x, 1 - y)]


def _allgather_small(blk, name, after=None):
    R, C = blk.shape
    extra = _as_list(after)

    def body(*refs):
        x_ref = refs[0]
        out_ref, send_sems, recv_sems, local_sem = refs[1 + len(extra):]
        x, y, c = _xyc()
        me = 4 * x + 2 * y + c
        mine = pltpu.make_async_copy(x_ref, out_ref.at[me], local_sem)
        mine.start()
        peers = []
        for k in range(1, N_DEV):
            px = 1 - x if (k >> 2) & 1 else x
            py = 1 - y if (k >> 1) & 1 else y
            pc = 1 - c if k & 1 else c
            peers.append((px, py, pc))
        sends = []
        for k, peer in enumerate(peers):
            cp = pltpu.make_async_remote_copy(
                src_ref=x_ref, dst_ref=out_ref.at[me], send_sem=send_sems.at[k], recv_sem=recv_sems.at[k],
                device_id=peer, device_id_type=MESH)
            cp.start()
            sends.append(cp)
        for k, (px, py, pc) in enumerate(peers):
            pltpu.make_async_remote_copy(
                src_ref=x_ref, dst_ref=out_ref.at[4 * px + 2 * py + pc], send_sem=send_sems.at[k],
                recv_sem=recv_sems.at[k], device_id=(px, py, pc), device_id_type=MESH).wait_recv()
        for cp in sends:
            cp.wait_send()
        mine.wait()

    return pl.pallas_call(
        body, name=name, in_specs=[VMEM_SPEC] + [ANY] * len(extra), out_specs=VMEM_SPEC,
        out_shape=jax.ShapeDtypeStruct((N_DEV, R, C), blk.dtype),
        scratch_shapes=[pltpu.SemaphoreType.DMA((N_DEV - 1,)), pltpu.SemaphoreType.DMA((N_DEV - 1,)),
                        pltpu.SemaphoreType.DMA],
        compiler_params=pltpu.CompilerParams(vmem_limit_bytes=VMEM_LIMIT),
    )(blk, *extra)


def _full_region(ref, kind, chip, half, shard_shape):
    r, cn = shard_shape
    hr = r // 2
    if kind == "col":
        rows = pl.ds(0, r) if half is None else pl.ds(pl.multiple_of(half * hr, 16), hr)
        return ref.at[rows, pl.ds(pl.multiple_of(chip * cn, LANES), cn)]
    if half is None:
        return ref.at[pl.ds(pl.multiple_of(chip * r, 16), r), :]
    return ref.at[pl.ds(pl.multiple_of(chip * r + half * hr, 16), hr), :]


def _allgather_weights(fulls, kinds, shapes, name):
    n = len(fulls)

    def body(*refs):
        outs = refs[n:2 * n]
        send_sems, recv_sems = refs[2 * n:]
        x, y, c = _xyc()
        chip = 2 * x + y
        sib = (x, y, 1 - c)
        others = _other_chips(x, y)
        started = []
        for w in range(n):
            mine = _full_region(outs[w], kinds[w], chip, c, shapes[w])
            for j, (ox, oy) in enumerate(others):
                cp = pltpu.make_async_remote_copy(
                    src_ref=mine, dst_ref=mine, send_sem=send_sems.at[6 * w + j], recv_sem=recv_sems.at[6 * w + j],
                    device_id=(ox, oy, c), device_id_type=MESH)
                cp.start()
                started.append(cp)
        for w in range(n):
            for j, (ox, oy) in enumerate(others):
                landed = _full_region(outs[w], kinds[w], 2 * ox + oy, c, shapes[w])
                pltpu.make_async_remote_copy(
                    src_ref=landed, dst_ref=landed, send_sem=send_sems.at[6 * w + j], recv_sem=recv_sems.at[6 * w + j],
                    device_id=(ox, oy, c), device_id_type=MESH).wait_recv()
                cp = pltpu.make_async_remote_copy(
                    src_ref=landed, dst_ref=landed, send_sem=send_sems.at[6 * w + 3 + j],
                    recv_sem=recv_sems.at[6 * w + 3 + j], device_id=sib, device_id_type=MESH)
                cp.start()
                started.append(cp)
        for w in range(n):
            for j, (ox, oy) in enumerate(others):
                theirs = _full_region(outs[w], kinds[w], 2 * ox + oy, 1 - c, shapes[w])
                pltpu.make_async_remote_copy(
                    src_ref=theirs, dst_ref=theirs, send_sem=send_sems.at[6 * w + 3 + j],
                    recv_sem=recv_sems.at[6 * w + 3 + j], device_id=sib, device_id_type=MESH).wait_recv()
        _drain(started)

    return pl.pallas_call(
        body, name=name, in_specs=[ANY] * n, out_specs=[ANY] * n,
        out_shape=[jax.ShapeDtypeStruct(f.shape, f.dtype) for f in fulls],
        input_output_aliases={w: w for w in range(n)},
        scratch_shapes=[pltpu.SemaphoreType.DMA((6 * n,)), pltpu.SemaphoreType.DMA((6 * n,))],
    )(*fulls)


def _region_of_size(ref, kind, shard_shape, count):
    r, cn = shard_shape
    if kind == "col":
        return ref.at[pl.ds(0, r // 2), pl.ds(0, count * cn)]
    return ref.at[pl.ds(0, count * (r // 2)), :]


def _allgather_weights_seq(fulls, kinds, shapes, name, collective_id):
    n = len(fulls)
    refs = [jax.new_ref(f, memory_space=pltpu.MemorySpace.HBM) for f in fulls]

    def body(send_sems, recv_sems):
        x, y, c = _xyc()
        chip = 2 * x + y
        sib = (x, y, 1 - c)
        others = _other_chips(x, y)
        peers = [(ox, oy, c) for ox, oy in others] + [sib]
        barrier = pltpu.get_barrier_semaphore()
        for peer in peers:
            pl.semaphore_signal(barrier, inc=1, device_id=peer, device_id_type=MESH)
        pl.semaphore_wait(barrier, len(peers))

        def copy(w, region, sem, to):
            return pltpu.make_async_remote_copy(src_ref=region, dst_ref=region, send_sem=send_sems.at[sem],
                                                recv_sem=recv_sems.at[sem], device_id=to, device_id_type=MESH)

        for w in range(n):
            mine = _full_region(refs[w], kinds[w], chip, c, shapes[w])
            for ox, oy in others:
                copy(w, mine, 2 * w, (ox, oy, c)).start()
        for w in range(n):
            three = _region_of_size(refs[w], kinds[w], shapes[w], 3)
            copy(w, three, 2 * w, sib).wait_recv()
            for ox, oy in others:
                copy(w, _full_region(refs[w], kinds[w], 2 * ox + oy, c, shapes[w]), 2 * w + 1, sib).start()
        for w in range(n):
            three = _region_of_size(refs[w], kinds[w], shapes[w], 3)
            copy(w, three, 2 * w + 1, sib).wait_recv()
            copy(w, three, 2 * w, sib).wait_send()
            copy(w, three, 2 * w + 1, sib).wait_send()

    pl.kernel(
        body, out_type=(), mesh=plsc.ScalarSubcoreMesh(axis_name="seq", num_cores=1), name=name,
        scratch_types=[pltpu.SemaphoreType.DMA((2 * n,)), pltpu.SemaphoreType.DMA((2 * n,))],
        compiler_params=pltpu.CompilerParams(collective_id=collective_id),
    )()
    return [r[...] for r in refs]


def _half_of(ref, kind, half):
    r, cn = ref.shape
    if kind == "col":
        return ref.at[pl.ds(pl.multiple_of(half * (r // 2), 16), r // 2), :]
    return ref.at[:, pl.ds(pl.multiple_of(half * (cn // 2), LANES), cn // 2)]


def _shard_of(ref, kind, chip):
    r, cn = ref.shape
    if kind == "col":
        return ref.at[:, pl.ds(pl.multiple_of(chip * (cn // N_CHIPS), LANES), cn // N_CHIPS)]
    return ref.at[pl.ds(pl.multiple_of(chip * (r // N_CHIPS), 16), r // N_CHIPS), :]


def _exchange_halves(grads, kinds, name):
    n = len(grads)

    def body(*refs):
        gs = refs[:n]
        ts = refs[n:2 * n]
        send_sems, recv_sems = refs[2 * n:]
        x, y, c = _xyc()
        cps = []
        for w in range(n):
            cp = pltpu.make_async_remote_copy(
                src_ref=_half_of(gs[w], kinds[w], 1 - c), dst_ref=ts[w], send_sem=send_sems.at[w],
                recv_sem=recv_sems.at[w], device_id=(x, y, 1 - c), device_id_type=MESH)
            cp.start()
            cps.append(cp)
        for cp in cps:
            cp.wait()

    out_shape = []
    for gr, kind in zip(grads, kinds):
        r, cn = gr.shape
        out_shape.append(jax.ShapeDtypeStruct((r // 2, cn) if kind == "col" else (r, cn // 2), gr.dtype))
    return pl.pallas_call(
        body, name=name, in_specs=[ANY] * n, out_specs=[ANY] * n, out_shape=out_shape,
        scratch_shapes=[pltpu.SemaphoreType.DMA((n,)), pltpu.SemaphoreType.DMA((n,))],
    )(*grads)


def _exchange_halves_seq(grads, kinds, name, collective_id):
    n = len(grads)

    def body(*refs):
        gs = refs[:n]
        ts = refs[n:2 * n]
        send_sems, recv_sems = refs[2 * n:]
        x, y, c = _xyc()
        sib = (x, y, 1 - c)
        barrier = pltpu.get_barrier_semaphore()
        pl.semaphore_signal(barrier, inc=1, device_id=sib, device_id_type=MESH)
        pl.semaphore_wait(barrier, 1)
        cps = []
        for w in range(n):
            cp = pltpu.make_async_remote_copy(
                src_ref=_half_of(gs[w], kinds[w], 1 - c), dst_ref=ts[w], send_sem=send_sems.at[w],
                recv_sem=recv_sems.at[w], device_id=sib, device_id_type=MESH)
            cp.start()
            cps.append(cp)
        for cp in cps:
            cp.wait()

    out_type = []
    for gr, kind in zip(grads, kinds):
        r, cn = gr.shape
        out_type.append(jax.ShapeDtypeStruct((r // 2, cn) if kind == "col" else (r, cn // 2), gr.dtype))
    return pl.kernel(
        body, out_type=out_type, mesh=plsc.ScalarSubcoreMesh(axis_name="seq", num_cores=1), name=name,
        scratch_types=[pltpu.SemaphoreType.DMA((n,)), pltpu.SemaphoreType.DMA((n,))],
        compiler_params=pltpu.CompilerParams(collective_id=collective_id),
    )(*grads)


def _scatter_partials(parts, kinds, name):
    n = len(parts)

    def body(*refs):
        ps = refs[:n]
        us = refs[n:2 * n]
        send_sems, recv_sems = refs[2 * n:]
        x, y, c = _xyc()
        others = _other_chips(x, y)
        cps = []
        for w in range(n):
            for j, (ox, oy) in enumerate(others):
                cp = pltpu.make_async_remote_copy(
                    src_ref=_shard_of(ps[w], kinds[w], 2 * ox + oy), dst_ref=us[w].at[j],
                    send_sem=send_sems.at[3 * w + j], recv_sem=recv_sems.at[3 * w + j],
                    device_id=(ox, oy, c), device_id_type=MESH)
                cp.start()
                cps.append(cp)
        for cp in cps:
            cp.wait()

    out_shape = []
    for p, kind in zip(parts, kinds):
        r, cn = p.shape
        hs = (r, cn // N_CHIPS) if kind == "col" else (r // N_CHIPS, cn)
        out_shape.append(jax.ShapeDtypeStruct((N_CHIPS - 1,) + hs, p.dtype))
    return pl.pallas_call(
        body, name=name, in_specs=[ANY] * n, out_specs=[ANY] * n, out_shape=out_shape,
        scratch_shapes=[pltpu.SemaphoreType.DMA((3 * n,)), pltpu.SemaphoreType.DMA((3 * n,))],
    )(*parts)


def _scatter_partials_seq(parts, kinds, name, collective_id):
    n = len(parts)

    def body(*refs):
        ps = refs[:n]
        us = refs[n:2 * n]
        send_sems, recv_sems = refs[2 * n:]
        x, y, c = _xyc()
        others = _other_chips(x, y)
        barrier = pltpu.get_barrier_semaphore()
        for ox, oy in others:
            pl.semaphore_signal(barrier, inc=1, device_id=(ox, oy, c), device_id_type=MESH)
        pl.semaphore_wait(barrier, len(others))
        for w in range(n):
            for j, (ox, oy) in enumerate(others):
                pltpu.make_async_remote_copy(
                    src_ref=_shard_of(ps[w], kinds[w], 2 * ox + oy), dst_ref=us[w].at[j],
                    send_sem=send_sems.at[w], recv_sem=recv_sems.at[w],
                    device_id=(ox, oy, c), device_id_type=MESH).start()
        for w in range(n):
            pltpu.make_async_remote_copy(
                src_ref=us[w], dst_ref=us[w], send_sem=send_sems.at[w], recv_sem=recv_sems.at[w],
                device_id=(x, y, c), device_id_type=MESH).wait()

    out_type = []
    for p, kind in zip(parts, kinds):
        r, cn = p.shape
        hs = (r, cn // N_CHIPS) if kind == "col" else (r // N_CHIPS, cn)
        out_type.append(jax.ShapeDtypeStruct((N_CHIPS - 1,) + hs, p.dtype))
    return pl.kernel(
        body, out_type=out_type, mesh=plsc.ScalarSubcoreMesh(axis_name="seq", num_cores=1), name=name,
        scratch_types=[pltpu.SemaphoreType.DMA((n,)), pltpu.SemaphoreType.DMA((n,))],
        compiler_params=pltpu.CompilerParams(collective_id=collective_id),
    )(*parts)


def _join_halves(halves, kinds, name):
    n = len(halves)

    def body(*refs):
        outs = refs[n:2 * n]
        send_sems, recv_sems = refs[2 * n:]
        x, y, c = _xyc()
        cps = []
        for w in range(n):
            mine = _half_of(outs[w], kinds[w], c)
            cp = pltpu.make_async_remote_copy(
                src_ref=mine, dst_ref=mine, send_sem=send_sems.at[w], recv_sem=recv_sems.at[w],
                device_id=(x, y, 1 - c), device_id_type=MESH)
            cp.start()
            cps.append(cp)
        for w in range(n):
            theirs = _half_of(outs[w], kinds[w], 1 - c)
            pltpu.make_async_remote_copy(
                src_ref=theirs, dst_ref=theirs, send_sem=send_sems.at[w], recv_sem=recv_sems.at[w],
                device_id=(x, y, 1 - c), device_id_type=MESH).wait_recv()
        _drain(cps)

    return pl.pallas_call(
        body, name=name, in_specs=[ANY] * n, out_specs=[ANY] * n,
        out_shape=[jax.ShapeDtypeStruct(h.shape, h.dtype) for h in halves],
        input_output_aliases={w: w for w in range(n)},
        scratch_shapes=[pltpu.SemaphoreType.DMA((n,)), pltpu.SemaphoreType.DMA((n,))],
    )(*halves)


def kernel(x, a_norm, a_w_in, a_conv_w, a_conv_b, a_ln_g, a_ln_b, a_w_out, kv_norm, w_kv, b_norm, b_w_in, b_w_out, rel_bias, final_norm, loss_target, m_a_norm, m_a_w_in, m_a_conv_w, m_a_conv_b, m_a_ln_g, m_a_ln_b, m_a_w_out, m_kv_norm, m_w_kv, m_b_norm, m_b_w_in, m_b_w_out, m_rel_bias, m_final_norm, v_a_norm, v_a_w_in, v_a_conv_w, v_a_conv_b, v_a_ln_g, v_a_ln_b, v_a_w_out, v_kv_norm, v_w_kv, v_b_norm, v_b_w_in, v_b_w_out, v_rel_bias, v_final_norm):
    S, D = x.shape[1], x.shape[2]
    E = a_w_out.shape[1] * N_CHIPS
    A = b_w_out.shape[1] * N_CHIPS
    H = A // HEAD_DIM
    DC = D // N_CHIPS
    xs = x.reshape(S, D)
    tgt = loss_target.reshape(S, D)
    cx, cy, cc = _xyc()
    chip = 2 * cx + cy
    c_idx = jnp.reshape(cc, (1,)).astype(jnp.int32)

    big_names = ["a_w_in", "a_w_out", "w_kv", "b_w_in", "b_w_out"]
    kinds = ["col", "row", "col", "col", "row"]
    big_w = [a_w_in[0], a_w_out[0], w_kv, b_w_in[0], b_w_out[0]]
    big_m = [m_a_w_in[0], m_a_w_out[0], m_w_kv, m_b_w_in[0], m_b_w_out[0]]
    big_v = [v_a_w_in[0], v_a_w_out[0], v_w_kv, v_b_w_in[0], v_b_w_out[0]]
    chip_idx = jnp.reshape(chip, (1,)).astype(jnp.int32)
    placed = [_cast_into_full(big_w[w], kinds[w], chip_idx, "cast_" + big_names[w]) for w in range(5)]
    shard_shapes = [w.shape for w in big_w]
    (wa_in,) = _allgather_weights_seq(placed[0:1], kinds[0:1], shard_shapes[0:1], "ag_seq_a_in", 0)
    (wa_out,) = _allgather_weights_seq(placed[1:2], kinds[1:2], shard_shapes[1:2], "ag_seq_a_out", 1)
    (wkv,) = _allgather_weights_seq(placed[2:3], kinds[2:3], shard_shapes[2:3], "ag_seq_kv", 5)
    wb_in, wb_out = _allgather_weights_seq(placed[3:5], kinds[3:5], shard_shapes[3:5], "ag_seq_b", 2)

    def row_at(vec, q):
        return jnp.pad(vec, ((q, 7 - q), (0, 0)))

    def pack_sharded(an, cw, cb, lg, lb):
        return jnp.concatenate([row_at(an, 0), jnp.pad(cw[0], ((0, 1), (0, 0))),
                                row_at(lg, 0) + row_at(lb, 1) + row_at(cb, 2)], axis=0)

    small_w = pack_sharded(a_norm, a_conv_w, a_conv_b, a_ln_g, a_ln_b)
    gathered = _allgather_small(small_w, "ag_small_params")
    small_full = jnp.concatenate([gathered[2 * k] for k in range(N_CHIPS)], axis=1)
    g_a = small_full[0:1]
    conv_w32 = small_full[8:8 + HALO]
    ln_g = small_full[40:41]
    ln_b = small_full[41:42]
    conv_b = small_full[42:43]
    g_kv = kv_norm.reshape(1, D)
    g_b = b_norm.reshape(1, D)
    g_f = final_norm.reshape(1, D)

    rb_t = jnp.pad(rel_bias.T, ((0, 0), (0, LANES - N_BUCKETS)))
    onehots = [_onehot(dil) for _, dil in GROUPS]
    biases = [_bias_table(rb_t, onehots[g], "bias_table_%d" % g).reshape(H, BLOCK, 2 * BLOCK)
              for g in range(len(GROUPS))]

    dils = tuple(dil for _, dil in GROUPS)
    assert dils[0] == 1
    n_g = len(GROUPS)
    ((h0,),) = _rms_fwd(xs, [g_a], (1,), "rms_a")
    proj3 = _matmul(h0, wa_in, "nn", BF16, "mm_a_in", out_slab=E)
    conv = _conv_fwd(proj3, conv_w32, conv_b, "conv_fwd")
    y_a = _ln_gate_fwd(conv, proj3, ln_g, ln_b, "ln_gate_fwd")
    x1 = _matmul(y_a, wa_out, "nn", F32, "mm_a_out", res=xs)
    hks, hbs = _rms_fwd(x1, [g_kv, g_b], dils, "rms_kv_b")
    kvs = [_matmul(hks[g], wkv, "nn", BF16, "mm_kv_%d" % g, out_slab=A, b_off=2 * g * A, n_cols=2 * A)
           for g in range(n_g)]
    qs = [_matmul(hbs[g], wb_in, "nn", BF16, "mm_q_%d" % g, b_off=g * A, n_cols=A, after=kvs)
          for g in range(n_g)]
    zb = _matmul(hbs[0], wb_in, "nn", BF16, "mm_zb", b_off=n_g * A, n_cols=A, after=kvs)
    os_, lses = [], []
    for g, dil in enumerate(dils):
        o_g, lse_g = _attn_fwd(qs[g], kvs[g], biases[g], dil, "attn_fwd_%d" % g)
        os_.append(o_g)
        lses.append(lse_g)
    y_b, o_m, lse_d = _attn_merge(os_, lses, zb, dils, "attn_merge")
    x2 = _matmul(y_b, wb_out, "nn", F32, "mm_b_out", res=x1)
    loss_part, dx2, dx2b, gg_f = _final_head(x2, g_f, tgt, "final_head")
    loss = lax.psum(loss_part[0, 0], ("x", "y", "c"))

    dw_tiles = dict(tm=1024, tn=1024, tk=4096)
    dy_b = _matmul(dx2b, wb_out, "nt", BF16, "mm_b_out_dx", after=loss.reshape(1, 1))
    dwb_out = _matmul(y_b, dx2b, "tn", BF16, "mm_b_out_dw", **dw_tiles)
    dos, dhs, dzb = _gate_bwd(dy_b, o_m, zb, dils, "gate_bwd")
    dbs, cots = [], []
    dwb_in = dwkv = None
    for g, dil in enumerate(dils):
        dq, dkv, db = _attn_bwd(qs[g], kvs[g], dos[g], lse_d[g], dhs[g], biases[g], dil, "attn_bwd_%d" % g)
        dbs.append(db.reshape(H, BLOCK * 2 * BLOCK))
        dwb_in = _matmul(hbs[g], dq, "tn", BF16, "mm_q_dw_%d" % g, out_off=g * A, out_cols=(n_g + 1) * A,
                         out_alias=dwb_in, **dw_tiles)
        dwkv = _matmul(hks[g], dkv, "tn", BF16, "mm_kv_dw_%d" % g, b_slab=True, out_off=2 * g * A,
                       out_cols=2 * n_g * A, out_alias=dwkv, **dw_tiles)
        cots.append((_matmul(dkv, wkv, "nt", BF16, "mm_kv_dx_%d" % g, a_slab=True, b_off=2 * g * A), 0, dil))
        cots.append((_matmul(dq, wb_in, "nt", BF16, "mm_q_dx_%d" % g, b_off=g * A), 1, dil))
    dwb_in = _matmul(hbs[0], dzb, "tn", BF16, "mm_zb_dw", out_off=n_g * A, out_cols=(n_g + 1) * A,
                     out_alias=dwb_in, **dw_tiles)
    cots.append((_matmul(dzb, wb_in, "nt", BF16, "mm_zb_dx", b_off=n_g * A), 1, 1))
    chip_c = jnp.stack([chip, cc]).astype(jnp.int32)

    def scatter_group(idx, grads, tag, collective_id, exchange_id=None, behind=None):
        ks = [kinds[w] for w in idx]
        if exchange_id is None:
            theirs = _exchange_halves(grads, ks, "rs_exchange_" + tag)
        else:
            theirs = _exchange_halves_seq(grads, ks, "rs_exchange_seq_" + tag, exchange_id)
        parts = [_add_half(grads[q], theirs[q], c_idx, ks[q], "rs_add_half_%d" % w, after=behind)
                 for q, w in enumerate(idx)]
        return parts, _scatter_partials_seq(parts, ks, "rs_seq_" + tag, collective_id)

    def reduce_group(idx, parts, slots, tag):
        ks = [kinds[w] for w in idx]
        halves = [_sum_into_shard(parts[q], slots[q], chip_c, ks[q], "rs_sum_chips_%d" % w)
                  for q, w in enumerate(idx)]
        return _join_halves(halves, ks, "rs_join_" + tag)

    parts_b, slots_b = scatter_group([2, 3, 4], [dwkv, dwb_in, dwb_out], "b", 3, exchange_id=6,
                                     behind=[ct[0] for ct in cots])
    g_rel_t = _bias_grad(dbs, onehots, "bias_grad")
    dx1, dx1b, gg_kvb = _rms_bwd(x1, cots, [g_kv, g_b], dx2, "rms_kv_b_bwd", after=parts_b)
    dy_a = _matmul(dx1b, wa_out, "nt", BF16, "mm_a_out_dx")
    dwa_out = _matmul(y_a, dx1b, "tn", BF16, "mm_a_out_dw", **dw_tiles)
    dconv, dproj3, gg_ln = _ln_gate_bwd(conv, proj3, dy_a, ln_g, ln_b, "ln_gate_bwd")
    dproj3, g_conv_w = _conv_bwd(proj3, dconv, conv_w32, dproj3, "conv_bwd")
    dwa_in = _matmul(h0, dproj3, "tn", BF16, "mm_a_in_dw", b_slab=True, **dw_tiles)
    parts_a, slots_a = scatter_group([0, 1], [dwa_in, dwa_out], "a", 4)
    dh0 = _matmul(dproj3, wa_in, "nt", BF16, "mm_a_in_dx", a_slab=True, after=parts_a, tn=512)
    grad_x, _, gg_a = _rms_bwd(xs, [(dh0, 0, 1)], [g_a], dx1, "rms_a_bwd")

    big_g = [None] * 5
    big_g[2:5] = reduce_group([2, 3, 4], parts_b, slots_b, "b")
    big_g[0:2] = reduce_group([0, 1], parts_a, slots_a, "a")

    def rel_rows(rb):
        return jnp.pad(rb.reshape(1, N_BUCKETS * H), ((0, 7), (0, D - N_BUCKETS * H)))

    small_g = jnp.concatenate([gg_a, g_conv_w, gg_ln, gg_kvb, gg_f, rel_rows(g_rel_t[:, :N_BUCKETS].T)], axis=0)
    small_sum = _sum_leading(_allgather_small(small_g, "ag_small_grads", after=[slots_a[0], slots_b[0]]), F32,
                             "sum_small_grads", tr=72)
    g_sharded = lax.dynamic_slice(small_sum, (0, chip * DC), (48, DC))
    g_repl = small_sum[48:72]

    outs_g, outs_d, outs_m, outs_v = {}, {}, {}, {}
    for w, nm in enumerate(big_names):
        d_, m_, v_ = _adamw(big_w[w], big_g[w], big_m[w], big_v[w], "adamw_" + nm)
        outs_g[nm], outs_d[nm], outs_m[nm], outs_v[nm] = big_g[w], d_, m_, v_
    sm_m = pack_sharded(m_a_norm, m_a_conv_w, m_a_conv_b, m_a_ln_g, m_a_ln_b)
    sm_v = pack_sharded(v_a_norm, v_a_conv_w, v_a_conv_b, v_a_ln_g, v_a_ln_b)
    sd, smm, svv = _adamw(small_w, g_sharded, sm_m, sm_v, "adamw_small_sharded")

    def unpack_sharded(p):
        return {"a_norm": p[0:1], "a_conv_w": p[8:8 + CONV_TAPS].reshape(1, CONV_TAPS, DC), "a_ln_g": p[40:41],
                "a_ln_b": p[41:42], "a_conv_b": p[42:43]}

    for src, dst in ((g_sharded, outs_g), (sd, outs_d), (smm, outs_m), (svv, outs_v)):
        dst.update(unpack_sharded(src))

    def pack_repl(kn, bn, fn, rb):
        return jnp.concatenate([row_at(kn.reshape(1, D), 0) + row_at(bn.reshape(1, D), 1),
                                row_at(fn.reshape(1, D), 0), rel_rows(rb)], axis=0)

    rp_w = pack_repl(kv_norm, b_norm, final_norm, rel_bias)
    rp_m = pack_repl(m_kv_norm, m_b_norm, m_final_norm, m_rel_bias)
    rp_v = pack_repl(v_kv_norm, v_b_norm, v_final_norm, v_rel_bias)
    rd, rmm, rvv = _adamw(rp_w, g_repl, rp_m, rp_v, "adamw_small_replicated")

    def unpack_repl(p):
        return {"kv_norm": p[0], "b_norm": p[1:2], "final_norm": p[8],
                "rel_bias": p[16, :N_BUCKETS * H].reshape(N_BUCKETS, H)}

    for src, dst in ((g_repl, outs_g), (rd, outs_d), (rmm, outs_m), (rvv, outs_v)):
        dst.update(unpack_repl(src))

    order = ["a_norm", "a_w_in", "a_conv_w", "a_conv_b", "a_ln_g", "a_ln_b", "a_w_out", "kv_norm", "w_kv",
             "b_norm", "b_w_in", "b_w_out", "rel_bias", "final_norm"]
    lead = {"a_w_in", "a_w_out", "b_w_in", "b_w_out"}

    def shaped(nm, val):
        return val[None] if nm in lead else val

    result = [loss, grad_x.reshape(1, S, D)]
    for table in (outs_g, outs_d, outs_m, outs_v):
        result.extend(shaped(nm, table[nm]) for nm in order)
    return tuple(result)
```

```python
import functools

import numpy as np
import jax
import jax.numpy as jnp
from jax import lax
from jax.experimental import pallas as pl
from jax.experimental.pallas import tpu as pltpu
from jax.experimental.pallas import tpu_sc as plsc

F32 = jnp.float32
BF16 = jnp.bfloat16
MESH = pl.DeviceIdType.MESH
ANY = pl.BlockSpec(memory_space=pl.ANY)
VMEM_SPEC = pl.BlockSpec(memory_space=pltpu.VMEM)

EPS = 1e-6
HEAD_DIM = 128
BLOCK = 128
GROUPS = ((128, 1), (512, 4), (2048, 16))
SCALE = HEAD_DIM ** -0.5
CONV_TAPS = 31
HALO = 32
N_BUCKETS = 32
MAX_EXACT = 16
MAX_DISTANCE = 2048
NEG = -1e30
PRODUCTS_AHEAD = 2
SCORES_AHEAD = 4
N_CHIPS = 4
N_DEV = 8
LANES = 128
VMEM_LIMIT = 56 * 1024 * 1024

ADAM_LR = 0.001
ADAM_B1 = 0.9
ADAM_B2 = 0.999
ADAM_EPS = 1e-08
ADAM_WD = 0.01
ADAM_STEP = 10


def _tile(n, pref, mult=LANES):
    t = (min(pref, n) // mult) * mult
    while t >= mult:
        if n % t == 0:
            return t
        t -= mult
    return n


def _params(*sem):
    return pltpu.CompilerParams(dimension_semantics=sem, vmem_limit_bytes=VMEM_LIMIT)


def _sigmoid(v):
    return 1.0 / (1.0 + jnp.exp(-v))


def _dot(a, b, dims):
    return lax.dot_general(a, b, (dims, ((), ())), preferred_element_type=F32)


NN = ((1,), (0,))
NT = ((1,), (1,))
TN = ((0,), (0,))


def _as_list(after):
    if after is None:
        return []
    return list(after) if isinstance(after, (list, tuple)) else [after]


def _stack_rows(rows, total):
    width = rows[0].shape[1]
    rid = lax.broadcasted_iota(jnp.int32, (total, width), 0)
    out = jnp.zeros((total, width), F32)
    for q, row in enumerate(rows):
        out = jnp.where(rid == q, jnp.broadcast_to(row, (total, width)), out)
    return out


def _lane_col(arr, h, lane):
    return jnp.sum(jnp.where(lane == h, arr, 0.0), axis=-1, keepdims=True)


def _matmul(a, b, mode, out_dtype, name, res=None, a_slab=False, b_slab=False, out_slab=0,
            b_off=0, n_cols=None, out_off=0, out_cols=None, out_alias=None, after=None,
            tm=1024, tn=1024, tk=2048):
    if a_slab:
        na, M, W = a.shape
        K = na * W
    elif mode == "tn":
        K, M = a.shape
    else:
        M, K = a.shape
    if b_slab:
        nbs, _, Wb = b.shape
        N = nbs * Wb
    elif mode == "nt":
        N = b.shape[0]
    else:
        N = n_cols if n_cols else b.shape[1]
    tm = _tile(M, tm)
    tn = _tile(Wb if b_slab else (out_slab if out_slab else N), tn)
    tk = _tile(W if a_slab else K, tk)
    all_slabs = a_slab and mode == "nt" and tk == W
    if all_slabs:
        tk = K
    nk = K // tk
    grid = (M // tm, N // tn, nk)
    bo = b_off // (tk if mode == "nt" else tn)
    oo = out_off // tn

    if all_slabs:
        a_spec = pl.BlockSpec((na, tm, W), lambda i, j, k: (0, i, 0))
    elif a_slab:
        per = W // tk
        a_spec = pl.BlockSpec((None, tm, tk), lambda i, j, k: (k // per, i, k % per))
    elif mode == "tn":
        a_spec = pl.BlockSpec((tk, tm), lambda i, j, k: (k, i))
    else:
        a_spec = pl.BlockSpec((tm, tk), lambda i, j, k: (i, k))
    if b_slab:
        perb = Wb // tn
        b_spec = pl.BlockSpec((None, tk, tn), lambda i, j, k: (j // perb, k, j % perb))
    elif mode == "nt":
        b_spec = pl.BlockSpec((tn, tk), lambda i, j, k: (j, k + bo))
    else:
        b_spec = pl.BlockSpec((tk, tn), lambda i, j, k: (k, j + bo))
    if out_slab:
        pero = out_slab // tn
        o_spec = pl.BlockSpec((None, tm, tn), lambda i, j, k: (j // pero, i, j % pero))
        out_shape = jax.ShapeDtypeStruct((N // out_slab, M, out_slab), out_dtype)
    else:
        o_spec = pl.BlockSpec((tm, tn), lambda i, j, k: (i, j + oo))
        out_shape = jax.ShapeDtypeStruct((M, out_cols if out_cols else N), out_dtype)
    in_specs = [a_spec, b_spec]
    operands = [a, b]
    if res is not None:
        in_specs.append(pl.BlockSpec((tm, tn), lambda i, j, k: (i, j)))
        operands.append(res)
    aliases = {}
    if out_alias is not None:
        aliases[len(operands)] = 0
        in_specs.append(ANY)
        operands.append(out_alias)
    for arr in _as_list(after):
        in_specs.append(ANY)
        operands.append(arr)
    dims = {"nn": NN, "nt": NT, "tn": TN}[mode]
    has_res = res is not None
    n_in = len(operands)

    def body(*refs):
        a_ref, b_ref = refs[0], refs[1]
        r_ref = refs[2] if has_res else None
        o_ref = refs[n_in]
        if all_slabs:
            prod = _dot(a_ref[0], b_ref[:, 0:W], dims)
            for q in range(1, na):
                prod = prod + _dot(a_ref[q], b_ref[:, q * W:(q + 1) * W], dims)
        else:
            prod = _dot(a_ref[...], b_ref[...], dims)

        def finish(val):
            if has_res:
                val = val + r_ref[...]
            o_ref[...] = val.astype(out_dtype)

        if nk == 1:
            finish(prod)
        else:
            acc_ref = refs[n_in + 1]
            k = pl.program_id(2)

            @pl.when(k == 0)
            def _():
                acc_ref[...] = prod

            @pl.when(k > 0)
            def _():
                acc_ref[...] += prod

            @pl.when(k == nk - 1)
            def _():
                finish(acc_ref[...])

    scratch = [pltpu.VMEM((tm, tn), F32)] if nk > 1 else []
    return pl.pallas_call(
        body, name=name, grid=grid, in_specs=in_specs, out_specs=o_spec, out_shape=out_shape,
        scratch_shapes=scratch, input_output_aliases=aliases,
        compiler_params=_params("parallel", "parallel", "arbitrary"),
    )(*operands)


def _group_spec(d, ts, width):
    if d == 1:
        return pl.BlockSpec((ts, width), lambda i: (i, 0))
    return pl.BlockSpec((d, ts // d, width), lambda i: (0, i, 0))


def _group_shape(d, S, width, dtype):
    return jax.ShapeDtypeStruct((S, width) if d == 1 else (d, S // d, width), dtype)


def _chunk_buf(ts, width):
    return pltpu.VMEM((width // LANES, ts, LANES), F32)


def _fill_chunks(buf, val):
    for c in range(buf.shape[0]):
        buf[c] = val[:, c * LANES:(c + 1) * LANES]


def _read_chunks(buf):
    return jnp.concatenate([buf[c] for c in range(buf.shape[0])], axis=1)


def _emit_group_order(o_ref, buf, d, dtype):
    n = buf.shape[1] // d
    for r in range(d):
        for c in range(buf.shape[0]):
            o_ref[r, :, c * LANES:(c + 1) * LANES] = buf[c, pl.ds(r, n, stride=d), :].astype(dtype)


def _store_token_order(buf, i_ref, d):
    n = buf.shape[1] // d
    for r in range(d):
        for c in range(buf.shape[0]):
            buf[c, pl.ds(r, n, stride=d), :] = i_ref[r, :, c * LANES:(c + 1) * LANES].astype(F32)


def _rms_fwd(x, gains, dils, name, ts=256):
    S, D = x.shape
    ts = _tile(S, ts, 16 * max(dils))
    n = len(gains)
    nd = len(dils)

    def body(*refs):
        buf = refs[1 + n + n * nd]
        xv = refs[0][...]
        nrm = xv * lax.rsqrt(jnp.mean(xv * xv, axis=-1, keepdims=True) + EPS)
        for q in range(n):
            val = nrm * refs[1 + q][...]
            if max(dils) > 1:
                _fill_chunks(buf, val)
            for e, d in enumerate(dils):
                if d == 1:
                    refs[1 + n + q * nd + e][...] = val.astype(BF16)
                else:
                    _emit_group_order(refs[1 + n + q * nd + e], buf, d, BF16)

    row = pl.BlockSpec((ts, D), lambda i: (i, 0))
    vec = pl.BlockSpec((1, D), lambda i: (0, 0))
    outs = pl.pallas_call(
        body, name=name, grid=(S // ts,), in_specs=[row] + [vec] * n,
        out_specs=[_group_spec(d, ts, D) for _ in range(n) for d in dils],
        out_shape=[_group_shape(d, S, D, BF16) for _ in range(n) for d in dils],
        scratch_shapes=[_chunk_buf(ts, D)],
        compiler_params=_params("parallel"),
    )(x, *gains)
    return [[outs[q * nd + e].reshape(S, D) for e in range(nd)] for q in range(n)]


def _rms_bwd(x, cots, gains, dres, name, after=None, ts=256):
    S, D = x.shape
    ts = _tile(S, ts, 16 * max(d for _, _, d in cots))
    n = len(cots)
    ng = len(gains)
    extra = _as_list(after)
    n_in = 2 + n + ng + len(extra)

    def body(*refs):
        x_ref = refs[0]
        dh_refs = refs[1:1 + n]
        g_refs = refs[1 + n:1 + n + ng]
        dres_ref = refs[1 + n + ng]
        dx_ref, dxb_ref, gg_ref, buf = refs[n_in:n_in + 4]
        i = pl.program_id(0)
        xv = x_ref[...]
        r = lax.rsqrt(jnp.mean(xv * xv, axis=-1, keepdims=True) + EPS)
        nrm = xv * r
        dn = jnp.zeros_like(xv)
        rows = [jnp.zeros((1, D), F32) for _ in range(ng)]
        for q, (_, gi, d) in enumerate(cots):
            if d == 1:
                dh = dh_refs[q][...].astype(F32)
            else:
                _store_token_order(buf, dh_refs[q], d)
                dh = _read_chunks(buf)
            dn = dn + dh * g_refs[gi][...]
            rows[gi] = rows[gi] + jnp.sum(dh * nrm, axis=0, keepdims=True)
        dx = dres_ref[...] + r * (dn - nrm * jnp.mean(dn * nrm, axis=-1, keepdims=True))
        dx_ref[...] = dx
        dxb_ref[...] = dx.astype(BF16)
        upd = _stack_rows(rows, 8)

        @pl.when(i == 0)
        def _():
            gg_ref[...] = upd

        @pl.when(i > 0)
        def _():
            gg_ref[...] += upd

    row = pl.BlockSpec((ts, D), lambda i: (i, 0))
    vec = pl.BlockSpec((1, D), lambda i: (0, 0))
    acc = pl.BlockSpec((8, D), lambda i: (0, 0))
    return pl.pallas_call(
        body, name=name, grid=(S // ts,),
        in_specs=[row] + [_group_spec(d, ts, D) for _, _, d in cots] + [vec] * ng + [row] + [ANY] * len(extra),
        out_specs=[row, row, acc],
        out_shape=[jax.ShapeDtypeStruct((S, D), F32), jax.ShapeDtypeStruct((S, D), BF16),
                   jax.ShapeDtypeStruct((8, D), F32)],
        scratch_shapes=[_chunk_buf(ts, D)],
        compiler_params=_params("arbitrary"),
    )(x, *[a if d == 1 else a.reshape(d, S // d, D) for a, _, d in cots], *gains, dres, *extra)


def _final_head(x2, gain, target, name, ts=256):
    S, D = x2.shape
    ts = _tile(S, ts, 16)

    def body(x_ref, g_ref, t_ref, loss_ref, dx_ref, dxb_ref, gg_ref):
        i = pl.program_id(0)
        xv = x_ref[...]
        g = g_ref[...]
        r = lax.rsqrt(jnp.mean(xv * xv, axis=-1, keepdims=True) + EPS)
        nrm = xv * r
        err = nrm * g - t_ref[...]
        part = 0.5 * jnp.sum(jnp.mean(err * err, axis=-1, keepdims=True), axis=0, keepdims=True)
        dout = err * (1.0 / D)
        dn = dout * g
        dx = r * (dn - nrm * jnp.mean(dn * nrm, axis=-1, keepdims=True))
        dx_ref[...] = dx
        dxb_ref[...] = dx.astype(BF16)
        upd = _stack_rows([jnp.sum(dout * nrm, axis=0, keepdims=True)], 8)
        lpart = jnp.broadcast_to(part, (1, LANES))

        @pl.when(i == 0)
        def _():
            gg_ref[...] = upd
            loss_ref[...] = lpart

        @pl.when(i > 0)
        def _():
            gg_ref[...] += upd
            loss_ref[...] += lpart

    row = pl.BlockSpec((ts, D), lambda i: (i, 0))
    vec = pl.BlockSpec((1, D), lambda i: (0, 0))
    return pl.pallas_call(
        body, name=name, grid=(S // ts,), in_specs=[row, vec, row],
        out_specs=[pl.BlockSpec((1, LANES), lambda i: (0, 0)), row, row, pl.BlockSpec((8, D), lambda i: (0, 0))],
        out_shape=[jax.ShapeDtypeStruct((1, LANES), F32), jax.ShapeDtypeStruct((S, D), F32),
                   jax.ShapeDtypeStruct((S, D), BF16), jax.ShapeDtypeStruct((8, D), F32)],
        compiler_params=_params("arbitrary"),
    )(x2, gain, target)


CONV_ROWS = 64


SUBLANES = 8


def _shifted_buf(ts, cw):
    return pltpu.VMEM((SUBLANES - 1, ts + HALO - SUBLANES, cw), F32)


def _fill_shifted(shifted, buf):
    rows = shifted.shape[1]
    for s in range(1, SUBLANES):
        shifted[s - 1] = buf[s:s + rows, :]


def _window(buf, shifted, off, rows):
    s = off % SUBLANES
    base = off - s
    if s == 0:
        return buf[base:base + rows, :]
    return shifted[s - 1, base:base + rows, :]


def _conv_fwd(proj3, conv_w32, conv_b, name, ts=256, cw=256):
    _, S, E = proj3.shape
    ts = _tile(S, ts, HALO)
    cw = _tile(E, cw)
    per = ts // HALO
    rc = min(CONV_ROWS, ts)

    def body(a_ref, b_ref, ap_ref, bp_ref, w_ref, cb_ref, c_ref, ubuf, shifted):
        i = pl.program_id(0)
        up = ap_ref[...].astype(F32) * _sigmoid(bp_ref[...].astype(F32))
        ubuf[0:HALO, :] = jnp.where(i > 0, up, 0.0)
        ubuf[HALO:HALO + ts, :] = a_ref[...].astype(F32) * _sigmoid(b_ref[...].astype(F32))
        _fill_shifted(shifted, ubuf)
        for r0 in range(0, ts, rc):
            acc = jnp.broadcast_to(cb_ref[...], (rc, cw))
            for k in range(CONV_TAPS):
                off = r0 + HALO - (CONV_TAPS - 1) + k
                acc = acc + _window(ubuf, shifted, off, rc) * w_ref[k:k + 1, :]
            c_ref[r0:r0 + rc, :] = acc

    return pl.pallas_call(
        body, name=name, grid=(S // ts, E // cw),
        in_specs=[
            pl.BlockSpec((None, ts, cw), lambda i, j: (0, i, j)),
            pl.BlockSpec((None, ts, cw), lambda i, j: (1, i, j)),
            pl.BlockSpec((None, HALO, cw), lambda i, j: (0, jnp.maximum(i * per - 1, 0), j)),
            pl.BlockSpec((None, HALO, cw), lambda i, j: (1, jnp.maximum(i * per - 1, 0), j)),
            pl.BlockSpec((HALO, cw), lambda i, j: (0, j)),
            pl.BlockSpec((1, cw), lambda i, j: (0, j)),
        ],
        out_specs=pl.BlockSpec((ts, cw), lambda i, j: (i, j)),
        out_shape=jax.ShapeDtypeStruct((S, E), F32),
        scratch_shapes=[pltpu.VMEM((HALO + ts, cw), F32), _shifted_buf(ts, cw)],
        compiler_params=_params("parallel", "parallel"),
    )(proj3, proj3, proj3, proj3, conv_w32, conv_b)


def _ln_gate_fwd(c, proj3, ln_g, ln_b, name, ts=256):
    S, E = c.shape
    ts = _tile(S, ts, 16)

    def body(c_ref, z_ref, g_ref, b_ref, y_ref):
        cv = c_ref[...]
        mu = jnp.mean(cv, axis=-1, keepdims=True)
        d = cv - mu
        var = jnp.mean(d * d, axis=-1, keepdims=True)
        cn = d * lax.rsqrt(var + EPS) * g_ref[...] + b_ref[...]
        z = z_ref[...].astype(F32)
        y_ref[...] = ((cn * _sigmoid(cn)).astype(F32) * (z * _sigmoid(z))).astype(BF16)

    row = pl.BlockSpec((ts, E), lambda i: (i, 0))
    vec = pl.BlockSpec((1, E), lambda i: (0, 0))
    return pl.pallas_call(
        body, name=name, grid=(S // ts,),
        in_specs=[row, pl.BlockSpec((None, ts, E), lambda i: (2, i, 0)), vec, vec],
        out_specs=row, out_shape=jax.ShapeDtypeStruct((S, E), BF16),
        compiler_params=_params("parallel"),
    )(c, proj3, ln_g, ln_b)


def _ln_gate_bwd(c, proj3, dy, ln_g, ln_b, name, ts=256):
    S, E = c.shape
    ts = _tile(S, ts, 16)

    def body(c_ref, z_ref, dy_ref, g_ref, b_ref, dc_ref, dz_ref, acc_ref):
        i = pl.program_id(0)
        cv = c_ref[...]
        g = g_ref[...]
        mu = jnp.mean(cv, axis=-1, keepdims=True)
        d = cv - mu
        var = jnp.mean(d * d, axis=-1, keepdims=True)
        rstd = lax.rsqrt(var + EPS)
        chat = d * rstd
        cn = chat * g + b_ref[...]
        z = z_ref[...].astype(F32)
        dyv = dy_ref[...].astype(F32)
        sc = _sigmoid(cn)
        sz = _sigmoid(z)
        dcn = dyv * (z * sz) * (sc * (1.0 + cn * (1.0 - sc)))
        dz_ref[...] = (dyv * (cn * sc) * (sz * (1.0 + z * (1.0 - sz)))).astype(BF16)
        dchat = dcn * g
        dcv = rstd * (dchat - jnp.mean(dchat, axis=-1, keepdims=True)
                      - chat * jnp.mean(dchat * chat, axis=-1, keepdims=True))
        dc_ref[...] = dcv
        upd = _stack_rows([jnp.sum(dcn * chat, axis=0, keepdims=True),
                           jnp.sum(dcn, axis=0, keepdims=True),
                           jnp.sum(dcv, axis=0, keepdims=True)], 8)

        @pl.when(i == 0)
        def _():
            acc_ref[...] = upd

        @pl.when(i > 0)
        def _():
            acc_ref[...] += upd

    row = pl.BlockSpec((ts, E), lambda i: (i, 0))
    vec = pl.BlockSpec((1, E), lambda i: (0, 0))
    return pl.pallas_call(
        body, name=name, grid=(S // ts,),
        in_specs=[row, pl.BlockSpec((None, ts, E), lambda i: (2, i, 0)), row, vec, vec],
        out_specs=[row, pl.BlockSpec((None, ts, E), lambda i: (2, i, 0)), pl.BlockSpec((8, E), lambda i: (0, 0))],
        out_shape=[jax.ShapeDtypeStruct((S, E), F32), jax.ShapeDtypeStruct((3, S, E), BF16),
                   jax.ShapeDtypeStruct((8, E), F32)],
        compiler_params=_params("arbitrary"),
    )(c, proj3, dy, ln_g, ln_b)


def _conv_bwd(proj3, dc, conv_w32, dproj3, name, ts=256, cw=256):
    _, S, E = proj3.shape
    ts = _tile(S, ts, HALO)
    cw = _tile(E, cw)
    per = ts // HALO
    n_i = S // ts
    last_halo = S // HALO - 1
    rc = min(CONV_ROWS, ts)

    def body(a_ref, b_ref, dc_ref, dcn_ref, w_ref, dp_in, dab_ref, dw_ref, dcbuf, ubuf, dwacc, shifted):
        del dp_in
        i = pl.program_id(1)
        dcbuf[0:ts, :] = dc_ref[...]
        dcbuf[ts:ts + HALO, :] = jnp.where(i < n_i - 1, dcn_ref[...], 0.0)
        _fill_shifted(shifted, dcbuf)
        av = a_ref[...].astype(F32)
        sb = _sigmoid(b_ref[...].astype(F32))
        ubuf[...] = av * sb

        @pl.when(i == 0)
        def _():
            dwacc[...] = jnp.zeros_like(dwacc)

        for r0 in range(0, ts, rc):
            uv = ubuf[r0:r0 + rc, :]
            du = jnp.zeros((rc, cw), F32)
            for d in range(CONV_TAPS):
                k = CONV_TAPS - 1 - d
                win = _window(dcbuf, shifted, r0 + d, rc)
                du = du + win * w_ref[k:k + 1, :]
                dwacc[k:k + 1, :] += jnp.sum(uv * win, axis=0, keepdims=True)
            a_c = a_ref[r0:r0 + rc, :].astype(F32)
            s_c = _sigmoid(b_ref[r0:r0 + rc, :].astype(F32))
            dab_ref[0, r0:r0 + rc, :] = (du * s_c).astype(BF16)
            dab_ref[1, r0:r0 + rc, :] = (du * a_c * s_c * (1.0 - s_c)).astype(BF16)

        @pl.when(i == n_i - 1)
        def _():
            dw_ref[...] = dwacc[...]

    return pl.pallas_call(
        body, name=name, grid=(E // cw, n_i),
        in_specs=[
            pl.BlockSpec((None, ts, cw), lambda j, i: (0, i, j)),
            pl.BlockSpec((None, ts, cw), lambda j, i: (1, i, j)),
            pl.BlockSpec((ts, cw), lambda j, i: (i, j)),
            pl.BlockSpec((HALO, cw), lambda j, i: (jnp.minimum((i + 1) * per, last_halo), j)),
            pl.BlockSpec((HALO, cw), lambda j, i: (0, j)),
            ANY,
        ],
        out_specs=[pl.BlockSpec((2, ts, cw), lambda j, i: (0, i, j)),
                   pl.BlockSpec((HALO, cw), lambda j, i: (0, j))],
        out_shape=[jax.ShapeDtypeStruct((3, S, E), BF16), jax.ShapeDtypeStruct((HALO, E), F32)],
        scratch_shapes=[pltpu.VMEM((ts + HALO, cw), F32), pltpu.VMEM((ts, cw), F32), pltpu.VMEM((HALO, cw), F32),
                        _shifted_buf(ts, cw)],
        input_output_aliases={5: 0},
        compiler_params=_params("parallel", "arbitrary"),
    )(proj3, proj3, dc, dc, conv_w32, dproj3)


def _bucket_table(dil):
    delta = (np.arange(BLOCK)[:, None] + BLOCK) - np.arange(2 * BLOCK)[None, :]
    dist = np.clip(delta, 0, None) * dil
    large = MAX_EXACT + (np.log(np.maximum(dist, 1).astype(np.float32) / MAX_EXACT)
                         / np.log(MAX_DISTANCE / MAX_EXACT) * (N_BUCKETS - MAX_EXACT)).astype(np.int32)
    large = np.minimum(large, N_BUCKETS - 1)
    return np.where(dist < MAX_EXACT, dist, large).astype(np.int32).reshape(-1)


def _onehot(dil):
    tbl = jnp.asarray(_bucket_table(dil))
    return (tbl[None, :] == jnp.arange(LANES, dtype=jnp.int32)[:, None]).astype(BF16)


def _split3(v):
    hi = v.astype(BF16)
    r1 = v - hi.astype(F32)
    mid = r1.astype(BF16)
    lo = (r1 - mid.astype(F32)).astype(BF16)
    return hi, mid, lo


def _bias_table(rb_t, onehot, name):
    H = rb_t.shape[0]
    N = onehot.shape[1]

    def body(r_ref, oh_ref, o_ref):
        oh = oh_ref[...]
        hi, mid, lo = _split3(r_ref[...])
        o_ref[...] = (_dot(lo, oh, NN) + _dot(mid, oh, NN)) + _dot(hi, oh, NN)

    return pl.pallas_call(
        body, name=name, in_specs=[VMEM_SPEC, VMEM_SPEC], out_specs=VMEM_SPEC,
        out_shape=jax.ShapeDtypeStruct((H, N), F32),
        compiler_params=pltpu.CompilerParams(vmem_limit_bytes=VMEM_LIMIT),
    )(rb_t, onehot)


def _bias_grad(dbs, onehots, name):
    H = dbs[0].shape[0]
    n = len(dbs)

    def body(*refs):
        acc = jnp.zeros((H, LANES), F32)
        for q in range(n):
            oh = refs[n + q][...]
            hi, mid, lo = _split3(refs[q][...])
            acc = acc + ((_dot(lo, oh, NT) + _dot(mid, oh, NT)) + _dot(hi, oh, NT))
        refs[2 * n][...] = acc

    return pl.pallas_call(
        body, name=name, in_specs=[VMEM_SPEC] * (2 * n), out_specs=VMEM_SPEC,
        out_shape=jax.ShapeDtypeStruct((H, LANES), F32),
        compiler_params=pltpu.CompilerParams(vmem_limit_bytes=VMEM_LIMIT),
    )(*dbs, *onehots)


def _pair_mask(has_prev):
    qi = lax.broadcasted_iota(jnp.int32, (BLOCK, 2 * BLOCK), 0)
    ki = lax.broadcasted_iota(jnp.int32, (BLOCK, 2 * BLOCK), 1)
    prev = jnp.logical_and(jnp.logical_and(ki < BLOCK, ki >= qi), has_prev)
    return jnp.logical_or(prev, jnp.logical_and(ki >= BLOCK, ki - BLOCK <= qi))


def _attn_fwd(q, kv, bias, dil, name):
    S, A = q.shape
    H = A // HEAD_DIM
    L = S // dil
    nb = L // BLOCK
    qv = q.reshape(dil, L, A)
    kvv = kv.reshape(2, dil, L, A)

    def body(q_ref, kp_ref, kc_ref, vp_ref, vc_ref, b_ref, o_ref, lse_ref):
        i = pl.program_id(1)
        qi = lax.broadcasted_iota(jnp.int32, (BLOCK, BLOCK), 0)
        ki = lax.broadcasted_iota(jnp.int32, (BLOCK, BLOCK), 1)
        mask = _pair_mask(i > 0)
        lane = lax.broadcasted_iota(jnp.int32, (BLOCK, LANES), 1)
        lse_acc = jnp.zeros((BLOCK, LANES), F32)

        def scores(h):
            sl = slice(h * HEAD_DIM, (h + 1) * HEAD_DIM)
            return _dot(q_ref[:, sl], jnp.concatenate([kp_ref[:, sl], kc_ref[:, sl]], axis=0), NT)

        ahead = [scores(h) for h in range(min(SCORES_AHEAD, H))]
        for h in range(H):
            sl = slice(h * HEAD_DIM, (h + 1) * HEAD_DIM)
            raw = ahead.pop(0)
            if h + SCORES_AHEAD < H:
                ahead.append(scores(h + SCORES_AHEAD))
            s = jnp.where(mask, raw * SCALE + b_ref[h], NEG)
            m = jnp.max(s, axis=-1, keepdims=True)
            p = jnp.exp(s - m)
            den = jnp.sum(p, axis=-1, keepdims=True)
            acc = _dot(p.astype(BF16), jnp.concatenate([vp_ref[:, sl], vc_ref[:, sl]], axis=0), NN)
            o_ref[:, sl] = acc / den
            lse_acc = jnp.where(lane == h, m + jnp.log(den), lse_acc)
        lse_ref[...] = lse_acc

    def blk(slab, prev):
        if prev:
            return pl.BlockSpec((None, None, BLOCK, A), lambda r, i: (slab, r, jnp.maximum(i - 1, 0), 0))
        return pl.BlockSpec((None, None, BLOCK, A), lambda r, i: (slab, r, i, 0))

    o, lse = pl.pallas_call(
        body, name=name, grid=(dil, nb),
        in_specs=[pl.BlockSpec((None, BLOCK, A), lambda r, i: (r, i, 0)),
                  blk(0, True), blk(0, False), blk(1, True), blk(1, False),
                  pl.BlockSpec((H, BLOCK, 2 * BLOCK), lambda r, i: (0, 0, 0))],
        out_specs=[pl.BlockSpec((None, BLOCK, A), lambda r, i: (r, i, 0)),
                   pl.BlockSpec((None, BLOCK, LANES), lambda r, i: (r, i, 0))],
        out_shape=[jax.ShapeDtypeStruct((dil, L, A), F32), jax.ShapeDtypeStruct((dil, L, LANES), F32)],
        compiler_params=_params("parallel", "parallel"),
    )(qv, kvv, kvv, kvv, kvv, bias)
    return o.reshape(S, A), lse.reshape(S, LANES)


def _attn_merge(os_, lses, z, dils, name, ts=256):
    S, A = z.shape
    H = A // HEAD_DIM
    ts = _tile(S, ts, 16 * max(dils))
    n = len(os_)

    def body(*refs):
        z_ref = refs[2 * n]
        y_ref, om_ref = refs[2 * n + 1:2 * n + 3]
        lse_refs = refs[2 * n + 3:3 * n + 3]
        o_refs = refs[3 * n + 3:4 * n + 3]
        l_bufs = refs[4 * n + 3:5 * n + 3]
        lse_buf = refs[5 * n + 3]
        ls = []
        for q, d in enumerate(dils):
            if d == 1:
                ls.append(refs[n + q][...])
            else:
                _store_token_order(o_refs[q], refs[q], d)
                _store_token_order(l_bufs[q], refs[n + q], d)
                ls.append(l_bufs[q][0])
        m = ls[0]
        for q in range(1, n):
            m = jnp.maximum(m, ls[q])
        es = [jnp.exp(v - m) for v in ls]
        den = es[0]
        for q in range(1, n):
            den = den + es[q]
        alphas = [e / den for e in es]
        lse = m + jnp.log(den)
        lse_buf[0] = lse
        for q, d in enumerate(dils):
            if d == 1:
                lse_refs[q][...] = lse
            else:
                _emit_group_order(lse_refs[q], lse_buf, d, F32)
        lane = lax.broadcasted_iota(jnp.int32, (ts, LANES), 1)
        for h in range(H):
            sl = slice(h * HEAD_DIM, (h + 1) * HEAD_DIM)
            om = jnp.zeros((ts, HEAD_DIM), F32)
            for q, d in enumerate(dils):
                o_h = refs[q][:, sl] if d == 1 else o_refs[q][h]
                om = om + _lane_col(alphas[q], h, lane) * o_h
            z = z_ref[:, sl].astype(F32)
            y_ref[:, sl] = (om * (z * _sigmoid(z))).astype(BF16)
            om_ref[:, sl] = om.astype(BF16)

    row = pl.BlockSpec((ts, A), lambda i: (i, 0))
    outs = pl.pallas_call(
        body, name=name, grid=(S // ts,),
        in_specs=[_group_spec(d, ts, A) for d in dils] + [_group_spec(d, ts, LANES) for d in dils] + [row],
        out_specs=[row, row] + [_group_spec(d, ts, LANES) for d in dils],
        out_shape=[jax.ShapeDtypeStruct((S, A), BF16), jax.ShapeDtypeStruct((S, A), BF16)]
        + [_group_shape(d, S, LANES, F32) for d in dils],
        scratch_shapes=[_chunk_buf(ts, A)] * n + [_chunk_buf(ts, LANES)] * (n + 1),
        compiler_params=_params("parallel"),
    )(*[o if d == 1 else o.reshape(d, S // d, A) for o, d in zip(os_, dils)],
      *[v if d == 1 else v.reshape(d, S // d, LANES) for v, d in zip(lses, dils)], z)
    return outs[0], outs[1], [v.reshape(S, LANES) for v in outs[2:]]


def _gate_bwd(dy, om, z, dils, name, ts=256):
    S, A = dy.shape
    H = A // HEAD_DIM
    ts = _tile(S, ts, 16 * max(dils))
    n = len(dils)

    def body(*refs):
        dy_ref, om_ref, z_ref = refs[:3]
        do_refs = refs[3:3 + n]
        dh_refs = refs[3 + n:3 + 2 * n]
        dz_ref = refs[3 + 2 * n]
        do_buf, dh_buf = refs[4 + 2 * n:6 + 2 * n]
        lane = lax.broadcasted_iota(jnp.int32, (ts, LANES), 1)
        acc = jnp.zeros((ts, LANES), F32)
        for h in range(H):
            sl = slice(h * HEAD_DIM, (h + 1) * HEAD_DIM)
            dyv = dy_ref[:, sl].astype(F32)
            omv = om_ref[:, sl].astype(F32)
            zv = z_ref[:, sl].astype(F32)
            sz = _sigmoid(zv)
            dob = (dyv * (zv * sz)).astype(BF16)
            do_buf[h] = dob.astype(F32)
            for q, d in enumerate(dils):
                if d == 1:
                    do_refs[q][:, sl] = dob
            dz_ref[:, sl] = (dyv * omv * (sz * (1.0 + zv * (1.0 - sz)))).astype(BF16)
            acc = jnp.where(lane == h, jnp.sum(dob.astype(F32) * omv, axis=-1, keepdims=True), acc)
        dh_buf[0] = acc
        for q, d in enumerate(dils):
            if d == 1:
                dh_refs[q][...] = acc
            else:
                _emit_group_order(do_refs[q], do_buf, d, BF16)
                _emit_group_order(dh_refs[q], dh_buf, d, F32)

    row = pl.BlockSpec((ts, A), lambda i: (i, 0))
    outs = pl.pallas_call(
        body, name=name, grid=(S // ts,), in_specs=[row, row, row],
        out_specs=[_group_spec(d, ts, A) for d in dils] + [_group_spec(d, ts, LANES) for d in dils] + [row],
        out_shape=[_group_shape(d, S, A, BF16) for d in dils] + [_group_shape(d, S, LANES, F32) for d in dils]
        + [jax.ShapeDtypeStruct((S, A), BF16)],
        scratch_shapes=[_chunk_buf(ts, A), _chunk_buf(ts, LANES)],
        compiler_params=_params("parallel"),
    )(dy, om, z)
    return ([v.reshape(S, A) for v in outs[:n]], [v.reshape(S, LANES) for v in outs[n:2 * n]], outs[2 * n])


def _attn_bwd(q, kv, do, lse, dh, bias, dil, name):
    S, A = q.shape
    H = A // HEAD_DIM
    L = S // dil
    nb = L // BLOCK
    qv = q.reshape(dil, L, A)
    kvv = kv.reshape(2, dil, L, A)
    dov = do.reshape(dil, L, A)
    lsev = lse.reshape(dil, L, LANES)
    dhv = dh.reshape(dil, L, LANES)

    def body(*refs):
        (q_ref, qn_ref, kp_ref, kc_ref, vp_ref, vc_ref, do_ref, don_ref, l_ref, ln_ref, d_ref, dn_ref,
         b_ref) = refs[:13]
        dq_ref, dkv_ref, db_ref = refs[13:16]
        r = pl.program_id(0)
        i = pl.program_id(1)
        qi = lax.broadcasted_iota(jnp.int32, (BLOCK, BLOCK), 0)
        ki = lax.broadcasted_iota(jnp.int32, (BLOCK, BLOCK), 1)
        mask_c = ki <= qi
        band = ki >= qi
        mask_p = jnp.logical_and(band, i > 0)
        mask_n = jnp.logical_and(band, i < nb - 1)
        lane = lax.broadcasted_iota(jnp.int32, (BLOCK, LANES), 1)

        @pl.when(jnp.logical_and(r == 0, i == 0))
        def _():
            db_ref[...] = jnp.zeros_like(db_ref)

        def products(h):
            sl = slice(h * HEAD_DIM, (h + 1) * HEAD_DIM)
            q_i, q_n = q_ref[:, sl], qn_ref[:, sl]
            k_p, k_c = kp_ref[:, sl], kc_ref[:, sl]
            v_p, v_c = vp_ref[:, sl], vc_ref[:, sl]
            do_i, do_n = do_ref[:, sl], don_ref[:, sl]
            return (_dot(q_i, k_c, NT), _dot(do_i, v_c, NT), _dot(q_i, k_p, NT), _dot(do_i, v_p, NT),
                    _dot(q_n, k_c, NT), _dot(do_n, v_c, NT))

        ahead = [products(h) for h in range(min(PRODUCTS_AHEAD, H))]
        for h in range(H):
            sl = slice(h * HEAD_DIM, (h + 1) * HEAD_DIM)
            s1, dp1, s2, dp2, s3, dp3 = ahead.pop(0)
            if h + PRODUCTS_AHEAD < H:
                ahead.append(products(h + PRODUCTS_AHEAD))
            q_i, q_n = q_ref[:, sl], qn_ref[:, sl]
            k_p, k_c = kp_ref[:, sl], kc_ref[:, sl]
            do_i, do_n = do_ref[:, sl], don_ref[:, sl]
            l_i, l_n = _lane_col(l_ref[...], h, lane), _lane_col(ln_ref[...], h, lane)
            d_i, d_n = _lane_col(d_ref[...], h, lane), _lane_col(dn_ref[...], h, lane)
            b_c = b_ref[h, :, BLOCK:]
            b_p = b_ref[h, :, :BLOCK]
            p1 = jnp.exp(jnp.where(mask_c, s1 * SCALE + b_c, NEG) - l_i)
            ds1 = p1 * (dp1 - d_i)
            ds1b = ds1.astype(BF16)
            p1b = p1.astype(BF16)
            p2 = jnp.exp(jnp.where(mask_p, s2 * SCALE + b_p, NEG) - l_i)
            ds2 = p2 * (dp2 - d_i)
            ds2b = ds2.astype(BF16)
            p3 = jnp.exp(jnp.where(mask_n, s3 * SCALE + b_p, NEG) - l_n)
            ds3b = (p3 * (dp3 - d_n)).astype(BF16)
            p3b = p3.astype(BF16)
            dq = _dot(ds1b, k_c, NN) + _dot(ds2b, k_p, NN)
            dk = _dot(ds1b, q_i, TN) + _dot(ds3b, q_n, TN)
            dv = _dot(p1b, do_i, TN) + _dot(p3b, do_n, TN)
            dq_ref[:, sl] = (dq * SCALE).astype(BF16)
            dkv_ref[0, :, sl] = (dk * SCALE).astype(BF16)
            dkv_ref[1, :, sl] = dv.astype(BF16)
            db_ref[h, :, BLOCK:] += ds1
            db_ref[h, :, :BLOCK] += ds2

    def blk(slab, shift):
        if shift < 0:
            return pl.BlockSpec((None, None, BLOCK, A), lambda r, i: (slab, r, jnp.maximum(i - 1, 0), 0))
        return pl.BlockSpec((None, None, BLOCK, A), lambda r, i: (slab, r, i, 0))

    def row(width, shift):
        if shift > 0:
            return pl.BlockSpec((None, BLOCK, width), lambda r, i: (r, jnp.minimum(i + 1, nb - 1), 0))
        return pl.BlockSpec((None, BLOCK, width), lambda r, i: (r, i, 0))

    in_specs = [row(A, 0), row(A, 1), blk(0, -1), blk(0, 0), blk(1, -1), blk(1, 0),
                row(A, 0), row(A, 1), row(LANES, 0), row(LANES, 1), row(LANES, 0), row(LANES, 1),
                pl.BlockSpec((H, BLOCK, 2 * BLOCK), lambda r, i: (0, 0, 0))]
    dq, dkv, db = pl.pallas_call(
        body, name=name, grid=(dil, nb), in_specs=in_specs,
        out_specs=[pl.BlockSpec((None, BLOCK, A), lambda r, i: (r, i, 0)),
                   pl.BlockSpec((2, None, BLOCK, A), lambda r, i: (0, r, i, 0)),
                   pl.BlockSpec((H, BLOCK, 2 * BLOCK), lambda r, i: (0, 0, 0))],
        out_shape=[jax.ShapeDtypeStruct((dil, L, A), BF16), jax.ShapeDtypeStruct((2, dil, L, A), BF16),
                   jax.ShapeDtypeStruct((H, BLOCK, 2 * BLOCK), F32)],
        compiler_params=_params("arbitrary", "arbitrary"),
    )(qv, qv, kvv, kvv, kvv, kvv, dov, dov, lsev, lsev, dhv, dhv, bias)
    return dq.reshape(S, A), dkv.reshape(2, S, A), db


def _sum_leading(stack, out_dtype, name, tr=256, tc=2048):
    n, R, C = stack.shape
    tr = _tile(R, tr, 16)
    tc = _tile(C, tc)

    def body(s_ref, o_ref):
        acc = s_ref[0].astype(F32)
        for q in range(1, n):
            acc = acc + s_ref[q].astype(F32)
        o_ref[...] = acc.astype(out_dtype)

    return pl.pallas_call(
        body, name=name, grid=(R // tr, C // tc),
        in_specs=[pl.BlockSpec((n, tr, tc), lambda i, j: (0, i, j))],
        out_specs=pl.BlockSpec((tr, tc), lambda i, j: (i, j)),
        out_shape=jax.ShapeDtypeStruct((R, C), out_dtype),
        compiler_params=_params("parallel", "parallel"),
    )(stack)


def _add_half(g, t, c_idx, kind, name, after=None, tr=256, tc=2048):
    R, C = t.shape
    tr = _tile(R, tr, 16)
    tc = _tile(C, tc)
    nrb, ncb = R // tr, C // tc
    extra = _as_list(after)

    def body(c_ref, g_ref, t_ref, *rest):
        del c_ref
        o_ref = rest[len(extra)]
        o_ref[...] = (g_ref[...].astype(F32) + t_ref[...].astype(F32)).astype(BF16)

    if kind == "col":
        g_map = lambda i, j, c_ref: (c_ref[0] * nrb + i, j)
    else:
        g_map = lambda i, j, c_ref: (i, c_ref[0] * ncb + j)
    same = lambda i, j, c_ref: (i, j)
    return pl.pallas_call(
        body, name=name,
        grid_spec=pltpu.PrefetchScalarGridSpec(
            num_scalar_prefetch=1, grid=(nrb, ncb),
            in_specs=[pl.BlockSpec((tr, tc), g_map), pl.BlockSpec((tr, tc), same)] + [ANY] * len(extra),
            out_specs=pl.BlockSpec((tr, tc), same)),
        out_shape=jax.ShapeDtypeStruct((R, C), BF16),
        compiler_params=_params("parallel", "parallel"),
    )(c_idx, g, t, *extra)


def _cast_into_full(w, kind, chip_idx, name, tr=256, tc=2048):
    R, C = w.shape
    tr = _tile(R, tr, 16)
    tc = _tile(C, tc)
    nrb, ncb = R // tr, C // tc

    def body(k_ref, w_ref, o_ref):
        del k_ref
        o_ref[...] = w_ref[...].astype(BF16)

    if kind == "col":
        o_map = lambda i, j, k_ref: (i, k_ref[0] * ncb + j)
        full = (R, N_CHIPS * C)
    else:
        o_map = lambda i, j, k_ref: (k_ref[0] * nrb + i, j)
        full = (N_CHIPS * R, C)
    return pl.pallas_call(
        body, name=name,
        grid_spec=pltpu.PrefetchScalarGridSpec(
            num_scalar_prefetch=1, grid=(nrb, ncb),
            in_specs=[pl.BlockSpec((tr, tc), lambda i, j, k_ref: (i, j))],
            out_specs=pl.BlockSpec((tr, tc), o_map)),
        out_shape=jax.ShapeDtypeStruct(full, BF16),
        compiler_params=_params("parallel", "parallel"),
    )(chip_idx, w)


def _sum_into_shard(p, u, idx, kind, name, tr=256, tc=2048):
    _, R, C = u.shape
    tr = _tile(R, tr, 16)
    tc = _tile(C, tc)
    nrb, ncb = R // tr, C // tc

    def body(i_ref, p_ref, u_ref, o_ref):
        del i_ref
        acc = p_ref[...].astype(F32)
        for q in range(N_CHIPS - 1):
            acc = acc + u_ref[q].astype(F32)
        o_ref[...] = acc

    if kind == "col":
        p_map = lambda i, j, r: (i, r[0] * ncb + j)
        o_map = lambda i, j, r: (r[1] * nrb + i, j)
        full = (2 * R, C)
    else:
        p_map = lambda i, j, r: (r[0] * nrb + i, j)
        o_map = lambda i, j, r: (i, r[1] * ncb + j)
        full = (R, 2 * C)
    return pl.pallas_call(
        body, name=name,
        grid_spec=pltpu.PrefetchScalarGridSpec(
            num_scalar_prefetch=1, grid=(nrb, ncb),
            in_specs=[pl.BlockSpec((tr, tc), p_map), pl.BlockSpec((N_CHIPS - 1, tr, tc), lambda i, j, r: (0, i, j))],
            out_specs=pl.BlockSpec((tr, tc), o_map)),
        out_shape=jax.ShapeDtypeStruct(full, F32),
        compiler_params=_params("parallel", "parallel"),
    )(idx, p, u)


def _adamw(w, g, m, v, name, emit_grad=False, tr=256, tc=2048):
    R, C = w.shape
    tr = _tile(R, tr, 8)
    tc = _tile(C, tc)
    c1 = 1.0 - ADAM_B1 ** ADAM_STEP
    c2 = 1.0 - ADAM_B2 ** ADAM_STEP
    n_out = 4 if emit_grad else 3

    def body(w_ref, g_ref, m_ref, v_ref, d_ref, nm_ref, nv_ref, *rest):
        gv = g_ref[...]
        if emit_grad:
            rest[0][...] = gv
        nm = ADAM_B1 * m_ref[...] + (1.0 - ADAM_B1) * gv
        nv = ADAM_B2 * v_ref[...] + (1.0 - ADAM_B2) * (gv * gv)
        d_ref[...] = -ADAM_LR * ((nm / c1) / (jnp.sqrt(nv / c2) + ADAM_EPS) + ADAM_WD * w_ref[...])
        nm_ref[...] = nm
        nv_ref[...] = nv

    blk = pl.BlockSpec((tr, tc), lambda i, j: (i, j))
    sh = jax.ShapeDtypeStruct((R, C), F32)
    return pl.pallas_call(
        body, name=name, grid=(R // tr, C // tc), in_specs=[blk] * 4, out_specs=[blk] * n_out,
        out_shape=[sh] * n_out, compiler_params=_params("parallel", "parallel"),
    )(w, g, m, v)


def _xyc():
    return lax.axis_index("x"), lax.axis_index("y"), lax.axis_index("c")


def _drain(copies):
    for cp in copies:
        if cp.is_remote:
            cp.wait_send()
        else:
            cp.wait()


def _other_chips(x, y):
    return [(1 - x, y), (x, 1 - y), (1 - x, 1 - y)]


def _allgather_small(blk, name, after=None):
    R, C = blk.shape
    extra = _as_list(after)

    def body(*refs):
        x_ref = refs[0]
        out_ref, send_sems, recv_sems, local_sem = refs[1 + len(extra):]
        x, y, c = _xyc()
        me = 4 * x + 2 * y + c
        mine = pltpu.make_async_copy(x_ref, out_ref.at[me], local_sem)
        mine.start()
        peers = []
        for k in range(1, N_DEV):
            px = 1 - x if (k >> 2) & 1 else x
            py = 1 - y if (k >> 1) & 1 else y
            pc = 1 - c if k & 1 else c
            peers.append((px, py, pc))
        sends = []
        for k, peer in enumerate(peers):
            cp = pltpu.make_async_remote_copy(
                src_ref=x_ref, dst_ref=out_ref.at[me], send_sem=send_sems.at[k], recv_sem=recv_sems.at[k],
                device_id=peer, device_id_type=MESH)
            cp.start()
            sends.append(cp)
        for k, (px, py, pc) in enumerate(peers):
            pltpu.make_async_remote_copy(
                src_ref=x_ref, dst_ref=out_ref.at[4 * px + 2 * py + pc], send_sem=send_sems.at[k],
                recv_sem=recv_sems.at[k], device_id=(px, py, pc), device_id_type=MESH).wait_recv()
        for cp in sends:
            cp.wait_send()
        mine.wait()

    return pl.pallas_call(
        body, name=name, in_specs=[VMEM_SPEC] + [ANY] * len(extra), out_specs=VMEM_SPEC,
        out_shape=jax.ShapeDtypeStruct((N_DEV, R, C), blk.dtype),
        scratch_shapes=[pltpu.SemaphoreType.DMA((N_DEV - 1,)), pltpu.SemaphoreType.DMA((N_DEV - 1,)),
                        pltpu.SemaphoreType.DMA],
        compiler_params=pltpu.CompilerParams(vmem_limit_bytes=VMEM_LIMIT),
    )(blk, *extra)


def _full_region(ref, kind, chip, half, shard_shape):
    r, cn = shard_shape
    hr = r // 2
    if kind == "col":
        rows = pl.ds(0, r) if half is None else pl.ds(pl.multiple_of(half * hr, 16), hr)
        return ref.at[rows, pl.ds(pl.multiple_of(chip * cn, LANES), cn)]
    if half is None:
        return ref.at[pl.ds(pl.multiple_of(chip * r, 16), r), :]
    return ref.at[pl.ds(pl.multiple_of(chip * r + half * hr, 16), hr), :]


def _allgather_weights(fulls, kinds, shapes, name):
    n = len(fulls)

    def body(*refs):
        outs = refs[n:2 * n]
        send_sems, recv_sems = refs[2 * n:]
        x, y, c = _xyc()
        chip = 2 * x + y
        sib = (x, y, 1 - c)
        others = _other_chips(x, y)
        started = []
        for w in range(n):
            mine = _full_region(outs[w], kinds[w], chip, c, shapes[w])
            for j, (ox, oy) in enumerate(others):
                cp = pltpu.make_async_remote_copy(
                    src_ref=mine, dst_ref=mine, send_sem=send_sems.at[6 * w + j], recv_sem=recv_sems.at[6 * w + j],
                    device_id=(ox, oy, c), device_id_type=MESH)
                cp.start()
                started.append(cp)
        for w in range(n):
            for j, (ox, oy) in enumerate(others):
                landed = _full_region(outs[w], kinds[w], 2 * ox + oy, c, shapes[w])
                pltpu.make_async_remote_copy(
                    src_ref=landed, dst_ref=landed, send_sem=send_sems.at[6 * w + j], recv_sem=recv_sems.at[6 * w + j],
                    device_id=(ox, oy, c), device_id_type=MESH).wait_recv()
                cp = pltpu.make_async_remote_copy(
                    src_ref=landed, dst_ref=landed, send_sem=send_sems.at[6 * w + 3 + j],
                    recv_sem=recv_sems.at[6 * w + 3 + j], device_id=sib, device_id_type=MESH)
                cp.start()
                started.append(cp)
        for w in range(n):
            for j, (ox, oy) in enumerate(others):
                theirs = _full_region(outs[w], kinds[w], 2 * ox + oy, 1 - c, shapes[w])
                pltpu.make_async_remote_copy(
                    src_ref=theirs, dst_ref=theirs, send_sem=send_sems.at[6 * w + 3 + j],
                    recv_sem=recv_sems.at[6 * w + 3 + j], device_id=sib, device_id_type=MESH).wait_recv()
        _drain(started)

    return pl.pallas_call(
        body, name=name, in_specs=[ANY] * n, out_specs=[ANY] * n,
        out_shape=[jax.ShapeDtypeStruct(f.shape, f.dtype) for f in fulls],
        input_output_aliases={w: w for w in range(n)},
        scratch_shapes=[pltpu.SemaphoreType.DMA((6 * n,)), pltpu.SemaphoreType.DMA((6 * n,))],
    )(*fulls)


def _region_of_size(ref, kind, shard_shape, count):
    r, cn = shard_shape
    if kind == "col":
        return ref.at[pl.ds(0, r // 2), pl.ds(0, count * cn)]
    return ref.at[pl.ds(0, count * (r // 2)), :]


def _allgather_weights_seq(fulls, kinds, shapes, name, collective_id):
    n = len(fulls)
    refs = [jax.new_ref(f, memory_space=pltpu.MemorySpace.HBM) for f in fulls]

    def body(send_sems, recv_sems):
        x, y, c = _xyc()
        chip = 2 * x + y
        sib = (x, y, 1 - c)
        others = _other_chips(x, y)
        peers = [(ox, oy, c) for ox, oy in others] + [sib]
        barrier = pltpu.get_barrier_semaphore()
        for peer in peers:
            pl.semaphore_signal(barrier, inc=1, device_id=peer, device_id_type=MESH)
        pl.semaphore_wait(barrier, len(peers))

        def copy(w, region, sem, to):
            return pltpu.make_async_remote_copy(src_ref=region, dst_ref=region, send_sem=send_sems.at[sem],
                                                recv_sem=recv_sems.at[sem], device_id=to, device_id_type=MESH)

        for w in range(n):
            mine = _full_region(refs[w], kinds[w], chip, c, shapes[w])
            for ox, oy in others:
                copy(w, mine, 2 * w, (ox, oy, c)).start()
        for w in range(n):
            three = _region_of_size(refs[w], kinds[w], shapes[w], 3)
            copy(w, three, 2 * w, sib).wait_recv()
            for ox, oy in others:
                copy(w, _full_region(refs[w], kinds[w], 2 * ox + oy, c, shapes[w]), 2 * w + 1, sib).start()
        for w in range(n):
            three = _region_of_size(refs[w], kinds[w], shapes[w], 3)
            copy(w, three, 2 * w + 1, sib).wait_recv()
            copy(w, three, 2 * w, sib).wait_send()
            copy(w, three, 2 * w + 1, sib).wait_send()

    pl.kernel(
        body, out_type=(), mesh=plsc.ScalarSubcoreMesh(axis_name="seq", num_cores=1), name=name,
        scratch_types=[pltpu.SemaphoreType.DMA((2 * n,)), pltpu.SemaphoreType.DMA((2 * n,))],
        compiler_params=pltpu.CompilerParams(collective_id=collective_id),
    )()
    return [r[...] for r in refs]


def _half_of(ref, kind, half):
    r, cn = ref.shape
    if kind == "col":
        return ref.at[pl.ds(pl.multiple_of(half * (r // 2), 16), r // 2), :]
    return ref.at[:, pl.ds(pl.multiple_of(half * (cn // 2), LANES), cn // 2)]


def _shard_of(ref, kind, chip):
    r, cn = ref.shape
    if kind == "col":
        return ref.at[:, pl.ds(pl.multiple_of(chip * (cn // N_CHIPS), LANES), cn // N_CHIPS)]
    return ref.at[pl.ds(pl.multiple_of(chip * (r // N_CHIPS), 16), r // N_CHIPS), :]


def _exchange_halves(grads, kinds, name):
    n = len(grads)

    def body(*refs):
        gs = refs[:n]
        ts = refs[n:2 * n]
        send_sems, recv_sems = refs[2 * n:]
        x, y, c = _xyc()
        cps = []
        for w in range(n):
            cp = pltpu.make_async_remote_copy(
                src_ref=_half_of(gs[w], kinds[w], 1 - c), dst_ref=ts[w], send_sem=send_sems.at[w],
                recv_sem=recv_sems.at[w], device_id=(x, y, 1 - c), device_id_type=MESH)
            cp.start()
            cps.append(cp)
        for cp in cps:
            cp.wait()

    out_shape = []
    for gr, kind in zip(grads, kinds):
        r, cn = gr.shape
        out_shape.append(jax.ShapeDtypeStruct((r // 2, cn) if kind == "col" else (r, cn // 2), gr.dtype))
    return pl.pallas_call(
        body, name=name, in_specs=[ANY] * n, out_specs=[ANY] * n, out_shape=out_shape,
        scratch_shapes=[pltpu.SemaphoreType.DMA((n,)), pltpu.SemaphoreType.DMA((n,))],
    )(*grads)


def _exchange_halves_seq(grads, kinds, name, collective_id):
    n = len(grads)

    def body(*refs):
        gs = refs[:n]
        ts = refs[n:2 * n]
        send_sems, recv_sems = refs[2 * n:]
        x, y, c = _xyc()
        sib = (x, y, 1 - c)
        barrier = pltpu.get_barrier_semaphore()
        pl.semaphore_signal(barrier, inc=1, device_id=sib, device_id_type=MESH)
        pl.semaphore_wait(barrier, 1)
        cps = []
        for w in range(n):
            cp = pltpu.make_async_remote_copy(
                src_ref=_half_of(gs[w], kinds[w], 1 - c), dst_ref=ts[w], send_sem=send_sems.at[w],
                recv_sem=recv_sems.at[w], device_id=sib, device_id_type=MESH)
            cp.start()
            cps.append(cp)
        for cp in cps:
            cp.wait()

    out_type = []
    for gr, kind in zip(grads, kinds):
        r, cn = gr.shape
        out_type.append(jax.ShapeDtypeStruct((r // 2, cn) if kind == "col" else (r, cn // 2), gr.dtype))
    return pl.kernel(
        body, out_type=out_type, mesh=plsc.ScalarSubcoreMesh(axis_name="seq", num_cores=1), name=name,
        scratch_types=[pltpu.SemaphoreType.DMA((n,)), pltpu.SemaphoreType.DMA((n,))],
        compiler_params=pltpu.CompilerParams(collective_id=collective_id),
    )(*grads)


def _scatter_partials(parts, kinds, name):
    n = len(parts)

    def body(*refs):
        ps = refs[:n]
        us = refs[n:2 * n]
        send_sems, recv_sems = refs[2 * n:]
        x, y, c = _xyc()
        others = _other_chips(x, y)
        cps = []
        for w in range(n):
            for j, (ox, oy) in enumerate(others):
                cp = pltpu.make_async_remote_copy(
                    src_ref=_shard_of(ps[w], kinds[w], 2 * ox + oy), dst_ref=us[w].at[j],
                    send_sem=send_sems.at[3 * w + j], recv_sem=recv_sems.at[3 * w + j],
                    device_id=(ox, oy, c), device_id_type=MESH)
                cp.start()
                cps.append(cp)
        for cp in cps:
            cp.wait()

    out_shape = []
    for p, kind in zip(parts, kinds):
        r, cn = p.shape
        hs = (r, cn // N_CHIPS) if kind == "col" else (r // N_CHIPS, cn)
        out_shape.append(jax.ShapeDtypeStruct((N_CHIPS - 1,) + hs, p.dtype))
    return pl.pallas_call(
        body, name=name, in_specs=[ANY] * n, out_specs=[ANY] * n, out_shape=out_shape,
        scratch_shapes=[pltpu.SemaphoreType.DMA((3 * n,)), pltpu.SemaphoreType.DMA((3 * n,))],
    )(*parts)


def _scatter_partials_seq(parts, kinds, name, collective_id):
    n = len(parts)

    def body(*refs):
        ps = refs[:n]
        us = refs[n:2 * n]
        send_sems, recv_sems = refs[2 * n:]
        x, y, c = _xyc()
        others = _other_chips(x, y)
        barrier = pltpu.get_barrier_semaphore()
        for ox, oy in others:
            pl.semaphore_signal(barrier, inc=1, device_id=(ox, oy, c), device_id_type=MESH)
        pl.semaphore_wait(barrier, len(others))
        for w in range(n):
            for j, (ox, oy) in enumerate(others):
                pltpu.make_async_remote_copy(
                    src_ref=_shard_of(ps[w], kinds[w], 2 * ox + oy), dst_ref=us[w].at[j],
                    send_sem=send_sems.at[w], recv_sem=recv_sems.at[w],
                    device_id=(ox, oy, c), device_id_type=MESH).start()
        for w in range(n):
            pltpu.make_async_remote_copy(
                src_ref=us[w], dst_ref=us[w], send_sem=send_sems.at[w], recv_sem=recv_sems.at[w],
                device_id=(x, y, c), device_id_type=MESH).wait()

    out_type = []
    for p, kind in zip(parts, kinds):
        r, cn = p.shape
        hs = (r, cn // N_CHIPS) if kind == "col" else (r // N_CHIPS, cn)
        out_type.append(jax.ShapeDtypeStruct((N_CHIPS - 1,) + hs, p.dtype))
    return pl.kernel(
        body, out_type=out_type, mesh=plsc.ScalarSubcoreMesh(axis_name="seq", num_cores=1), name=name,
        scratch_types=[pltpu.SemaphoreType.DMA((n,)), pltpu.SemaphoreType.DMA((n,))],
        compiler_params=pltpu.CompilerParams(collective_id=collective_id),
    )(*parts)


def _join_halves(halves, kinds, name):
    n = len(halves)

    def body(*refs):
        outs = refs[n:2 * n]
        send_sems, recv_sems = refs[2 * n:]
        x, y, c = _xyc()
        cps = []
        for w in range(n):
            mine = _half_of(outs[w], kinds[w], c)
            cp = pltpu.make_async_remote_copy(
                src_ref=mine, dst_ref=mine, send_sem=send_sems.at[w], recv_sem=recv_sems.at[w],
                device_id=(x, y, 1 - c), device_id_type=MESH)
            cp.start()
            cps.append(cp)
        for w in range(n):
            theirs = _half_of(outs[w], kinds[w], 1 - c)
            pltpu.make_async_remote_copy(
                src_ref=theirs, dst_ref=theirs, send_sem=send_sems.at[w], recv_sem=recv_sems.at[w],
                device_id=(x, y, 1 - c), device_id_type=MESH).wait_recv()
        _drain(cps)

    return pl.pallas_call(
        body, name=name, in_specs=[ANY] * n, out_specs=[ANY] * n,
        out_shape=[jax.ShapeDtypeStruct(h.shape, h.dtype) for h in halves],
        input_output_aliases={w: w for w in range(n)},
        scratch_shapes=[pltpu.SemaphoreType.DMA((n,)), pltpu.SemaphoreType.DMA((n,))],
    )(*halves)


def kernel(x, a_norm, a_w_in, a_conv_w, a_conv_b, a_ln_g, a_ln_b, a_w_out, kv_norm, w_kv, b_norm, b_w_in, b_w_out, rel_bias, final_norm, loss_target, m_a_norm, m_a_w_in, m_a_conv_w, m_a_conv_b, m_a_ln_g, m_a_ln_b, m_a_w_out, m_kv_norm, m_w_kv, m_b_norm, m_b_w_in, m_b_w_out, m_rel_bias, m_final_norm, v_a_norm, v_a_w_in, v_a_conv_w, v_a_conv_b, v_a_ln_g, v_a_ln_b, v_a_w_out, v_kv_norm, v_w_kv, v_b_norm, v_b_w_in, v_b_w_out, v_rel_bias, v_final_norm):
    S, D = x.shape[1], x.shape[2]
    E = a_w_out.shape[1] * N_CHIPS
    A = b_w_out.shape[1] * N_CHIPS
    H = A // HEAD_DIM
    DC = D // N_CHIPS
    xs = x.reshape(S, D)
    tgt = loss_target.reshape(S, D)
    cx, cy, cc = _xyc()
    chip = 2 * cx + cy
    c_idx = jnp.reshape(cc, (1,)).astype(jnp.int32)

    big_names = ["a_w_in", "a_w_out", "w_kv", "b_w_in", "b_w_out"]
    kinds = ["col", "row", "col", "col", "row"]
    big_w = [a_w_in[0], a_w_out[0], w_kv, b_w_in[0], b_w_out[0]]
    big_m = [m_a_w_in[0], m_a_w_out[0], m_w_kv, m_b_w_in[0], m_b_w_out[0]]
    big_v = [v_a_w_in[0], v_a_w_out[0], v_w_kv, v_b_w_in[0], v_b_w_out[0]]
    chip_idx = jnp.reshape(chip, (1,)).astype(jnp.int32)
    placed = [_cast_into_full(big_w[w], kinds[w], chip_idx, "cast_" + big_names[w]) for w in range(5)]
    shard_shapes = [w.shape for w in big_w]
    (wa_in,) = _allgather_weights_seq(placed[0:1], kinds[0:1], shard_shapes[0:1], "ag_seq_a_in", 0)
    (wa_out,) = _allgather_weights_seq(placed[1:2], kinds[1:2], shard_shapes[1:2], "ag_seq_a_out", 1)
    (wkv,) = _allgather_weights_seq(placed[2:3], kinds[2:3], shard_shapes[2:3], "ag_seq_kv", 5)
    wb_in, wb_out = _allgather_weights_seq(placed[3:5], kinds[3:5], shard_shapes[3:5], "ag_seq_b", 2)

    def row_at(vec, q):
        return jnp.pad(vec, ((q, 7 - q), (0, 0)))

    def pack_sharded(an, cw, cb, lg, lb):
        return jnp.concatenate([row_at(an, 0), jnp.pad(cw[0], ((0, 1), (0, 0))),
                                row_at(lg, 0) + row_at(lb, 1) + row_at(cb, 2)], axis=0)

    small_w = pack_sharded(a_norm, a_conv_w, a_conv_b, a_ln_g, a_ln_b)
    gathered = _allgather_small(small_w, "ag_small_params")
    small_full = jnp.concatenate([gathered[2 * k] for k in range(N_CHIPS)], axis=1)
    g_a = small_full[0:1]
    conv_w32 = small_full[8:8 + HALO]
    ln_g = small_full[40:41]
    ln_b = small_full[41:42]
    conv_b = small_full[42:43]
    g_kv = kv_norm.reshape(1, D)
    g_b = b_norm.reshape(1, D)
    g_f = final_norm.reshape(1, D)

    rb_t = jnp.pad(rel_bias.T, ((0, 0), (0, LANES - N_BUCKETS)))
    onehots = [_onehot(dil) for _, dil in GROUPS]
    biases = [_bias_table(rb_t, onehots[g], "bias_table_%d" % g).reshape(H, BLOCK, 2 * BLOCK)
              for g in range(len(GROUPS))]

    dils = tuple(dil for _, dil in GROUPS)
    assert dils[0] == 1
    n_g = len(GROUPS)
    ((h0,),) = _rms_fwd(xs, [g_a], (1,), "rms_a")
    proj3 = _matmul(h0, wa_in, "nn", BF16, "mm_a_in", out_slab=E)
    conv = _conv_fwd(proj3, conv_w32, conv_b, "conv_fwd")
    y_a = _ln_gate_fwd(conv, proj3, ln_g, ln_b, "ln_gate_fwd")
    x1 = _matmul(y_a, wa_out, "nn", F32, "mm_a_out", res=xs)
    hks, hbs = _rms_fwd(x1, [g_kv, g_b], dils, "rms_kv_b")
    kvs = [_matmul(hks[g], wkv, "nn", BF16, "mm_kv_%d" % g, out_slab=A, b_off=2 * g * A, n_cols=2 * A)
           for g in range(n_g)]
    qs = [_matmul(hbs[g], wb_in, "nn", BF16, "mm_q_%d" % g, b_off=g * A, n_cols=A, after=kvs)
          for g in range(n_g)]
    zb = _matmul(hbs[0], wb_in, "nn", BF16, "mm_zb", b_off=n_g * A, n_cols=A, after=kvs)
    os_, lses = [], []
    for g, dil in enumerate(dils):
        o_g, lse_g = _attn_fwd(qs[g], kvs[g], biases[g], dil, "attn_fwd_%d" % g)
        os_.append(o_g)
        lses.append(lse_g)
    y_b, o_m, lse_d = _attn_merge(os_, lses, zb, dils, "attn_merge")
    x2 = _matmul(y_b, wb_out, "nn", F32, "mm_b_out", res=x1)
    loss_part, dx2, dx2b, gg_f = _final_head(x2, g_f, tgt, "final_head")
    loss = lax.psum(loss_part[0, 0], ("x", "y", "c"))

    dw_tiles = dict(tm=1024, tn=1024, tk=4096)
    dy_b = _matmul(dx2b, wb_out, "nt", BF16, "mm_b_out_dx", after=loss.reshape(1, 1))
    dwb_out = _matmul(y_b, dx2b, "tn", BF16, "mm_b_out_dw", **dw_tiles)
    dos, dhs, dzb = _gate_bwd(dy_b, o_m, zb, dils, "gate_bwd")
    dbs, cots = [], []
    dwb_in = dwkv = None
    for g, dil in enumerate(dils):
        dq, dkv, db = _attn_bwd(qs[g], kvs[g], dos[g], lse_d[g], dhs[g], biases[g], dil, "attn_bwd_%d" % g)
        dbs.append(db.reshape(H, BLOCK * 2 * BLOCK))
        dwb_in = _matmul(hbs[g], dq, "tn", BF16, "mm_q_dw_%d" % g, out_off=g * A, out_cols=(n_g + 1) * A,
                         out_alias=dwb_in, **dw_tiles)
        dwkv = _matmul(hks[g], dkv, "tn", BF16, "mm_kv_dw_%d" % g, b_slab=True, out_off=2 * g * A,
                       out_cols=2 * n_g * A, out_alias=dwkv, **dw_tiles)
        cots.append((_matmul(dkv, wkv, "nt", BF16, "mm_kv_dx_%d" % g, a_slab=True, b_off=2 * g * A), 0, dil))
        cots.append((_matmul(dq, wb_in, "nt", BF16, "mm_q_dx_%d" % g, b_off=g * A), 1, dil))
    dwb_in = _matmul(hbs[0], dzb, "tn", BF16, "mm_zb_dw", out_off=n_g * A, out_cols=(n_g + 1) * A,
                     out_alias=dwb_in, **dw_tiles)
    cots.append((_matmul(dzb, wb_in, "nt", BF16, "mm_zb_dx", b_off=n_g * A), 1, 1))
    chip_c = jnp.stack([chip, cc]).astype(jnp.int32)

    def scatter_group(idx, grads, tag, collective_id, exchange_id=None, behind=None):
        ks = [kinds[w] for w in idx]
        if exchange_id is None:
            theirs = _exchange_halves(grads, ks, "rs_exchange_" + tag)
        else:
            theirs = _exchange_halves_seq(grads, ks, "rs_exchange_seq_" + tag, exchange_id)
        parts = [_add_half(grads[q], theirs[q], c_idx, ks[q], "rs_add_half_%d" % w, after=behind)
                 for q, w in enumerate(idx)]
        return parts, _scatter_partials_seq(parts, ks, "rs_seq_" + tag, collective_id)

    def reduce_group(idx, parts, slots, tag):
        ks = [kinds[w] for w in idx]
        halves = [_sum_into_shard(parts[q], slots[q], chip_c, ks[q], "rs_sum_chips_%d" % w)
                  for q, w in enumerate(idx)]
        return _join_halves(halves, ks, "rs_join_" + tag)

    parts_b, slots_b = scatter_group([2, 3, 4], [dwkv, dwb_in, dwb_out], "b", 3, exchange_id=6,
                                     behind=[ct[0] for ct in cots])
    g_rel_t = _bias_grad(dbs, onehots, "bias_grad")
    dx1, dx1b, gg_kvb = _rms_bwd(x1, cots, [g_kv, g_b], dx2, "rms_kv_b_bwd", after=parts_b)
    dy_a = _matmul(dx1b, wa_out, "nt", BF16, "mm_a_out_dx")
    dwa_out = _matmul(y_a, dx1b, "tn", BF16, "mm_a_out_dw", **dw_tiles)
    dconv, dproj3, gg_ln = _ln_gate_bwd(conv, proj3, dy_a, ln_g, ln_b, "ln_gate_bwd")
    dproj3, g_conv_w = _conv_bwd(proj3, dconv, conv_w32, dproj3, "conv_bwd")
    dwa_in = _matmul(h0, dproj3, "tn", BF16, "mm_a_in_dw", b_slab=True, **dw_tiles)
    parts_a, slots_a = scatter_group([0, 1], [dwa_in, dwa_out], "a", 4)
    dh0 = _matmul(dproj3, wa_in, "nt", BF16, "mm_a_in_dx", a_slab=True, after=parts_a, tn=512)
    grad_x, _, gg_a = _rms_bwd(xs, [(dh0, 0, 1)], [g_a], dx1, "rms_a_bwd")

    big_g = [None] * 5
    big_g[2:5] = reduce_group([2, 3, 4], parts_b, slots_b, "b")
    big_g[0:2] = reduce_group([0, 1], parts_a, slots_a, "a")

    def rel_rows(rb):
        return jnp.pad(rb.reshape(1, N_BUCKETS * H), ((0, 7), (0, D - N_BUCKETS * H)))

    small_g = jnp.concatenate([gg_a, g_conv_w, gg_ln, gg_kvb, gg_f, rel_rows(g_rel_t[:, :N_BUCKETS].T)], axis=0)
    small_sum = _sum_leading(_allgather_small(small_g, "ag_small_grads", after=[slots_a[0], slots_b[0]]), F32,
                             "sum_small_grads", tr=72)
    g_sharded = lax.dynamic_slice(small_sum, (0, chip * DC), (48, DC))
    g_repl = small_sum[48:72]

    outs_g, outs_d, outs_m, outs_v = {}, {}, {}, {}
    for w, nm in enumerate(big_names):
        d_, m_, v_, g_ = _adamw(big_w[w], big_g[w], big_m[w], big_v[w], "adamw_" + nm, emit_grad=True)
        outs_g[nm], outs_d[nm], outs_m[nm], outs_v[nm] = g_, d_, m_, v_
    sm_m = pack_sharded(m_a_norm, m_a_conv_w, m_a_conv_b, m_a_ln_g, m_a_ln_b)
    sm_v = pack_sharded(v_a_norm, v_a_conv_w, v_a_conv_b, v_a_ln_g, v_a_ln_b)
    sd, smm, svv = _adamw(small_w, g_sharded, sm_m, sm_v, "adamw_small_sharded")

    def unpack_sharded(p):
        return {"a_norm": p[0:1], "a_conv_w": p[8:8 + CONV_TAPS].reshape(1, CONV_TAPS, DC), "a_ln_g": p[40:41],
                "a_ln_b": p[41:42], "a_conv_b": p[42:43]}

    for src, dst in ((g_sharded, outs_g), (sd, outs_d), (smm, outs_m), (svv, outs_v)):
        dst.update(unpack_sharded(src))

    def pack_repl(kn, bn, fn, rb):
        return jnp.concatenate([row_at(kn.reshape(1, D), 0) + row_at(bn.reshape(1, D), 1),
                                row_at(fn.reshape(1, D), 0), rel_rows(rb)], axis=0)

    rp_w = pack_repl(kv_norm, b_norm, final_norm, rel_bias)
    rp_m = pack_repl(m_kv_norm, m_b_norm, m_final_norm, m_rel_bias)
    rp_v = pack_repl(v_kv_norm, v_b_norm, v_final_norm, v_rel_bias)
    rd, rmm, rvv = _adamw(rp_w, g_repl, rp_m, rp_v, "adamw_small_replicated")

    def unpack_repl(p):
        return {"kv_norm": p[0], "b_norm": p[1:2], "final_norm": p[8],
                "rel_bias": p[16, :N_BUCKETS * H].reshape(N_BUCKETS, H)}

    for src, dst in ((g_repl, outs_g), (rd, outs_d), (rmm, outs_m), (rvv, outs_v)):
        dst.update(unpack_repl(src))

    order = ["a_norm", "a_w_in", "a_conv_w", "a_conv_b", "a_ln_g", "a_ln_b", "a_w_out", "kv_norm", "w_kv",
             "b_norm", "b_w_in", "b_w_out", "rel_bias", "final_norm"]
    lead = {"a_w_in", "a_w_out", "b_w_in", "b_w_out"}

    def shaped(nm, val):
        return val[None] if nm in lead else val

    result = [loss, grad_x.reshape(1, S, D)]
    for table in (outs_g, outs_d, outs_m, outs_v):
        result.extend(shaped(nm, table[nm]) for nm in order)
    return tuple(result)
```
